```python
import math
import jax, jax.numpy as jnp
from jax import lax
import numpy as np


D_MODEL = 1024
BATCH = 8
SEQ = 4096
DEPTH = 1
DEC_BATCH = 4
DEC_SEQ = 8192
PAST_LEN = 128

MLA_HEADS = 8
Q_LORA = 256
KV_LORA = 128
NOPE_DIM = 64
ROPE_DIM = 32
V_DIM = 64
ROPE_THETA = 10000.0
Q_BLOCK = 128
DN_HEADS = 8
DN_DK = 64
DN_DV = 64
CONV_K = 5
CHUNK = 64
N_EXPERTS = 32
TOP_K = 4
D_FF = 1024
SWIGLU_LIMIT = 7.0
SWIGLU_ALPHA = 1.702
MOE_BLOCK = 128
PLE_DIM = 256
DEEPNORM_ALPHA = (2.0 * DEPTH) ** 0.25
DEEPNORM_BETA = (8.0 * DEPTH) ** -0.25
SPLIT_SIZES = (Q_LORA, KV_LORA, ROPE_DIM, 2 * DN_HEADS * DN_DK + DN_HEADS * DN_DV, DN_HEADS * DN_DV, 2 * DN_HEADS, 2 * DN_HEADS, 2 * D_MODEL)
IN_DIM = Q_LORA + KV_LORA + ROPE_DIM + 2 * DN_HEADS * DN_DK + 2 * DN_HEADS * DN_DV + 4 * DN_HEADS + 2 * D_MODEL

kernel_name = 'hybrid_mla_gdn_moe_encoder'


def rms_norm(x, g, eps=1e-6):
    xf = x.astype(jnp.float32)
    y = xf * lax.rsqrt(jnp.mean(xf * xf, axis=-1, keepdims=True) + eps)
    return (y * g.astype(jnp.float32)).astype(x.dtype)


def layer_norm(x, g, b, eps=1e-5):
    xf = x.astype(jnp.float32)
    mu = jnp.mean(xf, axis=-1, keepdims=True)
    var = jnp.mean(jnp.square(xf - mu), axis=-1, keepdims=True)
    y = (xf - mu) * lax.rsqrt(var + eps) * g.astype(jnp.float32) + b.astype(jnp.float32)
    return y.astype(x.dtype)


def l2_norm(x, eps=1e-6):
    xf = x.astype(jnp.float32)
    return (xf * lax.rsqrt(jnp.sum(xf * xf, axis=-1, keepdims=True) + eps)).astype(x.dtype)


def rope_tables(seq, dtype):
    inv = ROPE_THETA ** (-jnp.arange(0, ROPE_DIM, 2, dtype=jnp.float32) / ROPE_DIM)
    ang = jnp.arange(seq, dtype=jnp.float32)[:, None] * inv[None, :]
    return jnp.cos(ang).astype(dtype), jnp.sin(ang).astype(dtype)


def apply_rope(x, cos, sin):
    x1, x2 = jnp.split(x, 2, axis=-1)
    return jnp.concatenate([x1 * cos - x2 * sin, x1 * sin + x2 * cos], axis=-1)


def mla_attention(c_q, c_kv, k_rope, q_a_norm, w_uq, kv_a_norm, w_ukv):
    B, S, _ = c_q.shape
    q = (rms_norm(c_q, q_a_norm) @ w_uq).reshape(B, S, MLA_HEADS, NOPE_DIM + ROPE_DIM)
    kv = (rms_norm(c_kv, kv_a_norm) @ w_ukv).reshape(B, S, MLA_HEADS, NOPE_DIM + V_DIM)
    q_nope, q_rope = q[..., :NOPE_DIM], q[..., NOPE_DIM:]
    k_nope, v = kv[..., :NOPE_DIM], kv[..., NOPE_DIM:]
    cos, sin = rope_tables(S, c_q.dtype)
    q_rope = apply_rope(q_rope, cos[None, :, None, :], sin[None, :, None, :])
    k_rope = apply_rope(k_rope, cos[None], sin[None])
    nq = S // Q_BLOCK
    scale = (NOPE_DIM + ROPE_DIM) ** -0.5

    def blocks(t):
        return t.reshape((B, nq, Q_BLOCK) + t.shape[2:]).swapaxes(0, 1)

    def attend(args):
        qn, qr = args
        s = jnp.einsum('bqhd,bkhd->bhqk', qn, k_nope) + jnp.einsum('bqhr,bkr->bhqk', qr, k_rope)
        p = jax.nn.softmax(s.astype(jnp.float32) * scale, axis=-1).astype(v.dtype)
        return jnp.einsum('bhqk,bkhd->bqhd', p, v)

    o = lax.map(attend, (blocks(q_nope), blocks(q_rope)))
    return o.swapaxes(0, 1).reshape(B, S, MLA_HEADS * V_DIM)


def conv_centred(x, w):
    C = x.shape[-1]
    return lax.conv_general_dilated(x, w.astype(x.dtype)[:, None, :], window_strides=(1,),
                                    padding=[(CONV_K // 2, CONV_K // 2)],
                                    dimension_numbers=('NWC', 'WIO', 'NWC'), feature_group_count=C)


def chunk_gated_delta(q, k, v, g, beta):
    out_dtype = v.dtype
    B, S, H, DK = q.shape
    DV = v.shape[-1]
    N = S // CHUNK
    f32 = jnp.float32

    def chunks(t):
        return t.astype(f32).reshape(B, N, CHUNK, H, -1).transpose(1, 0, 3, 2, 4)

    q_c, k_c, v_c = chunks(q), chunks(k), chunks(v)
    g_c = chunks(g[..., None])[..., 0]
    b_c = chunks(beta[..., None])[..., 0]
    g_cum = jnp.cumsum(g_c, axis=-1)
    idx = jnp.arange(CHUNK)
    incl = idx[:, None] >= idx[None, :]
    strict = idx[:, None] > idx[None, :]
    diff = g_cum[..., :, None] - g_cum[..., None, :]
    decay = jnp.where(incl, jnp.exp(jnp.where(incl, diff, 0.0)), 0.0)
    k_beta = k_c * b_c[..., None]
    v_beta = v_c * b_c[..., None]
    a_kk = jnp.where(strict, jnp.einsum('nbhid,nbhjd->nbhij', k_beta, k_c) * decay, 0.0)
    t_mat = a_kk + jnp.eye(CHUNK, dtype=f32)
    u = lax.linalg.triangular_solve(t_mat, v_beta, left_side=True, lower=True, unit_diagonal=True)
    w = lax.linalg.triangular_solve(t_mat, k_beta * jnp.exp(g_cum)[..., None], left_side=True, lower=True, unit_diagonal=True)
    a_qk = jnp.einsum('nbhid,nbhjd->nbhij', q_c, k_c) * decay
    q_dec = q_c * jnp.exp(g_cum)[..., None]
    g_last = g_cum[..., -1]
    k_dec = k_c * jnp.exp(g_last[..., None] - g_cum)[..., None]

    def step(state, xs):
        q_i, u_i, w_i, a_i, gl_i, k_i = xs
        v_new = u_i - jnp.einsum('bhck,bhkv->bhcv', w_i, state)
        o_i = jnp.einsum('bhck,bhkv->bhcv', q_i, state) + jnp.einsum('bhij,bhjv->bhiv', a_i, v_new)
        state = state * jnp.exp(gl_i)[..., None, None] + jnp.einsum('bhck,bhcv->bhkv', k_i, v_new)
        return state, o_i

    s0 = jnp.zeros((B, H, DK, DV), f32)
    _, o = lax.scan(step, s0, (q_dec, u, w, a_qk, g_last, k_dec))
    return o.transpose(1, 0, 3, 2, 4).reshape(B, S, H, DV).astype(out_dtype)


def gated_deltanet(dn_qkv, dn_z, dn_a, dn_b, dn_conv, dn_a_log, dn_dt_bias, dn_norm):
    B, S, _ = dn_qkv.shape
    qkv = jax.nn.silu(conv_centred(dn_qkv, dn_conv))
    q = qkv[..., :DN_HEADS * DN_DK].reshape(B, S, DN_HEADS, DN_DK)
    k = qkv[..., DN_HEADS * DN_DK:2 * DN_HEADS * DN_DK].reshape(B, S, DN_HEADS, DN_DK)
    v = qkv[..., 2 * DN_HEADS * DN_DK:].reshape(B, S, DN_HEADS, DN_DV)
    q = l2_norm(q) * (DN_DK ** -0.5)
    k = l2_norm(k)
    a = dn_a.reshape(B, S, 2, DN_HEADS).astype(jnp.float32)
    b = dn_b.reshape(B, S, 2, DN_HEADS).astype(jnp.float32)
    g = -jnp.exp(dn_a_log.astype(jnp.float32)) * jax.nn.softplus(a + dn_dt_bias.astype(jnp.float32))
    beta = jax.nn.sigmoid(b)
    o_fwd = chunk_gated_delta(q, k, v, g[:, :, 0], beta[:, :, 0])
    o_bwd = jnp.flip(chunk_gated_delta(jnp.flip(q, 1), jnp.flip(k, 1), jnp.flip(v, 1),
                                       jnp.flip(g[:, :, 1], 1), jnp.flip(beta[:, :, 1], 1)), 1)
    z = dn_z.reshape(B, S, DN_HEADS, DN_DV)
    o = rms_norm(o_fwd + o_bwd, dn_norm) * jax.nn.silu(z)
    return o.reshape(B, S, DN_HEADS * DN_DV)


def moe(h, router_w, router_b, w_gate_up, b_gate_up, w_down, b_down):
    B, S, D = h.shape
    T = B * S
    x = h.reshape(T, D)
    logits = (x @ router_w + router_b).astype(jnp.float32)
    top_v, top_i = lax.top_k(logits, TOP_K)
    gates = jax.nn.softmax(top_v, axis=-1).astype(x.dtype)
    A = T * TOP_K
    flat_e = top_i.reshape(A)
    order = jnp.argsort(flat_e)
    sorted_e = flat_e[order]
    counts = jnp.zeros((N_EXPERTS,), jnp.int32).at[flat_e].add(1)
    padded = ((counts + MOE_BLOCK - 1) // MOE_BLOCK) * MOE_BLOCK
    starts = jnp.cumsum(counts) - counts
    p_end = jnp.cumsum(padded)
    p_starts = p_end - padded
    dest = p_starts[sorted_e] + (jnp.arange(A, dtype=jnp.int32) - starts[sorted_e])
    n_blk = (A + MOE_BLOCK - 1) // MOE_BLOCK + N_EXPERTS
    P = n_blk * MOE_BLOCK
    tok_sorted = (order // TOP_K).astype(jnp.int32)
    slot_tok = jnp.full((P,), T, jnp.int32).at[dest].set(tok_sorted)
    x_pad = jnp.concatenate([x, jnp.zeros((1, D), x.dtype)], axis=0)
    xb = x_pad[slot_tok].reshape(n_blk, MOE_BLOCK, D)
    blk_e = jnp.minimum(jnp.searchsorted(p_end, jnp.arange(n_blk, dtype=jnp.int32) * MOE_BLOCK, side='right'), N_EXPERTS - 1)

    def expert_block(args):
        xb_i, e = args
        gu = xb_i @ w_gate_up[e] + b_gate_up[e]
        gate = jnp.minimum(gu[:, :D_FF], SWIGLU_LIMIT)
        up = jnp.clip(gu[:, D_FF:], -SWIGLU_LIMIT, SWIGLU_LIMIT)
        act = gate * jax.nn.sigmoid(SWIGLU_ALPHA * gate) * (up + 1.0)
        return act @ w_down[e] + b_down[e]

    yb = lax.map(expert_block, (xb, blk_e)).reshape(P, D)
    y_sorted = yb[dest] * gates.reshape(A)[order][:, None]
    out = jax.ops.segment_sum(y_sorted, tok_sorted, num_segments=T)
    return out.reshape(B, S, D)


def encoder_layer(x, p, w_in, q_a_norm, w_uq, kv_a_norm, w_ukv, w_o_attn, dn_conv, dn_a_log, dn_dt_bias,
                  dn_norm, w_o_dn, w_out, ln1_g, ln1_b, router_w, router_b, w_gate_up, b_gate_up, w_down,
                  b_down, ple_w_proj, ple_w_gate, ln2_g, ln2_b):
    proj = x @ w_in
    parts = []
    off = 0
    for size in SPLIT_SIZES:
        parts.append(proj[..., off:off + size])
        off += size
    c_q, c_kv, k_rope, dn_qkv, dn_z, dn_a, dn_b, gate_logits = parts
    o_attn = mla_attention(c_q, c_kv, k_rope, q_a_norm, w_uq, kv_a_norm, w_ukv) @ w_o_attn
    o_dn = gated_deltanet(dn_qkv, dn_z, dn_a, dn_b, dn_conv, dn_a_log, dn_dt_bias, dn_norm) @ w_o_dn
    gates = jax.nn.sigmoid(gate_logits)
    mix = (gates[..., :D_MODEL] * o_attn + gates[..., D_MODEL:] * o_dn) @ w_out
    h = layer_norm(DEEPNORM_ALPHA * x + mix, ln1_g, ln1_b)
    ple = jax.nn.sigmoid(h @ ple_w_gate) * (p @ ple_w_proj)
    ffn = moe(h, router_w, router_b, w_gate_up, b_gate_up, w_down, b_down)
    return layer_norm(DEEPNORM_ALPHA * h + ffn + ple, ln2_g, ln2_b)


def setup_inputs(seed: int = 0) -> dict:
    key = jax.random.key(seed)
    ks = jax.random.split(key, 32)
    L = DEPTH
    f32 = jnp.float32

    def nrm(k, shape, scale):
        return jax.random.normal(k, shape, f32) * scale

    dt = jnp.exp(jax.random.uniform(ks[12], (L, 2, DN_HEADS), f32, minval=math.log(1e-3), maxval=math.log(1e-1)))
    return {
        'x_prompt': nrm(ks[0], (BATCH, SEQ, D_MODEL), 1.0),
        'x_sample': nrm(ks[1], (DEC_BATCH, DEC_SEQ, D_MODEL), 1.0),
        'p_prompt': nrm(ks[2], (DEPTH, BATCH, SEQ, PLE_DIM), 1.0),
        'p_sample': nrm(ks[3], (DEPTH, DEC_BATCH, DEC_SEQ, PLE_DIM), 1.0),
        'w_in': nrm(ks[4], (L, D_MODEL, IN_DIM), D_MODEL ** -0.5),
        'q_a_norm': 1.0 + nrm(ks[5], (L, Q_LORA), 0.01),
        'w_uq': nrm(ks[6], (L, Q_LORA, MLA_HEADS * (NOPE_DIM + ROPE_DIM)), Q_LORA ** -0.5),
        'kv_a_norm': 1.0 + nrm(ks[7], (L, KV_LORA), 0.01),
        'w_ukv': nrm(ks[8], (L, KV_LORA, MLA_HEADS * (NOPE_DIM + V_DIM)), KV_LORA ** -0.5),
        'w_o_attn': nrm(ks[9], (L, MLA_HEADS * V_DIM, D_MODEL), DEEPNORM_BETA * (MLA_HEADS * V_DIM) ** -0.5),
        'dn_conv': nrm(ks[10], (L, CONV_K, 2 * DN_HEADS * DN_DK + DN_HEADS * DN_DV), CONV_K ** -0.5),
        'dn_a_log': jnp.log(jax.random.uniform(ks[11], (L, 2, DN_HEADS), f32, minval=1.0, maxval=16.0)),
        'dn_dt_bias': dt + jnp.log(-jnp.expm1(-dt)),
        'dn_norm': 1.0 + nrm(ks[13], (L, DN_DV), 0.01),
        'w_o_dn': nrm(ks[14], (L, DN_HEADS * DN_DV, D_MODEL), DEEPNORM_BETA * (DN_HEADS * DN_DV) ** -0.5),
        'w_out': nrm(ks[15], (L, D_MODEL, D_MODEL), DEEPNORM_BETA * D_MODEL ** -0.5),
        'ln1_g': 1.0 + nrm(ks[16], (L, D_MODEL), 0.01),
        'ln1_b': nrm(ks[17], (L, D_MODEL), 0.01),
        'router_w': nrm(ks[18], (L, D_MODEL, N_EXPERTS), D_MODEL ** -0.5),
        'router_b': nrm(ks[19], (L, N_EXPERTS), 0.01),
        'w_gate_up': nrm(ks[20], (L, N_EXPERTS, D_MODEL, 2 * D_FF), D_MODEL ** -0.5),
        'b_gate_up': nrm(ks[21], (L, N_EXPERTS, 2 * D_FF), 0.01),
        'w_down': nrm(ks[22], (L, N_EXPERTS, D_FF, D_MODEL), DEEPNORM_BETA * D_FF ** -0.5),
        'b_down': nrm(ks[23], (L, N_EXPERTS, D_MODEL), 0.01),
        'ple_w_proj': nrm(ks[24], (L, PLE_DIM, D_MODEL), PLE_DIM ** -0.5),
        'ple_w_gate': nrm(ks[25], (L, D_MODEL, D_MODEL), D_MODEL ** -0.5),
        'ln2_g': 1.0 + nrm(ks[26], (L, D_MODEL), 0.01),
        'ln2_b': nrm(ks[27], (L, D_MODEL), 0.01),
    }


def reference(x_prompt, x_sample, p_prompt, p_sample, w_in, q_a_norm, w_uq, kv_a_norm, w_ukv, w_o_attn,
              dn_conv, dn_a_log, dn_dt_bias, dn_norm, w_o_dn, w_out, ln1_g, ln1_b, router_w, router_b,
              w_gate_up, b_gate_up, w_down, b_down, ple_w_proj, ple_w_gate, ln2_g, ln2_b):
    y_prompt = x_prompt
    y_sample = x_sample
    for l in range(DEPTH):
        lw = (w_in[l], q_a_norm[l], w_uq[l], kv_a_norm[l], w_ukv[l], w_o_attn[l], dn_conv[l], dn_a_log[l],
              dn_dt_bias[l], dn_norm[l], w_o_dn[l], w_out[l], ln1_g[l], ln1_b[l], router_w[l], router_b[l],
              w_gate_up[l], b_gate_up[l], w_down[l], b_down[l], ple_w_proj[l], ple_w_gate[l], ln2_g[l], ln2_b[l])
        y_prompt = encoder_layer(y_prompt, p_prompt[l], *lw)
        y_sample = encoder_layer(y_sample, p_sample[l], *lw)
    return (y_prompt, y_sample)
```

```python
import functools
import math

import numpy as np
import jax
import jax.numpy as jnp
from jax import lax
from jax.experimental import pallas as pl
from jax.experimental.pallas import tpu as pltpu

D_MODEL = 1024
MLA_HEADS = 8
Q_LORA = 256
KV_LORA = 128
NOPE_DIM = 64
ROPE_DIM = 32
V_DIM = 64
ROPE_THETA = 10000.0
DN_HEADS = 8
DN_DK = 64
DN_DV = 64
CONV_K = 5
N_EXPERTS = 32
TOP_K = 4
D_FF = 1024
SWIGLU_LIMIT = 7.0
SWIGLU_ALPHA = 1.702
PLE_DIM = 256
DEPTH = 1
DEEPNORM_ALPHA = (2.0 * DEPTH) ** 0.25

LANES = 128
HEAD_PAD = 128
DN_BLOCK = 256
VMEM_LIMIT = 56 * 1024 * 1024

_C_CQ = 0
_C_CKV = _C_CQ + Q_LORA
_C_KR = _C_CKV + KV_LORA
_C_KRS = _C_KR + LANES
_C_AB0 = _C_KRS + LANES
_C_AB1 = _C_AB0 + LANES
_C_DNQ = _C_AB1 + LANES
_C_DNVK = _C_DNQ + DN_HEADS * DN_DK
_C_Z = _C_DNVK + DN_HEADS * (DN_DK + DN_DV)
_C_GATE = _C_Z + DN_HEADS * DN_DV
_C_END = _C_GATE + 2 * D_MODEL

BF16 = jnp.bfloat16
F32 = jnp.float32


def _dot(a, b):
    return jnp.dot(a, b, preferred_element_type=F32)


def _dot_nt(a, b):
    return lax.dot_general(a, b, (((1,), (1,)), ((), ())), preferred_element_type=F32)


def _split_bf16(x):
    hi = x.astype(BF16)
    lo = (x - hi.astype(F32)).astype(BF16)
    return hi, lo


def _const_spec(shape):
    n = len(shape)
    return pl.BlockSpec(shape, lambda *_: (0,) * n)


def _params(sem):
    return pltpu.CompilerParams(dimension_semantics=sem, vmem_limit_bytes=VMEM_LIMIT)


def _in_proj_kernel(x_ref, w1_ref, qan_ref, kvan_ref, wq_ref, wqs_ref, wk_ref, wv_ref,
                    cosq_ref, sinq_ref, cosk_ref, sink_ref, abp_ref,
                    q_ref, k_ref, v_ref, dn_ref, z_ref, gates_ref, gb_ref):
    xb = x_ref[...].astype(BF16)

    def proj(lo, hi):
        return _dot(xb, w1_ref[:, lo:hi])

    def rms(c, g):
        return (c * lax.rsqrt(jnp.mean(c * c, axis=-1, keepdims=True) + 1e-6) * g).astype(BF16)

    cqn = rms(proj(_C_CQ, _C_CKV), qan_ref[...])
    qa = _dot(cqn, wq_ref[...])
    qb = _dot(cqn, wqs_ref[...])
    ckvn = rms(proj(_C_CKV, _C_KR), kvan_ref[...])
    kw = _dot(ckvn, wk_ref[...])
    vw = _dot(ckvn, wv_ref[...])
    kr = proj(_C_KR, _C_KRS) * cosk_ref[...] + proj(_C_KRS, _C_AB0) * sink_ref[...]
    cosq = cosq_ref[...]
    sinq = sinq_ref[...]
    lane = lax.broadcasted_iota(jnp.int32, (1, HEAD_PAD), 1)
    ones_col = (lane == V_DIM).astype(F32)
    for h in range(MLA_HEADS):
        sl = slice(h * HEAD_PAD, (h + 1) * HEAD_PAD)
        q_ref[:, sl] = (qa[:, sl] * cosq + qb[:, sl] * sinq).astype(BF16)
        k_ref[:, sl] = (kw[:, sl] + kr).astype(BF16)
        v_ref[:, sl] = (vw[:, sl] + ones_col).astype(BF16)

    for d, c0 in enumerate((_C_AB0, _C_AB1)):
        ab = proj(c0, c0 + LANES)
        neg_a = abp_ref[2 * d:2 * d + 1, :]
        dtb = abp_ref[2 * d + 1:2 * d + 2, :]
        t = ab + dtb
        sp = jnp.maximum(t, 0.0) + jnp.log(1.0 + jnp.exp(-jnp.abs(t)))
        g = neg_a * sp
        beta = 1.0 / (1.0 + jnp.exp(-ab))
        gb_ref[d] = jnp.where(lane < DN_HEADS, g, beta)[:, :2 * DN_HEADS]

    dn_ref[...] = proj(_C_DNQ, _C_Z).astype(BF16)
    z_ref[...] = proj(_C_Z, _C_GATE).astype(BF16)
    gl = proj(_C_GATE, _C_END)
    gates_ref[...] = (1.0 / (1.0 + jnp.exp(-gl))).astype(BF16)


def _in_proj(x2, S, W, tm=256):
    T = x2.shape[0]
    nseq = S // tm
    row = lambda i: (i, 0)
    pos = lambda i: (i % nseq, 0)
    dn_w = _C_Z - _C_DNQ
    return pl.pallas_call(
        _in_proj_kernel,
        grid=(T // tm,),
        in_specs=[
            pl.BlockSpec((tm, D_MODEL), row),
            _const_spec((D_MODEL, _C_END)),
            _const_spec((1, Q_LORA)), _const_spec((1, KV_LORA)),
            _const_spec((Q_LORA, MLA_HEADS * HEAD_PAD)), _const_spec((Q_LORA, MLA_HEADS * HEAD_PAD)),
            _const_spec((KV_LORA, MLA_HEADS * HEAD_PAD)), _const_spec((KV_LORA, MLA_HEADS * HEAD_PAD)),
            pl.BlockSpec((tm, HEAD_PAD), pos), pl.BlockSpec((tm, HEAD_PAD), pos),
            pl.BlockSpec((tm, HEAD_PAD), pos), pl.BlockSpec((tm, HEAD_PAD), pos),
            _const_spec((8, LANES)),
        ],
        out_specs=[
            pl.BlockSpec((tm, MLA_HEADS * HEAD_PAD), row),
            pl.BlockSpec((tm, MLA_HEADS * HEAD_PAD), row),
            pl.BlockSpec((tm, MLA_HEADS * HEAD_PAD), row),
            pl.BlockSpec((tm, dn_w), row),
            pl.BlockSpec((tm, DN_HEADS * DN_DV), row),
            pl.BlockSpec((tm, 2 * D_MODEL), row),
            pl.BlockSpec((2, tm, 2 * DN_HEADS), lambda i: (0, i, 0)),
        ],
        out_shape=[
            jax.ShapeDtypeStruct((T, MLA_HEADS * HEAD_PAD), BF16),
            jax.ShapeDtypeStruct((T, MLA_HEADS * HEAD_PAD), BF16),
            jax.ShapeDtypeStruct((T, MLA_HEADS * HEAD_PAD), BF16),
            jax.ShapeDtypeStruct((T, dn_w), BF16),
            jax.ShapeDtypeStruct((T, DN_HEADS * DN_DV), BF16),
            jax.ShapeDtypeStruct((T, 2 * D_MODEL), BF16),
            jax.ShapeDtypeStruct((2, T, 2 * DN_HEADS), F32),
        ],
        compiler_params=_params(("parallel",)),
        name="in_proj",
    )(x2, W["w1"], W["qan"], W["kvan"], W["wq"], W["wqs"], W["wk"], W["wv"],
      W["cosq"], W["sinq"], W["cosk"], W["sink"], W["abp"])


def _attn_kernel(q_ref, k_ref, v_ref, o_ref, *, tk):
    tq = q_ref.shape[0]
    S = k_ref.shape[0]
    q = q_ref[...]

    def body(j, carry):
        m, acc = carry
        off = pl.multiple_of(j * tk, tk)
        s = _dot_nt(q, k_ref[pl.ds(off, tk), :])
        m_new = jnp.maximum(m, jnp.max(s, axis=-1, keepdims=True))
        p = jnp.exp2(s - m_new).astype(BF16)
        acc = acc * jnp.exp2(m - m_new) + _dot(p, v_ref[pl.ds(off, tk), :])
        return m_new, acc

    m0 = jnp.full((tq, 1), -1e30, F32)
    acc0 = jnp.zeros((tq, HEAD_PAD), F32)
    _, acc = lax.fori_loop(0, S // tk, body, (m0, acc0))
    o_ref[...] = (acc / acc[:, V_DIM:V_DIM + 1]).astype(BF16)


def _attention(q, k, v, B, S, tq=256, tk=512):
    T = q.shape[0]
    nq = S // tq
    tk = min(tk, S)
    return pl.pallas_call(
        functools.partial(_attn_kernel, tk=tk),
        grid=(B, MLA_HEADS, nq),
        in_specs=[
            pl.BlockSpec((tq, HEAD_PAD), lambda b, h, i: (b * nq + i, h)),
            pl.BlockSpec((S, HEAD_PAD), lambda b, h, i: (b, h)),
            pl.BlockSpec((S, HEAD_PAD), lambda b, h, i: (b, h)),
        ],
        out_specs=pl.BlockSpec((tq, HEAD_PAD), lambda b, h, i: (b * nq + i, h)),
        out_shape=jax.ShapeDtypeStruct((T, MLA_HEADS * HEAD_PAD), BF16),
        compiler_params=_params(("parallel", "parallel", "arbitrary")),
        name="attention",
    )(q, k, v)


_HALO = 16


def _dn_prep_kernel(x_ref, prev_ref, next_ref, cw_ref, eq_ref, eqt_ref, evk_ref, evkt_ref,
                    gb_ref, tri_ref, qn_ref, vk_ref, aux_ref, *, nseq):
    i = pl.program_id(0)
    tp = x_ref.shape[0]
    first = (i % nseq) == 0
    last = (i % nseq) == nseq - 1
    prev = jnp.where(first, 0.0, prev_ref[...].astype(F32))
    nxt = jnp.where(last, 0.0, next_ref[...].astype(F32))
    xe = jnp.concatenate([prev, x_ref[...].astype(F32), nxt], axis=0)
    n = tp + 2 * _HALO
    y = None
    for j in range(CONV_K):
        shift = (CONV_K // 2 - j) % n
        xs = xe if shift == 0 else pltpu.roll(xe, shift, axis=0)
        term = xs[_HALO:_HALO + tp, :] * cw_ref[j:j + 1, :]
        y = term if y is None else y + term
    y = y * (1.0 / (1.0 + jnp.exp(-y)))

    def group_scale(v, e_ref, et_ref):
        sq = v * v
        hi, lo = _split_bf16(sq)
        ss = _dot(hi, e_ref[...]) + _dot(lo, e_ref[...])
        inv = lax.rsqrt(ss + 1e-6)
        ih, il = _split_bf16(inv)
        return _dot(ih, et_ref[...]) + _dot(il, et_ref[...])

    nq = DN_HEADS * DN_DK
    yq = y[:, :nq]
    qn_ref[...] = (yq * group_scale(yq, eq_ref, eqt_ref) * (DN_DK ** -0.5)).astype(BF16)
    yvk = y[:, nq:]
    sc = group_scale(yvk, evk_ref, evkt_ref)
    lane = lax.broadcasted_iota(jnp.int32, (1, yvk.shape[1]), 1)
    is_k = (lane // DN_DV) % 2 == 1
    vk_ref[...] = (yvk * jnp.where(is_k, sc, 1.0)).astype(BF16)

    hcol = lax.broadcasted_iota(jnp.int32, (1, 2 * DN_HEADS), 1) < DN_HEADS
    for d in range(2):
        gb = gb_ref[d]
        for blk in range(tp // DN_BLOCK):
            rs = slice(blk * DN_BLOCK, (blk + 1) * DN_BLOCK)
            g = gb[rs]
            cs = jnp.dot(tri_ref[d], g, preferred_element_type=F32, precision=lax.Precision.HIGHEST)
            aux_ref[d, rs, :] = jnp.where(hcol, cs, g)


def _dn_prep(dn, gb, S, W, tp=256):
    T = dn.shape[0]
    nseq = S // tp
    hb = tp // _HALO
    nh = T // _HALO
    dn_w = dn.shape[1]
    nq = DN_HEADS * DN_DK
    return pl.pallas_call(
        functools.partial(_dn_prep_kernel, nseq=nseq),
        grid=(T // tp,),
        in_specs=[
            pl.BlockSpec((tp, dn_w), lambda i: (i, 0)),
            pl.BlockSpec((_HALO, dn_w), lambda i: (jnp.maximum(i * hb - 1, 0), 0)),
            pl.BlockSpec((_HALO, dn_w), lambda i: (jnp.minimum((i + 1) * hb, nh - 1), 0)),
            _const_spec((8, dn_w)),
            _const_spec((nq, LANES)), _const_spec((LANES, nq)),
            _const_spec((dn_w - nq, LANES)), _const_spec((LANES, dn_w - nq)),
            pl.BlockSpec((2, tp, 2 * DN_HEADS), lambda i: (0, i, 0)),
            _const_spec((2, DN_BLOCK, DN_BLOCK)),
        ],
        out_specs=[
            pl.BlockSpec((tp, nq), lambda i: (i, 0)),
            pl.BlockSpec((tp, dn_w - nq), lambda i: (i, 0)),
            pl.BlockSpec((2, tp, 2 * DN_HEADS), lambda i: (0, i, 0)),
        ],
        out_shape=[
            jax.ShapeDtypeStruct((T, nq), BF16),
            jax.ShapeDtypeStruct((T, dn_w - nq), BF16),
            jax.ShapeDtypeStruct((2, T, 2 * DN_HEADS), F32),
        ],
        compiler_params=_params(("parallel",)),
        name="dn_prep",
    )(dn, dn, dn, W["convw"], W["eq"], W["eqt"], W["evk"], W["evkt"], gb, W["tri"])


_M_INCL, _M_STRICT, _M_DIAG16 = 0, 1, 2
_MERGE_SIZES = (16, 32, 64, 128)
_N_MASKS = 3 + len(_MERGE_SIZES)


def _dn_masks():
    r = np.arange(DN_BLOCK)[:, None]
    c = np.arange(DN_BLOCK)[None, :]
    out = np.zeros((2, _N_MASKS, DN_BLOCK, DN_BLOCK), np.float32)
    for d in range(2):
        rr, cc = (r, c) if d == 0 else (c, r)
        out[d, _M_INCL] = rr >= cc
        out[d, _M_STRICT] = rr > cc
        out[d, _M_DIAG16] = (r // 16) == (c // 16)
        for k, s in enumerate(_MERGE_SIZES):
            out[d, 3 + k] = ((rr // s) % 2 == 1) & ((rr // s) == (cc // s) + 1)
    return out


def _deltanet_kernel(qn_ref, vk_ref, kt_ref, aux_ref, auxt_ref, mask_ref, o_ref, s_ref):
    d = pl.program_id(1)
    i = pl.program_id(2)
    C = DN_BLOCK

    @pl.when(i == 0)
    def _():
        s_ref[...] = jnp.zeros_like(s_ref)

    incl = mask_ref[0, _M_INCL]
    strict = mask_ref[0, _M_STRICT]
    rowi = lax.broadcasted_iota(jnp.int32, (C, C), 0)
    coli = lax.broadcasted_iota(jnp.int32, (C, C), 1)
    eye = (rowi == coli).astype(F32)
    lane = lax.broadcasted_iota(jnp.int32, (1, LANES), 1)
    r64 = lax.broadcasted_iota(jnp.int32, (DN_DK, LANES), 0)
    c64 = lax.broadcasted_iota(jnp.int32, (DN_DK, LANES), 1)
    zeros_kt = jnp.zeros((DN_DK, C), BF16)
    zeros_s = jnp.zeros((DN_DK, LANES), F32)

    o_pair = None
    for h in range(DN_HEADS):
        par = h % 2
        pair = h // 2
        q2 = qn_ref[:, pair * LANES:(pair + 1) * LANES]
        vk = vk_ref[:, h * LANES:(h + 1) * LANES]
        kt = kt_ref[h * DN_DK:(h + 1) * DN_DK, :]
        gc_c = aux_ref[0, :, h:h + 1]
        beta_c = aux_ref[0, :, DN_HEADS + h:DN_HEADS + h + 1]
        gc_r = auxt_ref[0, h:h + 1, :]
        g_tot = jnp.where(d == 0, gc_r[:, C - 1:C], gc_r[:, 0:1])

        kt_for_q = jnp.concatenate([kt, zeros_kt] if par == 0 else [zeros_kt, kt], axis=0)
        kt_for_k = jnp.concatenate([zeros_kt, kt], axis=0)
        qk = _dot(q2, kt_for_q)
        kk = _dot(vk, kt_for_k)
        e0 = jnp.exp(jnp.minimum(gc_c - gc_r, 0.0))
        a_qk = (qk * e0 * incl).astype(BF16)
        nmat = -(kk * e0 * strict) * beta_c

        nd = (nmat * mask_ref[0, _M_DIAG16]).astype(BF16)
        p = eye + nd.astype(F32)
        sq = _dot(nd, nd)
        for it in range(3):
            sqb = sq.astype(BF16)
            p = p + _dot(p.astype(BF16), sqb)
            if it < 2:
                sq = _dot(sqb, sqb)
        for k in range(len(_MERGE_SIZES)):
            off = (nmat * mask_ref[0, 3 + k]).astype(BF16)
            pb = p.astype(BF16)
            p = p + _dot(pb, _dot(off, pb).astype(BF16))
        pb = p.astype(BF16)

        egc = jnp.exp(gc_c)
        rhs = (vk.astype(F32) * beta_c * jnp.where(lane < DN_DV, 1.0, egc)).astype(BF16)
        uw = _dot(pb, rhs).astype(BF16)

        s_pl = s_ref[h]
        eye_pl = (c64 == r64 + par * DN_DV).astype(F32)
        s_aug = jnp.concatenate([eye_pl, -s_pl], axis=0).astype(BF16)
        v_new = _dot(uw, s_aug).astype(BF16)
        s_sel = jnp.concatenate([s_pl, zeros_s] if par == 0 else [zeros_s, s_pl], axis=0).astype(BF16)
        qd = (q2.astype(F32) * egc).astype(BF16)
        o_pl = _dot(qd, s_sel) + _dot(a_qk, v_new)
        kd = (kt.astype(F32) * jnp.exp(g_tot - gc_r)).astype(BF16)
        s_ref[h] = s_pl * jnp.exp(g_tot) + _dot(kd, v_new)

        if par == 0:
            o_pair = o_pl
        else:
            o_ref[0, :, pair * LANES:(pair + 1) * LANES] = o_pair + o_pl


def _deltanet(qn, vk, kt, aux, auxt, B, S, W):
    T = qn.shape[0]
    C = DN_BLOCK
    nb = S // C

    def blk(b, d, i):
        return b * nb + jnp.where(d == 0, i, nb - 1 - i)

    nq = DN_HEADS * DN_DK
    return pl.pallas_call(
        _deltanet_kernel,
        grid=(B, 2, nb),
        in_specs=[
            pl.BlockSpec((C, nq), lambda b, d, i: (blk(b, d, i), 0)),
            pl.BlockSpec((C, 2 * nq), lambda b, d, i: (blk(b, d, i), 0)),
            pl.BlockSpec((nq, C), lambda b, d, i: (0, blk(b, d, i))),
            pl.BlockSpec((1, C, 2 * DN_HEADS), lambda b, d, i: (d, blk(b, d, i), 0)),
            pl.BlockSpec((1, 2 * DN_HEADS, C), lambda b, d, i: (d, 0, blk(b, d, i))),
            pl.BlockSpec((1, _N_MASKS, C, C), lambda b, d, i: (d, 0, 0, 0)),
        ],
        out_specs=pl.BlockSpec((1, C, DN_HEADS * DN_DV), lambda b, d, i: (d, blk(b, d, i), 0)),
        out_shape=jax.ShapeDtypeStruct((2, T, DN_HEADS * DN_DV), F32),
        scratch_shapes=[pltpu.VMEM((DN_HEADS, DN_DK, LANES), F32)],
        compiler_params=_params(("parallel", "arbitrary", "arbitrary")),
        name="deltanet",
    )(qn, vk, kt, aux, auxt, W["dn_masks"])


def _layer_norm(v, g, b):
    mu = jnp.mean(v, axis=-1, keepdims=True)
    c = v - mu
    var = jnp.mean(c * c, axis=-1, keepdims=True)
    return c * lax.rsqrt(var + 1e-5) * g + b


def _mix_kernel(x_ref, attn_ref, of_ref, ob_ref, z_ref, gates_ref, p_ref,
                woa_ref, wod_ref, wout_ref, e8_ref, e8t_ref, dnorm_ref, ln1g_ref, ln1b_ref,
                wpg_ref, wpp_ref, rwh_ref, rwl_ref, rb_ref,
                r_ref, hb_ref, ti_ref, tg_ref):
    oa = _dot(attn_ref[...], woa_ref[...])
    o = of_ref[0] + ob_ref[0]
    hi, lo = _split_bf16(o * o)
    ms = (_dot(hi, e8_ref[...]) + _dot(lo, e8_ref[...])) * (1.0 / DN_DV)
    ih, il = _split_bf16(lax.rsqrt(ms + 1e-6))
    sc = _dot(ih, e8t_ref[...]) + _dot(il, e8t_ref[...])
    zf = z_ref[...].astype(F32)
    od_in = o * sc * dnorm_ref[...] * (zf * (1.0 / (1.0 + jnp.exp(-zf))))
    od = _dot(od_in.astype(BF16), wod_ref[...])
    mix = gates_ref[:, :D_MODEL].astype(F32) * oa + gates_ref[:, D_MODEL:].astype(F32) * od
    mo = _dot(mix.astype(BF16), wout_ref[...])
    h = _layer_norm(DEEPNORM_ALPHA * x_ref[...] + mo, ln1g_ref[...], ln1b_ref[...])
    hb = h.astype(BF16)
    hb_ref[...] = hb
    pg = _dot(hb, wpg_ref[...])
    pp = _dot(p_ref[...].astype(BF16), wpp_ref[...])
    r_ref[...] = DEEPNORM_ALPHA * h + pp * (1.0 / (1.0 + jnp.exp(-pg)))

    hl = (h - hb.astype(F32)).astype(BF16)
    logits = (_dot_nt(rwh_ref[...], hb) + _dot_nt(rwh_ref[...], hl)
              + _dot_nt(rwl_ref[...], hb) + rb_ref[...])
    eid = lax.broadcasted_iota(jnp.int32, logits.shape, 0)
    vals = []
    for k in range(TOP_K):
        m = jnp.max(logits, axis=0, keepdims=True)
        idx = jnp.min(jnp.where(logits == m, eid, N_EXPERTS), axis=0, keepdims=True)
        ti_ref[k:k + 1, :] = idx
        vals.append(m)
        logits = jnp.where(eid == idx, -jnp.inf, logits)
    es = [jnp.exp(v - vals[0]) for v in vals]
    den = es[0] + es[1] + es[2] + es[3]
    for k in range(TOP_K):
        tg_ref[k:k + 1, :] = es[k] / den


def _mix(x2, attn, o2, z, gates, p2, W, tm=256):
    T = x2.shape[0]
    row = lambda i: (i, 0)
    nd = DN_HEADS * DN_DV
    return pl.pallas_call(
        _mix_kernel,
        grid=(T // tm,),
        in_specs=[
            pl.BlockSpec((tm, D_MODEL), row),
            pl.BlockSpec((tm, MLA_HEADS * HEAD_PAD), row),
            pl.BlockSpec((1, tm, nd), lambda i: (0, i, 0)),
            pl.BlockSpec((1, tm, nd), lambda i: (1, i, 0)),
            pl.BlockSpec((tm, nd), row),
            pl.BlockSpec((tm, 2 * D_MODEL), row),
            pl.BlockSpec((tm, PLE_DIM), row),
            _const_spec((MLA_HEADS * HEAD_PAD, D_MODEL)), _const_spec((nd, D_MODEL)),
            _const_spec((D_MODEL, D_MODEL)),
            _const_spec((nd, LANES)), _const_spec((LANES, nd)), _const_spec((1, nd)),
            _const_spec((1, D_MODEL)), _const_spec((1, D_MODEL)),
            _const_spec((D_MODEL, D_MODEL)), _const_spec((PLE_DIM, D_MODEL)),
            _const_spec((N_EXPERTS, D_MODEL)), _const_spec((N_EXPERTS, D_MODEL)),
            _const_spec((N_EXPERTS, 1)),
        ],
        out_specs=[
            pl.BlockSpec((tm, D_MODEL), row),
            pl.BlockSpec((tm, D_MODEL), row),
            pl.BlockSpec((TOP_K, tm), lambda i: (0, i)),
            pl.BlockSpec((TOP_K, tm), lambda i: (0, i)),
        ],
        out_shape=[
            jax.ShapeDtypeStruct((T, D_MODEL), F32),
            jax.ShapeDtypeStruct((T, D_MODEL), BF16),
            jax.ShapeDtypeStruct((TOP_K, T), jnp.int32),
            jax.ShapeDtypeStruct((TOP_K, T), F32),
        ],
        compiler_params=_params(("parallel",)),
        name="mix",
    )(x2, attn, o2, o2, z, gates, p2, W["woa"], W["wod"], W["wout"], W["e8"], W["e8t"], W["dnorm"],
      W["ln1g"], W["ln1b"], W["wpg"], W["wpp"], W["rwh"], W["rwl"], W["rb"])


def _moe_kernel(blk_e_ref, nvalid_ref, xs_ref, sg_ref, wgu_ref, bgu_ref, wd_ref, bd_ref, y_ref, *, fc):
    i = pl.program_id(0)

    @pl.when(i < nvalid_ref[0])
    def _():
        xs = xs_ref[...]
        acc = None
        for c in range(D_FF // fc):
            lo, hi = c * fc, (c + 1) * fc
            gate = _dot(xs, wgu_ref[0, :, lo:hi]) + bgu_ref[0, :, lo:hi]
            up = _dot(xs, wgu_ref[0, :, D_FF + lo:D_FF + hi]) + bgu_ref[0, :, D_FF + lo:D_FF + hi]
            gate = jnp.minimum(gate, SWIGLU_LIMIT)
            up = jnp.clip(up, -SWIGLU_LIMIT, SWIGLU_LIMIT)
            act = gate * (1.0 / (1.0 + jnp.exp(-SWIGLU_ALPHA * gate))) * (up + 1.0)
            part = _dot(act.astype(BF16), wd_ref[0, lo:hi, :])
            acc = part if acc is None else acc + part
        y_ref[...] = ((acc + bd_ref[0]) * sg_ref[...]).astype(BF16)

    @pl.when(i >= nvalid_ref[0])
    def _():
        y_ref[...] = jnp.zeros_like(y_ref)


def _moe(xs, slot_gate, blk_e, nvalid, W, bm, fc=256):
    P = xs.shape[0]
    grid_spec = pltpu.PrefetchScalarGridSpec(
        num_scalar_prefetch=2,
        grid=(P // bm,),
        in_specs=[
            pl.BlockSpec((bm, D_MODEL), lambda i, be, nv: (i, 0)),
            pl.BlockSpec((bm, 1), lambda i, be, nv: (i, 0)),
            pl.BlockSpec((1, D_MODEL, 2 * D_FF), lambda i, be, nv: (be[i], 0, 0)),
            pl.BlockSpec((1, 1, 2 * D_FF), lambda i, be, nv: (be[i], 0, 0)),
            pl.BlockSpec((1, D_FF, D_MODEL), lambda i, be, nv: (be[i], 0, 0)),
            pl.BlockSpec((1, 1, D_MODEL), lambda i, be, nv: (be[i], 0, 0)),
        ],
        out_specs=pl.BlockSpec((bm, D_MODEL), lambda i, be, nv: (i, 0)),
    )
    return pl.pallas_call(
        functools.partial(_moe_kernel, fc=fc),
        grid_spec=grid_spec,
        out_shape=jax.ShapeDtypeStruct((P, D_MODEL), BF16),
        compiler_params=_params(("arbitrary",)),
        name="moe",
    )(blk_e, nvalid, xs, slot_gate, W["wgu"], W["bgu"], W["wd"], W["bd"])


def _final_kernel(r_ref, f_ref, g_ref, b_ref, y_ref):
    y_ref[...] = _layer_norm(r_ref[...] + f_ref[...].astype(F32), g_ref[...], b_ref[...])


def _final(r, ffn, W, tm=512):
    T = r.shape[0]
    tm = min(tm, T)
    row = lambda i: (i, 0)
    return pl.pallas_call(
        _final_kernel,
        grid=(T // tm,),
        in_specs=[pl.BlockSpec((tm, D_MODEL), row), pl.BlockSpec((tm, D_MODEL), row),
                  _const_spec((1, D_MODEL)), _const_spec((1, D_MODEL))],
        out_specs=pl.BlockSpec((tm, D_MODEL), row),
        out_shape=jax.ShapeDtypeStruct((T, D_MODEL), F32),
        compiler_params=_params(("parallel",)),
        name="final_ln",
    )(r, ffn, W["ln2g"], W["ln2b"])


def _pad_heads(w, n_heads, width, start, size, dst=0):
    K = w.shape[0]
    w3 = w.reshape(K, n_heads, width)[:, :, start:start + size]
    out = jnp.zeros((K, n_heads, HEAD_PAD), w.dtype)
    out = out.at[:, :, dst:dst + size].set(w3)
    return out.reshape(K, n_heads * HEAD_PAD)


def _prep_weights(w_in, q_a_norm, w_uq, kv_a_norm, w_ukv, w_o_attn, dn_conv, dn_a_log, dn_dt_bias,
                  dn_norm, w_o_dn, w_out, ln1_g, ln1_b, router_w, router_b, w_gate_up, b_gate_up,
                  w_down, b_down, ple_w_proj, ple_w_gate, ln2_g, ln2_b):
    W = {}
    half = ROPE_DIM // 2
    o = 0
    cq = w_in[:, o:o + Q_LORA]; o += Q_LORA
    ckv = w_in[:, o:o + KV_LORA]; o += KV_LORA
    kr = w_in[:, o:o + ROPE_DIM]; o += ROPE_DIM
    nqk = DN_HEADS * DN_DK
    dq = w_in[:, o:o + nqk]; o += nqk
    dk = w_in[:, o:o + nqk]; o += nqk
    dv = w_in[:, o:o + DN_HEADS * DN_DV]; o += DN_HEADS * DN_DV
    dz = w_in[:, o:o + DN_HEADS * DN_DV]; o += DN_HEADS * DN_DV
    da = w_in[:, o:o + 2 * DN_HEADS]; o += 2 * DN_HEADS
    db = w_in[:, o:o + 2 * DN_HEADS]; o += 2 * DN_HEADS
    gate = w_in[:, o:o + 2 * D_MODEL]

    def lane_block(parts):
        w = jnp.concatenate(parts, axis=1)
        return jnp.pad(w, ((0, 0), (0, LANES - w.shape[1])))

    zeros64 = jnp.zeros((D_MODEL, NOPE_DIM), F32)
    kr_blk = lane_block([zeros64, kr])
    krs_blk = lane_block([zeros64, kr[:, half:], kr[:, :half]])
    ab = [lane_block([da[:, d * DN_HEADS:(d + 1) * DN_HEADS], db[:, d * DN_HEADS:(d + 1) * DN_HEADS]])
          for d in range(2)]

    def interleave_vk(v, k):
        lead = v.shape[:-1]
        v3 = v.reshape(lead + (DN_HEADS, DN_DV))
        k3 = k.reshape(lead + (DN_HEADS, DN_DK))
        return jnp.concatenate([v3, k3], axis=-1).reshape(lead + (DN_HEADS * (DN_DV + DN_DK),))

    W["w1"] = jnp.concatenate([cq, ckv, kr_blk, krs_blk, ab[0], ab[1], dq, interleave_vk(dv, dk), dz, gate],
                              axis=1).astype(BF16)
    W["qan"] = q_a_norm.reshape(1, Q_LORA)
    W["kvan"] = kv_a_norm.reshape(1, KV_LORA)

    qw = NOPE_DIM + ROPE_DIM
    wq_nope = _pad_heads(w_uq, MLA_HEADS, qw, 0, NOPE_DIM, 0)
    wq_r1 = _pad_heads(w_uq, MLA_HEADS, qw, NOPE_DIM, half, NOPE_DIM)
    wq_r2 = _pad_heads(w_uq, MLA_HEADS, qw, NOPE_DIM + half, half, NOPE_DIM + half)
    W["wq"] = (wq_nope + wq_r1 + wq_r2).astype(BF16)
    wq_s1 = _pad_heads(w_uq, MLA_HEADS, qw, NOPE_DIM + half, half, NOPE_DIM)
    wq_s2 = _pad_heads(w_uq, MLA_HEADS, qw, NOPE_DIM, half, NOPE_DIM + half)
    W["wqs"] = (wq_s1 + wq_s2).astype(BF16)
    kvw = NOPE_DIM + V_DIM
    W["wk"] = _pad_heads(w_ukv, MLA_HEADS, kvw, 0, NOPE_DIM, 0).astype(BF16)
    W["wv"] = _pad_heads(w_ukv, MLA_HEADS, kvw, NOPE_DIM, V_DIM, 0).astype(BF16)

    neg_a = -jnp.exp(dn_a_log.astype(F32))
    abp = jnp.zeros((8, LANES), F32)
    for d in range(2):
        abp = abp.at[2 * d, :DN_HEADS].set(neg_a[d])
        abp = abp.at[2 * d + 1, :DN_HEADS].set(dn_dt_bias[d].astype(F32))
    W["abp"] = abp

    cw = jnp.concatenate([dn_conv[:, :nqk], interleave_vk(dn_conv[:, 2 * nqk:], dn_conv[:, nqk:2 * nqk])], axis=1)
    W["convw"] = jnp.pad(cw.astype(F32), ((0, 8 - CONV_K), (0, 0)))

    def group_indicator(width, group):
        e = (np.arange(width)[:, None] // group == np.arange(LANES)[None, :]).astype(np.float32)
        return e

    eq = group_indicator(nqk, DN_DK)
    W["eq"] = jnp.asarray(eq, BF16)
    W["eqt"] = jnp.asarray(eq.T, BF16)
    evk = group_indicator(2 * nqk, DN_DK)
    W["evk"] = jnp.asarray(evk, BF16)
    W["evkt"] = jnp.asarray(evk.T, BF16)
    W["e8"] = W["eq"]
    W["e8t"] = W["eqt"]
    r = np.arange(DN_BLOCK)
    W["tri"] = jnp.asarray(np.stack([r[:, None] >= r[None, :], r[:, None] <= r[None, :]]).astype(np.float32))
    W["dn_masks"] = jnp.asarray(_dn_masks())

    woa = w_o_attn.reshape(MLA_HEADS, V_DIM, D_MODEL)
    woa = jnp.pad(woa, ((0, 0), (0, HEAD_PAD - V_DIM), (0, 0)))
    W["woa"] = woa.reshape(MLA_HEADS * HEAD_PAD, D_MODEL).astype(BF16)
    W["wod"] = w_o_dn.astype(BF16)
    W["wout"] = w_out.astype(BF16)
    W["dnorm"] = jnp.tile(dn_norm.astype(F32), DN_HEADS).reshape(1, DN_HEADS * DN_DV)
    W["ln1g"] = ln1_g.reshape(1, D_MODEL)
    W["ln1b"] = ln1_b.reshape(1, D_MODEL)
    W["ln2g"] = ln2_g.reshape(1, D_MODEL)
    W["ln2b"] = ln2_b.reshape(1, D_MODEL)
    W["wpg"] = ple_w_gate.astype(BF16)
    W["wpp"] = ple_w_proj.astype(BF16)
    rwt = router_w.T.astype(F32)
    W["rwh"], W["rwl"] = _split_bf16(rwt)
    W["rb"] = router_b.reshape(N_EXPERTS, 1).astype(F32)
    W["wgu"] = w_gate_up.astype(BF16)
    W["bgu"] = b_gate_up.reshape(N_EXPERTS, 1, 2 * D_FF).astype(F32)
    W["wd"] = w_down.astype(BF16)
    W["bd"] = b_down.reshape(N_EXPERTS, 1, D_MODEL).astype(F32)
    return W


def _rope_tables(S):
    half = ROPE_DIM // 2
    inv = ROPE_THETA ** (-jnp.arange(0, ROPE_DIM, 2, dtype=F32) / ROPE_DIM)
    ang = jnp.arange(S, dtype=F32)[:, None] * inv[None, :]
    cos, sin = jnp.cos(ang), jnp.sin(ang)
    c = (NOPE_DIM + ROPE_DIM) ** -0.5 * math.log2(math.e)
    pad = jnp.zeros((S, HEAD_PAD - NOPE_DIM - ROPE_DIM), F32)
    cos_blk = jnp.concatenate([cos, cos, pad], axis=1)
    sin_blk = jnp.concatenate([-sin, sin, pad], axis=1)
    cosq = jnp.concatenate([jnp.ones((S, NOPE_DIM), F32), cos_blk], axis=1) * c
    sinq = jnp.concatenate([jnp.zeros((S, NOPE_DIM), F32), sin_blk], axis=1) * c
    cosk = jnp.concatenate([jnp.zeros((S, NOPE_DIM), F32), cos_blk], axis=1)
    sink = jnp.concatenate([jnp.zeros((S, NOPE_DIM), F32), sin_blk], axis=1)
    return cosq, sinq, cosk, sink


def _route(top_i, top_g, T, bm):
    A = TOP_K * T
    e = top_i.reshape(A)
    onehot = (e[:, None] == jnp.arange(N_EXPERTS, dtype=jnp.int32)[None, :]).astype(jnp.int32)
    csum = jnp.cumsum(onehot, axis=0)
    rank = jnp.take_along_axis(csum, e[:, None], axis=1)[:, 0] - 1
    counts = csum[-1]
    padded = ((counts + bm - 1) // bm) * bm
    p_end = jnp.cumsum(padded)
    p_start = p_end - padded
    dest = p_start[e] + rank
    nblk = A // bm + N_EXPERTS
    P = nblk * bm
    tok = jnp.tile(jnp.arange(T, dtype=jnp.int32), TOP_K)
    slot_tok = jnp.zeros((P,), jnp.int32).at[dest].set(tok)
    slot_gate = jnp.zeros((P,), F32).at[dest].set(top_g.reshape(A))
    blk_e = jnp.minimum(jnp.searchsorted(p_end, jnp.arange(nblk, dtype=jnp.int32) * bm, side="right"),
                        N_EXPERTS - 1).astype(jnp.int32)
    nvalid = (p_end[-1] // bm).astype(jnp.int32).reshape(1)
    return dest, slot_tok, slot_gate.reshape(P, 1), blk_e, nvalid


def _layer(x, p, W, bm):
    B, S, _ = x.shape
    T = B * S
    x2 = x.reshape(T, D_MODEL)
    p2 = p.reshape(T, PLE_DIM)
    Wl = dict(W)
    Wl["cosq"], Wl["sinq"], Wl["cosk"], Wl["sink"] = _rope_tables(S)

    q, k, v, dn, z, gates, gb = _in_proj(x2, S, Wl)
    attn = _attention(q, k, v, B, S)
    qn, vk, aux = _dn_prep(dn, gb, S, Wl)
    kt = vk.reshape(T, DN_HEADS, 2, DN_DK)[:, :, 1, :].reshape(T, DN_HEADS * DN_DK).T
    auxt = jnp.swapaxes(aux, 1, 2)
    o2 = _deltanet(qn, vk, kt, aux, auxt, B, S, Wl)
    r, hb, top_i, top_g = _mix(x2, attn, o2, z, gates, p2, Wl)

    dest, slot_tok, slot_gate, blk_e, nvalid = _route(top_i, top_g, T, bm)
    xs = jnp.take(hb, slot_tok, axis=0)
    yb = _moe(xs, slot_gate, blk_e, nvalid, Wl, bm)
    ffn = jnp.take(yb, dest, axis=0).reshape(TOP_K, T, D_MODEL).astype(F32).sum(axis=0)
    y = _final(r, ffn, Wl)
    return y.reshape(B, S, D_MODEL)


def kernel(x_prompt, x_sample, p_prompt, p_sample, w_in, q_a_norm, w_uq, kv_a_norm, w_ukv, w_o_attn, dn_conv, dn_a_log, dn_dt_bias, dn_norm, w_o_dn, w_out, ln1_g, ln1_b, router_w, router_b, w_gate_up, b_gate_up, w_down, b_down, ple_w_proj, ple_w_gate, ln2_g, ln2_b):
    y_prompt, y_sample = x_prompt, x_sample
    for l in range(DEPTH):
        W = _prep_weights(w_in[l], q_a_norm[l], w_uq[l], kv_a_norm[l], w_ukv[l], w_o_attn[l], dn_conv[l],
                          dn_a_log[l], dn_dt_bias[l], dn_norm[l], w_o_dn[l], w_out[l], ln1_g[l], ln1_b[l],
                          router_w[l], router_b[l], w_gate_up[l], b_gate_up[l], w_down[l], b_down[l],
                          ple_w_proj[l], ple_w_gate[l], ln2_g[l], ln2_b[l])
        y_prompt = _layer(y_prompt, p_prompt[l], W, bm=256)
        y_sample = _layer(y_sample, p_sample[l], W, bm=256)
    return (y_prompt, y_sample)
```

```python
import functools
import math

import numpy as np
import jax
import jax.numpy as jnp
from jax import lax
from jax.experimental import pallas as pl
from jax.experimental.pallas import tpu as pltpu

D_MODEL = 1024
MLA_HEADS = 8
Q_LORA = 256
KV_LORA = 128
NOPE_DIM = 64
ROPE_DIM = 32
V_DIM = 64
ROPE_THETA = 10000.0
DN_HEADS = 8
DN_DK = 64
DN_DV = 64
CONV_K = 5
N_EXPERTS = 32
TOP_K = 4
D_FF = 1024
SWIGLU_LIMIT = 7.0
SWIGLU_ALPHA = 1.702
PLE_DIM = 256
DEPTH = 1
DEEPNORM_ALPHA = (2.0 * DEPTH) ** 0.25

LANES = 128
HEAD_PAD = 128
DN_BLOCK = 256
VMEM_LIMIT = 56 * 1024 * 1024

_C_CQ = 0
_C_CKV = _C_CQ + Q_LORA
_C_KR = _C_CKV + KV_LORA
_C_KRS = _C_KR + LANES
_C_AB0 = _C_KRS + LANES
_C_AB1 = _C_AB0 + LANES
_C_DNQ = _C_AB1 + LANES
_C_DNVK = _C_DNQ + DN_HEADS * DN_DK
_C_Z = _C_DNVK + DN_HEADS * (DN_DK + DN_DV)
_C_GATE = _C_Z + DN_HEADS * DN_DV
_C_END = _C_GATE + 2 * D_MODEL

BF16 = jnp.bfloat16
F32 = jnp.float32


def _dot(a, b):
    return jnp.dot(a, b, preferred_element_type=F32)


def _dot_nt(a, b):
    return lax.dot_general(a, b, (((1,), (1,)), ((), ())), preferred_element_type=F32)


def _split_bf16(x):
    hi = x.astype(BF16)
    lo = (x - hi.astype(F32)).astype(BF16)
    return hi, lo


def _const_spec(shape):
    n = len(shape)
    return pl.BlockSpec(shape, lambda *_: (0,) * n)


def _params(sem):
    return pltpu.CompilerParams(dimension_semantics=sem, vmem_limit_bytes=VMEM_LIMIT)


def _in_proj_kernel(x_ref, w1_ref, qan_ref, kvan_ref, wq_ref, wqs_ref, wk_ref, wv_ref,
                    cosq_ref, sinq_ref, cosk_ref, sink_ref, abp_ref,
                    q_ref, k_ref, v_ref, dn_ref, z_ref, gates_ref, gb_ref):
    xb = x_ref[...].astype(BF16)

    def proj(lo, hi):
        return _dot(xb, w1_ref[:, lo:hi])

    def rms(c, g):
        return (c * lax.rsqrt(jnp.mean(c * c, axis=-1, keepdims=True) + 1e-6) * g).astype(BF16)

    cqn = rms(proj(_C_CQ, _C_CKV), qan_ref[...])
    qa = _dot(cqn, wq_ref[...])
    qb = _dot(cqn, wqs_ref[...])
    ckvn = rms(proj(_C_CKV, _C_KR), kvan_ref[...])
    kw = _dot(ckvn, wk_ref[...])
    vw = _dot(ckvn, wv_ref[...])
    kr = proj(_C_KR, _C_KRS) * cosk_ref[...] + proj(_C_KRS, _C_AB0) * sink_ref[...]
    cosq = cosq_ref[...]
    sinq = sinq_ref[...]
    lane = lax.broadcasted_iota(jnp.int32, (1, HEAD_PAD), 1)
    ones_col = (lane == V_DIM).astype(F32)
    for h in range(MLA_HEADS):
        sl = slice(h * HEAD_PAD, (h + 1) * HEAD_PAD)
        q_ref[:, sl] = (qa[:, sl] * cosq + qb[:, sl] * sinq).astype(BF16)
        k_ref[:, sl] = (kw[:, sl] + kr).astype(BF16)
        v_ref[:, sl] = (vw[:, sl] + ones_col).astype(BF16)

    for d, c0 in enumerate((_C_AB0, _C_AB1)):
        ab = proj(c0, c0 + LANES)
        neg_a = abp_ref[2 * d:2 * d + 1, :]
        dtb = abp_ref[2 * d + 1:2 * d + 2, :]
        t = ab + dtb
        sp = jnp.maximum(t, 0.0) + jnp.log(1.0 + jnp.exp(-jnp.abs(t)))
        g = neg_a * sp
        beta = 1.0 / (1.0 + jnp.exp(-ab))
        gb_ref[d] = jnp.where(lane < DN_HEADS, g, beta)[:, :2 * DN_HEADS]

    dn_ref[...] = proj(_C_DNQ, _C_Z).astype(BF16)
    z_ref[...] = proj(_C_Z, _C_GATE).astype(BF16)
    gl = proj(_C_GATE, _C_END)
    gates_ref[...] = (1.0 / (1.0 + jnp.exp(-gl))).astype(BF16)


def _in_proj(x2, S, W, tm=256):
    T = x2.shape[0]
    nseq = S // tm
    row = lambda i: (i, 0)
    pos = lambda i: (i % nseq, 0)
    dn_w = _C_Z - _C_DNQ
    return pl.pallas_call(
        _in_proj_kernel,
        grid=(T // tm,),
        in_specs=[
            pl.BlockSpec((tm, D_MODEL), row),
            _const_spec((D_MODEL, _C_END)),
            _const_spec((1, Q_LORA)), _const_spec((1, KV_LORA)),
            _const_spec((Q_LORA, MLA_HEADS * HEAD_PAD)), _const_spec((Q_LORA, MLA_HEADS * HEAD_PAD)),
            _const_spec((KV_LORA, MLA_HEADS * HEAD_PAD)), _const_spec((KV_LORA, MLA_HEADS * HEAD_PAD)),
            pl.BlockSpec((tm, HEAD_PAD), pos), pl.BlockSpec((tm, HEAD_PAD), pos),
            pl.BlockSpec((tm, HEAD_PAD), pos), pl.BlockSpec((tm, HEAD_PAD), pos),
            _const_spec((8, LANES)),
        ],
        out_specs=[
            pl.BlockSpec((tm, MLA_HEADS * HEAD_PAD), row),
            pl.BlockSpec((tm, MLA_HEADS * HEAD_PAD), row),
            pl.BlockSpec((tm, MLA_HEADS * HEAD_PAD), row),
            pl.BlockSpec((tm, dn_w), row),
            pl.BlockSpec((tm, DN_HEADS * DN_DV), row),
            pl.BlockSpec((tm, 2 * D_MODEL), row),
            pl.BlockSpec((2, tm, 2 * DN_HEADS), lambda i: (0, i, 0)),
        ],
        out_shape=[
            jax.ShapeDtypeStruct((T, MLA_HEADS * HEAD_PAD), BF16),
            jax.ShapeDtypeStruct((T, MLA_HEADS * HEAD_PAD), BF16),
            jax.ShapeDtypeStruct((T, MLA_HEADS * HEAD_PAD), BF16),
            jax.ShapeDtypeStruct((T, dn_w), BF16),
            jax.ShapeDtypeStruct((T, DN_HEADS * DN_DV), BF16),
            jax.ShapeDtypeStruct((T, 2 * D_MODEL), BF16),
            jax.ShapeDtypeStruct((2, T, 2 * DN_HEADS), F32),
        ],
        compiler_params=_params(("parallel",)),
        name="in_proj",
    )(x2, W["w1"], W["qan"], W["kvan"], W["wq"], W["wqs"], W["wk"], W["wv"],
      W["cosq"], W["sinq"], W["cosk"], W["sink"], W["abp"])


def _attn_kernel(q_ref, k_ref, v_ref, o_ref, *, tk):
    tq = q_ref.shape[0]
    S = k_ref.shape[0]
    q = q_ref[...]

    def body(j, carry):
        m, acc = carry
        off = pl.multiple_of(j * tk, tk)
        s = _dot_nt(q, k_ref[pl.ds(off, tk), :])
        m_new = jnp.maximum(m, jnp.max(s, axis=-1, keepdims=True))
        p = jnp.exp2(s - m_new).astype(BF16)
        acc = acc * jnp.exp2(m - m_new) + _dot(p, v_ref[pl.ds(off, tk), :])
        return m_new, acc

    m0 = jnp.full((tq, 1), -1e30, F32)
    acc0 = jnp.zeros((tq, HEAD_PAD), F32)
    _, acc = lax.fori_loop(0, S // tk, body, (m0, acc0), unroll=2)
    o_ref[...] = (acc / acc[:, V_DIM:V_DIM + 1]).astype(BF16)


def _attention(q, k, v, B, S, tq=512, tk=1024):
    T = q.shape[0]
    nq = S // tq
    tk = min(tk, S)
    return pl.pallas_call(
        functools.partial(_attn_kernel, tk=tk),
        grid=(B, MLA_HEADS, nq),
        in_specs=[
            pl.BlockSpec((tq, HEAD_PAD), lambda b, h, i: (b * nq + i, h)),
            pl.BlockSpec((S, HEAD_PAD), lambda b, h, i: (b, h)),
            pl.BlockSpec((S, HEAD_PAD), lambda b, h, i: (b, h)),
        ],
        out_specs=pl.BlockSpec((tq, HEAD_PAD), lambda b, h, i: (b * nq + i, h)),
        out_shape=jax.ShapeDtypeStruct((T, MLA_HEADS * HEAD_PAD), BF16),
        compiler_params=_params(("parallel", "parallel", "arbitrary")),
        name="attention",
    )(q, k, v)


_HALO = 16


def _dn_prep_kernel(x_ref, prev_ref, next_ref, cw_ref, eq_ref, eqt_ref, evk_ref, evkt_ref,
                    gb_ref, tri_ref, qn_ref, vk_ref, aux_ref, *, nseq):
    i = pl.program_id(0)
    tp = x_ref.shape[0]
    first = (i % nseq) == 0
    last = (i % nseq) == nseq - 1
    prev = jnp.where(first, 0.0, prev_ref[...].astype(F32))
    nxt = jnp.where(last, 0.0, next_ref[...].astype(F32))
    xe = jnp.concatenate([prev, x_ref[...].astype(F32), nxt], axis=0)
    n = tp + 2 * _HALO
    y = None
    for j in range(CONV_K):
        shift = (CONV_K // 2 - j) % n
        xs = xe if shift == 0 else pltpu.roll(xe, shift, axis=0)
        term = xs[_HALO:_HALO + tp, :] * cw_ref[j:j + 1, :]
        y = term if y is None else y + term
    y = y * (1.0 / (1.0 + jnp.exp(-y)))

    def group_scale(v, e_ref, et_ref):
        sq = v * v
        hi, lo = _split_bf16(sq)
        ss = _dot(hi, e_ref[...]) + _dot(lo, e_ref[...])
        inv = lax.rsqrt(ss + 1e-6)
        ih, il = _split_bf16(inv)
        return _dot(ih, et_ref[...]) + _dot(il, et_ref[...])

    nq = DN_HEADS * DN_DK
    yq = y[:, :nq]
    qn_ref[...] = (yq * group_scale(yq, eq_ref, eqt_ref) * (DN_DK ** -0.5)).astype(BF16)
    yvk = y[:, nq:]
    sc = group_scale(yvk, evk_ref, evkt_ref)
    lane = lax.broadcasted_iota(jnp.int32, (1, yvk.shape[1]), 1)
    is_k = (lane // DN_DV) % 2 == 1
    vk_ref[...] = (yvk * jnp.where(is_k, sc, 1.0)).astype(BF16)

    hcol = lax.broadcasted_iota(jnp.int32, (1, 2 * DN_HEADS), 1) < DN_HEADS
    for d in range(2):
        gb = gb_ref[d]
        for blk in range(tp // DN_BLOCK):
            rs = slice(blk * DN_BLOCK, (blk + 1) * DN_BLOCK)
            g = gb[rs]
            cs = jnp.dot(tri_ref[d], g, preferred_element_type=F32, precision=lax.Precision.HIGHEST)
            aux_ref[d, rs, :] = jnp.where(hcol, cs, g)


def _dn_prep(dn, gb, S, W, tp=256):
    T = dn.shape[0]
    nseq = S // tp
    hb = tp // _HALO
    nh = T // _HALO
    dn_w = dn.shape[1]
    nq = DN_HEADS * DN_DK
    return pl.pallas_call(
        functools.partial(_dn_prep_kernel, nseq=nseq),
        grid=(T // tp,),
        in_specs=[
            pl.BlockSpec((tp, dn_w), lambda i: (i, 0)),
            pl.BlockSpec((_HALO, dn_w), lambda i: (jnp.maximum(i * hb - 1, 0), 0)),
            pl.BlockSpec((_HALO, dn_w), lambda i: (jnp.minimum((i + 1) * hb, nh - 1), 0)),
            _const_spec((8, dn_w)),
            _const_spec((nq, LANES)), _const_spec((LANES, nq)),
            _const_spec((dn_w - nq, LANES)), _const_spec((LANES, dn_w - nq)),
            pl.BlockSpec((2, tp, 2 * DN_HEADS), lambda i: (0, i, 0)),
            _const_spec((2, DN_BLOCK, DN_BLOCK)),
        ],
        out_specs=[
            pl.BlockSpec((tp, nq), lambda i: (i, 0)),
            pl.BlockSpec((tp, dn_w - nq), lambda i: (i, 0)),
            pl.BlockSpec((2, tp, 2 * DN_HEADS), lambda i: (0, i, 0)),
        ],
        out_shape=[
            jax.ShapeDtypeStruct((T, nq), BF16),
            jax.ShapeDtypeStruct((T, dn_w - nq), BF16),
            jax.ShapeDtypeStruct((2, T, 2 * DN_HEADS), F32),
        ],
        compiler_params=_params(("parallel",)),
        name="dn_prep",
    )(dn, dn, dn, W["convw"], W["eq"], W["eqt"], W["evk"], W["evkt"], gb, W["tri"])


_M_INCL, _M_STRICT, _M_DIAG16 = 0, 1, 2
_MERGE_SIZES = (16, 32, 64, 128)
_N_MASKS = 3 + len(_MERGE_SIZES)


def _dn_masks():
    r = np.arange(DN_BLOCK)[:, None]
    c = np.arange(DN_BLOCK)[None, :]
    out = np.zeros((2, _N_MASKS, DN_BLOCK, DN_BLOCK), np.float32)
    for d in range(2):
        rr, cc = (r, c) if d == 0 else (c, r)
        out[d, _M_INCL] = rr >= cc
        out[d, _M_STRICT] = rr > cc
        out[d, _M_DIAG16] = (r // 16) == (c // 16)
        for k, s in enumerate(_MERGE_SIZES):
            out[d, 3 + k] = ((rr // s) % 2 == 1) & ((rr // s) == (cc // s) + 1)
    return out


def _deltanet_kernel(qn_ref, vk_ref, kt_ref, aux_ref, auxt_ref, mask_ref, o_ref,
                     s_ref, nm_ref, p_ref, x_ref, aqk_ref):
    d = pl.program_id(1)
    i = pl.program_id(2)
    C = DN_BLOCK
    heads = range(DN_HEADS)

    @pl.when(i == 0)
    def _():
        s_ref[...] = jnp.zeros_like(s_ref)

    rowi = lax.broadcasted_iota(jnp.int32, (C, C), 0)
    coli = lax.broadcasted_iota(jnp.int32, (C, C), 1)
    eye = (rowi == coli).astype(F32)
    lane = lax.broadcasted_iota(jnp.int32, (1, LANES), 1)
    r64 = lax.broadcasted_iota(jnp.int32, (DN_DK, LANES), 0)
    c64 = lax.broadcasted_iota(jnp.int32, (DN_DK, LANES), 1)
    zeros_kt = jnp.zeros((DN_DK, C), BF16)
    zeros_s = jnp.zeros((DN_DK, LANES), F32)

    def q_pair(h):
        return qn_ref[:, (h // 2) * LANES:(h // 2 + 1) * LANES]

    def vk_head(h):
        return vk_ref[:, h * LANES:(h + 1) * LANES]

    def kt_head(h):
        return kt_ref[h * DN_DK:(h + 1) * DN_DK, :]

    def gc_col(h):
        return aux_ref[0, :, h:h + 1]

    def beta_col(h):
        return aux_ref[0, :, DN_HEADS + h:DN_HEADS + h + 1]

    def gc_row(h):
        return auxt_ref[0, h:h + 1, :]

    for h in heads:
        kt = kt_head(h)
        kt_for_q = jnp.concatenate([kt, zeros_kt] if h % 2 == 0 else [zeros_kt, kt], axis=0)
        kt_for_k = jnp.concatenate([zeros_kt, kt], axis=0)
        qk = _dot(q_pair(h), kt_for_q)
        kk = _dot(vk_head(h), kt_for_k)
        e0 = jnp.exp(jnp.minimum(gc_col(h) - gc_row(h), 0.0))
        aqk_ref[h] = (qk * e0 * mask_ref[0, _M_INCL]).astype(BF16)
        nm_ref[h] = (-(kk * e0 * mask_ref[0, _M_STRICT]) * beta_col(h)).astype(BF16)

    diag16 = mask_ref[0, _M_DIAG16].astype(BF16)
    for h in heads:
        nd = nm_ref[h] * diag16
        p_ref[h] = (eye + nd.astype(F32)).astype(BF16)
        x_ref[h] = _dot(nd, nd).astype(BF16)
    for it in range(3):
        for h in heads:
            pb = p_ref[h]
            sqb = x_ref[h]
            p_ref[h] = (pb.astype(F32) + _dot(pb, sqb)).astype(BF16)
            if it < 2:
                x_ref[h] = _dot(sqb, sqb).astype(BF16)
    for k in range(len(_MERGE_SIZES)):
        off_mask = mask_ref[0, 3 + k].astype(BF16)
        for h in heads:
            x_ref[h] = _dot(nm_ref[h] * off_mask, p_ref[h]).astype(BF16)
        for h in heads:
            pb = p_ref[h]
            p_ref[h] = (pb.astype(F32) + _dot(pb, x_ref[h])).astype(BF16)

    o_pair = None
    for h in heads:
        par = h % 2
        gc_c = gc_col(h)
        gc_r = gc_row(h)
        g_tot = jnp.where(d == 0, gc_r[:, C - 1:C], gc_r[:, 0:1])
        egc = jnp.exp(gc_c)
        rhs = (vk_head(h).astype(F32) * beta_col(h) * jnp.where(lane < DN_DV, 1.0, egc)).astype(BF16)
        uw = _dot(p_ref[h], rhs).astype(BF16)
        s_pl = s_ref[h]
        eye_pl = (c64 == r64 + par * DN_DV).astype(F32)
        s_aug = jnp.concatenate([eye_pl, -s_pl], axis=0).astype(BF16)
        v_new = _dot(uw, s_aug).astype(BF16)
        s_sel = jnp.concatenate([s_pl, zeros_s] if par == 0 else [zeros_s, s_pl], axis=0).astype(BF16)
        qd = (q_pair(h).astype(F32) * egc).astype(BF16)
        o_pl = _dot(qd, s_sel) + _dot(aqk_ref[h], v_new)
        kd = (kt_head(h).astype(F32) * jnp.exp(g_tot - gc_r)).astype(BF16)
        s_ref[h] = s_pl * jnp.exp(g_tot) + _dot(kd, v_new)
        if par == 0:
            o_pair = o_pl
        else:
            o_ref[0, :, (h // 2) * LANES:(h // 2 + 1) * LANES] = o_pair + o_pl


def _deltanet(qn, vk, kt, aux, auxt, B, S, W):
    T = qn.shape[0]
    C = DN_BLOCK
    nb = S // C

    def blk(b, d, i):
        return b * nb + jnp.where(d == 0, i, nb - 1 - i)

    nq = DN_HEADS * DN_DK
    return pl.pallas_call(
        _deltanet_kernel,
        grid=(B, 2, nb),
        in_specs=[
            pl.BlockSpec((C, nq), lambda b, d, i: (blk(b, d, i), 0)),
            pl.BlockSpec((C, 2 * nq), lambda b, d, i: (blk(b, d, i), 0)),
            pl.BlockSpec((nq, C), lambda b, d, i: (0, blk(b, d, i))),
            pl.BlockSpec((1, C, 2 * DN_HEADS), lambda b, d, i: (d, blk(b, d, i), 0)),
            pl.BlockSpec((1, 2 * DN_HEADS, C), lambda b, d, i: (d, 0, blk(b, d, i))),
            pl.BlockSpec((1, _N_MASKS, C, C), lambda b, d, i: (d, 0, 0, 0)),
        ],
        out_specs=pl.BlockSpec((1, C, DN_HEADS * DN_DV), lambda b, d, i: (d, blk(b, d, i), 0)),
        out_shape=jax.ShapeDtypeStruct((2, T, DN_HEADS * DN_DV), F32),
        scratch_shapes=[pltpu.VMEM((DN_HEADS, DN_DK, LANES), F32)]
        + [pltpu.VMEM((DN_HEADS, C, C), BF16)] * 4,
        compiler_params=_params(("parallel", "arbitrary", "arbitrary")),
        name="deltanet",
    )(qn, vk, kt, aux, auxt, W["dn_masks"])


def _layer_norm(v, g, b):
    mu = jnp.mean(v, axis=-1, keepdims=True)
    c = v - mu
    var = jnp.mean(c * c, axis=-1, keepdims=True)
    return c * lax.rsqrt(var + 1e-5) * g + b


def _mix_kernel(x_ref, attn_ref, of_ref, ob_ref, z_ref, gates_ref, p_ref,
                woa_ref, wod_ref, wout_ref, e8_ref, e8t_ref, dnorm_ref, ln1g_ref, ln1b_ref,
                wpg_ref, wpp_ref, rwh_ref, rwl_ref, rb_ref,
                r_ref, hb_ref, ti_ref, tg_ref):
    oa = _dot(attn_ref[...], woa_ref[...])
    o = of_ref[0] + ob_ref[0]
    hi, lo = _split_bf16(o * o)
    ms = (_dot(hi, e8_ref[...]) + _dot(lo, e8_ref[...])) * (1.0 / DN_DV)
    ih, il = _split_bf16(lax.rsqrt(ms + 1e-6))
    sc = _dot(ih, e8t_ref[...]) + _dot(il, e8t_ref[...])
    zf = z_ref[...].astype(F32)
    od_in = o * sc * dnorm_ref[...] * (zf * (1.0 / (1.0 + jnp.exp(-zf))))
    od = _dot(od_in.astype(BF16), wod_ref[...])
    mix = gates_ref[:, :D_MODEL].astype(F32) * oa + gates_ref[:, D_MODEL:].astype(F32) * od
    mo = _dot(mix.astype(BF16), wout_ref[...])
    h = _layer_norm(DEEPNORM_ALPHA * x_ref[...] + mo, ln1g_ref[...], ln1b_ref[...])
    hb = h.astype(BF16)
    hb_ref[...] = hb
    pg = _dot(hb, wpg_ref[...])
    pp = _dot(p_ref[...].astype(BF16), wpp_ref[...])
    r_ref[...] = DEEPNORM_ALPHA * h + pp * (1.0 / (1.0 + jnp.exp(-pg)))

    hl = (h - hb.astype(F32)).astype(BF16)
    logits = (_dot_nt(rwh_ref[...], hb) + _dot_nt(rwh_ref[...], hl)
              + _dot_nt(rwl_ref[...], hb) + rb_ref[...])
    eid = lax.broadcasted_iota(jnp.int32, logits.shape, 0)
    vals = []
    for k in range(TOP_K):
        m = jnp.max(logits, axis=0, keepdims=True)
        idx = jnp.min(jnp.where(logits == m, eid, N_EXPERTS), axis=0, keepdims=True)
        ti_ref[k:k + 1, :] = idx
        vals.append(m)
        logits = jnp.where(eid == idx, -jnp.inf, logits)
    es = [jnp.exp(v - vals[0]) for v in vals]
    den = es[0] + es[1] + es[2] + es[3]
    for k in range(TOP_K):
        tg_ref[k:k + 1, :] = es[k] / den


def _mix(x2, attn, o2, z, gates, p2, W, tm=256):
    T = x2.shape[0]
    row = lambda i: (i, 0)
    nd = DN_HEADS * DN_DV
    return pl.pallas_call(
        _mix_kernel,
        grid=(T // tm,),
        in_specs=[
            pl.BlockSpec((tm, D_MODEL), row),
            pl.BlockSpec((tm, MLA_HEADS * HEAD_PAD), row),
            pl.BlockSpec((1, tm, nd), lambda i: (0, i, 0)),
            pl.BlockSpec((1, tm, nd), lambda i: (1, i, 0)),
            pl.BlockSpec((tm, nd), row),
            pl.BlockSpec((tm, 2 * D_MODEL), row),
            pl.BlockSpec((tm, PLE_DIM), row),
            _const_spec((MLA_HEADS * HEAD_PAD, D_MODEL)), _const_spec((nd, D_MODEL)),
            _const_spec((D_MODEL, D_MODEL)),
            _const_spec((nd, LANES)), _const_spec((LANES, nd)), _const_spec((1, nd)),
            _const_spec((1, D_MODEL)), _const_spec((1, D_MODEL)),
            _const_spec((D_MODEL, D_MODEL)), _const_spec((PLE_DIM, D_MODEL)),
            _const_spec((N_EXPERTS, D_MODEL)), _const_spec((N_EXPERTS, D_MODEL)),
            _const_spec((N_EXPERTS, 1)),
        ],
        out_specs=[
            pl.BlockSpec((tm, D_MODEL), row),
            pl.BlockSpec((tm, D_MODEL), row),
            pl.BlockSpec((TOP_K, tm), lambda i: (0, i)),
            pl.BlockSpec((TOP_K, tm), lambda i: (0, i)),
        ],
        out_shape=[
            jax.ShapeDtypeStruct((T, D_MODEL), F32),
            jax.ShapeDtypeStruct((T, D_MODEL), BF16),
            jax.ShapeDtypeStruct((TOP_K, T), jnp.int32),
            jax.ShapeDtypeStruct((TOP_K, T), F32),
        ],
        compiler_params=_params(("parallel",)),
        name="mix",
    )(x2, attn, o2, o2, z, gates, p2, W["woa"], W["wod"], W["wout"], W["e8"], W["e8t"], W["dnorm"],
      W["ln1g"], W["ln1b"], W["wpg"], W["wpp"], W["rwh"], W["rwl"], W["rb"])


def _moe_kernel(blk_e_ref, nvalid_ref, xs_ref, sg_ref, wgu_ref, bgu_ref, wd_ref, bd_ref, y_ref, *, fc):
    i = pl.program_id(0)

    @pl.when(i < nvalid_ref[0])
    def _():
        xs = xs_ref[...]
        acc = None
        for c in range(D_FF // fc):
            lo, hi = c * fc, (c + 1) * fc
            gate = _dot(xs, wgu_ref[0, :, lo:hi]) + bgu_ref[0, :, lo:hi]
            up = _dot(xs, wgu_ref[0, :, D_FF + lo:D_FF + hi]) + bgu_ref[0, :, D_FF + lo:D_FF + hi]
            gate = jnp.minimum(gate, SWIGLU_LIMIT)
            up = jnp.clip(up, -SWIGLU_LIMIT, SWIGLU_LIMIT)
            act = gate * (1.0 / (1.0 + jnp.exp(-SWIGLU_ALPHA * gate))) * (up + 1.0)
            part = _dot(act.astype(BF16), wd_ref[0, lo:hi, :])
            acc = part if acc is None else acc + part
        y_ref[...] = ((acc + bd_ref[0]) * sg_ref[...]).astype(BF16)

    @pl.when(i >= nvalid_ref[0])
    def _():
        y_ref[...] = jnp.zeros_like(y_ref)


def _moe(xs, slot_gate, blk_e, nvalid, W, bm, fc=256):
    P = xs.shape[0]
    grid_spec = pltpu.PrefetchScalarGridSpec(
        num_scalar_prefetch=2,
        grid=(P // bm,),
        in_specs=[
            pl.BlockSpec((bm, D_MODEL), lambda i, be, nv: (i, 0)),
            pl.BlockSpec((bm, 1), lambda i, be, nv: (i, 0)),
            pl.BlockSpec((1, D_MODEL, 2 * D_FF), lambda i, be, nv: (be[i], 0, 0)),
            pl.BlockSpec((1, 1, 2 * D_FF), lambda i, be, nv: (be[i], 0, 0)),
            pl.BlockSpec((1, D_FF, D_MODEL), lambda i, be, nv: (be[i], 0, 0)),
            pl.BlockSpec((1, 1, D_MODEL), lambda i, be, nv: (be[i], 0, 0)),
        ],
        out_specs=pl.BlockSpec((bm, D_MODEL), lambda i, be, nv: (i, 0)),
    )
    return pl.pallas_call(
        functools.partial(_moe_kernel, fc=fc),
        grid_spec=grid_spec,
        out_shape=jax.ShapeDtypeStruct((P, D_MODEL), BF16),
        compiler_params=_params(("arbitrary",)),
        name="moe",
    )(blk_e, nvalid, xs, slot_gate, W["wgu"], W["bgu"], W["wd"], W["bd"])


def _final_kernel(r_ref, f_ref, g_ref, b_ref, y_ref):
    y_ref[...] = _layer_norm(r_ref[...] + f_ref[...].astype(F32), g_ref[...], b_ref[...])


def _final(r, ffn, W, tm=512):
    T = r.shape[0]
    tm = min(tm, T)
    row = lambda i: (i, 0)
    return pl.pallas_call(
        _final_kernel,
        grid=(T // tm,),
        in_specs=[pl.BlockSpec((tm, D_MODEL), row), pl.BlockSpec((tm, D_MODEL), row),
                  _const_spec((1, D_MODEL)), _const_spec((1, D_MODEL))],
        out_specs=pl.BlockSpec((tm, D_MODEL), row),
        out_shape=jax.ShapeDtypeStruct((T, D_MODEL), F32),
        compiler_params=_params(("parallel",)),
        name="final_ln",
    )(r, ffn, W["ln2g"], W["ln2b"])


def _pad_heads(w, n_heads, width, start, size, dst=0):
    K = w.shape[0]
    w3 = w.reshape(K, n_heads, width)[:, :, start:start + size]
    out = jnp.zeros((K, n_heads, HEAD_PAD), w.dtype)
    out = out.at[:, :, dst:dst + size].set(w3)
    return out.reshape(K, n_heads * HEAD_PAD)


def _prep_weights(w_in, q_a_norm, w_uq, kv_a_norm, w_ukv, w_o_attn, dn_conv, dn_a_log, dn_dt_bias,
                  dn_norm, w_o_dn, w_out, ln1_g, ln1_b, router_w, router_b, w_gate_up, b_gate_up,
                  w_down, b_down, ple_w_proj, ple_w_gate, ln2_g, ln2_b):
    W = {}
    half = ROPE_DIM // 2
    o = 0
    cq = w_in[:, o:o + Q_LORA]; o += Q_LORA
    ckv = w_in[:, o:o + KV_LORA]; o += KV_LORA
    kr = w_in[:, o:o + ROPE_DIM]; o += ROPE_DIM
    nqk = DN_HEADS * DN_DK
    dq = w_in[:, o:o + nqk]; o += nqk
    dk = w_in[:, o:o + nqk]; o += nqk
    dv = w_in[:, o:o + DN_HEADS * DN_DV]; o += DN_HEADS * DN_DV
    dz = w_in[:, o:o + DN_HEADS * DN_DV]; o += DN_HEADS * DN_DV
    da = w_in[:, o:o + 2 * DN_HEADS]; o += 2 * DN_HEADS
    db = w_in[:, o:o + 2 * DN_HEADS]; o += 2 * DN_HEADS
    gate = w_in[:, o:o + 2 * D_MODEL]

    def lane_block(parts):
        w = jnp.concatenate(parts, axis=1)
        return jnp.pad(w, ((0, 0), (0, LANES - w.shape[1])))

    zeros64 = jnp.zeros((D_MODEL, NOPE_DIM), F32)
    kr_blk = lane_block([zeros64, kr])
    krs_blk = lane_block([zeros64, kr[:, half:], kr[:, :half]])
    ab = [lane_block([da[:, d * DN_HEADS:(d + 1) * DN_HEADS], db[:, d * DN_HEADS:(d + 1) * DN_HEADS]])
          for d in range(2)]

    def interleave_vk(v, k):
        lead = v.shape[:-1]
        v3 = v.reshape(lead + (DN_HEADS, DN_DV))
        k3 = k.reshape(lead + (DN_HEADS, DN_DK))
        return jnp.concatenate([v3, k3], axis=-1).reshape(lead + (DN_HEADS * (DN_DV + DN_DK),))

    W["w1"] = jnp.concatenate([cq, ckv, kr_blk, krs_blk, ab[0], ab[1], dq, interleave_vk(dv, dk), dz, gate],
                              axis=1).astype(BF16)
    W["qan"] = q_a_norm.reshape(1, Q_LORA)
    W["kvan"] = kv_a_norm.reshape(1, KV_LORA)

    qw = NOPE_DIM + ROPE_DIM
    wq_nope = _pad_heads(w_uq, MLA_HEADS, qw, 0, NOPE_DIM, 0)
    wq_r1 = _pad_heads(w_uq, MLA_HEADS, qw, NOPE_DIM, half, NOPE_DIM)
    wq_r2 = _pad_heads(w_uq, MLA_HEADS, qw, NOPE_DIM + half, half, NOPE_DIM + half)
    W["wq"] = (wq_nope + wq_r1 + wq_r2).astype(BF16)
    wq_s1 = _pad_heads(w_uq, MLA_HEADS, qw, NOPE_DIM + half, half, NOPE_DIM)
    wq_s2 = _pad_heads(w_uq, MLA_HEADS, qw, NOPE_DIM, half, NOPE_DIM + half)
    W["wqs"] = (wq_s1 + wq_s2).astype(BF16)
    kvw = NOPE_DIM + V_DIM
    W["wk"] = _pad_heads(w_ukv, MLA_HEADS, kvw, 0, NOPE_DIM, 0).astype(BF16)
    W["wv"] = _pad_heads(w_ukv, MLA_HEADS, kvw, NOPE_DIM, V_DIM, 0).astype(BF16)

    neg_a = -jnp.exp(dn_a_log.astype(F32))
    abp = jnp.zeros((8, LANES), F32)
    for d in range(2):
        abp = abp.at[2 * d, :DN_HEADS].set(neg_a[d])
        abp = abp.at[2 * d + 1, :DN_HEADS].set(dn_dt_bias[d].astype(F32))
    W["abp"] = abp

    cw = jnp.concatenate([dn_conv[:, :nqk], interleave_vk(dn_conv[:, 2 * nqk:], dn_conv[:, nqk:2 * nqk])], axis=1)
    W["convw"] = jnp.pad(cw.astype(F32), ((0, 8 - CONV_K), (0, 0)))

    def group_indicator(width, group):
        e = (np.arange(width)[:, None] // group == np.arange(LANES)[None, :]).astype(np.float32)
        return e

    eq = group_indicator(nqk, DN_DK)
    W["eq"] = jnp.asarray(eq, BF16)
    W["eqt"] = jnp.asarray(eq.T, BF16)
    evk = group_indicator(2 * nqk, DN_DK)
    W["evk"] = jnp.asarray(evk, BF16)
    W["evkt"] = jnp.asarray(evk.T, BF16)
    W["e8"] = W["eq"]
    W["e8t"] = W["eqt"]
    r = np.arange(DN_BLOCK)
    W["tri"] = jnp.asarray(np.stack([r[:, None] >= r[None, :], r[:, None] <= r[None, :]]).astype(np.float32))
    W["dn_masks"] = jnp.asarray(_dn_masks())

    woa = w_o_attn.reshape(MLA_HEADS, V_DIM, D_MODEL)
    woa = jnp.pad(woa, ((0, 0), (0, HEAD_PAD - V_DIM), (0, 0)))
    W["woa"] = woa.reshape(MLA_HEADS * HEAD_PAD, D_MODEL).astype(BF16)
    W["wod"] = w_o_dn.astype(BF16)
    W["wout"] = w_out.astype(BF16)
    W["dnorm"] = jnp.tile(dn_norm.astype(F32), DN_HEADS).reshape(1, DN_HEADS * DN_DV)
    W["ln1g"] = ln1_g.reshape(1, D_MODEL)
    W["ln1b"] = ln1_b.reshape(1, D_MODEL)
    W["ln2g"] = ln2_g.reshape(1, D_MODEL)
    W["ln2b"] = ln2_b.reshape(1, D_MODEL)
    W["wpg"] = ple_w_gate.astype(BF16)
    W["wpp"] = ple_w_proj.astype(BF16)
    rwt = router_w.T.astype(F32)
    W["rwh"], W["rwl"] = _split_bf16(rwt)
    W["rb"] = router_b.reshape(N_EXPERTS, 1).astype(F32)
    W["wgu"] = w_gate_up.astype(BF16)
    W["bgu"] = b_gate_up.reshape(N_EXPERTS, 1, 2 * D_FF).astype(F32)
    W["wd"] = w_down.astype(BF16)
    W["bd"] = b_down.reshape(N_EXPERTS, 1, D_MODEL).astype(F32)
    return W


def _rope_tables(S):
    half = ROPE_DIM // 2
    inv = ROPE_THETA ** (-jnp.arange(0, ROPE_DIM, 2, dtype=F32) / ROPE_DIM)
    ang = jnp.arange(S, dtype=F32)[:, None] * inv[None, :]
    cos, sin = jnp.cos(ang), jnp.sin(ang)
    c = (NOPE_DIM + ROPE_DIM) ** -0.5 * math.log2(math.e)
    pad = jnp.zeros((S, HEAD_PAD - NOPE_DIM - ROPE_DIM), F32)
    cos_blk = jnp.concatenate([cos, cos, pad], axis=1)
    sin_blk = jnp.concatenate([-sin, sin, pad], axis=1)
    cosq = jnp.concatenate([jnp.ones((S, NOPE_DIM), F32), cos_blk], axis=1) * c
    sinq = jnp.concatenate([jnp.zeros((S, NOPE_DIM), F32), sin_blk], axis=1) * c
    cosk = jnp.concatenate([jnp.zeros((S, NOPE_DIM), F32), cos_blk], axis=1)
    sink = jnp.concatenate([jnp.zeros((S, NOPE_DIM), F32), sin_blk], axis=1)
    return cosq, sinq, cosk, sink


def _route(top_i, top_g, T, bm):
    A = TOP_K * T
    e = top_i.reshape(A)
    onehot = (e[:, None] == jnp.arange(N_EXPERTS, dtype=jnp.int32)[None, :]).astype(jnp.int32)
    csum = jnp.cumsum(onehot, axis=0)
    rank = jnp.take_along_axis(csum, e[:, None], axis=1)[:, 0] - 1
    counts = csum[-1]
    padded = ((counts + bm - 1) // bm) * bm
    p_end = jnp.cumsum(padded)
    p_start = p_end - padded
    dest = p_start[e] + rank
    nblk = A // bm + N_EXPERTS
    P = nblk * bm
    tok = jnp.tile(jnp.arange(T, dtype=jnp.int32), TOP_K)
    slot_tok = jnp.zeros((P,), jnp.int32).at[dest].set(tok)
    slot_gate = jnp.zeros((P,), F32).at[dest].set(top_g.reshape(A))
    blk_e = jnp.minimum(jnp.searchsorted(p_end, jnp.arange(nblk, dtype=jnp.int32) * bm, side="right"),
                        N_EXPERTS - 1).astype(jnp.int32)
    nvalid = (p_end[-1] // bm).astype(jnp.int32).reshape(1)
    return dest, slot_tok, slot_gate.reshape(P, 1), blk_e, nvalid


def _layer(x, p, W, bm):
    B, S, _ = x.shape
    T = B * S
    x2 = x.reshape(T, D_MODEL)
    p2 = p.reshape(T, PLE_DIM)
    Wl = dict(W)
    Wl["cosq"], Wl["sinq"], Wl["cosk"], Wl["sink"] = _rope_tables(S)

    q, k, v, dn, z, gates, gb = _in_proj(x2, S, Wl)
    attn = _attention(q, k, v, B, S)
    qn, vk, aux = _dn_prep(dn, gb, S, Wl)
    kt = vk.reshape(T, DN_HEADS, 2, DN_DK)[:, :, 1, :].reshape(T, DN_HEADS * DN_DK).T
    auxt = jnp.swapaxes(aux, 1, 2)
    o2 = _deltanet(qn, vk, kt, aux, auxt, B, S, Wl)
    r, hb, top_i, top_g = _mix(x2, attn, o2, z, gates, p2, Wl)

    dest, slot_tok, slot_gate, blk_e, nvalid = _route(top_i, top_g, T, bm)
    xs = jnp.take(hb, slot_tok, axis=0)
    yb = _moe(xs, slot_gate, blk_e, nvalid, Wl, bm)
    ffn = jnp.take(yb, dest, axis=0).reshape(TOP_K, T, D_MODEL).astype(F32).sum(axis=0)
    y = _final(r, ffn, Wl)
    return y.reshape(B, S, D_MODEL)


def kernel(x_prompt, x_sample, p_prompt, p_sample, w_in, q_a_norm, w_uq, kv_a_norm, w_ukv, w_o_attn, dn_conv, dn_a_log, dn_dt_bias, dn_norm, w_o_dn, w_out, ln1_g, ln1_b, router_w, router_b, w_gate_up, b_gate_up, w_down, b_down, ple_w_proj, ple_w_gate, ln2_g, ln2_b):
    y_prompt, y_sample = x_prompt, x_sample
    for l in range(DEPTH):
        W = _prep_weights(w_in[l], q_a_norm[l], w_uq[l], kv_a_norm[l], w_ukv[l], w_o_attn[l], dn_conv[l],
                          dn_a_log[l], dn_dt_bias[l], dn_norm[l], w_o_dn[l], w_out[l], ln1_g[l], ln1_b[l],
                          router_w[l], router_b[l], w_gate_up[l], b_gate_up[l], w_down[l], b_down[l],
                          ple_w_proj[l], ple_w_gate[l], ln2_g[l], ln2_b[l])
        y_prompt = _layer(y_prompt, p_prompt[l], W, bm=256)
        y_sample = _layer(y_sample, p_sample[l], W, bm=256)
    return (y_prompt, y_sample)
```

```python
import functools
import math

import numpy as np
import jax
import jax.numpy as jnp
from jax import lax
from jax.experimental import pallas as pl
from jax.experimental.pallas import tpu as pltpu

D_MODEL = 1024
MLA_HEADS = 8
Q_LORA = 256
KV_LORA = 128
NOPE_DIM = 64
ROPE_DIM = 32
V_DIM = 64
ROPE_THETA = 10000.0
DN_HEADS = 8
DN_DK = 64
DN_DV = 64
CONV_K = 5
N_EXPERTS = 32
TOP_K = 4
D_FF = 1024
SWIGLU_LIMIT = 7.0
SWIGLU_ALPHA = 1.702
PLE_DIM = 256
DEPTH = 1
DEEPNORM_ALPHA = (2.0 * DEPTH) ** 0.25

LANES = 128
HEAD_PAD = 128
DN_BLOCK = 256
MIX_TILE = 256
VMEM_LIMIT = 56 * 1024 * 1024

_C_CQ = 0
_C_CKV = _C_CQ + Q_LORA
_C_KR = _C_CKV + KV_LORA
_C_KRS = _C_KR + LANES
_C_AB0 = _C_KRS + LANES
_C_AB1 = _C_AB0 + LANES
_C_DNQ = _C_AB1 + LANES
_C_DNVK = _C_DNQ + DN_HEADS * DN_DK
_C_Z = _C_DNVK + DN_HEADS * (DN_DK + DN_DV)
_C_GATE = _C_Z + DN_HEADS * DN_DV
_C_END = _C_GATE + 2 * D_MODEL

BF16 = jnp.bfloat16
F32 = jnp.float32


def _dot(a, b):
    return jnp.dot(a, b, preferred_element_type=F32)


def _dot_nt(a, b):
    return lax.dot_general(a, b, (((1,), (1,)), ((), ())), preferred_element_type=F32)


def _split_bf16(x):
    hi = x.astype(BF16)
    lo = (x - hi.astype(F32)).astype(BF16)
    return hi, lo


def _const_spec(shape):
    n = len(shape)
    return pl.BlockSpec(shape, lambda *_: (0,) * n)


def _params(sem):
    return pltpu.CompilerParams(dimension_semantics=sem, vmem_limit_bytes=VMEM_LIMIT)


def _in_proj_kernel(x_ref, w1_ref, qan_ref, kvan_ref, wq_ref, wqs_ref, wk_ref, wv_ref,
                    cosq_ref, sinq_ref, cosk_ref, sink_ref, abp_ref,
                    q_ref, k_ref, v_ref, dn_ref, z_ref, gates_ref, gb_ref):
    xb = x_ref[...].astype(BF16)

    def proj(lo, hi):
        return _dot(xb, w1_ref[:, lo:hi])

    def rms(c, g):
        return (c * lax.rsqrt(jnp.mean(c * c, axis=-1, keepdims=True) + 1e-6) * g).astype(BF16)

    cqn = rms(proj(_C_CQ, _C_CKV), qan_ref[...])
    qa = _dot(cqn, wq_ref[...])
    qb = _dot(cqn, wqs_ref[...])
    ckvn = rms(proj(_C_CKV, _C_KR), kvan_ref[...])
    kw = _dot(ckvn, wk_ref[...])
    vw = _dot(ckvn, wv_ref[...])
    kr = proj(_C_KR, _C_KRS) * cosk_ref[...] + proj(_C_KRS, _C_AB0) * sink_ref[...]
    cosq = cosq_ref[...]
    sinq = sinq_ref[...]
    lane = lax.broadcasted_iota(jnp.int32, (1, HEAD_PAD), 1)
    ones_col = (lane == V_DIM).astype(F32)
    for h in range(MLA_HEADS):
        sl = slice(h * HEAD_PAD, (h + 1) * HEAD_PAD)
        q_ref[:, sl] = (qa[:, sl] * cosq + qb[:, sl] * sinq).astype(BF16)
        k_ref[:, sl] = (kw[:, sl] + kr).astype(BF16)
        v_ref[:, sl] = (vw[:, sl] + ones_col).astype(BF16)

    for d, c0 in enumerate((_C_AB0, _C_AB1)):
        ab = proj(c0, c0 + LANES)
        neg_a = abp_ref[2 * d:2 * d + 1, :]
        dtb = abp_ref[2 * d + 1:2 * d + 2, :]
        t = ab + dtb
        sp = jnp.maximum(t, 0.0) + jnp.log(1.0 + jnp.exp(-jnp.abs(t)))
        g = neg_a * sp
        beta = 1.0 / (1.0 + jnp.exp(-ab))
        gb_ref[d] = jnp.where(lane < DN_HEADS, g, beta)[:, :2 * DN_HEADS]

    dn_ref[...] = proj(_C_DNQ, _C_Z).astype(BF16)
    z_ref[...] = proj(_C_Z, _C_GATE).astype(BF16)
    gl = proj(_C_GATE, _C_END)
    gates_ref[...] = (1.0 / (1.0 + jnp.exp(-gl))).astype(BF16)


def _in_proj(x2, S, W, tm=256):
    T = x2.shape[0]
    nseq = S // tm
    row = lambda i: (i, 0)
    pos = lambda i: (i % nseq, 0)
    dn_w = _C_Z - _C_DNQ
    return pl.pallas_call(
        _in_proj_kernel,
        grid=(T // tm,),
        in_specs=[
            pl.BlockSpec((tm, D_MODEL), row),
            _const_spec((D_MODEL, _C_END)),
            _const_spec((1, Q_LORA)), _const_spec((1, KV_LORA)),
            _const_spec((Q_LORA, MLA_HEADS * HEAD_PAD)), _const_spec((Q_LORA, MLA_HEADS * HEAD_PAD)),
            _const_spec((KV_LORA, MLA_HEADS * HEAD_PAD)), _const_spec((KV_LORA, MLA_HEADS * HEAD_PAD)),
            pl.BlockSpec((tm, HEAD_PAD), pos), pl.BlockSpec((tm, HEAD_PAD), pos),
            pl.BlockSpec((tm, HEAD_PAD), pos), pl.BlockSpec((tm, HEAD_PAD), pos),
            _const_spec((8, LANES)),
        ],
        out_specs=[
            pl.BlockSpec((tm, MLA_HEADS * HEAD_PAD), row),
            pl.BlockSpec((tm, MLA_HEADS * HEAD_PAD), row),
            pl.BlockSpec((tm, MLA_HEADS * HEAD_PAD), row),
            pl.BlockSpec((tm, dn_w), row),
            pl.BlockSpec((tm, DN_HEADS * DN_DV), row),
            pl.BlockSpec((tm, 2 * D_MODEL), row),
            pl.BlockSpec((2, tm, 2 * DN_HEADS), lambda i: (0, i, 0)),
        ],
        out_shape=[
            jax.ShapeDtypeStruct((T, MLA_HEADS * HEAD_PAD), BF16),
            jax.ShapeDtypeStruct((T, MLA_HEADS * HEAD_PAD), BF16),
            jax.ShapeDtypeStruct((T, MLA_HEADS * HEAD_PAD), BF16),
            jax.ShapeDtypeStruct((T, dn_w), BF16),
            jax.ShapeDtypeStruct((T, DN_HEADS * DN_DV), BF16),
            jax.ShapeDtypeStruct((T, 2 * D_MODEL), BF16),
            jax.ShapeDtypeStruct((2, T, 2 * DN_HEADS), F32),
        ],
        compiler_params=_params(("parallel",)),
        name="in_proj",
    )(x2, W["w1"], W["qan"], W["kvan"], W["wq"], W["wqs"], W["wk"], W["wv"],
      W["cosq"], W["sinq"], W["cosk"], W["sink"], W["abp"])


def _attn_kernel(q_ref, k_ref, v_ref, o_ref, *, tk):
    tq = q_ref.shape[0]
    S = k_ref.shape[0]
    q = q_ref[...]

    def body(j, carry):
        m, acc = carry
        off = pl.multiple_of(j * tk, tk)
        s = _dot_nt(q, k_ref[pl.ds(off, tk), :])
        m_new = jnp.maximum(m, jnp.max(s, axis=-1, keepdims=True))
        p = jnp.exp2(s - m_new).astype(BF16)
        acc = acc * jnp.exp2(m - m_new) + _dot(p, v_ref[pl.ds(off, tk), :])
        return m_new, acc

    m0 = jnp.full((tq, 1), -1e30, F32)
    acc0 = jnp.zeros((tq, HEAD_PAD), F32)
    _, acc = lax.fori_loop(0, S // tk, body, (m0, acc0), unroll=2)
    o_ref[...] = (acc / acc[:, V_DIM:V_DIM + 1]).astype(BF16)


def _attention(q, k, v, B, S, tq=512, tk=1024):
    T = q.shape[0]
    nq = S // tq
    tk = min(tk, S)
    return pl.pallas_call(
        functools.partial(_attn_kernel, tk=tk),
        grid=(B, MLA_HEADS, nq),
        in_specs=[
            pl.BlockSpec((tq, HEAD_PAD), lambda b, h, i: (b * nq + i, h)),
            pl.BlockSpec((S, HEAD_PAD), lambda b, h, i: (b, h)),
            pl.BlockSpec((S, HEAD_PAD), lambda b, h, i: (b, h)),
        ],
        out_specs=pl.BlockSpec((tq, HEAD_PAD), lambda b, h, i: (b * nq + i, h)),
        out_shape=jax.ShapeDtypeStruct((T, MLA_HEADS * HEAD_PAD), BF16),
        compiler_params=_params(("parallel", "parallel", "arbitrary")),
        name="attention",
    )(q, k, v)


_HALO = 16


def _dn_prep_kernel(x_ref, prev_ref, next_ref, cw_ref, eq_ref, eqt_ref, evk_ref, evkt_ref,
                    gb_ref, tri_ref, qn_ref, vk_ref, aux_ref, *, nseq):
    i = pl.program_id(0)
    tp = x_ref.shape[0]
    first = (i % nseq) == 0
    last = (i % nseq) == nseq - 1
    prev = jnp.where(first, 0.0, prev_ref[...].astype(F32))
    nxt = jnp.where(last, 0.0, next_ref[...].astype(F32))
    xe = jnp.concatenate([prev, x_ref[...].astype(F32), nxt], axis=0)
    n = tp + 2 * _HALO
    y = None
    for j in range(CONV_K):
        shift = (CONV_K // 2 - j) % n
        xs = xe if shift == 0 else pltpu.roll(xe, shift, axis=0)
        term = xs[_HALO:_HALO + tp, :] * cw_ref[j:j + 1, :]
        y = term if y is None else y + term
    y = y * (1.0 / (1.0 + jnp.exp(-y)))

    def group_scale(v, e_ref, et_ref):
        sq = v * v
        hi, lo = _split_bf16(sq)
        ss = _dot(hi, e_ref[...]) + _dot(lo, e_ref[...])
        inv = lax.rsqrt(ss + 1e-6)
        ih, il = _split_bf16(inv)
        return _dot(ih, et_ref[...]) + _dot(il, et_ref[...])

    nq = DN_HEADS * DN_DK
    yq = y[:, :nq]
    qn_ref[...] = (yq * group_scale(yq, eq_ref, eqt_ref) * (DN_DK ** -0.5)).astype(BF16)
    yvk = y[:, nq:]
    sc = group_scale(yvk, evk_ref, evkt_ref)
    lane = lax.broadcasted_iota(jnp.int32, (1, yvk.shape[1]), 1)
    is_k = (lane // DN_DV) % 2 == 1
    vk_ref[...] = (yvk * jnp.where(is_k, sc, 1.0)).astype(BF16)

    hcol = lax.broadcasted_iota(jnp.int32, (1, 2 * DN_HEADS), 1) < DN_HEADS
    for d in range(2):
        gb = gb_ref[d]
        for blk in range(tp // DN_BLOCK):
            rs = slice(blk * DN_BLOCK, (blk + 1) * DN_BLOCK)
            g = gb[rs]
            cs = jnp.dot(tri_ref[d], g, preferred_element_type=F32, precision=lax.Precision.HIGHEST)
            aux_ref[d, rs, :] = jnp.where(hcol, cs, g)


def _dn_prep(dn, gb, S, W, tp=256):
    T = dn.shape[0]
    nseq = S // tp
    hb = tp // _HALO
    nh = T // _HALO
    dn_w = dn.shape[1]
    nq = DN_HEADS * DN_DK
    return pl.pallas_call(
        functools.partial(_dn_prep_kernel, nseq=nseq),
        grid=(T // tp,),
        in_specs=[
            pl.BlockSpec((tp, dn_w), lambda i: (i, 0)),
            pl.BlockSpec((_HALO, dn_w), lambda i: (jnp.maximum(i * hb - 1, 0), 0)),
            pl.BlockSpec((_HALO, dn_w), lambda i: (jnp.minimum((i + 1) * hb, nh - 1), 0)),
            _const_spec((8, dn_w)),
            _const_spec((nq, LANES)), _const_spec((LANES, nq)),
            _const_spec((dn_w - nq, LANES)), _const_spec((LANES, dn_w - nq)),
            pl.BlockSpec((2, tp, 2 * DN_HEADS), lambda i: (0, i, 0)),
            _const_spec((2, DN_BLOCK, DN_BLOCK)),
        ],
        out_specs=[
            pl.BlockSpec((tp, nq), lambda i: (i, 0)),
            pl.BlockSpec((tp, dn_w - nq), lambda i: (i, 0)),
            pl.BlockSpec((2, tp, 2 * DN_HEADS), lambda i: (0, i, 0)),
        ],
        out_shape=[
            jax.ShapeDtypeStruct((T, nq), BF16),
            jax.ShapeDtypeStruct((T, dn_w - nq), BF16),
            jax.ShapeDtypeStruct((2, T, 2 * DN_HEADS), F32),
        ],
        compiler_params=_params(("parallel",)),
        name="dn_prep",
    )(dn, dn, dn, W["convw"], W["eq"], W["eqt"], W["evk"], W["evkt"], gb, W["tri"])


_M_INCL, _M_STRICT, _M_DIAG16 = 0, 1, 2
_MERGE_SIZES = (16, 32, 64, 128)
_N_MASKS = 3 + len(_MERGE_SIZES)


def _dn_masks():
    r = np.arange(DN_BLOCK)[:, None]
    c = np.arange(DN_BLOCK)[None, :]
    out = np.zeros((2, _N_MASKS, DN_BLOCK, DN_BLOCK), np.float32)
    for d in range(2):
        rr, cc = (r, c) if d == 0 else (c, r)
        out[d, _M_INCL] = rr >= cc
        out[d, _M_STRICT] = rr > cc
        out[d, _M_DIAG16] = (r // 16) == (c // 16)
        for k, s in enumerate(_MERGE_SIZES):
            out[d, 3 + k] = ((rr // s) % 2 == 1) & ((rr // s) == (cc // s) + 1)
    return out


def _deltanet_kernel(qn_ref, vk_ref, kt_ref, aux_ref, auxt_ref, mask_ref, o_ref,
                     s_ref, nm_ref, p_ref, x_ref, aqk_ref):
    d = pl.program_id(1)
    i = pl.program_id(2)
    C = DN_BLOCK
    heads = range(DN_HEADS)

    @pl.when(i == 0)
    def _():
        s_ref[...] = jnp.zeros_like(s_ref)

    rowi = lax.broadcasted_iota(jnp.int32, (C, C), 0)
    coli = lax.broadcasted_iota(jnp.int32, (C, C), 1)
    eye = (rowi == coli).astype(F32)
    lane = lax.broadcasted_iota(jnp.int32, (1, LANES), 1)
    r64 = lax.broadcasted_iota(jnp.int32, (DN_DK, LANES), 0)
    c64 = lax.broadcasted_iota(jnp.int32, (DN_DK, LANES), 1)
    zeros_kt = jnp.zeros((DN_DK, C), BF16)
    zeros_s = jnp.zeros((DN_DK, LANES), F32)

    def q_pair(h):
        return qn_ref[:, (h // 2) * LANES:(h // 2 + 1) * LANES]

    def vk_head(h):
        return vk_ref[:, h * LANES:(h + 1) * LANES]

    def kt_head(h):
        return kt_ref[h * DN_DK:(h + 1) * DN_DK, :]

    def gc_col(h):
        return aux_ref[0, :, h:h + 1]

    def beta_col(h):
        return aux_ref[0, :, DN_HEADS + h:DN_HEADS + h + 1]

    def gc_row(h):
        return auxt_ref[0, h:h + 1, :]

    for h in heads:
        kt = kt_head(h)
        kt_for_q = jnp.concatenate([kt, zeros_kt] if h % 2 == 0 else [zeros_kt, kt], axis=0)
        kt_for_k = jnp.concatenate([zeros_kt, kt], axis=0)
        qk = _dot(q_pair(h), kt_for_q)
        kk = _dot(vk_head(h), kt_for_k)
        e0 = jnp.exp(jnp.minimum(gc_col(h) - gc_row(h), 0.0))
        aqk_ref[h] = (qk * e0 * mask_ref[0, _M_INCL]).astype(BF16)
        nm_ref[h] = (-(kk * e0 * mask_ref[0, _M_STRICT]) * beta_col(h)).astype(BF16)

    diag16 = mask_ref[0, _M_DIAG16].astype(BF16)
    for h in heads:
        nd = nm_ref[h] * diag16
        p_ref[h] = (eye + nd.astype(F32)).astype(BF16)
        x_ref[h] = _dot(nd, nd).astype(BF16)
    for it in range(3):
        for h in heads:
            pb = p_ref[h]
            sqb = x_ref[h]
            p_ref[h] = (pb.astype(F32) + _dot(pb, sqb)).astype(BF16)
            if it < 2:
                x_ref[h] = _dot(sqb, sqb).astype(BF16)
    for k in range(len(_MERGE_SIZES)):
        off_mask = mask_ref[0, 3 + k].astype(BF16)
        for h in heads:
            x_ref[h] = _dot(nm_ref[h] * off_mask, p_ref[h]).astype(BF16)
        for h in heads:
            pb = p_ref[h]
            p_ref[h] = (pb.astype(F32) + _dot(pb, x_ref[h])).astype(BF16)

    o_pair = None
    for h in heads:
        par = h % 2
        gc_c = gc_col(h)
        gc_r = gc_row(h)
        g_tot = jnp.where(d == 0, gc_r[:, C - 1:C], gc_r[:, 0:1])
        egc = jnp.exp(gc_c)
        rhs = (vk_head(h).astype(F32) * beta_col(h) * jnp.where(lane < DN_DV, 1.0, egc)).astype(BF16)
        uw = _dot(p_ref[h], rhs).astype(BF16)
        s_pl = s_ref[h]
        eye_pl = (c64 == r64 + par * DN_DV).astype(F32)
        s_aug = jnp.concatenate([eye_pl, -s_pl], axis=0).astype(BF16)
        v_new = _dot(uw, s_aug).astype(BF16)
        s_sel = jnp.concatenate([s_pl, zeros_s] if par == 0 else [zeros_s, s_pl], axis=0).astype(BF16)
        qd = (q_pair(h).astype(F32) * egc).astype(BF16)
        o_pl = _dot(qd, s_sel) + _dot(aqk_ref[h], v_new)
        kd = (kt_head(h).astype(F32) * jnp.exp(g_tot - gc_r)).astype(BF16)
        s_ref[h] = s_pl * jnp.exp(g_tot) + _dot(kd, v_new)
        if par == 0:
            o_pair = o_pl
        else:
            o_ref[0, :, (h // 2) * LANES:(h // 2 + 1) * LANES] = o_pair + o_pl


def _deltanet(qn, vk, kt, aux, auxt, B, S, W):
    T = qn.shape[0]
    C = DN_BLOCK
    nb = S // C

    def blk(b, d, i):
        return b * nb + jnp.where(d == 0, i, nb - 1 - i)

    nq = DN_HEADS * DN_DK
    return pl.pallas_call(
        _deltanet_kernel,
        grid=(B, 2, nb),
        in_specs=[
            pl.BlockSpec((C, nq), lambda b, d, i: (blk(b, d, i), 0)),
            pl.BlockSpec((C, 2 * nq), lambda b, d, i: (blk(b, d, i), 0)),
            pl.BlockSpec((nq, C), lambda b, d, i: (0, blk(b, d, i))),
            pl.BlockSpec((1, C, 2 * DN_HEADS), lambda b, d, i: (d, blk(b, d, i), 0)),
            pl.BlockSpec((1, 2 * DN_HEADS, C), lambda b, d, i: (d, 0, blk(b, d, i))),
            pl.BlockSpec((1, _N_MASKS, C, C), lambda b, d, i: (d, 0, 0, 0)),
        ],
        out_specs=pl.BlockSpec((1, C, DN_HEADS * DN_DV), lambda b, d, i: (d, blk(b, d, i), 0)),
        out_shape=jax.ShapeDtypeStruct((2, T, DN_HEADS * DN_DV), F32),
        scratch_shapes=[pltpu.VMEM((DN_HEADS, DN_DK, LANES), F32)]
        + [pltpu.VMEM((DN_HEADS, C, C), BF16)] * 4,
        compiler_params=_params(("parallel", "arbitrary", "arbitrary")),
        name="deltanet",
    )(qn, vk, kt, aux, auxt, W["dn_masks"])


def _layer_norm(v, g, b):
    mu = jnp.mean(v, axis=-1, keepdims=True)
    c = v - mu
    var = jnp.mean(c * c, axis=-1, keepdims=True)
    return c * lax.rsqrt(var + 1e-5) * g + b


def _mix_kernel(x_ref, attn_ref, of_ref, ob_ref, z_ref, gates_ref, p_ref,
                woa_ref, wod_ref, wout_ref, e8_ref, e8t_ref, dnorm_ref, ln1g_ref, ln1b_ref,
                wpg_ref, wpp_ref, rwh_ref, rwl_ref, rb_ref, ustrict_ref,
                r_ref, hb_ref, ti_ref, tg_ref, rank_ref, cnt_ref, run_ref):
    @pl.when(pl.program_id(0) == 0)
    def _():
        run_ref[...] = jnp.zeros_like(run_ref)

    oa = _dot(attn_ref[...], woa_ref[...])
    o = of_ref[0] + ob_ref[0]
    hi, lo = _split_bf16(o * o)
    ms = (_dot(hi, e8_ref[...]) + _dot(lo, e8_ref[...])) * (1.0 / DN_DV)
    ih, il = _split_bf16(lax.rsqrt(ms + 1e-6))
    sc = _dot(ih, e8t_ref[...]) + _dot(il, e8t_ref[...])
    zf = z_ref[...].astype(F32)
    od_in = o * sc * dnorm_ref[...] * (zf * (1.0 / (1.0 + jnp.exp(-zf))))
    od = _dot(od_in.astype(BF16), wod_ref[...])
    mix = gates_ref[:, :D_MODEL].astype(F32) * oa + gates_ref[:, D_MODEL:].astype(F32) * od
    mo = _dot(mix.astype(BF16), wout_ref[...])
    h = _layer_norm(DEEPNORM_ALPHA * x_ref[...] + mo, ln1g_ref[...], ln1b_ref[...])
    hb = h.astype(BF16)
    hb_ref[...] = hb
    pg = _dot(hb, wpg_ref[...])
    pp = _dot(p_ref[...].astype(BF16), wpp_ref[...])
    r_ref[...] = DEEPNORM_ALPHA * h + pp * (1.0 / (1.0 + jnp.exp(-pg)))

    hl = (h - hb.astype(F32)).astype(BF16)
    logits = (_dot_nt(rwh_ref[...], hb) + _dot_nt(rwh_ref[...], hl)
              + _dot_nt(rwl_ref[...], hb) + rb_ref[...])
    eid = lax.broadcasted_iota(jnp.int32, logits.shape, 0)
    vals = []
    run = run_ref[:, 0:1]
    for k in range(TOP_K):
        m = jnp.max(logits, axis=0, keepdims=True)
        idx = jnp.min(jnp.where(logits == m, eid, N_EXPERTS), axis=0, keepdims=True)
        ti_ref[k:k + 1, :] = idx
        vals.append(m)
        hit = eid == idx
        logits = jnp.where(hit, -jnp.inf, logits)
        onehot = hit.astype(F32)
        earlier = _dot(onehot.astype(BF16), ustrict_ref[...])
        rank = jnp.sum(onehot * (run + earlier), axis=0, keepdims=True)
        rank_ref[k:k + 1, :] = rank.astype(jnp.int32)
        run = run + jnp.sum(onehot, axis=1, keepdims=True)
    run_ref[...] = jnp.broadcast_to(run, run_ref.shape)
    cnt_ref[...] = jnp.broadcast_to(run, cnt_ref.shape)
    es = [jnp.exp(v - vals[0]) for v in vals]
    den = es[0] + es[1] + es[2] + es[3]
    for k in range(TOP_K):
        tg_ref[k:k + 1, :] = es[k] / den


def _mix(x2, attn, o2, z, gates, p2, W):
    tm = MIX_TILE
    T = x2.shape[0]
    row = lambda i: (i, 0)
    nd = DN_HEADS * DN_DV
    return pl.pallas_call(
        _mix_kernel,
        grid=(T // tm,),
        in_specs=[
            pl.BlockSpec((tm, D_MODEL), row),
            pl.BlockSpec((tm, MLA_HEADS * HEAD_PAD), row),
            pl.BlockSpec((1, tm, nd), lambda i: (0, i, 0)),
            pl.BlockSpec((1, tm, nd), lambda i: (1, i, 0)),
            pl.BlockSpec((tm, nd), row),
            pl.BlockSpec((tm, 2 * D_MODEL), row),
            pl.BlockSpec((tm, PLE_DIM), row),
            _const_spec((MLA_HEADS * HEAD_PAD, D_MODEL)), _const_spec((nd, D_MODEL)),
            _const_spec((D_MODEL, D_MODEL)),
            _const_spec((nd, LANES)), _const_spec((LANES, nd)), _const_spec((1, nd)),
            _const_spec((1, D_MODEL)), _const_spec((1, D_MODEL)),
            _const_spec((D_MODEL, D_MODEL)), _const_spec((PLE_DIM, D_MODEL)),
            _const_spec((N_EXPERTS, D_MODEL)), _const_spec((N_EXPERTS, D_MODEL)),
            _const_spec((N_EXPERTS, 1)),
            _const_spec((tm, tm)),
        ],
        out_specs=[
            pl.BlockSpec((tm, D_MODEL), row),
            pl.BlockSpec((tm, D_MODEL), row),
            pl.BlockSpec((TOP_K, tm), lambda i: (0, i)),
            pl.BlockSpec((TOP_K, tm), lambda i: (0, i)),
            pl.BlockSpec((TOP_K, tm), lambda i: (0, i)),
            _const_spec((N_EXPERTS, LANES)),
        ],
        out_shape=[
            jax.ShapeDtypeStruct((T, D_MODEL), F32),
            jax.ShapeDtypeStruct((T, D_MODEL), BF16),
            jax.ShapeDtypeStruct((TOP_K, T), jnp.int32),
            jax.ShapeDtypeStruct((TOP_K, T), F32),
            jax.ShapeDtypeStruct((TOP_K, T), jnp.int32),
            jax.ShapeDtypeStruct((N_EXPERTS, LANES), F32),
        ],
        scratch_shapes=[pltpu.VMEM((N_EXPERTS, LANES), F32)],
        compiler_params=_params(("arbitrary",)),
        name="mix",
    )(x2, attn, o2, o2, z, gates, p2, W["woa"], W["wod"], W["wout"], W["e8"], W["e8t"], W["dnorm"],
      W["ln1g"], W["ln1b"], W["wpg"], W["wpp"], W["rwh"], W["rwl"], W["rb"], W["ustrict"])


def _moe_kernel(blk_e_ref, nvalid_ref, xs_ref, wgu_ref, bgu_ref, wd_ref, bd_ref, y_ref, *, fc):
    i = pl.program_id(0)

    @pl.when(i < nvalid_ref[0])
    def _():
        xs = xs_ref[...]
        acc = None
        for c in range(D_FF // fc):
            lo, hi = c * fc, (c + 1) * fc
            gate = _dot(xs, wgu_ref[0, :, lo:hi]) + bgu_ref[0, :, lo:hi]
            up = _dot(xs, wgu_ref[0, :, D_FF + lo:D_FF + hi]) + bgu_ref[0, :, D_FF + lo:D_FF + hi]
            gate = jnp.minimum(gate, SWIGLU_LIMIT)
            up = jnp.clip(up, -SWIGLU_LIMIT, SWIGLU_LIMIT)
            act = gate * (1.0 / (1.0 + jnp.exp(-SWIGLU_ALPHA * gate))) * (up + 1.0)
            part = _dot(act.astype(BF16), wd_ref[0, lo:hi, :])
            acc = part if acc is None else acc + part
        y_ref[...] = (acc + bd_ref[0]).astype(BF16)

    @pl.when(i >= nvalid_ref[0])
    def _():
        y_ref[...] = jnp.zeros_like(y_ref)


def _moe(xs, blk_e, nvalid, W, bm, fc=256):
    P = xs.shape[0]
    grid_spec = pltpu.PrefetchScalarGridSpec(
        num_scalar_prefetch=2,
        grid=(P // bm,),
        in_specs=[
            pl.BlockSpec((bm, D_MODEL), lambda i, be, nv: (i, 0)),
            pl.BlockSpec((1, D_MODEL, 2 * D_FF), lambda i, be, nv: (be[i], 0, 0)),
            pl.BlockSpec((1, 1, 2 * D_FF), lambda i, be, nv: (be[i], 0, 0)),
            pl.BlockSpec((1, D_FF, D_MODEL), lambda i, be, nv: (be[i], 0, 0)),
            pl.BlockSpec((1, 1, D_MODEL), lambda i, be, nv: (be[i], 0, 0)),
        ],
        out_specs=pl.BlockSpec((bm, D_MODEL), lambda i, be, nv: (i, 0)),
    )
    return pl.pallas_call(
        functools.partial(_moe_kernel, fc=fc),
        grid_spec=grid_spec,
        out_shape=jax.ShapeDtypeStruct((P, D_MODEL), BF16),
        compiler_params=_params(("arbitrary",)),
        name="moe",
    )(blk_e, nvalid, xs, W["wgu"], W["bgu"], W["wd"], W["bd"])


def _final_kernel(r_ref, yg_ref, tg_ref, g_ref, b_ref, y_ref):
    acc = r_ref[...]
    for k in range(TOP_K):
        acc = acc + yg_ref[k].astype(F32) * tg_ref[:, k:k + 1]
    y_ref[...] = _layer_norm(acc, g_ref[...], b_ref[...])


def _final(r, yg, tg, W, tm=512):
    T = r.shape[0]
    tm = min(tm, T)
    row = lambda i: (i, 0)
    return pl.pallas_call(
        _final_kernel,
        grid=(T // tm,),
        in_specs=[pl.BlockSpec((tm, D_MODEL), row),
                  pl.BlockSpec((TOP_K, tm, D_MODEL), lambda i: (0, i, 0)),
                  pl.BlockSpec((tm, TOP_K), row),
                  _const_spec((1, D_MODEL)), _const_spec((1, D_MODEL))],
        out_specs=pl.BlockSpec((tm, D_MODEL), row),
        out_shape=jax.ShapeDtypeStruct((T, D_MODEL), F32),
        compiler_params=_params(("parallel",)),
        name="final_ln",
    )(r, yg, tg, W["ln2g"], W["ln2b"])


def _pad_heads(w, n_heads, width, start, size, dst=0):
    K = w.shape[0]
    w3 = w.reshape(K, n_heads, width)[:, :, start:start + size]
    out = jnp.zeros((K, n_heads, HEAD_PAD), w.dtype)
    out = out.at[:, :, dst:dst + size].set(w3)
    return out.reshape(K, n_heads * HEAD_PAD)


def _prep_weights(w_in, q_a_norm, w_uq, kv_a_norm, w_ukv, w_o_attn, dn_conv, dn_a_log, dn_dt_bias,
                  dn_norm, w_o_dn, w_out, ln1_g, ln1_b, router_w, router_b, w_gate_up, b_gate_up,
                  w_down, b_down, ple_w_proj, ple_w_gate, ln2_g, ln2_b):
    W = {}
    half = ROPE_DIM // 2
    o = 0
    cq = w_in[:, o:o + Q_LORA]; o += Q_LORA
    ckv = w_in[:, o:o + KV_LORA]; o += KV_LORA
    kr = w_in[:, o:o + ROPE_DIM]; o += ROPE_DIM
    nqk = DN_HEADS * DN_DK
    dq = w_in[:, o:o + nqk]; o += nqk
    dk = w_in[:, o:o + nqk]; o += nqk
    dv = w_in[:, o:o + DN_HEADS * DN_DV]; o += DN_HEADS * DN_DV
    dz = w_in[:, o:o + DN_HEADS * DN_DV]; o += DN_HEADS * DN_DV
    da = w_in[:, o:o + 2 * DN_HEADS]; o += 2 * DN_HEADS
    db = w_in[:, o:o + 2 * DN_HEADS]; o += 2 * DN_HEADS
    gate = w_in[:, o:o + 2 * D_MODEL]

    def lane_block(parts):
        w = jnp.concatenate(parts, axis=1)
        return jnp.pad(w, ((0, 0), (0, LANES - w.shape[1])))

    zeros64 = jnp.zeros((D_MODEL, NOPE_DIM), F32)
    kr_blk = lane_block([zeros64, kr])
    krs_blk = lane_block([zeros64, kr[:, half:], kr[:, :half]])
    ab = [lane_block([da[:, d * DN_HEADS:(d + 1) * DN_HEADS], db[:, d * DN_HEADS:(d + 1) * DN_HEADS]])
          for d in range(2)]

    def interleave_vk(v, k):
        lead = v.shape[:-1]
        v3 = v.reshape(lead + (DN_HEADS, DN_DV))
        k3 = k.reshape(lead + (DN_HEADS, DN_DK))
        return jnp.concatenate([v3, k3], axis=-1).reshape(lead + (DN_HEADS * (DN_DV + DN_DK),))

    W["w1"] = jnp.concatenate([cq, ckv, kr_blk, krs_blk, ab[0], ab[1], dq, interleave_vk(dv, dk), dz, gate],
                              axis=1).astype(BF16)
    W["qan"] = q_a_norm.reshape(1, Q_LORA)
    W["kvan"] = kv_a_norm.reshape(1, KV_LORA)

    qw = NOPE_DIM + ROPE_DIM
    wq_nope = _pad_heads(w_uq, MLA_HEADS, qw, 0, NOPE_DIM, 0)
    wq_r1 = _pad_heads(w_uq, MLA_HEADS, qw, NOPE_DIM, half, NOPE_DIM)
    wq_r2 = _pad_heads(w_uq, MLA_HEADS, qw, NOPE_DIM + half, half, NOPE_DIM + half)
    W["wq"] = (wq_nope + wq_r1 + wq_r2).astype(BF16)
    wq_s1 = _pad_heads(w_uq, MLA_HEADS, qw, NOPE_DIM + half, half, NOPE_DIM)
    wq_s2 = _pad_heads(w_uq, MLA_HEADS, qw, NOPE_DIM, half, NOPE_DIM + half)
    W["wqs"] = (wq_s1 + wq_s2).astype(BF16)
    kvw = NOPE_DIM + V_DIM
    W["wk"] = _pad_heads(w_ukv, MLA_HEADS, kvw, 0, NOPE_DIM, 0).astype(BF16)
    W["wv"] = _pad_heads(w_ukv, MLA_HEADS, kvw, NOPE_DIM, V_DIM, 0).astype(BF16)

    neg_a = -jnp.exp(dn_a_log.astype(F32))
    abp = jnp.zeros((8, LANES), F32)
    for d in range(2):
        abp = abp.at[2 * d, :DN_HEADS].set(neg_a[d])
        abp = abp.at[2 * d + 1, :DN_HEADS].set(dn_dt_bias[d].astype(F32))
    W["abp"] = abp

    cw = jnp.concatenate([dn_conv[:, :nqk], interleave_vk(dn_conv[:, 2 * nqk:], dn_conv[:, nqk:2 * nqk])], axis=1)
    W["convw"] = jnp.pad(cw.astype(F32), ((0, 8 - CONV_K), (0, 0)))

    def group_indicator(width, group):
        e = (np.arange(width)[:, None] // group == np.arange(LANES)[None, :]).astype(np.float32)
        return e

    eq = group_indicator(nqk, DN_DK)
    W["eq"] = jnp.asarray(eq, BF16)
    W["eqt"] = jnp.asarray(eq.T, BF16)
    evk = group_indicator(2 * nqk, DN_DK)
    W["evk"] = jnp.asarray(evk, BF16)
    W["evkt"] = jnp.asarray(evk.T, BF16)
    W["e8"] = W["eq"]
    W["e8t"] = W["eqt"]
    r = np.arange(DN_BLOCK)
    W["tri"] = jnp.asarray(np.stack([r[:, None] >= r[None, :], r[:, None] <= r[None, :]]).astype(np.float32))
    W["dn_masks"] = jnp.asarray(_dn_masks())
    rt = np.arange(MIX_TILE)
    W["ustrict"] = jnp.asarray((rt[:, None] < rt[None, :]).astype(np.float32), BF16)

    woa = w_o_attn.reshape(MLA_HEADS, V_DIM, D_MODEL)
    woa = jnp.pad(woa, ((0, 0), (0, HEAD_PAD - V_DIM), (0, 0)))
    W["woa"] = woa.reshape(MLA_HEADS * HEAD_PAD, D_MODEL).astype(BF16)
    W["wod"] = w_o_dn.astype(BF16)
    W["wout"] = w_out.astype(BF16)
    W["dnorm"] = jnp.tile(dn_norm.astype(F32), DN_HEADS).reshape(1, DN_HEADS * DN_DV)
    W["ln1g"] = ln1_g.reshape(1, D_MODEL)
    W["ln1b"] = ln1_b.reshape(1, D_MODEL)
    W["ln2g"] = ln2_g.reshape(1, D_MODEL)
    W["ln2b"] = ln2_b.reshape(1, D_MODEL)
    W["wpg"] = ple_w_gate.astype(BF16)
    W["wpp"] = ple_w_proj.astype(BF16)
    rwt = router_w.T.astype(F32)
    W["rwh"], W["rwl"] = _split_bf16(rwt)
    W["rb"] = router_b.reshape(N_EXPERTS, 1).astype(F32)
    W["wgu"] = w_gate_up.astype(BF16)
    W["bgu"] = b_gate_up.reshape(N_EXPERTS, 1, 2 * D_FF).astype(F32)
    W["wd"] = w_down.astype(BF16)
    W["bd"] = b_down.reshape(N_EXPERTS, 1, D_MODEL).astype(F32)
    return W


def _rope_tables(S):
    half = ROPE_DIM // 2
    inv = ROPE_THETA ** (-jnp.arange(0, ROPE_DIM, 2, dtype=F32) / ROPE_DIM)
    ang = jnp.arange(S, dtype=F32)[:, None] * inv[None, :]
    cos, sin = jnp.cos(ang), jnp.sin(ang)
    c = (NOPE_DIM + ROPE_DIM) ** -0.5 * math.log2(math.e)
    pad = jnp.zeros((S, HEAD_PAD - NOPE_DIM - ROPE_DIM), F32)
    cos_blk = jnp.concatenate([cos, cos, pad], axis=1)
    sin_blk = jnp.concatenate([-sin, sin, pad], axis=1)
    cosq = jnp.concatenate([jnp.ones((S, NOPE_DIM), F32), cos_blk], axis=1) * c
    sinq = jnp.concatenate([jnp.zeros((S, NOPE_DIM), F32), sin_blk], axis=1) * c
    cosk = jnp.concatenate([jnp.zeros((S, NOPE_DIM), F32), cos_blk], axis=1)
    sink = jnp.concatenate([jnp.zeros((S, NOPE_DIM), F32), sin_blk], axis=1)
    return cosq, sinq, cosk, sink


def _route(top_i, rank, counts, T, bm):
    A = TOP_K * T
    counts = counts.astype(jnp.int32)
    padded = ((counts + bm - 1) // bm) * bm
    p_end = jnp.cumsum(padded)
    p_start = p_end - padded
    experts = jnp.arange(N_EXPERTS, dtype=jnp.int32)
    dest = rank + jnp.sum(jnp.where(top_i[..., None] == experts, p_start, 0), axis=-1)
    nblk = A // bm + N_EXPERTS
    P = nblk * bm
    tok = jnp.tile(jnp.arange(T, dtype=jnp.int32), TOP_K)
    slot_tok = jnp.zeros((P,), jnp.int32).at[dest.reshape(A)].set(tok)
    blk_e = jnp.minimum(jnp.searchsorted(p_end, jnp.arange(nblk, dtype=jnp.int32) * bm, side="right"),
                        N_EXPERTS - 1).astype(jnp.int32)
    nvalid = (p_end[-1] // bm).astype(jnp.int32).reshape(1)
    return dest, slot_tok, blk_e, nvalid


def _layer(x, p, W, bm):
    B, S, _ = x.shape
    T = B * S
    x2 = x.reshape(T, D_MODEL)
    p2 = p.reshape(T, PLE_DIM)
    Wl = dict(W)
    Wl["cosq"], Wl["sinq"], Wl["cosk"], Wl["sink"] = _rope_tables(S)

    q, k, v, dn, z, gates, gb = _in_proj(x2, S, Wl)
    attn = _attention(q, k, v, B, S)
    qn, vk, aux = _dn_prep(dn, gb, S, Wl)
    kt = vk.reshape(T, DN_HEADS, 2, DN_DK)[:, :, 1, :].reshape(T, DN_HEADS * DN_DK).T
    auxt = jnp.swapaxes(aux, 1, 2)
    o2 = _deltanet(qn, vk, kt, aux, auxt, B, S, Wl)
    r, hb, top_i, top_g, rank, cnt = _mix(x2, attn, o2, z, gates, p2, Wl)

    dest, slot_tok, blk_e, nvalid = _route(top_i, rank, cnt[:, 0], T, bm)
    xs = jnp.take(hb, slot_tok, axis=0)
    yb = _moe(xs, blk_e, nvalid, Wl, bm)
    yg = jnp.take(yb, dest.reshape(TOP_K * T), axis=0).reshape(TOP_K, T, D_MODEL)
    y = _final(r, yg, top_g.T, Wl)
    return y.reshape(B, S, D_MODEL)


def kernel(x_prompt, x_sample, p_prompt, p_sample, w_in, q_a_norm, w_uq, kv_a_norm, w_ukv, w_o_attn, dn_conv, dn_a_log, dn_dt_bias, dn_norm, w_o_dn, w_out, ln1_g, ln1_b, router_w, router_b, w_gate_up, b_gate_up, w_down, b_down, ple_w_proj, ple_w_gate, ln2_g, ln2_b):
    y_prompt, y_sample = x_prompt, x_sample
    for l in range(DEPTH):
        W = _prep_weights(w_in[l], q_a_norm[l], w_uq[l], kv_a_norm[l], w_ukv[l], w_o_attn[l], dn_conv[l],
                          dn_a_log[l], dn_dt_bias[l], dn_norm[l], w_o_dn[l], w_out[l], ln1_g[l], ln1_b[l],
                          router_w[l], router_b[l], w_gate_up[l], b_gate_up[l], w_down[l], b_down[l],
                          ple_w_proj[l], ple_w_gate[l], ln2_g[l], ln2_b[l])
        y_prompt = _layer(y_prompt, p_prompt[l], W, bm=256)
        y_sample = _layer(y_sample, p_sample[l], W, bm=256)
    return (y_prompt, y_sample)
```

```python
import functools
import math

import numpy as np
import jax
import jax.numpy as jnp
from jax import lax
from jax.experimental import pallas as pl
from jax.experimental.pallas import tpu as pltpu

D_MODEL = 1024
MLA_HEADS = 8
Q_LORA = 256
KV_LORA = 128
NOPE_DIM = 64
ROPE_DIM = 32
V_DIM = 64
ROPE_THETA = 10000.0
DN_HEADS = 8
DN_DK = 64
DN_DV = 64
CONV_K = 5
N_EXPERTS = 32
TOP_K = 4
D_FF = 1024
SWIGLU_LIMIT = 7.0
SWIGLU_ALPHA = 1.702
PLE_DIM = 256
DEPTH = 1
DEEPNORM_ALPHA = (2.0 * DEPTH) ** 0.25

LANES = 128
HEAD_PAD = 128
DN_BLOCK = 256
MOE_BLOCK = 512
MIX_TILE = 512
VMEM_LIMIT = 56 * 1024 * 1024

_C_CQ = 0
_C_CKV = _C_CQ + Q_LORA
_C_KR = _C_CKV + KV_LORA
_C_KRS = _C_KR + LANES
_C_AB0 = _C_KRS + LANES
_C_AB1 = _C_AB0 + LANES
_C_DNQ = _C_AB1 + LANES
_C_DNVK = _C_DNQ + DN_HEADS * DN_DK
_C_Z = _C_DNVK + DN_HEADS * (DN_DK + DN_DV)
_C_GATE = _C_Z + DN_HEADS * DN_DV
_C_END = _C_GATE + 2 * D_MODEL

BF16 = jnp.bfloat16
F32 = jnp.float32


def _dot(a, b):
    return jnp.dot(a, b, preferred_element_type=F32)


def _dot_nt(a, b):
    return lax.dot_general(a, b, (((1,), (1,)), ((), ())), preferred_element_type=F32)


def _split_bf16(x):
    hi = x.astype(BF16)
    lo = (x - hi.astype(F32)).astype(BF16)
    return hi, lo


def _const_spec(shape):
    n = len(shape)
    return pl.BlockSpec(shape, lambda *_: (0,) * n, pipeline_mode=pl.Buffered(1))


def _params(sem):
    return pltpu.CompilerParams(dimension_semantics=sem, vmem_limit_bytes=VMEM_LIMIT)


def _in_proj_kernel(x_ref, w1_ref, qan_ref, kvan_ref, wq_ref, wqs_ref, wk_ref, wv_ref,
                    cosq_ref, sinq_ref, cosk_ref, sink_ref, abp_ref,
                    q_ref, k_ref, v_ref, dn_ref, z_ref, gates_ref, gb_ref):
    xb = x_ref[...].astype(BF16)

    def proj(lo, hi):
        return _dot(xb, w1_ref[:, lo:hi])

    def rms(c, g):
        return (c * lax.rsqrt(jnp.mean(c * c, axis=-1, keepdims=True) + 1e-6) * g).astype(BF16)

    cqn = rms(proj(_C_CQ, _C_CKV), qan_ref[...])
    qa = _dot(cqn, wq_ref[...])
    qb = _dot(cqn, wqs_ref[...])
    ckvn = rms(proj(_C_CKV, _C_KR), kvan_ref[...])
    kw = _dot(ckvn, wk_ref[...])
    vw = _dot(ckvn, wv_ref[...])
    kr = proj(_C_KR, _C_KRS) * cosk_ref[...] + proj(_C_KRS, _C_AB0) * sink_ref[...]
    cosq = cosq_ref[...]
    sinq = sinq_ref[...]
    lane = lax.broadcasted_iota(jnp.int32, (1, HEAD_PAD), 1)
    ones_col = (lane == V_DIM).astype(F32)
    for h in range(MLA_HEADS):
        sl = slice(h * HEAD_PAD, (h + 1) * HEAD_PAD)
        q_ref[:, sl] = (qa[:, sl] * cosq + qb[:, sl] * sinq).astype(BF16)
        k_ref[:, sl] = (kw[:, sl] + kr).astype(BF16)
        v_ref[:, sl] = (vw[:, sl] + ones_col).astype(BF16)

    for d, c0 in enumerate((_C_AB0, _C_AB1)):
        ab = proj(c0, c0 + LANES)
        neg_a = abp_ref[2 * d:2 * d + 1, :]
        dtb = abp_ref[2 * d + 1:2 * d + 2, :]
        t = ab + dtb
        sp = jnp.maximum(t, 0.0) + jnp.log(1.0 + jnp.exp(-jnp.abs(t)))
        g = neg_a * sp
        beta = 1.0 / (1.0 + jnp.exp(-ab))
        gb_ref[d] = jnp.where(lane < DN_HEADS, g, beta)[:, :2 * DN_HEADS]

    dn_ref[...] = proj(_C_DNQ, _C_Z).astype(BF16)
    z_ref[...] = proj(_C_Z, _C_GATE).astype(BF16)
    gl = proj(_C_GATE, _C_END)
    gates_ref[...] = (1.0 / (1.0 + jnp.exp(-gl))).astype(BF16)


def _in_proj(x2, S, W, tm=512):
    T = x2.shape[0]
    nseq = S // tm
    row = lambda i: (i, 0)
    pos = lambda i: (i % nseq, 0)
    dn_w = _C_Z - _C_DNQ
    return pl.pallas_call(
        _in_proj_kernel,
        grid=(T // tm,),
        in_specs=[
            pl.BlockSpec((tm, D_MODEL), row),
            _const_spec((D_MODEL, _C_END)),
            _const_spec((1, Q_LORA)), _const_spec((1, KV_LORA)),
            _const_spec((Q_LORA, MLA_HEADS * HEAD_PAD)), _const_spec((Q_LORA, MLA_HEADS * HEAD_PAD)),
            _const_spec((KV_LORA, MLA_HEADS * HEAD_PAD)), _const_spec((KV_LORA, MLA_HEADS * HEAD_PAD)),
            pl.BlockSpec((tm, HEAD_PAD), pos), pl.BlockSpec((tm, HEAD_PAD), pos),
            pl.BlockSpec((tm, HEAD_PAD), pos), pl.BlockSpec((tm, HEAD_PAD), pos),
            _const_spec((8, LANES)),
        ],
        out_specs=[
            pl.BlockSpec((tm, MLA_HEADS * HEAD_PAD), row),
            pl.BlockSpec((tm, MLA_HEADS * HEAD_PAD), row),
            pl.BlockSpec((tm, MLA_HEADS * HEAD_PAD), row),
            pl.BlockSpec((tm, dn_w), row),
            pl.BlockSpec((tm, DN_HEADS * DN_DV), row),
            pl.BlockSpec((tm, 2 * D_MODEL), row),
            pl.BlockSpec((2, tm, 2 * DN_HEADS), lambda i: (0, i, 0)),
        ],
        out_shape=[
            jax.ShapeDtypeStruct((T, MLA_HEADS * HEAD_PAD), BF16),
            jax.ShapeDtypeStruct((T, MLA_HEADS * HEAD_PAD), BF16),
            jax.ShapeDtypeStruct((T, MLA_HEADS * HEAD_PAD), BF16),
            jax.ShapeDtypeStruct((T, dn_w), BF16),
            jax.ShapeDtypeStruct((T, DN_HEADS * DN_DV), BF16),
            jax.ShapeDtypeStruct((T, 2 * D_MODEL), BF16),
            jax.ShapeDtypeStruct((2, T, 2 * DN_HEADS), F32),
        ],
        compiler_params=_params(("parallel",)),
        name="in_proj",
    )(x2, W["w1"], W["qan"], W["kvan"], W["wq"], W["wqs"], W["wk"], W["wv"],
      W["cosq"], W["sinq"], W["cosk"], W["sink"], W["abp"])


def _attn_kernel(q_ref, k_ref, v_ref, o_ref, *, tk):
    tq = q_ref.shape[0]
    S = k_ref.shape[0]
    q = q_ref[...]

    def body(j, carry):
        m, acc = carry
        off = pl.multiple_of(j * tk, tk)
        s = _dot_nt(q, k_ref[pl.ds(off, tk), :])
        m_new = jnp.maximum(m, jnp.max(s, axis=-1, keepdims=True))
        p = jnp.exp2(s - m_new).astype(BF16)
        acc = acc * jnp.exp2(m - m_new) + _dot(p, v_ref[pl.ds(off, tk), :])
        return m_new, acc

    m0 = jnp.full((tq, 1), -1e30, F32)
    acc0 = jnp.zeros((tq, HEAD_PAD), F32)
    _, acc = lax.fori_loop(0, S // tk, body, (m0, acc0), unroll=4)
    o_ref[...] = (acc / acc[:, V_DIM:V_DIM + 1]).astype(BF16)


def _attention(q, k, v, B, S, tq=512, tk=1024):
    T = q.shape[0]
    nq = S // tq
    tk = min(tk, S)
    return pl.pallas_call(
        functools.partial(_attn_kernel, tk=tk),
        grid=(B, MLA_HEADS, nq),
        in_specs=[
            pl.BlockSpec((tq, HEAD_PAD), lambda b, h, i: (b * nq + i, h)),
            pl.BlockSpec((S, HEAD_PAD), lambda b, h, i: (b, h)),
            pl.BlockSpec((S, HEAD_PAD), lambda b, h, i: (b, h)),
        ],
        out_specs=pl.BlockSpec((tq, HEAD_PAD), lambda b, h, i: (b * nq + i, h)),
        out_shape=jax.ShapeDtypeStruct((T, MLA_HEADS * HEAD_PAD), BF16),
        compiler_params=_params(("parallel", "parallel", "arbitrary")),
        name="attention",
    )(q, k, v)


_HALO = 16


def _dn_prep_kernel(x_ref, prev_ref, next_ref, cw_ref, eq_ref, eqt_ref, evk_ref, evkt_ref,
                    gb_ref, tri_ref, qn_ref, vk_ref, aux_ref, *, nseq):
    i = pl.program_id(0)
    tp = x_ref.shape[0]
    first = (i % nseq) == 0
    last = (i % nseq) == nseq - 1
    prev = jnp.where(first, 0.0, prev_ref[...].astype(F32))
    nxt = jnp.where(last, 0.0, next_ref[...].astype(F32))
    xe = jnp.concatenate([prev, x_ref[...].astype(F32), nxt], axis=0)
    n = tp + 2 * _HALO
    y = None
    for j in range(CONV_K):
        shift = (CONV_K // 2 - j) % n
        xs = xe if shift == 0 else pltpu.roll(xe, shift, axis=0)
        term = xs[_HALO:_HALO + tp, :] * cw_ref[j:j + 1, :]
        y = term if y is None else y + term
    y = y * (1.0 / (1.0 + jnp.exp(-y)))

    def group_scale(v, e_ref, et_ref):
        sq = v * v
        hi, lo = _split_bf16(sq)
        ss = _dot(hi, e_ref[...]) + _dot(lo, e_ref[...])
        inv = lax.rsqrt(ss + 1e-6)
        ih, il = _split_bf16(inv)
        return _dot(ih, et_ref[...]) + _dot(il, et_ref[...])

    nq = DN_HEADS * DN_DK
    yq = y[:, :nq]
    qn_ref[...] = (yq * group_scale(yq, eq_ref, eqt_ref) * (DN_DK ** -0.5)).astype(BF16)
    yvk = y[:, nq:]
    sc = group_scale(yvk, evk_ref, evkt_ref)
    lane = lax.broadcasted_iota(jnp.int32, (1, yvk.shape[1]), 1)
    is_k = (lane // DN_DV) % 2 == 1
    vk_ref[...] = (yvk * jnp.where(is_k, sc, 1.0)).astype(BF16)

    hcol = lax.broadcasted_iota(jnp.int32, (1, 2 * DN_HEADS), 1) < DN_HEADS
    for d in range(2):
        gb = gb_ref[d]
        for blk in range(tp // DN_BLOCK):
            rs = slice(blk * DN_BLOCK, (blk + 1) * DN_BLOCK)
            g = gb[rs]
            cs = jnp.dot(tri_ref[d], g, preferred_element_type=F32, precision=lax.Precision.HIGHEST)
            aux_ref[d, rs, :] = jnp.where(hcol, cs, g)


def _dn_prep(dn, gb, S, W, tp=256):
    T = dn.shape[0]
    nseq = S // tp
    hb = tp // _HALO
    nh = T // _HALO
    dn_w = dn.shape[1]
    nq = DN_HEADS * DN_DK
    return pl.pallas_call(
        functools.partial(_dn_prep_kernel, nseq=nseq),
        grid=(T // tp,),
        in_specs=[
            pl.BlockSpec((tp, dn_w), lambda i: (i, 0)),
            pl.BlockSpec((_HALO, dn_w), lambda i: (jnp.maximum(i * hb - 1, 0), 0)),
            pl.BlockSpec((_HALO, dn_w), lambda i: (jnp.minimum((i + 1) * hb, nh - 1), 0)),
            _const_spec((8, dn_w)),
            _const_spec((nq, LANES)), _const_spec((LANES, nq)),
            _const_spec((dn_w - nq, LANES)), _const_spec((LANES, dn_w - nq)),
            pl.BlockSpec((2, tp, 2 * DN_HEADS), lambda i: (0, i, 0)),
            _const_spec((2, DN_BLOCK, DN_BLOCK)),
        ],
        out_specs=[
            pl.BlockSpec((tp, nq), lambda i: (i, 0)),
            pl.BlockSpec((tp, dn_w - nq), lambda i: (i, 0)),
            pl.BlockSpec((2, tp, 2 * DN_HEADS), lambda i: (0, i, 0)),
        ],
        out_shape=[
            jax.ShapeDtypeStruct((T, nq), BF16),
            jax.ShapeDtypeStruct((T, dn_w - nq), BF16),
            jax.ShapeDtypeStruct((2, T, 2 * DN_HEADS), F32),
        ],
        compiler_params=_params(("parallel",)),
        name="dn_prep",
    )(dn, dn, dn, W["convw"], W["eq"], W["eqt"], W["evk"], W["evkt"], gb, W["tri"])


_M_INCL, _M_STRICT, _M_DIAG16 = 0, 1, 2
_MERGE_SIZES = (16, 32, 64, 128)
_N_MASKS = 3 + len(_MERGE_SIZES)


def _dn_masks():
    r = np.arange(DN_BLOCK)[:, None]
    c = np.arange(DN_BLOCK)[None, :]
    out = np.zeros((2, _N_MASKS, DN_BLOCK, DN_BLOCK), np.float32)
    for d in range(2):
        rr, cc = (r, c) if d == 0 else (c, r)
        out[d, _M_INCL] = rr >= cc
        out[d, _M_STRICT] = rr > cc
        out[d, _M_DIAG16] = (r // 16) == (c // 16)
        for k, s in enumerate(_MERGE_SIZES):
            out[d, 3 + k] = ((rr // s) % 2 == 1) & ((rr // s) == (cc // s) + 1)
    return out


def _deltanet_kernel(qn_ref, vk_ref, kt_ref, aux_ref, auxt_ref, mask_ref, o_ref,
                     s_ref, nm_ref, p_ref, x_ref, aqk_ref):
    d = pl.program_id(1)
    i = pl.program_id(2)
    C = DN_BLOCK
    heads = range(DN_HEADS)

    @pl.when(i == 0)
    def _():
        s_ref[...] = jnp.zeros_like(s_ref)

    rowi = lax.broadcasted_iota(jnp.int32, (C, C), 0)
    coli = lax.broadcasted_iota(jnp.int32, (C, C), 1)
    eye = (rowi == coli).astype(F32)
    lane = lax.broadcasted_iota(jnp.int32, (1, LANES), 1)
    r64 = lax.broadcasted_iota(jnp.int32, (DN_DK, LANES), 0)
    c64 = lax.broadcasted_iota(jnp.int32, (DN_DK, LANES), 1)
    zeros_kt = jnp.zeros((DN_DK, C), BF16)
    zeros_s = jnp.zeros((DN_DK, LANES), F32)

    def q_pair(h):
        return qn_ref[:, (h // 2) * LANES:(h // 2 + 1) * LANES]

    def vk_head(h):
        return vk_ref[:, h * LANES:(h + 1) * LANES]

    def kt_head(h):
        return kt_ref[h * DN_DK:(h + 1) * DN_DK, :]

    def gc_col(h):
        return aux_ref[0, :, h:h + 1]

    def beta_col(h):
        return aux_ref[0, :, DN_HEADS + h:DN_HEADS + h + 1]

    def gc_row(h):
        return auxt_ref[0, h:h + 1, :]

    for h in heads:
        kt = kt_head(h)
        kt_for_q = jnp.concatenate([kt, zeros_kt] if h % 2 == 0 else [zeros_kt, kt], axis=0)
        kt_for_k = jnp.concatenate([zeros_kt, kt], axis=0)
        qk = _dot(q_pair(h), kt_for_q)
        kk = _dot(vk_head(h), kt_for_k)
        e0 = jnp.exp(jnp.minimum(gc_col(h) - gc_row(h), 0.0))
        aqk_ref[h] = (qk * e0 * mask_ref[0, _M_INCL]).astype(BF16)
        nm_ref[h] = (-(kk * e0 * mask_ref[0, _M_STRICT]) * beta_col(h)).astype(BF16)

    diag16 = mask_ref[0, _M_DIAG16].astype(BF16)
    for h in heads:
        nd = nm_ref[h] * diag16
        p_ref[h] = (eye + nd.astype(F32)).astype(BF16)
        x_ref[h] = _dot(nd, nd).astype(BF16)
    for it in range(3):
        for h in heads:
            pb = p_ref[h]
            sqb = x_ref[h]
            p_ref[h] = (pb.astype(F32) + _dot(pb, sqb)).astype(BF16)
            if it < 2:
                x_ref[h] = _dot(sqb, sqb).astype(BF16)
    for k in range(len(_MERGE_SIZES)):
        off_mask = mask_ref[0, 3 + k].astype(BF16)
        for h in heads:
            x_ref[h] = _dot(nm_ref[h] * off_mask, p_ref[h]).astype(BF16)
        for h in heads:
            pb = p_ref[h]
            p_ref[h] = (pb.astype(F32) + _dot(pb, x_ref[h])).astype(BF16)

    o_pair = None
    for h in heads:
        par = h % 2
        gc_c = gc_col(h)
        gc_r = gc_row(h)
        g_tot = jnp.where(d == 0, gc_r[:, C - 1:C], gc_r[:, 0:1])
        egc = jnp.exp(gc_c)
        rhs = (vk_head(h).astype(F32) * beta_col(h) * jnp.where(lane < DN_DV, 1.0, egc)).astype(BF16)
        uw = _dot(p_ref[h], rhs).astype(BF16)
        s_pl = s_ref[h]
        eye_pl = (c64 == r64 + par * DN_DV).astype(F32)
        s_aug = jnp.concatenate([eye_pl, -s_pl], axis=0).astype(BF16)
        v_new = _dot(uw, s_aug).astype(BF16)
        s_sel = jnp.concatenate([s_pl, zeros_s] if par == 0 else [zeros_s, s_pl], axis=0).astype(BF16)
        qd = (q_pair(h).astype(F32) * egc).astype(BF16)
        o_pl = _dot(qd, s_sel) + _dot(aqk_ref[h], v_new)
        kd = (kt_head(h).astype(F32) * jnp.exp(g_tot - gc_r)).astype(BF16)
        s_ref[h] = s_pl * jnp.exp(g_tot) + _dot(kd, v_new)
        if par == 0:
            o_pair = o_pl
        else:
            o_ref[0, :, (h // 2) * LANES:(h // 2 + 1) * LANES] = o_pair + o_pl


def _deltanet(qn, vk, kt, aux, auxt, B, S, W):
    T = qn.shape[0]
    C = DN_BLOCK
    nb = S // C

    def blk(b, d, i):
        return b * nb + jnp.where(d == 0, i, nb - 1 - i)

    nq = DN_HEADS * DN_DK
    return pl.pallas_call(
        _deltanet_kernel,
        grid=(B, 2, nb),
        in_specs=[
            pl.BlockSpec((C, nq), lambda b, d, i: (blk(b, d, i), 0)),
            pl.BlockSpec((C, 2 * nq), lambda b, d, i: (blk(b, d, i), 0)),
            pl.BlockSpec((nq, C), lambda b, d, i: (0, blk(b, d, i))),
            pl.BlockSpec((1, C, 2 * DN_HEADS), lambda b, d, i: (d, blk(b, d, i), 0)),
            pl.BlockSpec((1, 2 * DN_HEADS, C), lambda b, d, i: (d, 0, blk(b, d, i))),
            pl.BlockSpec((1, _N_MASKS, C, C), lambda b, d, i: (d, 0, 0, 0)),
        ],
        out_specs=pl.BlockSpec((1, C, DN_HEADS * DN_DV), lambda b, d, i: (d, blk(b, d, i), 0)),
        out_shape=jax.ShapeDtypeStruct((2, T, DN_HEADS * DN_DV), F32),
        scratch_shapes=[pltpu.VMEM((DN_HEADS, DN_DK, LANES), F32)]
        + [pltpu.VMEM((DN_HEADS, C, C), BF16)] * 4,
        compiler_params=_params(("parallel", "arbitrary", "arbitrary")),
        name="deltanet",
    )(qn, vk, kt, aux, auxt, W["dn_masks"])


def _layer_norm(v, g, b):
    mu = jnp.mean(v, axis=-1, keepdims=True)
    c = v - mu
    var = jnp.mean(c * c, axis=-1, keepdims=True)
    return c * lax.rsqrt(var + 1e-5) * g + b


def _mix_kernel(x_ref, attn_ref, of_ref, ob_ref, z_ref, gates_ref, p_ref,
                woa_ref, wod_ref, wout_ref, e8_ref, e8t_ref, dnorm_ref, ln1g_ref, ln1b_ref,
                wpg_ref, wpp_ref, rwh_ref, rwl_ref, rb_ref, ustrict_ref,
                r_ref, hb_ref, ti_ref, tg_ref, rank_ref, cnt_ref, run_ref):
    @pl.when(pl.program_id(0) == 0)
    def _():
        run_ref[...] = jnp.zeros_like(run_ref)

    oa = _dot(attn_ref[...], woa_ref[...])
    o = of_ref[0] + ob_ref[0]
    hi, lo = _split_bf16(o * o)
    ms = (_dot(hi, e8_ref[...]) + _dot(lo, e8_ref[...])) * (1.0 / DN_DV)
    ih, il = _split_bf16(lax.rsqrt(ms + 1e-6))
    sc = _dot(ih, e8t_ref[...]) + _dot(il, e8t_ref[...])
    zf = z_ref[...].astype(F32)
    od_in = o * sc * dnorm_ref[...] * (zf * (1.0 / (1.0 + jnp.exp(-zf))))
    od = _dot(od_in.astype(BF16), wod_ref[...])
    mix = gates_ref[:, :D_MODEL].astype(F32) * oa + gates_ref[:, D_MODEL:].astype(F32) * od
    mo = _dot(mix.astype(BF16), wout_ref[...])
    h = _layer_norm(DEEPNORM_ALPHA * x_ref[...] + mo, ln1g_ref[...], ln1b_ref[...])
    hb = h.astype(BF16)
    hb_ref[...] = hb
    pg = _dot(hb, wpg_ref[...])
    pp = _dot(p_ref[...].astype(BF16), wpp_ref[...])
    r_ref[...] = DEEPNORM_ALPHA * h + pp * (1.0 / (1.0 + jnp.exp(-pg)))

    hl = (h - hb.astype(F32)).astype(BF16)
    logits = (_dot_nt(rwh_ref[...], hb) + _dot_nt(rwh_ref[...], hl)
              + _dot_nt(rwl_ref[...], hb) + rb_ref[...])
    eid = lax.broadcasted_iota(jnp.int32, logits.shape, 0)
    vals = []
    run = run_ref[:, 0:1]
    for k in range(TOP_K):
        m = jnp.max(logits, axis=0, keepdims=True)
        idx = jnp.min(jnp.where(logits == m, eid, N_EXPERTS), axis=0, keepdims=True)
        ti_ref[k:k + 1, :] = idx
        vals.append(m)
        hit = eid == idx
        logits = jnp.where(hit, -jnp.inf, logits)
        onehot = hit.astype(F32)
        earlier = _dot(onehot.astype(BF16), ustrict_ref[...])
        rank = jnp.sum(onehot * (run + earlier), axis=0, keepdims=True)
        rank_ref[k:k + 1, :] = rank.astype(jnp.int32)
        run = run + jnp.sum(onehot, axis=1, keepdims=True)
    run_ref[...] = jnp.broadcast_to(run, run_ref.shape)
    cnt_ref[...] = jnp.broadcast_to(run, cnt_ref.shape)
    es = [jnp.exp(v - vals[0]) for v in vals]
    den = es[0] + es[1] + es[2] + es[3]
    for k in range(TOP_K):
        tg_ref[k:k + 1, :] = es[k] / den


def _mix(x2, attn, o2, z, gates, p2, W):
    tm = MIX_TILE
    T = x2.shape[0]
    row = lambda i: (i, 0)
    nd = DN_HEADS * DN_DV
    return pl.pallas_call(
        _mix_kernel,
        grid=(T // tm,),
        in_specs=[
            pl.BlockSpec((tm, D_MODEL), row),
            pl.BlockSpec((tm, MLA_HEADS * HEAD_PAD), row),
            pl.BlockSpec((1, tm, nd), lambda i: (0, i, 0)),
            pl.BlockSpec((1, tm, nd), lambda i: (1, i, 0)),
            pl.BlockSpec((tm, nd), row),
            pl.BlockSpec((tm, 2 * D_MODEL), row),
            pl.BlockSpec((tm, PLE_DIM), row),
            _const_spec((MLA_HEADS * HEAD_PAD, D_MODEL)), _const_spec((nd, D_MODEL)),
            _const_spec((D_MODEL, D_MODEL)),
            _const_spec((nd, LANES)), _const_spec((LANES, nd)), _const_spec((1, nd)),
            _const_spec((1, D_MODEL)), _const_spec((1, D_MODEL)),
            _const_spec((D_MODEL, D_MODEL)), _const_spec((PLE_DIM, D_MODEL)),
            _const_spec((N_EXPERTS, D_MODEL)), _const_spec((N_EXPERTS, D_MODEL)),
            _const_spec((N_EXPERTS, 1)),
            _const_spec((tm, tm)),
        ],
        out_specs=[
            pl.BlockSpec((tm, D_MODEL), row),
            pl.BlockSpec((tm, D_MODEL), row),
            pl.BlockSpec((TOP_K, tm), lambda i: (0, i)),
            pl.BlockSpec((TOP_K, tm), lambda i: (0, i)),
            pl.BlockSpec((TOP_K, tm), lambda i: (0, i)),
            _const_spec((N_EXPERTS, LANES)),
        ],
        out_shape=[
            jax.ShapeDtypeStruct((T, D_MODEL), F32),
            jax.ShapeDtypeStruct((T, D_MODEL), BF16),
            jax.ShapeDtypeStruct((TOP_K, T), jnp.int32),
            jax.ShapeDtypeStruct((TOP_K, T), F32),
            jax.ShapeDtypeStruct((TOP_K, T), jnp.int32),
            jax.ShapeDtypeStruct((N_EXPERTS, LANES), F32),
        ],
        scratch_shapes=[pltpu.VMEM((N_EXPERTS, LANES), F32)],
        compiler_params=_params(("arbitrary",)),
        name="mix",
    )(x2, attn, o2, o2, z, gates, p2, W["woa"], W["wod"], W["wout"], W["e8"], W["e8t"], W["dnorm"],
      W["ln1g"], W["ln1b"], W["wpg"], W["wpp"], W["rwh"], W["rwl"], W["rb"], W["ustrict"])


def _moe_kernel(blk_e_ref, nvalid_ref, xs_ref, wgu_ref, bgu_ref, wd_ref, bd_ref, y_ref, *, fc):
    i = pl.program_id(0)

    @pl.when(i < nvalid_ref[0])
    def _():
        xs = xs_ref[...]
        acc = None
        for c in range(D_FF // fc):
            lo, hi = c * fc, (c + 1) * fc
            gate = _dot(xs, wgu_ref[0, :, lo:hi]) + bgu_ref[0, :, lo:hi]
            up = _dot(xs, wgu_ref[0, :, D_FF + lo:D_FF + hi]) + bgu_ref[0, :, D_FF + lo:D_FF + hi]
            gate = jnp.minimum(gate, SWIGLU_LIMIT)
            up = jnp.clip(up, -SWIGLU_LIMIT, SWIGLU_LIMIT)
            act = gate * (1.0 / (1.0 + jnp.exp(-SWIGLU_ALPHA * gate))) * (up + 1.0)
            part = _dot(act.astype(BF16), wd_ref[0, lo:hi, :])
            acc = part if acc is None else acc + part
        y_ref[...] = (acc + bd_ref[0]).astype(BF16)

    @pl.when(i >= nvalid_ref[0])
    def _():
        y_ref[...] = jnp.zeros_like(y_ref)


def _moe(xs, blk_e, nvalid, W, bm, fc=256):
    P = xs.shape[0]
    grid_spec = pltpu.PrefetchScalarGridSpec(
        num_scalar_prefetch=2,
        grid=(P // bm,),
        in_specs=[
            pl.BlockSpec((bm, D_MODEL), lambda i, be, nv: (i, 0)),
            pl.BlockSpec((1, D_MODEL, 2 * D_FF), lambda i, be, nv: (be[i], 0, 0)),
            pl.BlockSpec((1, 1, 2 * D_FF), lambda i, be, nv: (be[i], 0, 0)),
            pl.BlockSpec((1, D_FF, D_MODEL), lambda i, be, nv: (be[i], 0, 0)),
            pl.BlockSpec((1, 1, D_MODEL), lambda i, be, nv: (be[i], 0, 0)),
        ],
        out_specs=pl.BlockSpec((bm, D_MODEL), lambda i, be, nv: (i, 0)),
    )
    return pl.pallas_call(
        functools.partial(_moe_kernel, fc=fc),
        grid_spec=grid_spec,
        out_shape=jax.ShapeDtypeStruct((P, D_MODEL), BF16),
        compiler_params=_params(("arbitrary",)),
        name="moe",
    )(blk_e, nvalid, xs, W["wgu"], W["bgu"], W["wd"], W["bd"])


def _final_kernel(r_ref, yg_ref, tg_ref, g_ref, b_ref, y_ref):
    acc = r_ref[...]
    for k in range(TOP_K):
        acc = acc + yg_ref[k].astype(F32) * tg_ref[:, k:k + 1]
    y_ref[...] = _layer_norm(acc, g_ref[...], b_ref[...])


def _final(r, yg, tg, W, tm=512):
    T = r.shape[0]
    tm = min(tm, T)
    row = lambda i: (i, 0)
    return pl.pallas_call(
        _final_kernel,
        grid=(T // tm,),
        in_specs=[pl.BlockSpec((tm, D_MODEL), row),
                  pl.BlockSpec((TOP_K, tm, D_MODEL), lambda i: (0, i, 0)),
                  pl.BlockSpec((tm, TOP_K), row),
                  _const_spec((1, D_MODEL)), _const_spec((1, D_MODEL))],
        out_specs=pl.BlockSpec((tm, D_MODEL), row),
        out_shape=jax.ShapeDtypeStruct((T, D_MODEL), F32),
        compiler_params=_params(("parallel",)),
        name="final_ln",
    )(r, yg, tg, W["ln2g"], W["ln2b"])


def _pad_heads(w, n_heads, width, start, size, dst=0):
    K = w.shape[0]
    w3 = w.reshape(K, n_heads, width)[:, :, start:start + size]
    out = jnp.zeros((K, n_heads, HEAD_PAD), w.dtype)
    out = out.at[:, :, dst:dst + size].set(w3)
    return out.reshape(K, n_heads * HEAD_PAD)


def _prep_weights(w_in, q_a_norm, w_uq, kv_a_norm, w_ukv, w_o_attn, dn_conv, dn_a_log, dn_dt_bias,
                  dn_norm, w_o_dn, w_out, ln1_g, ln1_b, router_w, router_b, w_gate_up, b_gate_up,
                  w_down, b_down, ple_w_proj, ple_w_gate, ln2_g, ln2_b):
    W = {}
    half = ROPE_DIM // 2
    o = 0
    cq = w_in[:, o:o + Q_LORA]; o += Q_LORA
    ckv = w_in[:, o:o + KV_LORA]; o += KV_LORA
    kr = w_in[:, o:o + ROPE_DIM]; o += ROPE_DIM
    nqk = DN_HEADS * DN_DK
    dq = w_in[:, o:o + nqk]; o += nqk
    dk = w_in[:, o:o + nqk]; o += nqk
    dv = w_in[:, o:o + DN_HEADS * DN_DV]; o += DN_HEADS * DN_DV
    dz = w_in[:, o:o + DN_HEADS * DN_DV]; o += DN_HEADS * DN_DV
    da = w_in[:, o:o + 2 * DN_HEADS]; o += 2 * DN_HEADS
    db = w_in[:, o:o + 2 * DN_HEADS]; o += 2 * DN_HEADS
    gate = w_in[:, o:o + 2 * D_MODEL]

    def lane_block(parts):
        w = jnp.concatenate(parts, axis=1)
        return jnp.pad(w, ((0, 0), (0, LANES - w.shape[1])))

    zeros64 = jnp.zeros((D_MODEL, NOPE_DIM), F32)
    kr_blk = lane_block([zeros64, kr])
    krs_blk = lane_block([zeros64, kr[:, half:], kr[:, :half]])
    ab = [lane_block([da[:, d * DN_HEADS:(d + 1) * DN_HEADS], db[:, d * DN_HEADS:(d + 1) * DN_HEADS]])
          for d in range(2)]

    def interleave_vk(v, k):
        lead = v.shape[:-1]
        v3 = v.reshape(lead + (DN_HEADS, DN_DV))
        k3 = k.reshape(lead + (DN_HEADS, DN_DK))
        return jnp.concatenate([v3, k3], axis=-1).reshape(lead + (DN_HEADS * (DN_DV + DN_DK),))

    W["w1"] = jnp.concatenate([cq, ckv, kr_blk, krs_blk, ab[0], ab[1], dq, interleave_vk(dv, dk), dz, gate],
                              axis=1).astype(BF16)
    W["qan"] = q_a_norm.reshape(1, Q_LORA)
    W["kvan"] = kv_a_norm.reshape(1, KV_LORA)

    qw = NOPE_DIM + ROPE_DIM
    wq_nope = _pad_heads(w_uq, MLA_HEADS, qw, 0, NOPE_DIM, 0)
    wq_r1 = _pad_heads(w_uq, MLA_HEADS, qw, NOPE_DIM, half, NOPE_DIM)
    wq_r2 = _pad_heads(w_uq, MLA_HEADS, qw, NOPE_DIM + half, half, NOPE_DIM + half)
    W["wq"] = (wq_nope + wq_r1 + wq_r2).astype(BF16)
    wq_s1 = _pad_heads(w_uq, MLA_HEADS, qw, NOPE_DIM + half, half, NOPE_DIM)
    wq_s2 = _pad_heads(w_uq, MLA_HEADS, qw, NOPE_DIM, half, NOPE_DIM + half)
    W["wqs"] = (wq_s1 + wq_s2).astype(BF16)
    kvw = NOPE_DIM + V_DIM
    W["wk"] = _pad_heads(w_ukv, MLA_HEADS, kvw, 0, NOPE_DIM, 0).astype(BF16)
    W["wv"] = _pad_heads(w_ukv, MLA_HEADS, kvw, NOPE_DIM, V_DIM, 0).astype(BF16)

    neg_a = -jnp.exp(dn_a_log.astype(F32))
    abp = jnp.zeros((8, LANES), F32)
    for d in range(2):
        abp = abp.at[2 * d, :DN_HEADS].set(neg_a[d])
        abp = abp.at[2 * d + 1, :DN_HEADS].set(dn_dt_bias[d].astype(F32))
    W["abp"] = abp

    cw = jnp.concatenate([dn_conv[:, :nqk], interleave_vk(dn_conv[:, 2 * nqk:], dn_conv[:, nqk:2 * nqk])], axis=1)
    W["convw"] = jnp.pad(cw.astype(F32), ((0, 8 - CONV_K), (0, 0)))

    def group_indicator(width, group):
        e = (np.arange(width)[:, None] // group == np.arange(LANES)[None, :]).astype(np.float32)
        return e

    eq = group_indicator(nqk, DN_DK)
    W["eq"] = jnp.asarray(eq, BF16)
    W["eqt"] = jnp.asarray(eq.T, BF16)
    evk = group_indicator(2 * nqk, DN_DK)
    W["evk"] = jnp.asarray(evk, BF16)
    W["evkt"] = jnp.asarray(evk.T, BF16)
    W["e8"] = W["eq"]
    W["e8t"] = W["eqt"]
    r = np.arange(DN_BLOCK)
    W["tri"] = jnp.asarray(np.stack([r[:, None] >= r[None, :], r[:, None] <= r[None, :]]).astype(np.float32))
    W["dn_masks"] = jnp.asarray(_dn_masks())
    rt = np.arange(MIX_TILE)
    W["ustrict"] = jnp.asarray((rt[:, None] < rt[None, :]).astype(np.float32), BF16)

    woa = w_o_attn.reshape(MLA_HEADS, V_DIM, D_MODEL)
    woa = jnp.pad(woa, ((0, 0), (0, HEAD_PAD - V_DIM), (0, 0)))
    W["woa"] = woa.reshape(MLA_HEADS * HEAD_PAD, D_MODEL).astype(BF16)
    W["wod"] = w_o_dn.astype(BF16)
    W["wout"] = w_out.astype(BF16)
    W["dnorm"] = jnp.tile(dn_norm.astype(F32), DN_HEADS).reshape(1, DN_HEADS * DN_DV)
    W["ln1g"] = ln1_g.reshape(1, D_MODEL)
    W["ln1b"] = ln1_b.reshape(1, D_MODEL)
    W["ln2g"] = ln2_g.reshape(1, D_MODEL)
    W["ln2b"] = ln2_b.reshape(1, D_MODEL)
    W["wpg"] = ple_w_gate.astype(BF16)
    W["wpp"] = ple_w_proj.astype(BF16)
    rwt = router_w.T.astype(F32)
    W["rwh"], W["rwl"] = _split_bf16(rwt)
    W["rb"] = router_b.reshape(N_EXPERTS, 1).astype(F32)
    W["wgu"] = w_gate_up.astype(BF16)
    W["bgu"] = b_gate_up.reshape(N_EXPERTS, 1, 2 * D_FF).astype(F32)
    W["wd"] = w_down.astype(BF16)
    W["bd"] = b_down.reshape(N_EXPERTS, 1, D_MODEL).astype(F32)
    return W


def _rope_tables(S):
    half = ROPE_DIM // 2
    inv = ROPE_THETA ** (-jnp.arange(0, ROPE_DIM, 2, dtype=F32) / ROPE_DIM)
    ang = jnp.arange(S, dtype=F32)[:, None] * inv[None, :]
    cos, sin = jnp.cos(ang), jnp.sin(ang)
    c = (NOPE_DIM + ROPE_DIM) ** -0.5 * math.log2(math.e)
    pad = jnp.zeros((S, HEAD_PAD - NOPE_DIM - ROPE_DIM), F32)
    cos_blk = jnp.concatenate([cos, cos, pad], axis=1)
    sin_blk = jnp.concatenate([-sin, sin, pad], axis=1)
    cosq = jnp.concatenate([jnp.ones((S, NOPE_DIM), F32), cos_blk], axis=1) * c
    sinq = jnp.concatenate([jnp.zeros((S, NOPE_DIM), F32), sin_blk], axis=1) * c
    cosk = jnp.concatenate([jnp.zeros((S, NOPE_DIM), F32), cos_blk], axis=1)
    sink = jnp.concatenate([jnp.zeros((S, NOPE_DIM), F32), sin_blk], axis=1)
    return cosq, sinq, cosk, sink


def _dest_kernel(pstart_ref, ti_ref, rank_ref, dest_ref):
    ti = ti_ref[...]
    dest = rank_ref[...]
    for e in range(N_EXPERTS):
        dest = dest + jnp.where(ti == e, pstart_ref[e], 0)
    dest_ref[...] = dest


def _dest(p_start, top_i, rank, tile=8192):
    T = top_i.shape[1]
    tile = min(tile, T)
    spec = pl.BlockSpec((TOP_K, tile), lambda i, ps: (0, i))
    return pl.pallas_call(
        _dest_kernel,
        grid_spec=pltpu.PrefetchScalarGridSpec(num_scalar_prefetch=1, grid=(T // tile,),
                                               in_specs=[spec, spec], out_specs=spec),
        out_shape=jax.ShapeDtypeStruct((TOP_K, T), jnp.int32),
        compiler_params=_params(("parallel",)),
        name="slot_index",
    )(p_start, top_i, rank)


def _route(top_i, rank, counts, T, bm):
    A = TOP_K * T
    counts = counts.astype(jnp.int32)
    padded = ((counts + bm - 1) // bm) * bm
    p_end = jnp.cumsum(padded)
    p_start = p_end - padded
    dest = _dest(p_start, top_i, rank)
    nblk = A // bm + N_EXPERTS
    P = nblk * bm
    tok = jnp.tile(jnp.arange(T, dtype=jnp.int32), TOP_K)
    slot_tok = jnp.zeros((P,), jnp.int32).at[dest.reshape(A)].set(tok)
    blk_start = jnp.arange(nblk, dtype=jnp.int32) * bm
    blk_e = jnp.minimum(jnp.sum(p_end[None, :] <= blk_start[:, None], axis=1), N_EXPERTS - 1).astype(jnp.int32)
    nvalid = (p_end[-1] // bm).astype(jnp.int32).reshape(1)
    return dest, slot_tok, blk_e, nvalid


def _layer(x, p, W, bm):
    B, S, _ = x.shape
    T = B * S
    x2 = x.reshape(T, D_MODEL)
    p2 = p.reshape(T, PLE_DIM)
    Wl = dict(W)
    Wl["cosq"], Wl["sinq"], Wl["cosk"], Wl["sink"] = _rope_tables(S)

    q, k, v, dn, z, gates, gb = _in_proj(x2, S, Wl)
    attn = _attention(q, k, v, B, S)
    qn, vk, aux = _dn_prep(dn, gb, S, Wl)
    kt = vk.reshape(T, DN_HEADS, 2, DN_DK)[:, :, 1, :].reshape(T, DN_HEADS * DN_DK).T
    auxt = jnp.swapaxes(aux, 1, 2)
    o2 = _deltanet(qn, vk, kt, aux, auxt, B, S, Wl)
    r, hb, top_i, top_g, rank, cnt = _mix(x2, attn, o2, z, gates, p2, Wl)

    dest, slot_tok, blk_e, nvalid = _route(top_i, rank, cnt[:, 0], T, bm)
    xs = jnp.take(hb, slot_tok, axis=0)
    yb = _moe(xs, blk_e, nvalid, Wl, bm)
    yg = jnp.take(yb, dest.reshape(TOP_K * T), axis=0).reshape(TOP_K, T, D_MODEL)
    y = _final(r, yg, top_g.T, Wl)
    return y.reshape(B, S, D_MODEL)


def kernel(x_prompt, x_sample, p_prompt, p_sample, w_in, q_a_norm, w_uq, kv_a_norm, w_ukv, w_o_attn, dn_conv, dn_a_log, dn_dt_bias, dn_norm, w_o_dn, w_out, ln1_g, ln1_b, router_w, router_b, w_gate_up, b_gate_up, w_down, b_down, ple_w_proj, ple_w_gate, ln2_g, ln2_b):
    y_prompt, y_sample = x_prompt, x_sample
    for l in range(DEPTH):
        W = _prep_weights(w_in[l], q_a_norm[l], w_uq[l], kv_a_norm[l], w_ukv[l], w_o_attn[l], dn_conv[l],
                          dn_a_log[l], dn_dt_bias[l], dn_norm[l], w_o_dn[l], w_out[l], ln1_g[l], ln1_b[l],
                          router_w[l], router_b[l], w_gate_up[l], b_gate_up[l], w_down[l], b_down[l],
                          ple_w_proj[l], ple_w_gate[l], ln2_g[l], ln2_b[l])
        y_prompt = _layer(y_prompt, p_prompt[l], W, bm=MOE_BLOCK)
        y_sample = _layer(y_sample, p_sample[l], W, bm=MOE_BLOCK)
    return (y_prompt, y_sample)
```

```python
import functools
import math

import numpy as np
import jax
import jax.numpy as jnp
from jax import lax
from jax.experimental import pallas as pl
from jax.experimental.pallas import tpu as pltpu

D_MODEL = 1024
MLA_HEADS = 8
Q_LORA = 256
KV_LORA = 128
NOPE_DIM = 64
ROPE_DIM = 32
V_DIM = 64
ROPE_THETA = 10000.0
DN_HEADS = 8
DN_DK = 64
DN_DV = 64
CONV_K = 5
N_EXPERTS = 32
TOP_K = 4
D_FF = 1024
SWIGLU_LIMIT = 7.0
SWIGLU_ALPHA = 1.702
PLE_DIM = 256
DEPTH = 1
DEEPNORM_ALPHA = (2.0 * DEPTH) ** 0.25

LANES = 128
HEAD_PAD = 128
DN_BLOCK = 256
MOE_BLOCK = 512
MIX_TILE = 512
VMEM_LIMIT = 56 * 1024 * 1024

_C_CQ = 0
_C_CKV = _C_CQ + Q_LORA
_C_KR = _C_CKV + KV_LORA
_C_KRS = _C_KR + LANES
_C_AB0 = _C_KRS + LANES
_C_AB1 = _C_AB0 + LANES
_C_DNQ = _C_AB1 + LANES
_C_DNVK = _C_DNQ + DN_HEADS * DN_DK
_C_Z = _C_DNVK + DN_HEADS * (DN_DK + DN_DV)
_C_GATE = _C_Z + DN_HEADS * DN_DV
_C_END = _C_GATE + 2 * D_MODEL

BF16 = jnp.bfloat16
F32 = jnp.float32


def _dot(a, b):
    return jnp.dot(a, b, preferred_element_type=F32)


def _dot_nt(a, b):
    return lax.dot_general(a, b, (((1,), (1,)), ((), ())), preferred_element_type=F32)


def _split_bf16(x):
    hi = x.astype(BF16)
    lo = (x - hi.astype(F32)).astype(BF16)
    return hi, lo


def _const_spec(shape):
    n = len(shape)
    return pl.BlockSpec(shape, lambda *_: (0,) * n, pipeline_mode=pl.Buffered(1))


def _params(sem):
    return pltpu.CompilerParams(dimension_semantics=sem, vmem_limit_bytes=VMEM_LIMIT)


def _in_proj_kernel(x_ref, w1_ref, qan_ref, kvan_ref, wq_ref, wqs_ref, wk_ref, wv_ref,
                    cosq_ref, sinq_ref, cosk_ref, sink_ref, abp_ref,
                    q_ref, k_ref, v_ref, dn_ref, z_ref, gates_ref, gb_ref):
    xb = x_ref[...].astype(BF16)

    def proj(lo, hi):
        return _dot(xb, w1_ref[:, lo:hi])

    def rms(c, g):
        return (c * lax.rsqrt(jnp.mean(c * c, axis=-1, keepdims=True) + 1e-6) * g).astype(BF16)

    cqn = rms(proj(_C_CQ, _C_CKV), qan_ref[...])
    qa = _dot(cqn, wq_ref[...])
    qb = _dot(cqn, wqs_ref[...])
    ckvn = rms(proj(_C_CKV, _C_KR), kvan_ref[...])
    kw = _dot(ckvn, wk_ref[...])
    vw = _dot(ckvn, wv_ref[...])
    kr = proj(_C_KR, _C_KRS) * cosk_ref[...] + proj(_C_KRS, _C_AB0) * sink_ref[...]
    cosq = cosq_ref[...]
    sinq = sinq_ref[...]
    lane = lax.broadcasted_iota(jnp.int32, (1, HEAD_PAD), 1)
    ones_col = (lane == V_DIM).astype(F32)
    for h in range(MLA_HEADS):
        sl = slice(h * HEAD_PAD, (h + 1) * HEAD_PAD)
        q_ref[:, sl] = (qa[:, sl] * cosq + qb[:, sl] * sinq).astype(BF16)
        k_ref[:, sl] = (kw[:, sl] + kr).astype(BF16)
        v_ref[:, sl] = (vw[:, sl] + ones_col).astype(BF16)

    for d, c0 in enumerate((_C_AB0, _C_AB1)):
        ab = proj(c0, c0 + LANES)
        neg_a = abp_ref[2 * d:2 * d + 1, :]
        dtb = abp_ref[2 * d + 1:2 * d + 2, :]
        t = ab + dtb
        sp = jnp.maximum(t, 0.0) + jnp.log(1.0 + jnp.exp(-jnp.abs(t)))
        g = neg_a * sp
        beta = 1.0 / (1.0 + jnp.exp(-ab))
        gb_ref[d] = jnp.where(lane < DN_HEADS, g, beta)[:, :2 * DN_HEADS]

    dn_ref[...] = proj(_C_DNQ, _C_Z).astype(BF16)
    z_ref[...] = proj(_C_Z, _C_GATE).astype(BF16)
    gl = proj(_C_GATE, _C_END)
    gates_ref[...] = (1.0 / (1.0 + jnp.exp(-gl))).astype(BF16)


def _in_proj(x2, S, W, tm=512):
    T = x2.shape[0]
    nseq = S // tm
    row = lambda i: (i, 0)
    pos = lambda i: (i % nseq, 0)
    dn_w = _C_Z - _C_DNQ
    return pl.pallas_call(
        _in_proj_kernel,
        grid=(T // tm,),
        in_specs=[
            pl.BlockSpec((tm, D_MODEL), row),
            _const_spec((D_MODEL, _C_END)),
            _const_spec((1, Q_LORA)), _const_spec((1, KV_LORA)),
            _const_spec((Q_LORA, MLA_HEADS * HEAD_PAD)), _const_spec((Q_LORA, MLA_HEADS * HEAD_PAD)),
            _const_spec((KV_LORA, MLA_HEADS * HEAD_PAD)), _const_spec((KV_LORA, MLA_HEADS * HEAD_PAD)),
            pl.BlockSpec((tm, HEAD_PAD), pos), pl.BlockSpec((tm, HEAD_PAD), pos),
            pl.BlockSpec((tm, HEAD_PAD), pos), pl.BlockSpec((tm, HEAD_PAD), pos),
            _const_spec((8, LANES)),
        ],
        out_specs=[
            pl.BlockSpec((tm, MLA_HEADS * HEAD_PAD), row),
            pl.BlockSpec((tm, MLA_HEADS * HEAD_PAD), row),
            pl.BlockSpec((tm, MLA_HEADS * HEAD_PAD), row),
            pl.BlockSpec((tm, dn_w), row),
            pl.BlockSpec((tm, DN_HEADS * DN_DV), row),
            pl.BlockSpec((tm, 2 * D_MODEL), row),
            pl.BlockSpec((2, tm, 2 * DN_HEADS), lambda i: (0, i, 0)),
        ],
        out_shape=[
            jax.ShapeDtypeStruct((T, MLA_HEADS * HEAD_PAD), BF16),
            jax.ShapeDtypeStruct((T, MLA_HEADS * HEAD_PAD), BF16),
            jax.ShapeDtypeStruct((T, MLA_HEADS * HEAD_PAD), BF16),
            jax.ShapeDtypeStruct((T, dn_w), BF16),
            jax.ShapeDtypeStruct((T, DN_HEADS * DN_DV), BF16),
            jax.ShapeDtypeStruct((T, 2 * D_MODEL), BF16),
            jax.ShapeDtypeStruct((2, T, 2 * DN_HEADS), F32),
        ],
        compiler_params=_params(("parallel",)),
        name="in_proj",
    )(x2, W["w1"], W["qan"], W["kvan"], W["wq"], W["wqs"], W["wk"], W["wv"],
      W["cosq"], W["sinq"], W["cosk"], W["sink"], W["abp"])


def _attn_kernel(q_ref, k_ref, v_ref, o_ref, *, tk):
    tq = q_ref.shape[0]
    S = k_ref.shape[0]
    q = q_ref[...]

    def body(j, carry):
        m, acc = carry
        off = pl.multiple_of(j * tk, tk)
        s = _dot_nt(q, k_ref[pl.ds(off, tk), :])
        m_new = jnp.maximum(m, jnp.max(s, axis=-1, keepdims=True))
        p = jnp.exp2(s - m_new).astype(BF16)
        acc = acc * jnp.exp2(m - m_new) + _dot(p, v_ref[pl.ds(off, tk), :])
        return m_new, acc

    m0 = jnp.full((tq, 1), -1e30, F32)
    acc0 = jnp.zeros((tq, HEAD_PAD), F32)
    _, acc = lax.fori_loop(0, S // tk, body, (m0, acc0), unroll=4)
    o_ref[...] = (acc / acc[:, V_DIM:V_DIM + 1]).astype(BF16)


def _attention(q, k, v, B, S, tq=512, tk=1024):
    T = q.shape[0]
    nq = S // tq
    tk = min(tk, S)
    return pl.pallas_call(
        functools.partial(_attn_kernel, tk=tk),
        grid=(B, MLA_HEADS, nq),
        in_specs=[
            pl.BlockSpec((tq, HEAD_PAD), lambda b, h, i: (b * nq + i, h)),
            pl.BlockSpec((S, HEAD_PAD), lambda b, h, i: (b, h)),
            pl.BlockSpec((S, HEAD_PAD), lambda b, h, i: (b, h)),
        ],
        out_specs=pl.BlockSpec((tq, HEAD_PAD), lambda b, h, i: (b * nq + i, h)),
        out_shape=jax.ShapeDtypeStruct((T, MLA_HEADS * HEAD_PAD), BF16),
        compiler_params=_params(("parallel", "parallel", "arbitrary")),
        name="attention",
    )(q, k, v)


_HALO = 16


def _dn_prep_kernel(x_ref, prev_ref, next_ref, cw_ref, eq_ref, eqt_ref, evk_ref, evkt_ref,
                    gb_ref, tri_ref, qn_ref, vk_ref, aux_ref, *, nseq):
    i = pl.program_id(0)
    tp = x_ref.shape[0]
    first = (i % nseq) == 0
    last = (i % nseq) == nseq - 1
    prev = jnp.where(first, 0.0, prev_ref[...].astype(F32))
    nxt = jnp.where(last, 0.0, next_ref[...].astype(F32))
    xe = jnp.concatenate([prev, x_ref[...].astype(F32), nxt], axis=0)
    n = tp + 2 * _HALO
    y = None
    for j in range(CONV_K):
        shift = (CONV_K // 2 - j) % n
        xs = xe if shift == 0 else pltpu.roll(xe, shift, axis=0)
        term = xs[_HALO:_HALO + tp, :] * cw_ref[j:j + 1, :]
        y = term if y is None else y + term
    y = y * (1.0 / (1.0 + jnp.exp(-y)))

    def group_scale(v, e_ref, et_ref):
        sq = v * v
        hi, lo = _split_bf16(sq)
        ss = _dot(hi, e_ref[...]) + _dot(lo, e_ref[...])
        inv = lax.rsqrt(ss + 1e-6)
        ih, il = _split_bf16(inv)
        return _dot(ih, et_ref[...]) + _dot(il, et_ref[...])

    nq = DN_HEADS * DN_DK
    yq = y[:, :nq]
    qn_ref[...] = (yq * group_scale(yq, eq_ref, eqt_ref) * (DN_DK ** -0.5)).astype(BF16)
    yvk = y[:, nq:]
    sc = group_scale(yvk, evk_ref, evkt_ref)
    lane = lax.broadcasted_iota(jnp.int32, (1, yvk.shape[1]), 1)
    is_k = (lane // DN_DV) % 2 == 1
    vk_ref[...] = (yvk * jnp.where(is_k, sc, 1.0)).astype(BF16)

    hcol = lax.broadcasted_iota(jnp.int32, (1, 2 * DN_HEADS), 1) < DN_HEADS
    for d in range(2):
        gb = gb_ref[d]
        for blk in range(tp // DN_BLOCK):
            rs = slice(blk * DN_BLOCK, (blk + 1) * DN_BLOCK)
            g = gb[rs]
            cs = jnp.dot(tri_ref[d], g, preferred_element_type=F32, precision=lax.Precision.HIGHEST)
            aux_ref[d, rs, :] = jnp.where(hcol, cs, g)


def _dn_prep(dn, gb, S, W, tp=256):
    T = dn.shape[0]
    nseq = S // tp
    hb = tp // _HALO
    nh = T // _HALO
    dn_w = dn.shape[1]
    nq = DN_HEADS * DN_DK
    return pl.pallas_call(
        functools.partial(_dn_prep_kernel, nseq=nseq),
        grid=(T // tp,),
        in_specs=[
            pl.BlockSpec((tp, dn_w), lambda i: (i, 0)),
            pl.BlockSpec((_HALO, dn_w), lambda i: (jnp.maximum(i * hb - 1, 0), 0)),
            pl.BlockSpec((_HALO, dn_w), lambda i: (jnp.minimum((i + 1) * hb, nh - 1), 0)),
            _const_spec((8, dn_w)),
            _const_spec((nq, LANES)), _const_spec((LANES, nq)),
            _const_spec((dn_w - nq, LANES)), _const_spec((LANES, dn_w - nq)),
            pl.BlockSpec((2, tp, 2 * DN_HEADS), lambda i: (0, i, 0)),
            _const_spec((2, DN_BLOCK, DN_BLOCK)),
        ],
        out_specs=[
            pl.BlockSpec((tp, nq), lambda i: (i, 0)),
            pl.BlockSpec((tp, dn_w - nq), lambda i: (i, 0)),
            pl.BlockSpec((2, tp, 2 * DN_HEADS), lambda i: (0, i, 0)),
        ],
        out_shape=[
            jax.ShapeDtypeStruct((T, nq), BF16),
            jax.ShapeDtypeStruct((T, dn_w - nq), BF16),
            jax.ShapeDtypeStruct((2, T, 2 * DN_HEADS), F32),
        ],
        compiler_params=_params(("parallel",)),
        name="dn_prep",
    )(dn, dn, dn, W["convw"], W["eq"], W["eqt"], W["evk"], W["evkt"], gb, W["tri"])


_MERGE_SIZES = (16, 32, 64, 128)


def _active_blocks(d, s):
    return range(1 - d, DN_BLOCK // s, 2)


def _dn_masks(d):
    r = np.arange(DN_BLOCK)[:, None]
    c = np.arange(DN_BLOCK)[None, :]
    rr, cc = (r, c) if d == 0 else (c, r)
    tri = np.stack([rr >= cc, rr > cc]).astype(np.float32)
    diag16 = ((r // 16) == (c // 16)).astype(np.float32)
    offc = []
    for s in _MERGE_SIZES:
        full = ((rr // s) % 2 == 1) & ((rr // s) == (cc // s) + 1)
        rows = np.concatenate([np.arange(b * s, (b + 1) * s) for b in _active_blocks(d, s)])
        assert not np.delete(full, rows, axis=0).any()
        offc.append(full[rows].astype(np.float32))
    return tri, diag16, np.stack(offc)


def _deltanet_kernel(qn_ref, vk_ref, kt_ref, aux_ref, auxt_ref, tri_ref, diag16_ref, offc_ref, o_ref,
                     s_ref, nm_ref, p_ref, x_ref, aqk_ref, *, d):
    i = pl.program_id(1)
    C = DN_BLOCK
    heads = range(DN_HEADS)

    @pl.when(i == 0)
    def _():
        s_ref[...] = jnp.zeros_like(s_ref)

    rowi = lax.broadcasted_iota(jnp.int32, (C, C), 0)
    coli = lax.broadcasted_iota(jnp.int32, (C, C), 1)
    eye = (rowi == coli).astype(F32)
    lane = lax.broadcasted_iota(jnp.int32, (1, LANES), 1)
    r64 = lax.broadcasted_iota(jnp.int32, (DN_DK, LANES), 0)
    c64 = lax.broadcasted_iota(jnp.int32, (DN_DK, LANES), 1)
    zeros_kt = jnp.zeros((DN_DK, C), BF16)
    zeros_s = jnp.zeros((DN_DK, LANES), F32)

    def q_pair(h):
        return qn_ref[:, (h // 2) * LANES:(h // 2 + 1) * LANES]

    def vk_head(h):
        return vk_ref[:, h * LANES:(h + 1) * LANES]

    def kt_head(h):
        return kt_ref[h * DN_DK:(h + 1) * DN_DK, :]

    def gc_col(h):
        return aux_ref[0, :, h:h + 1]

    def beta_col(h):
        return aux_ref[0, :, DN_HEADS + h:DN_HEADS + h + 1]

    def gc_row(h):
        return auxt_ref[0, h:h + 1, :]

    for h in heads:
        kt = kt_head(h)
        kt_for_q = jnp.concatenate([kt, zeros_kt] if h % 2 == 0 else [zeros_kt, kt], axis=0)
        kt_for_k = jnp.concatenate([zeros_kt, kt], axis=0)
        qk = _dot(q_pair(h), kt_for_q)
        kk = _dot(vk_head(h), kt_for_k)
        e0 = jnp.exp(jnp.minimum(gc_col(h) - gc_row(h), 0.0))
        aqk_ref[h] = (qk * e0 * tri_ref[0]).astype(BF16)
        nm_ref[h] = (-(kk * e0 * tri_ref[1]) * beta_col(h)).astype(BF16)

    diag16 = diag16_ref[...]
    for h in heads:
        nd = nm_ref[h] * diag16
        p_ref[h] = (eye + nd.astype(F32)).astype(BF16)
        x_ref[h] = _dot(nd, nd).astype(BF16)
    for it in range(3):
        for h in heads:
            pb = p_ref[h]
            sqb = x_ref[h]
            p_ref[h] = (pb.astype(F32) + _dot(pb, sqb)).astype(BF16)
            if it < 2:
                x_ref[h] = _dot(sqb, sqb).astype(BF16)
    for k, s in enumerate(_MERGE_SIZES):
        blocks = list(_active_blocks(d, s))
        zeros_blk = jnp.zeros((s, C), BF16)

        def active_rows(ref, h):
            return jnp.concatenate([ref[h, b * s:(b + 1) * s, :] for b in blocks], axis=0)

        for h in heads:
            xc = _dot(active_rows(nm_ref, h) * offc_ref[k], p_ref[h]).astype(BF16)
            pieces = []
            for j in range(len(blocks)):
                piece = xc[j * s:(j + 1) * s, :]
                pieces += [zeros_blk, piece] if d == 0 else [piece, zeros_blk]
            x_ref[h] = jnp.concatenate(pieces, axis=0)
        for h in heads:
            pr = active_rows(p_ref, h)
            prn = (pr.astype(F32) + _dot(pr, x_ref[h])).astype(BF16)
            for j, b in enumerate(blocks):
                p_ref[h, b * s:(b + 1) * s, :] = prn[j * s:(j + 1) * s, :]

    o_pair = None
    for h in heads:
        par = h % 2
        gc_c = gc_col(h)
        gc_r = gc_row(h)
        g_tot = gc_r[:, C - 1:C] if d == 0 else gc_r[:, 0:1]
        egc = jnp.exp(gc_c)
        rhs = (vk_head(h).astype(F32) * beta_col(h) * jnp.where(lane < DN_DV, 1.0, egc)).astype(BF16)
        uw = _dot(p_ref[h], rhs).astype(BF16)
        s_pl = s_ref[h]
        eye_pl = (c64 == r64 + par * DN_DV).astype(F32)
        s_aug = jnp.concatenate([eye_pl, -s_pl], axis=0).astype(BF16)
        v_new = _dot(uw, s_aug).astype(BF16)
        s_sel = jnp.concatenate([s_pl, zeros_s] if par == 0 else [zeros_s, s_pl], axis=0).astype(BF16)
        qd = (q_pair(h).astype(F32) * egc).astype(BF16)
        o_pl = _dot(qd, s_sel) + _dot(aqk_ref[h], v_new)
        kd = (kt_head(h).astype(F32) * jnp.exp(g_tot - gc_r)).astype(BF16)
        s_ref[h] = s_pl * jnp.exp(g_tot) + _dot(kd, v_new)
        if par == 0:
            o_pair = o_pl
        else:
            o_ref[:, (h // 2) * LANES:(h // 2 + 1) * LANES] = o_pair + o_pl


def _deltanet(qn, vk, kt, aux, auxt, B, S, W, d):
    T = qn.shape[0]
    C = DN_BLOCK
    nb = S // C

    def blk(b, i):
        return b * nb + (i if d == 0 else nb - 1 - i)

    nq = DN_HEADS * DN_DK
    tri, diag16, offc = W["dn_masks"][d]
    return pl.pallas_call(
        functools.partial(_deltanet_kernel, d=d),
        grid=(B, nb),
        in_specs=[
            pl.BlockSpec((C, nq), lambda b, i: (blk(b, i), 0)),
            pl.BlockSpec((C, 2 * nq), lambda b, i: (blk(b, i), 0)),
            pl.BlockSpec((nq, C), lambda b, i: (0, blk(b, i))),
            pl.BlockSpec((1, C, 2 * DN_HEADS), lambda b, i: (d, blk(b, i), 0)),
            pl.BlockSpec((1, 2 * DN_HEADS, C), lambda b, i: (d, 0, blk(b, i))),
            _const_spec((2, C, C)), _const_spec((C, C)), _const_spec((len(_MERGE_SIZES), C // 2, C)),
        ],
        out_specs=pl.BlockSpec((C, DN_HEADS * DN_DV), lambda b, i: (blk(b, i), 0)),
        out_shape=jax.ShapeDtypeStruct((T, DN_HEADS * DN_DV), F32),
        scratch_shapes=[pltpu.VMEM((DN_HEADS, DN_DK, LANES), F32)]
        + [pltpu.VMEM((DN_HEADS, C, C), BF16)] * 4,
        compiler_params=_params(("parallel", "arbitrary")),
        name="deltanet_fwd" if d == 0 else "deltanet_bwd",
    )(qn, vk, kt, aux, auxt, tri, diag16, offc)


def _layer_norm(v, g, b):
    mu = jnp.mean(v, axis=-1, keepdims=True)
    c = v - mu
    var = jnp.mean(c * c, axis=-1, keepdims=True)
    return c * lax.rsqrt(var + 1e-5) * g + b


def _mix_kernel(x_ref, attn_ref, of_ref, ob_ref, z_ref, gates_ref, p_ref,
                woa_ref, wod_ref, wout_ref, e8_ref, e8t_ref, dnorm_ref, ln1g_ref, ln1b_ref,
                wpg_ref, wpp_ref, rwh_ref, rwl_ref, rb_ref, ustrict_ref,
                r_ref, hb_ref, ti_ref, tg_ref, rank_ref, cnt_ref, run_ref):
    @pl.when(pl.program_id(0) == 0)
    def _():
        run_ref[...] = jnp.zeros_like(run_ref)

    oa = _dot(attn_ref[...], woa_ref[...])
    o = of_ref[...] + ob_ref[...]
    hi, lo = _split_bf16(o * o)
    ms = (_dot(hi, e8_ref[...]) + _dot(lo, e8_ref[...])) * (1.0 / DN_DV)
    ih, il = _split_bf16(lax.rsqrt(ms + 1e-6))
    sc = _dot(ih, e8t_ref[...]) + _dot(il, e8t_ref[...])
    zf = z_ref[...].astype(F32)
    od_in = o * sc * dnorm_ref[...] * (zf * (1.0 / (1.0 + jnp.exp(-zf))))
    od = _dot(od_in.astype(BF16), wod_ref[...])
    mix = gates_ref[:, :D_MODEL].astype(F32) * oa + gates_ref[:, D_MODEL:].astype(F32) * od
    mo = _dot(mix.astype(BF16), wout_ref[...])
    h = _layer_norm(DEEPNORM_ALPHA * x_ref[...] + mo, ln1g_ref[...], ln1b_ref[...])
    hb = h.astype(BF16)
    hb_ref[...] = hb
    pg = _dot(hb, wpg_ref[...])
    pp = _dot(p_ref[...].astype(BF16), wpp_ref[...])
    r_ref[...] = DEEPNORM_ALPHA * h + pp * (1.0 / (1.0 + jnp.exp(-pg)))

    hl = (h - hb.astype(F32)).astype(BF16)
    logits = (_dot_nt(rwh_ref[...], hb) + _dot_nt(rwh_ref[...], hl)
              + _dot_nt(rwl_ref[...], hb) + rb_ref[...])
    eid = lax.broadcasted_iota(jnp.int32, logits.shape, 0)
    vals = []
    run = run_ref[:, 0:1]
    for k in range(TOP_K):
        m = jnp.max(logits, axis=0, keepdims=True)
        idx = jnp.min(jnp.where(logits == m, eid, N_EXPERTS), axis=0, keepdims=True)
        ti_ref[k:k + 1, :] = idx
        vals.append(m)
        hit = eid == idx
        logits = jnp.where(hit, -jnp.inf, logits)
        onehot = hit.astype(F32)
        earlier = _dot(onehot.astype(BF16), ustrict_ref[...])
        rank = jnp.sum(onehot * (run + earlier), axis=0, keepdims=True)
        rank_ref[k:k + 1, :] = rank.astype(jnp.int32)
        run = run + jnp.sum(onehot, axis=1, keepdims=True)
    run_ref[...] = jnp.broadcast_to(run, run_ref.shape)
    cnt_ref[...] = jnp.broadcast_to(run, cnt_ref.shape)
    es = [jnp.exp(v - vals[0]) for v in vals]
    den = es[0] + es[1] + es[2] + es[3]
    for k in range(TOP_K):
        tg_ref[k:k + 1, :] = es[k] / den


def _mix(x2, attn, o_f, o_b, z, gates, p2, W):
    tm = MIX_TILE
    T = x2.shape[0]
    row = lambda i: (i, 0)
    nd = DN_HEADS * DN_DV
    return pl.pallas_call(
        _mix_kernel,
        grid=(T // tm,),
        in_specs=[
            pl.BlockSpec((tm, D_MODEL), row),
            pl.BlockSpec((tm, MLA_HEADS * HEAD_PAD), row),
            pl.BlockSpec((tm, nd), row),
            pl.BlockSpec((tm, nd), row),
            pl.BlockSpec((tm, nd), row),
            pl.BlockSpec((tm, 2 * D_MODEL), row),
            pl.BlockSpec((tm, PLE_DIM), row),
            _const_spec((MLA_HEADS * HEAD_PAD, D_MODEL)), _const_spec((nd, D_MODEL)),
            _const_spec((D_MODEL, D_MODEL)),
            _const_spec((nd, LANES)), _const_spec((LANES, nd)), _const_spec((1, nd)),
            _const_spec((1, D_MODEL)), _const_spec((1, D_MODEL)),
            _const_spec((D_MODEL, D_MODEL)), _const_spec((PLE_DIM, D_MODEL)),
            _const_spec((N_EXPERTS, D_MODEL)), _const_spec((N_EXPERTS, D_MODEL)),
            _const_spec((N_EXPERTS, 1)),
            _const_spec((tm, tm)),
        ],
        out_specs=[
            pl.BlockSpec((tm, D_MODEL), row),
            pl.BlockSpec((tm, D_MODEL), row),
            pl.BlockSpec((TOP_K, tm), lambda i: (0, i)),
            pl.BlockSpec((TOP_K, tm), lambda i: (0, i)),
            pl.BlockSpec((TOP_K, tm), lambda i: (0, i)),
            _const_spec((N_EXPERTS, LANES)),
        ],
        out_shape=[
            jax.ShapeDtypeStruct((T, D_MODEL), F32),
            jax.ShapeDtypeStruct((T, D_MODEL), BF16),
            jax.ShapeDtypeStruct((TOP_K, T), jnp.int32),
            jax.ShapeDtypeStruct((TOP_K, T), F32),
            jax.ShapeDtypeStruct((TOP_K, T), jnp.int32),
            jax.ShapeDtypeStruct((N_EXPERTS, LANES), F32),
        ],
        scratch_shapes=[pltpu.VMEM((N_EXPERTS, LANES), F32)],
        compiler_params=_params(("arbitrary",)),
        name="mix",
    )(x2, attn, o_f, o_b, z, gates, p2, W["woa"], W["wod"], W["wout"], W["e8"], W["e8t"], W["dnorm"],
      W["ln1g"], W["ln1b"], W["wpg"], W["wpp"], W["rwh"], W["rwl"], W["rb"], W["ustrict"])


def _moe_kernel(blk_e_ref, nvalid_ref, xs_ref, wgu_ref, bgu_ref, wd_ref, bd_ref, y_ref, *, fc):
    i = pl.program_id(0)

    @pl.when(i < nvalid_ref[0])
    def _():
        xs = xs_ref[...]
        acc = None
        for c in range(D_FF // fc):
            lo, hi = c * fc, (c + 1) * fc
            gate = _dot(xs, wgu_ref[0, :, lo:hi]) + bgu_ref[0, :, lo:hi]
            up = _dot(xs, wgu_ref[0, :, D_FF + lo:D_FF + hi]) + bgu_ref[0, :, D_FF + lo:D_FF + hi]
            gate = jnp.minimum(gate, SWIGLU_LIMIT)
            up = jnp.clip(up, -SWIGLU_LIMIT, SWIGLU_LIMIT)
            act = gate * (1.0 / (1.0 + jnp.exp(-SWIGLU_ALPHA * gate))) * (up + 1.0)
            part = _dot(act.astype(BF16), wd_ref[0, lo:hi, :])
            acc = part if acc is None else acc + part
        y_ref[...] = (acc + bd_ref[0]).astype(BF16)

    @pl.when(i >= nvalid_ref[0])
    def _():
        y_ref[...] = jnp.zeros_like(y_ref)


def _moe(xs, blk_e, nvalid, W, bm, fc=256):
    P = xs.shape[0]
    grid_spec = pltpu.PrefetchScalarGridSpec(
        num_scalar_prefetch=2,
        grid=(P // bm,),
        in_specs=[
            pl.BlockSpec((bm, D_MODEL), lambda i, be, nv: (i, 0)),
            pl.BlockSpec((1, D_MODEL, 2 * D_FF), lambda i, be, nv: (be[i], 0, 0)),
            pl.BlockSpec((1, 1, 2 * D_FF), lambda i, be, nv: (be[i], 0, 0)),
            pl.BlockSpec((1, D_FF, D_MODEL), lambda i, be, nv: (be[i], 0, 0)),
            pl.BlockSpec((1, 1, D_MODEL), lambda i, be, nv: (be[i], 0, 0)),
        ],
        out_specs=pl.BlockSpec((bm, D_MODEL), lambda i, be, nv: (i, 0)),
    )
    return pl.pallas_call(
        functools.partial(_moe_kernel, fc=fc),
        grid_spec=grid_spec,
        out_shape=jax.ShapeDtypeStruct((P, D_MODEL), BF16),
        compiler_params=_params(("arbitrary",)),
        name="moe",
    )(blk_e, nvalid, xs, W["wgu"], W["bgu"], W["wd"], W["bd"])


def _final_kernel(r_ref, yg_ref, tg_ref, g_ref, b_ref, y_ref):
    acc = r_ref[...]
    for k in range(TOP_K):
        acc = acc + yg_ref[k].astype(F32) * tg_ref[:, k:k + 1]
    y_ref[...] = _layer_norm(acc, g_ref[...], b_ref[...])


def _final(r, yg, tg, W, tm=512):
    T = r.shape[0]
    tm = min(tm, T)
    row = lambda i: (i, 0)
    return pl.pallas_call(
        _final_kernel,
        grid=(T // tm,),
        in_specs=[pl.BlockSpec((tm, D_MODEL), row),
                  pl.BlockSpec((TOP_K, tm, D_MODEL), lambda i: (0, i, 0)),
                  pl.BlockSpec((tm, TOP_K), row),
                  _const_spec((1, D_MODEL)), _const_spec((1, D_MODEL))],
        out_specs=pl.BlockSpec((tm, D_MODEL), row),
        out_shape=jax.ShapeDtypeStruct((T, D_MODEL), F32),
        compiler_params=_params(("parallel",)),
        name="final_ln",
    )(r, yg, tg, W["ln2g"], W["ln2b"])


def _pad_heads(w, n_heads, width, start, size, dst=0):
    K = w.shape[0]
    w3 = w.reshape(K, n_heads, width)[:, :, start:start + size]
    out = jnp.zeros((K, n_heads, HEAD_PAD), w.dtype)
    out = out.at[:, :, dst:dst + size].set(w3)
    return out.reshape(K, n_heads * HEAD_PAD)


def _prep_weights(w_in, q_a_norm, w_uq, kv_a_norm, w_ukv, w_o_attn, dn_conv, dn_a_log, dn_dt_bias,
                  dn_norm, w_o_dn, w_out, ln1_g, ln1_b, router_w, router_b, w_gate_up, b_gate_up,
                  w_down, b_down, ple_w_proj, ple_w_gate, ln2_g, ln2_b):
    W = {}
    half = ROPE_DIM // 2
    o = 0
    cq = w_in[:, o:o + Q_LORA]; o += Q_LORA
    ckv = w_in[:, o:o + KV_LORA]; o += KV_LORA
    kr = w_in[:, o:o + ROPE_DIM]; o += ROPE_DIM
    nqk = DN_HEADS * DN_DK
    dq = w_in[:, o:o + nqk]; o += nqk
    dk = w_in[:, o:o + nqk]; o += nqk
    dv = w_in[:, o:o + DN_HEADS * DN_DV]; o += DN_HEADS * DN_DV
    dz = w_in[:, o:o + DN_HEADS * DN_DV]; o += DN_HEADS * DN_DV
    da = w_in[:, o:o + 2 * DN_HEADS]; o += 2 * DN_HEADS
    db = w_in[:, o:o + 2 * DN_HEADS]; o += 2 * DN_HEADS
    gate = w_in[:, o:o + 2 * D_MODEL]

    def lane_block(parts):
        w = jnp.concatenate(parts, axis=1)
        return jnp.pad(w, ((0, 0), (0, LANES - w.shape[1])))

    zeros64 = jnp.zeros((D_MODEL, NOPE_DIM), F32)
    kr_blk = lane_block([zeros64, kr])
    krs_blk = lane_block([zeros64, kr[:, half:], kr[:, :half]])
    ab = [lane_block([da[:, d * DN_HEADS:(d + 1) * DN_HEADS], db[:, d * DN_HEADS:(d + 1) * DN_HEADS]])
          for d in range(2)]

    def interleave_vk(v, k):
        lead = v.shape[:-1]
        v3 = v.reshape(lead + (DN_HEADS, DN_DV))
        k3 = k.reshape(lead + (DN_HEADS, DN_DK))
        return jnp.concatenate([v3, k3], axis=-1).reshape(lead + (DN_HEADS * (DN_DV + DN_DK),))

    W["w1"] = jnp.concatenate([cq, ckv, kr_blk, krs_blk, ab[0], ab[1], dq, interleave_vk(dv, dk), dz, gate],
                              axis=1).astype(BF16)
    W["qan"] = q_a_norm.reshape(1, Q_LORA)
    W["kvan"] = kv_a_norm.reshape(1, KV_LORA)

    qw = NOPE_DIM + ROPE_DIM
    wq_nope = _pad_heads(w_uq, MLA_HEADS, qw, 0, NOPE_DIM, 0)
    wq_r1 = _pad_heads(w_uq, MLA_HEADS, qw, NOPE_DIM, half, NOPE_DIM)
    wq_r2 = _pad_heads(w_uq, MLA_HEADS, qw, NOPE_DIM + half, half, NOPE_DIM + half)
    W["wq"] = (wq_nope + wq_r1 + wq_r2).astype(BF16)
    wq_s1 = _pad_heads(w_uq, MLA_HEADS, qw, NOPE_DIM + half, half, NOPE_DIM)
    wq_s2 = _pad_heads(w_uq, MLA_HEADS, qw, NOPE_DIM, half, NOPE_DIM + half)
    W["wqs"] = (wq_s1 + wq_s2).astype(BF16)
    kvw = NOPE_DIM + V_DIM
    W["wk"] = _pad_heads(w_ukv, MLA_HEADS, kvw, 0, NOPE_DIM, 0).astype(BF16)
    W["wv"] = _pad_heads(w_ukv, MLA_HEADS, kvw, NOPE_DIM, V_DIM, 0).astype(BF16)

    neg_a = -jnp.exp(dn_a_log.astype(F32))
    abp = jnp.zeros((8, LANES), F32)
    for d in range(2):
        abp = abp.at[2 * d, :DN_HEADS].set(neg_a[d])
        abp = abp.at[2 * d + 1, :DN_HEADS].set(dn_dt_bias[d].astype(F32))
    W["abp"] = abp

    cw = jnp.concatenate([dn_conv[:, :nqk], interleave_vk(dn_conv[:, 2 * nqk:], dn_conv[:, nqk:2 * nqk])], axis=1)
    W["convw"] = jnp.pad(cw.astype(F32), ((0, 8 - CONV_K), (0, 0)))

    def group_indicator(width, group):
        e = (np.arange(width)[:, None] // group == np.arange(LANES)[None, :]).astype(np.float32)
        return e

    eq = group_indicator(nqk, DN_DK)
    W["eq"] = jnp.asarray(eq, BF16)
    W["eqt"] = jnp.asarray(eq.T, BF16)
    evk = group_indicator(2 * nqk, DN_DK)
    W["evk"] = jnp.asarray(evk, BF16)
    W["evkt"] = jnp.asarray(evk.T, BF16)
    W["e8"] = W["eq"]
    W["e8t"] = W["eqt"]
    r = np.arange(DN_BLOCK)
    W["tri"] = jnp.asarray(np.stack([r[:, None] >= r[None, :], r[:, None] <= r[None, :]]).astype(np.float32))
    W["dn_masks"] = [(jnp.asarray(t), jnp.asarray(g, BF16), jnp.asarray(o, BF16))
                     for t, g, o in (_dn_masks(0), _dn_masks(1))]
    rt = np.arange(MIX_TILE)
    W["ustrict"] = jnp.asarray((rt[:, None] < rt[None, :]).astype(np.float32), BF16)

    woa = w_o_attn.reshape(MLA_HEADS, V_DIM, D_MODEL)
    woa = jnp.pad(woa, ((0, 0), (0, HEAD_PAD - V_DIM), (0, 0)))
    W["woa"] = woa.reshape(MLA_HEADS * HEAD_PAD, D_MODEL).astype(BF16)
    W["wod"] = w_o_dn.astype(BF16)
    W["wout"] = w_out.astype(BF16)
    W["dnorm"] = jnp.tile(dn_norm.astype(F32), DN_HEADS).reshape(1, DN_HEADS * DN_DV)
    W["ln1g"] = ln1_g.reshape(1, D_MODEL)
    W["ln1b"] = ln1_b.reshape(1, D_MODEL)
    W["ln2g"] = ln2_g.reshape(1, D_MODEL)
    W["ln2b"] = ln2_b.reshape(1, D_MODEL)
    W["wpg"] = ple_w_gate.astype(BF16)
    W["wpp"] = ple_w_proj.astype(BF16)
    rwt = router_w.T.astype(F32)
    W["rwh"], W["rwl"] = _split_bf16(rwt)
    W["rb"] = router_b.reshape(N_EXPERTS, 1).astype(F32)
    W["wgu"] = w_gate_up.astype(BF16)
    W["bgu"] = b_gate_up.reshape(N_EXPERTS, 1, 2 * D_FF).astype(F32)
    W["wd"] = w_down.astype(BF16)
    W["bd"] = b_down.reshape(N_EXPERTS, 1, D_MODEL).astype(F32)
    return W


def _rope_tables(S):
    half = ROPE_DIM // 2
    inv = ROPE_THETA ** (-jnp.arange(0, ROPE_DIM, 2, dtype=F32) / ROPE_DIM)
    ang = jnp.arange(S, dtype=F32)[:, None] * inv[None, :]
    cos, sin = jnp.cos(ang), jnp.sin(ang)
    c = (NOPE_DIM + ROPE_DIM) ** -0.5 * math.log2(math.e)
    pad = jnp.zeros((S, HEAD_PAD - NOPE_DIM - ROPE_DIM), F32)
    cos_blk = jnp.concatenate([cos, cos, pad], axis=1)
    sin_blk = jnp.concatenate([-sin, sin, pad], axis=1)
    cosq = jnp.concatenate([jnp.ones((S, NOPE_DIM), F32), cos_blk], axis=1) * c
    sinq = jnp.concatenate([jnp.zeros((S, NOPE_DIM), F32), sin_blk], axis=1) * c
    cosk = jnp.concatenate([jnp.zeros((S, NOPE_DIM), F32), cos_blk], axis=1)
    sink = jnp.concatenate([jnp.zeros((S, NOPE_DIM), F32), sin_blk], axis=1)
    return cosq, sinq, cosk, sink


def _dest_kernel(pstart_ref, ti_ref, rank_ref, dest_ref):
    ti = ti_ref[...]
    dest = rank_ref[...]
    for e in range(N_EXPERTS):
        dest = dest + jnp.where(ti == e, pstart_ref[e], 0)
    dest_ref[...] = dest


def _dest(p_start, top_i, rank, tile=8192):
    T = top_i.shape[1]
    tile = min(tile, T)
    spec = pl.BlockSpec((TOP_K, tile), lambda i, ps: (0, i))
    return pl.pallas_call(
        _dest_kernel,
        grid_spec=pltpu.PrefetchScalarGridSpec(num_scalar_prefetch=1, grid=(T // tile,),
                                               in_specs=[spec, spec], out_specs=spec),
        out_shape=jax.ShapeDtypeStruct((TOP_K, T), jnp.int32),
        compiler_params=_params(("parallel",)),
        name="slot_index",
    )(p_start, top_i, rank)


def _route(top_i, rank, counts, T, bm):
    A = TOP_K * T
    counts = counts.astype(jnp.int32)
    padded = ((counts + bm - 1) // bm) * bm
    p_end = jnp.cumsum(padded)
    p_start = p_end - padded
    dest = _dest(p_start, top_i, rank)
    nblk = A // bm + N_EXPERTS
    P = nblk * bm
    tok = jnp.tile(jnp.arange(T, dtype=jnp.int32), TOP_K)
    slot_tok = jnp.zeros((P,), jnp.int32).at[dest.reshape(A)].set(tok)
    blk_start = jnp.arange(nblk, dtype=jnp.int32) * bm
    blk_e = jnp.minimum(jnp.sum(p_end[None, :] <= blk_start[:, None], axis=1), N_EXPERTS - 1).astype(jnp.int32)
    nvalid = (p_end[-1] // bm).astype(jnp.int32).reshape(1)
    return dest, slot_tok, blk_e, nvalid


def _layer(x, p, W, bm):
    B, S, _ = x.shape
    T = B * S
    x2 = x.reshape(T, D_MODEL)
    p2 = p.reshape(T, PLE_DIM)
    Wl = dict(W)
    Wl["cosq"], Wl["sinq"], Wl["cosk"], Wl["sink"] = _rope_tables(S)

    q, k, v, dn, z, gates, gb = _in_proj(x2, S, Wl)
    attn = _attention(q, k, v, B, S)
    qn, vk, aux = _dn_prep(dn, gb, S, Wl)
    kt = vk.reshape(T, DN_HEADS, 2, DN_DK)[:, :, 1, :].reshape(T, DN_HEADS * DN_DK).T
    auxt = jnp.swapaxes(aux, 1, 2)
    o_f = _deltanet(qn, vk, kt, aux, auxt, B, S, Wl, 0)
    o_b = _deltanet(qn, vk, kt, aux, auxt, B, S, Wl, 1)
    r, hb, top_i, top_g, rank, cnt = _mix(x2, attn, o_f, o_b, z, gates, p2, Wl)

    dest, slot_tok, blk_e, nvalid = _route(top_i, rank, cnt[:, 0], T, bm)
    xs = hb.at[slot_tok].get(mode="promise_in_bounds")
    yb = _moe(xs, blk_e, nvalid, Wl, bm)
    yg = yb.at[dest.reshape(TOP_K * T)].get(mode="promise_in_bounds").reshape(TOP_K, T, D_MODEL)
    y = _final(r, yg, top_g.T, Wl)
    return y.reshape(B, S, D_MODEL)


def kernel(x_prompt, x_sample, p_prompt, p_sample, w_in, q_a_norm, w_uq, kv_a_norm, w_ukv, w_o_attn, dn_conv, dn_a_log, dn_dt_bias, dn_norm, w_o_dn, w_out, ln1_g, ln1_b, router_w, router_b, w_gate_up, b_gate_up, w_down, b_down, ple_w_proj, ple_w_gate, ln2_g, ln2_b):
    y_prompt, y_sample = x_prompt, x_sample
    for l in range(DEPTH):
        W = _prep_weights(w_in[l], q_a_norm[l], w_uq[l], kv_a_norm[l], w_ukv[l], w_o_attn[l], dn_conv[l],
                          dn_a_log[l], dn_dt_bias[l], dn_norm[l], w_o_dn[l], w_out[l], ln1_g[l], ln1_b[l],
                          router_w[l], router_b[l], w_gate_up[l], b_gate_up[l], w_down[l], b_down[l],
                          ple_w_proj[l], ple_w_gate[l], ln2_g[l], ln2_b[l])
        y_prompt = _layer(y_prompt, p_prompt[l], W, bm=MOE_BLOCK)
        y_sample = _layer(y_sample, p_sample[l], W, bm=MOE_BLOCK)
    return (y_prompt, y_sample)
```

```python
import functools
import math

import numpy as np
import jax
import jax.numpy as jnp
from jax import lax
from jax.experimental import pallas as pl
from jax.experimental.pallas import tpu as pltpu
from jax.experimental.pallas import tpu_sc as plsc

D_MODEL = 1024
MLA_HEADS = 8
Q_LORA = 256
KV_LORA = 128
NOPE_DIM = 64
ROPE_DIM = 32
V_DIM = 64
ROPE_THETA = 10000.0
DN_HEADS = 8
DN_DK = 64
DN_DV = 64
CONV_K = 5
N_EXPERTS = 32
TOP_K = 4
D_FF = 1024
SWIGLU_LIMIT = 7.0
SWIGLU_ALPHA = 1.702
PLE_DIM = 256
DEPTH = 1
DEEPNORM_ALPHA = (2.0 * DEPTH) ** 0.25

LANES = 128
HEAD_PAD = 128
DN_BLOCK = 256
MOE_BLOCK = 512
MIX_TILE = 512
VMEM_LIMIT = 56 * 1024 * 1024

_C_CQ = 0
_C_CKV = _C_CQ + Q_LORA
_C_KR = _C_CKV + KV_LORA
_C_KRS = _C_KR + LANES
_C_AB0 = _C_KRS + LANES
_C_AB1 = _C_AB0 + LANES
_C_DNQ = _C_AB1 + LANES
_C_DNVK = _C_DNQ + DN_HEADS * DN_DK
_C_Z = _C_DNVK + DN_HEADS * (DN_DK + DN_DV)
_C_GATE = _C_Z + DN_HEADS * DN_DV
_C_END = _C_GATE + 2 * D_MODEL

BF16 = jnp.bfloat16
F32 = jnp.float32


def _dot(a, b):
    return jnp.dot(a, b, preferred_element_type=F32)


def _dot_nt(a, b):
    return lax.dot_general(a, b, (((1,), (1,)), ((), ())), preferred_element_type=F32)


def _split_bf16(x):
    hi = x.astype(BF16)
    lo = (x - hi.astype(F32)).astype(BF16)
    return hi, lo


_HI_HALFWORD = 0xFFFF0000


def _pack_halves(x):
    w = x.shape[1] // 2
    bits = lax.bitcast_convert_type(x.astype(BF16).astype(F32), jnp.uint32)
    return (bits[:, :w] >> 16) | (bits[:, w:] & jnp.uint32(_HI_HALFWORD))


def _unpack_halves(words):
    lo = lax.bitcast_convert_type(words << 16, F32)
    hi = lax.bitcast_convert_type(words & jnp.uint32(_HI_HALFWORD), F32)
    return lo, hi


def _const_spec(shape):
    n = len(shape)
    return pl.BlockSpec(shape, lambda *_: (0,) * n, pipeline_mode=pl.Buffered(1))


def _params(sem):
    return pltpu.CompilerParams(dimension_semantics=sem, vmem_limit_bytes=VMEM_LIMIT)


def _in_proj_kernel(x_ref, w1_ref, qan_ref, kvan_ref, wq_ref, wqs_ref, wk_ref, wv_ref,
                    cosq_ref, sinq_ref, cosk_ref, sink_ref, abp_ref,
                    q_ref, k_ref, v_ref, dn_ref, z_ref, gates_ref, gb_ref):
    xb = x_ref[...].astype(BF16)

    def proj(lo, hi):
        return _dot(xb, w1_ref[:, lo:hi])

    def rms(c, g):
        return (c * lax.rsqrt(jnp.mean(c * c, axis=-1, keepdims=True) + 1e-6) * g).astype(BF16)

    cqn = rms(proj(_C_CQ, _C_CKV), qan_ref[...])
    qa = _dot(cqn, wq_ref[...])
    qb = _dot(cqn, wqs_ref[...])
    ckvn = rms(proj(_C_CKV, _C_KR), kvan_ref[...])
    kw = _dot(ckvn, wk_ref[...])
    vw = _dot(ckvn, wv_ref[...])
    kr = proj(_C_KR, _C_KRS) * cosk_ref[...] + proj(_C_KRS, _C_AB0) * sink_ref[...]
    cosq = cosq_ref[...]
    sinq = sinq_ref[...]
    lane = lax.broadcasted_iota(jnp.int32, (1, HEAD_PAD), 1)
    ones_col = (lane == V_DIM).astype(F32)
    for h in range(MLA_HEADS):
        sl = slice(h * HEAD_PAD, (h + 1) * HEAD_PAD)
        q_ref[:, sl] = (qa[:, sl] * cosq + qb[:, sl] * sinq).astype(BF16)
        k_ref[:, sl] = (kw[:, sl] + kr).astype(BF16)
        v_ref[:, sl] = (vw[:, sl] + ones_col).astype(BF16)

    for d, c0 in enumerate((_C_AB0, _C_AB1)):
        ab = proj(c0, c0 + LANES)
        neg_a = abp_ref[2 * d:2 * d + 1, :]
        dtb = abp_ref[2 * d + 1:2 * d + 2, :]
        t = ab + dtb
        sp = jnp.maximum(t, 0.0) + jnp.log(1.0 + jnp.exp(-jnp.abs(t)))
        g = neg_a * sp
        beta = 1.0 / (1.0 + jnp.exp(-ab))
        gb_ref[d] = jnp.where(lane < DN_HEADS, g, beta)[:, :2 * DN_HEADS]

    dn_ref[...] = proj(_C_DNQ, _C_Z).astype(BF16)
    z_ref[...] = proj(_C_Z, _C_GATE).astype(BF16)
    gl = proj(_C_GATE, _C_END)
    gates_ref[...] = (1.0 / (1.0 + jnp.exp(-gl))).astype(BF16)


def _in_proj(x2, S, W, tm=512):
    T = x2.shape[0]
    nseq = S // tm
    row = lambda i: (i, 0)
    pos = lambda i: (i % nseq, 0)
    dn_w = _C_Z - _C_DNQ
    return pl.pallas_call(
        _in_proj_kernel,
        grid=(T // tm,),
        in_specs=[
            pl.BlockSpec((tm, D_MODEL), row),
            _const_spec((D_MODEL, _C_END)),
            _const_spec((1, Q_LORA)), _const_spec((1, KV_LORA)),
            _const_spec((Q_LORA, MLA_HEADS * HEAD_PAD)), _const_spec((Q_LORA, MLA_HEADS * HEAD_PAD)),
            _const_spec((KV_LORA, MLA_HEADS * HEAD_PAD)), _const_spec((KV_LORA, MLA_HEADS * HEAD_PAD)),
            pl.BlockSpec((tm, HEAD_PAD), pos), pl.BlockSpec((tm, HEAD_PAD), pos),
            pl.BlockSpec((tm, HEAD_PAD), pos), pl.BlockSpec((tm, HEAD_PAD), pos),
            _const_spec((8, LANES)),
        ],
        out_specs=[
            pl.BlockSpec((tm, MLA_HEADS * HEAD_PAD), row),
            pl.BlockSpec((tm, MLA_HEADS * HEAD_PAD), row),
            pl.BlockSpec((tm, MLA_HEADS * HEAD_PAD), row),
            pl.BlockSpec((tm, dn_w), row),
            pl.BlockSpec((tm, DN_HEADS * DN_DV), row),
            pl.BlockSpec((tm, 2 * D_MODEL), row),
            pl.BlockSpec((2, tm, 2 * DN_HEADS), lambda i: (0, i, 0)),
        ],
        out_shape=[
            jax.ShapeDtypeStruct((T, MLA_HEADS * HEAD_PAD), BF16),
            jax.ShapeDtypeStruct((T, MLA_HEADS * HEAD_PAD), BF16),
            jax.ShapeDtypeStruct((T, MLA_HEADS * HEAD_PAD), BF16),
            jax.ShapeDtypeStruct((T, dn_w), BF16),
            jax.ShapeDtypeStruct((T, DN_HEADS * DN_DV), BF16),
            jax.ShapeDtypeStruct((T, 2 * D_MODEL), BF16),
            jax.ShapeDtypeStruct((2, T, 2 * DN_HEADS), F32),
        ],
        compiler_params=_params(("parallel",)),
        name="in_proj",
    )(x2, W["w1"], W["qan"], W["kvan"], W["wq"], W["wqs"], W["wk"], W["wv"],
      W["cosq"], W["sinq"], W["cosk"], W["sink"], W["abp"])


def _attn_kernel(q_ref, k_ref, v_ref, o_ref, *, tk):
    tq = q_ref.shape[0]
    S = k_ref.shape[0]
    q = q_ref[...]

    def body(j, carry):
        m, acc = carry
        off = pl.multiple_of(j * tk, tk)
        s = _dot_nt(q, k_ref[pl.ds(off, tk), :])
        m_new = jnp.maximum(m, jnp.max(s, axis=-1, keepdims=True))
        p = jnp.exp2(s - m_new).astype(BF16)
        acc = acc * jnp.exp2(m - m_new) + _dot(p, v_ref[pl.ds(off, tk), :])
        return m_new, acc

    m0 = jnp.full((tq, 1), -1e30, F32)
    acc0 = jnp.zeros((tq, HEAD_PAD), F32)
    _, acc = lax.fori_loop(0, S // tk, body, (m0, acc0), unroll=4)
    o_ref[...] = (acc / acc[:, V_DIM:V_DIM + 1]).astype(BF16)


def _attention(q, k, v, B, S, tq=512, tk=1024):
    T = q.shape[0]
    nq = S // tq
    tk = min(tk, S)
    return pl.pallas_call(
        functools.partial(_attn_kernel, tk=tk),
        grid=(B, MLA_HEADS, nq),
        in_specs=[
            pl.BlockSpec((tq, HEAD_PAD), lambda b, h, i: (b * nq + i, h)),
            pl.BlockSpec((S, HEAD_PAD), lambda b, h, i: (b, h)),
            pl.BlockSpec((S, HEAD_PAD), lambda b, h, i: (b, h)),
        ],
        out_specs=pl.BlockSpec((tq, HEAD_PAD), lambda b, h, i: (b * nq + i, h)),
        out_shape=jax.ShapeDtypeStruct((T, MLA_HEADS * HEAD_PAD), BF16),
        compiler_params=_params(("parallel", "parallel", "arbitrary")),
        name="attention",
    )(q, k, v)


_HALO = 16


def _dn_prep_kernel(x_ref, prev_ref, next_ref, cw_ref, eq_ref, eqt_ref, evk_ref, evkt_ref,
                    gb_ref, tri_ref, qn_ref, vk_ref, aux_ref, *, nseq):
    i = pl.program_id(0)
    tp = x_ref.shape[0]
    first = (i % nseq) == 0
    last = (i % nseq) == nseq - 1
    prev = jnp.where(first, 0.0, prev_ref[...].astype(F32))
    nxt = jnp.where(last, 0.0, next_ref[...].astype(F32))
    xe = jnp.concatenate([prev, x_ref[...].astype(F32), nxt], axis=0)
    n = tp + 2 * _HALO
    y = None
    for j in range(CONV_K):
        shift = (CONV_K // 2 - j) % n
        xs = xe if shift == 0 else pltpu.roll(xe, shift, axis=0)
        term = xs[_HALO:_HALO + tp, :] * cw_ref[j:j + 1, :]
        y = term if y is None else y + term
    y = y * (1.0 / (1.0 + jnp.exp(-y)))

    def group_scale(v, e_ref, et_ref):
        sq = v * v
        hi, lo = _split_bf16(sq)
        ss = _dot(hi, e_ref[...]) + _dot(lo, e_ref[...])
        inv = lax.rsqrt(ss + 1e-6)
        ih, il = _split_bf16(inv)
        return _dot(ih, et_ref[...]) + _dot(il, et_ref[...])

    nq = DN_HEADS * DN_DK
    yq = y[:, :nq]
    qn_ref[...] = (yq * group_scale(yq, eq_ref, eqt_ref) * (DN_DK ** -0.5)).astype(BF16)
    yvk = y[:, nq:]
    sc = group_scale(yvk, evk_ref, evkt_ref)
    lane = lax.broadcasted_iota(jnp.int32, (1, yvk.shape[1]), 1)
    is_k = (lane // DN_DV) % 2 == 1
    vk_ref[...] = (yvk * jnp.where(is_k, sc, 1.0)).astype(BF16)

    hcol = lax.broadcasted_iota(jnp.int32, (1, 2 * DN_HEADS), 1) < DN_HEADS
    for d in range(2):
        gb = gb_ref[d]
        for blk in range(tp // DN_BLOCK):
            rs = slice(blk * DN_BLOCK, (blk + 1) * DN_BLOCK)
            g = gb[rs]
            cs = jnp.dot(tri_ref[d], g, preferred_element_type=F32, precision=lax.Precision.HIGHEST)
            aux_ref[d, rs, :] = jnp.where(hcol, cs, g)


def _dn_prep(dn, gb, S, W, tp=256):
    T = dn.shape[0]
    nseq = S // tp
    hb = tp // _HALO
    nh = T // _HALO
    dn_w = dn.shape[1]
    nq = DN_HEADS * DN_DK
    return pl.pallas_call(
        functools.partial(_dn_prep_kernel, nseq=nseq),
        grid=(T // tp,),
        in_specs=[
            pl.BlockSpec((tp, dn_w), lambda i: (i, 0)),
            pl.BlockSpec((_HALO, dn_w), lambda i: (jnp.maximum(i * hb - 1, 0), 0)),
            pl.BlockSpec((_HALO, dn_w), lambda i: (jnp.minimum((i + 1) * hb, nh - 1), 0)),
            _const_spec((8, dn_w)),
            _const_spec((nq, LANES)), _const_spec((LANES, nq)),
            _const_spec((dn_w - nq, LANES)), _const_spec((LANES, dn_w - nq)),
            pl.BlockSpec((2, tp, 2 * DN_HEADS), lambda i: (0, i, 0)),
            _const_spec((2, DN_BLOCK, DN_BLOCK)),
        ],
        out_specs=[
            pl.BlockSpec((tp, nq), lambda i: (i, 0)),
            pl.BlockSpec((tp, dn_w - nq), lambda i: (i, 0)),
            pl.BlockSpec((2, tp, 2 * DN_HEADS), lambda i: (0, i, 0)),
        ],
        out_shape=[
            jax.ShapeDtypeStruct((T, nq), BF16),
            jax.ShapeDtypeStruct((T, dn_w - nq), BF16),
            jax.ShapeDtypeStruct((2, T, 2 * DN_HEADS), F32),
        ],
        compiler_params=_params(("parallel",)),
        name="dn_prep",
    )(dn, dn, dn, W["convw"], W["eq"], W["eqt"], W["evk"], W["evkt"], gb, W["tri"])


_MERGE_SIZES = (16, 32, 64, 128)


def _active_blocks(d, s):
    return range(1 - d, DN_BLOCK // s, 2)


def _dn_masks(d):
    r = np.arange(DN_BLOCK)[:, None]
    c = np.arange(DN_BLOCK)[None, :]
    rr, cc = (r, c) if d == 0 else (c, r)
    tri = np.stack([rr >= cc, rr > cc]).astype(np.float32)
    diag16 = ((r // 16) == (c // 16)).astype(np.float32)
    offc = []
    for s in _MERGE_SIZES:
        full = ((rr // s) % 2 == 1) & ((rr // s) == (cc // s) + 1)
        rows = np.concatenate([np.arange(b * s, (b + 1) * s) for b in _active_blocks(d, s)])
        assert not np.delete(full, rows, axis=0).any()
        offc.append(full[rows].astype(np.float32))
    return tri, diag16, np.stack(offc)


def _deltanet_kernel(qn_ref, vk_ref, kt_ref, aux_ref, auxt_ref, tri_ref, diag16_ref, offc_ref, o_ref,
                     s_ref, nm_ref, p_ref, x_ref, aqk_ref, *, d):
    i = pl.program_id(1)
    C = DN_BLOCK
    heads = range(DN_HEADS)

    @pl.when(i == 0)
    def _():
        s_ref[...] = jnp.zeros_like(s_ref)

    rowi = lax.broadcasted_iota(jnp.int32, (C, C), 0)
    coli = lax.broadcasted_iota(jnp.int32, (C, C), 1)
    eye = (rowi == coli).astype(F32)
    lane = lax.broadcasted_iota(jnp.int32, (1, LANES), 1)
    r64 = lax.broadcasted_iota(jnp.int32, (DN_DK, LANES), 0)
    c64 = lax.broadcasted_iota(jnp.int32, (DN_DK, LANES), 1)
    zeros_kt = jnp.zeros((DN_DK, C), BF16)
    zeros_s = jnp.zeros((DN_DK, LANES), F32)

    def q_pair(h):
        return qn_ref[:, (h // 2) * LANES:(h // 2 + 1) * LANES]

    def vk_head(h):
        return vk_ref[:, h * LANES:(h + 1) * LANES]

    def kt_head(h):
        return kt_ref[h * DN_DK:(h + 1) * DN_DK, :]

    def gc_col(h):
        return aux_ref[0, :, h:h + 1]

    def beta_col(h):
        return aux_ref[0, :, DN_HEADS + h:DN_HEADS + h + 1]

    def gc_row(h):
        return auxt_ref[0, h:h + 1, :]

    for h in heads:
        kt = kt_head(h)
        kt_for_q = jnp.concatenate([kt, zeros_kt] if h % 2 == 0 else [zeros_kt, kt], axis=0)
        kt_for_k = jnp.concatenate([zeros_kt, kt], axis=0)
        qk = _dot(q_pair(h), kt_for_q)
        kk = _dot(vk_head(h), kt_for_k)
        e0 = jnp.exp(jnp.minimum(gc_col(h) - gc_row(h), 0.0))
        aqk_ref[h] = (qk * e0 * tri_ref[0]).astype(BF16)
        nm_ref[h] = (-(kk * e0 * tri_ref[1]) * beta_col(h)).astype(BF16)

    diag16 = diag16_ref[...]
    for h in heads:
        nd = nm_ref[h] * diag16
        p_ref[h] = (eye + nd.astype(F32)).astype(BF16)
        x_ref[h] = _dot(nd, nd).astype(BF16)
    for it in range(3):
        for h in heads:
            pb = p_ref[h]
            sqb = x_ref[h]
            p_ref[h] = (pb.astype(F32) + _dot(pb, sqb)).astype(BF16)
            if it < 2:
                x_ref[h] = _dot(sqb, sqb).astype(BF16)
    for k, s in enumerate(_MERGE_SIZES):
        blocks = list(_active_blocks(d, s))
        zeros_blk = jnp.zeros((s, C), BF16)

        def active_rows(ref, h):
            return jnp.concatenate([ref[h, b * s:(b + 1) * s, :] for b in blocks], axis=0)

        for h in heads:
            xc = _dot(active_rows(nm_ref, h) * offc_ref[k], p_ref[h]).astype(BF16)
            pieces = []
            for j in range(len(blocks)):
                piece = xc[j * s:(j + 1) * s, :]
                pieces += [zeros_blk, piece] if d == 0 else [piece, zeros_blk]
            x_ref[h] = jnp.concatenate(pieces, axis=0)
        for h in heads:
            pr = active_rows(p_ref, h)
            prn = (pr.astype(F32) + _dot(pr, x_ref[h])).astype(BF16)
            for j, b in enumerate(blocks):
                p_ref[h, b * s:(b + 1) * s, :] = prn[j * s:(j + 1) * s, :]

    o_pair = None
    for h in heads:
        par = h % 2
        gc_c = gc_col(h)
        gc_r = gc_row(h)
        g_tot = gc_r[:, C - 1:C] if d == 0 else gc_r[:, 0:1]
        egc = jnp.exp(gc_c)
        rhs = (vk_head(h).astype(F32) * beta_col(h) * jnp.where(lane < DN_DV, 1.0, egc)).astype(BF16)
        uw = _dot(p_ref[h], rhs).astype(BF16)
        s_pl = s_ref[h]
        eye_pl = (c64 == r64 + par * DN_DV).astype(F32)
        s_aug = jnp.concatenate([eye_pl, -s_pl], axis=0).astype(BF16)
        v_new = _dot(uw, s_aug).astype(BF16)
        s_sel = jnp.concatenate([s_pl, zeros_s] if par == 0 else [zeros_s, s_pl], axis=0).astype(BF16)
        qd = (q_pair(h).astype(F32) * egc).astype(BF16)
        o_pl = _dot(qd, s_sel) + _dot(aqk_ref[h], v_new)
        kd = (kt_head(h).astype(F32) * jnp.exp(g_tot - gc_r)).astype(BF16)
        s_ref[h] = s_pl * jnp.exp(g_tot) + _dot(kd, v_new)
        if par == 0:
            o_pair = o_pl
        else:
            o_ref[:, (h // 2) * LANES:(h // 2 + 1) * LANES] = o_pair + o_pl


def _deltanet(qn, vk, kt, aux, auxt, B, S, W, d):
    T = qn.shape[0]
    C = DN_BLOCK
    nb = S // C

    def blk(b, i):
        return b * nb + (i if d == 0 else nb - 1 - i)

    nq = DN_HEADS * DN_DK
    tri, diag16, offc = W["dn_masks"][d]
    return pl.pallas_call(
        functools.partial(_deltanet_kernel, d=d),
        grid=(B, nb),
        in_specs=[
            pl.BlockSpec((C, nq), lambda b, i: (blk(b, i), 0)),
            pl.BlockSpec((C, 2 * nq), lambda b, i: (blk(b, i), 0)),
            pl.BlockSpec((nq, C), lambda b, i: (0, blk(b, i))),
            pl.BlockSpec((1, C, 2 * DN_HEADS), lambda b, i: (d, blk(b, i), 0)),
            pl.BlockSpec((1, 2 * DN_HEADS, C), lambda b, i: (d, 0, blk(b, i))),
            _const_spec((2, C, C)), _const_spec((C, C)), _const_spec((len(_MERGE_SIZES), C // 2, C)),
        ],
        out_specs=pl.BlockSpec((C, DN_HEADS * DN_DV), lambda b, i: (blk(b, i), 0)),
        out_shape=jax.ShapeDtypeStruct((T, DN_HEADS * DN_DV), F32),
        scratch_shapes=[pltpu.VMEM((DN_HEADS, DN_DK, LANES), F32)]
        + [pltpu.VMEM((DN_HEADS, C, C), BF16)] * 4,
        compiler_params=_params(("parallel", "arbitrary")),
        name="deltanet_fwd" if d == 0 else "deltanet_bwd",
    )(qn, vk, kt, aux, auxt, tri, diag16, offc)


def _layer_norm(v, g, b):
    mu = jnp.mean(v, axis=-1, keepdims=True)
    c = v - mu
    var = jnp.mean(c * c, axis=-1, keepdims=True)
    return c * lax.rsqrt(var + 1e-5) * g + b


def _mix_kernel(x_ref, attn_ref, of_ref, ob_ref, z_ref, gates_ref, p_ref,
                woa_ref, wod_ref, wout_ref, e8_ref, e8t_ref, dnorm_ref, ln1g_ref, ln1b_ref,
                wpg_ref, wpp_ref, rwh_ref, rwl_ref, rb_ref, ustrict_ref,
                r_ref, hb_ref, ti_ref, tg_ref, rank_ref, cnt_ref, run_ref):
    @pl.when(pl.program_id(0) == 0)
    def _():
        run_ref[...] = jnp.zeros_like(run_ref)

    oa = _dot(attn_ref[...], woa_ref[...])
    o = of_ref[...] + ob_ref[...]
    hi, lo = _split_bf16(o * o)
    ms = (_dot(hi, e8_ref[...]) + _dot(lo, e8_ref[...])) * (1.0 / DN_DV)
    ih, il = _split_bf16(lax.rsqrt(ms + 1e-6))
    sc = _dot(ih, e8t_ref[...]) + _dot(il, e8t_ref[...])
    zf = z_ref[...].astype(F32)
    od_in = o * sc * dnorm_ref[...] * (zf * (1.0 / (1.0 + jnp.exp(-zf))))
    od = _dot(od_in.astype(BF16), wod_ref[...])
    mix = gates_ref[:, :D_MODEL].astype(F32) * oa + gates_ref[:, D_MODEL:].astype(F32) * od
    mo = _dot(mix.astype(BF16), wout_ref[...])
    h = _layer_norm(DEEPNORM_ALPHA * x_ref[...] + mo, ln1g_ref[...], ln1b_ref[...])
    hb = h.astype(BF16)
    hb_ref[...] = _pack_halves(h)
    pg = _dot(hb, wpg_ref[...])
    pp = _dot(p_ref[...].astype(BF16), wpp_ref[...])
    r_ref[...] = DEEPNORM_ALPHA * h + pp * (1.0 / (1.0 + jnp.exp(-pg)))

    hl = (h - hb.astype(F32)).astype(BF16)
    logits = (_dot_nt(rwh_ref[...], hb) + _dot_nt(rwh_ref[...], hl)
              + _dot_nt(rwl_ref[...], hb) + rb_ref[...])
    eid = lax.broadcasted_iota(jnp.int32, logits.shape, 0)
    vals = []
    run = run_ref[:, 0:1]
    for k in range(TOP_K):
        m = jnp.max(logits, axis=0, keepdims=True)
        idx = jnp.min(jnp.where(logits == m, eid, N_EXPERTS), axis=0, keepdims=True)
        ti_ref[k:k + 1, :] = idx
        vals.append(m)
        hit = eid == idx
        logits = jnp.where(hit, -jnp.inf, logits)
        onehot = hit.astype(F32)
        earlier = _dot(onehot.astype(BF16), ustrict_ref[...])
        rank = jnp.sum(onehot * (run + earlier), axis=0, keepdims=True)
        rank_ref[k:k + 1, :] = rank.astype(jnp.int32)
        run = run + jnp.sum(onehot, axis=1, keepdims=True)
    run_ref[...] = jnp.broadcast_to(run, run_ref.shape)
    cnt_ref[...] = jnp.broadcast_to(run, cnt_ref.shape)
    es = [jnp.exp(v - vals[0]) for v in vals]
    den = es[0] + es[1] + es[2] + es[3]
    for k in range(TOP_K):
        tg_ref[k:k + 1, :] = es[k] / den


def _mix(x2, attn, o_f, o_b, z, gates, p2, W):
    tm = MIX_TILE
    T = x2.shape[0]
    row = lambda i: (i, 0)
    nd = DN_HEADS * DN_DV
    return pl.pallas_call(
        _mix_kernel,
        grid=(T // tm,),
        in_specs=[
            pl.BlockSpec((tm, D_MODEL), row),
            pl.BlockSpec((tm, MLA_HEADS * HEAD_PAD), row),
            pl.BlockSpec((tm, nd), row),
            pl.BlockSpec((tm, nd), row),
            pl.BlockSpec((tm, nd), row),
            pl.BlockSpec((tm, 2 * D_MODEL), row),
            pl.BlockSpec((tm, PLE_DIM), row),
            _const_spec((MLA_HEADS * HEAD_PAD, D_MODEL)), _const_spec((nd, D_MODEL)),
            _const_spec((D_MODEL, D_MODEL)),
            _const_spec((nd, LANES)), _const_spec((LANES, nd)), _const_spec((1, nd)),
            _const_spec((1, D_MODEL)), _const_spec((1, D_MODEL)),
            _const_spec((D_MODEL, D_MODEL)), _const_spec((PLE_DIM, D_MODEL)),
            _const_spec((N_EXPERTS, D_MODEL)), _const_spec((N_EXPERTS, D_MODEL)),
            _const_spec((N_EXPERTS, 1)),
            _const_spec((tm, tm)),
        ],
        out_specs=[
            pl.BlockSpec((tm, D_MODEL), row),
            pl.BlockSpec((tm, D_MODEL // 2), row),
            pl.BlockSpec((TOP_K, tm), lambda i: (0, i)),
            pl.BlockSpec((TOP_K, tm), lambda i: (0, i)),
            pl.BlockSpec((TOP_K, tm), lambda i: (0, i)),
            _const_spec((N_EXPERTS, LANES)),
        ],
        out_shape=[
            jax.ShapeDtypeStruct((T, D_MODEL), F32),
            jax.ShapeDtypeStruct((T, D_MODEL // 2), jnp.uint32),
            jax.ShapeDtypeStruct((TOP_K, T), jnp.int32),
            jax.ShapeDtypeStruct((TOP_K, T), F32),
            jax.ShapeDtypeStruct((TOP_K, T), jnp.int32),
            jax.ShapeDtypeStruct((N_EXPERTS, LANES), F32),
        ],
        scratch_shapes=[pltpu.VMEM((N_EXPERTS, LANES), F32)],
        compiler_params=_params(("arbitrary",)),
        name="mix",
    )(x2, attn, o_f, o_b, z, gates, p2, W["woa"], W["wod"], W["wout"], W["e8"], W["e8t"], W["dnorm"],
      W["ln1g"], W["ln1b"], W["wpg"], W["wpp"], W["rwh"], W["rwl"], W["rb"], W["ustrict"])


def _moe_kernel(blk_e_ref, nvalid_ref, xs_ref, wgu_ref, bgu_ref, wd_ref, bd_ref, y_ref, *, fc):
    i = pl.program_id(0)

    @pl.when(i < nvalid_ref[0])
    def _():
        xs = jnp.concatenate(_unpack_halves(xs_ref[...]), axis=1).astype(BF16)
        acc = None
        for c in range(D_FF // fc):
            lo, hi = c * fc, (c + 1) * fc
            gate = _dot(xs, wgu_ref[0, :, lo:hi]) + bgu_ref[0, :, lo:hi]
            up = _dot(xs, wgu_ref[0, :, D_FF + lo:D_FF + hi]) + bgu_ref[0, :, D_FF + lo:D_FF + hi]
            gate = jnp.minimum(gate, SWIGLU_LIMIT)
            up = jnp.clip(up, -SWIGLU_LIMIT, SWIGLU_LIMIT)
            act = gate * (1.0 / (1.0 + jnp.exp(-SWIGLU_ALPHA * gate))) * (up + 1.0)
            part = _dot(act.astype(BF16), wd_ref[0, lo:hi, :])
            acc = part if acc is None else acc + part
        y_ref[...] = _pack_halves(acc + bd_ref[0])

    @pl.when(i >= nvalid_ref[0])
    def _():
        y_ref[...] = jnp.zeros_like(y_ref)


def _moe(xs, blk_e, nvalid, W, bm, fc=256):
    P = xs.shape[0]
    grid_spec = pltpu.PrefetchScalarGridSpec(
        num_scalar_prefetch=2,
        grid=(P // bm,),
        in_specs=[
            pl.BlockSpec((bm, D_MODEL // 2), lambda i, be, nv: (i, 0)),
            pl.BlockSpec((1, D_MODEL, 2 * D_FF), lambda i, be, nv: (be[i], 0, 0)),
            pl.BlockSpec((1, 1, 2 * D_FF), lambda i, be, nv: (be[i], 0, 0)),
            pl.BlockSpec((1, D_FF, D_MODEL), lambda i, be, nv: (be[i], 0, 0)),
            pl.BlockSpec((1, 1, D_MODEL), lambda i, be, nv: (be[i], 0, 0)),
        ],
        out_specs=pl.BlockSpec((bm, D_MODEL // 2), lambda i, be, nv: (i, 0)),
    )
    return pl.pallas_call(
        functools.partial(_moe_kernel, fc=fc),
        grid_spec=grid_spec,
        out_shape=jax.ShapeDtypeStruct((P, D_MODEL // 2), jnp.uint32),
        compiler_params=_params(("arbitrary",)),
        name="moe",
    )(blk_e, nvalid, xs, W["wgu"], W["bgu"], W["wd"], W["bd"])


def _final_kernel(r_ref, yg_ref, tg_ref, g_ref, b_ref, y_ref):
    ffn_lo = ffn_hi = None
    for k in range(TOP_K):
        lo, hi = _unpack_halves(yg_ref[k])
        g = tg_ref[:, k:k + 1]
        ffn_lo = lo * g if ffn_lo is None else ffn_lo + lo * g
        ffn_hi = hi * g if ffn_hi is None else ffn_hi + hi * g
    acc = r_ref[...] + jnp.concatenate([ffn_lo, ffn_hi], axis=1)
    y_ref[...] = _layer_norm(acc, g_ref[...], b_ref[...])


def _final(r, yg, tg, W, tm=512):
    T = r.shape[0]
    tm = min(tm, T)
    row = lambda i: (i, 0)
    return pl.pallas_call(
        _final_kernel,
        grid=(T // tm,),
        in_specs=[pl.BlockSpec((tm, D_MODEL), row),
                  pl.BlockSpec((TOP_K, tm, D_MODEL // 2), lambda i: (0, i, 0)),
                  pl.BlockSpec((tm, TOP_K), row),
                  _const_spec((1, D_MODEL)), _const_spec((1, D_MODEL))],
        out_specs=pl.BlockSpec((tm, D_MODEL), row),
        out_shape=jax.ShapeDtypeStruct((T, D_MODEL), F32),
        compiler_params=_params(("parallel",)),
        name="final_ln",
    )(r, yg, tg, W["ln2g"], W["ln2b"])


def _pad_heads(w, n_heads, width, start, size, dst=0):
    K = w.shape[0]
    w3 = w.reshape(K, n_heads, width)[:, :, start:start + size]
    out = jnp.zeros((K, n_heads, HEAD_PAD), w.dtype)
    out = out.at[:, :, dst:dst + size].set(w3)
    return out.reshape(K, n_heads * HEAD_PAD)


def _prep_weights(w_in, q_a_norm, w_uq, kv_a_norm, w_ukv, w_o_attn, dn_conv, dn_a_log, dn_dt_bias,
                  dn_norm, w_o_dn, w_out, ln1_g, ln1_b, router_w, router_b, w_gate_up, b_gate_up,
                  w_down, b_down, ple_w_proj, ple_w_gate, ln2_g, ln2_b):
    W = {}
    half = ROPE_DIM // 2
    o = 0
    cq = w_in[:, o:o + Q_LORA]; o += Q_LORA
    ckv = w_in[:, o:o + KV_LORA]; o += KV_LORA
    kr = w_in[:, o:o + ROPE_DIM]; o += ROPE_DIM
    nqk = DN_HEADS * DN_DK
    dq = w_in[:, o:o + nqk]; o += nqk
    dk = w_in[:, o:o + nqk]; o += nqk
    dv = w_in[:, o:o + DN_HEADS * DN_DV]; o += DN_HEADS * DN_DV
    dz = w_in[:, o:o + DN_HEADS * DN_DV]; o += DN_HEADS * DN_DV
    da = w_in[:, o:o + 2 * DN_HEADS]; o += 2 * DN_HEADS
    db = w_in[:, o:o + 2 * DN_HEADS]; o += 2 * DN_HEADS
    gate = w_in[:, o:o + 2 * D_MODEL]

    def lane_block(parts):
        w = jnp.concatenate(parts, axis=1)
        return jnp.pad(w, ((0, 0), (0, LANES - w.shape[1])))

    zeros64 = jnp.zeros((D_MODEL, NOPE_DIM), F32)
    kr_blk = lane_block([zeros64, kr])
    krs_blk = lane_block([zeros64, kr[:, half:], kr[:, :half]])
    ab = [lane_block([da[:, d * DN_HEADS:(d + 1) * DN_HEADS], db[:, d * DN_HEADS:(d + 1) * DN_HEADS]])
          for d in range(2)]

    def interleave_vk(v, k):
        lead = v.shape[:-1]
        v3 = v.reshape(lead + (DN_HEADS, DN_DV))
        k3 = k.reshape(lead + (DN_HEADS, DN_DK))
        return jnp.concatenate([v3, k3], axis=-1).reshape(lead + (DN_HEADS * (DN_DV + DN_DK),))

    W["w1"] = jnp.concatenate([cq, ckv, kr_blk, krs_blk, ab[0], ab[1], dq, interleave_vk(dv, dk), dz, gate],
                              axis=1).astype(BF16)
    W["qan"] = q_a_norm.reshape(1, Q_LORA)
    W["kvan"] = kv_a_norm.reshape(1, KV_LORA)

    qw = NOPE_DIM + ROPE_DIM
    wq_nope = _pad_heads(w_uq, MLA_HEADS, qw, 0, NOPE_DIM, 0)
    wq_r1 = _pad_heads(w_uq, MLA_HEADS, qw, NOPE_DIM, half, NOPE_DIM)
    wq_r2 = _pad_heads(w_uq, MLA_HEADS, qw, NOPE_DIM + half, half, NOPE_DIM + half)
    W["wq"] = (wq_nope + wq_r1 + wq_r2).astype(BF16)
    wq_s1 = _pad_heads(w_uq, MLA_HEADS, qw, NOPE_DIM + half, half, NOPE_DIM)
    wq_s2 = _pad_heads(w_uq, MLA_HEADS, qw, NOPE_DIM, half, NOPE_DIM + half)
    W["wqs"] = (wq_s1 + wq_s2).astype(BF16)
    kvw = NOPE_DIM + V_DIM
    W["wk"] = _pad_heads(w_ukv, MLA_HEADS, kvw, 0, NOPE_DIM, 0).astype(BF16)
    W["wv"] = _pad_heads(w_ukv, MLA_HEADS, kvw, NOPE_DIM, V_DIM, 0).astype(BF16)

    neg_a = -jnp.exp(dn_a_log.astype(F32))
    abp = jnp.zeros((8, LANES), F32)
    for d in range(2):
        abp = abp.at[2 * d, :DN_HEADS].set(neg_a[d])
        abp = abp.at[2 * d + 1, :DN_HEADS].set(dn_dt_bias[d].astype(F32))
    W["abp"] = abp

    cw = jnp.concatenate([dn_conv[:, :nqk], interleave_vk(dn_conv[:, 2 * nqk:], dn_conv[:, nqk:2 * nqk])], axis=1)
    W["convw"] = jnp.pad(cw.astype(F32), ((0, 8 - CONV_K), (0, 0)))

    def group_indicator(width, group):
        e = (np.arange(width)[:, None] // group == np.arange(LANES)[None, :]).astype(np.float32)
        return e

    eq = group_indicator(nqk, DN_DK)
    W["eq"] = jnp.asarray(eq, BF16)
    W["eqt"] = jnp.asarray(eq.T, BF16)
    evk = group_indicator(2 * nqk, DN_DK)
    W["evk"] = jnp.asarray(evk, BF16)
    W["evkt"] = jnp.asarray(evk.T, BF16)
    W["e8"] = W["eq"]
    W["e8t"] = W["eqt"]
    r = np.arange(DN_BLOCK)
    W["tri"] = jnp.asarray(np.stack([r[:, None] >= r[None, :], r[:, None] <= r[None, :]]).astype(np.float32))
    W["dn_masks"] = [(jnp.asarray(t), jnp.asarray(g, BF16), jnp.asarray(o, BF16))
                     for t, g, o in (_dn_masks(0), _dn_masks(1))]
    rt = np.arange(MIX_TILE)
    W["ustrict"] = jnp.asarray((rt[:, None] < rt[None, :]).astype(np.float32), BF16)

    woa = w_o_attn.reshape(MLA_HEADS, V_DIM, D_MODEL)
    woa = jnp.pad(woa, ((0, 0), (0, HEAD_PAD - V_DIM), (0, 0)))
    W["woa"] = woa.reshape(MLA_HEADS * HEAD_PAD, D_MODEL).astype(BF16)
    W["wod"] = w_o_dn.astype(BF16)
    W["wout"] = w_out.astype(BF16)
    W["dnorm"] = jnp.tile(dn_norm.astype(F32), DN_HEADS).reshape(1, DN_HEADS * DN_DV)
    W["ln1g"] = ln1_g.reshape(1, D_MODEL)
    W["ln1b"] = ln1_b.reshape(1, D_MODEL)
    W["ln2g"] = ln2_g.reshape(1, D_MODEL)
    W["ln2b"] = ln2_b.reshape(1, D_MODEL)
    W["wpg"] = ple_w_gate.astype(BF16)
    W["wpp"] = ple_w_proj.astype(BF16)
    rwt = router_w.T.astype(F32)
    W["rwh"], W["rwl"] = _split_bf16(rwt)
    W["rb"] = router_b.reshape(N_EXPERTS, 1).astype(F32)
    W["wgu"] = w_gate_up.astype(BF16)
    W["bgu"] = b_gate_up.reshape(N_EXPERTS, 1, 2 * D_FF).astype(F32)
    W["wd"] = w_down.astype(BF16)
    W["bd"] = b_down.reshape(N_EXPERTS, 1, D_MODEL).astype(F32)
    return W


def _rope_tables(S):
    half = ROPE_DIM // 2
    inv = ROPE_THETA ** (-jnp.arange(0, ROPE_DIM, 2, dtype=F32) / ROPE_DIM)
    ang = jnp.arange(S, dtype=F32)[:, None] * inv[None, :]
    cos, sin = jnp.cos(ang), jnp.sin(ang)
    c = (NOPE_DIM + ROPE_DIM) ** -0.5 * math.log2(math.e)
    pad = jnp.zeros((S, HEAD_PAD - NOPE_DIM - ROPE_DIM), F32)
    cos_blk = jnp.concatenate([cos, cos, pad], axis=1)
    sin_blk = jnp.concatenate([-sin, sin, pad], axis=1)
    cosq = jnp.concatenate([jnp.ones((S, NOPE_DIM), F32), cos_blk], axis=1) * c
    sinq = jnp.concatenate([jnp.zeros((S, NOPE_DIM), F32), sin_blk], axis=1) * c
    cosk = jnp.concatenate([jnp.zeros((S, NOPE_DIM), F32), cos_blk], axis=1)
    sink = jnp.concatenate([jnp.zeros((S, NOPE_DIM), F32), sin_blk], axis=1)
    return cosq, sinq, cosk, sink


def _dest_kernel(pstart_ref, ti_ref, rank_ref, dest_ref):
    ti = ti_ref[...]
    dest = rank_ref[...]
    for e in range(N_EXPERTS):
        dest = dest + jnp.where(ti == e, pstart_ref[e], 0)
    dest_ref[...] = dest


def _dest(p_start, top_i, rank, tile=8192):
    T = top_i.shape[1]
    tile = min(tile, T)
    spec = pl.BlockSpec((TOP_K, tile), lambda i, ps: (0, i))
    return pl.pallas_call(
        _dest_kernel,
        grid_spec=pltpu.PrefetchScalarGridSpec(num_scalar_prefetch=1, grid=(T // tile,),
                                               in_specs=[spec, spec], out_specs=spec),
        out_shape=jax.ShapeDtypeStruct((TOP_K, T), jnp.int32),
        compiler_params=_params(("parallel",)),
        name="slot_index",
    )(p_start, top_i, rank)


SC_CHUNK = 128


def _sc_mesh():
    info = plsc.get_sparse_core_info()
    mesh = plsc.VectorSubcoreMesh(core_axis_name="c", subcore_axis_name="s")
    return mesh, info.num_cores, info.num_cores * info.num_subcores


def _sc_dispatch(rows, dest3, P):
    T, D = rows.shape
    K = dest3.shape[0]
    mesh, n_cores, n_workers = _sc_mesh()
    n_chunks = T // (n_workers * SC_CHUNK)

    @functools.partial(
        pl.kernel, mesh=mesh, out_type=jax.ShapeDtypeStruct((P, D), rows.dtype),
        scratch_types=[pltpu.VMEM((K, SC_CHUNK), jnp.int32), pltpu.VMEM((SC_CHUNK, D), rows.dtype),
                       pltpu.SemaphoreType.DMA])
    def dispatch(rows_hbm, dest_hbm, out_hbm, idx_v, rows_v, sem):
        worker = lax.axis_index("s") * n_cores + lax.axis_index("c")

        @pl.loop(0, n_chunks)
        def _(j):
            g = worker * n_chunks + j
            base = pl.multiple_of(g * SC_CHUNK, SC_CHUNK)
            pltpu.sync_copy(rows_hbm.at[pl.ds(base, SC_CHUNK)], rows_v)
            pltpu.sync_copy(dest_hbm.at[:, g], idx_v)
            for k in range(K):
                pltpu.async_copy(rows_v, out_hbm.at[idx_v.at[k]], sem).wait()

    return dispatch(rows, dest3)


def _sc_gather(table, idx):
    M = idx.shape[0]
    D = table.shape[1]
    mesh, n_cores, n_workers = _sc_mesh()
    n_chunks = M // (n_workers * SC_CHUNK)

    @functools.partial(
        pl.kernel, mesh=mesh, out_type=jax.ShapeDtypeStruct((M, D), table.dtype),
        scratch_types=[pltpu.VMEM((SC_CHUNK,), jnp.int32), pltpu.VMEM((SC_CHUNK, D), table.dtype),
                       pltpu.SemaphoreType.DMA])
    def gather(table_hbm, idx_hbm, out_hbm, idx_v, rows_v, sem):
        worker = lax.axis_index("s") * n_cores + lax.axis_index("c")

        @pl.loop(0, n_chunks)
        def _(j):
            base = pl.multiple_of((worker * n_chunks + j) * SC_CHUNK, SC_CHUNK)
            pltpu.sync_copy(idx_hbm.at[pl.ds(base, SC_CHUNK)], idx_v)
            pltpu.async_copy(table_hbm.at[idx_v], rows_v, sem).wait()
            pltpu.sync_copy(rows_v, out_hbm.at[pl.ds(base, SC_CHUNK)])

    return gather(table, idx)


def _route(top_i, rank, counts, T, bm):
    A = TOP_K * T
    counts = counts.astype(jnp.int32)
    padded = ((counts + bm - 1) // bm) * bm
    p_end = jnp.cumsum(padded)
    p_start = p_end - padded
    dest = _dest(p_start, top_i, rank)
    nblk = A // bm + N_EXPERTS
    blk_start = jnp.arange(nblk, dtype=jnp.int32) * bm
    blk_e = jnp.minimum(jnp.sum(p_end[None, :] <= blk_start[:, None], axis=1), N_EXPERTS - 1).astype(jnp.int32)
    nvalid = (p_end[-1] // bm).astype(jnp.int32).reshape(1)
    return dest, nblk * bm, blk_e, nvalid


def _layer(x, p, W, bm):
    B, S, _ = x.shape
    T = B * S
    x2 = x.reshape(T, D_MODEL)
    p2 = p.reshape(T, PLE_DIM)
    Wl = dict(W)
    Wl["cosq"], Wl["sinq"], Wl["cosk"], Wl["sink"] = _rope_tables(S)

    q, k, v, dn, z, gates, gb = _in_proj(x2, S, Wl)
    attn = _attention(q, k, v, B, S)
    qn, vk, aux = _dn_prep(dn, gb, S, Wl)
    kt = vk.reshape(T, DN_HEADS, 2, DN_DK)[:, :, 1, :].reshape(T, DN_HEADS * DN_DK).T
    auxt = jnp.swapaxes(aux, 1, 2)
    o_f = _deltanet(qn, vk, kt, aux, auxt, B, S, Wl, 0)
    o_b = _deltanet(qn, vk, kt, aux, auxt, B, S, Wl, 1)
    r, hb, top_i, top_g, rank, cnt = _mix(x2, attn, o_f, o_b, z, gates, p2, Wl)

    dest, P, blk_e, nvalid = _route(top_i, rank, cnt[:, 0], T, bm)
    xs = _sc_dispatch(hb, dest.reshape(TOP_K, T // SC_CHUNK, SC_CHUNK), P)
    yb = _moe(xs, blk_e, nvalid, Wl, bm)
    yg = _sc_gather(yb, dest.reshape(TOP_K * T)).reshape(TOP_K, T, D_MODEL // 2)
    y = _final(r, yg, top_g.T, Wl)
    return y.reshape(B, S, D_MODEL)


def kernel(x_prompt, x_sample, p_prompt, p_sample, w_in, q_a_norm, w_uq, kv_a_norm, w_ukv, w_o_attn, dn_conv, dn_a_log, dn_dt_bias, dn_norm, w_o_dn, w_out, ln1_g, ln1_b, router_w, router_b, w_gate_up, b_gate_up, w_down, b_down, ple_w_proj, ple_w_gate, ln2_g, ln2_b):
    y_prompt, y_sample = x_prompt, x_sample
    for l in range(DEPTH):
        W = _prep_weights(w_in[l], q_a_norm[l], w_uq[l], kv_a_norm[l], w_ukv[l], w_o_attn[l], dn_conv[l],
                          dn_a_log[l], dn_dt_bias[l], dn_norm[l], w_o_dn[l], w_out[l], ln1_g[l], ln1_b[l],
                          router_w[l], router_b[l], w_gate_up[l], b_gate_up[l], w_down[l], b_down[l],
                          ple_w_proj[l], ple_w_gate[l], ln2_g[l], ln2_b[l])
        y_prompt = _layer(y_prompt, p_prompt[l], W, bm=MOE_BLOCK)
        y_sample = _layer(y_sample, p_sample[l], W, bm=MOE_BLOCK)
    return (y_prompt, y_sample)
```

```python
import functools
import math

import numpy as np
import jax
import jax.numpy as jnp
from jax import lax
from jax.experimental import pallas as pl
from jax.experimental.pallas import tpu as pltpu
from jax.experimental.pallas import tpu_sc as plsc

D_MODEL = 1024
MLA_HEADS = 8
Q_LORA = 256
KV_LORA = 128
NOPE_DIM = 64
ROPE_DIM = 32
V_DIM = 64
ROPE_THETA = 10000.0
DN_HEADS = 8
DN_DK = 64
DN_DV = 64
CONV_K = 5
N_EXPERTS = 32
TOP_K = 4
D_FF = 1024
SWIGLU_LIMIT = 7.0
SWIGLU_ALPHA = 1.702
PLE_DIM = 256
DEPTH = 1
DEEPNORM_ALPHA = (2.0 * DEPTH) ** 0.25

LANES = 128
HEAD_PAD = 128
DN_BLOCK = 256
MOE_BLOCK = 512
MIX_TILE = 512
VMEM_LIMIT = 56 * 1024 * 1024

_C_CQ = 0
_C_CKV = _C_CQ + Q_LORA
_C_KR = _C_CKV + KV_LORA
_C_KRS = _C_KR + LANES
_C_AB0 = _C_KRS + LANES
_C_AB1 = _C_AB0 + LANES
_C_DNQ = _C_AB1 + LANES
_C_DNVK = _C_DNQ + DN_HEADS * DN_DK
_C_Z = _C_DNVK + DN_HEADS * (DN_DK + DN_DV)
_C_GATE = _C_Z + DN_HEADS * DN_DV
_C_END = _C_GATE + 2 * D_MODEL

BF16 = jnp.bfloat16
F32 = jnp.float32


def _dot(a, b):
    return jnp.dot(a, b, preferred_element_type=F32)


def _dot_nt(a, b):
    return lax.dot_general(a, b, (((1,), (1,)), ((), ())), preferred_element_type=F32)


def _split_bf16(x):
    hi = x.astype(BF16)
    lo = (x - hi.astype(F32)).astype(BF16)
    return hi, lo


_HI_HALFWORD = 0xFFFF0000


def _pack_halves(x):
    w = x.shape[1] // 2
    bits = lax.bitcast_convert_type(x.astype(BF16).astype(F32), jnp.uint32)
    return (bits[:, :w] >> 16) | (bits[:, w:] & jnp.uint32(_HI_HALFWORD))


def _unpack_halves(words):
    lo = lax.bitcast_convert_type(words << 16, F32)
    hi = lax.bitcast_convert_type(words & jnp.uint32(_HI_HALFWORD), F32)
    return lo, hi


def _const_spec(shape):
    n = len(shape)
    return pl.BlockSpec(shape, lambda *_: (0,) * n, pipeline_mode=pl.Buffered(1))


def _params(sem):
    return pltpu.CompilerParams(dimension_semantics=sem, vmem_limit_bytes=VMEM_LIMIT)


def _in_proj_kernel(x_ref, w1_ref, qan_ref, kvan_ref, wq_ref, wqs_ref, wk_ref, wv_ref,
                    cosq_ref, sinq_ref, cosk_ref, sink_ref, abp_ref,
                    q_ref, k_ref, v_ref, dn_ref, z_ref, gates_ref, gb_ref):
    xb = x_ref[...].astype(BF16)

    def proj(lo, hi):
        return _dot(xb, w1_ref[:, lo:hi])

    def rms(c, g):
        return (c * lax.rsqrt(jnp.mean(c * c, axis=-1, keepdims=True) + 1e-6) * g).astype(BF16)

    cqn = rms(proj(_C_CQ, _C_CKV), qan_ref[...])
    qa = _dot(cqn, wq_ref[...])
    qb = _dot(cqn, wqs_ref[...])
    ckvn = rms(proj(_C_CKV, _C_KR), kvan_ref[...])
    kw = _dot(ckvn, wk_ref[...])
    vw = _dot(ckvn, wv_ref[...])
    kr = proj(_C_KR, _C_KRS) * cosk_ref[...] + proj(_C_KRS, _C_AB0) * sink_ref[...]
    cosq = cosq_ref[...]
    sinq = sinq_ref[...]
    lane = lax.broadcasted_iota(jnp.int32, (1, HEAD_PAD), 1)
    ones_col = (lane == V_DIM).astype(F32)
    for h in range(MLA_HEADS):
        sl = slice(h * HEAD_PAD, (h + 1) * HEAD_PAD)
        q_ref[:, sl] = (qa[:, sl] * cosq + qb[:, sl] * sinq).astype(BF16)
        k_ref[:, sl] = (kw[:, sl] + kr).astype(BF16)
        v_ref[:, sl] = (vw[:, sl] + ones_col).astype(BF16)

    for d, c0 in enumerate((_C_AB0, _C_AB1)):
        ab = proj(c0, c0 + LANES)
        neg_a = abp_ref[2 * d:2 * d + 1, :]
        dtb = abp_ref[2 * d + 1:2 * d + 2, :]
        t = ab + dtb
        sp = jnp.maximum(t, 0.0) + jnp.log(1.0 + jnp.exp(-jnp.abs(t)))
        g = neg_a * sp
        beta = 1.0 / (1.0 + jnp.exp(-ab))
        gb_ref[d] = jnp.where(lane < DN_HEADS, g, beta)[:, :2 * DN_HEADS]

    dn_ref[...] = proj(_C_DNQ, _C_Z).astype(BF16)
    z_ref[...] = proj(_C_Z, _C_GATE).astype(BF16)
    gl = proj(_C_GATE, _C_END)
    gates_ref[...] = (1.0 / (1.0 + jnp.exp(-gl))).astype(BF16)


def _in_proj(x2, S, W, tm=512):
    T = x2.shape[0]
    nseq = S // tm
    row = lambda i: (i, 0)
    pos = lambda i: (i % nseq, 0)
    dn_w = _C_Z - _C_DNQ
    return pl.pallas_call(
        _in_proj_kernel,
        grid=(T // tm,),
        in_specs=[
            pl.BlockSpec((tm, D_MODEL), row),
            _const_spec((D_MODEL, _C_END)),
            _const_spec((1, Q_LORA)), _const_spec((1, KV_LORA)),
            _const_spec((Q_LORA, MLA_HEADS * HEAD_PAD)), _const_spec((Q_LORA, MLA_HEADS * HEAD_PAD)),
            _const_spec((KV_LORA, MLA_HEADS * HEAD_PAD)), _const_spec((KV_LORA, MLA_HEADS * HEAD_PAD)),
            pl.BlockSpec((tm, HEAD_PAD), pos), pl.BlockSpec((tm, HEAD_PAD), pos),
            pl.BlockSpec((tm, HEAD_PAD), pos), pl.BlockSpec((tm, HEAD_PAD), pos),
            _const_spec((8, LANES)),
        ],
        out_specs=[
            pl.BlockSpec((tm, MLA_HEADS * HEAD_PAD), row),
            pl.BlockSpec((tm, MLA_HEADS * HEAD_PAD), row),
            pl.BlockSpec((tm, MLA_HEADS * HEAD_PAD), row),
            pl.BlockSpec((tm, dn_w), row),
            pl.BlockSpec((tm, DN_HEADS * DN_DV), row),
            pl.BlockSpec((tm, 2 * D_MODEL), row),
            pl.BlockSpec((2, tm, 2 * DN_HEADS), lambda i: (0, i, 0)),
        ],
        out_shape=[
            jax.ShapeDtypeStruct((T, MLA_HEADS * HEAD_PAD), BF16),
            jax.ShapeDtypeStruct((T, MLA_HEADS * HEAD_PAD), BF16),
            jax.ShapeDtypeStruct((T, MLA_HEADS * HEAD_PAD), BF16),
            jax.ShapeDtypeStruct((T, dn_w), BF16),
            jax.ShapeDtypeStruct((T, DN_HEADS * DN_DV), BF16),
            jax.ShapeDtypeStruct((T, 2 * D_MODEL), BF16),
            jax.ShapeDtypeStruct((2, T, 2 * DN_HEADS), F32),
        ],
        compiler_params=_params(("parallel",)),
        name="in_proj",
    )(x2, W["w1"], W["qan"], W["kvan"], W["wq"], W["wqs"], W["wk"], W["wv"],
      W["cosq"], W["sinq"], W["cosk"], W["sink"], W["abp"])


def _attn_kernel(q_ref, k_ref, v_ref, o_ref, *, tk, unroll):
    tq = q_ref.shape[0]
    S = k_ref.shape[0]
    q = q_ref[...]

    def body(j, carry):
        m, acc = carry
        off = pl.multiple_of(j * tk, tk)
        s = _dot_nt(q, k_ref[pl.ds(off, tk), :])
        m_new = jnp.maximum(m, jnp.max(s, axis=-1, keepdims=True))
        p = jnp.exp2(s - m_new).astype(BF16)
        acc = acc * jnp.exp2(m - m_new) + _dot(p, v_ref[pl.ds(off, tk), :])
        return m_new, acc

    m0 = jnp.full((tq, 1), -1e30, F32)
    acc0 = jnp.zeros((tq, HEAD_PAD), F32)
    _, acc = lax.fori_loop(0, S // tk, body, (m0, acc0), unroll=unroll)
    o_ref[...] = (acc / acc[:, V_DIM:V_DIM + 1]).astype(BF16)


def _attention(q, k, v, B, S, tq=1024, tk=2048, unroll=4):
    T = q.shape[0]
    tq = min(tq, S)
    tk = min(tk, S)
    nq = S // tq
    return pl.pallas_call(
        functools.partial(_attn_kernel, tk=tk, unroll=unroll),
        grid=(B, MLA_HEADS, nq),
        in_specs=[
            pl.BlockSpec((tq, HEAD_PAD), lambda b, h, i: (b * nq + i, h)),
            pl.BlockSpec((S, HEAD_PAD), lambda b, h, i: (b, h)),
            pl.BlockSpec((S, HEAD_PAD), lambda b, h, i: (b, h)),
        ],
        out_specs=pl.BlockSpec((tq, HEAD_PAD), lambda b, h, i: (b * nq + i, h)),
        out_shape=jax.ShapeDtypeStruct((T, MLA_HEADS * HEAD_PAD), BF16),
        compiler_params=_params(("parallel", "parallel", "arbitrary")),
        name="attention",
    )(q, k, v)


_HALO = 16


def _dn_prep_kernel(x_ref, prev_ref, next_ref, cw_ref, eq_ref, eqt_ref, evk_ref, evkt_ref,
                    gb_ref, tri_ref, qn_ref, vk_ref, aux_ref, *, nseq):
    i = pl.program_id(0)
    tp = x_ref.shape[0]
    first = (i % nseq) == 0
    last = (i % nseq) == nseq - 1
    prev = jnp.where(first, 0.0, prev_ref[...].astype(F32))
    nxt = jnp.where(last, 0.0, next_ref[...].astype(F32))
    xe = jnp.concatenate([prev, x_ref[...].astype(F32), nxt], axis=0)
    n = tp + 2 * _HALO
    y = None
    for j in range(CONV_K):
        shift = (CONV_K // 2 - j) % n
        xs = xe if shift == 0 else pltpu.roll(xe, shift, axis=0)
        term = xs[_HALO:_HALO + tp, :] * cw_ref[j:j + 1, :]
        y = term if y is None else y + term
    y = y * (1.0 / (1.0 + jnp.exp(-y)))

    def group_scale(v, e_ref, et_ref):
        sq = v * v
        hi, lo = _split_bf16(sq)
        ss = _dot(hi, e_ref[...]) + _dot(lo, e_ref[...])
        inv = lax.rsqrt(ss + 1e-6)
        ih, il = _split_bf16(inv)
        return _dot(ih, et_ref[...]) + _dot(il, et_ref[...])

    nq = DN_HEADS * DN_DK
    yq = y[:, :nq]
    qn_ref[...] = (yq * group_scale(yq, eq_ref, eqt_ref) * (DN_DK ** -0.5)).astype(BF16)
    yvk = y[:, nq:]
    sc = group_scale(yvk, evk_ref, evkt_ref)
    lane = lax.broadcasted_iota(jnp.int32, (1, yvk.shape[1]), 1)
    is_k = (lane // DN_DV) % 2 == 1
    vk_ref[...] = (yvk * jnp.where(is_k, sc, 1.0)).astype(BF16)

    hcol = lax.broadcasted_iota(jnp.int32, (1, 2 * DN_HEADS), 1) < DN_HEADS
    for d in range(2):
        gb = gb_ref[d]
        for blk in range(tp // DN_BLOCK):
            rs = slice(blk * DN_BLOCK, (blk + 1) * DN_BLOCK)
            g = gb[rs]
            cs = jnp.dot(tri_ref[d], g, preferred_element_type=F32, precision=lax.Precision.HIGHEST)
            aux_ref[d, rs, :] = jnp.where(hcol, cs, g)


def _dn_prep(dn, gb, S, W, tp=256):
    T = dn.shape[0]
    nseq = S // tp
    hb = tp // _HALO
    nh = T // _HALO
    dn_w = dn.shape[1]
    nq = DN_HEADS * DN_DK
    return pl.pallas_call(
        functools.partial(_dn_prep_kernel, nseq=nseq),
        grid=(T // tp,),
        in_specs=[
            pl.BlockSpec((tp, dn_w), lambda i: (i, 0)),
            pl.BlockSpec((_HALO, dn_w), lambda i: (jnp.maximum(i * hb - 1, 0), 0)),
            pl.BlockSpec((_HALO, dn_w), lambda i: (jnp.minimum((i + 1) * hb, nh - 1), 0)),
            _const_spec((8, dn_w)),
            _const_spec((nq, LANES)), _const_spec((LANES, nq)),
            _const_spec((dn_w - nq, LANES)), _const_spec((LANES, dn_w - nq)),
            pl.BlockSpec((2, tp, 2 * DN_HEADS), lambda i: (0, i, 0)),
            _const_spec((2, DN_BLOCK, DN_BLOCK)),
        ],
        out_specs=[
            pl.BlockSpec((tp, nq), lambda i: (i, 0)),
            pl.BlockSpec((tp, dn_w - nq), lambda i: (i, 0)),
            pl.BlockSpec((2, tp, 2 * DN_HEADS), lambda i: (0, i, 0)),
        ],
        out_shape=[
            jax.ShapeDtypeStruct((T, nq), BF16),
            jax.ShapeDtypeStruct((T, dn_w - nq), BF16),
            jax.ShapeDtypeStruct((2, T, 2 * DN_HEADS), F32),
        ],
        compiler_params=_params(("parallel",)),
        name="dn_prep",
    )(dn, dn, dn, W["convw"], W["eq"], W["eqt"], W["evk"], W["evkt"], gb, W["tri"])


_MERGE_SIZES = (16, 32, 64, 128)


def _active_blocks(d, s):
    return range(1 - d, DN_BLOCK // s, 2)


def _dn_masks(d):
    r = np.arange(DN_BLOCK)[:, None]
    c = np.arange(DN_BLOCK)[None, :]
    rr, cc = (r, c) if d == 0 else (c, r)
    tri = np.stack([rr >= cc, rr > cc]).astype(np.float32)
    diag16 = ((r // 16) == (c // 16)).astype(np.float32)
    offc = []
    for s in _MERGE_SIZES:
        full = ((rr // s) % 2 == 1) & ((rr // s) == (cc // s) + 1)
        rows = np.concatenate([np.arange(b * s, (b + 1) * s) for b in _active_blocks(d, s)])
        assert not np.delete(full, rows, axis=0).any()
        offc.append(full[rows].astype(np.float32))
    return tri, diag16, np.stack(offc)


def _deltanet_kernel(qn_ref, vk_ref, kt_ref, aux_ref, auxt_ref, tri_ref, diag16_ref, offc_ref, o_ref,
                     s_ref, nm_ref, p_ref, x_ref, aqk_ref, *, d):
    i = pl.program_id(1)
    C = DN_BLOCK
    heads = range(DN_HEADS)

    @pl.when(i == 0)
    def _():
        s_ref[...] = jnp.zeros_like(s_ref)

    rowi = lax.broadcasted_iota(jnp.int32, (C, C), 0)
    coli = lax.broadcasted_iota(jnp.int32, (C, C), 1)
    eye = (rowi == coli).astype(F32)
    lane = lax.broadcasted_iota(jnp.int32, (1, LANES), 1)
    r64 = lax.broadcasted_iota(jnp.int32, (DN_DK, LANES), 0)
    c64 = lax.broadcasted_iota(jnp.int32, (DN_DK, LANES), 1)
    zeros_kt = jnp.zeros((DN_DK, C), BF16)
    zeros_s = jnp.zeros((DN_DK, LANES), F32)

    def q_pair(h):
        return qn_ref[:, (h // 2) * LANES:(h // 2 + 1) * LANES]

    def vk_head(h):
        return vk_ref[:, h * LANES:(h + 1) * LANES]

    def kt_head(h):
        return kt_ref[h * DN_DK:(h + 1) * DN_DK, :]

    def gc_col(h):
        return aux_ref[0, :, h:h + 1]

    def beta_col(h):
        return aux_ref[0, :, DN_HEADS + h:DN_HEADS + h + 1]

    def gc_row(h):
        return auxt_ref[0, h:h + 1, :]

    for h in heads:
        kt = kt_head(h)
        kt_for_q = jnp.concatenate([kt, zeros_kt] if h % 2 == 0 else [zeros_kt, kt], axis=0)
        kt_for_k = jnp.concatenate([zeros_kt, kt], axis=0)
        qk = _dot(q_pair(h), kt_for_q)
        kk = _dot(vk_head(h), kt_for_k)
        e0 = jnp.exp(jnp.minimum(gc_col(h) - gc_row(h), 0.0))
        aqk_ref[h] = (qk * e0 * tri_ref[0]).astype(BF16)
        nm_ref[h] = (-(kk * e0 * tri_ref[1]) * beta_col(h)).astype(BF16)

    diag16 = diag16_ref[...]
    for h in heads:
        nd = nm_ref[h] * diag16
        p_ref[h] = (eye + nd.astype(F32)).astype(BF16)
        x_ref[h] = _dot(nd, nd).astype(BF16)
    for it in range(3):
        for h in heads:
            pb = p_ref[h]
            sqb = x_ref[h]
            p_ref[h] = (pb.astype(F32) + _dot(pb, sqb)).astype(BF16)
            if it < 2:
                x_ref[h] = _dot(sqb, sqb).astype(BF16)
    for k, s in enumerate(_MERGE_SIZES):
        blocks = list(_active_blocks(d, s))
        zeros_blk = jnp.zeros((s, C), BF16)

        def active_rows(ref, h):
            return jnp.concatenate([ref[h, b * s:(b + 1) * s, :] for b in blocks], axis=0)

        for h in heads:
            xc = _dot(active_rows(nm_ref, h) * offc_ref[k], p_ref[h]).astype(BF16)
            pieces = []
            for j in range(len(blocks)):
                piece = xc[j * s:(j + 1) * s, :]
                pieces += [zeros_blk, piece] if d == 0 else [piece, zeros_blk]
            x_ref[h] = jnp.concatenate(pieces, axis=0)
        for h in heads:
            pr = active_rows(p_ref, h)
            prn = (pr.astype(F32) + _dot(pr, x_ref[h])).astype(BF16)
            for j, b in enumerate(blocks):
                p_ref[h, b * s:(b + 1) * s, :] = prn[j * s:(j + 1) * s, :]

    o_pair = None
    for h in heads:
        par = h % 2
        gc_c = gc_col(h)
        gc_r = gc_row(h)
        g_tot = gc_r[:, C - 1:C] if d == 0 else gc_r[:, 0:1]
        egc = jnp.exp(gc_c)
        rhs = (vk_head(h).astype(F32) * beta_col(h) * jnp.where(lane < DN_DV, 1.0, egc)).astype(BF16)
        uw = _dot(p_ref[h], rhs).astype(BF16)
        s_pl = s_ref[h]
        eye_pl = (c64 == r64 + par * DN_DV).astype(F32)
        s_aug = jnp.concatenate([eye_pl, -s_pl], axis=0).astype(BF16)
        v_new = _dot(uw, s_aug).astype(BF16)
        s_sel = jnp.concatenate([s_pl, zeros_s] if par == 0 else [zeros_s, s_pl], axis=0).astype(BF16)
        qd = (q_pair(h).astype(F32) * egc).astype(BF16)
        o_pl = _dot(qd, s_sel) + _dot(aqk_ref[h], v_new)
        kd = (kt_head(h).astype(F32) * jnp.exp(g_tot - gc_r)).astype(BF16)
        s_ref[h] = s_pl * jnp.exp(g_tot) + _dot(kd, v_new)
        if par == 0:
            o_pair = o_pl
        else:
            o_ref[:, (h // 2) * LANES:(h // 2 + 1) * LANES] = o_pair + o_pl


def _deltanet(qn, vk, kt, aux, auxt, B, S, W, d):
    T = qn.shape[0]
    C = DN_BLOCK
    nb = S // C

    def blk(b, i):
        return b * nb + (i if d == 0 else nb - 1 - i)

    nq = DN_HEADS * DN_DK
    tri, diag16, offc = W["dn_masks"][d]
    return pl.pallas_call(
        functools.partial(_deltanet_kernel, d=d),
        grid=(B, nb),
        in_specs=[
            pl.BlockSpec((C, nq), lambda b, i: (blk(b, i), 0)),
            pl.BlockSpec((C, 2 * nq), lambda b, i: (blk(b, i), 0)),
            pl.BlockSpec((nq, C), lambda b, i: (0, blk(b, i))),
            pl.BlockSpec((1, C, 2 * DN_HEADS), lambda b, i: (d, blk(b, i), 0)),
            pl.BlockSpec((1, 2 * DN_HEADS, C), lambda b, i: (d, 0, blk(b, i))),
            _const_spec((2, C, C)), _const_spec((C, C)), _const_spec((len(_MERGE_SIZES), C // 2, C)),
        ],
        out_specs=pl.BlockSpec((C, DN_HEADS * DN_DV), lambda b, i: (blk(b, i), 0)),
        out_shape=jax.ShapeDtypeStruct((T, DN_HEADS * DN_DV), F32),
        scratch_shapes=[pltpu.VMEM((DN_HEADS, DN_DK, LANES), F32)]
        + [pltpu.VMEM((DN_HEADS, C, C), BF16)] * 4,
        compiler_params=_params(("parallel", "arbitrary")),
        name="deltanet_fwd" if d == 0 else "deltanet_bwd",
    )(qn, vk, kt, aux, auxt, tri, diag16, offc)


def _layer_norm(v, g, b):
    mu = jnp.mean(v, axis=-1, keepdims=True)
    c = v - mu
    var = jnp.mean(c * c, axis=-1, keepdims=True)
    return c * lax.rsqrt(var + 1e-5) * g + b


def _mix_kernel(x_ref, attn_ref, of_ref, ob_ref, z_ref, gates_ref, p_ref,
                woa_ref, wod_ref, wout_ref, e8_ref, e8t_ref, dnorm_ref, ln1g_ref, ln1b_ref,
                wpg_ref, wpp_ref, rwh_ref, rwl_ref, rb_ref, ustrict_ref,
                r_ref, hb_ref, ti_ref, tg_ref, rank_ref, cnt_ref, run_ref):
    @pl.when(pl.program_id(0) == 0)
    def _():
        run_ref[...] = jnp.zeros_like(run_ref)

    oa = _dot(attn_ref[...], woa_ref[...])
    o = of_ref[...] + ob_ref[...]
    hi, lo = _split_bf16(o * o)
    ms = (_dot(hi, e8_ref[...]) + _dot(lo, e8_ref[...])) * (1.0 / DN_DV)
    ih, il = _split_bf16(lax.rsqrt(ms + 1e-6))
    sc = _dot(ih, e8t_ref[...]) + _dot(il, e8t_ref[...])
    zf = z_ref[...].astype(F32)
    od_in = o * sc * dnorm_ref[...] * (zf * (1.0 / (1.0 + jnp.exp(-zf))))
    od = _dot(od_in.astype(BF16), wod_ref[...])
    mix = gates_ref[:, :D_MODEL].astype(F32) * oa + gates_ref[:, D_MODEL:].astype(F32) * od
    mo = _dot(mix.astype(BF16), wout_ref[...])
    h = _layer_norm(DEEPNORM_ALPHA * x_ref[...] + mo, ln1g_ref[...], ln1b_ref[...])
    hb = h.astype(BF16)
    hb_ref[...] = _pack_halves(h)
    pg = _dot(hb, wpg_ref[...])
    pp = _dot(p_ref[...].astype(BF16), wpp_ref[...])
    r_ref[...] = DEEPNORM_ALPHA * h + pp * (1.0 / (1.0 + jnp.exp(-pg)))

    hl = (h - hb.astype(F32)).astype(BF16)
    logits = (_dot_nt(rwh_ref[...], hb) + _dot_nt(rwh_ref[...], hl)
              + _dot_nt(rwl_ref[...], hb) + rb_ref[...])
    eid = lax.broadcasted_iota(jnp.int32, logits.shape, 0)
    vals = []
    run = run_ref[:, 0:1]
    for k in range(TOP_K):
        m = jnp.max(logits, axis=0, keepdims=True)
        idx = jnp.min(jnp.where(logits == m, eid, N_EXPERTS), axis=0, keepdims=True)
        ti_ref[k:k + 1, :] = idx
        vals.append(m)
        hit = eid == idx
        logits = jnp.where(hit, -jnp.inf, logits)
        onehot = hit.astype(F32)
        earlier = _dot(onehot.astype(BF16), ustrict_ref[...])
        rank = jnp.sum(onehot * (run + earlier), axis=0, keepdims=True)
        rank_ref[k:k + 1, :] = rank.astype(jnp.int32)
        run = run + jnp.sum(onehot, axis=1, keepdims=True)
    run_ref[...] = jnp.broadcast_to(run, run_ref.shape)
    cnt_ref[...] = jnp.broadcast_to(run, cnt_ref.shape)
    es = [jnp.exp(v - vals[0]) for v in vals]
    den = es[0] + es[1] + es[2] + es[3]
    for k in range(TOP_K):
        tg_ref[k:k + 1, :] = es[k] / den


def _mix(x2, attn, o_f, o_b, z, gates, p2, W):
    tm = MIX_TILE
    T = x2.shape[0]
    row = lambda i: (i, 0)
    nd = DN_HEADS * DN_DV
    return pl.pallas_call(
        _mix_kernel,
        grid=(T // tm,),
        in_specs=[
            pl.BlockSpec((tm, D_MODEL), row),
            pl.BlockSpec((tm, MLA_HEADS * HEAD_PAD), row),
            pl.BlockSpec((tm, nd), row),
            pl.BlockSpec((tm, nd), row),
            pl.BlockSpec((tm, nd), row),
            pl.BlockSpec((tm, 2 * D_MODEL), row),
            pl.BlockSpec((tm, PLE_DIM), row),
            _const_spec((MLA_HEADS * HEAD_PAD, D_MODEL)), _const_spec((nd, D_MODEL)),
            _const_spec((D_MODEL, D_MODEL)),
            _const_spec((nd, LANES)), _const_spec((LANES, nd)), _const_spec((1, nd)),
            _const_spec((1, D_MODEL)), _const_spec((1, D_MODEL)),
            _const_spec((D_MODEL, D_MODEL)), _const_spec((PLE_DIM, D_MODEL)),
            _const_spec((N_EXPERTS, D_MODEL)), _const_spec((N_EXPERTS, D_MODEL)),
            _const_spec((N_EXPERTS, 1)),
            _const_spec((tm, tm)),
        ],
        out_specs=[
            pl.BlockSpec((tm, D_MODEL), row),
            pl.BlockSpec((tm, D_MODEL // 2), row),
            pl.BlockSpec((TOP_K, tm), lambda i: (0, i)),
            pl.BlockSpec((TOP_K, tm), lambda i: (0, i)),
            pl.BlockSpec((TOP_K, tm), lambda i: (0, i)),
            _const_spec((N_EXPERTS, LANES)),
        ],
        out_shape=[
            jax.ShapeDtypeStruct((T, D_MODEL), F32),
            jax.ShapeDtypeStruct((T, D_MODEL // 2), jnp.uint32),
            jax.ShapeDtypeStruct((TOP_K, T), jnp.int32),
            jax.ShapeDtypeStruct((TOP_K, T), F32),
            jax.ShapeDtypeStruct((TOP_K, T), jnp.int32),
            jax.ShapeDtypeStruct((N_EXPERTS, LANES), F32),
        ],
        scratch_shapes=[pltpu.VMEM((N_EXPERTS, LANES), F32)],
        compiler_params=_params(("arbitrary",)),
        name="mix",
    )(x2, attn, o_f, o_b, z, gates, p2, W["woa"], W["wod"], W["wout"], W["e8"], W["e8t"], W["dnorm"],
      W["ln1g"], W["ln1b"], W["wpg"], W["wpp"], W["rwh"], W["rwl"], W["rb"], W["ustrict"])


def _moe_kernel(blk_e_ref, nvalid_ref, xs_ref, wgu_ref, bgu_ref, wd_ref, bd_ref, y_ref, *, fc):
    i = pl.program_id(0)

    @pl.when(i < nvalid_ref[0])
    def _():
        xs = jnp.concatenate(_unpack_halves(xs_ref[...]), axis=1).astype(BF16)
        acc = None
        for c in range(D_FF // fc):
            lo, hi = c * fc, (c + 1) * fc
            gate = _dot(xs, wgu_ref[0, :, lo:hi]) + bgu_ref[0, :, lo:hi]
            up = _dot(xs, wgu_ref[0, :, D_FF + lo:D_FF + hi]) + bgu_ref[0, :, D_FF + lo:D_FF + hi]
            gate = jnp.minimum(gate, SWIGLU_LIMIT)
            up = jnp.clip(up, -SWIGLU_LIMIT, SWIGLU_LIMIT)
            act = gate * (1.0 / (1.0 + jnp.exp(-SWIGLU_ALPHA * gate))) * (up + 1.0)
            part = _dot(act.astype(BF16), wd_ref[0, lo:hi, :])
            acc = part if acc is None else acc + part
        y_ref[...] = _pack_halves(acc + bd_ref[0])

    @pl.when(i >= nvalid_ref[0])
    def _():
        y_ref[...] = jnp.zeros_like(y_ref)


def _moe(xs, blk_e, nvalid, W, bm, fc=256):
    P = xs.shape[0]
    grid_spec = pltpu.PrefetchScalarGridSpec(
        num_scalar_prefetch=2,
        grid=(P // bm,),
        in_specs=[
            pl.BlockSpec((bm, D_MODEL // 2), lambda i, be, nv: (i, 0)),
            pl.BlockSpec((1, D_MODEL, 2 * D_FF), lambda i, be, nv: (be[i], 0, 0)),
            pl.BlockSpec((1, 1, 2 * D_FF), lambda i, be, nv: (be[i], 0, 0)),
            pl.BlockSpec((1, D_FF, D_MODEL), lambda i, be, nv: (be[i], 0, 0)),
            pl.BlockSpec((1, 1, D_MODEL), lambda i, be, nv: (be[i], 0, 0)),
        ],
        out_specs=pl.BlockSpec((bm, D_MODEL // 2), lambda i, be, nv: (i, 0)),
    )
    return pl.pallas_call(
        functools.partial(_moe_kernel, fc=fc),
        grid_spec=grid_spec,
        out_shape=jax.ShapeDtypeStruct((P, D_MODEL // 2), jnp.uint32),
        compiler_params=_params(("arbitrary",)),
        name="moe",
    )(blk_e, nvalid, xs, W["wgu"], W["bgu"], W["wd"], W["bd"])


def _final_kernel(r_ref, yg_ref, tg_ref, g_ref, b_ref, y_ref):
    ffn_lo = ffn_hi = None
    for k in range(TOP_K):
        lo, hi = _unpack_halves(yg_ref[k])
        g = tg_ref[:, k:k + 1]
        ffn_lo = lo * g if ffn_lo is None else ffn_lo + lo * g
        ffn_hi = hi * g if ffn_hi is None else ffn_hi + hi * g
    acc = r_ref[...] + jnp.concatenate([ffn_lo, ffn_hi], axis=1)
    y_ref[...] = _layer_norm(acc, g_ref[...], b_ref[...])


def _final(r, yg, tg, W, tm=512):
    T = r.shape[0]
    tm = min(tm, T)
    row = lambda i: (i, 0)
    return pl.pallas_call(
        _final_kernel,
        grid=(T // tm,),
        in_specs=[pl.BlockSpec((tm, D_MODEL), row),
                  pl.BlockSpec((TOP_K, tm, D_MODEL // 2), lambda i: (0, i, 0)),
                  pl.BlockSpec((tm, TOP_K), row),
                  _const_spec((1, D_MODEL)), _const_spec((1, D_MODEL))],
        out_specs=pl.BlockSpec((tm, D_MODEL), row),
        out_shape=jax.ShapeDtypeStruct((T, D_MODEL), F32),
        compiler_params=_params(("parallel",)),
        name="final_ln",
    )(r, yg, tg, W["ln2g"], W["ln2b"])


def _pad_heads(w, n_heads, width, start, size, dst=0):
    K = w.shape[0]
    w3 = w.reshape(K, n_heads, width)[:, :, start:start + size]
    out = jnp.zeros((K, n_heads, HEAD_PAD), w.dtype)
    out = out.at[:, :, dst:dst + size].set(w3)
    return out.reshape(K, n_heads * HEAD_PAD)


def _prep_weights(w_in, q_a_norm, w_uq, kv_a_norm, w_ukv, w_o_attn, dn_conv, dn_a_log, dn_dt_bias,
                  dn_norm, w_o_dn, w_out, ln1_g, ln1_b, router_w, router_b, w_gate_up, b_gate_up,
                  w_down, b_down, ple_w_proj, ple_w_gate, ln2_g, ln2_b):
    W = {}
    half = ROPE_DIM // 2
    o = 0
    cq = w_in[:, o:o + Q_LORA]; o += Q_LORA
    ckv = w_in[:, o:o + KV_LORA]; o += KV_LORA
    kr = w_in[:, o:o + ROPE_DIM]; o += ROPE_DIM
    nqk = DN_HEADS * DN_DK
    dq = w_in[:, o:o + nqk]; o += nqk
    dk = w_in[:, o:o + nqk]; o += nqk
    dv = w_in[:, o:o + DN_HEADS * DN_DV]; o += DN_HEADS * DN_DV
    dz = w_in[:, o:o + DN_HEADS * DN_DV]; o += DN_HEADS * DN_DV
    da = w_in[:, o:o + 2 * DN_HEADS]; o += 2 * DN_HEADS
    db = w_in[:, o:o + 2 * DN_HEADS]; o += 2 * DN_HEADS
    gate = w_in[:, o:o + 2 * D_MODEL]

    def lane_block(parts):
        w = jnp.concatenate(parts, axis=1)
        return jnp.pad(w, ((0, 0), (0, LANES - w.shape[1])))

    zeros64 = jnp.zeros((D_MODEL, NOPE_DIM), F32)
    kr_blk = lane_block([zeros64, kr])
    krs_blk = lane_block([zeros64, kr[:, half:], kr[:, :half]])
    ab = [lane_block([da[:, d * DN_HEADS:(d + 1) * DN_HEADS], db[:, d * DN_HEADS:(d + 1) * DN_HEADS]])
          for d in range(2)]

    def interleave_vk(v, k):
        lead = v.shape[:-1]
        v3 = v.reshape(lead + (DN_HEADS, DN_DV))
        k3 = k.reshape(lead + (DN_HEADS, DN_DK))
        return jnp.concatenate([v3, k3], axis=-1).reshape(lead + (DN_HEADS * (DN_DV + DN_DK),))

    W["w1"] = jnp.concatenate([cq, ckv, kr_blk, krs_blk, ab[0], ab[1], dq, interleave_vk(dv, dk), dz, gate],
                              axis=1).astype(BF16)
    W["qan"] = q_a_norm.reshape(1, Q_LORA)
    W["kvan"] = kv_a_norm.reshape(1, KV_LORA)

    qw = NOPE_DIM + ROPE_DIM
    wq_nope = _pad_heads(w_uq, MLA_HEADS, qw, 0, NOPE_DIM, 0)
    wq_r1 = _pad_heads(w_uq, MLA_HEADS, qw, NOPE_DIM, half, NOPE_DIM)
    wq_r2 = _pad_heads(w_uq, MLA_HEADS, qw, NOPE_DIM + half, half, NOPE_DIM + half)
    W["wq"] = (wq_nope + wq_r1 + wq_r2).astype(BF16)
    wq_s1 = _pad_heads(w_uq, MLA_HEADS, qw, NOPE_DIM + half, half, NOPE_DIM)
    wq_s2 = _pad_heads(w_uq, MLA_HEADS, qw, NOPE_DIM, half, NOPE_DIM + half)
    W["wqs"] = (wq_s1 + wq_s2).astype(BF16)
    kvw = NOPE_DIM + V_DIM
    W["wk"] = _pad_heads(w_ukv, MLA_HEADS, kvw, 0, NOPE_DIM, 0).astype(BF16)
    W["wv"] = _pad_heads(w_ukv, MLA_HEADS, kvw, NOPE_DIM, V_DIM, 0).astype(BF16)

    neg_a = -jnp.exp(dn_a_log.astype(F32))
    abp = jnp.zeros((8, LANES), F32)
    for d in range(2):
        abp = abp.at[2 * d, :DN_HEADS].set(neg_a[d])
        abp = abp.at[2 * d + 1, :DN_HEADS].set(dn_dt_bias[d].astype(F32))
    W["abp"] = abp

    cw = jnp.concatenate([dn_conv[:, :nqk], interleave_vk(dn_conv[:, 2 * nqk:], dn_conv[:, nqk:2 * nqk])], axis=1)
    W["convw"] = jnp.pad(cw.astype(F32), ((0, 8 - CONV_K), (0, 0)))

    def group_indicator(width, group):
        e = (np.arange(width)[:, None] // group == np.arange(LANES)[None, :]).astype(np.float32)
        return e

    eq = group_indicator(nqk, DN_DK)
    W["eq"] = jnp.asarray(eq, BF16)
    W["eqt"] = jnp.asarray(eq.T, BF16)
    evk = group_indicator(2 * nqk, DN_DK)
    W["evk"] = jnp.asarray(evk, BF16)
    W["evkt"] = jnp.asarray(evk.T, BF16)
    W["e8"] = W["eq"]
    W["e8t"] = W["eqt"]
    r = np.arange(DN_BLOCK)
    W["tri"] = jnp.asarray(np.stack([r[:, None] >= r[None, :], r[:, None] <= r[None, :]]).astype(np.float32))
    W["dn_masks"] = [(jnp.asarray(t), jnp.asarray(g, BF16), jnp.asarray(o, BF16))
                     for t, g, o in (_dn_masks(0), _dn_masks(1))]
    rt = np.arange(MIX_TILE)
    W["ustrict"] = jnp.asarray((rt[:, None] < rt[None, :]).astype(np.float32), BF16)

    woa = w_o_attn.reshape(MLA_HEADS, V_DIM, D_MODEL)
    woa = jnp.pad(woa, ((0, 0), (0, HEAD_PAD - V_DIM), (0, 0)))
    W["woa"] = woa.reshape(MLA_HEADS * HEAD_PAD, D_MODEL).astype(BF16)
    W["wod"] = w_o_dn.astype(BF16)
    W["wout"] = w_out.astype(BF16)
    W["dnorm"] = jnp.tile(dn_norm.astype(F32), DN_HEADS).reshape(1, DN_HEADS * DN_DV)
    W["ln1g"] = ln1_g.reshape(1, D_MODEL)
    W["ln1b"] = ln1_b.reshape(1, D_MODEL)
    W["ln2g"] = ln2_g.reshape(1, D_MODEL)
    W["ln2b"] = ln2_b.reshape(1, D_MODEL)
    W["wpg"] = ple_w_gate.astype(BF16)
    W["wpp"] = ple_w_proj.astype(BF16)
    rwt = router_w.T.astype(F32)
    W["rwh"], W["rwl"] = _split_bf16(rwt)
    W["rb"] = router_b.reshape(N_EXPERTS, 1).astype(F32)
    W["wgu"] = w_gate_up.astype(BF16)
    W["bgu"] = b_gate_up.reshape(N_EXPERTS, 1, 2 * D_FF).astype(F32)
    W["wd"] = w_down.astype(BF16)
    W["bd"] = b_down.reshape(N_EXPERTS, 1, D_MODEL).astype(F32)
    return W


def _rope_tables(S):
    half = ROPE_DIM // 2
    inv = ROPE_THETA ** (-jnp.arange(0, ROPE_DIM, 2, dtype=F32) / ROPE_DIM)
    ang = jnp.arange(S, dtype=F32)[:, None] * inv[None, :]
    cos, sin = jnp.cos(ang), jnp.sin(ang)
    c = (NOPE_DIM + ROPE_DIM) ** -0.5 * math.log2(math.e)
    pad = jnp.zeros((S, HEAD_PAD - NOPE_DIM - ROPE_DIM), F32)
    cos_blk = jnp.concatenate([cos, cos, pad], axis=1)
    sin_blk = jnp.concatenate([-sin, sin, pad], axis=1)
    cosq = jnp.concatenate([jnp.ones((S, NOPE_DIM), F32), cos_blk], axis=1) * c
    sinq = jnp.concatenate([jnp.zeros((S, NOPE_DIM), F32), sin_blk], axis=1) * c
    cosk = jnp.concatenate([jnp.zeros((S, NOPE_DIM), F32), cos_blk], axis=1)
    sink = jnp.concatenate([jnp.zeros((S, NOPE_DIM), F32), sin_blk], axis=1)
    return cosq, sinq, cosk, sink


def _dest_kernel(pstart_ref, ti_ref, rank_ref, dest_ref):
    ti = ti_ref[...]
    dest = rank_ref[...]
    for e in range(N_EXPERTS):
        dest = dest + jnp.where(ti == e, pstart_ref[e], 0)
    dest_ref[...] = dest


def _dest(p_start, top_i, rank, tile=8192):
    T = top_i.shape[1]
    tile = min(tile, T)
    spec = pl.BlockSpec((TOP_K, tile), lambda i, ps: (0, i))
    return pl.pallas_call(
        _dest_kernel,
        grid_spec=pltpu.PrefetchScalarGridSpec(num_scalar_prefetch=1, grid=(T // tile,),
                                               in_specs=[spec, spec], out_specs=spec),
        out_shape=jax.ShapeDtypeStruct((TOP_K, T), jnp.int32),
        compiler_params=_params(("parallel",)),
        name="slot_index",
    )(p_start, top_i, rank)


SC_CHUNK = 128


def _sc_mesh():
    info = plsc.get_sparse_core_info()
    mesh = plsc.VectorSubcoreMesh(core_axis_name="c", subcore_axis_name="s")
    return mesh, info.num_cores, info.num_cores * info.num_subcores


def _sc_dispatch(rows, dest3, P):
    T, D = rows.shape
    K = dest3.shape[0]
    mesh, n_cores, n_workers = _sc_mesh()
    n_chunks = T // (n_workers * SC_CHUNK)

    @functools.partial(
        pl.kernel, mesh=mesh, out_type=jax.ShapeDtypeStruct((P, D), rows.dtype),
        scratch_types=[pltpu.VMEM((K, SC_CHUNK), jnp.int32), pltpu.VMEM((SC_CHUNK, D), rows.dtype),
                       pltpu.SemaphoreType.DMA])
    def dispatch(rows_hbm, dest_hbm, out_hbm, idx_v, rows_v, sem):
        worker = lax.axis_index("s") * n_cores + lax.axis_index("c")

        @pl.loop(0, n_chunks)
        def _(j):
            g = worker * n_chunks + j
            base = pl.multiple_of(g * SC_CHUNK, SC_CHUNK)
            pltpu.sync_copy(rows_hbm.at[pl.ds(base, SC_CHUNK)], rows_v)
            pltpu.sync_copy(dest_hbm.at[:, g], idx_v)
            for k in range(K):
                pltpu.async_copy(rows_v, out_hbm.at[idx_v.at[k]], sem).wait()

    return dispatch(rows, dest3)


def _sc_gather(table, idx):
    M = idx.shape[0]
    D = table.shape[1]
    mesh, n_cores, n_workers = _sc_mesh()
    n_chunks = M // (n_workers * SC_CHUNK)

    @functools.partial(
        pl.kernel, mesh=mesh, out_type=jax.ShapeDtypeStruct((M, D), table.dtype),
        scratch_types=[pltpu.VMEM((SC_CHUNK,), jnp.int32), pltpu.VMEM((SC_CHUNK, D), table.dtype),
                       pltpu.SemaphoreType.DMA])
    def gather(table_hbm, idx_hbm, out_hbm, idx_v, rows_v, sem):
        worker = lax.axis_index("s") * n_cores + lax.axis_index("c")

        @pl.loop(0, n_chunks)
        def _(j):
            base = pl.multiple_of((worker * n_chunks + j) * SC_CHUNK, SC_CHUNK)
            pltpu.sync_copy(idx_hbm.at[pl.ds(base, SC_CHUNK)], idx_v)
            pltpu.async_copy(table_hbm.at[idx_v], rows_v, sem).wait()
            pltpu.sync_copy(rows_v, out_hbm.at[pl.ds(base, SC_CHUNK)])

    return gather(table, idx)


def _route(top_i, rank, counts, T, bm):
    A = TOP_K * T
    counts = counts.astype(jnp.int32)
    padded = ((counts + bm - 1) // bm) * bm
    p_end = jnp.cumsum(padded)
    p_start = p_end - padded
    dest = _dest(p_start, top_i, rank)
    nblk = A // bm + N_EXPERTS
    blk_start = jnp.arange(nblk, dtype=jnp.int32) * bm
    blk_e = jnp.minimum(jnp.sum(p_end[None, :] <= blk_start[:, None], axis=1), N_EXPERTS - 1).astype(jnp.int32)
    nvalid = (p_end[-1] // bm).astype(jnp.int32).reshape(1)
    return dest, nblk * bm, blk_e, nvalid


def _layer(x, p, W, bm):
    B, S, _ = x.shape
    T = B * S
    x2 = x.reshape(T, D_MODEL)
    p2 = p.reshape(T, PLE_DIM)
    Wl = dict(W)
    Wl["cosq"], Wl["sinq"], Wl["cosk"], Wl["sink"] = _rope_tables(S)

    q, k, v, dn, z, gates, gb = _in_proj(x2, S, Wl)
    attn = _attention(q, k, v, B, S)
    qn, vk, aux = _dn_prep(dn, gb, S, Wl)
    kt = vk.reshape(T, DN_HEADS, 2, DN_DK)[:, :, 1, :].reshape(T, DN_HEADS * DN_DK).T
    auxt = jnp.swapaxes(aux, 1, 2)
    o_f = _deltanet(qn, vk, kt, aux, auxt, B, S, Wl, 0)
    o_b = _deltanet(qn, vk, kt, aux, auxt, B, S, Wl, 1)
    r, hb, top_i, top_g, rank, cnt = _mix(x2, attn, o_f, o_b, z, gates, p2, Wl)

    dest, P, blk_e, nvalid = _route(top_i, rank, cnt[:, 0], T, bm)
    xs = _sc_dispatch(hb, dest.reshape(TOP_K, T // SC_CHUNK, SC_CHUNK), P)
    yb = _moe(xs, blk_e, nvalid, Wl, bm)
    yg = _sc_gather(yb, dest.reshape(TOP_K * T)).reshape(TOP_K, T, D_MODEL // 2)
    y = _final(r, yg, top_g.T, Wl)
    return y.reshape(B, S, D_MODEL)


def kernel(x_prompt, x_sample, p_prompt, p_sample, w_in, q_a_norm, w_uq, kv_a_norm, w_ukv, w_o_attn, dn_conv, dn_a_log, dn_dt_bias, dn_norm, w_o_dn, w_out, ln1_g, ln1_b, router_w, router_b, w_gate_up, b_gate_up, w_down, b_down, ple_w_proj, ple_w_gate, ln2_g, ln2_b):
    y_prompt, y_sample = x_prompt, x_sample
    for l in range(DEPTH):
        W = _prep_weights(w_in[l], q_a_norm[l], w_uq[l], kv_a_norm[l], w_ukv[l], w_o_attn[l], dn_conv[l],
                          dn_a_log[l], dn_dt_bias[l], dn_norm[l], w_o_dn[l], w_out[l], ln1_g[l], ln1_b[l],
                          router_w[l], router_b[l], w_gate_up[l], b_gate_up[l], w_down[l], b_down[l],
                          ple_w_proj[l], ple_w_gate[l], ln2_g[l], ln2_b[l])
        y_prompt = _layer(y_prompt, p_prompt[l], W, bm=MOE_BLOCK)
        y_sample = _layer(y_sample, p_sample[l], W, bm=MOE_BLOCK)
    return (y_prompt, y_sample)
```

```python
import functools
import math

import numpy as np
import jax
import jax.numpy as jnp
from jax import lax
from jax.experimental import pallas as pl
from jax.experimental.pallas import tpu as pltpu
from jax.experimental.pallas import tpu_sc as plsc

D_MODEL = 1024
MLA_HEADS = 8
Q_LORA = 256
KV_LORA = 128
NOPE_DIM = 64
ROPE_DIM = 32
V_DIM = 64
ROPE_THETA = 10000.0
DN_HEADS = 8
DN_DK = 64
DN_DV = 64
CONV_K = 5
N_EXPERTS = 32
TOP_K = 4
D_FF = 1024
SWIGLU_LIMIT = 7.0
SWIGLU_ALPHA = 1.702
PLE_DIM = 256
DEPTH = 1
DEEPNORM_ALPHA = (2.0 * DEPTH) ** 0.25

LANES = 128
HEAD_PAD = 128
DN_BLOCK = 256
MOE_BLOCK = 512
MIX_TILE = 512
VMEM_LIMIT = 56 * 1024 * 1024

_C_CQ = 0
_C_CKV = _C_CQ + Q_LORA
_C_KR = _C_CKV + KV_LORA
_C_KRS = _C_KR + LANES
_C_AB0 = _C_KRS + LANES
_C_AB1 = _C_AB0 + LANES
_C_DNQ = _C_AB1 + LANES
_C_DNVK = _C_DNQ + DN_HEADS * DN_DK
_C_Z = _C_DNVK + DN_HEADS * (DN_DK + DN_DV)
_C_GATE = _C_Z + DN_HEADS * DN_DV
_C_END = _C_GATE + 2 * D_MODEL

BF16 = jnp.bfloat16
F32 = jnp.float32


def _dot(a, b):
    return jnp.dot(a, b, preferred_element_type=F32)


def _dot_nt(a, b):
    return lax.dot_general(a, b, (((1,), (1,)), ((), ())), preferred_element_type=F32)


def _split_bf16(x):
    hi = x.astype(BF16)
    lo = (x - hi.astype(F32)).astype(BF16)
    return hi, lo


_HI_HALFWORD = 0xFFFF0000


def _pack_halves(x):
    w = x.shape[1] // 2
    bits = lax.bitcast_convert_type(x.astype(BF16).astype(F32), jnp.uint32)
    return (bits[:, :w] >> 16) | (bits[:, w:] & jnp.uint32(_HI_HALFWORD))


def _unpack_halves(words):
    lo = lax.bitcast_convert_type(words << 16, F32)
    hi = lax.bitcast_convert_type(words & jnp.uint32(_HI_HALFWORD), F32)
    return lo, hi


def _const_spec(shape):
    n = len(shape)
    return pl.BlockSpec(shape, lambda *_: (0,) * n, pipeline_mode=pl.Buffered(1))


def _params(sem):
    return pltpu.CompilerParams(dimension_semantics=sem, vmem_limit_bytes=VMEM_LIMIT)


def _in_proj_kernel(x_ref, w1_ref, qan_ref, kvan_ref, wq_ref, wqs_ref, wk_ref, wv_ref,
                    cosq_ref, sinq_ref, cosk_ref, sink_ref, abp_ref,
                    q_ref, k_ref, v_ref, dn_ref, z_ref, gates_ref, gb_ref):
    xb = x_ref[...].astype(BF16)

    def proj(lo, hi):
        return _dot(xb, w1_ref[:, lo:hi])

    def rms(c, g):
        return (c * lax.rsqrt(jnp.mean(c * c, axis=-1, keepdims=True) + 1e-6) * g).astype(BF16)

    cqn = rms(proj(_C_CQ, _C_CKV), qan_ref[...])
    qa = _dot(cqn, wq_ref[...])
    qb = _dot(cqn, wqs_ref[...])
    ckvn = rms(proj(_C_CKV, _C_KR), kvan_ref[...])
    kw = _dot(ckvn, wk_ref[...])
    vw = _dot(ckvn, wv_ref[...])
    kr = proj(_C_KR, _C_KRS) * cosk_ref[...] + proj(_C_KRS, _C_AB0) * sink_ref[...]
    cosq = cosq_ref[...]
    sinq = sinq_ref[...]
    lane = lax.broadcasted_iota(jnp.int32, (1, HEAD_PAD), 1)
    ones_col = (lane == V_DIM).astype(F32)
    for h in range(MLA_HEADS):
        sl = slice(h * HEAD_PAD, (h + 1) * HEAD_PAD)
        q_ref[:, sl] = (qa[:, sl] * cosq + qb[:, sl] * sinq).astype(BF16)
        k_ref[:, sl] = (kw[:, sl] + kr).astype(BF16)
        v_ref[:, sl] = (vw[:, sl] + ones_col).astype(BF16)

    for d, c0 in enumerate((_C_AB0, _C_AB1)):
        ab = proj(c0, c0 + LANES)
        neg_a = abp_ref[2 * d:2 * d + 1, :]
        dtb = abp_ref[2 * d + 1:2 * d + 2, :]
        t = ab + dtb
        sp = jnp.maximum(t, 0.0) + jnp.log(1.0 + jnp.exp(-jnp.abs(t)))
        g = neg_a * sp
        beta = 1.0 / (1.0 + jnp.exp(-ab))
        gb_ref[d] = jnp.where(lane < DN_HEADS, g, beta)[:, :2 * DN_HEADS]

    dn_ref[...] = proj(_C_DNQ, _C_Z).astype(BF16)
    z_ref[...] = proj(_C_Z, _C_GATE).astype(BF16)
    gl = proj(_C_GATE, _C_END)
    gates_ref[...] = (1.0 / (1.0 + jnp.exp(-gl))).astype(BF16)


def _in_proj(x2, S, W, tm=512):
    T = x2.shape[0]
    nseq = S // tm
    row = lambda i: (i, 0)
    pos = lambda i: (i % nseq, 0)
    dn_w = _C_Z - _C_DNQ
    return pl.pallas_call(
        _in_proj_kernel,
        grid=(T // tm,),
        in_specs=[
            pl.BlockSpec((tm, D_MODEL), row),
            _const_spec((D_MODEL, _C_END)),
            _const_spec((1, Q_LORA)), _const_spec((1, KV_LORA)),
            _const_spec((Q_LORA, MLA_HEADS * HEAD_PAD)), _const_spec((Q_LORA, MLA_HEADS * HEAD_PAD)),
            _const_spec((KV_LORA, MLA_HEADS * HEAD_PAD)), _const_spec((KV_LORA, MLA_HEADS * HEAD_PAD)),
            pl.BlockSpec((tm, HEAD_PAD), pos), pl.BlockSpec((tm, HEAD_PAD), pos),
            pl.BlockSpec((tm, HEAD_PAD), pos), pl.BlockSpec((tm, HEAD_PAD), pos),
            _const_spec((8, LANES)),
        ],
        out_specs=[
            pl.BlockSpec((tm, MLA_HEADS * HEAD_PAD), row),
            pl.BlockSpec((tm, MLA_HEADS * HEAD_PAD), row),
            pl.BlockSpec((tm, MLA_HEADS * HEAD_PAD), row),
            pl.BlockSpec((tm, dn_w), row),
            pl.BlockSpec((tm, DN_HEADS * DN_DV), row),
            pl.BlockSpec((tm, 2 * D_MODEL), row),
            pl.BlockSpec((2, tm, 2 * DN_HEADS), lambda i: (0, i, 0)),
        ],
        out_shape=[
            jax.ShapeDtypeStruct((T, MLA_HEADS * HEAD_PAD), BF16),
            jax.ShapeDtypeStruct((T, MLA_HEADS * HEAD_PAD), BF16),
            jax.ShapeDtypeStruct((T, MLA_HEADS * HEAD_PAD), BF16),
            jax.ShapeDtypeStruct((T, dn_w), BF16),
            jax.ShapeDtypeStruct((T, DN_HEADS * DN_DV), BF16),
            jax.ShapeDtypeStruct((T, 2 * D_MODEL), BF16),
            jax.ShapeDtypeStruct((2, T, 2 * DN_HEADS), F32),
        ],
        compiler_params=_params(("parallel",)),
        name="in_proj",
    )(x2, W["w1"], W["qan"], W["kvan"], W["wq"], W["wqs"], W["wk"], W["wv"],
      W["cosq"], W["sinq"], W["cosk"], W["sink"], W["abp"])


def _attn_kernel(q_ref, k_ref, v_ref, o_ref, *, tk, unroll):
    tq = q_ref.shape[0]
    S = k_ref.shape[0]
    q = q_ref[...]

    def body(j, carry):
        m, acc = carry
        off = pl.multiple_of(j * tk, tk)
        s = _dot_nt(q, k_ref[pl.ds(off, tk), :])
        m_new = jnp.maximum(m, jnp.max(s, axis=-1, keepdims=True))
        p = jnp.exp2(s - m_new).astype(BF16)
        acc = acc * jnp.exp2(m - m_new) + _dot(p, v_ref[pl.ds(off, tk), :])
        return m_new, acc

    m0 = jnp.full((tq, 1), -1e30, F32)
    acc0 = jnp.zeros((tq, HEAD_PAD), F32)
    _, acc = lax.fori_loop(0, S // tk, body, (m0, acc0), unroll=unroll)
    o_ref[...] = (acc / acc[:, V_DIM:V_DIM + 1]).astype(BF16)


def _attention(q, k, v, B, S, tq=1024, tk=2048, unroll=4):
    T = q.shape[0]
    tq = min(tq, S)
    tk = min(tk, S)
    nq = S // tq
    return pl.pallas_call(
        functools.partial(_attn_kernel, tk=tk, unroll=unroll),
        grid=(B, MLA_HEADS, nq),
        in_specs=[
            pl.BlockSpec((tq, HEAD_PAD), lambda b, h, i: (b * nq + i, h)),
            pl.BlockSpec((S, HEAD_PAD), lambda b, h, i: (b, h)),
            pl.BlockSpec((S, HEAD_PAD), lambda b, h, i: (b, h)),
        ],
        out_specs=pl.BlockSpec((tq, HEAD_PAD), lambda b, h, i: (b * nq + i, h)),
        out_shape=jax.ShapeDtypeStruct((T, MLA_HEADS * HEAD_PAD), BF16),
        compiler_params=_params(("parallel", "parallel", "arbitrary")),
        name="attention",
    )(q, k, v)


_HALO = 16


def _dn_prep_kernel(x_ref, prev_ref, next_ref, cw_ref, eq_ref, eqt_ref, evk_ref, evkt_ref,
                    gb_ref, tri_ref, qn_ref, vk_ref, aux_ref, *, nseq):
    i = pl.program_id(0)
    tp = x_ref.shape[0]
    first = (i % nseq) == 0
    last = (i % nseq) == nseq - 1
    prev = jnp.where(first, 0.0, prev_ref[...].astype(F32))
    nxt = jnp.where(last, 0.0, next_ref[...].astype(F32))
    xe = jnp.concatenate([prev, x_ref[...].astype(F32), nxt], axis=0)
    n = tp + 2 * _HALO
    y = None
    for j in range(CONV_K):
        shift = (CONV_K // 2 - j) % n
        xs = xe if shift == 0 else pltpu.roll(xe, shift, axis=0)
        term = xs[_HALO:_HALO + tp, :] * cw_ref[j:j + 1, :]
        y = term if y is None else y + term
    y = y * (1.0 / (1.0 + jnp.exp(-y)))

    def group_scale(v, e_ref, et_ref):
        sq = v * v
        hi, lo = _split_bf16(sq)
        ss = _dot(hi, e_ref[...]) + _dot(lo, e_ref[...])
        inv = lax.rsqrt(ss + 1e-6)
        ih, il = _split_bf16(inv)
        return _dot(ih, et_ref[...]) + _dot(il, et_ref[...])

    nq = DN_HEADS * DN_DK
    yq = y[:, :nq]
    qn_ref[...] = (yq * group_scale(yq, eq_ref, eqt_ref) * (DN_DK ** -0.5)).astype(BF16)
    yvk = y[:, nq:]
    sc = group_scale(yvk, evk_ref, evkt_ref)
    lane = lax.broadcasted_iota(jnp.int32, (1, yvk.shape[1]), 1)
    is_k = (lane // DN_DV) % 2 == 1
    vk_ref[...] = (yvk * jnp.where(is_k, sc, 1.0)).astype(BF16)

    hcol = lax.broadcasted_iota(jnp.int32, (1, 2 * DN_HEADS), 1) < DN_HEADS
    for d in range(2):
        gb = gb_ref[d]
        for blk in range(tp // DN_BLOCK):
            rs = slice(blk * DN_BLOCK, (blk + 1) * DN_BLOCK)
            g = gb[rs]
            cs = jnp.dot(tri_ref[d], g, preferred_element_type=F32, precision=lax.Precision.HIGHEST)
            aux_ref[d, rs, :] = jnp.where(hcol, cs, g)


def _dn_prep(dn, gb, S, W, tp=256):
    T = dn.shape[0]
    nseq = S // tp
    hb = tp // _HALO
    nh = T // _HALO
    dn_w = dn.shape[1]
    nq = DN_HEADS * DN_DK
    return pl.pallas_call(
        functools.partial(_dn_prep_kernel, nseq=nseq),
        grid=(T // tp,),
        in_specs=[
            pl.BlockSpec((tp, dn_w), lambda i: (i, 0)),
            pl.BlockSpec((_HALO, dn_w), lambda i: (jnp.maximum(i * hb - 1, 0), 0)),
            pl.BlockSpec((_HALO, dn_w), lambda i: (jnp.minimum((i + 1) * hb, nh - 1), 0)),
            _const_spec((8, dn_w)),
            _const_spec((nq, LANES)), _const_spec((LANES, nq)),
            _const_spec((dn_w - nq, LANES)), _const_spec((LANES, dn_w - nq)),
            pl.BlockSpec((2, tp, 2 * DN_HEADS), lambda i: (0, i, 0)),
            _const_spec((2, DN_BLOCK, DN_BLOCK)),
        ],
        out_specs=[
            pl.BlockSpec((tp, nq), lambda i: (i, 0)),
            pl.BlockSpec((tp, dn_w - nq), lambda i: (i, 0)),
            pl.BlockSpec((2, tp, 2 * DN_HEADS), lambda i: (0, i, 0)),
        ],
        out_shape=[
            jax.ShapeDtypeStruct((T, nq), BF16),
            jax.ShapeDtypeStruct((T, dn_w - nq), BF16),
            jax.ShapeDtypeStruct((2, T, 2 * DN_HEADS), F32),
        ],
        compiler_params=_params(("parallel",)),
        name="dn_prep",
    )(dn, dn, dn, W["convw"], W["eq"], W["eqt"], W["evk"], W["evkt"], gb, W["tri"])


_MERGE_SIZES = (16, 32, 64, 128)


def _active_blocks(d, s):
    return range(1 - d, DN_BLOCK // s, 2)


def _dn_masks(d):
    r = np.arange(DN_BLOCK)[:, None]
    c = np.arange(DN_BLOCK)[None, :]
    rr, cc = (r, c) if d == 0 else (c, r)
    tri = np.stack([rr >= cc, rr > cc]).astype(np.float32)
    diag16 = ((r // 16) == (c // 16)).astype(np.float32)
    offc = []
    for s in _MERGE_SIZES:
        full = ((rr // s) % 2 == 1) & ((rr // s) == (cc // s) + 1)
        rows = np.concatenate([np.arange(b * s, (b + 1) * s) for b in _active_blocks(d, s)])
        assert not np.delete(full, rows, axis=0).any()
        offc.append(full[rows].astype(np.float32))
    return tri, diag16, np.stack(offc)


def _deltanet_kernel(qn_ref, vk_ref, kt_ref, aux_ref, auxt_ref, tri_ref, diag16_ref, offc_ref, o_ref,
                     s_ref, nm_ref, p_ref, x_ref, aqk_ref, uw_ref, vn_ref, *, d):
    i = pl.program_id(1)
    C = DN_BLOCK
    heads = range(DN_HEADS)

    @pl.when(i == 0)
    def _():
        s_ref[...] = jnp.zeros_like(s_ref)

    rowi = lax.broadcasted_iota(jnp.int32, (C, C), 0)
    coli = lax.broadcasted_iota(jnp.int32, (C, C), 1)
    eye = (rowi == coli).astype(F32)
    lane = lax.broadcasted_iota(jnp.int32, (1, LANES), 1)
    r64 = lax.broadcasted_iota(jnp.int32, (DN_DK, LANES), 0)
    c64 = lax.broadcasted_iota(jnp.int32, (DN_DK, LANES), 1)
    zeros_kt = jnp.zeros((DN_DK, C), BF16)
    zeros_s = jnp.zeros((DN_DK, LANES), F32)

    def q_pair(h):
        return qn_ref[:, (h // 2) * LANES:(h // 2 + 1) * LANES]

    def vk_head(h):
        return vk_ref[:, h * LANES:(h + 1) * LANES]

    def kt_head(h):
        return kt_ref[h * DN_DK:(h + 1) * DN_DK, :]

    def gc_col(h):
        return aux_ref[0, :, h:h + 1]

    def beta_col(h):
        return aux_ref[0, :, DN_HEADS + h:DN_HEADS + h + 1]

    def gc_row(h):
        return auxt_ref[0, h:h + 1, :]

    for h in heads:
        kt = kt_head(h)
        kt_for_q = jnp.concatenate([kt, zeros_kt] if h % 2 == 0 else [zeros_kt, kt], axis=0)
        kt_for_k = jnp.concatenate([zeros_kt, kt], axis=0)
        qk = _dot(q_pair(h), kt_for_q)
        kk = _dot(vk_head(h), kt_for_k)
        e0 = jnp.exp(jnp.minimum(gc_col(h) - gc_row(h), 0.0))
        aqk_ref[h] = (qk * e0 * tri_ref[0]).astype(BF16)
        nm_ref[h] = (-(kk * e0 * tri_ref[1]) * beta_col(h)).astype(BF16)

    diag16 = diag16_ref[...]
    for h in heads:
        nd = nm_ref[h] * diag16
        p_ref[h] = (eye + nd.astype(F32)).astype(BF16)
        x_ref[h] = _dot(nd, nd).astype(BF16)
    for it in range(3):
        for h in heads:
            pb = p_ref[h]
            sqb = x_ref[h]
            p_ref[h] = (pb.astype(F32) + _dot(pb, sqb)).astype(BF16)
            if it < 2:
                x_ref[h] = _dot(sqb, sqb).astype(BF16)
    for k, s in enumerate(_MERGE_SIZES):
        blocks = list(_active_blocks(d, s))
        zeros_blk = jnp.zeros((s, C), BF16)

        def active_rows(ref, h):
            return jnp.concatenate([ref[h, b * s:(b + 1) * s, :] for b in blocks], axis=0)

        for h in heads:
            xc = _dot(active_rows(nm_ref, h) * offc_ref[k], p_ref[h]).astype(BF16)
            pieces = []
            for j in range(len(blocks)):
                piece = xc[j * s:(j + 1) * s, :]
                pieces += [zeros_blk, piece] if d == 0 else [piece, zeros_blk]
            x_ref[h] = jnp.concatenate(pieces, axis=0)
        for h in heads:
            pr = active_rows(p_ref, h)
            prn = (pr.astype(F32) + _dot(pr, x_ref[h])).astype(BF16)
            for j, b in enumerate(blocks):
                p_ref[h, b * s:(b + 1) * s, :] = prn[j * s:(j + 1) * s, :]

    for h in heads:
        egc = jnp.exp(gc_col(h))
        rhs = (vk_head(h).astype(F32) * beta_col(h) * jnp.where(lane < DN_DV, 1.0, egc)).astype(BF16)
        uw_ref[h] = _dot(p_ref[h], rhs).astype(BF16)
    for h in heads:
        eye_pl = (c64 == r64 + (h % 2) * DN_DV).astype(F32)
        s_aug = jnp.concatenate([eye_pl, -s_ref[h]], axis=0).astype(BF16)
        vn_ref[h] = _dot(uw_ref[h], s_aug).astype(BF16)
    o_pair = None
    for h in heads:
        par = h % 2
        gc_r = gc_row(h)
        g_tot = gc_r[:, C - 1:C] if d == 0 else gc_r[:, 0:1]
        s_pl = s_ref[h]
        v_new = vn_ref[h]
        s_sel = jnp.concatenate([s_pl, zeros_s] if par == 0 else [zeros_s, s_pl], axis=0).astype(BF16)
        qd = (q_pair(h).astype(F32) * jnp.exp(gc_col(h))).astype(BF16)
        o_pl = _dot(qd, s_sel) + _dot(aqk_ref[h], v_new)
        kd = (kt_head(h).astype(F32) * jnp.exp(g_tot - gc_r)).astype(BF16)
        s_ref[h] = s_pl * jnp.exp(g_tot) + _dot(kd, v_new)
        if par == 0:
            o_pair = o_pl
        else:
            o_ref[:, (h // 2) * LANES:(h // 2 + 1) * LANES] = o_pair + o_pl


def _deltanet(qn, vk, kt, aux, auxt, B, S, W, d):
    T = qn.shape[0]
    C = DN_BLOCK
    nb = S // C

    def blk(b, i):
        return b * nb + (i if d == 0 else nb - 1 - i)

    nq = DN_HEADS * DN_DK
    tri, diag16, offc = W["dn_masks"][d]
    return pl.pallas_call(
        functools.partial(_deltanet_kernel, d=d),
        grid=(B, nb),
        in_specs=[
            pl.BlockSpec((C, nq), lambda b, i: (blk(b, i), 0)),
            pl.BlockSpec((C, 2 * nq), lambda b, i: (blk(b, i), 0)),
            pl.BlockSpec((nq, C), lambda b, i: (0, blk(b, i))),
            pl.BlockSpec((1, C, 2 * DN_HEADS), lambda b, i: (d, blk(b, i), 0)),
            pl.BlockSpec((1, 2 * DN_HEADS, C), lambda b, i: (d, 0, blk(b, i))),
            _const_spec((2, C, C)), _const_spec((C, C)), _const_spec((len(_MERGE_SIZES), C // 2, C)),
        ],
        out_specs=pl.BlockSpec((C, DN_HEADS * DN_DV), lambda b, i: (blk(b, i), 0)),
        out_shape=jax.ShapeDtypeStruct((T, DN_HEADS * DN_DV), F32),
        scratch_shapes=[pltpu.VMEM((DN_HEADS, DN_DK, LANES), F32)]
        + [pltpu.VMEM((DN_HEADS, C, C), BF16)] * 4
        + [pltpu.VMEM((DN_HEADS, C, LANES), BF16)] * 2,
        compiler_params=_params(("parallel", "arbitrary")),
        name="deltanet_fwd" if d == 0 else "deltanet_bwd",
    )(qn, vk, kt, aux, auxt, tri, diag16, offc)


def _layer_norm(v, g, b):
    mu = jnp.mean(v, axis=-1, keepdims=True)
    c = v - mu
    var = jnp.mean(c * c, axis=-1, keepdims=True)
    return c * lax.rsqrt(var + 1e-5) * g + b


def _mix_kernel(x_ref, attn_ref, of_ref, ob_ref, z_ref, gates_ref, p_ref,
                woa_ref, wod_ref, wout_ref, e8_ref, e8t_ref, dnorm_ref, ln1g_ref, ln1b_ref,
                wpg_ref, wpp_ref, rwh_ref, rwl_ref, rb_ref, ustrict_ref,
                r_ref, hb_ref, ti_ref, tg_ref, rank_ref, cnt_ref, run_ref):
    @pl.when(pl.program_id(0) == 0)
    def _():
        run_ref[...] = jnp.zeros_like(run_ref)

    oa = _dot(attn_ref[...], woa_ref[...])
    o = of_ref[...] + ob_ref[...]
    hi, lo = _split_bf16(o * o)
    ms = (_dot(hi, e8_ref[...]) + _dot(lo, e8_ref[...])) * (1.0 / DN_DV)
    ih, il = _split_bf16(lax.rsqrt(ms + 1e-6))
    sc = _dot(ih, e8t_ref[...]) + _dot(il, e8t_ref[...])
    zf = z_ref[...].astype(F32)
    od_in = o * sc * dnorm_ref[...] * (zf * (1.0 / (1.0 + jnp.exp(-zf))))
    od = _dot(od_in.astype(BF16), wod_ref[...])
    mix = gates_ref[:, :D_MODEL].astype(F32) * oa + gates_ref[:, D_MODEL:].astype(F32) * od
    mo = _dot(mix.astype(BF16), wout_ref[...])
    h = _layer_norm(DEEPNORM_ALPHA * x_ref[...] + mo, ln1g_ref[...], ln1b_ref[...])
    hb = h.astype(BF16)
    hb_ref[...] = _pack_halves(h)
    pg = _dot(hb, wpg_ref[...])
    pp = _dot(p_ref[...].astype(BF16), wpp_ref[...])
    r_ref[...] = DEEPNORM_ALPHA * h + pp * (1.0 / (1.0 + jnp.exp(-pg)))

    hl = (h - hb.astype(F32)).astype(BF16)
    logits = (_dot_nt(rwh_ref[...], hb) + _dot_nt(rwh_ref[...], hl)
              + _dot_nt(rwl_ref[...], hb) + rb_ref[...])
    eid = lax.broadcasted_iota(jnp.int32, logits.shape, 0)
    vals = []
    run = run_ref[:, 0:1]
    for k in range(TOP_K):
        m = jnp.max(logits, axis=0, keepdims=True)
        idx = jnp.min(jnp.where(logits == m, eid, N_EXPERTS), axis=0, keepdims=True)
        ti_ref[k:k + 1, :] = idx
        vals.append(m)
        hit = eid == idx
        logits = jnp.where(hit, -jnp.inf, logits)
        onehot = hit.astype(F32)
        earlier = _dot(onehot.astype(BF16), ustrict_ref[...])
        rank = jnp.sum(onehot * (run + earlier), axis=0, keepdims=True)
        rank_ref[k:k + 1, :] = rank.astype(jnp.int32)
        run = run + jnp.sum(onehot, axis=1, keepdims=True)
    run_ref[...] = jnp.broadcast_to(run, run_ref.shape)
    cnt_ref[...] = jnp.broadcast_to(run, cnt_ref.shape)
    es = [jnp.exp(v - vals[0]) for v in vals]
    den = es[0] + es[1] + es[2] + es[3]
    for k in range(TOP_K):
        tg_ref[k:k + 1, :] = es[k] / den


def _mix(x2, attn, o_f, o_b, z, gates, p2, W):
    tm = MIX_TILE
    T = x2.shape[0]
    row = lambda i: (i, 0)
    nd = DN_HEADS * DN_DV
    return pl.pallas_call(
        _mix_kernel,
        grid=(T // tm,),
        in_specs=[
            pl.BlockSpec((tm, D_MODEL), row),
            pl.BlockSpec((tm, MLA_HEADS * HEAD_PAD), row),
            pl.BlockSpec((tm, nd), row),
            pl.BlockSpec((tm, nd), row),
            pl.BlockSpec((tm, nd), row),
            pl.BlockSpec((tm, 2 * D_MODEL), row),
            pl.BlockSpec((tm, PLE_DIM), row),
            _const_spec((MLA_HEADS * HEAD_PAD, D_MODEL)), _const_spec((nd, D_MODEL)),
            _const_spec((D_MODEL, D_MODEL)),
            _const_spec((nd, LANES)), _const_spec((LANES, nd)), _const_spec((1, nd)),
            _const_spec((1, D_MODEL)), _const_spec((1, D_MODEL)),
            _const_spec((D_MODEL, D_MODEL)), _const_spec((PLE_DIM, D_MODEL)),
            _const_spec((N_EXPERTS, D_MODEL)), _const_spec((N_EXPERTS, D_MODEL)),
            _const_spec((N_EXPERTS, 1)),
            _const_spec((tm, tm)),
        ],
        out_specs=[
            pl.BlockSpec((tm, D_MODEL), row),
            pl.BlockSpec((tm, D_MODEL // 2), row),
            pl.BlockSpec((TOP_K, tm), lambda i: (0, i)),
            pl.BlockSpec((TOP_K, tm), lambda i: (0, i)),
            pl.BlockSpec((TOP_K, tm), lambda i: (0, i)),
            _const_spec((N_EXPERTS, LANES)),
        ],
        out_shape=[
            jax.ShapeDtypeStruct((T, D_MODEL), F32),
            jax.ShapeDtypeStruct((T, D_MODEL // 2), jnp.uint32),
            jax.ShapeDtypeStruct((TOP_K, T), jnp.int32),
            jax.ShapeDtypeStruct((TOP_K, T), F32),
            jax.ShapeDtypeStruct((TOP_K, T), jnp.int32),
            jax.ShapeDtypeStruct((N_EXPERTS, LANES), F32),
        ],
        scratch_shapes=[pltpu.VMEM((N_EXPERTS, LANES), F32)],
        compiler_params=_params(("arbitrary",)),
        name="mix",
    )(x2, attn, o_f, o_b, z, gates, p2, W["woa"], W["wod"], W["wout"], W["e8"], W["e8t"], W["dnorm"],
      W["ln1g"], W["ln1b"], W["wpg"], W["wpp"], W["rwh"], W["rwl"], W["rb"], W["ustrict"])


def _moe_kernel(blk_e_ref, nvalid_ref, xs_ref, wgu_ref, bgu_ref, wd_ref, bd_ref, y_ref, *, fc):
    i = pl.program_id(0)

    @pl.when(i < nvalid_ref[0])
    def _():
        xs = jnp.concatenate(_unpack_halves(xs_ref[...]), axis=1).astype(BF16)
        acc = None
        for c in range(D_FF // fc):
            lo, hi = c * fc, (c + 1) * fc
            gate = _dot(xs, wgu_ref[0, :, lo:hi]) + bgu_ref[0, :, lo:hi]
            up = _dot(xs, wgu_ref[0, :, D_FF + lo:D_FF + hi]) + bgu_ref[0, :, D_FF + lo:D_FF + hi]
            gate = jnp.minimum(gate, SWIGLU_LIMIT)
            up = jnp.clip(up, -SWIGLU_LIMIT, SWIGLU_LIMIT)
            act = gate * (1.0 / (1.0 + jnp.exp(-SWIGLU_ALPHA * gate))) * (up + 1.0)
            part = _dot(act.astype(BF16), wd_ref[0, lo:hi, :])
            acc = part if acc is None else acc + part
        y_ref[...] = _pack_halves(acc + bd_ref[0])

    @pl.when(i >= nvalid_ref[0])
    def _():
        y_ref[...] = jnp.zeros_like(y_ref)


def _moe(xs, blk_e, nvalid, W, bm, fc=256):
    P = xs.shape[0]
    grid_spec = pltpu.PrefetchScalarGridSpec(
        num_scalar_prefetch=2,
        grid=(P // bm,),
        in_specs=[
            pl.BlockSpec((bm, D_MODEL // 2), lambda i, be, nv: (i, 0)),
            pl.BlockSpec((1, D_MODEL, 2 * D_FF), lambda i, be, nv: (be[i], 0, 0)),
            pl.BlockSpec((1, 1, 2 * D_FF), lambda i, be, nv: (be[i], 0, 0)),
            pl.BlockSpec((1, D_FF, D_MODEL), lambda i, be, nv: (be[i], 0, 0)),
            pl.BlockSpec((1, 1, D_MODEL), lambda i, be, nv: (be[i], 0, 0)),
        ],
        out_specs=pl.BlockSpec((bm, D_MODEL // 2), lambda i, be, nv: (i, 0)),
    )
    return pl.pallas_call(
        functools.partial(_moe_kernel, fc=fc),
        grid_spec=grid_spec,
        out_shape=jax.ShapeDtypeStruct((P, D_MODEL // 2), jnp.uint32),
        compiler_params=_params(("arbitrary",)),
        name="moe",
    )(blk_e, nvalid, xs, W["wgu"], W["bgu"], W["wd"], W["bd"])


def _final_kernel(r_ref, yg_ref, tg_ref, g_ref, b_ref, y_ref):
    ffn_lo = ffn_hi = None
    for k in range(TOP_K):
        lo, hi = _unpack_halves(yg_ref[k])
        g = tg_ref[:, k:k + 1]
        ffn_lo = lo * g if ffn_lo is None else ffn_lo + lo * g
        ffn_hi = hi * g if ffn_hi is None else ffn_hi + hi * g
    acc = r_ref[...] + jnp.concatenate([ffn_lo, ffn_hi], axis=1)
    y_ref[...] = _layer_norm(acc, g_ref[...], b_ref[...])


def _final(r, yg, tg, W, tm=512):
    T = r.shape[0]
    tm = min(tm, T)
    row = lambda i: (i, 0)
    return pl.pallas_call(
        _final_kernel,
        grid=(T // tm,),
        in_specs=[pl.BlockSpec((tm, D_MODEL), row),
                  pl.BlockSpec((TOP_K, tm, D_MODEL // 2), lambda i: (0, i, 0)),
                  pl.BlockSpec((tm, TOP_K), row),
                  _const_spec((1, D_MODEL)), _const_spec((1, D_MODEL))],
        out_specs=pl.BlockSpec((tm, D_MODEL), row),
        out_shape=jax.ShapeDtypeStruct((T, D_MODEL), F32),
        compiler_params=_params(("parallel",)),
        name="final_ln",
    )(r, yg, tg, W["ln2g"], W["ln2b"])


def _pad_heads(w, n_heads, width, start, size, dst=0):
    K = w.shape[0]
    w3 = w.reshape(K, n_heads, width)[:, :, start:start + size]
    out = jnp.zeros((K, n_heads, HEAD_PAD), w.dtype)
    out = out.at[:, :, dst:dst + size].set(w3)
    return out.reshape(K, n_heads * HEAD_PAD)


def _prep_weights(w_in, q_a_norm, w_uq, kv_a_norm, w_ukv, w_o_attn, dn_conv, dn_a_log, dn_dt_bias,
                  dn_norm, w_o_dn, w_out, ln1_g, ln1_b, router_w, router_b, w_gate_up, b_gate_up,
                  w_down, b_down, ple_w_proj, ple_w_gate, ln2_g, ln2_b):
    W = {}
    half = ROPE_DIM // 2
    o = 0
    cq = w_in[:, o:o + Q_LORA]; o += Q_LORA
    ckv = w_in[:, o:o + KV_LORA]; o += KV_LORA
    kr = w_in[:, o:o + ROPE_DIM]; o += ROPE_DIM
    nqk = DN_HEADS * DN_DK
    dq = w_in[:, o:o + nqk]; o += nqk
    dk = w_in[:, o:o + nqk]; o += nqk
    dv = w_in[:, o:o + DN_HEADS * DN_DV]; o += DN_HEADS * DN_DV
    dz = w_in[:, o:o + DN_HEADS * DN_DV]; o += DN_HEADS * DN_DV
    da = w_in[:, o:o + 2 * DN_HEADS]; o += 2 * DN_HEADS
    db = w_in[:, o:o + 2 * DN_HEADS]; o += 2 * DN_HEADS
    gate = w_in[:, o:o + 2 * D_MODEL]

    def lane_block(parts):
        w = jnp.concatenate(parts, axis=1)
        return jnp.pad(w, ((0, 0), (0, LANES - w.shape[1])))

    zeros64 = jnp.zeros((D_MODEL, NOPE_DIM), F32)
    kr_blk = lane_block([zeros64, kr])
    krs_blk = lane_block([zeros64, kr[:, half:], kr[:, :half]])
    ab = [lane_block([da[:, d * DN_HEADS:(d + 1) * DN_HEADS], db[:, d * DN_HEADS:(d + 1) * DN_HEADS]])
          for d in range(2)]

    def interleave_vk(v, k):
        lead = v.shape[:-1]
        v3 = v.reshape(lead + (DN_HEADS, DN_DV))
        k3 = k.reshape(lead + (DN_HEADS, DN_DK))
        return jnp.concatenate([v3, k3], axis=-1).reshape(lead + (DN_HEADS * (DN_DV + DN_DK),))

    W["w1"] = jnp.concatenate([cq, ckv, kr_blk, krs_blk, ab[0], ab[1], dq, interleave_vk(dv, dk), dz, gate],
                              axis=1).astype(BF16)
    W["qan"] = q_a_norm.reshape(1, Q_LORA)
    W["kvan"] = kv_a_norm.reshape(1, KV_LORA)

    qw = NOPE_DIM + ROPE_DIM
    wq_nope = _pad_heads(w_uq, MLA_HEADS, qw, 0, NOPE_DIM, 0)
    wq_r1 = _pad_heads(w_uq, MLA_HEADS, qw, NOPE_DIM, half, NOPE_DIM)
    wq_r2 = _pad_heads(w_uq, MLA_HEADS, qw, NOPE_DIM + half, half, NOPE_DIM + half)
    W["wq"] = (wq_nope + wq_r1 + wq_r2).astype(BF16)
    wq_s1 = _pad_heads(w_uq, MLA_HEADS, qw, NOPE_DIM + half, half, NOPE_DIM)
    wq_s2 = _pad_heads(w_uq, MLA_HEADS, qw, NOPE_DIM, half, NOPE_DIM + half)
    W["wqs"] = (wq_s1 + wq_s2).astype(BF16)
    kvw = NOPE_DIM + V_DIM
    W["wk"] = _pad_heads(w_ukv, MLA_HEADS, kvw, 0, NOPE_DIM, 0).astype(BF16)
    W["wv"] = _pad_heads(w_ukv, MLA_HEADS, kvw, NOPE_DIM, V_DIM, 0).astype(BF16)

    neg_a = -jnp.exp(dn_a_log.astype(F32))
    abp = jnp.zeros((8, LANES), F32)
    for d in range(2):
        abp = abp.at[2 * d, :DN_HEADS].set(neg_a[d])
        abp = abp.at[2 * d + 1, :DN_HEADS].set(dn_dt_bias[d].astype(F32))
    W["abp"] = abp

    cw = jnp.concatenate([dn_conv[:, :nqk], interleave_vk(dn_conv[:, 2 * nqk:], dn_conv[:, nqk:2 * nqk])], axis=1)
    W["convw"] = jnp.pad(cw.astype(F32), ((0, 8 - CONV_K), (0, 0)))

    def group_indicator(width, group):
        e = (np.arange(width)[:, None] // group == np.arange(LANES)[None, :]).astype(np.float32)
        return e

    eq = group_indicator(nqk, DN_DK)
    W["eq"] = jnp.asarray(eq, BF16)
    W["eqt"] = jnp.asarray(eq.T, BF16)
    evk = group_indicator(2 * nqk, DN_DK)
    W["evk"] = jnp.asarray(evk, BF16)
    W["evkt"] = jnp.asarray(evk.T, BF16)
    W["e8"] = W["eq"]
    W["e8t"] = W["eqt"]
    r = np.arange(DN_BLOCK)
    W["tri"] = jnp.asarray(np.stack([r[:, None] >= r[None, :], r[:, None] <= r[None, :]]).astype(np.float32))
    W["dn_masks"] = [(jnp.asarray(t), jnp.asarray(g, BF16), jnp.asarray(o, BF16))
                     for t, g, o in (_dn_masks(0), _dn_masks(1))]
    rt = np.arange(MIX_TILE)
    W["ustrict"] = jnp.asarray((rt[:, None] < rt[None, :]).astype(np.float32), BF16)

    woa = w_o_attn.reshape(MLA_HEADS, V_DIM, D_MODEL)
    woa = jnp.pad(woa, ((0, 0), (0, HEAD_PAD - V_DIM), (0, 0)))
    W["woa"] = woa.reshape(MLA_HEADS * HEAD_PAD, D_MODEL).astype(BF16)
    W["wod"] = w_o_dn.astype(BF16)
    W["wout"] = w_out.astype(BF16)
    W["dnorm"] = jnp.tile(dn_norm.astype(F32), DN_HEADS).reshape(1, DN_HEADS * DN_DV)
    W["ln1g"] = ln1_g.reshape(1, D_MODEL)
    W["ln1b"] = ln1_b.reshape(1, D_MODEL)
    W["ln2g"] = ln2_g.reshape(1, D_MODEL)
    W["ln2b"] = ln2_b.reshape(1, D_MODEL)
    W["wpg"] = ple_w_gate.astype(BF16)
    W["wpp"] = ple_w_proj.astype(BF16)
    rwt = router_w.T.astype(F32)
    W["rwh"], W["rwl"] = _split_bf16(rwt)
    W["rb"] = router_b.reshape(N_EXPERTS, 1).astype(F32)
    W["wgu"] = w_gate_up.astype(BF16)
    W["bgu"] = b_gate_up.reshape(N_EXPERTS, 1, 2 * D_FF).astype(F32)
    W["wd"] = w_down.astype(BF16)
    W["bd"] = b_down.reshape(N_EXPERTS, 1, D_MODEL).astype(F32)
    return W


def _rope_tables(S):
    half = ROPE_DIM // 2
    inv = ROPE_THETA ** (-jnp.arange(0, ROPE_DIM, 2, dtype=F32) / ROPE_DIM)
    ang = jnp.arange(S, dtype=F32)[:, None] * inv[None, :]
    cos, sin = jnp.cos(ang), jnp.sin(ang)
    c = (NOPE_DIM + ROPE_DIM) ** -0.5 * math.log2(math.e)
    pad = jnp.zeros((S, HEAD_PAD - NOPE_DIM - ROPE_DIM), F32)
    cos_blk = jnp.concatenate([cos, cos, pad], axis=1)
    sin_blk = jnp.concatenate([-sin, sin, pad], axis=1)
    cosq = jnp.concatenate([jnp.ones((S, NOPE_DIM), F32), cos_blk], axis=1) * c
    sinq = jnp.concatenate([jnp.zeros((S, NOPE_DIM), F32), sin_blk], axis=1) * c
    cosk = jnp.concatenate([jnp.zeros((S, NOPE_DIM), F32), cos_blk], axis=1)
    sink = jnp.concatenate([jnp.zeros((S, NOPE_DIM), F32), sin_blk], axis=1)
    return cosq, sinq, cosk, sink


def _dest_kernel(pstart_ref, ti_ref, rank_ref, dest_ref):
    ti = ti_ref[...]
    dest = rank_ref[...]
    for e in range(N_EXPERTS):
        dest = dest + jnp.where(ti == e, pstart_ref[e], 0)
    dest_ref[...] = dest


def _dest(p_start, top_i, rank, tile=8192):
    T = top_i.shape[1]
    tile = min(tile, T)
    spec = pl.BlockSpec((TOP_K, tile), lambda i, ps: (0, i))
    return pl.pallas_call(
        _dest_kernel,
        grid_spec=pltpu.PrefetchScalarGridSpec(num_scalar_prefetch=1, grid=(T // tile,),
                                               in_specs=[spec, spec], out_specs=spec),
        out_shape=jax.ShapeDtypeStruct((TOP_K, T), jnp.int32),
        compiler_params=_params(("parallel",)),
        name="slot_index",
    )(p_start, top_i, rank)


SC_CHUNK = 128


def _sc_mesh():
    info = plsc.get_sparse_core_info()
    mesh = plsc.VectorSubcoreMesh(core_axis_name="c", subcore_axis_name="s")
    return mesh, info.num_cores, info.num_cores * info.num_subcores


def _sc_dispatch(rows, dest3, P):
    T, D = rows.shape
    K = dest3.shape[0]
    mesh, n_cores, n_workers = _sc_mesh()
    n_chunks = T // (n_workers * SC_CHUNK)

    @functools.partial(
        pl.kernel, mesh=mesh, out_type=jax.ShapeDtypeStruct((P, D), rows.dtype),
        scratch_types=[pltpu.VMEM((K, SC_CHUNK), jnp.int32), pltpu.VMEM((SC_CHUNK, D), rows.dtype),
                       pltpu.SemaphoreType.DMA])
    def dispatch(rows_hbm, dest_hbm, out_hbm, idx_v, rows_v, sem):
        worker = lax.axis_index("s") * n_cores + lax.axis_index("c")

        @pl.loop(0, n_chunks)
        def _(j):
            g = worker * n_chunks + j
            base = pl.multiple_of(g * SC_CHUNK, SC_CHUNK)
            pltpu.sync_copy(rows_hbm.at[pl.ds(base, SC_CHUNK)], rows_v)
            pltpu.sync_copy(dest_hbm.at[:, g], idx_v)
            for k in range(K):
                pltpu.async_copy(rows_v, out_hbm.at[idx_v.at[k]], sem).wait()

    return dispatch(rows, dest3)


def _sc_gather(table, idx):
    M = idx.shape[0]
    D = table.shape[1]
    mesh, n_cores, n_workers = _sc_mesh()
    n_chunks = M // (n_workers * SC_CHUNK)

    @functools.partial(
        pl.kernel, mesh=mesh, out_type=jax.ShapeDtypeStruct((M, D), table.dtype),
        scratch_types=[pltpu.VMEM((SC_CHUNK,), jnp.int32), pltpu.VMEM((SC_CHUNK, D), table.dtype),
                       pltpu.SemaphoreType.DMA])
    def gather(table_hbm, idx_hbm, out_hbm, idx_v, rows_v, sem):
        worker = lax.axis_index("s") * n_cores + lax.axis_index("c")

        @pl.loop(0, n_chunks)
        def _(j):
            base = pl.multiple_of((worker * n_chunks + j) * SC_CHUNK, SC_CHUNK)
            pltpu.sync_copy(idx_hbm.at[pl.ds(base, SC_CHUNK)], idx_v)
            pltpu.async_copy(table_hbm.at[idx_v], rows_v, sem).wait()
            pltpu.sync_copy(rows_v, out_hbm.at[pl.ds(base, SC_CHUNK)])

    return gather(table, idx)


def _route(top_i, rank, counts, T, bm):
    A = TOP_K * T
    counts = counts.astype(jnp.int32)
    padded = ((counts + bm - 1) // bm) * bm
    p_end = jnp.cumsum(padded)
    p_start = p_end - padded
    dest = _dest(p_start, top_i, rank)
    nblk = A // bm + N_EXPERTS
    blk_start = jnp.arange(nblk, dtype=jnp.int32) * bm
    blk_e = jnp.minimum(jnp.sum(p_end[None, :] <= blk_start[:, None], axis=1), N_EXPERTS - 1).astype(jnp.int32)
    nvalid = (p_end[-1] // bm).astype(jnp.int32).reshape(1)
    return dest, nblk * bm, blk_e, nvalid


def _layer(x, p, W, bm):
    B, S, _ = x.shape
    T = B * S
    x2 = x.reshape(T, D_MODEL)
    p2 = p.reshape(T, PLE_DIM)
    Wl = dict(W)
    Wl["cosq"], Wl["sinq"], Wl["cosk"], Wl["sink"] = _rope_tables(S)

    q, k, v, dn, z, gates, gb = _in_proj(x2, S, Wl)
    attn = _attention(q, k, v, B, S)
    qn, vk, aux = _dn_prep(dn, gb, S, Wl)
    kt = vk.reshape(T, DN_HEADS, 2, DN_DK)[:, :, 1, :].reshape(T, DN_HEADS * DN_DK).T
    auxt = jnp.swapaxes(aux, 1, 2)
    o_f = _deltanet(qn, vk, kt, aux, auxt, B, S, Wl, 0)
    o_b = _deltanet(qn, vk, kt, aux, auxt, B, S, Wl, 1)
    r, hb, top_i, top_g, rank, cnt = _mix(x2, attn, o_f, o_b, z, gates, p2, Wl)

    dest, P, blk_e, nvalid = _route(top_i, rank, cnt[:, 0], T, bm)
    xs = _sc_dispatch(hb, dest.reshape(TOP_K, T // SC_CHUNK, SC_CHUNK), P)
    yb = _moe(xs, blk_e, nvalid, Wl, bm)
    yg = _sc_gather(yb, dest.reshape(TOP_K * T)).reshape(TOP_K, T, D_MODEL // 2)
    y = _final(r, yg, top_g.T, Wl)
    return y.reshape(B, S, D_MODEL)


def kernel(x_prompt, x_sample, p_prompt, p_sample, w_in, q_a_norm, w_uq, kv_a_norm, w_ukv, w_o_attn, dn_conv, dn_a_log, dn_dt_bias, dn_norm, w_o_dn, w_out, ln1_g, ln1_b, router_w, router_b, w_gate_up, b_gate_up, w_down, b_down, ple_w_proj, ple_w_gate, ln2_g, ln2_b):
    y_prompt, y_sample = x_prompt, x_sample
    for l in range(DEPTH):
        W = _prep_weights(w_in[l], q_a_norm[l], w_uq[l], kv_a_norm[l], w_ukv[l], w_o_attn[l], dn_conv[l],
                          dn_a_log[l], dn_dt_bias[l], dn_norm[l], w_o_dn[l], w_out[l], ln1_g[l], ln1_b[l],
                          router_w[l], router_b[l], w_gate_up[l], b_gate_up[l], w_down[l], b_down[l],
                          ple_w_proj[l], ple_w_gate[l], ln2_g[l], ln2_b[l])
        y_prompt = _layer(y_prompt, p_prompt[l], W, bm=MOE_BLOCK)
        y_sample = _layer(y_sample, p_sample[l], W, bm=MOE_BLOCK)
    return (y_prompt, y_sample)
```

```python
import functools
import math

import numpy as np
import jax
import jax.numpy as jnp
from jax import lax
from jax.experimental import pallas as pl
from jax.experimental.pallas import tpu as pltpu
from jax.experimental.pallas import tpu_sc as plsc

D_MODEL = 1024
MLA_HEADS = 8
Q_LORA = 256
KV_LORA = 128
NOPE_DIM = 64
ROPE_DIM = 32
V_DIM = 64
ROPE_THETA = 10000.0
DN_HEADS = 8
DN_DK = 64
DN_DV = 64
CONV_K = 5
N_EXPERTS = 32
TOP_K = 4
D_FF = 1024
SWIGLU_LIMIT = 7.0
SWIGLU_ALPHA = 1.702
PLE_DIM = 256
DEPTH = 1
DEEPNORM_ALPHA = (2.0 * DEPTH) ** 0.25

LANES = 128
HEAD_PAD = 128
DN_BLOCK = 256
DN_PREP_TILE = 256
MOE_BLOCK = 512
MIX_TILE = 512
VMEM_LIMIT = 56 * 1024 * 1024

_C_CQ = 0
_C_CKV = _C_CQ + Q_LORA
_C_KR = _C_CKV + KV_LORA
_C_KRS = _C_KR + LANES
_C_AB0 = _C_KRS + LANES
_C_AB1 = _C_AB0 + LANES
_C_DNQ = _C_AB1 + LANES
_C_DNVK = _C_DNQ + DN_HEADS * DN_DK
_C_Z = _C_DNVK + DN_HEADS * (DN_DK + DN_DV)
_C_GATE = _C_Z + DN_HEADS * DN_DV
_C_END = _C_GATE + 2 * D_MODEL

BF16 = jnp.bfloat16
F32 = jnp.float32


def _dot(a, b):
    return jnp.dot(a, b, preferred_element_type=F32)


def _dot_nt(a, b):
    return lax.dot_general(a, b, (((1,), (1,)), ((), ())), preferred_element_type=F32)


def _split_bf16(x):
    hi = x.astype(BF16)
    lo = (x - hi.astype(F32)).astype(BF16)
    return hi, lo


_HI_HALFWORD = 0xFFFF0000


def _pack_halves(x):
    w = x.shape[1] // 2
    bits = lax.bitcast_convert_type(x.astype(BF16).astype(F32), jnp.uint32)
    return (bits[:, :w] >> 16) | (bits[:, w:] & jnp.uint32(_HI_HALFWORD))


def _unpack_halves(words):
    lo = lax.bitcast_convert_type(words << 16, F32)
    hi = lax.bitcast_convert_type(words & jnp.uint32(_HI_HALFWORD), F32)
    return lo, hi


def _const_spec(shape):
    n = len(shape)
    return pl.BlockSpec(shape, lambda *_: (0,) * n, pipeline_mode=pl.Buffered(1))


def _params(sem):
    return pltpu.CompilerParams(dimension_semantics=sem, vmem_limit_bytes=VMEM_LIMIT)


def _in_proj_kernel(x_ref, w1_ref, qan_ref, kvan_ref, wq_ref, wqs_ref, wk_ref, wv_ref,
                    cosq_ref, sinq_ref, cosk_ref, sink_ref, abp_ref,
                    q_ref, k_ref, v_ref, dn_ref, z_ref, gates_ref, gb_ref):
    xb = x_ref[...].astype(BF16)

    def proj(lo, hi):
        return _dot(xb, w1_ref[:, lo:hi])

    def rms(c, g):
        return (c * lax.rsqrt(jnp.mean(c * c, axis=-1, keepdims=True) + 1e-6) * g).astype(BF16)

    cqn = rms(proj(_C_CQ, _C_CKV), qan_ref[...])
    qa = _dot(cqn, wq_ref[...])
    qb = _dot(cqn, wqs_ref[...])
    ckvn = rms(proj(_C_CKV, _C_KR), kvan_ref[...])
    kw = _dot(ckvn, wk_ref[...])
    vw = _dot(ckvn, wv_ref[...])
    kr = proj(_C_KR, _C_KRS) * cosk_ref[...] + proj(_C_KRS, _C_AB0) * sink_ref[...]
    cosq = cosq_ref[...]
    sinq = sinq_ref[...]
    lane = lax.broadcasted_iota(jnp.int32, (1, HEAD_PAD), 1)
    ones_col = (lane == V_DIM).astype(F32)
    for h in range(MLA_HEADS):
        sl = slice(h * HEAD_PAD, (h + 1) * HEAD_PAD)
        q_ref[:, sl] = (qa[:, sl] * cosq + qb[:, sl] * sinq).astype(BF16)
        k_ref[:, sl] = (kw[:, sl] + kr).astype(BF16)
        v_ref[:, sl] = (vw[:, sl] + ones_col).astype(BF16)

    for d, c0 in enumerate((_C_AB0, _C_AB1)):
        ab = proj(c0, c0 + LANES)
        neg_a = abp_ref[2 * d:2 * d + 1, :]
        dtb = abp_ref[2 * d + 1:2 * d + 2, :]
        t = ab + dtb
        sp = jnp.maximum(t, 0.0) + jnp.log(1.0 + jnp.exp(-jnp.abs(t)))
        g = neg_a * sp
        beta = 1.0 / (1.0 + jnp.exp(-ab))
        gb_ref[d] = jnp.where(lane < DN_HEADS, g, beta)[:, :2 * DN_HEADS]

    dn_ref[...] = proj(_C_DNQ, _C_Z).astype(BF16)
    z_ref[...] = proj(_C_Z, _C_GATE).astype(BF16)
    gl = proj(_C_GATE, _C_END)
    gates_ref[...] = (1.0 / (1.0 + jnp.exp(-gl))).astype(BF16)


def _in_proj(x2, S, W, tm=512):
    T = x2.shape[0]
    nseq = S // tm
    row = lambda i: (i, 0)
    pos = lambda i: (i % nseq, 0)
    dn_w = _C_Z - _C_DNQ
    return pl.pallas_call(
        _in_proj_kernel,
        grid=(T // tm,),
        in_specs=[
            pl.BlockSpec((tm, D_MODEL), row),
            _const_spec((D_MODEL, _C_END)),
            _const_spec((1, Q_LORA)), _const_spec((1, KV_LORA)),
            _const_spec((Q_LORA, MLA_HEADS * HEAD_PAD)), _const_spec((Q_LORA, MLA_HEADS * HEAD_PAD)),
            _const_spec((KV_LORA, MLA_HEADS * HEAD_PAD)), _const_spec((KV_LORA, MLA_HEADS * HEAD_PAD)),
            pl.BlockSpec((tm, HEAD_PAD), pos), pl.BlockSpec((tm, HEAD_PAD), pos),
            pl.BlockSpec((tm, HEAD_PAD), pos), pl.BlockSpec((tm, HEAD_PAD), pos),
            _const_spec((8, LANES)),
        ],
        out_specs=[
            pl.BlockSpec((tm, MLA_HEADS * HEAD_PAD), row),
            pl.BlockSpec((tm, MLA_HEADS * HEAD_PAD), row),
            pl.BlockSpec((tm, MLA_HEADS * HEAD_PAD), row),
            pl.BlockSpec((tm, dn_w), row),
            pl.BlockSpec((tm, DN_HEADS * DN_DV), row),
            pl.BlockSpec((tm, 2 * D_MODEL), row),
            pl.BlockSpec((2, tm, 2 * DN_HEADS), lambda i: (0, i, 0)),
        ],
        out_shape=[
            jax.ShapeDtypeStruct((T, MLA_HEADS * HEAD_PAD), BF16),
            jax.ShapeDtypeStruct((T, MLA_HEADS * HEAD_PAD), BF16),
            jax.ShapeDtypeStruct((T, MLA_HEADS * HEAD_PAD), BF16),
            jax.ShapeDtypeStruct((T, dn_w), BF16),
            jax.ShapeDtypeStruct((T, DN_HEADS * DN_DV), BF16),
            jax.ShapeDtypeStruct((T, 2 * D_MODEL), BF16),
            jax.ShapeDtypeStruct((2, T, 2 * DN_HEADS), F32),
        ],
        compiler_params=_params(("parallel",)),
        name="in_proj",
    )(x2, W["w1"], W["qan"], W["kvan"], W["wq"], W["wqs"], W["wk"], W["wv"],
      W["cosq"], W["sinq"], W["cosk"], W["sink"], W["abp"])


def _attn_kernel(q_ref, k_ref, v_ref, o_ref, *, tk, unroll):
    tq = q_ref.shape[0]
    S = k_ref.shape[0]
    q = q_ref[...]

    def body(j, carry):
        m, acc = carry
        off = pl.multiple_of(j * tk, tk)
        s = _dot_nt(q, k_ref[pl.ds(off, tk), :])
        m_new = jnp.maximum(m, jnp.max(s, axis=-1, keepdims=True))
        p = jnp.exp2(s - m_new).astype(BF16)
        acc = acc * jnp.exp2(m - m_new) + _dot(p, v_ref[pl.ds(off, tk), :])
        return m_new, acc

    m0 = jnp.full((tq, 1), -1e30, F32)
    acc0 = jnp.zeros((tq, HEAD_PAD), F32)
    _, acc = lax.fori_loop(0, S // tk, body, (m0, acc0), unroll=unroll)
    o_ref[...] = (acc / acc[:, V_DIM:V_DIM + 1]).astype(BF16)


def _attention(q, k, v, B, S, tq=1024, tk=2048, unroll=4):
    T = q.shape[0]
    tq = min(tq, S)
    tk = min(tk, S)
    nq = S // tq
    return pl.pallas_call(
        functools.partial(_attn_kernel, tk=tk, unroll=unroll),
        grid=(B, MLA_HEADS, nq),
        in_specs=[
            pl.BlockSpec((tq, HEAD_PAD), lambda b, h, i: (b * nq + i, h)),
            pl.BlockSpec((S, HEAD_PAD), lambda b, h, i: (b, h)),
            pl.BlockSpec((S, HEAD_PAD), lambda b, h, i: (b, h)),
        ],
        out_specs=pl.BlockSpec((tq, HEAD_PAD), lambda b, h, i: (b * nq + i, h)),
        out_shape=jax.ShapeDtypeStruct((T, MLA_HEADS * HEAD_PAD), BF16),
        compiler_params=_params(("parallel", "parallel", "arbitrary")),
        name="attention",
    )(q, k, v)


_HALO = 16


def _dn_prep_kernel(x_ref, prev_ref, next_ref, cw_ref, eq_ref, eqt_ref, evk_ref, evkt_ref,
                    gb_ref, tri_ref, shift_ref, qn_ref, vk_ref, aux_ref, *, nseq):
    i = pl.program_id(0)
    tp = x_ref.shape[0]
    first = (i % nseq) == 0
    last = (i % nseq) == nseq - 1
    half = CONV_K // 2
    xb = x_ref[...]

    y = xb.astype(F32) * cw_ref[half:half + 1, :]
    for j in range(CONV_K):
        if j != half:
            y = y + _dot(shift_ref[j], xb) * cw_ref[j:j + 1, :]

    def edge(slab, row0):
        n = slab.shape[0]
        acc = None
        for j in range(CONV_K):
            shift = (half - j) % n
            rolled = slab if shift == 0 else pltpu.roll(slab, shift, axis=0)
            term = rolled[row0:row0 + 8, :] * cw_ref[j:j + 1, :]
            acc = term if acc is None else acc + term
        return acc

    prev = jnp.where(first, 0.0, prev_ref[...].astype(F32))
    nxt = jnp.where(last, 0.0, next_ref[...].astype(F32))
    top = edge(jnp.concatenate([prev, xb[:_HALO].astype(F32)], axis=0), _HALO)
    bot = edge(jnp.concatenate([xb[tp - _HALO:].astype(F32), nxt], axis=0), _HALO - 8)
    y = jnp.concatenate([top, y[8:tp - 8], bot], axis=0)
    y = y * (1.0 / (1.0 + jnp.exp(-y)))

    def group_scale(v, e_ref, et_ref):
        ss = _dot((v * v).astype(BF16), e_ref[...])
        return _dot(lax.rsqrt(ss + 1e-6).astype(BF16), et_ref[...])

    nq = DN_HEADS * DN_DK
    yq = y[:, :nq]
    qn_ref[...] = (yq * group_scale(yq, eq_ref, eqt_ref) * (DN_DK ** -0.5)).astype(BF16)
    yvk = y[:, nq:]
    sc = group_scale(yvk, evk_ref, evkt_ref)
    lane = lax.broadcasted_iota(jnp.int32, (1, yvk.shape[1]), 1)
    is_k = (lane // DN_DV) % 2 == 1
    vk_ref[...] = (yvk * jnp.where(is_k, sc, 1.0)).astype(BF16)

    hcol = lax.broadcasted_iota(jnp.int32, (1, 2 * DN_HEADS), 1) < DN_HEADS
    for d in range(2):
        gb = gb_ref[d]
        for blk in range(tp // DN_BLOCK):
            rs = slice(blk * DN_BLOCK, (blk + 1) * DN_BLOCK)
            g = gb[rs]
            g_hi = g.astype(BF16)
            g_mid, g_lo = _split_bf16(g - g_hi.astype(F32))
            cs = _dot(tri_ref[d], g_hi) + _dot(tri_ref[d], g_mid) + _dot(tri_ref[d], g_lo)
            aux_ref[d, rs, :] = jnp.where(hcol, cs, g)


def _dn_prep(dn, gb, S, W):
    tp = DN_PREP_TILE
    T = dn.shape[0]
    nseq = S // tp
    hb = tp // _HALO
    nh = T // _HALO
    dn_w = dn.shape[1]
    nq = DN_HEADS * DN_DK
    return pl.pallas_call(
        functools.partial(_dn_prep_kernel, nseq=nseq),
        grid=(T // tp,),
        in_specs=[
            pl.BlockSpec((tp, dn_w), lambda i: (i, 0)),
            pl.BlockSpec((_HALO, dn_w), lambda i: (jnp.maximum(i * hb - 1, 0), 0)),
            pl.BlockSpec((_HALO, dn_w), lambda i: (jnp.minimum((i + 1) * hb, nh - 1), 0)),
            _const_spec((8, dn_w)),
            _const_spec((nq, LANES)), _const_spec((LANES, nq)),
            _const_spec((dn_w - nq, LANES)), _const_spec((LANES, dn_w - nq)),
            pl.BlockSpec((2, tp, 2 * DN_HEADS), lambda i: (0, i, 0)),
            _const_spec((2, DN_BLOCK, DN_BLOCK)),
            _const_spec((CONV_K, tp, tp)),
        ],
        out_specs=[
            pl.BlockSpec((tp, nq), lambda i: (i, 0)),
            pl.BlockSpec((tp, dn_w - nq), lambda i: (i, 0)),
            pl.BlockSpec((2, tp, 2 * DN_HEADS), lambda i: (0, i, 0)),
        ],
        out_shape=[
            jax.ShapeDtypeStruct((T, nq), BF16),
            jax.ShapeDtypeStruct((T, dn_w - nq), BF16),
            jax.ShapeDtypeStruct((2, T, 2 * DN_HEADS), F32),
        ],
        compiler_params=_params(("parallel",)),
        name="dn_prep",
    )(dn, dn, dn, W["convw"], W["eq"], W["eqt"], W["evk"], W["evkt"], gb, W["tri"], W["conv_shift"])


_MERGE_SIZES = (16, 32, 64, 128)


def _active_blocks(d, s):
    return range(1 - d, DN_BLOCK // s, 2)


def _dn_masks(d):
    r = np.arange(DN_BLOCK)[:, None]
    c = np.arange(DN_BLOCK)[None, :]
    rr, cc = (r, c) if d == 0 else (c, r)
    tri = np.stack([rr >= cc, rr > cc]).astype(np.float32)
    diag16 = ((r // 16) == (c // 16)).astype(np.float32)
    offc = []
    for s in _MERGE_SIZES:
        full = ((rr // s) % 2 == 1) & ((rr // s) == (cc // s) + 1)
        rows = np.concatenate([np.arange(b * s, (b + 1) * s) for b in _active_blocks(d, s)])
        assert not np.delete(full, rows, axis=0).any()
        offc.append(full[rows].astype(np.float32))
    return tri, diag16, np.stack(offc)


def _deltanet_kernel(qn_ref, vk_ref, kt_ref, aux_ref, auxt_ref, tri_ref, diag16_ref, offc_ref, o_ref,
                     s_ref, nm_ref, p_ref, x_ref, aqk_ref, uw_ref, vn_ref, *, d):
    i = pl.program_id(1)
    C = DN_BLOCK
    heads = range(DN_HEADS)

    @pl.when(i == 0)
    def _():
        s_ref[...] = jnp.zeros_like(s_ref)

    rowi = lax.broadcasted_iota(jnp.int32, (C, C), 0)
    coli = lax.broadcasted_iota(jnp.int32, (C, C), 1)
    eye = (rowi == coli).astype(F32)
    lane = lax.broadcasted_iota(jnp.int32, (1, LANES), 1)
    r64 = lax.broadcasted_iota(jnp.int32, (DN_DK, LANES), 0)
    c64 = lax.broadcasted_iota(jnp.int32, (DN_DK, LANES), 1)
    zeros_kt = jnp.zeros((DN_DK, C), BF16)
    zeros_s = jnp.zeros((DN_DK, LANES), F32)

    def q_pair(h):
        return qn_ref[:, (h // 2) * LANES:(h // 2 + 1) * LANES]

    def vk_head(h):
        return vk_ref[:, h * LANES:(h + 1) * LANES]

    def kt_head(h):
        return kt_ref[h * DN_DK:(h + 1) * DN_DK, :]

    def gc_col(h):
        return aux_ref[0, :, h:h + 1]

    def beta_col(h):
        return aux_ref[0, :, DN_HEADS + h:DN_HEADS + h + 1]

    def gc_row(h):
        return auxt_ref[0, h:h + 1, :]

    for h in heads:
        kt = kt_head(h)
        kt_for_q = jnp.concatenate([kt, zeros_kt] if h % 2 == 0 else [zeros_kt, kt], axis=0)
        kt_for_k = jnp.concatenate([zeros_kt, kt], axis=0)
        qk = _dot(q_pair(h), kt_for_q)
        kk = _dot(vk_head(h), kt_for_k)
        e0 = jnp.exp(jnp.minimum(gc_col(h) - gc_row(h), 0.0))
        aqk_ref[h] = (qk * e0 * tri_ref[0]).astype(BF16)
        nm_ref[h] = (-(kk * e0 * tri_ref[1]) * beta_col(h)).astype(BF16)

    diag16 = diag16_ref[...]
    for h in heads:
        nd = nm_ref[h] * diag16
        p_ref[h] = (eye + nd.astype(F32)).astype(BF16)
        x_ref[h] = _dot(nd, nd).astype(BF16)
    for it in range(3):
        for h in heads:
            pb = p_ref[h]
            sqb = x_ref[h]
            p_ref[h] = (pb.astype(F32) + _dot(pb, sqb)).astype(BF16)
            if it < 2:
                x_ref[h] = _dot(sqb, sqb).astype(BF16)
    for k, s in enumerate(_MERGE_SIZES):
        blocks = list(_active_blocks(d, s))
        zeros_blk = jnp.zeros((s, C), BF16)

        def active_rows(ref, h):
            return jnp.concatenate([ref[h, b * s:(b + 1) * s, :] for b in blocks], axis=0)

        for h in heads:
            xc = _dot(active_rows(nm_ref, h) * offc_ref[k], p_ref[h]).astype(BF16)
            pieces = []
            for j in range(len(blocks)):
                piece = xc[j * s:(j + 1) * s, :]
                pieces += [zeros_blk, piece] if d == 0 else [piece, zeros_blk]
            x_ref[h] = jnp.concatenate(pieces, axis=0)
        for h in heads:
            pr = active_rows(p_ref, h)
            prn = (pr.astype(F32) + _dot(pr, x_ref[h])).astype(BF16)
            for j, b in enumerate(blocks):
                p_ref[h, b * s:(b + 1) * s, :] = prn[j * s:(j + 1) * s, :]

    for h in heads:
        egc = jnp.exp(gc_col(h))
        rhs = (vk_head(h).astype(F32) * beta_col(h) * jnp.where(lane < DN_DV, 1.0, egc)).astype(BF16)
        uw_ref[h] = _dot(p_ref[h], rhs).astype(BF16)
    for h in heads:
        eye_pl = (c64 == r64 + (h % 2) * DN_DV).astype(F32)
        s_aug = jnp.concatenate([eye_pl, -s_ref[h]], axis=0).astype(BF16)
        vn_ref[h] = _dot(uw_ref[h], s_aug).astype(BF16)
    o_pair = None
    for h in heads:
        par = h % 2
        gc_r = gc_row(h)
        g_tot = gc_r[:, C - 1:C] if d == 0 else gc_r[:, 0:1]
        s_pl = s_ref[h]
        v_new = vn_ref[h]
        s_sel = jnp.concatenate([s_pl, zeros_s] if par == 0 else [zeros_s, s_pl], axis=0).astype(BF16)
        qd = (q_pair(h).astype(F32) * jnp.exp(gc_col(h))).astype(BF16)
        o_pl = _dot(qd, s_sel) + _dot(aqk_ref[h], v_new)
        kd = (kt_head(h).astype(F32) * jnp.exp(g_tot - gc_r)).astype(BF16)
        s_ref[h] = s_pl * jnp.exp(g_tot) + _dot(kd, v_new)
        if par == 0:
            o_pair = o_pl
        else:
            o_ref[:, (h // 2) * LANES:(h // 2 + 1) * LANES] = o_pair + o_pl


def _deltanet(qn, vk, kt, aux, auxt, B, S, W, d):
    T = qn.shape[0]
    C = DN_BLOCK
    nb = S // C

    def blk(b, i):
        return b * nb + (i if d == 0 else nb - 1 - i)

    nq = DN_HEADS * DN_DK
    tri, diag16, offc = W["dn_masks"][d]
    return pl.pallas_call(
        functools.partial(_deltanet_kernel, d=d),
        grid=(B, nb),
        in_specs=[
            pl.BlockSpec((C, nq), lambda b, i: (blk(b, i), 0)),
            pl.BlockSpec((C, 2 * nq), lambda b, i: (blk(b, i), 0)),
            pl.BlockSpec((nq, C), lambda b, i: (0, blk(b, i))),
            pl.BlockSpec((1, C, 2 * DN_HEADS), lambda b, i: (d, blk(b, i), 0)),
            pl.BlockSpec((1, 2 * DN_HEADS, C), lambda b, i: (d, 0, blk(b, i))),
            _const_spec((2, C, C)), _const_spec((C, C)), _const_spec((len(_MERGE_SIZES), C // 2, C)),
        ],
        out_specs=pl.BlockSpec((C, DN_HEADS * DN_DV), lambda b, i: (blk(b, i), 0)),
        out_shape=jax.ShapeDtypeStruct((T, DN_HEADS * DN_DV), F32),
        scratch_shapes=[pltpu.VMEM((DN_HEADS, DN_DK, LANES), F32)]
        + [pltpu.VMEM((DN_HEADS, C, C), BF16)] * 4
        + [pltpu.VMEM((DN_HEADS, C, LANES), BF16)] * 2,
        compiler_params=_params(("parallel", "arbitrary")),
        name="deltanet_fwd" if d == 0 else "deltanet_bwd",
    )(qn, vk, kt, aux, auxt, tri, diag16, offc)


def _layer_norm(v, g, b):
    mu = jnp.mean(v, axis=-1, keepdims=True)
    c = v - mu
    var = jnp.mean(c * c, axis=-1, keepdims=True)
    return c * lax.rsqrt(var + 1e-5) * g + b


def _mix_kernel(x_ref, attn_ref, of_ref, ob_ref, z_ref, gates_ref, p_ref,
                woa_ref, wod_ref, wout_ref, e8_ref, e8t_ref, dnorm_ref, ln1g_ref, ln1b_ref,
                wpg_ref, wpp_ref, rwh_ref, rwl_ref, rb_ref, ustrict_ref,
                r_ref, hb_ref, ti_ref, tg_ref, rank_ref, cnt_ref, run_ref):
    @pl.when(pl.program_id(0) == 0)
    def _():
        run_ref[...] = jnp.zeros_like(run_ref)

    oa = _dot(attn_ref[...], woa_ref[...])
    o = of_ref[...] + ob_ref[...]
    hi, lo = _split_bf16(o * o)
    ms = (_dot(hi, e8_ref[...]) + _dot(lo, e8_ref[...])) * (1.0 / DN_DV)
    ih, il = _split_bf16(lax.rsqrt(ms + 1e-6))
    sc = _dot(ih, e8t_ref[...]) + _dot(il, e8t_ref[...])
    zf = z_ref[...].astype(F32)
    od_in = o * sc * dnorm_ref[...] * (zf * (1.0 / (1.0 + jnp.exp(-zf))))
    od = _dot(od_in.astype(BF16), wod_ref[...])
    mix = gates_ref[:, :D_MODEL].astype(F32) * oa + gates_ref[:, D_MODEL:].astype(F32) * od
    mo = _dot(mix.astype(BF16), wout_ref[...])
    h = _layer_norm(DEEPNORM_ALPHA * x_ref[...] + mo, ln1g_ref[...], ln1b_ref[...])
    hb = h.astype(BF16)
    hb_ref[...] = _pack_halves(h)
    pg = _dot(hb, wpg_ref[...])
    pp = _dot(p_ref[...].astype(BF16), wpp_ref[...])
    r_ref[...] = DEEPNORM_ALPHA * h + pp * (1.0 / (1.0 + jnp.exp(-pg)))

    hl = (h - hb.astype(F32)).astype(BF16)
    logits = (_dot_nt(rwh_ref[...], hb) + _dot_nt(rwh_ref[...], hl)
              + _dot_nt(rwl_ref[...], hb) + rb_ref[...])
    eid = lax.broadcasted_iota(jnp.int32, logits.shape, 0)
    vals = []
    run = run_ref[:, 0:1]
    for k in range(TOP_K):
        m = jnp.max(logits, axis=0, keepdims=True)
        idx = jnp.min(jnp.where(logits == m, eid, N_EXPERTS), axis=0, keepdims=True)
        ti_ref[k:k + 1, :] = idx
        vals.append(m)
        hit = eid == idx
        logits = jnp.where(hit, -jnp.inf, logits)
        onehot = hit.astype(F32)
        earlier = _dot(onehot.astype(BF16), ustrict_ref[...])
        rank = jnp.sum(onehot * (run + earlier), axis=0, keepdims=True)
        rank_ref[k:k + 1, :] = rank.astype(jnp.int32)
        run = run + jnp.sum(onehot, axis=1, keepdims=True)
    run_ref[...] = jnp.broadcast_to(run, run_ref.shape)
    cnt_ref[...] = jnp.broadcast_to(run, cnt_ref.shape)
    es = [jnp.exp(v - vals[0]) for v in vals]
    den = es[0] + es[1] + es[2] + es[3]
    for k in range(TOP_K):
        tg_ref[k:k + 1, :] = es[k] / den


def _mix(x2, attn, o_f, o_b, z, gates, p2, W):
    tm = MIX_TILE
    T = x2.shape[0]
    row = lambda i: (i, 0)
    nd = DN_HEADS * DN_DV
    return pl.pallas_call(
        _mix_kernel,
        grid=(T // tm,),
        in_specs=[
            pl.BlockSpec((tm, D_MODEL), row),
            pl.BlockSpec((tm, MLA_HEADS * HEAD_PAD), row),
            pl.BlockSpec((tm, nd), row),
            pl.BlockSpec((tm, nd), row),
            pl.BlockSpec((tm, nd), row),
            pl.BlockSpec((tm, 2 * D_MODEL), row),
            pl.BlockSpec((tm, PLE_DIM), row),
            _const_spec((MLA_HEADS * HEAD_PAD, D_MODEL)), _const_spec((nd, D_MODEL)),
            _const_spec((D_MODEL, D_MODEL)),
            _const_spec((nd, LANES)), _const_spec((LANES, nd)), _const_spec((1, nd)),
            _const_spec((1, D_MODEL)), _const_spec((1, D_MODEL)),
            _const_spec((D_MODEL, D_MODEL)), _const_spec((PLE_DIM, D_MODEL)),
            _const_spec((N_EXPERTS, D_MODEL)), _const_spec((N_EXPERTS, D_MODEL)),
            _const_spec((N_EXPERTS, 1)),
            _const_spec((tm, tm)),
        ],
        out_specs=[
            pl.BlockSpec((tm, D_MODEL), row),
            pl.BlockSpec((tm, D_MODEL // 2), row),
            pl.BlockSpec((TOP_K, tm), lambda i: (0, i)),
            pl.BlockSpec((TOP_K, tm), lambda i: (0, i)),
            pl.BlockSpec((TOP_K, tm), lambda i: (0, i)),
            _const_spec((N_EXPERTS, LANES)),
        ],
        out_shape=[
            jax.ShapeDtypeStruct((T, D_MODEL), F32),
            jax.ShapeDtypeStruct((T, D_MODEL // 2), jnp.uint32),
            jax.ShapeDtypeStruct((TOP_K, T), jnp.int32),
            jax.ShapeDtypeStruct((TOP_K, T), F32),
            jax.ShapeDtypeStruct((TOP_K, T), jnp.int32),
            jax.ShapeDtypeStruct((N_EXPERTS, LANES), F32),
        ],
        scratch_shapes=[pltpu.VMEM((N_EXPERTS, LANES), F32)],
        compiler_params=_params(("arbitrary",)),
        name="mix",
    )(x2, attn, o_f, o_b, z, gates, p2, W["woa"], W["wod"], W["wout"], W["e8"], W["e8t"], W["dnorm"],
      W["ln1g"], W["ln1b"], W["wpg"], W["wpp"], W["rwh"], W["rwl"], W["rb"], W["ustrict"])


def _moe_kernel(blk_e_ref, nvalid_ref, xs_ref, wgu_ref, bgu_ref, wd_ref, bd_ref, y_ref, *, fc):
    i = pl.program_id(0)

    @pl.when(i < nvalid_ref[0])
    def _():
        xs = jnp.concatenate(_unpack_halves(xs_ref[...]), axis=1).astype(BF16)
        acc = None
        for c in range(D_FF // fc):
            lo, hi = c * fc, (c + 1) * fc
            gate = _dot(xs, wgu_ref[0, :, lo:hi]) + bgu_ref[0, :, lo:hi]
            up = _dot(xs, wgu_ref[0, :, D_FF + lo:D_FF + hi]) + bgu_ref[0, :, D_FF + lo:D_FF + hi]
            gate = jnp.minimum(gate, SWIGLU_LIMIT)
            up = jnp.clip(up, -SWIGLU_LIMIT, SWIGLU_LIMIT)
            act = gate * (1.0 / (1.0 + jnp.exp(-SWIGLU_ALPHA * gate))) * (up + 1.0)
            part = _dot(act.astype(BF16), wd_ref[0, lo:hi, :])
            acc = part if acc is None else acc + part
        y_ref[...] = _pack_halves(acc + bd_ref[0])

    @pl.when(i >= nvalid_ref[0])
    def _():
        y_ref[...] = jnp.zeros_like(y_ref)


def _moe(xs, blk_e, nvalid, W, bm, fc=512):
    P = xs.shape[0]
    grid_spec = pltpu.PrefetchScalarGridSpec(
        num_scalar_prefetch=2,
        grid=(P // bm,),
        in_specs=[
            pl.BlockSpec((bm, D_MODEL // 2), lambda i, be, nv: (i, 0)),
            pl.BlockSpec((1, D_MODEL, 2 * D_FF), lambda i, be, nv: (be[i], 0, 0)),
            pl.BlockSpec((1, 1, 2 * D_FF), lambda i, be, nv: (be[i], 0, 0)),
            pl.BlockSpec((1, D_FF, D_MODEL), lambda i, be, nv: (be[i], 0, 0)),
            pl.BlockSpec((1, 1, D_MODEL), lambda i, be, nv: (be[i], 0, 0)),
        ],
        out_specs=pl.BlockSpec((bm, D_MODEL // 2), lambda i, be, nv: (i, 0)),
    )
    return pl.pallas_call(
        functools.partial(_moe_kernel, fc=fc),
        grid_spec=grid_spec,
        out_shape=jax.ShapeDtypeStruct((P, D_MODEL // 2), jnp.uint32),
        compiler_params=_params(("arbitrary",)),
        name="moe",
    )(blk_e, nvalid, xs, W["wgu"], W["bgu"], W["wd"], W["bd"])


def _final_kernel(r_ref, yg_ref, tg_ref, g_ref, b_ref, y_ref):
    ffn_lo = ffn_hi = None
    for k in range(TOP_K):
        lo, hi = _unpack_halves(yg_ref[k])
        g = tg_ref[:, k:k + 1]
        ffn_lo = lo * g if ffn_lo is None else ffn_lo + lo * g
        ffn_hi = hi * g if ffn_hi is None else ffn_hi + hi * g
    acc = r_ref[...] + jnp.concatenate([ffn_lo, ffn_hi], axis=1)
    y_ref[...] = _layer_norm(acc, g_ref[...], b_ref[...])


def _final(r, yg, tg, W, tm=512):
    T = r.shape[0]
    tm = min(tm, T)
    row = lambda i: (i, 0)
    return pl.pallas_call(
        _final_kernel,
        grid=(T // tm,),
        in_specs=[pl.BlockSpec((tm, D_MODEL), row),
                  pl.BlockSpec((TOP_K, tm, D_MODEL // 2), lambda i: (0, i, 0)),
                  pl.BlockSpec((tm, TOP_K), row),
                  _const_spec((1, D_MODEL)), _const_spec((1, D_MODEL))],
        out_specs=pl.BlockSpec((tm, D_MODEL), row),
        out_shape=jax.ShapeDtypeStruct((T, D_MODEL), F32),
        compiler_params=_params(("parallel",)),
        name="final_ln",
    )(r, yg, tg, W["ln2g"], W["ln2b"])


def _pad_heads(w, n_heads, width, start, size, dst=0):
    K = w.shape[0]
    w3 = w.reshape(K, n_heads, width)[:, :, start:start + size]
    out = jnp.zeros((K, n_heads, HEAD_PAD), w.dtype)
    out = out.at[:, :, dst:dst + size].set(w3)
    return out.reshape(K, n_heads * HEAD_PAD)


def _prep_weights(w_in, q_a_norm, w_uq, kv_a_norm, w_ukv, w_o_attn, dn_conv, dn_a_log, dn_dt_bias,
                  dn_norm, w_o_dn, w_out, ln1_g, ln1_b, router_w, router_b, w_gate_up, b_gate_up,
                  w_down, b_down, ple_w_proj, ple_w_gate, ln2_g, ln2_b):
    W = {}
    half = ROPE_DIM // 2
    o = 0
    cq = w_in[:, o:o + Q_LORA]; o += Q_LORA
    ckv = w_in[:, o:o + KV_LORA]; o += KV_LORA
    kr = w_in[:, o:o + ROPE_DIM]; o += ROPE_DIM
    nqk = DN_HEADS * DN_DK
    dq = w_in[:, o:o + nqk]; o += nqk
    dk = w_in[:, o:o + nqk]; o += nqk
    dv = w_in[:, o:o + DN_HEADS * DN_DV]; o += DN_HEADS * DN_DV
    dz = w_in[:, o:o + DN_HEADS * DN_DV]; o += DN_HEADS * DN_DV
    da = w_in[:, o:o + 2 * DN_HEADS]; o += 2 * DN_HEADS
    db = w_in[:, o:o + 2 * DN_HEADS]; o += 2 * DN_HEADS
    gate = w_in[:, o:o + 2 * D_MODEL]

    def lane_block(parts):
        w = jnp.concatenate(parts, axis=1)
        return jnp.pad(w, ((0, 0), (0, LANES - w.shape[1])))

    zeros64 = jnp.zeros((D_MODEL, NOPE_DIM), F32)
    kr_blk = lane_block([zeros64, kr])
    krs_blk = lane_block([zeros64, kr[:, half:], kr[:, :half]])
    ab = [lane_block([da[:, d * DN_HEADS:(d + 1) * DN_HEADS], db[:, d * DN_HEADS:(d + 1) * DN_HEADS]])
          for d in range(2)]

    def interleave_vk(v, k):
        lead = v.shape[:-1]
        v3 = v.reshape(lead + (DN_HEADS, DN_DV))
        k3 = k.reshape(lead + (DN_HEADS, DN_DK))
        return jnp.concatenate([v3, k3], axis=-1).reshape(lead + (DN_HEADS * (DN_DV + DN_DK),))

    W["w1"] = jnp.concatenate([cq, ckv, kr_blk, krs_blk, ab[0], ab[1], dq, interleave_vk(dv, dk), dz, gate],
                              axis=1).astype(BF16)
    W["qan"] = q_a_norm.reshape(1, Q_LORA)
    W["kvan"] = kv_a_norm.reshape(1, KV_LORA)

    qw = NOPE_DIM + ROPE_DIM
    wq_nope = _pad_heads(w_uq, MLA_HEADS, qw, 0, NOPE_DIM, 0)
    wq_r1 = _pad_heads(w_uq, MLA_HEADS, qw, NOPE_DIM, half, NOPE_DIM)
    wq_r2 = _pad_heads(w_uq, MLA_HEADS, qw, NOPE_DIM + half, half, NOPE_DIM + half)
    W["wq"] = (wq_nope + wq_r1 + wq_r2).astype(BF16)
    wq_s1 = _pad_heads(w_uq, MLA_HEADS, qw, NOPE_DIM + half, half, NOPE_DIM)
    wq_s2 = _pad_heads(w_uq, MLA_HEADS, qw, NOPE_DIM, half, NOPE_DIM + half)
    W["wqs"] = (wq_s1 + wq_s2).astype(BF16)
    kvw = NOPE_DIM + V_DIM
    W["wk"] = _pad_heads(w_ukv, MLA_HEADS, kvw, 0, NOPE_DIM, 0).astype(BF16)
    W["wv"] = _pad_heads(w_ukv, MLA_HEADS, kvw, NOPE_DIM, V_DIM, 0).astype(BF16)

    neg_a = -jnp.exp(dn_a_log.astype(F32))
    abp = jnp.zeros((8, LANES), F32)
    for d in range(2):
        abp = abp.at[2 * d, :DN_HEADS].set(neg_a[d])
        abp = abp.at[2 * d + 1, :DN_HEADS].set(dn_dt_bias[d].astype(F32))
    W["abp"] = abp

    cw = jnp.concatenate([dn_conv[:, :nqk], interleave_vk(dn_conv[:, 2 * nqk:], dn_conv[:, nqk:2 * nqk])], axis=1)
    W["convw"] = jnp.pad(cw.astype(F32), ((0, 8 - CONV_K), (0, 0)))

    def group_indicator(width, group):
        e = (np.arange(width)[:, None] // group == np.arange(LANES)[None, :]).astype(np.float32)
        return e

    eq = group_indicator(nqk, DN_DK)
    W["eq"] = jnp.asarray(eq, BF16)
    W["eqt"] = jnp.asarray(eq.T, BF16)
    evk = group_indicator(2 * nqk, DN_DK)
    W["evk"] = jnp.asarray(evk, BF16)
    W["evkt"] = jnp.asarray(evk.T, BF16)
    W["e8"] = W["eq"]
    W["e8t"] = W["eqt"]
    r = np.arange(DN_BLOCK)
    W["tri"] = jnp.asarray(np.stack([r[:, None] >= r[None, :], r[:, None] <= r[None, :]]).astype(np.float32), BF16)
    rp = np.arange(DN_PREP_TILE)
    W["conv_shift"] = jnp.asarray(
        np.stack([rp[None, :] == rp[:, None] + (j - CONV_K // 2) for j in range(CONV_K)]).astype(np.float32), BF16)
    W["dn_masks"] = [(jnp.asarray(t), jnp.asarray(g, BF16), jnp.asarray(o, BF16))
                     for t, g, o in (_dn_masks(0), _dn_masks(1))]
    rt = np.arange(MIX_TILE)
    W["ustrict"] = jnp.asarray((rt[:, None] < rt[None, :]).astype(np.float32), BF16)

    woa = w_o_attn.reshape(MLA_HEADS, V_DIM, D_MODEL)
    woa = jnp.pad(woa, ((0, 0), (0, HEAD_PAD - V_DIM), (0, 0)))
    W["woa"] = woa.reshape(MLA_HEADS * HEAD_PAD, D_MODEL).astype(BF16)
    W["wod"] = w_o_dn.astype(BF16)
    W["wout"] = w_out.astype(BF16)
    W["dnorm"] = jnp.tile(dn_norm.astype(F32), DN_HEADS).reshape(1, DN_HEADS * DN_DV)
    W["ln1g"] = ln1_g.reshape(1, D_MODEL)
    W["ln1b"] = ln1_b.reshape(1, D_MODEL)
    W["ln2g"] = ln2_g.reshape(1, D_MODEL)
    W["ln2b"] = ln2_b.reshape(1, D_MODEL)
    W["wpg"] = ple_w_gate.astype(BF16)
    W["wpp"] = ple_w_proj.astype(BF16)
    rwt = router_w.T.astype(F32)
    W["rwh"], W["rwl"] = _split_bf16(rwt)
    W["rb"] = router_b.reshape(N_EXPERTS, 1).astype(F32)
    W["wgu"] = w_gate_up.astype(BF16)
    W["bgu"] = b_gate_up.reshape(N_EXPERTS, 1, 2 * D_FF).astype(F32)
    W["wd"] = w_down.astype(BF16)
    W["bd"] = b_down.reshape(N_EXPERTS, 1, D_MODEL).astype(F32)
    return W


def _rope_tables(S):
    half = ROPE_DIM // 2
    inv = ROPE_THETA ** (-jnp.arange(0, ROPE_DIM, 2, dtype=F32) / ROPE_DIM)
    ang = jnp.arange(S, dtype=F32)[:, None] * inv[None, :]
    cos, sin = jnp.cos(ang), jnp.sin(ang)
    c = (NOPE_DIM + ROPE_DIM) ** -0.5 * math.log2(math.e)
    pad = jnp.zeros((S, HEAD_PAD - NOPE_DIM - ROPE_DIM), F32)
    cos_blk = jnp.concatenate([cos, cos, pad], axis=1)
    sin_blk = jnp.concatenate([-sin, sin, pad], axis=1)
    cosq = jnp.concatenate([jnp.ones((S, NOPE_DIM), F32), cos_blk], axis=1) * c
    sinq = jnp.concatenate([jnp.zeros((S, NOPE_DIM), F32), sin_blk], axis=1) * c
    cosk = jnp.concatenate([jnp.zeros((S, NOPE_DIM), F32), cos_blk], axis=1)
    sink = jnp.concatenate([jnp.zeros((S, NOPE_DIM), F32), sin_blk], axis=1)
    return cosq, sinq, cosk, sink


def _dest_kernel(pstart_ref, ti_ref, rank_ref, dest_ref):
    ti = ti_ref[...]
    dest = rank_ref[...]
    for e in range(N_EXPERTS):
        dest = dest + jnp.where(ti == e, pstart_ref[e], 0)
    dest_ref[...] = dest


def _dest(p_start, top_i, rank, tile=8192):
    T = top_i.shape[1]
    tile = min(tile, T)
    spec = pl.BlockSpec((TOP_K, tile), lambda i, ps: (0, i))
    return pl.pallas_call(
        _dest_kernel,
        grid_spec=pltpu.PrefetchScalarGridSpec(num_scalar_prefetch=1, grid=(T // tile,),
                                               in_specs=[spec, spec], out_specs=spec),
        out_shape=jax.ShapeDtypeStruct((TOP_K, T), jnp.int32),
        compiler_params=_params(("parallel",)),
        name="slot_index",
    )(p_start, top_i, rank)


SC_CHUNK = 128


def _sc_mesh():
    info = plsc.get_sparse_core_info()
    mesh = plsc.VectorSubcoreMesh(core_axis_name="c", subcore_axis_name="s")
    return mesh, info.num_cores, info.num_cores * info.num_subcores


def _sc_dispatch(rows, dest3, P):
    T, D = rows.shape
    K = dest3.shape[0]
    mesh, n_cores, n_workers = _sc_mesh()
    n_chunks = T // (n_workers * SC_CHUNK)

    @functools.partial(
        pl.kernel, mesh=mesh, out_type=jax.ShapeDtypeStruct((P, D), rows.dtype),
        scratch_types=[pltpu.VMEM((K, SC_CHUNK), jnp.int32), pltpu.VMEM((SC_CHUNK, D), rows.dtype),
                       pltpu.SemaphoreType.DMA])
    def dispatch(rows_hbm, dest_hbm, out_hbm, idx_v, rows_v, sem):
        worker = lax.axis_index("s") * n_cores + lax.axis_index("c")

        @pl.loop(0, n_chunks)
        def _(j):
            g = worker * n_chunks + j
            base = pl.multiple_of(g * SC_CHUNK, SC_CHUNK)
            pltpu.sync_copy(rows_hbm.at[pl.ds(base, SC_CHUNK)], rows_v)
            pltpu.sync_copy(dest_hbm.at[:, g], idx_v)
            for k in range(K):
                pltpu.async_copy(rows_v, out_hbm.at[idx_v.at[k]], sem).wait()

    return dispatch(rows, dest3)


def _sc_gather(table, idx):
    M = idx.shape[0]
    D = table.shape[1]
    mesh, n_cores, n_workers = _sc_mesh()
    n_chunks = M // (n_workers * SC_CHUNK)

    @functools.partial(
        pl.kernel, mesh=mesh, out_type=jax.ShapeDtypeStruct((M, D), table.dtype),
        scratch_types=[pltpu.VMEM((SC_CHUNK,), jnp.int32), pltpu.VMEM((SC_CHUNK, D), table.dtype),
                       pltpu.SemaphoreType.DMA])
    def gather(table_hbm, idx_hbm, out_hbm, idx_v, rows_v, sem):
        worker = lax.axis_index("s") * n_cores + lax.axis_index("c")

        @pl.loop(0, n_chunks)
        def _(j):
            base = pl.multiple_of((worker * n_chunks + j) * SC_CHUNK, SC_CHUNK)
            pltpu.sync_copy(idx_hbm.at[pl.ds(base, SC_CHUNK)], idx_v)
            pltpu.async_copy(table_hbm.at[idx_v], rows_v, sem).wait()
            pltpu.sync_copy(rows_v, out_hbm.at[pl.ds(base, SC_CHUNK)])

    return gather(table, idx)


def _route(top_i, rank, counts, T, bm):
    A = TOP_K * T
    counts = counts.astype(jnp.int32)
    padded = ((counts + bm - 1) // bm) * bm
    p_end = jnp.cumsum(padded)
    p_start = p_end - padded
    dest = _dest(p_start, top_i, rank)
    nblk = A // bm + N_EXPERTS
    blk_start = jnp.arange(nblk, dtype=jnp.int32) * bm
    blk_e = jnp.minimum(jnp.sum(p_end[None, :] <= blk_start[:, None], axis=1), N_EXPERTS - 1).astype(jnp.int32)
    nvalid = (p_end[-1] // bm).astype(jnp.int32).reshape(1)
    return dest, nblk * bm, blk_e, nvalid


def _layer(x, p, W, bm):
    B, S, _ = x.shape
    T = B * S
    x2 = x.reshape(T, D_MODEL)
    p2 = p.reshape(T, PLE_DIM)
    Wl = dict(W)
    Wl["cosq"], Wl["sinq"], Wl["cosk"], Wl["sink"] = _rope_tables(S)

    q, k, v, dn, z, gates, gb = _in_proj(x2, S, Wl)
    attn = _attention(q, k, v, B, S)
    qn, vk, aux = _dn_prep(dn, gb, S, Wl)
    kt = vk.reshape(T, DN_HEADS, 2, DN_DK)[:, :, 1, :].reshape(T, DN_HEADS * DN_DK).T
    auxt = jnp.swapaxes(aux, 1, 2)
    o_f = _deltanet(qn, vk, kt, aux, auxt, B, S, Wl, 0)
    o_b = _deltanet(qn, vk, kt, aux, auxt, B, S, Wl, 1)
    r, hb, top_i, top_g, rank, cnt = _mix(x2, attn, o_f, o_b, z, gates, p2, Wl)

    dest, P, blk_e, nvalid = _route(top_i, rank, cnt[:, 0], T, bm)
    xs = _sc_dispatch(hb, dest.reshape(TOP_K, T // SC_CHUNK, SC_CHUNK), P)
    yb = _moe(xs, blk_e, nvalid, Wl, bm)
    yg = _sc_gather(yb, dest.reshape(TOP_K * T)).reshape(TOP_K, T, D_MODEL // 2)
    y = _final(r, yg, top_g.T, Wl)
    return y.reshape(B, S, D_MODEL)


def kernel(x_prompt, x_sample, p_prompt, p_sample, w_in, q_a_norm, w_uq, kv_a_norm, w_ukv, w_o_attn, dn_conv, dn_a_log, dn_dt_bias, dn_norm, w_o_dn, w_out, ln1_g, ln1_b, router_w, router_b, w_gate_up, b_gate_up, w_down, b_down, ple_w_proj, ple_w_gate, ln2_g, ln2_b):
    y_prompt, y_sample = x_prompt, x_sample
    for l in range(DEPTH):
        W = _prep_weights(w_in[l], q_a_norm[l], w_uq[l], kv_a_norm[l], w_ukv[l], w_o_attn[l], dn_conv[l],
                          dn_a_log[l], dn_dt_bias[l], dn_norm[l], w_o_dn[l], w_out[l], ln1_g[l], ln1_b[l],
                          router_w[l], router_b[l], w_gate_up[l], b_gate_up[l], w_down[l], b_down[l],
                          ple_w_proj[l], ple_w_gate[l], ln2_g[l], ln2_b[l])
        y_prompt = _layer(y_prompt, p_prompt[l], W, bm=MOE_BLOCK)
        y_sample = _layer(y_sample, p_sample[l], W, bm=MOE_BLOCK)
    return (y_prompt, y_sample)
```

```python
import functools
import math

import numpy as np
import jax
import jax.numpy as jnp
from jax import lax
from jax.experimental import pallas as pl
from jax.experimental.pallas import tpu as pltpu
from jax.experimental.pallas import tpu_sc as plsc

D_MODEL = 1024
MLA_HEADS = 8
Q_LORA = 256
KV_LORA = 128
NOPE_DIM = 64
ROPE_DIM = 32
V_DIM = 64
ROPE_THETA = 10000.0
DN_HEADS = 8
DN_DK = 64
DN_DV = 64
CONV_K = 5
N_EXPERTS = 32
TOP_K = 4
D_FF = 1024
SWIGLU_LIMIT = 7.0
SWIGLU_ALPHA = 1.702
PLE_DIM = 256
DEPTH = 1
DEEPNORM_ALPHA = (2.0 * DEPTH) ** 0.25

LANES = 128
HEAD_PAD = 128
DN_BLOCK = 256
DN_PREP_TILE = 256
MOE_BLOCK = 512
MIX_TILE = 512
VMEM_LIMIT = 56 * 1024 * 1024

_C_CQ = 0
_C_CKV = _C_CQ + Q_LORA
_C_KR = _C_CKV + KV_LORA
_C_KRS = _C_KR + LANES
_C_AB0 = _C_KRS + LANES
_C_AB1 = _C_AB0 + LANES
_C_DNQ = _C_AB1 + LANES
_C_DNVK = _C_DNQ + DN_HEADS * DN_DK
_C_Z = _C_DNVK + DN_HEADS * (DN_DK + DN_DV)
_C_GATE = _C_Z + DN_HEADS * DN_DV
_C_END = _C_GATE + 2 * D_MODEL

BF16 = jnp.bfloat16
F32 = jnp.float32


def _dot(a, b):
    return jnp.dot(a, b, preferred_element_type=F32)


def _dot_nt(a, b):
    return lax.dot_general(a, b, (((1,), (1,)), ((), ())), preferred_element_type=F32)


def _split_bf16(x):
    hi = x.astype(BF16)
    lo = (x - hi.astype(F32)).astype(BF16)
    return hi, lo


_HI_HALFWORD = 0xFFFF0000


def _pack_halves(x):
    w = x.shape[1] // 2
    bits = lax.bitcast_convert_type(x.astype(BF16).astype(F32), jnp.uint32)
    return (bits[:, :w] >> 16) | (bits[:, w:] & jnp.uint32(_HI_HALFWORD))


def _unpack_halves(words):
    lo = lax.bitcast_convert_type(words << 16, F32)
    hi = lax.bitcast_convert_type(words & jnp.uint32(_HI_HALFWORD), F32)
    return lo, hi


def _const_spec(shape):
    n = len(shape)
    return pl.BlockSpec(shape, lambda *_: (0,) * n, pipeline_mode=pl.Buffered(1))


def _params(sem):
    return pltpu.CompilerParams(dimension_semantics=sem, vmem_limit_bytes=VMEM_LIMIT)


def _in_proj_kernel(x_ref, w1_ref, qan_ref, kvan_ref, wq_ref, wqs_ref, wk_ref, wv_ref,
                    cosq_ref, sinq_ref, cosk_ref, sink_ref, abp_ref,
                    q_ref, k_ref, v_ref, dn_ref, z_ref, gates_ref, gb_ref):
    xb = x_ref[...].astype(BF16)

    def proj(lo, hi):
        return _dot(xb, w1_ref[:, lo:hi])

    def rms(c, g):
        return (c * lax.rsqrt(jnp.mean(c * c, axis=-1, keepdims=True) + 1e-6) * g).astype(BF16)

    cqn = rms(proj(_C_CQ, _C_CKV), qan_ref[...])
    qa = _dot(cqn, wq_ref[...])
    qb = _dot(cqn, wqs_ref[...])
    ckvn = rms(proj(_C_CKV, _C_KR), kvan_ref[...])
    kw = _dot(ckvn, wk_ref[...])
    vw = _dot(ckvn, wv_ref[...])
    kr = proj(_C_KR, _C_KRS) * cosk_ref[...] + proj(_C_KRS, _C_AB0) * sink_ref[...]
    cosq = cosq_ref[...]
    sinq = sinq_ref[...]
    lane = lax.broadcasted_iota(jnp.int32, (1, HEAD_PAD), 1)
    ones_col = (lane == V_DIM).astype(F32)
    for h in range(MLA_HEADS):
        sl = slice(h * HEAD_PAD, (h + 1) * HEAD_PAD)
        q_ref[:, sl] = (qa[:, sl] * cosq + qb[:, sl] * sinq).astype(BF16)
        k_ref[:, sl] = (kw[:, sl] + kr).astype(BF16)
        v_ref[:, sl] = (vw[:, sl] + ones_col).astype(BF16)

    for d, c0 in enumerate((_C_AB0, _C_AB1)):
        ab = proj(c0, c0 + LANES)
        neg_a = abp_ref[2 * d:2 * d + 1, :]
        dtb = abp_ref[2 * d + 1:2 * d + 2, :]
        t = ab + dtb
        sp = jnp.maximum(t, 0.0) + jnp.log(1.0 + jnp.exp(-jnp.abs(t)))
        g = neg_a * sp
        beta = 1.0 / (1.0 + jnp.exp(-ab))
        gb_ref[d] = jnp.where(lane < DN_HEADS, g, beta)[:, :2 * DN_HEADS]

    dn_ref[...] = proj(_C_DNQ, _C_Z).astype(BF16)
    z_ref[...] = proj(_C_Z, _C_GATE).astype(BF16)
    gl = proj(_C_GATE, _C_END)
    gates_ref[...] = (1.0 / (1.0 + jnp.exp(-gl))).astype(BF16)


def _in_proj(x2, S, W, tm=512):
    T = x2.shape[0]
    nseq = S // tm
    row = lambda i: (i, 0)
    pos = lambda i: (i % nseq, 0)
    dn_w = _C_Z - _C_DNQ
    return pl.pallas_call(
        _in_proj_kernel,
        grid=(T // tm,),
        in_specs=[
            pl.BlockSpec((tm, D_MODEL), row),
            _const_spec((D_MODEL, _C_END)),
            _const_spec((1, Q_LORA)), _const_spec((1, KV_LORA)),
            _const_spec((Q_LORA, MLA_HEADS * HEAD_PAD)), _const_spec((Q_LORA, MLA_HEADS * HEAD_PAD)),
            _const_spec((KV_LORA, MLA_HEADS * HEAD_PAD)), _const_spec((KV_LORA, MLA_HEADS * HEAD_PAD)),
            pl.BlockSpec((tm, HEAD_PAD), pos), pl.BlockSpec((tm, HEAD_PAD), pos),
            pl.BlockSpec((tm, HEAD_PAD), pos), pl.BlockSpec((tm, HEAD_PAD), pos),
            _const_spec((8, LANES)),
        ],
        out_specs=[
            pl.BlockSpec((tm, MLA_HEADS * HEAD_PAD), row),
            pl.BlockSpec((tm, MLA_HEADS * HEAD_PAD), row),
            pl.BlockSpec((tm, MLA_HEADS * HEAD_PAD), row),
            pl.BlockSpec((tm, dn_w), row),
            pl.BlockSpec((tm, DN_HEADS * DN_DV), row),
            pl.BlockSpec((tm, 2 * D_MODEL), row),
            pl.BlockSpec((2, tm, 2 * DN_HEADS), lambda i: (0, i, 0)),
        ],
        out_shape=[
            jax.ShapeDtypeStruct((T, MLA_HEADS * HEAD_PAD), BF16),
            jax.ShapeDtypeStruct((T, MLA_HEADS * HEAD_PAD), BF16),
            jax.ShapeDtypeStruct((T, MLA_HEADS * HEAD_PAD), BF16),
            jax.ShapeDtypeStruct((T, dn_w), BF16),
            jax.ShapeDtypeStruct((T, DN_HEADS * DN_DV), BF16),
            jax.ShapeDtypeStruct((T, 2 * D_MODEL), BF16),
            jax.ShapeDtypeStruct((2, T, 2 * DN_HEADS), F32),
        ],
        compiler_params=_params(("parallel",)),
        name="in_proj",
    )(x2, W["w1"], W["qan"], W["kvan"], W["wq"], W["wqs"], W["wk"], W["wv"],
      W["cosq"], W["sinq"], W["cosk"], W["sink"], W["abp"])


def _attn_kernel(q_ref, k_ref, v_ref, o_ref, *, tk, unroll):
    tq = q_ref.shape[0]
    S = k_ref.shape[0]
    q = q_ref[...]

    def body(j, carry):
        m, acc = carry
        off = pl.multiple_of(j * tk, tk)
        s = _dot_nt(q, k_ref[pl.ds(off, tk), :])
        m_new = jnp.maximum(m, jnp.max(s, axis=-1, keepdims=True))
        p = jnp.exp2(s - m_new).astype(BF16)
        acc = acc * jnp.exp2(m - m_new) + _dot(p, v_ref[pl.ds(off, tk), :])
        return m_new, acc

    m0 = jnp.full((tq, 1), -1e30, F32)
    acc0 = jnp.zeros((tq, HEAD_PAD), F32)
    _, acc = lax.fori_loop(0, S // tk, body, (m0, acc0), unroll=unroll)
    o_ref[...] = (acc / acc[:, V_DIM:V_DIM + 1]).astype(BF16)


def _attention(q, k, v, B, S, tq=1024, tk=2048, unroll=4):
    T = q.shape[0]
    tq = min(tq, S)
    tk = min(tk, S)
    nq = S // tq
    return pl.pallas_call(
        functools.partial(_attn_kernel, tk=tk, unroll=unroll),
        grid=(B, MLA_HEADS, nq),
        in_specs=[
            pl.BlockSpec((tq, HEAD_PAD), lambda b, h, i: (b * nq + i, h)),
            pl.BlockSpec((S, HEAD_PAD), lambda b, h, i: (b, h)),
            pl.BlockSpec((S, HEAD_PAD), lambda b, h, i: (b, h)),
        ],
        out_specs=pl.BlockSpec((tq, HEAD_PAD), lambda b, h, i: (b * nq + i, h)),
        out_shape=jax.ShapeDtypeStruct((T, MLA_HEADS * HEAD_PAD), BF16),
        compiler_params=_params(("parallel", "parallel", "arbitrary")),
        name="attention",
    )(q, k, v)


_HALO = 16


def _dn_prep_kernel(x_ref, prev_ref, next_ref, cw_ref, eq_ref, eqt_ref, evk_ref, evkt_ref,
                    gb_ref, tri_ref, shift_ref, qn_ref, vk_ref, aux_ref, *, nseq):
    i = pl.program_id(0)
    tp = x_ref.shape[0]
    first = (i % nseq) == 0
    last = (i % nseq) == nseq - 1
    half = CONV_K // 2
    xb = x_ref[...]

    y = xb.astype(F32) * cw_ref[half:half + 1, :]
    for j in range(CONV_K):
        if j != half:
            y = y + _dot(shift_ref[j], xb) * cw_ref[j:j + 1, :]

    def edge(slab, row0):
        n = slab.shape[0]
        acc = None
        for j in range(CONV_K):
            shift = (half - j) % n
            rolled = slab if shift == 0 else pltpu.roll(slab, shift, axis=0)
            term = rolled[row0:row0 + 8, :] * cw_ref[j:j + 1, :]
            acc = term if acc is None else acc + term
        return acc

    prev = jnp.where(first, 0.0, prev_ref[...].astype(F32))
    nxt = jnp.where(last, 0.0, next_ref[...].astype(F32))
    top = edge(jnp.concatenate([prev, xb[:_HALO].astype(F32)], axis=0), _HALO)
    bot = edge(jnp.concatenate([xb[tp - _HALO:].astype(F32), nxt], axis=0), _HALO - 8)
    y = jnp.concatenate([top, y[8:tp - 8], bot], axis=0)
    y = y * (1.0 / (1.0 + jnp.exp(-y)))

    def group_scale(v, e_ref, et_ref):
        ss = _dot((v * v).astype(BF16), e_ref[...])
        return _dot(lax.rsqrt(ss + 1e-6).astype(BF16), et_ref[...])

    nq = DN_HEADS * DN_DK
    yq = y[:, :nq]
    qn_ref[...] = (yq * group_scale(yq, eq_ref, eqt_ref) * (DN_DK ** -0.5)).astype(BF16)
    yvk = y[:, nq:]
    sc = group_scale(yvk, evk_ref, evkt_ref)
    lane = lax.broadcasted_iota(jnp.int32, (1, yvk.shape[1]), 1)
    is_k = (lane // DN_DV) % 2 == 1
    vk_ref[...] = (yvk * jnp.where(is_k, sc, 1.0)).astype(BF16)

    hcol = lax.broadcasted_iota(jnp.int32, (1, 2 * DN_HEADS), 1) < DN_HEADS
    for d in range(2):
        gb = gb_ref[d]
        for blk in range(tp // DN_BLOCK):
            rs = slice(blk * DN_BLOCK, (blk + 1) * DN_BLOCK)
            g = gb[rs]
            g_hi = g.astype(BF16)
            g_mid, g_lo = _split_bf16(g - g_hi.astype(F32))
            cs = _dot(tri_ref[d], g_hi) + _dot(tri_ref[d], g_mid) + _dot(tri_ref[d], g_lo)
            aux_ref[d, rs, :] = jnp.where(hcol, cs, g)


def _dn_prep(dn, gb, S, W):
    tp = DN_PREP_TILE
    T = dn.shape[0]
    nseq = S // tp
    hb = tp // _HALO
    nh = T // _HALO
    dn_w = dn.shape[1]
    nq = DN_HEADS * DN_DK
    return pl.pallas_call(
        functools.partial(_dn_prep_kernel, nseq=nseq),
        grid=(T // tp,),
        in_specs=[
            pl.BlockSpec((tp, dn_w), lambda i: (i, 0)),
            pl.BlockSpec((_HALO, dn_w), lambda i: (jnp.maximum(i * hb - 1, 0), 0)),
            pl.BlockSpec((_HALO, dn_w), lambda i: (jnp.minimum((i + 1) * hb, nh - 1), 0)),
            _const_spec((8, dn_w)),
            _const_spec((nq, LANES)), _const_spec((LANES, nq)),
            _const_spec((dn_w - nq, LANES)), _const_spec((LANES, dn_w - nq)),
            pl.BlockSpec((2, tp, 2 * DN_HEADS), lambda i: (0, i, 0)),
            _const_spec((2, DN_BLOCK, DN_BLOCK)),
            _const_spec((CONV_K, tp, tp)),
        ],
        out_specs=[
            pl.BlockSpec((tp, nq), lambda i: (i, 0)),
            pl.BlockSpec((tp, dn_w - nq), lambda i: (i, 0)),
            pl.BlockSpec((2, tp, 2 * DN_HEADS), lambda i: (0, i, 0)),
        ],
        out_shape=[
            jax.ShapeDtypeStruct((T, nq), BF16),
            jax.ShapeDtypeStruct((T, dn_w - nq), BF16),
            jax.ShapeDtypeStruct((2, T, 2 * DN_HEADS), F32),
        ],
        compiler_params=_params(("parallel",)),
        name="dn_prep",
    )(dn, dn, dn, W["convw"], W["eq"], W["eqt"], W["evk"], W["evkt"], gb, W["tri"], W["conv_shift"])


_MERGE_SIZES = (16, 32, 64, 128)


def _active_blocks(d, s):
    return range(1 - d, DN_BLOCK // s, 2)


def _dn_masks(d):
    r = np.arange(DN_BLOCK)[:, None]
    c = np.arange(DN_BLOCK)[None, :]
    rr, cc = (r, c) if d == 0 else (c, r)
    tri = np.stack([rr >= cc, rr > cc]).astype(np.float32)
    diag16 = ((r // 16) == (c // 16)).astype(np.float32)
    offc = []
    for s in _MERGE_SIZES:
        full = ((rr // s) % 2 == 1) & ((rr // s) == (cc // s) + 1)
        rows = np.concatenate([np.arange(b * s, (b + 1) * s) for b in _active_blocks(d, s)])
        assert not np.delete(full, rows, axis=0).any()
        offc.append(full[rows].astype(np.float32))
    return tri, diag16, np.stack(offc)


def _deltanet_kernel(qn_ref, vk_ref, kt_ref, aux_ref, auxt_ref, tri_ref, diag16_ref, offc_ref, o_ref,
                     s_ref, nm_ref, p_ref, x_ref, aqk_ref, uw_ref, vn_ref, *, d):
    i = pl.program_id(1)
    C = DN_BLOCK
    heads = range(DN_HEADS)

    @pl.when(i == 0)
    def _():
        s_ref[...] = jnp.zeros_like(s_ref)

    rowi = lax.broadcasted_iota(jnp.int32, (C, C), 0)
    coli = lax.broadcasted_iota(jnp.int32, (C, C), 1)
    eye = (rowi == coli).astype(F32)
    lane = lax.broadcasted_iota(jnp.int32, (1, LANES), 1)
    r64 = lax.broadcasted_iota(jnp.int32, (DN_DK, LANES), 0)
    c64 = lax.broadcasted_iota(jnp.int32, (DN_DK, LANES), 1)
    zeros_kt = jnp.zeros((DN_DK, C), BF16)
    zeros_s = jnp.zeros((DN_DK, LANES), F32)

    def q_pair(h):
        return qn_ref[:, (h // 2) * LANES:(h // 2 + 1) * LANES]

    def vk_head(h):
        return vk_ref[:, h * LANES:(h + 1) * LANES]

    def kt_head(h):
        return kt_ref[h * DN_DK:(h + 1) * DN_DK, :]

    def gc_col(h):
        return aux_ref[0, :, h:h + 1]

    def beta_col(h):
        return aux_ref[0, :, DN_HEADS + h:DN_HEADS + h + 1]

    def gc_row(h):
        return auxt_ref[0, h:h + 1, :]

    for h in heads:
        kt = kt_head(h)
        kt_for_q = jnp.concatenate([kt, zeros_kt] if h % 2 == 0 else [zeros_kt, kt], axis=0)
        kt_for_k = jnp.concatenate([zeros_kt, kt], axis=0)
        qk = _dot(q_pair(h), kt_for_q)
        kk = _dot(vk_head(h), kt_for_k)
        e0 = jnp.exp(jnp.minimum(gc_col(h) - gc_row(h), 0.0))
        aqk_ref[h] = (qk * e0 * tri_ref[0]).astype(BF16)
        nm_ref[h] = (-(kk * e0 * tri_ref[1]) * beta_col(h)).astype(BF16)

    diag16 = diag16_ref[...]
    for h in heads:
        nd = nm_ref[h] * diag16
        p_ref[h] = (eye + nd.astype(F32)).astype(BF16)
        x_ref[h] = _dot(nd, nd).astype(BF16)
    for it in range(3):
        for h in heads:
            pb = p_ref[h]
            sqb = x_ref[h]
            p_ref[h] = (pb.astype(F32) + _dot(pb, sqb)).astype(BF16)
            if it < 2:
                x_ref[h] = _dot(sqb, sqb).astype(BF16)
    for k, s in enumerate(_MERGE_SIZES):
        blocks = list(_active_blocks(d, s))
        zeros_blk = jnp.zeros((s, C), BF16)

        def active_rows(ref, h):
            return jnp.concatenate([ref[h, b * s:(b + 1) * s, :] for b in blocks], axis=0)

        for h in heads:
            xc = _dot(active_rows(nm_ref, h) * offc_ref[k], p_ref[h]).astype(BF16)
            pieces = []
            for j in range(len(blocks)):
                piece = xc[j * s:(j + 1) * s, :]
                pieces += [zeros_blk, piece] if d == 0 else [piece, zeros_blk]
            x_ref[h] = jnp.concatenate(pieces, axis=0)
        for h in heads:
            pr = active_rows(p_ref, h)
            prn = (pr.astype(F32) + _dot(pr, x_ref[h])).astype(BF16)
            for j, b in enumerate(blocks):
                p_ref[h, b * s:(b + 1) * s, :] = prn[j * s:(j + 1) * s, :]

    for h in heads:
        egc = jnp.exp(gc_col(h))
        rhs = (vk_head(h).astype(F32) * beta_col(h) * jnp.where(lane < DN_DV, 1.0, egc)).astype(BF16)
        uw_ref[h] = _dot(p_ref[h], rhs).astype(BF16)
    for h in heads:
        eye_pl = (c64 == r64 + (h % 2) * DN_DV).astype(F32)
        s_aug = jnp.concatenate([eye_pl, -s_ref[h]], axis=0).astype(BF16)
        vn_ref[h] = _dot(uw_ref[h], s_aug).astype(BF16)
    o_pair = None
    for h in heads:
        par = h % 2
        gc_r = gc_row(h)
        g_tot = gc_r[:, C - 1:C] if d == 0 else gc_r[:, 0:1]
        s_pl = s_ref[h]
        v_new = vn_ref[h]
        s_sel = jnp.concatenate([s_pl, zeros_s] if par == 0 else [zeros_s, s_pl], axis=0).astype(BF16)
        qd = (q_pair(h).astype(F32) * jnp.exp(gc_col(h))).astype(BF16)
        o_pl = _dot(qd, s_sel) + _dot(aqk_ref[h], v_new)
        kd = (kt_head(h).astype(F32) * jnp.exp(g_tot - gc_r)).astype(BF16)
        s_ref[h] = s_pl * jnp.exp(g_tot) + _dot(kd, v_new)
        if par == 0:
            o_pair = o_pl
        else:
            o_ref[:, (h // 2) * LANES:(h // 2 + 1) * LANES] = o_pair + o_pl


def _deltanet(qn, vk, kt, aux, auxt, B, S, W, d):
    T = qn.shape[0]
    C = DN_BLOCK
    nb = S // C

    def blk(b, i):
        return b * nb + (i if d == 0 else nb - 1 - i)

    nq = DN_HEADS * DN_DK
    tri, diag16, offc = W["dn_masks"][d]
    return pl.pallas_call(
        functools.partial(_deltanet_kernel, d=d),
        grid=(B, nb),
        in_specs=[
            pl.BlockSpec((C, nq), lambda b, i: (blk(b, i), 0)),
            pl.BlockSpec((C, 2 * nq), lambda b, i: (blk(b, i), 0)),
            pl.BlockSpec((nq, C), lambda b, i: (0, blk(b, i))),
            pl.BlockSpec((1, C, 2 * DN_HEADS), lambda b, i: (d, blk(b, i), 0)),
            pl.BlockSpec((1, 2 * DN_HEADS, C), lambda b, i: (d, 0, blk(b, i))),
            _const_spec((2, C, C)), _const_spec((C, C)), _const_spec((len(_MERGE_SIZES), C // 2, C)),
        ],
        out_specs=pl.BlockSpec((C, DN_HEADS * DN_DV), lambda b, i: (blk(b, i), 0)),
        out_shape=jax.ShapeDtypeStruct((T, DN_HEADS * DN_DV), F32),
        scratch_shapes=[pltpu.VMEM((DN_HEADS, DN_DK, LANES), F32)]
        + [pltpu.VMEM((DN_HEADS, C, C), BF16)] * 4
        + [pltpu.VMEM((DN_HEADS, C, LANES), BF16)] * 2,
        compiler_params=_params(("parallel", "arbitrary")),
        name="deltanet_fwd" if d == 0 else "deltanet_bwd",
    )(qn, vk, kt, aux, auxt, tri, diag16, offc)


def _layer_norm(v, g, b):
    mu = jnp.mean(v, axis=-1, keepdims=True)
    c = v - mu
    var = jnp.mean(c * c, axis=-1, keepdims=True)
    return c * lax.rsqrt(var + 1e-5) * g + b


def _mix_kernel(x_ref, attn_ref, of_ref, ob_ref, z_ref, gates_ref, p_ref,
                woa_ref, wod_ref, wout_ref, e8_ref, e8t_ref, dnorm_ref, ln1g_ref, ln1b_ref,
                wpg_ref, wpp_ref, rwh_ref, rwl_ref, rb_ref, ustrict_ref,
                r_ref, hb_ref, ti_ref, tg_ref, rank_ref, cnt_ref, run_ref):
    @pl.when(pl.program_id(0) == 0)
    def _():
        run_ref[...] = jnp.zeros_like(run_ref)

    oa = _dot(attn_ref[...], woa_ref[...])
    o = of_ref[...] + ob_ref[...]
    hi, lo = _split_bf16(o * o)
    ms = (_dot(hi, e8_ref[...]) + _dot(lo, e8_ref[...])) * (1.0 / DN_DV)
    ih, il = _split_bf16(lax.rsqrt(ms + 1e-6))
    sc = _dot(ih, e8t_ref[...]) + _dot(il, e8t_ref[...])
    zf = z_ref[...].astype(F32)
    od_in = o * sc * dnorm_ref[...] * (zf * (1.0 / (1.0 + jnp.exp(-zf))))
    od = _dot(od_in.astype(BF16), wod_ref[...])
    mix = gates_ref[:, :D_MODEL].astype(F32) * oa + gates_ref[:, D_MODEL:].astype(F32) * od
    mo = _dot(mix.astype(BF16), wout_ref[...])
    h = _layer_norm(DEEPNORM_ALPHA * x_ref[...] + mo, ln1g_ref[...], ln1b_ref[...])
    hb = h.astype(BF16)
    hb_ref[...] = _pack_halves(h)
    pg = _dot(hb, wpg_ref[...])
    pp = _dot(p_ref[...].astype(BF16), wpp_ref[...])
    r_ref[...] = DEEPNORM_ALPHA * h + pp * (1.0 / (1.0 + jnp.exp(-pg)))

    hl = (h - hb.astype(F32)).astype(BF16)
    logits = (_dot_nt(rwh_ref[...], hb) + _dot_nt(rwh_ref[...], hl)
              + _dot_nt(rwl_ref[...], hb) + rb_ref[...])
    eid = lax.broadcasted_iota(jnp.int32, logits.shape, 0)
    vals = []
    run = run_ref[:, 0:1]
    for k in range(TOP_K):
        m = jnp.max(logits, axis=0, keepdims=True)
        idx = jnp.min(jnp.where(logits == m, eid, N_EXPERTS), axis=0, keepdims=True)
        ti_ref[k:k + 1, :] = idx
        vals.append(m)
        hit = eid == idx
        logits = jnp.where(hit, -jnp.inf, logits)
        onehot = hit.astype(F32)
        earlier = _dot(onehot.astype(BF16), ustrict_ref[...])
        rank = jnp.sum(onehot * (run + earlier), axis=0, keepdims=True)
        rank_ref[k:k + 1, :] = rank.astype(jnp.int32)
        run = run + jnp.sum(onehot, axis=1, keepdims=True)
    run_ref[...] = jnp.broadcast_to(run, run_ref.shape)
    cnt_ref[...] = jnp.broadcast_to(run, cnt_ref.shape)
    es = [jnp.exp(v - vals[0]) for v in vals]
    den = es[0] + es[1] + es[2] + es[3]
    for k in range(TOP_K):
        tg_ref[k:k + 1, :] = es[k] / den


def _mix(x2, attn, o_f, o_b, z, gates, p2, W):
    tm = MIX_TILE
    T = x2.shape[0]
    row = lambda i: (i, 0)
    nd = DN_HEADS * DN_DV
    return pl.pallas_call(
        _mix_kernel,
        grid=(T // tm,),
        in_specs=[
            pl.BlockSpec((tm, D_MODEL), row),
            pl.BlockSpec((tm, MLA_HEADS * HEAD_PAD), row),
            pl.BlockSpec((tm, nd), row),
            pl.BlockSpec((tm, nd), row),
            pl.BlockSpec((tm, nd), row),
            pl.BlockSpec((tm, 2 * D_MODEL), row),
            pl.BlockSpec((tm, PLE_DIM), row),
            _const_spec((MLA_HEADS * HEAD_PAD, D_MODEL)), _const_spec((nd, D_MODEL)),
            _const_spec((D_MODEL, D_MODEL)),
            _const_spec((nd, LANES)), _const_spec((LANES, nd)), _const_spec((1, nd)),
            _const_spec((1, D_MODEL)), _const_spec((1, D_MODEL)),
            _const_spec((D_MODEL, D_MODEL)), _const_spec((PLE_DIM, D_MODEL)),
            _const_spec((N_EXPERTS, D_MODEL)), _const_spec((N_EXPERTS, D_MODEL)),
            _const_spec((N_EXPERTS, 1)),
            _const_spec((tm, tm)),
        ],
        out_specs=[
            pl.BlockSpec((tm, D_MODEL), row),
            pl.BlockSpec((tm, D_MODEL // 2), row),
            pl.BlockSpec((TOP_K, tm), lambda i: (0, i)),
            pl.BlockSpec((TOP_K, tm), lambda i: (0, i)),
            pl.BlockSpec((TOP_K, tm), lambda i: (0, i)),
            _const_spec((N_EXPERTS, LANES)),
        ],
        out_shape=[
            jax.ShapeDtypeStruct((T, D_MODEL), F32),
            jax.ShapeDtypeStruct((T, D_MODEL // 2), jnp.uint32),
            jax.ShapeDtypeStruct((TOP_K, T), jnp.int32),
            jax.ShapeDtypeStruct((TOP_K, T), F32),
            jax.ShapeDtypeStruct((TOP_K, T), jnp.int32),
            jax.ShapeDtypeStruct((N_EXPERTS, LANES), F32),
        ],
        scratch_shapes=[pltpu.VMEM((N_EXPERTS, LANES), F32)],
        compiler_params=_params(("arbitrary",)),
        name="mix",
    )(x2, attn, o_f, o_b, z, gates, p2, W["woa"], W["wod"], W["wout"], W["e8"], W["e8t"], W["dnorm"],
      W["ln1g"], W["ln1b"], W["wpg"], W["wpp"], W["rwh"], W["rwl"], W["rb"], W["ustrict"])


_CAST_ROWS = 256


def _moe_kernel(blk_e_ref, nvalid_ref, xs_ref, wgu32_ref, bgu_ref, wd32_ref, bd_ref, y_ref,
                wgu_ref, wd_ref, *, fc):
    i = pl.program_id(0)
    valid = i < nvalid_ref[0]
    new_expert = (i == 0) | (blk_e_ref[i] != blk_e_ref[jnp.maximum(i - 1, 0)])

    @pl.when(valid & new_expert)
    def _():
        for r in range(0, D_MODEL, _CAST_ROWS):
            wgu_ref[r:r + _CAST_ROWS, :] = wgu32_ref[0, r:r + _CAST_ROWS, :].astype(BF16)
        for r in range(0, D_FF, _CAST_ROWS):
            wd_ref[r:r + _CAST_ROWS, :] = wd32_ref[0, r:r + _CAST_ROWS, :].astype(BF16)

    @pl.when(valid)
    def _():
        xs = jnp.concatenate(_unpack_halves(xs_ref[...]), axis=1).astype(BF16)
        acc = None
        for c in range(D_FF // fc):
            lo, hi = c * fc, (c + 1) * fc
            gate = _dot(xs, wgu_ref[:, lo:hi]) + bgu_ref[0, :, lo:hi]
            up = _dot(xs, wgu_ref[:, D_FF + lo:D_FF + hi]) + bgu_ref[0, :, D_FF + lo:D_FF + hi]
            gate = jnp.minimum(gate, SWIGLU_LIMIT)
            up = jnp.clip(up, -SWIGLU_LIMIT, SWIGLU_LIMIT)
            act = gate * (1.0 / (1.0 + jnp.exp(-SWIGLU_ALPHA * gate))) * (up + 1.0)
            part = _dot(act.astype(BF16), wd_ref[lo:hi, :])
            acc = part if acc is None else acc + part
        y_ref[...] = _pack_halves(acc + bd_ref[0])

    @pl.when(jnp.logical_not(valid))
    def _():
        y_ref[...] = jnp.zeros_like(y_ref)


def _moe(xs, blk_e, nvalid, W, bm, fc=512):
    P = xs.shape[0]
    grid_spec = pltpu.PrefetchScalarGridSpec(
        num_scalar_prefetch=2,
        grid=(P // bm,),
        in_specs=[
            pl.BlockSpec((bm, D_MODEL // 2), lambda i, be, nv: (i, 0)),
            pl.BlockSpec((1, D_MODEL, 2 * D_FF), lambda i, be, nv: (be[i], 0, 0)),
            pl.BlockSpec((1, 1, 2 * D_FF), lambda i, be, nv: (be[i], 0, 0)),
            pl.BlockSpec((1, D_FF, D_MODEL), lambda i, be, nv: (be[i], 0, 0)),
            pl.BlockSpec((1, 1, D_MODEL), lambda i, be, nv: (be[i], 0, 0)),
        ],
        out_specs=pl.BlockSpec((bm, D_MODEL // 2), lambda i, be, nv: (i, 0)),
        scratch_shapes=[pltpu.VMEM((D_MODEL, 2 * D_FF), BF16), pltpu.VMEM((D_FF, D_MODEL), BF16)],
    )
    return pl.pallas_call(
        functools.partial(_moe_kernel, fc=fc),
        grid_spec=grid_spec,
        out_shape=jax.ShapeDtypeStruct((P, D_MODEL // 2), jnp.uint32),
        compiler_params=_params(("arbitrary",)),
        name="moe",
    )(blk_e, nvalid, xs, W["wgu"], W["bgu"], W["wd"], W["bd"])


def _final_kernel(r_ref, yg_ref, tg_ref, g_ref, b_ref, y_ref):
    ffn_lo = ffn_hi = None
    for k in range(TOP_K):
        lo, hi = _unpack_halves(yg_ref[k])
        g = tg_ref[:, k:k + 1]
        ffn_lo = lo * g if ffn_lo is None else ffn_lo + lo * g
        ffn_hi = hi * g if ffn_hi is None else ffn_hi + hi * g
    acc = r_ref[...] + jnp.concatenate([ffn_lo, ffn_hi], axis=1)
    y_ref[...] = _layer_norm(acc, g_ref[...], b_ref[...])


def _final(r, yg, tg, W, tm=512):
    T = r.shape[0]
    tm = min(tm, T)
    row = lambda i: (i, 0)
    return pl.pallas_call(
        _final_kernel,
        grid=(T // tm,),
        in_specs=[pl.BlockSpec((tm, D_MODEL), row),
                  pl.BlockSpec((TOP_K, tm, D_MODEL // 2), lambda i: (0, i, 0)),
                  pl.BlockSpec((tm, TOP_K), row),
                  _const_spec((1, D_MODEL)), _const_spec((1, D_MODEL))],
        out_specs=pl.BlockSpec((tm, D_MODEL), row),
        out_shape=jax.ShapeDtypeStruct((T, D_MODEL), F32),
        compiler_params=_params(("parallel",)),
        name="final_ln",
    )(r, yg, tg, W["ln2g"], W["ln2b"])


def _pad_heads(w, n_heads, width, start, size, dst=0):
    K = w.shape[0]
    w3 = w.reshape(K, n_heads, width)[:, :, start:start + size]
    out = jnp.zeros((K, n_heads, HEAD_PAD), w.dtype)
    out = out.at[:, :, dst:dst + size].set(w3)
    return out.reshape(K, n_heads * HEAD_PAD)


def _prep_weights(w_in, q_a_norm, w_uq, kv_a_norm, w_ukv, w_o_attn, dn_conv, dn_a_log, dn_dt_bias,
                  dn_norm, w_o_dn, w_out, ln1_g, ln1_b, router_w, router_b, w_gate_up, b_gate_up,
                  w_down, b_down, ple_w_proj, ple_w_gate, ln2_g, ln2_b):
    W = {}
    half = ROPE_DIM // 2
    o = 0
    cq = w_in[:, o:o + Q_LORA]; o += Q_LORA
    ckv = w_in[:, o:o + KV_LORA]; o += KV_LORA
    kr = w_in[:, o:o + ROPE_DIM]; o += ROPE_DIM
    nqk = DN_HEADS * DN_DK
    dq = w_in[:, o:o + nqk]; o += nqk
    dk = w_in[:, o:o + nqk]; o += nqk
    dv = w_in[:, o:o + DN_HEADS * DN_DV]; o += DN_HEADS * DN_DV
    dz = w_in[:, o:o + DN_HEADS * DN_DV]; o += DN_HEADS * DN_DV
    da = w_in[:, o:o + 2 * DN_HEADS]; o += 2 * DN_HEADS
    db = w_in[:, o:o + 2 * DN_HEADS]; o += 2 * DN_HEADS
    gate = w_in[:, o:o + 2 * D_MODEL]

    def lane_block(parts):
        w = jnp.concatenate(parts, axis=1)
        return jnp.pad(w, ((0, 0), (0, LANES - w.shape[1])))

    zeros64 = jnp.zeros((D_MODEL, NOPE_DIM), F32)
    kr_blk = lane_block([zeros64, kr])
    krs_blk = lane_block([zeros64, kr[:, half:], kr[:, :half]])
    ab = [lane_block([da[:, d * DN_HEADS:(d + 1) * DN_HEADS], db[:, d * DN_HEADS:(d + 1) * DN_HEADS]])
          for d in range(2)]

    def interleave_vk(v, k):
        lead = v.shape[:-1]
        v3 = v.reshape(lead + (DN_HEADS, DN_DV))
        k3 = k.reshape(lead + (DN_HEADS, DN_DK))
        return jnp.concatenate([v3, k3], axis=-1).reshape(lead + (DN_HEADS * (DN_DV + DN_DK),))

    W["w1"] = jnp.concatenate([cq, ckv, kr_blk, krs_blk, ab[0], ab[1], dq, interleave_vk(dv, dk), dz, gate],
                              axis=1).astype(BF16)
    W["qan"] = q_a_norm.reshape(1, Q_LORA)
    W["kvan"] = kv_a_norm.reshape(1, KV_LORA)

    qw = NOPE_DIM + ROPE_DIM
    wq_nope = _pad_heads(w_uq, MLA_HEADS, qw, 0, NOPE_DIM, 0)
    wq_r1 = _pad_heads(w_uq, MLA_HEADS, qw, NOPE_DIM, half, NOPE_DIM)
    wq_r2 = _pad_heads(w_uq, MLA_HEADS, qw, NOPE_DIM + half, half, NOPE_DIM + half)
    W["wq"] = (wq_nope + wq_r1 + wq_r2).astype(BF16)
    wq_s1 = _pad_heads(w_uq, MLA_HEADS, qw, NOPE_DIM + half, half, NOPE_DIM)
    wq_s2 = _pad_heads(w_uq, MLA_HEADS, qw, NOPE_DIM, half, NOPE_DIM + half)
    W["wqs"] = (wq_s1 + wq_s2).astype(BF16)
    kvw = NOPE_DIM + V_DIM
    W["wk"] = _pad_heads(w_ukv, MLA_HEADS, kvw, 0, NOPE_DIM, 0).astype(BF16)
    W["wv"] = _pad_heads(w_ukv, MLA_HEADS, kvw, NOPE_DIM, V_DIM, 0).astype(BF16)

    neg_a = -jnp.exp(dn_a_log.astype(F32))
    abp = jnp.zeros((8, LANES), F32)
    for d in range(2):
        abp = abp.at[2 * d, :DN_HEADS].set(neg_a[d])
        abp = abp.at[2 * d + 1, :DN_HEADS].set(dn_dt_bias[d].astype(F32))
    W["abp"] = abp

    cw = jnp.concatenate([dn_conv[:, :nqk], interleave_vk(dn_conv[:, 2 * nqk:], dn_conv[:, nqk:2 * nqk])], axis=1)
    W["convw"] = jnp.pad(cw.astype(F32), ((0, 8 - CONV_K), (0, 0)))

    def group_indicator(width, group):
        e = (np.arange(width)[:, None] // group == np.arange(LANES)[None, :]).astype(np.float32)
        return e

    eq = group_indicator(nqk, DN_DK)
    W["eq"] = jnp.asarray(eq, BF16)
    W["eqt"] = jnp.asarray(eq.T, BF16)
    evk = group_indicator(2 * nqk, DN_DK)
    W["evk"] = jnp.asarray(evk, BF16)
    W["evkt"] = jnp.asarray(evk.T, BF16)
    W["e8"] = W["eq"]
    W["e8t"] = W["eqt"]
    r = np.arange(DN_BLOCK)
    W["tri"] = jnp.asarray(np.stack([r[:, None] >= r[None, :], r[:, None] <= r[None, :]]).astype(np.float32), BF16)
    rp = np.arange(DN_PREP_TILE)
    W["conv_shift"] = jnp.asarray(
        np.stack([rp[None, :] == rp[:, None] + (j - CONV_K // 2) for j in range(CONV_K)]).astype(np.float32), BF16)
    W["dn_masks"] = [(jnp.asarray(t), jnp.asarray(g, BF16), jnp.asarray(o, BF16))
                     for t, g, o in (_dn_masks(0), _dn_masks(1))]
    rt = np.arange(MIX_TILE)
    W["ustrict"] = jnp.asarray((rt[:, None] < rt[None, :]).astype(np.float32), BF16)

    woa = w_o_attn.reshape(MLA_HEADS, V_DIM, D_MODEL)
    woa = jnp.pad(woa, ((0, 0), (0, HEAD_PAD - V_DIM), (0, 0)))
    W["woa"] = woa.reshape(MLA_HEADS * HEAD_PAD, D_MODEL).astype(BF16)
    W["wod"] = w_o_dn.astype(BF16)
    W["wout"] = w_out.astype(BF16)
    W["dnorm"] = jnp.tile(dn_norm.astype(F32), DN_HEADS).reshape(1, DN_HEADS * DN_DV)
    W["ln1g"] = ln1_g.reshape(1, D_MODEL)
    W["ln1b"] = ln1_b.reshape(1, D_MODEL)
    W["ln2g"] = ln2_g.reshape(1, D_MODEL)
    W["ln2b"] = ln2_b.reshape(1, D_MODEL)
    W["wpg"] = ple_w_gate.astype(BF16)
    W["wpp"] = ple_w_proj.astype(BF16)
    rwt = router_w.T.astype(F32)
    W["rwh"], W["rwl"] = _split_bf16(rwt)
    W["rb"] = router_b.reshape(N_EXPERTS, 1).astype(F32)
    W["wgu"] = w_gate_up.astype(F32)
    W["bgu"] = b_gate_up.reshape(N_EXPERTS, 1, 2 * D_FF).astype(F32)
    W["wd"] = w_down.astype(F32)
    W["bd"] = b_down.reshape(N_EXPERTS, 1, D_MODEL).astype(F32)
    return W


def _rope_tables(S):
    half = ROPE_DIM // 2
    inv = ROPE_THETA ** (-jnp.arange(0, ROPE_DIM, 2, dtype=F32) / ROPE_DIM)
    ang = jnp.arange(S, dtype=F32)[:, None] * inv[None, :]
    cos, sin = jnp.cos(ang), jnp.sin(ang)
    c = (NOPE_DIM + ROPE_DIM) ** -0.5 * math.log2(math.e)
    pad = jnp.zeros((S, HEAD_PAD - NOPE_DIM - ROPE_DIM), F32)
    cos_blk = jnp.concatenate([cos, cos, pad], axis=1)
    sin_blk = jnp.concatenate([-sin, sin, pad], axis=1)
    cosq = jnp.concatenate([jnp.ones((S, NOPE_DIM), F32), cos_blk], axis=1) * c
    sinq = jnp.concatenate([jnp.zeros((S, NOPE_DIM), F32), sin_blk], axis=1) * c
    cosk = jnp.concatenate([jnp.zeros((S, NOPE_DIM), F32), cos_blk], axis=1)
    sink = jnp.concatenate([jnp.zeros((S, NOPE_DIM), F32), sin_blk], axis=1)
    return cosq, sinq, cosk, sink


def _dest_kernel(pstart_ref, ti_ref, rank_ref, dest_ref):
    ti = ti_ref[...]
    dest = rank_ref[...]
    for e in range(N_EXPERTS):
        dest = dest + jnp.where(ti == e, pstart_ref[e], 0)
    dest_ref[...] = dest


def _dest(p_start, top_i, rank, tile=8192):
    T = top_i.shape[1]
    tile = min(tile, T)
    spec = pl.BlockSpec((TOP_K, tile), lambda i, ps: (0, i))
    return pl.pallas_call(
        _dest_kernel,
        grid_spec=pltpu.PrefetchScalarGridSpec(num_scalar_prefetch=1, grid=(T // tile,),
                                               in_specs=[spec, spec], out_specs=spec),
        out_shape=jax.ShapeDtypeStruct((TOP_K, T), jnp.int32),
        compiler_params=_params(("parallel",)),
        name="slot_index",
    )(p_start, top_i, rank)


SC_CHUNK = 128


def _sc_mesh():
    info = plsc.get_sparse_core_info()
    mesh = plsc.VectorSubcoreMesh(core_axis_name="c", subcore_axis_name="s")
    return mesh, info.num_cores, info.num_cores * info.num_subcores


def _sc_dispatch(rows, dest3, P):
    T, D = rows.shape
    K = dest3.shape[0]
    mesh, n_cores, n_workers = _sc_mesh()
    n_chunks = T // (n_workers * SC_CHUNK)

    @functools.partial(
        pl.kernel, mesh=mesh, out_type=jax.ShapeDtypeStruct((P, D), rows.dtype),
        scratch_types=[pltpu.VMEM((K, SC_CHUNK), jnp.int32), pltpu.VMEM((SC_CHUNK, D), rows.dtype),
                       pltpu.SemaphoreType.DMA])
    def dispatch(rows_hbm, dest_hbm, out_hbm, idx_v, rows_v, sem):
        worker = lax.axis_index("s") * n_cores + lax.axis_index("c")

        @pl.loop(0, n_chunks)
        def _(j):
            g = worker * n_chunks + j
            base = pl.multiple_of(g * SC_CHUNK, SC_CHUNK)
            pltpu.sync_copy(rows_hbm.at[pl.ds(base, SC_CHUNK)], rows_v)
            pltpu.sync_copy(dest_hbm.at[:, g], idx_v)
            for k in range(K):
                pltpu.async_copy(rows_v, out_hbm.at[idx_v.at[k]], sem).wait()

    return dispatch(rows, dest3)


def _sc_gather(table, idx):
    M = idx.shape[0]
    D = table.shape[1]
    mesh, n_cores, n_workers = _sc_mesh()
    n_chunks = M // (n_workers * SC_CHUNK)

    @functools.partial(
        pl.kernel, mesh=mesh, out_type=jax.ShapeDtypeStruct((M, D), table.dtype),
        scratch_types=[pltpu.VMEM((SC_CHUNK,), jnp.int32), pltpu.VMEM((SC_CHUNK, D), table.dtype),
                       pltpu.SemaphoreType.DMA])
    def gather(table_hbm, idx_hbm, out_hbm, idx_v, rows_v, sem):
        worker = lax.axis_index("s") * n_cores + lax.axis_index("c")

        @pl.loop(0, n_chunks)
        def _(j):
            base = pl.multiple_of((worker * n_chunks + j) * SC_CHUNK, SC_CHUNK)
            pltpu.sync_copy(idx_hbm.at[pl.ds(base, SC_CHUNK)], idx_v)
            pltpu.async_copy(table_hbm.at[idx_v], rows_v, sem).wait()
            pltpu.sync_copy(rows_v, out_hbm.at[pl.ds(base, SC_CHUNK)])

    return gather(table, idx)


def _route(top_i, rank, counts, T, bm):
    A = TOP_K * T
    counts = counts.astype(jnp.int32)
    padded = ((counts + bm - 1) // bm) * bm
    p_end = jnp.cumsum(padded)
    p_start = p_end - padded
    dest = _dest(p_start, top_i, rank)
    nblk = A // bm + N_EXPERTS
    blk_start = jnp.arange(nblk, dtype=jnp.int32) * bm
    blk_e = jnp.minimum(jnp.sum(p_end[None, :] <= blk_start[:, None], axis=1), N_EXPERTS - 1).astype(jnp.int32)
    nvalid = (p_end[-1] // bm).astype(jnp.int32).reshape(1)
    return dest, nblk * bm, blk_e, nvalid


def _layer(x, p, W, bm):
    B, S, _ = x.shape
    T = B * S
    x2 = x.reshape(T, D_MODEL)
    p2 = p.reshape(T, PLE_DIM)
    Wl = dict(W)
    Wl["cosq"], Wl["sinq"], Wl["cosk"], Wl["sink"] = _rope_tables(S)

    q, k, v, dn, z, gates, gb = _in_proj(x2, S, Wl)
    attn = _attention(q, k, v, B, S)
    qn, vk, aux = _dn_prep(dn, gb, S, Wl)
    kt = vk.reshape(T, DN_HEADS, 2, DN_DK)[:, :, 1, :].reshape(T, DN_HEADS * DN_DK).T
    auxt = jnp.swapaxes(aux, 1, 2)
    o_f = _deltanet(qn, vk, kt, aux, auxt, B, S, Wl, 0)
    o_b = _deltanet(qn, vk, kt, aux, auxt, B, S, Wl, 1)
    r, hb, top_i, top_g, rank, cnt = _mix(x2, attn, o_f, o_b, z, gates, p2, Wl)

    dest, P, blk_e, nvalid = _route(top_i, rank, cnt[:, 0], T, bm)
    xs = _sc_dispatch(hb, dest.reshape(TOP_K, T // SC_CHUNK, SC_CHUNK), P)
    yb = _moe(xs, blk_e, nvalid, Wl, bm)
    yg = _sc_gather(yb, dest.reshape(TOP_K * T)).reshape(TOP_K, T, D_MODEL // 2)
    y = _final(r, yg, top_g.T, Wl)
    return y.reshape(B, S, D_MODEL)


def kernel(x_prompt, x_sample, p_prompt, p_sample, w_in, q_a_norm, w_uq, kv_a_norm, w_ukv, w_o_attn, dn_conv, dn_a_log, dn_dt_bias, dn_norm, w_o_dn, w_out, ln1_g, ln1_b, router_w, router_b, w_gate_up, b_gate_up, w_down, b_down, ple_w_proj, ple_w_gate, ln2_g, ln2_b):
    y_prompt, y_sample = x_prompt, x_sample
    for l in range(DEPTH):
        W = _prep_weights(w_in[l], q_a_norm[l], w_uq[l], kv_a_norm[l], w_ukv[l], w_o_attn[l], dn_conv[l],
                          dn_a_log[l], dn_dt_bias[l], dn_norm[l], w_o_dn[l], w_out[l], ln1_g[l], ln1_b[l],
                          router_w[l], router_b[l], w_gate_up[l], b_gate_up[l], w_down[l], b_down[l],
                          ple_w_proj[l], ple_w_gate[l], ln2_g[l], ln2_b[l])
        y_prompt = _layer(y_prompt, p_prompt[l], W, bm=MOE_BLOCK)
        y_sample = _layer(y_sample, p_sample[l], W, bm=MOE_BLOCK)
    return (y_prompt, y_sample)
```

```python
import functools
import math

import numpy as np
import jax
import jax.numpy as jnp
from jax import lax
from jax.experimental import pallas as pl
from jax.experimental.pallas import tpu as pltpu
from jax.experimental.pallas import tpu_sc as plsc

D_MODEL = 1024
MLA_HEADS = 8
Q_LORA = 256
KV_LORA = 128
NOPE_DIM = 64
ROPE_DIM = 32
V_DIM = 64
ROPE_THETA = 10000.0
DN_HEADS = 8
DN_DK = 64
DN_DV = 64
CONV_K = 5
N_EXPERTS = 32
TOP_K = 4
D_FF = 1024
SWIGLU_LIMIT = 7.0
SWIGLU_ALPHA = 1.702
PLE_DIM = 256
DEPTH = 1
DEEPNORM_ALPHA = (2.0 * DEPTH) ** 0.25

LANES = 128
HEAD_PAD = 128
DN_BLOCK = 256
DN_PREP_TILE = 256
MOE_BLOCK = 512
MIX_TILE = 512
VMEM_LIMIT = 56 * 1024 * 1024

_C_CQ = 0
_C_CKV = _C_CQ + Q_LORA
_C_MISC0 = _C_CKV + KV_LORA
_C_MISC1 = _C_MISC0 + LANES
_C_DNQ = _C_MISC1 + LANES
_C_DNVK = _C_DNQ + DN_HEADS * DN_DK
_C_Z = _C_DNVK + DN_HEADS * (DN_DK + DN_DV)
_C_GATE = _C_Z + DN_HEADS * DN_DV
_C_END = _C_GATE + 2 * D_MODEL

BF16 = jnp.bfloat16
F32 = jnp.float32


def _dot(a, b):
    return jnp.dot(a, b, preferred_element_type=F32)


def _dot_nt(a, b):
    return lax.dot_general(a, b, (((1,), (1,)), ((), ())), preferred_element_type=F32)


def _split_bf16(x):
    hi = x.astype(BF16)
    lo = (x - hi.astype(F32)).astype(BF16)
    return hi, lo


_HI_HALFWORD = 0xFFFF0000


def _pack_halves(x):
    w = x.shape[1] // 2
    bits = lax.bitcast_convert_type(x.astype(BF16).astype(F32), jnp.uint32)
    return (bits[:, :w] >> 16) | (bits[:, w:] & jnp.uint32(_HI_HALFWORD))


def _unpack_halves(words):
    lo = lax.bitcast_convert_type(words << 16, F32)
    hi = lax.bitcast_convert_type(words & jnp.uint32(_HI_HALFWORD), F32)
    return lo, hi


def _const_spec(shape):
    n = len(shape)
    return pl.BlockSpec(shape, lambda *_: (0,) * n, pipeline_mode=pl.Buffered(1))


def _params(sem):
    return pltpu.CompilerParams(dimension_semantics=sem, vmem_limit_bytes=VMEM_LIMIT)


def _in_proj_kernel(x_ref, w1_ref, qan_ref, kvan_ref, wq_ref, wqs_ref, wk_ref, wv_ref,
                    cosq_ref, sinq_ref, cosk_ref, sink_ref, abp_ref,
                    q_ref, k_ref, v_ref, dn_ref, z_ref, gates_ref, gb_ref):
    xb = x_ref[...].astype(BF16)

    def proj(lo, hi):
        return _dot(xb, w1_ref[:, lo:hi])

    def rms(c, g):
        return (c * lax.rsqrt(jnp.mean(c * c, axis=-1, keepdims=True) + 1e-6) * g).astype(BF16)

    cqn = rms(proj(_C_CQ, _C_CKV), qan_ref[...])
    qa = _dot(cqn, wq_ref[...])
    qb = _dot(cqn, wqs_ref[...])
    ckvn = rms(proj(_C_CKV, _C_MISC0), kvan_ref[...])
    kw = _dot(ckvn, wk_ref[...])
    vw = _dot(ckvn, wv_ref[...])
    misc = (proj(_C_MISC0, _C_MISC1), proj(_C_MISC1, _C_DNQ))
    kr = misc[0] * cosk_ref[...] + misc[1] * sink_ref[...]
    cosq = cosq_ref[...]
    sinq = sinq_ref[...]
    lane = lax.broadcasted_iota(jnp.int32, (1, HEAD_PAD), 1)
    ones_col = (lane == V_DIM).astype(F32)
    for h in range(MLA_HEADS):
        sl = slice(h * HEAD_PAD, (h + 1) * HEAD_PAD)
        q_ref[:, sl] = (qa[:, sl] * cosq + qb[:, sl] * sinq).astype(BF16)
        k_ref[:, sl] = (kw[:, sl] + kr).astype(BF16)
        v_ref[:, sl] = (vw[:, sl] + ones_col).astype(BF16)

    for d in range(2):
        ab = misc[d]
        neg_a = abp_ref[2 * d:2 * d + 1, :]
        dtb = abp_ref[2 * d + 1:2 * d + 2, :]
        t = ab + dtb
        sp = jnp.maximum(t, 0.0) + jnp.log(1.0 + jnp.exp(-jnp.abs(t)))
        g = neg_a * sp
        beta = 1.0 / (1.0 + jnp.exp(-ab))
        gb_ref[d] = jnp.where(lane < DN_HEADS, g, beta)[:, :2 * DN_HEADS]

    dn_ref[...] = proj(_C_DNQ, _C_Z).astype(BF16)
    z_ref[...] = proj(_C_Z, _C_GATE).astype(BF16)
    gl = proj(_C_GATE, _C_END)
    gates_ref[...] = (1.0 / (1.0 + jnp.exp(-gl))).astype(BF16)


def _in_proj(x2, S, W, tm=512):
    T = x2.shape[0]
    nseq = S // tm
    row = lambda i: (i, 0)
    pos = lambda i: (i % nseq, 0)
    dn_w = _C_Z - _C_DNQ
    return pl.pallas_call(
        _in_proj_kernel,
        grid=(T // tm,),
        in_specs=[
            pl.BlockSpec((tm, D_MODEL), row),
            _const_spec((D_MODEL, _C_END)),
            _const_spec((1, Q_LORA)), _const_spec((1, KV_LORA)),
            _const_spec((Q_LORA, MLA_HEADS * HEAD_PAD)), _const_spec((Q_LORA, MLA_HEADS * HEAD_PAD)),
            _const_spec((KV_LORA, MLA_HEADS * HEAD_PAD)), _const_spec((KV_LORA, MLA_HEADS * HEAD_PAD)),
            pl.BlockSpec((tm, HEAD_PAD), pos), pl.BlockSpec((tm, HEAD_PAD), pos),
            pl.BlockSpec((tm, HEAD_PAD), pos), pl.BlockSpec((tm, HEAD_PAD), pos),
            _const_spec((8, LANES)),
        ],
        out_specs=[
            pl.BlockSpec((tm, MLA_HEADS * HEAD_PAD), row),
            pl.BlockSpec((tm, MLA_HEADS * HEAD_PAD), row),
            pl.BlockSpec((tm, MLA_HEADS * HEAD_PAD), row),
            pl.BlockSpec((tm, dn_w), row),
            pl.BlockSpec((tm, DN_HEADS * DN_DV), row),
            pl.BlockSpec((tm, 2 * D_MODEL), row),
            pl.BlockSpec((2, tm, 2 * DN_HEADS), lambda i: (0, i, 0)),
        ],
        out_shape=[
            jax.ShapeDtypeStruct((T, MLA_HEADS * HEAD_PAD), BF16),
            jax.ShapeDtypeStruct((T, MLA_HEADS * HEAD_PAD), BF16),
            jax.ShapeDtypeStruct((T, MLA_HEADS * HEAD_PAD), BF16),
            jax.ShapeDtypeStruct((T, dn_w), BF16),
            jax.ShapeDtypeStruct((T, DN_HEADS * DN_DV), BF16),
            jax.ShapeDtypeStruct((T, 2 * D_MODEL), BF16),
            jax.ShapeDtypeStruct((2, T, 2 * DN_HEADS), F32),
        ],
        compiler_params=_params(("parallel",)),
        name="in_proj",
    )(x2, W["w1"], W["qan"], W["kvan"], W["wq"], W["wqs"], W["wk"], W["wv"],
      W["cosq"], W["sinq"], W["cosk"], W["sink"], W["abp"])


def _attn_kernel(q_ref, k_ref, v_ref, o_ref, *, tk, unroll):
    tq = q_ref.shape[0]
    S = k_ref.shape[0]
    q = q_ref[...]

    def body(j, carry):
        m, acc = carry
        off = pl.multiple_of(j * tk, tk)
        s = _dot_nt(q, k_ref[pl.ds(off, tk), :])
        m_new = jnp.maximum(m, jnp.max(s, axis=-1, keepdims=True))
        p = jnp.exp2(s - m_new).astype(BF16)
        acc = acc * jnp.exp2(m - m_new) + _dot(p, v_ref[pl.ds(off, tk), :])
        return m_new, acc

    m0 = jnp.full((tq, 1), -1e30, F32)
    acc0 = jnp.zeros((tq, HEAD_PAD), F32)
    _, acc = lax.fori_loop(0, S // tk, body, (m0, acc0), unroll=unroll)
    o_ref[...] = (acc / acc[:, V_DIM:V_DIM + 1]).astype(BF16)


def _attention(q, k, v, B, S, tq=1024, tk=2048, unroll=4):
    T = q.shape[0]
    tq = min(tq, S)
    tk = min(tk, S)
    nq = S // tq
    return pl.pallas_call(
        functools.partial(_attn_kernel, tk=tk, unroll=unroll),
        grid=(B, MLA_HEADS, nq),
        in_specs=[
            pl.BlockSpec((tq, HEAD_PAD), lambda b, h, i: (b * nq + i, h)),
            pl.BlockSpec((S, HEAD_PAD), lambda b, h, i: (b, h)),
            pl.BlockSpec((S, HEAD_PAD), lambda b, h, i: (b, h)),
        ],
        out_specs=pl.BlockSpec((tq, HEAD_PAD), lambda b, h, i: (b * nq + i, h)),
        out_shape=jax.ShapeDtypeStruct((T, MLA_HEADS * HEAD_PAD), BF16),
        compiler_params=_params(("parallel", "parallel", "arbitrary")),
        name="attention",
    )(q, k, v)


_HALO = 16


def _dn_prep_kernel(x_ref, prev_ref, next_ref, cw_ref, eq_ref, eqt_ref, evk_ref, evkt_ref,
                    gb_ref, tri_ref, shift_ref, qn_ref, vk_ref, kt_ref, aux_ref, *, nseq):
    i = pl.program_id(0)
    tp = x_ref.shape[0]
    first = (i % nseq) == 0
    last = (i % nseq) == nseq - 1
    half = CONV_K // 2
    xb = x_ref[...]

    y = xb.astype(F32) * cw_ref[half:half + 1, :]
    for j in range(CONV_K):
        if j != half:
            y = y + _dot(shift_ref[j], xb) * cw_ref[j:j + 1, :]

    def edge(slab, row0):
        n = slab.shape[0]
        acc = None
        for j in range(CONV_K):
            shift = (half - j) % n
            rolled = slab if shift == 0 else pltpu.roll(slab, shift, axis=0)
            term = rolled[row0:row0 + 8, :] * cw_ref[j:j + 1, :]
            acc = term if acc is None else acc + term
        return acc

    prev = jnp.where(first, 0.0, prev_ref[...].astype(F32))
    nxt = jnp.where(last, 0.0, next_ref[...].astype(F32))
    top = edge(jnp.concatenate([prev, xb[:_HALO].astype(F32)], axis=0), _HALO)
    bot = edge(jnp.concatenate([xb[tp - _HALO:].astype(F32), nxt], axis=0), _HALO - 8)
    y = jnp.concatenate([top, y[8:tp - 8], bot], axis=0)
    y = y * (1.0 / (1.0 + jnp.exp(-y)))

    def group_scale(v, e_ref, et_ref):
        ss = _dot((v * v).astype(BF16), e_ref[...])
        return _dot(lax.rsqrt(ss + 1e-6).astype(BF16), et_ref[...])

    nq = DN_HEADS * DN_DK
    yq = y[:, :nq]
    qn_ref[...] = (yq * group_scale(yq, eq_ref, eqt_ref) * (DN_DK ** -0.5)).astype(BF16)
    yvk = y[:, nq:]
    sc = group_scale(yvk, evk_ref, evkt_ref)
    lane = lax.broadcasted_iota(jnp.int32, (1, yvk.shape[1]), 1)
    is_k = (lane // DN_DV) % 2 == 1
    vk = yvk * jnp.where(is_k, sc, 1.0)
    vk_ref[...] = vk.astype(BF16)
    vkt = vk.T
    for h in range(DN_HEADS):
        r0 = h * (DN_DV + DN_DK) + DN_DV
        kt_ref[h * DN_DK:(h + 1) * DN_DK, :] = vkt[r0:r0 + DN_DK, :].astype(BF16)

    hcol = lax.broadcasted_iota(jnp.int32, (1, 2 * DN_HEADS), 1) < DN_HEADS
    for d in range(2):
        gb = gb_ref[d]
        for blk in range(tp // DN_BLOCK):
            rs = slice(blk * DN_BLOCK, (blk + 1) * DN_BLOCK)
            g = gb[rs]
            g_hi = g.astype(BF16)
            g_mid, g_lo = _split_bf16(g - g_hi.astype(F32))
            cs = _dot(tri_ref[d], g_hi) + _dot(tri_ref[d], g_mid) + _dot(tri_ref[d], g_lo)
            aux_ref[d, rs, :] = jnp.where(hcol, cs, g)


def _dn_prep(dn, gb, S, W):
    tp = DN_PREP_TILE
    T = dn.shape[0]
    nseq = S // tp
    hb = tp // _HALO
    nh = T // _HALO
    dn_w = dn.shape[1]
    nq = DN_HEADS * DN_DK
    return pl.pallas_call(
        functools.partial(_dn_prep_kernel, nseq=nseq),
        grid=(T // tp,),
        in_specs=[
            pl.BlockSpec((tp, dn_w), lambda i: (i, 0)),
            pl.BlockSpec((_HALO, dn_w), lambda i: (jnp.maximum(i * hb - 1, 0), 0)),
            pl.BlockSpec((_HALO, dn_w), lambda i: (jnp.minimum((i + 1) * hb, nh - 1), 0)),
            _const_spec((8, dn_w)),
            _const_spec((nq, LANES)), _const_spec((LANES, nq)),
            _const_spec((dn_w - nq, LANES)), _const_spec((LANES, dn_w - nq)),
            pl.BlockSpec((2, tp, 2 * DN_HEADS), lambda i: (0, i, 0)),
            _const_spec((2, DN_BLOCK, DN_BLOCK)),
            _const_spec((CONV_K, tp, tp)),
        ],
        out_specs=[
            pl.BlockSpec((tp, nq), lambda i: (i, 0)),
            pl.BlockSpec((tp, dn_w - nq), lambda i: (i, 0)),
            pl.BlockSpec((nq, tp), lambda i: (0, i)),
            pl.BlockSpec((2, tp, 2 * DN_HEADS), lambda i: (0, i, 0)),
        ],
        out_shape=[
            jax.ShapeDtypeStruct((T, nq), BF16),
            jax.ShapeDtypeStruct((T, dn_w - nq), BF16),
            jax.ShapeDtypeStruct((nq, T), BF16),
            jax.ShapeDtypeStruct((2, T, 2 * DN_HEADS), F32),
        ],
        compiler_params=_params(("parallel",)),
        name="dn_prep",
    )(dn, dn, dn, W["convw"], W["eq"], W["eqt"], W["evk"], W["evkt"], gb, W["tri"], W["conv_shift"])


_MERGE_SIZES = (16, 32, 64, 128)


def _active_blocks(d, s):
    return range(1 - d, DN_BLOCK // s, 2)


def _dn_masks(d):
    r = np.arange(DN_BLOCK)[:, None]
    c = np.arange(DN_BLOCK)[None, :]
    rr, cc = (r, c) if d == 0 else (c, r)
    tri = np.stack([rr >= cc, rr > cc]).astype(np.float32)
    diag16 = ((r // 16) == (c // 16)).astype(np.float32)
    offc = []
    for s in _MERGE_SIZES:
        full = ((rr // s) % 2 == 1) & ((rr // s) == (cc // s) + 1)
        rows = np.concatenate([np.arange(b * s, (b + 1) * s) for b in _active_blocks(d, s)])
        assert not np.delete(full, rows, axis=0).any()
        offc.append(full[rows].astype(np.float32))
    return tri, diag16, np.stack(offc)


def _deltanet_kernel(qn_ref, vk_ref, kt_ref, aux_ref, auxt_ref, tri_ref, diag16_ref, offc_ref, o_ref,
                     s_ref, nm_ref, p_ref, x_ref, aqk_ref, uw_ref, vn_ref, *, d):
    i = pl.program_id(1)
    C = DN_BLOCK
    heads = range(DN_HEADS)

    @pl.when(i == 0)
    def _():
        s_ref[...] = jnp.zeros_like(s_ref)

    rowi = lax.broadcasted_iota(jnp.int32, (C, C), 0)
    coli = lax.broadcasted_iota(jnp.int32, (C, C), 1)
    eye = (rowi == coli).astype(F32)
    lane = lax.broadcasted_iota(jnp.int32, (1, LANES), 1)
    r64 = lax.broadcasted_iota(jnp.int32, (DN_DK, LANES), 0)
    c64 = lax.broadcasted_iota(jnp.int32, (DN_DK, LANES), 1)
    zeros_kt = jnp.zeros((DN_DK, C), BF16)
    zeros_s = jnp.zeros((DN_DK, LANES), F32)

    def q_pair(h):
        return qn_ref[:, (h // 2) * LANES:(h // 2 + 1) * LANES]

    def vk_head(h):
        return vk_ref[:, h * LANES:(h + 1) * LANES]

    def kt_head(h):
        return kt_ref[h * DN_DK:(h + 1) * DN_DK, :]

    def gc_col(h):
        return aux_ref[0, :, h:h + 1]

    def beta_col(h):
        return aux_ref[0, :, DN_HEADS + h:DN_HEADS + h + 1]

    def gc_row(h):
        return auxt_ref[0, h:h + 1, :]

    for h in heads:
        kt = kt_head(h)
        kt_for_q = jnp.concatenate([kt, zeros_kt] if h % 2 == 0 else [zeros_kt, kt], axis=0)
        kt_for_k = jnp.concatenate([zeros_kt, kt], axis=0)
        qk = _dot(q_pair(h), kt_for_q)
        kk = _dot(vk_head(h), kt_for_k)
        e0 = jnp.exp(jnp.minimum(gc_col(h) - gc_row(h), 0.0))
        aqk_ref[h] = (qk * e0 * tri_ref[0]).astype(BF16)
        nm_ref[h] = (-(kk * e0 * tri_ref[1]) * beta_col(h)).astype(BF16)

    diag16 = diag16_ref[...]
    for h in heads:
        nd = nm_ref[h] * diag16
        p_ref[h] = (eye + nd.astype(F32)).astype(BF16)
        x_ref[h] = _dot(nd, nd).astype(BF16)
    for it in range(3):
        for h in heads:
            pb = p_ref[h]
            sqb = x_ref[h]
            p_ref[h] = (pb.astype(F32) + _dot(pb, sqb)).astype(BF16)
            if it < 2:
                x_ref[h] = _dot(sqb, sqb).astype(BF16)
    for k, s in enumerate(_MERGE_SIZES):
        blocks = list(_active_blocks(d, s))
        zeros_blk = jnp.zeros((s, C), BF16)

        def active_rows(ref, h):
            return jnp.concatenate([ref[h, b * s:(b + 1) * s, :] for b in blocks], axis=0)

        for h in heads:
            xc = _dot(active_rows(nm_ref, h) * offc_ref[k], p_ref[h]).astype(BF16)
            pieces = []
            for j in range(len(blocks)):
                piece = xc[j * s:(j + 1) * s, :]
                pieces += [zeros_blk, piece] if d == 0 else [piece, zeros_blk]
            x_ref[h] = jnp.concatenate(pieces, axis=0)
        for h in heads:
            pr = active_rows(p_ref, h)
            prn = (pr.astype(F32) + _dot(pr, x_ref[h])).astype(BF16)
            for j, b in enumerate(blocks):
                p_ref[h, b * s:(b + 1) * s, :] = prn[j * s:(j + 1) * s, :]

    for h in heads:
        egc = jnp.exp(gc_col(h))
        rhs = (vk_head(h).astype(F32) * beta_col(h) * jnp.where(lane < DN_DV, 1.0, egc)).astype(BF16)
        uw_ref[h] = _dot(p_ref[h], rhs).astype(BF16)
    for h in heads:
        eye_pl = (c64 == r64 + (h % 2) * DN_DV).astype(F32)
        s_aug = jnp.concatenate([eye_pl, -s_ref[h]], axis=0).astype(BF16)
        vn_ref[h] = _dot(uw_ref[h], s_aug).astype(BF16)
    o_pair = None
    for h in heads:
        par = h % 2
        gc_r = gc_row(h)
        g_tot = gc_r[:, C - 1:C] if d == 0 else gc_r[:, 0:1]
        s_pl = s_ref[h]
        v_new = vn_ref[h]
        s_sel = jnp.concatenate([s_pl, zeros_s] if par == 0 else [zeros_s, s_pl], axis=0).astype(BF16)
        qd = (q_pair(h).astype(F32) * jnp.exp(gc_col(h))).astype(BF16)
        o_pl = _dot(qd, s_sel) + _dot(aqk_ref[h], v_new)
        kd = (kt_head(h).astype(F32) * jnp.exp(g_tot - gc_r)).astype(BF16)
        s_ref[h] = s_pl * jnp.exp(g_tot) + _dot(kd, v_new)
        if par == 0:
            o_pair = o_pl
        else:
            o_ref[:, (h // 2) * LANES:(h // 2 + 1) * LANES] = o_pair + o_pl


def _deltanet(qn, vk, kt, aux, auxt, B, S, W, d):
    T = qn.shape[0]
    C = DN_BLOCK
    nb = S // C

    def blk(b, i):
        return b * nb + (i if d == 0 else nb - 1 - i)

    nq = DN_HEADS * DN_DK
    tri, diag16, offc = W["dn_masks"][d]
    return pl.pallas_call(
        functools.partial(_deltanet_kernel, d=d),
        grid=(B, nb),
        in_specs=[
            pl.BlockSpec((C, nq), lambda b, i: (blk(b, i), 0)),
            pl.BlockSpec((C, 2 * nq), lambda b, i: (blk(b, i), 0)),
            pl.BlockSpec((nq, C), lambda b, i: (0, blk(b, i))),
            pl.BlockSpec((1, C, 2 * DN_HEADS), lambda b, i: (d, blk(b, i), 0)),
            pl.BlockSpec((1, 2 * DN_HEADS, C), lambda b, i: (d, 0, blk(b, i))),
            _const_spec((2, C, C)), _const_spec((C, C)), _const_spec((len(_MERGE_SIZES), C // 2, C)),
        ],
        out_specs=pl.BlockSpec((C, DN_HEADS * DN_DV), lambda b, i: (blk(b, i), 0)),
        out_shape=jax.ShapeDtypeStruct((T, DN_HEADS * DN_DV), F32),
        scratch_shapes=[pltpu.VMEM((DN_HEADS, DN_DK, LANES), F32)]
        + [pltpu.VMEM((DN_HEADS, C, C), BF16)] * 4
        + [pltpu.VMEM((DN_HEADS, C, LANES), BF16)] * 2,
        compiler_params=_params(("parallel", "arbitrary")),
        name="deltanet_fwd" if d == 0 else "deltanet_bwd",
    )(qn, vk, kt, aux, auxt, tri, diag16, offc)


def _layer_norm(v, g, b):
    mu = jnp.mean(v, axis=-1, keepdims=True)
    c = v - mu
    var = jnp.mean(c * c, axis=-1, keepdims=True)
    return c * lax.rsqrt(var + 1e-5) * g + b


def _mix_kernel(x_ref, attn_ref, of_ref, ob_ref, z_ref, gates_ref, p_ref,
                woa_ref, wod_ref, wout_ref, e8_ref, e8t_ref, dnorm_ref, ln1g_ref, ln1b_ref,
                wpg_ref, wpp_ref, rwh_ref, rwl_ref, rb_ref, ustrict_ref,
                r_ref, hb_ref, ti_ref, tg_ref, rank_ref, cnt_ref, run_ref):
    @pl.when(pl.program_id(0) == 0)
    def _():
        run_ref[...] = jnp.zeros_like(run_ref)

    oa = _dot(attn_ref[...], woa_ref[...])
    o = of_ref[...] + ob_ref[...]
    hi, lo = _split_bf16(o * o)
    ms = (_dot(hi, e8_ref[...]) + _dot(lo, e8_ref[...])) * (1.0 / DN_DV)
    ih, il = _split_bf16(lax.rsqrt(ms + 1e-6))
    sc = _dot(ih, e8t_ref[...]) + _dot(il, e8t_ref[...])
    zf = z_ref[...].astype(F32)
    od_in = o * sc * dnorm_ref[...] * (zf * (1.0 / (1.0 + jnp.exp(-zf))))
    od = _dot(od_in.astype(BF16), wod_ref[...])
    mix = gates_ref[:, :D_MODEL].astype(F32) * oa + gates_ref[:, D_MODEL:].astype(F32) * od
    mo = _dot(mix.astype(BF16), wout_ref[...])
    h = _layer_norm(DEEPNORM_ALPHA * x_ref[...] + mo, ln1g_ref[...], ln1b_ref[...])
    hb = h.astype(BF16)
    hb_ref[...] = _pack_halves(h)
    pg = _dot(hb, wpg_ref[...])
    pp = _dot(p_ref[...].astype(BF16), wpp_ref[...])
    r_ref[...] = DEEPNORM_ALPHA * h + pp * (1.0 / (1.0 + jnp.exp(-pg)))

    hl = (h - hb.astype(F32)).astype(BF16)
    logits = (_dot_nt(rwh_ref[...], hb) + _dot_nt(rwh_ref[...], hl)
              + _dot_nt(rwl_ref[...], hb) + rb_ref[...])
    eid = lax.broadcasted_iota(jnp.int32, logits.shape, 0)
    vals = []
    run = run_ref[:, 0:1]
    for k in range(TOP_K):
        m = jnp.max(logits, axis=0, keepdims=True)
        idx = jnp.min(jnp.where(logits == m, eid, N_EXPERTS), axis=0, keepdims=True)
        ti_ref[k:k + 1, :] = idx
        vals.append(m)
        hit = eid == idx
        logits = jnp.where(hit, -jnp.inf, logits)
        onehot = hit.astype(F32)
        earlier = _dot(onehot.astype(BF16), ustrict_ref[...])
        rank = jnp.sum(onehot * (run + earlier), axis=0, keepdims=True)
        rank_ref[k:k + 1, :] = rank.astype(jnp.int32)
        run = run + jnp.sum(onehot, axis=1, keepdims=True)
    run_ref[...] = jnp.broadcast_to(run, run_ref.shape)
    cnt_ref[...] = jnp.broadcast_to(run, cnt_ref.shape)
    es = [jnp.exp(v - vals[0]) for v in vals]
    den = es[0] + es[1] + es[2] + es[3]
    for k in range(TOP_K):
        tg_ref[k:k + 1, :] = es[k] / den


def _mix(x2, attn, o_f, o_b, z, gates, p2, W):
    tm = MIX_TILE
    T = x2.shape[0]
    row = lambda i: (i, 0)
    nd = DN_HEADS * DN_DV
    return pl.pallas_call(
        _mix_kernel,
        grid=(T // tm,),
        in_specs=[
            pl.BlockSpec((tm, D_MODEL), row),
            pl.BlockSpec((tm, MLA_HEADS * HEAD_PAD), row),
            pl.BlockSpec((tm, nd), row),
            pl.BlockSpec((tm, nd), row),
            pl.BlockSpec((tm, nd), row),
            pl.BlockSpec((tm, 2 * D_MODEL), row),
            pl.BlockSpec((tm, PLE_DIM), row),
            _const_spec((MLA_HEADS * HEAD_PAD, D_MODEL)), _const_spec((nd, D_MODEL)),
            _const_spec((D_MODEL, D_MODEL)),
            _const_spec((nd, LANES)), _const_spec((LANES, nd)), _const_spec((1, nd)),
            _const_spec((1, D_MODEL)), _const_spec((1, D_MODEL)),
            _const_spec((D_MODEL, D_MODEL)), _const_spec((PLE_DIM, D_MODEL)),
            _const_spec((N_EXPERTS, D_MODEL)), _const_spec((N_EXPERTS, D_MODEL)),
            _const_spec((N_EXPERTS, 1)),
            _const_spec((tm, tm)),
        ],
        out_specs=[
            pl.BlockSpec((tm, D_MODEL), row),
            pl.BlockSpec((tm, D_MODEL // 2), row),
            pl.BlockSpec((TOP_K, tm), lambda i: (0, i)),
            pl.BlockSpec((TOP_K, tm), lambda i: (0, i)),
            pl.BlockSpec((TOP_K, tm), lambda i: (0, i)),
            _const_spec((N_EXPERTS, LANES)),
        ],
        out_shape=[
            jax.ShapeDtypeStruct((T, D_MODEL), F32),
            jax.ShapeDtypeStruct((T, D_MODEL // 2), jnp.uint32),
            jax.ShapeDtypeStruct((TOP_K, T), jnp.int32),
            jax.ShapeDtypeStruct((TOP_K, T), F32),
            jax.ShapeDtypeStruct((TOP_K, T), jnp.int32),
            jax.ShapeDtypeStruct((N_EXPERTS, LANES), F32),
        ],
        scratch_shapes=[pltpu.VMEM((N_EXPERTS, LANES), F32)],
        compiler_params=_params(("arbitrary",)),
        name="mix",
    )(x2, attn, o_f, o_b, z, gates, p2, W["woa"], W["wod"], W["wout"], W["e8"], W["e8t"], W["dnorm"],
      W["ln1g"], W["ln1b"], W["wpg"], W["wpp"], W["rwh"], W["rwl"], W["rb"], W["ustrict"])


_CAST_ROWS = 256


def _moe_kernel(blk_e_ref, nvalid_ref, xs_ref, wgu32_ref, bgu_ref, wd32_ref, bd_ref, y_ref,
                wgu_ref, wd_ref, *, fc):
    i = pl.program_id(0)
    valid = i < nvalid_ref[0]
    new_expert = (i == 0) | (blk_e_ref[i] != blk_e_ref[jnp.maximum(i - 1, 0)])

    @pl.when(valid & new_expert)
    def _():
        for r in range(0, D_MODEL, _CAST_ROWS):
            wgu_ref[r:r + _CAST_ROWS, :] = wgu32_ref[0, r:r + _CAST_ROWS, :].astype(BF16)
        for r in range(0, D_FF, _CAST_ROWS):
            wd_ref[r:r + _CAST_ROWS, :] = wd32_ref[0, r:r + _CAST_ROWS, :].astype(BF16)

    @pl.when(valid)
    def _():
        xs = jnp.concatenate(_unpack_halves(xs_ref[...]), axis=1).astype(BF16)
        acc = None
        for c in range(D_FF // fc):
            lo, hi = c * fc, (c + 1) * fc
            gate = _dot(xs, wgu_ref[:, lo:hi]) + bgu_ref[0, :, lo:hi]
            up = _dot(xs, wgu_ref[:, D_FF + lo:D_FF + hi]) + bgu_ref[0, :, D_FF + lo:D_FF + hi]
            gate = jnp.minimum(gate, SWIGLU_LIMIT)
            up = jnp.clip(up, -SWIGLU_LIMIT, SWIGLU_LIMIT)
            act = gate * (1.0 / (1.0 + jnp.exp(-SWIGLU_ALPHA * gate))) * (up + 1.0)
            part = _dot(act.astype(BF16), wd_ref[lo:hi, :])
            acc = part if acc is None else acc + part
        y_ref[...] = _pack_halves(acc + bd_ref[0])

    @pl.when(jnp.logical_not(valid))
    def _():
        y_ref[...] = jnp.zeros_like(y_ref)


def _moe(xs, blk_e, nvalid, W, bm, fc=512):
    P = xs.shape[0]
    grid_spec = pltpu.PrefetchScalarGridSpec(
        num_scalar_prefetch=2,
        grid=(P // bm,),
        in_specs=[
            pl.BlockSpec((bm, D_MODEL // 2), lambda i, be, nv: (i, 0)),
            pl.BlockSpec((1, D_MODEL, 2 * D_FF), lambda i, be, nv: (be[i], 0, 0)),
            pl.BlockSpec((1, 1, 2 * D_FF), lambda i, be, nv: (be[i], 0, 0)),
            pl.BlockSpec((1, D_FF, D_MODEL), lambda i, be, nv: (be[i], 0, 0)),
            pl.BlockSpec((1, 1, D_MODEL), lambda i, be, nv: (be[i], 0, 0)),
        ],
        out_specs=pl.BlockSpec((bm, D_MODEL // 2), lambda i, be, nv: (i, 0)),
        scratch_shapes=[pltpu.VMEM((D_MODEL, 2 * D_FF), BF16), pltpu.VMEM((D_FF, D_MODEL), BF16)],
    )
    return pl.pallas_call(
        functools.partial(_moe_kernel, fc=fc),
        grid_spec=grid_spec,
        out_shape=jax.ShapeDtypeStruct((P, D_MODEL // 2), jnp.uint32),
        compiler_params=_params(("arbitrary",)),
        name="moe",
    )(blk_e, nvalid, xs, W["wgu"], W["bgu"], W["wd"], W["bd"])


def _final_kernel(r_ref, yg_ref, tg_ref, g_ref, b_ref, y_ref):
    ffn_lo = ffn_hi = None
    for k in range(TOP_K):
        lo, hi = _unpack_halves(yg_ref[k])
        g = tg_ref[:, k:k + 1]
        ffn_lo = lo * g if ffn_lo is None else ffn_lo + lo * g
        ffn_hi = hi * g if ffn_hi is None else ffn_hi + hi * g
    acc = r_ref[...] + jnp.concatenate([ffn_lo, ffn_hi], axis=1)
    y_ref[...] = _layer_norm(acc, g_ref[...], b_ref[...])


def _final(r, yg, tg, W, tm=512):
    T = r.shape[0]
    tm = min(tm, T)
    row = lambda i: (i, 0)
    return pl.pallas_call(
        _final_kernel,
        grid=(T // tm,),
        in_specs=[pl.BlockSpec((tm, D_MODEL), row),
                  pl.BlockSpec((TOP_K, tm, D_MODEL // 2), lambda i: (0, i, 0)),
                  pl.BlockSpec((tm, TOP_K), row),
                  _const_spec((1, D_MODEL)), _const_spec((1, D_MODEL))],
        out_specs=pl.BlockSpec((tm, D_MODEL), row),
        out_shape=jax.ShapeDtypeStruct((T, D_MODEL), F32),
        compiler_params=_params(("parallel",)),
        name="final_ln",
    )(r, yg, tg, W["ln2g"], W["ln2b"])


def _pad_heads(w, n_heads, width, start, size, dst=0):
    K = w.shape[0]
    w3 = w.reshape(K, n_heads, width)[:, :, start:start + size]
    out = jnp.zeros((K, n_heads, HEAD_PAD), w.dtype)
    out = out.at[:, :, dst:dst + size].set(w3)
    return out.reshape(K, n_heads * HEAD_PAD)


def _prep_weights(w_in, q_a_norm, w_uq, kv_a_norm, w_ukv, w_o_attn, dn_conv, dn_a_log, dn_dt_bias,
                  dn_norm, w_o_dn, w_out, ln1_g, ln1_b, router_w, router_b, w_gate_up, b_gate_up,
                  w_down, b_down, ple_w_proj, ple_w_gate, ln2_g, ln2_b):
    W = {}
    half = ROPE_DIM // 2
    o = 0
    cq = w_in[:, o:o + Q_LORA]; o += Q_LORA
    ckv = w_in[:, o:o + KV_LORA]; o += KV_LORA
    kr = w_in[:, o:o + ROPE_DIM]; o += ROPE_DIM
    nqk = DN_HEADS * DN_DK
    dq = w_in[:, o:o + nqk]; o += nqk
    dk = w_in[:, o:o + nqk]; o += nqk
    dv = w_in[:, o:o + DN_HEADS * DN_DV]; o += DN_HEADS * DN_DV
    dz = w_in[:, o:o + DN_HEADS * DN_DV]; o += DN_HEADS * DN_DV
    da = w_in[:, o:o + 2 * DN_HEADS]; o += 2 * DN_HEADS
    db = w_in[:, o:o + 2 * DN_HEADS]; o += 2 * DN_HEADS
    gate = w_in[:, o:o + 2 * D_MODEL]

    def lane_block(parts):
        w = jnp.concatenate(parts, axis=1)
        return jnp.pad(w, ((0, 0), (0, LANES - w.shape[1])))

    def misc_block(d, rope_cols):
        ab = [da[:, d * DN_HEADS:(d + 1) * DN_HEADS], db[:, d * DN_HEADS:(d + 1) * DN_HEADS]]
        return lane_block(ab + [jnp.zeros((D_MODEL, NOPE_DIM - 2 * DN_HEADS), F32), rope_cols])

    misc0 = misc_block(0, kr)
    misc1 = misc_block(1, jnp.concatenate([kr[:, half:], kr[:, :half]], axis=1))

    def interleave_vk(v, k):
        lead = v.shape[:-1]
        v3 = v.reshape(lead + (DN_HEADS, DN_DV))
        k3 = k.reshape(lead + (DN_HEADS, DN_DK))
        return jnp.concatenate([v3, k3], axis=-1).reshape(lead + (DN_HEADS * (DN_DV + DN_DK),))

    W["w1"] = jnp.concatenate([cq, ckv, misc0, misc1, dq, interleave_vk(dv, dk), dz, gate],
                              axis=1).astype(BF16)
    W["qan"] = q_a_norm.reshape(1, Q_LORA)
    W["kvan"] = kv_a_norm.reshape(1, KV_LORA)

    qw = NOPE_DIM + ROPE_DIM
    wq_nope = _pad_heads(w_uq, MLA_HEADS, qw, 0, NOPE_DIM, 0)
    wq_r1 = _pad_heads(w_uq, MLA_HEADS, qw, NOPE_DIM, half, NOPE_DIM)
    wq_r2 = _pad_heads(w_uq, MLA_HEADS, qw, NOPE_DIM + half, half, NOPE_DIM + half)
    W["wq"] = (wq_nope + wq_r1 + wq_r2).astype(BF16)
    wq_s1 = _pad_heads(w_uq, MLA_HEADS, qw, NOPE_DIM + half, half, NOPE_DIM)
    wq_s2 = _pad_heads(w_uq, MLA_HEADS, qw, NOPE_DIM, half, NOPE_DIM + half)
    W["wqs"] = (wq_s1 + wq_s2).astype(BF16)
    kvw = NOPE_DIM + V_DIM
    W["wk"] = _pad_heads(w_ukv, MLA_HEADS, kvw, 0, NOPE_DIM, 0).astype(BF16)
    W["wv"] = _pad_heads(w_ukv, MLA_HEADS, kvw, NOPE_DIM, V_DIM, 0).astype(BF16)

    neg_a = -jnp.exp(dn_a_log.astype(F32))
    abp = jnp.zeros((8, LANES), F32)
    for d in range(2):
        abp = abp.at[2 * d, :DN_HEADS].set(neg_a[d])
        abp = abp.at[2 * d + 1, :DN_HEADS].set(dn_dt_bias[d].astype(F32))
    W["abp"] = abp

    cw = jnp.concatenate([dn_conv[:, :nqk], interleave_vk(dn_conv[:, 2 * nqk:], dn_conv[:, nqk:2 * nqk])], axis=1)
    W["convw"] = jnp.pad(cw.astype(F32), ((0, 8 - CONV_K), (0, 0)))

    def group_indicator(width, group):
        e = (np.arange(width)[:, None] // group == np.arange(LANES)[None, :]).astype(np.float32)
        return e

    eq = group_indicator(nqk, DN_DK)
    W["eq"] = jnp.asarray(eq, BF16)
    W["eqt"] = jnp.asarray(eq.T, BF16)
    evk = group_indicator(2 * nqk, DN_DK)
    W["evk"] = jnp.asarray(evk, BF16)
    W["evkt"] = jnp.asarray(evk.T, BF16)
    W["e8"] = W["eq"]
    W["e8t"] = W["eqt"]
    r = np.arange(DN_BLOCK)
    W["tri"] = jnp.asarray(np.stack([r[:, None] >= r[None, :], r[:, None] <= r[None, :]]).astype(np.float32), BF16)
    rp = np.arange(DN_PREP_TILE)
    W["conv_shift"] = jnp.asarray(
        np.stack([rp[None, :] == rp[:, None] + (j - CONV_K // 2) for j in range(CONV_K)]).astype(np.float32), BF16)
    W["dn_masks"] = [(jnp.asarray(t), jnp.asarray(g, BF16), jnp.asarray(o, BF16))
                     for t, g, o in (_dn_masks(0), _dn_masks(1))]
    rt = np.arange(MIX_TILE)
    W["ustrict"] = jnp.asarray((rt[:, None] < rt[None, :]).astype(np.float32), BF16)

    woa = w_o_attn.reshape(MLA_HEADS, V_DIM, D_MODEL)
    woa = jnp.pad(woa, ((0, 0), (0, HEAD_PAD - V_DIM), (0, 0)))
    W["woa"] = woa.reshape(MLA_HEADS * HEAD_PAD, D_MODEL).astype(BF16)
    W["wod"] = w_o_dn.astype(BF16)
    W["wout"] = w_out.astype(BF16)
    W["dnorm"] = jnp.tile(dn_norm.astype(F32), DN_HEADS).reshape(1, DN_HEADS * DN_DV)
    W["ln1g"] = ln1_g.reshape(1, D_MODEL)
    W["ln1b"] = ln1_b.reshape(1, D_MODEL)
    W["ln2g"] = ln2_g.reshape(1, D_MODEL)
    W["ln2b"] = ln2_b.reshape(1, D_MODEL)
    W["wpg"] = ple_w_gate.astype(BF16)
    W["wpp"] = ple_w_proj.astype(BF16)
    rwt = router_w.T.astype(F32)
    W["rwh"], W["rwl"] = _split_bf16(rwt)
    W["rb"] = router_b.reshape(N_EXPERTS, 1).astype(F32)
    W["wgu"] = w_gate_up.astype(F32)
    W["bgu"] = b_gate_up.reshape(N_EXPERTS, 1, 2 * D_FF).astype(F32)
    W["wd"] = w_down.astype(F32)
    W["bd"] = b_down.reshape(N_EXPERTS, 1, D_MODEL).astype(F32)
    return W


def _rope_tables(S):
    half = ROPE_DIM // 2
    inv = ROPE_THETA ** (-jnp.arange(0, ROPE_DIM, 2, dtype=F32) / ROPE_DIM)
    ang = jnp.arange(S, dtype=F32)[:, None] * inv[None, :]
    cos, sin = jnp.cos(ang), jnp.sin(ang)
    c = (NOPE_DIM + ROPE_DIM) ** -0.5 * math.log2(math.e)
    pad = jnp.zeros((S, HEAD_PAD - NOPE_DIM - ROPE_DIM), F32)
    cos_blk = jnp.concatenate([cos, cos, pad], axis=1)
    sin_blk = jnp.concatenate([-sin, sin, pad], axis=1)
    cosq = jnp.concatenate([jnp.ones((S, NOPE_DIM), F32), cos_blk], axis=1) * c
    sinq = jnp.concatenate([jnp.zeros((S, NOPE_DIM), F32), sin_blk], axis=1) * c
    cosk = jnp.concatenate([jnp.zeros((S, NOPE_DIM), F32), cos_blk], axis=1)
    sink = jnp.concatenate([jnp.zeros((S, NOPE_DIM), F32), sin_blk], axis=1)
    return cosq, sinq, cosk, sink


def _dest_kernel(pstart_ref, ti_ref, rank_ref, dest_ref):
    ti = ti_ref[...]
    dest = rank_ref[...]
    for e in range(N_EXPERTS):
        dest = dest + jnp.where(ti == e, pstart_ref[e], 0)
    dest_ref[...] = dest


def _dest(p_start, top_i, rank, tile=8192):
    T = top_i.shape[1]
    tile = min(tile, T)
    spec = pl.BlockSpec((TOP_K, tile), lambda i, ps: (0, i))
    return pl.pallas_call(
        _dest_kernel,
        grid_spec=pltpu.PrefetchScalarGridSpec(num_scalar_prefetch=1, grid=(T // tile,),
                                               in_specs=[spec, spec], out_specs=spec),
        out_shape=jax.ShapeDtypeStruct((TOP_K, T), jnp.int32),
        compiler_params=_params(("parallel",)),
        name="slot_index",
    )(p_start, top_i, rank)


SC_CHUNK = 128


def _sc_mesh():
    info = plsc.get_sparse_core_info()
    mesh = plsc.VectorSubcoreMesh(core_axis_name="c", subcore_axis_name="s")
    return mesh, info.num_cores, info.num_cores * info.num_subcores


def _sc_dispatch(rows, dest3, P):
    T, D = rows.shape
    K = dest3.shape[0]
    mesh, n_cores, n_workers = _sc_mesh()
    n_chunks = T // (n_workers * SC_CHUNK)

    @functools.partial(
        pl.kernel, mesh=mesh, out_type=jax.ShapeDtypeStruct((P, D), rows.dtype),
        scratch_types=[pltpu.VMEM((K, SC_CHUNK), jnp.int32), pltpu.VMEM((SC_CHUNK, D), rows.dtype),
                       pltpu.SemaphoreType.DMA])
    def dispatch(rows_hbm, dest_hbm, out_hbm, idx_v, rows_v, sem):
        worker = lax.axis_index("s") * n_cores + lax.axis_index("c")

        @pl.loop(0, n_chunks)
        def _(j):
            g = worker * n_chunks + j
            base = pl.multiple_of(g * SC_CHUNK, SC_CHUNK)
            pltpu.sync_copy(rows_hbm.at[pl.ds(base, SC_CHUNK)], rows_v)
            pltpu.sync_copy(dest_hbm.at[:, g], idx_v)
            for k in range(K):
                pltpu.async_copy(rows_v, out_hbm.at[idx_v.at[k]], sem).wait()

    return dispatch(rows, dest3)


def _sc_gather(table, idx):
    M = idx.shape[0]
    D = table.shape[1]
    mesh, n_cores, n_workers = _sc_mesh()
    n_chunks = M // (n_workers * SC_CHUNK)

    @functools.partial(
        pl.kernel, mesh=mesh, out_type=jax.ShapeDtypeStruct((M, D), table.dtype),
        scratch_types=[pltpu.VMEM((SC_CHUNK,), jnp.int32), pltpu.VMEM((SC_CHUNK, D), table.dtype),
                       pltpu.SemaphoreType.DMA])
    def gather(table_hbm, idx_hbm, out_hbm, idx_v, rows_v, sem):
        worker = lax.axis_index("s") * n_cores + lax.axis_index("c")

        @pl.loop(0, n_chunks)
        def _(j):
            base = pl.multiple_of((worker * n_chunks + j) * SC_CHUNK, SC_CHUNK)
            pltpu.sync_copy(idx_hbm.at[pl.ds(base, SC_CHUNK)], idx_v)
            pltpu.async_copy(table_hbm.at[idx_v], rows_v, sem).wait()
            pltpu.sync_copy(rows_v, out_hbm.at[pl.ds(base, SC_CHUNK)])

    return gather(table, idx)


def _route(top_i, rank, counts, T, bm):
    A = TOP_K * T
    counts = counts.astype(jnp.int32)
    padded = ((counts + bm - 1) // bm) * bm
    p_end = jnp.cumsum(padded)
    p_start = p_end - padded
    dest = _dest(p_start, top_i, rank)
    nblk = A // bm + N_EXPERTS
    blk_start = jnp.arange(nblk, dtype=jnp.int32) * bm
    blk_e = jnp.minimum(jnp.sum(p_end[None, :] <= blk_start[:, None], axis=1), N_EXPERTS - 1).astype(jnp.int32)
    nvalid = (p_end[-1] // bm).astype(jnp.int32).reshape(1)
    return dest, nblk * bm, blk_e, nvalid


def _layer(x, p, W, bm):
    B, S, _ = x.shape
    T = B * S
    x2 = x.reshape(T, D_MODEL)
    p2 = p.reshape(T, PLE_DIM)
    Wl = dict(W)
    Wl["cosq"], Wl["sinq"], Wl["cosk"], Wl["sink"] = _rope_tables(S)

    q, k, v, dn, z, gates, gb = _in_proj(x2, S, Wl)
    attn = _attention(q, k, v, B, S)
    qn, vk, kt, aux = _dn_prep(dn, gb, S, Wl)
    auxt = jnp.swapaxes(aux, 1, 2)
    o_f = _deltanet(qn, vk, kt, aux, auxt, B, S, Wl, 0)
    o_b = _deltanet(qn, vk, kt, aux, auxt, B, S, Wl, 1)
    r, hb, top_i, top_g, rank, cnt = _mix(x2, attn, o_f, o_b, z, gates, p2, Wl)

    dest, P, blk_e, nvalid = _route(top_i, rank, cnt[:, 0], T, bm)
    xs = _sc_dispatch(hb, dest.reshape(TOP_K, T // SC_CHUNK, SC_CHUNK), P)
    yb = _moe(xs, blk_e, nvalid, Wl, bm)
    yg = _sc_gather(yb, dest.reshape(TOP_K * T)).reshape(TOP_K, T, D_MODEL // 2)
    y = _final(r, yg, top_g.T, Wl)
    return y.reshape(B, S, D_MODEL)


def kernel(x_prompt, x_sample, p_prompt, p_sample, w_in, q_a_norm, w_uq, kv_a_norm, w_ukv, w_o_attn, dn_conv, dn_a_log, dn_dt_bias, dn_norm, w_o_dn, w_out, ln1_g, ln1_b, router_w, router_b, w_gate_up, b_gate_up, w_down, b_down, ple_w_proj, ple_w_gate, ln2_g, ln2_b):
    y_prompt, y_sample = x_prompt, x_sample
    for l in range(DEPTH):
        W = _prep_weights(w_in[l], q_a_norm[l], w_uq[l], kv_a_norm[l], w_ukv[l], w_o_attn[l], dn_conv[l],
                          dn_a_log[l], dn_dt_bias[l], dn_norm[l], w_o_dn[l], w_out[l], ln1_g[l], ln1_b[l],
                          router_w[l], router_b[l], w_gate_up[l], b_gate_up[l], w_down[l], b_down[l],
                          ple_w_proj[l], ple_w_gate[l], ln2_g[l], ln2_b[l])
        y_prompt = _layer(y_prompt, p_prompt[l], W, bm=MOE_BLOCK)
        y_sample = _layer(y_sample, p_sample[l], W, bm=MOE_BLOCK)
    return (y_prompt, y_sample)
```

```python
import functools
import math

import numpy as np
import jax
import jax.numpy as jnp
from jax import lax
from jax.experimental import pallas as pl
from jax.experimental.pallas import tpu as pltpu
from jax.experimental.pallas import tpu_sc as plsc

D_MODEL = 1024
MLA_HEADS = 8
Q_LORA = 256
KV_LORA = 128
NOPE_DIM = 64
ROPE_DIM = 32
V_DIM = 64
ROPE_THETA = 10000.0
DN_HEADS = 8
DN_DK = 64
DN_DV = 64
CONV_K = 5
N_EXPERTS = 32
TOP_K = 4
D_FF = 1024
SWIGLU_LIMIT = 7.0
SWIGLU_ALPHA = 1.702
PLE_DIM = 256
DEPTH = 1
DEEPNORM_ALPHA = (2.0 * DEPTH) ** 0.25

LANES = 128
HEAD_PAD = 128
DN_BLOCK = 256
DN_PREP_TILE = 256
MOE_BLOCK = 512
MIX_TILE = 512
VMEM_LIMIT = 56 * 1024 * 1024

_C_CQ = 0
_C_CKV = _C_CQ + Q_LORA
_C_MISC0 = _C_CKV + KV_LORA
_C_MISC1 = _C_MISC0 + LANES
_C_DNQ = _C_MISC1 + LANES
_C_DNVK = _C_DNQ + DN_HEADS * DN_DK
_C_Z = _C_DNVK + DN_HEADS * (DN_DK + DN_DV)
_C_GATE = _C_Z + DN_HEADS * DN_DV
_C_END = _C_GATE + 2 * D_MODEL

BF16 = jnp.bfloat16
F32 = jnp.float32


def _dot(a, b):
    return jnp.dot(a, b, preferred_element_type=F32)


def _dot_nt(a, b):
    return lax.dot_general(a, b, (((1,), (1,)), ((), ())), preferred_element_type=F32)


def _split_bf16(x):
    hi = x.astype(BF16)
    lo = (x - hi.astype(F32)).astype(BF16)
    return hi, lo


_HI_HALFWORD = 0xFFFF0000


def _pack_halves(x):
    w = x.shape[1] // 2
    bits = lax.bitcast_convert_type(x.astype(BF16).astype(F32), jnp.uint32)
    return (bits[:, :w] >> 16) | (bits[:, w:] & jnp.uint32(_HI_HALFWORD))


def _unpack_halves(words):
    lo = lax.bitcast_convert_type(words << 16, F32)
    hi = lax.bitcast_convert_type(words & jnp.uint32(_HI_HALFWORD), F32)
    return lo, hi


def _const_spec(shape):
    n = len(shape)
    return pl.BlockSpec(shape, lambda *_: (0,) * n, pipeline_mode=pl.Buffered(1))


def _params(sem):
    return pltpu.CompilerParams(dimension_semantics=sem, vmem_limit_bytes=VMEM_LIMIT)


def _in_proj_kernel(x_ref, w1_ref, qan_ref, kvan_ref, wq_ref, wqs_ref, wk_ref, wv_ref,
                    cosq_ref, sinq_ref, cosk_ref, sink_ref, abp_ref,
                    q_ref, k_ref, v_ref, dn_ref, z_ref, gates_ref, gb_ref):
    xb = x_ref[...].astype(BF16)

    def proj(lo, hi):
        return _dot(xb, w1_ref[:, lo:hi])

    def rms(c, g):
        return (c * lax.rsqrt(jnp.mean(c * c, axis=-1, keepdims=True) + 1e-6) * g).astype(BF16)

    cqn = rms(proj(_C_CQ, _C_CKV), qan_ref[...])
    qa = _dot(cqn, wq_ref[...])
    qb = _dot(cqn, wqs_ref[...])
    ckvn = rms(proj(_C_CKV, _C_MISC0), kvan_ref[...])
    kw = _dot(ckvn, wk_ref[...])
    vw = _dot(ckvn, wv_ref[...])
    misc = (proj(_C_MISC0, _C_MISC1), proj(_C_MISC1, _C_DNQ))
    kr = misc[0] * cosk_ref[...] + misc[1] * sink_ref[...]
    cosq = cosq_ref[...]
    sinq = sinq_ref[...]
    lane = lax.broadcasted_iota(jnp.int32, (1, HEAD_PAD), 1)
    ones_col = (lane == V_DIM).astype(F32)
    for h in range(MLA_HEADS):
        sl = slice(h * HEAD_PAD, (h + 1) * HEAD_PAD)
        q_ref[:, sl] = (qa[:, sl] * cosq + qb[:, sl] * sinq).astype(BF16)
        k_ref[:, sl] = (kw[:, sl] + kr).astype(BF16)
        v_ref[:, sl] = (vw[:, sl] + ones_col).astype(BF16)

    for d in range(2):
        ab = misc[d]
        neg_a = abp_ref[2 * d:2 * d + 1, :]
        dtb = abp_ref[2 * d + 1:2 * d + 2, :]
        t = ab + dtb
        sp = jnp.maximum(t, 0.0) + jnp.log(1.0 + jnp.exp(-jnp.abs(t)))
        g = neg_a * sp
        beta = 1.0 / (1.0 + jnp.exp(-ab))
        gb_ref[d] = jnp.where(lane < DN_HEADS, g, beta)[:, :2 * DN_HEADS]

    dn_ref[...] = proj(_C_DNQ, _C_Z).astype(BF16)
    z_ref[...] = proj(_C_Z, _C_GATE).astype(BF16)
    gl = proj(_C_GATE, _C_END)
    gates_ref[...] = (1.0 / (1.0 + jnp.exp(-gl))).astype(BF16)


def _in_proj(x2, S, W, tm=512):
    T = x2.shape[0]
    nseq = S // tm
    row = lambda i: (i, 0)
    pos = lambda i: (i % nseq, 0)
    dn_w = _C_Z - _C_DNQ
    return pl.pallas_call(
        _in_proj_kernel,
        grid=(T // tm,),
        in_specs=[
            pl.BlockSpec((tm, D_MODEL), row),
            _const_spec((D_MODEL, _C_END)),
            _const_spec((1, Q_LORA)), _const_spec((1, KV_LORA)),
            _const_spec((Q_LORA, MLA_HEADS * HEAD_PAD)), _const_spec((Q_LORA, MLA_HEADS * HEAD_PAD)),
            _const_spec((KV_LORA, MLA_HEADS * HEAD_PAD)), _const_spec((KV_LORA, MLA_HEADS * HEAD_PAD)),
            pl.BlockSpec((tm, HEAD_PAD), pos), pl.BlockSpec((tm, HEAD_PAD), pos),
            pl.BlockSpec((tm, HEAD_PAD), pos), pl.BlockSpec((tm, HEAD_PAD), pos),
            _const_spec((8, LANES)),
        ],
        out_specs=[
            pl.BlockSpec((tm, MLA_HEADS * HEAD_PAD), row),
            pl.BlockSpec((tm, MLA_HEADS * HEAD_PAD), row),
            pl.BlockSpec((tm, MLA_HEADS * HEAD_PAD), row),
            pl.BlockSpec((tm, dn_w), row),
            pl.BlockSpec((tm, DN_HEADS * DN_DV), row),
            pl.BlockSpec((tm, 2 * D_MODEL), row),
            pl.BlockSpec((2, tm, 2 * DN_HEADS), lambda i: (0, i, 0)),
        ],
        out_shape=[
            jax.ShapeDtypeStruct((T, MLA_HEADS * HEAD_PAD), BF16),
            jax.ShapeDtypeStruct((T, MLA_HEADS * HEAD_PAD), BF16),
            jax.ShapeDtypeStruct((T, MLA_HEADS * HEAD_PAD), BF16),
            jax.ShapeDtypeStruct((T, dn_w), BF16),
            jax.ShapeDtypeStruct((T, DN_HEADS * DN_DV), BF16),
            jax.ShapeDtypeStruct((T, 2 * D_MODEL), BF16),
            jax.ShapeDtypeStruct((2, T, 2 * DN_HEADS), F32),
        ],
        compiler_params=_params(("parallel",)),
        name="in_proj",
    )(x2, W["w1"], W["qan"], W["kvan"], W["wq"], W["wqs"], W["wk"], W["wv"],
      W["cosq"], W["sinq"], W["cosk"], W["sink"], W["abp"])


def _attn_kernel(q_ref, k_ref, v_ref, o_ref, *, tk, unroll):
    tq = q_ref.shape[0]
    S = k_ref.shape[0]
    q = q_ref[...]

    def body(j, carry):
        m, acc = carry
        off = pl.multiple_of(j * tk, tk)
        s = _dot_nt(q, k_ref[pl.ds(off, tk), :])
        m_new = jnp.maximum(m, jnp.max(s, axis=-1, keepdims=True))
        p = jnp.exp2(s - m_new).astype(BF16)
        acc = acc * jnp.exp2(m - m_new) + _dot(p, v_ref[pl.ds(off, tk), :])
        return m_new, acc

    m0 = jnp.full((tq, 1), -1e30, F32)
    acc0 = jnp.zeros((tq, HEAD_PAD), F32)
    _, acc = lax.fori_loop(0, S // tk, body, (m0, acc0), unroll=unroll)
    o_ref[...] = (acc / acc[:, V_DIM:V_DIM + 1]).astype(BF16)


def _attention(q, k, v, B, S, tq=1024, tk=2048, unroll=4):
    T = q.shape[0]
    tq = min(tq, S)
    tk = min(tk, S)
    nq = S // tq
    return pl.pallas_call(
        functools.partial(_attn_kernel, tk=tk, unroll=unroll),
        grid=(B, MLA_HEADS, nq),
        in_specs=[
            pl.BlockSpec((tq, HEAD_PAD), lambda b, h, i: (b * nq + i, h)),
            pl.BlockSpec((S, HEAD_PAD), lambda b, h, i: (b, h)),
            pl.BlockSpec((S, HEAD_PAD), lambda b, h, i: (b, h)),
        ],
        out_specs=pl.BlockSpec((tq, HEAD_PAD), lambda b, h, i: (b * nq + i, h)),
        out_shape=jax.ShapeDtypeStruct((T, MLA_HEADS * HEAD_PAD), BF16),
        compiler_params=_params(("parallel", "parallel", "arbitrary")),
        name="attention",
    )(q, k, v)


_HALO = 16


def _dn_prep_kernel(x_ref, prev_ref, next_ref, cw_ref, eq_ref, eqt_ref, evk_ref, evkt_ref,
                    gb_ref, tri_ref, shift_ref, qn_ref, vk_ref, kt_ref, aux_ref, *, nseq):
    i = pl.program_id(0)
    tp = x_ref.shape[0]
    first = (i % nseq) == 0
    last = (i % nseq) == nseq - 1
    half = CONV_K // 2
    xb = x_ref[...]

    y = xb.astype(F32) * cw_ref[half:half + 1, :]
    for j in range(CONV_K):
        if j != half:
            y = y + _dot(shift_ref[j], xb) * cw_ref[j:j + 1, :]

    def edge(slab, row0):
        n = slab.shape[0]
        acc = None
        for j in range(CONV_K):
            shift = (half - j) % n
            rolled = slab if shift == 0 else pltpu.roll(slab, shift, axis=0)
            term = rolled[row0:row0 + 8, :] * cw_ref[j:j + 1, :]
            acc = term if acc is None else acc + term
        return acc

    prev = jnp.where(first, 0.0, prev_ref[...].astype(F32))
    nxt = jnp.where(last, 0.0, next_ref[...].astype(F32))
    top = edge(jnp.concatenate([prev, xb[:_HALO].astype(F32)], axis=0), _HALO)
    bot = edge(jnp.concatenate([xb[tp - _HALO:].astype(F32), nxt], axis=0), _HALO - 8)
    y = jnp.concatenate([top, y[8:tp - 8], bot], axis=0)
    y = y * (1.0 / (1.0 + jnp.exp(-y)))

    def group_scale(v, e_ref, et_ref):
        ss = _dot((v * v).astype(BF16), e_ref[...])
        return _dot(lax.rsqrt(ss + 1e-6).astype(BF16), et_ref[...])

    nq = DN_HEADS * DN_DK
    yq = y[:, :nq]
    qn_ref[...] = (yq * group_scale(yq, eq_ref, eqt_ref) * (DN_DK ** -0.5)).astype(BF16)
    yvk = y[:, nq:]
    sc = group_scale(yvk, evk_ref, evkt_ref)
    lane = lax.broadcasted_iota(jnp.int32, (1, yvk.shape[1]), 1)
    is_k = (lane // DN_DV) % 2 == 1
    vk = yvk * jnp.where(is_k, sc, 1.0)
    vk_ref[...] = vk.astype(BF16)
    vkt = vk.T
    for h in range(DN_HEADS):
        r0 = h * (DN_DV + DN_DK) + DN_DV
        kt_ref[h * DN_DK:(h + 1) * DN_DK, :] = vkt[r0:r0 + DN_DK, :].astype(BF16)

    hcol = lax.broadcasted_iota(jnp.int32, (1, 2 * DN_HEADS), 1) < DN_HEADS
    for d in range(2):
        gb = gb_ref[d]
        for blk in range(tp // DN_BLOCK):
            rs = slice(blk * DN_BLOCK, (blk + 1) * DN_BLOCK)
            g = gb[rs]
            g_hi = g.astype(BF16)
            g_mid, g_lo = _split_bf16(g - g_hi.astype(F32))
            cs = _dot(tri_ref[d], g_hi) + _dot(tri_ref[d], g_mid) + _dot(tri_ref[d], g_lo)
            aux_ref[d, rs, :] = jnp.where(hcol, cs, g)


def _dn_prep(dn, gb, S, W):
    tp = DN_PREP_TILE
    T = dn.shape[0]
    nseq = S // tp
    hb = tp // _HALO
    nh = T // _HALO
    dn_w = dn.shape[1]
    nq = DN_HEADS * DN_DK
    return pl.pallas_call(
        functools.partial(_dn_prep_kernel, nseq=nseq),
        grid=(T // tp,),
        in_specs=[
            pl.BlockSpec((tp, dn_w), lambda i: (i, 0)),
            pl.BlockSpec((_HALO, dn_w), lambda i: (jnp.maximum(i * hb - 1, 0), 0)),
            pl.BlockSpec((_HALO, dn_w), lambda i: (jnp.minimum((i + 1) * hb, nh - 1), 0)),
            _const_spec((8, dn_w)),
            _const_spec((nq, LANES)), _const_spec((LANES, nq)),
            _const_spec((dn_w - nq, LANES)), _const_spec((LANES, dn_w - nq)),
            pl.BlockSpec((2, tp, 2 * DN_HEADS), lambda i: (0, i, 0)),
            _const_spec((2, DN_BLOCK, DN_BLOCK)),
            _const_spec((CONV_K, tp, tp)),
        ],
        out_specs=[
            pl.BlockSpec((tp, nq), lambda i: (i, 0)),
            pl.BlockSpec((tp, dn_w - nq), lambda i: (i, 0)),
            pl.BlockSpec((nq, tp), lambda i: (0, i)),
            pl.BlockSpec((2, tp, 2 * DN_HEADS), lambda i: (0, i, 0)),
        ],
        out_shape=[
            jax.ShapeDtypeStruct((T, nq), BF16),
            jax.ShapeDtypeStruct((T, dn_w - nq), BF16),
            jax.ShapeDtypeStruct((nq, T), BF16),
            jax.ShapeDtypeStruct((2, T, 2 * DN_HEADS), F32),
        ],
        compiler_params=_params(("parallel",)),
        name="dn_prep",
    )(dn, dn, dn, W["convw"], W["eq"], W["eqt"], W["evk"], W["evkt"], gb, W["tri"], W["conv_shift"])


_LEAF = 4
_FULL_MERGE_SIZES = (4, 8)
_MERGE_SIZES = (16, 32, 64, 128)


def _active_blocks(d, s):
    return range(1 - d, DN_BLOCK // s, 2)


def _dn_masks(d):
    r = np.arange(DN_BLOCK)[:, None]
    c = np.arange(DN_BLOCK)[None, :]
    rr, cc = (r, c) if d == 0 else (c, r)
    tri = np.stack([rr >= cc, rr > cc]).astype(np.float32)
    def off(s):
        return ((rr // s) % 2 == 1) & ((rr // s) == (cc // s) + 1)

    small = np.stack([(r // _LEAF) == (c // _LEAF)] + [off(s) for s in _FULL_MERGE_SIZES]).astype(np.float32)
    offc = []
    for s in _MERGE_SIZES:
        rows = np.concatenate([np.arange(b * s, (b + 1) * s) for b in _active_blocks(d, s)])
        assert not np.delete(off(s), rows, axis=0).any()
        offc.append(off(s)[rows].astype(np.float32))
    return tri, small, np.stack(offc)


def _deltanet_kernel(qn_ref, vk_ref, kt_ref, aux_ref, auxt_ref, tri_ref, small_ref, offc_ref, o_ref,
                     s_ref, nm_ref, p_ref, x_ref, aqk_ref, uw_ref, vn_ref, *, d):
    i = pl.program_id(1)
    C = DN_BLOCK
    heads = range(DN_HEADS)

    @pl.when(i == 0)
    def _():
        s_ref[...] = jnp.zeros_like(s_ref)

    rowi = lax.broadcasted_iota(jnp.int32, (C, C), 0)
    coli = lax.broadcasted_iota(jnp.int32, (C, C), 1)
    eye = (rowi == coli).astype(F32)
    lane = lax.broadcasted_iota(jnp.int32, (1, LANES), 1)
    r64 = lax.broadcasted_iota(jnp.int32, (DN_DK, LANES), 0)
    c64 = lax.broadcasted_iota(jnp.int32, (DN_DK, LANES), 1)
    zeros_kt = jnp.zeros((DN_DK, C), BF16)
    zeros_s = jnp.zeros((DN_DK, LANES), F32)

    def q_pair(h):
        return qn_ref[:, (h // 2) * LANES:(h // 2 + 1) * LANES]

    def vk_head(h):
        return vk_ref[:, h * LANES:(h + 1) * LANES]

    def kt_head(h):
        return kt_ref[h * DN_DK:(h + 1) * DN_DK, :]

    def gc_col(h):
        return aux_ref[0, :, h:h + 1]

    def beta_col(h):
        return aux_ref[0, :, DN_HEADS + h:DN_HEADS + h + 1]

    def gc_row(h):
        return auxt_ref[0, h:h + 1, :]

    for h in heads:
        kt = kt_head(h)
        kt_for_q = jnp.concatenate([kt, zeros_kt] if h % 2 == 0 else [zeros_kt, kt], axis=0)
        kt_for_k = jnp.concatenate([zeros_kt, kt], axis=0)
        qk = _dot(q_pair(h), kt_for_q)
        kk = _dot(vk_head(h), kt_for_k)
        e0 = jnp.exp(jnp.minimum(gc_col(h) - gc_row(h), 0.0))
        aqk_ref[h] = (qk * e0 * tri_ref[0]).astype(BF16)
        nm_ref[h] = (-(kk * e0 * tri_ref[1]) * beta_col(h)).astype(BF16)

    leaf = small_ref[0]
    for h in heads:
        nd = nm_ref[h] * leaf
        p_ref[h] = (eye + nd.astype(F32)).astype(BF16)
        x_ref[h] = _dot(nd, nd).astype(BF16)
    for h in heads:
        pb = p_ref[h]
        p_ref[h] = (pb.astype(F32) + _dot(pb, x_ref[h])).astype(BF16)
    for k in range(len(_FULL_MERGE_SIZES)):
        off_mask = small_ref[1 + k]
        for h in heads:
            x_ref[h] = _dot(nm_ref[h] * off_mask, p_ref[h]).astype(BF16)
        for h in heads:
            pb = p_ref[h]
            p_ref[h] = (pb.astype(F32) + _dot(pb, x_ref[h])).astype(BF16)
    for k, s in enumerate(_MERGE_SIZES):
        blocks = list(_active_blocks(d, s))
        zeros_blk = jnp.zeros((s, C), BF16)

        def active_rows(ref, h):
            return jnp.concatenate([ref[h, b * s:(b + 1) * s, :] for b in blocks], axis=0)

        for h in heads:
            xc = _dot(active_rows(nm_ref, h) * offc_ref[k], p_ref[h]).astype(BF16)
            pieces = []
            for j in range(len(blocks)):
                piece = xc[j * s:(j + 1) * s, :]
                pieces += [zeros_blk, piece] if d == 0 else [piece, zeros_blk]
            x_ref[h] = jnp.concatenate(pieces, axis=0)
        for h in heads:
            pr = active_rows(p_ref, h)
            prn = (pr.astype(F32) + _dot(pr, x_ref[h])).astype(BF16)
            for j, b in enumerate(blocks):
                p_ref[h, b * s:(b + 1) * s, :] = prn[j * s:(j + 1) * s, :]

    for h in heads:
        egc = jnp.exp(gc_col(h))
        rhs = (vk_head(h).astype(F32) * beta_col(h) * jnp.where(lane < DN_DV, 1.0, egc)).astype(BF16)
        uw_ref[h] = _dot(p_ref[h], rhs).astype(BF16)
    for h in heads:
        eye_pl = (c64 == r64 + (h % 2) * DN_DV).astype(F32)
        s_aug = jnp.concatenate([eye_pl, -s_ref[h]], axis=0).astype(BF16)
        vn_ref[h] = _dot(uw_ref[h], s_aug).astype(BF16)
    o_pair = None
    for h in heads:
        par = h % 2
        gc_r = gc_row(h)
        g_tot = gc_r[:, C - 1:C] if d == 0 else gc_r[:, 0:1]
        s_pl = s_ref[h]
        v_new = vn_ref[h]
        s_sel = jnp.concatenate([s_pl, zeros_s] if par == 0 else [zeros_s, s_pl], axis=0).astype(BF16)
        qd = (q_pair(h).astype(F32) * jnp.exp(gc_col(h))).astype(BF16)
        o_pl = _dot(qd, s_sel) + _dot(aqk_ref[h], v_new)
        kd = (kt_head(h).astype(F32) * jnp.exp(g_tot - gc_r)).astype(BF16)
        s_ref[h] = s_pl * jnp.exp(g_tot) + _dot(kd, v_new)
        if par == 0:
            o_pair = o_pl
        else:
            o_ref[:, (h // 2) * LANES:(h // 2 + 1) * LANES] = o_pair + o_pl


def _deltanet(qn, vk, kt, aux, auxt, B, S, W, d):
    T = qn.shape[0]
    C = DN_BLOCK
    nb = S // C

    def blk(b, i):
        return b * nb + (i if d == 0 else nb - 1 - i)

    nq = DN_HEADS * DN_DK
    tri, small, offc = W["dn_masks"][d]
    return pl.pallas_call(
        functools.partial(_deltanet_kernel, d=d),
        grid=(B, nb),
        in_specs=[
            pl.BlockSpec((C, nq), lambda b, i: (blk(b, i), 0)),
            pl.BlockSpec((C, 2 * nq), lambda b, i: (blk(b, i), 0)),
            pl.BlockSpec((nq, C), lambda b, i: (0, blk(b, i))),
            pl.BlockSpec((1, C, 2 * DN_HEADS), lambda b, i: (d, blk(b, i), 0)),
            pl.BlockSpec((1, 2 * DN_HEADS, C), lambda b, i: (d, 0, blk(b, i))),
            _const_spec((2, C, C)), _const_spec((1 + len(_FULL_MERGE_SIZES), C, C)),
            _const_spec((len(_MERGE_SIZES), C // 2, C)),
        ],
        out_specs=pl.BlockSpec((C, DN_HEADS * DN_DV), lambda b, i: (blk(b, i), 0)),
        out_shape=jax.ShapeDtypeStruct((T, DN_HEADS * DN_DV), F32),
        scratch_shapes=[pltpu.VMEM((DN_HEADS, DN_DK, LANES), F32)]
        + [pltpu.VMEM((DN_HEADS, C, C), BF16)] * 4
        + [pltpu.VMEM((DN_HEADS, C, LANES), BF16)] * 2,
        compiler_params=_params(("parallel", "arbitrary")),
        name="deltanet_fwd" if d == 0 else "deltanet_bwd",
    )(qn, vk, kt, aux, auxt, tri, small, offc)


def _layer_norm(v, g, b):
    mu = jnp.mean(v, axis=-1, keepdims=True)
    c = v - mu
    var = jnp.mean(c * c, axis=-1, keepdims=True)
    return c * lax.rsqrt(var + 1e-5) * g + b


def _mix_kernel(x_ref, attn_ref, of_ref, ob_ref, z_ref, gates_ref, p_ref,
                woa_ref, wod_ref, wout_ref, e8_ref, e8t_ref, dnorm_ref, ln1g_ref, ln1b_ref,
                wpg_ref, wpp_ref, rwh_ref, rwl_ref, rb_ref, ustrict_ref,
                r_ref, hb_ref, ti_ref, tg_ref, rank_ref, cnt_ref, run_ref):
    @pl.when(pl.program_id(0) == 0)
    def _():
        run_ref[...] = jnp.zeros_like(run_ref)

    oa = _dot(attn_ref[...], woa_ref[...])
    o = of_ref[...] + ob_ref[...]
    hi, lo = _split_bf16(o * o)
    ms = (_dot(hi, e8_ref[...]) + _dot(lo, e8_ref[...])) * (1.0 / DN_DV)
    ih, il = _split_bf16(lax.rsqrt(ms + 1e-6))
    sc = _dot(ih, e8t_ref[...]) + _dot(il, e8t_ref[...])
    zf = z_ref[...].astype(F32)
    od_in = o * sc * dnorm_ref[...] * (zf * (1.0 / (1.0 + jnp.exp(-zf))))
    od = _dot(od_in.astype(BF16), wod_ref[...])
    mix = gates_ref[:, :D_MODEL].astype(F32) * oa + gates_ref[:, D_MODEL:].astype(F32) * od
    mo = _dot(mix.astype(BF16), wout_ref[...])
    h = _layer_norm(DEEPNORM_ALPHA * x_ref[...] + mo, ln1g_ref[...], ln1b_ref[...])
    hb = h.astype(BF16)
    hb_ref[...] = _pack_halves(h)
    pg = _dot(hb, wpg_ref[...])
    pp = _dot(p_ref[...].astype(BF16), wpp_ref[...])
    r_ref[...] = DEEPNORM_ALPHA * h + pp * (1.0 / (1.0 + jnp.exp(-pg)))

    hl = (h - hb.astype(F32)).astype(BF16)
    logits = (_dot_nt(rwh_ref[...], hb) + _dot_nt(rwh_ref[...], hl)
              + _dot_nt(rwl_ref[...], hb) + rb_ref[...])
    eid = lax.broadcasted_iota(jnp.int32, logits.shape, 0)
    vals = []
    run = run_ref[:, 0:1]
    for k in range(TOP_K):
        m = jnp.max(logits, axis=0, keepdims=True)
        idx = jnp.min(jnp.where(logits == m, eid, N_EXPERTS), axis=0, keepdims=True)
        ti_ref[k:k + 1, :] = idx
        vals.append(m)
        hit = eid == idx
        logits = jnp.where(hit, -jnp.inf, logits)
        onehot = hit.astype(F32)
        earlier = _dot(onehot.astype(BF16), ustrict_ref[...])
        rank = jnp.sum(onehot * (run + earlier), axis=0, keepdims=True)
        rank_ref[k:k + 1, :] = rank.astype(jnp.int32)
        run = run + jnp.sum(onehot, axis=1, keepdims=True)
    run_ref[...] = jnp.broadcast_to(run, run_ref.shape)
    cnt_ref[...] = jnp.broadcast_to(run, cnt_ref.shape)
    es = [jnp.exp(v - vals[0]) for v in vals]
    den = es[0] + es[1] + es[2] + es[3]
    for k in range(TOP_K):
        tg_ref[k:k + 1, :] = es[k] / den


def _mix(x2, attn, o_f, o_b, z, gates, p2, W):
    tm = MIX_TILE
    T = x2.shape[0]
    row = lambda i: (i, 0)
    nd = DN_HEADS * DN_DV
    return pl.pallas_call(
        _mix_kernel,
        grid=(T // tm,),
        in_specs=[
            pl.BlockSpec((tm, D_MODEL), row),
            pl.BlockSpec((tm, MLA_HEADS * HEAD_PAD), row),
            pl.BlockSpec((tm, nd), row),
            pl.BlockSpec((tm, nd), row),
            pl.BlockSpec((tm, nd), row),
            pl.BlockSpec((tm, 2 * D_MODEL), row),
            pl.BlockSpec((tm, PLE_DIM), row),
            _const_spec((MLA_HEADS * HEAD_PAD, D_MODEL)), _const_spec((nd, D_MODEL)),
            _const_spec((D_MODEL, D_MODEL)),
            _const_spec((nd, LANES)), _const_spec((LANES, nd)), _const_spec((1, nd)),
            _const_spec((1, D_MODEL)), _const_spec((1, D_MODEL)),
            _const_spec((D_MODEL, D_MODEL)), _const_spec((PLE_DIM, D_MODEL)),
            _const_spec((N_EXPERTS, D_MODEL)), _const_spec((N_EXPERTS, D_MODEL)),
            _const_spec((N_EXPERTS, 1)),
            _const_spec((tm, tm)),
        ],
        out_specs=[
            pl.BlockSpec((tm, D_MODEL), row),
            pl.BlockSpec((tm, D_MODEL // 2), row),
            pl.BlockSpec((TOP_K, tm), lambda i: (0, i)),
            pl.BlockSpec((TOP_K, tm), lambda i: (0, i)),
            pl.BlockSpec((TOP_K, tm), lambda i: (0, i)),
            _const_spec((N_EXPERTS, LANES)),
        ],
        out_shape=[
            jax.ShapeDtypeStruct((T, D_MODEL), F32),
            jax.ShapeDtypeStruct((T, D_MODEL // 2), jnp.uint32),
            jax.ShapeDtypeStruct((TOP_K, T), jnp.int32),
            jax.ShapeDtypeStruct((TOP_K, T), F32),
            jax.ShapeDtypeStruct((TOP_K, T), jnp.int32),
            jax.ShapeDtypeStruct((N_EXPERTS, LANES), F32),
        ],
        scratch_shapes=[pltpu.VMEM((N_EXPERTS, LANES), F32)],
        compiler_params=_params(("arbitrary",)),
        name="mix",
    )(x2, attn, o_f, o_b, z, gates, p2, W["woa"], W["wod"], W["wout"], W["e8"], W["e8t"], W["dnorm"],
      W["ln1g"], W["ln1b"], W["wpg"], W["wpp"], W["rwh"], W["rwl"], W["rb"], W["ustrict"])


_CAST_ROWS = 256


def _moe_kernel(blk_e_ref, nvalid_ref, xs_ref, wgu32_ref, bgu_ref, wd32_ref, bd_ref, y_ref,
                wgu_ref, wd_ref, *, fc):
    i = pl.program_id(0)
    valid = i < nvalid_ref[0]
    new_expert = (i == 0) | (blk_e_ref[i] != blk_e_ref[jnp.maximum(i - 1, 0)])

    @pl.when(valid & new_expert)
    def _():
        for r in range(0, D_MODEL, _CAST_ROWS):
            wgu_ref[r:r + _CAST_ROWS, :] = wgu32_ref[0, r:r + _CAST_ROWS, :].astype(BF16)
        for r in range(0, D_FF, _CAST_ROWS):
            wd_ref[r:r + _CAST_ROWS, :] = wd32_ref[0, r:r + _CAST_ROWS, :].astype(BF16)

    @pl.when(valid)
    def _():
        xs = jnp.concatenate(_unpack_halves(xs_ref[...]), axis=1).astype(BF16)
        acc = None
        for c in range(D_FF // fc):
            lo, hi = c * fc, (c + 1) * fc
            gate = _dot(xs, wgu_ref[:, lo:hi]) + bgu_ref[0, :, lo:hi]
            up = _dot(xs, wgu_ref[:, D_FF + lo:D_FF + hi]) + bgu_ref[0, :, D_FF + lo:D_FF + hi]
            gate = jnp.minimum(gate, SWIGLU_LIMIT)
            up = jnp.clip(up, -SWIGLU_LIMIT, SWIGLU_LIMIT)
            act = gate * (1.0 / (1.0 + jnp.exp(-SWIGLU_ALPHA * gate))) * (up + 1.0)
            part = _dot(act.astype(BF16), wd_ref[lo:hi, :])
            acc = part if acc is None else acc + part
        y_ref[...] = _pack_halves(acc + bd_ref[0])

    @pl.when(jnp.logical_not(valid))
    def _():
        y_ref[...] = jnp.zeros_like(y_ref)


def _moe(xs, blk_e, nvalid, W, bm, fc=512):
    P = xs.shape[0]
    grid_spec = pltpu.PrefetchScalarGridSpec(
        num_scalar_prefetch=2,
        grid=(P // bm,),
        in_specs=[
            pl.BlockSpec((bm, D_MODEL // 2), lambda i, be, nv: (i, 0)),
            pl.BlockSpec((1, D_MODEL, 2 * D_FF), lambda i, be, nv: (be[i], 0, 0)),
            pl.BlockSpec((1, 1, 2 * D_FF), lambda i, be, nv: (be[i], 0, 0)),
            pl.BlockSpec((1, D_FF, D_MODEL), lambda i, be, nv: (be[i], 0, 0)),
            pl.BlockSpec((1, 1, D_MODEL), lambda i, be, nv: (be[i], 0, 0)),
        ],
        out_specs=pl.BlockSpec((bm, D_MODEL // 2), lambda i, be, nv: (i, 0)),
        scratch_shapes=[pltpu.VMEM((D_MODEL, 2 * D_FF), BF16), pltpu.VMEM((D_FF, D_MODEL), BF16)],
    )
    return pl.pallas_call(
        functools.partial(_moe_kernel, fc=fc),
        grid_spec=grid_spec,
        out_shape=jax.ShapeDtypeStruct((P, D_MODEL // 2), jnp.uint32),
        compiler_params=_params(("arbitrary",)),
        name="moe",
    )(blk_e, nvalid, xs, W["wgu"], W["bgu"], W["wd"], W["bd"])


def _final_kernel(r_ref, yg_ref, tg_ref, g_ref, b_ref, y_ref):
    ffn_lo = ffn_hi = None
    for k in range(TOP_K):
        lo, hi = _unpack_halves(yg_ref[k])
        g = tg_ref[:, k:k + 1]
        ffn_lo = lo * g if ffn_lo is None else ffn_lo + lo * g
        ffn_hi = hi * g if ffn_hi is None else ffn_hi + hi * g
    acc = r_ref[...] + jnp.concatenate([ffn_lo, ffn_hi], axis=1)
    y_ref[...] = _layer_norm(acc, g_ref[...], b_ref[...])


def _final(r, yg, tg, W, tm=512):
    T = r.shape[0]
    tm = min(tm, T)
    row = lambda i: (i, 0)
    return pl.pallas_call(
        _final_kernel,
        grid=(T // tm,),
        in_specs=[pl.BlockSpec((tm, D_MODEL), row),
                  pl.BlockSpec((TOP_K, tm, D_MODEL // 2), lambda i: (0, i, 0)),
                  pl.BlockSpec((tm, TOP_K), row),
                  _const_spec((1, D_MODEL)), _const_spec((1, D_MODEL))],
        out_specs=pl.BlockSpec((tm, D_MODEL), row),
        out_shape=jax.ShapeDtypeStruct((T, D_MODEL), F32),
        compiler_params=_params(("parallel",)),
        name="final_ln",
    )(r, yg, tg, W["ln2g"], W["ln2b"])


def _pad_heads(w, n_heads, width, start, size, dst=0):
    K = w.shape[0]
    w3 = w.reshape(K, n_heads, width)[:, :, start:start + size]
    out = jnp.zeros((K, n_heads, HEAD_PAD), w.dtype)
    out = out.at[:, :, dst:dst + size].set(w3)
    return out.reshape(K, n_heads * HEAD_PAD)


def _prep_weights(w_in, q_a_norm, w_uq, kv_a_norm, w_ukv, w_o_attn, dn_conv, dn_a_log, dn_dt_bias,
                  dn_norm, w_o_dn, w_out, ln1_g, ln1_b, router_w, router_b, w_gate_up, b_gate_up,
                  w_down, b_down, ple_w_proj, ple_w_gate, ln2_g, ln2_b):
    W = {}
    half = ROPE_DIM // 2
    o = 0
    cq = w_in[:, o:o + Q_LORA]; o += Q_LORA
    ckv = w_in[:, o:o + KV_LORA]; o += KV_LORA
    kr = w_in[:, o:o + ROPE_DIM]; o += ROPE_DIM
    nqk = DN_HEADS * DN_DK
    dq = w_in[:, o:o + nqk]; o += nqk
    dk = w_in[:, o:o + nqk]; o += nqk
    dv = w_in[:, o:o + DN_HEADS * DN_DV]; o += DN_HEADS * DN_DV
    dz = w_in[:, o:o + DN_HEADS * DN_DV]; o += DN_HEADS * DN_DV
    da = w_in[:, o:o + 2 * DN_HEADS]; o += 2 * DN_HEADS
    db = w_in[:, o:o + 2 * DN_HEADS]; o += 2 * DN_HEADS
    gate = w_in[:, o:o + 2 * D_MODEL]

    def lane_block(parts):
        w = jnp.concatenate(parts, axis=1)
        return jnp.pad(w, ((0, 0), (0, LANES - w.shape[1])))

    def misc_block(d, rope_cols):
        ab = [da[:, d * DN_HEADS:(d + 1) * DN_HEADS], db[:, d * DN_HEADS:(d + 1) * DN_HEADS]]
        return lane_block(ab + [jnp.zeros((D_MODEL, NOPE_DIM - 2 * DN_HEADS), F32), rope_cols])

    misc0 = misc_block(0, kr)
    misc1 = misc_block(1, jnp.concatenate([kr[:, half:], kr[:, :half]], axis=1))

    def interleave_vk(v, k):
        lead = v.shape[:-1]
        v3 = v.reshape(lead + (DN_HEADS, DN_DV))
        k3 = k.reshape(lead + (DN_HEADS, DN_DK))
        return jnp.concatenate([v3, k3], axis=-1).reshape(lead + (DN_HEADS * (DN_DV + DN_DK),))

    W["w1"] = jnp.concatenate([cq, ckv, misc0, misc1, dq, interleave_vk(dv, dk), dz, gate],
                              axis=1).astype(BF16)
    W["qan"] = q_a_norm.reshape(1, Q_LORA)
    W["kvan"] = kv_a_norm.reshape(1, KV_LORA)

    qw = NOPE_DIM + ROPE_DIM
    wq_nope = _pad_heads(w_uq, MLA_HEADS, qw, 0, NOPE_DIM, 0)
    wq_r1 = _pad_heads(w_uq, MLA_HEADS, qw, NOPE_DIM, half, NOPE_DIM)
    wq_r2 = _pad_heads(w_uq, MLA_HEADS, qw, NOPE_DIM + half, half, NOPE_DIM + half)
    W["wq"] = (wq_nope + wq_r1 + wq_r2).astype(BF16)
    wq_s1 = _pad_heads(w_uq, MLA_HEADS, qw, NOPE_DIM + half, half, NOPE_DIM)
    wq_s2 = _pad_heads(w_uq, MLA_HEADS, qw, NOPE_DIM, half, NOPE_DIM + half)
    W["wqs"] = (wq_s1 + wq_s2).astype(BF16)
    kvw = NOPE_DIM + V_DIM
    W["wk"] = _pad_heads(w_ukv, MLA_HEADS, kvw, 0, NOPE_DIM, 0).astype(BF16)
    W["wv"] = _pad_heads(w_ukv, MLA_HEADS, kvw, NOPE_DIM, V_DIM, 0).astype(BF16)

    neg_a = -jnp.exp(dn_a_log.astype(F32))
    abp = jnp.zeros((8, LANES), F32)
    for d in range(2):
        abp = abp.at[2 * d, :DN_HEADS].set(neg_a[d])
        abp = abp.at[2 * d + 1, :DN_HEADS].set(dn_dt_bias[d].astype(F32))
    W["abp"] = abp

    cw = jnp.concatenate([dn_conv[:, :nqk], interleave_vk(dn_conv[:, 2 * nqk:], dn_conv[:, nqk:2 * nqk])], axis=1)
    W["convw"] = jnp.pad(cw.astype(F32), ((0, 8 - CONV_K), (0, 0)))

    def group_indicator(width, group):
        e = (np.arange(width)[:, None] // group == np.arange(LANES)[None, :]).astype(np.float32)
        return e

    eq = group_indicator(nqk, DN_DK)
    W["eq"] = jnp.asarray(eq, BF16)
    W["eqt"] = jnp.asarray(eq.T, BF16)
    evk = group_indicator(2 * nqk, DN_DK)
    W["evk"] = jnp.asarray(evk, BF16)
    W["evkt"] = jnp.asarray(evk.T, BF16)
    W["e8"] = W["eq"]
    W["e8t"] = W["eqt"]
    r = np.arange(DN_BLOCK)
    W["tri"] = jnp.asarray(np.stack([r[:, None] >= r[None, :], r[:, None] <= r[None, :]]).astype(np.float32), BF16)
    rp = np.arange(DN_PREP_TILE)
    W["conv_shift"] = jnp.asarray(
        np.stack([rp[None, :] == rp[:, None] + (j - CONV_K // 2) for j in range(CONV_K)]).astype(np.float32), BF16)
    W["dn_masks"] = [(jnp.asarray(t), jnp.asarray(g, BF16), jnp.asarray(o, BF16))
                     for t, g, o in (_dn_masks(0), _dn_masks(1))]
    rt = np.arange(MIX_TILE)
    W["ustrict"] = jnp.asarray((rt[:, None] < rt[None, :]).astype(np.float32), BF16)

    woa = w_o_attn.reshape(MLA_HEADS, V_DIM, D_MODEL)
    woa = jnp.pad(woa, ((0, 0), (0, HEAD_PAD - V_DIM), (0, 0)))
    W["woa"] = woa.reshape(MLA_HEADS * HEAD_PAD, D_MODEL).astype(BF16)
    W["wod"] = w_o_dn.astype(BF16)
    W["wout"] = w_out.astype(BF16)
    W["dnorm"] = jnp.tile(dn_norm.astype(F32), DN_HEADS).reshape(1, DN_HEADS * DN_DV)
    W["ln1g"] = ln1_g.reshape(1, D_MODEL)
    W["ln1b"] = ln1_b.reshape(1, D_MODEL)
    W["ln2g"] = ln2_g.reshape(1, D_MODEL)
    W["ln2b"] = ln2_b.reshape(1, D_MODEL)
    W["wpg"] = ple_w_gate.astype(BF16)
    W["wpp"] = ple_w_proj.astype(BF16)
    rwt = router_w.T.astype(F32)
    W["rwh"], W["rwl"] = _split_bf16(rwt)
    W["rb"] = router_b.reshape(N_EXPERTS, 1).astype(F32)
    W["wgu"] = w_gate_up.astype(F32)
    W["bgu"] = b_gate_up.reshape(N_EXPERTS, 1, 2 * D_FF).astype(F32)
    W["wd"] = w_down.astype(F32)
    W["bd"] = b_down.reshape(N_EXPERTS, 1, D_MODEL).astype(F32)
    return W


def _rope_tables(S):
    half = ROPE_DIM // 2
    inv = ROPE_THETA ** (-jnp.arange(0, ROPE_DIM, 2, dtype=F32) / ROPE_DIM)
    ang = jnp.arange(S, dtype=F32)[:, None] * inv[None, :]
    cos, sin = jnp.cos(ang), jnp.sin(ang)
    c = (NOPE_DIM + ROPE_DIM) ** -0.5 * math.log2(math.e)
    pad = jnp.zeros((S, HEAD_PAD - NOPE_DIM - ROPE_DIM), F32)
    cos_blk = jnp.concatenate([cos, cos, pad], axis=1)
    sin_blk = jnp.concatenate([-sin, sin, pad], axis=1)
    cosq = jnp.concatenate([jnp.ones((S, NOPE_DIM), F32), cos_blk], axis=1) * c
    sinq = jnp.concatenate([jnp.zeros((S, NOPE_DIM), F32), sin_blk], axis=1) * c
    cosk = jnp.concatenate([jnp.zeros((S, NOPE_DIM), F32), cos_blk], axis=1)
    sink = jnp.concatenate([jnp.zeros((S, NOPE_DIM), F32), sin_blk], axis=1)
    return cosq, sinq, cosk, sink


def _dest_kernel(pstart_ref, ti_ref, rank_ref, dest_ref):
    ti = ti_ref[...]
    dest = rank_ref[...]
    for e in range(N_EXPERTS):
        dest = dest + jnp.where(ti == e, pstart_ref[e], 0)
    dest_ref[...] = dest


def _dest(p_start, top_i, rank, tile=8192):
    T = top_i.shape[1]
    tile = min(tile, T)
    spec = pl.BlockSpec((TOP_K, tile), lambda i, ps: (0, i))
    return pl.pallas_call(
        _dest_kernel,
        grid_spec=pltpu.PrefetchScalarGridSpec(num_scalar_prefetch=1, grid=(T // tile,),
                                               in_specs=[spec, spec], out_specs=spec),
        out_shape=jax.ShapeDtypeStruct((TOP_K, T), jnp.int32),
        compiler_params=_params(("parallel",)),
        name="slot_index",
    )(p_start, top_i, rank)


SC_CHUNK = 128


def _sc_mesh():
    info = plsc.get_sparse_core_info()
    mesh = plsc.VectorSubcoreMesh(core_axis_name="c", subcore_axis_name="s")
    return mesh, info.num_cores, info.num_cores * info.num_subcores


def _sc_dispatch(rows, dest3, P):
    T, D = rows.shape
    K = dest3.shape[0]
    mesh, n_cores, n_workers = _sc_mesh()
    n_chunks = T // (n_workers * SC_CHUNK)

    @functools.partial(
        pl.kernel, mesh=mesh, out_type=jax.ShapeDtypeStruct((P, D), rows.dtype),
        scratch_types=[pltpu.VMEM((K, SC_CHUNK), jnp.int32), pltpu.VMEM((SC_CHUNK, D), rows.dtype),
                       pltpu.SemaphoreType.DMA])
    def dispatch(rows_hbm, dest_hbm, out_hbm, idx_v, rows_v, sem):
        worker = lax.axis_index("s") * n_cores + lax.axis_index("c")

        @pl.loop(0, n_chunks)
        def _(j):
            g = worker * n_chunks + j
            base = pl.multiple_of(g * SC_CHUNK, SC_CHUNK)
            pltpu.sync_copy(rows_hbm.at[pl.ds(base, SC_CHUNK)], rows_v)
            pltpu.sync_copy(dest_hbm.at[:, g], idx_v)
            for k in range(K):
                pltpu.async_copy(rows_v, out_hbm.at[idx_v.at[k]], sem).wait()

    return dispatch(rows, dest3)


def _sc_gather(table, idx):
    M = idx.shape[0]
    D = table.shape[1]
    mesh, n_cores, n_workers = _sc_mesh()
    n_chunks = M // (n_workers * SC_CHUNK)

    @functools.partial(
        pl.kernel, mesh=mesh, out_type=jax.ShapeDtypeStruct((M, D), table.dtype),
        scratch_types=[pltpu.VMEM((SC_CHUNK,), jnp.int32), pltpu.VMEM((SC_CHUNK, D), table.dtype),
                       pltpu.SemaphoreType.DMA])
    def gather(table_hbm, idx_hbm, out_hbm, idx_v, rows_v, sem):
        worker = lax.axis_index("s") * n_cores + lax.axis_index("c")

        @pl.loop(0, n_chunks)
        def _(j):
            base = pl.multiple_of((worker * n_chunks + j) * SC_CHUNK, SC_CHUNK)
            pltpu.sync_copy(idx_hbm.at[pl.ds(base, SC_CHUNK)], idx_v)
            pltpu.async_copy(table_hbm.at[idx_v], rows_v, sem).wait()
            pltpu.sync_copy(rows_v, out_hbm.at[pl.ds(base, SC_CHUNK)])

    return gather(table, idx)


def _route(top_i, rank, counts, T, bm):
    A = TOP_K * T
    counts = counts.astype(jnp.int32)
    padded = ((counts + bm - 1) // bm) * bm
    p_end = jnp.cumsum(padded)
    p_start = p_end - padded
    dest = _dest(p_start, top_i, rank)
    nblk = A // bm + N_EXPERTS
    blk_start = jnp.arange(nblk, dtype=jnp.int32) * bm
    blk_e = jnp.minimum(jnp.sum(p_end[None, :] <= blk_start[:, None], axis=1), N_EXPERTS - 1).astype(jnp.int32)
    nvalid = (p_end[-1] // bm).astype(jnp.int32).reshape(1)
    return dest, nblk * bm, blk_e, nvalid


def _layer(x, p, W, bm):
    B, S, _ = x.shape
    T = B * S
    x2 = x.reshape(T, D_MODEL)
    p2 = p.reshape(T, PLE_DIM)
    Wl = dict(W)
    Wl["cosq"], Wl["sinq"], Wl["cosk"], Wl["sink"] = _rope_tables(S)

    q, k, v, dn, z, gates, gb = _in_proj(x2, S, Wl)
    attn = _attention(q, k, v, B, S)
    qn, vk, kt, aux = _dn_prep(dn, gb, S, Wl)
    auxt = jnp.swapaxes(aux, 1, 2)
    o_f = _deltanet(qn, vk, kt, aux, auxt, B, S, Wl, 0)
    o_b = _deltanet(qn, vk, kt, aux, auxt, B, S, Wl, 1)
    r, hb, top_i, top_g, rank, cnt = _mix(x2, attn, o_f, o_b, z, gates, p2, Wl)

    dest, P, blk_e, nvalid = _route(top_i, rank, cnt[:, 0], T, bm)
    xs = _sc_dispatch(hb, dest.reshape(TOP_K, T // SC_CHUNK, SC_CHUNK), P)
    yb = _moe(xs, blk_e, nvalid, Wl, bm)
    yg = _sc_gather(yb, dest.reshape(TOP_K * T)).reshape(TOP_K, T, D_MODEL // 2)
    y = _final(r, yg, top_g.T, Wl)
    return y.reshape(B, S, D_MODEL)


def kernel(x_prompt, x_sample, p_prompt, p_sample, w_in, q_a_norm, w_uq, kv_a_norm, w_ukv, w_o_attn, dn_conv, dn_a_log, dn_dt_bias, dn_norm, w_o_dn, w_out, ln1_g, ln1_b, router_w, router_b, w_gate_up, b_gate_up, w_down, b_down, ple_w_proj, ple_w_gate, ln2_g, ln2_b):
    y_prompt, y_sample = x_prompt, x_sample
    for l in range(DEPTH):
        W = _prep_weights(w_in[l], q_a_norm[l], w_uq[l], kv_a_norm[l], w_ukv[l], w_o_attn[l], dn_conv[l],
                          dn_a_log[l], dn_dt_bias[l], dn_norm[l], w_o_dn[l], w_out[l], ln1_g[l], ln1_b[l],
                          router_w[l], router_b[l], w_gate_up[l], b_gate_up[l], w_down[l], b_down[l],
                          ple_w_proj[l], ple_w_gate[l], ln2_g[l], ln2_b[l])
        y_prompt = _layer(y_prompt, p_prompt[l], W, bm=MOE_BLOCK)
        y_sample = _layer(y_sample, p_sample[l], W, bm=MOE_BLOCK)
    return (y_prompt, y_sample)
```

```python
import functools
import math

import numpy as np
import jax
import jax.numpy as jnp
from jax import lax
from jax.experimental import pallas as pl
from jax.experimental.pallas import tpu as pltpu
from jax.experimental.pallas import tpu_sc as plsc

D_MODEL = 1024
MLA_HEADS = 8
Q_LORA = 256
KV_LORA = 128
NOPE_DIM = 64
ROPE_DIM = 32
V_DIM = 64
ROPE_THETA = 10000.0
DN_HEADS = 8
DN_DK = 64
DN_DV = 64
CONV_K = 5
N_EXPERTS = 32
TOP_K = 4
D_FF = 1024
SWIGLU_LIMIT = 7.0
SWIGLU_ALPHA = 1.702
PLE_DIM = 256
DEPTH = 1
DEEPNORM_ALPHA = (2.0 * DEPTH) ** 0.25

LANES = 128
HEAD_PAD = 128
DN_BLOCK = 256
DN_PREP_TILE = 256
MOE_BLOCK = 512
MIX_TILE = 512
VMEM_LIMIT = 56 * 1024 * 1024

_C_CQ = 0
_C_CKV = _C_CQ + Q_LORA
_C_MISC0 = _C_CKV + KV_LORA
_C_MISC1 = _C_MISC0 + LANES
_C_DNQ = _C_MISC1 + LANES
_C_DNVK = _C_DNQ + DN_HEADS * DN_DK
_C_Z = _C_DNVK + DN_HEADS * (DN_DK + DN_DV)
_C_GATE = _C_Z + DN_HEADS * DN_DV
_C_END = _C_GATE + 2 * D_MODEL

BF16 = jnp.bfloat16
F32 = jnp.float32


def _dot(a, b):
    return jnp.dot(a, b, preferred_element_type=F32)


def _dot_nt(a, b):
    return lax.dot_general(a, b, (((1,), (1,)), ((), ())), preferred_element_type=F32)


def _split_bf16(x):
    hi = x.astype(BF16)
    lo = (x - hi.astype(F32)).astype(BF16)
    return hi, lo


_HI_HALFWORD = 0xFFFF0000


def _pack_halves(x):
    w = x.shape[1] // 2
    bits = lax.bitcast_convert_type(x.astype(BF16).astype(F32), jnp.uint32)
    return (bits[:, :w] >> 16) | (bits[:, w:] & jnp.uint32(_HI_HALFWORD))


def _unpack_halves(words):
    lo = lax.bitcast_convert_type(words << 16, F32)
    hi = lax.bitcast_convert_type(words & jnp.uint32(_HI_HALFWORD), F32)
    return lo, hi


def _const_spec(shape):
    n = len(shape)
    return pl.BlockSpec(shape, lambda *_: (0,) * n, pipeline_mode=pl.Buffered(1))


def _params(sem):
    return pltpu.CompilerParams(dimension_semantics=sem, vmem_limit_bytes=VMEM_LIMIT)


def _in_proj_kernel(x_ref, w1_ref, qan_ref, kvan_ref, wq_ref, wqs_ref, wk_ref, wv_ref,
                    cosq_ref, sinq_ref, cosk_ref, sink_ref, abp_ref,
                    q_ref, k_ref, v_ref, dn_ref, z_ref, gates_ref, gb_ref):
    xb = x_ref[...].astype(BF16)

    def proj(lo, hi):
        return _dot(xb, w1_ref[:, lo:hi])

    def rms(c, g):
        return (c * lax.rsqrt(jnp.mean(c * c, axis=-1, keepdims=True) + 1e-6) * g).astype(BF16)

    cqn = rms(proj(_C_CQ, _C_CKV), qan_ref[...])
    qa = _dot(cqn, wq_ref[...])
    qb = _dot(cqn, wqs_ref[...])
    ckvn = rms(proj(_C_CKV, _C_MISC0), kvan_ref[...])
    kw = _dot(ckvn, wk_ref[...])
    vw = _dot(ckvn, wv_ref[...])
    misc = (proj(_C_MISC0, _C_MISC1), proj(_C_MISC1, _C_DNQ))
    kr = misc[0] * cosk_ref[...] + misc[1] * sink_ref[...]
    cosq = cosq_ref[...]
    sinq = sinq_ref[...]
    lane = lax.broadcasted_iota(jnp.int32, (1, HEAD_PAD), 1)
    ones_col = (lane == V_DIM).astype(F32)
    for h in range(MLA_HEADS):
        sl = slice(h * HEAD_PAD, (h + 1) * HEAD_PAD)
        q_ref[:, sl] = (qa[:, sl] * cosq + qb[:, sl] * sinq).astype(BF16)
        k_ref[:, sl] = (kw[:, sl] + kr).astype(BF16)
        v_ref[:, sl] = (vw[:, sl] + ones_col).astype(BF16)

    for d in range(2):
        ab = misc[d]
        neg_a = abp_ref[2 * d:2 * d + 1, :]
        dtb = abp_ref[2 * d + 1:2 * d + 2, :]
        t = ab + dtb
        sp = jnp.maximum(t, 0.0) + jnp.log(1.0 + jnp.exp(-jnp.abs(t)))
        g = neg_a * sp
        beta = 1.0 / (1.0 + jnp.exp(-ab))
        gb_ref[d] = jnp.where(lane < DN_HEADS, g, beta)[:, :2 * DN_HEADS]

    dn_ref[...] = proj(_C_DNQ, _C_Z).astype(BF16)
    z_ref[...] = proj(_C_Z, _C_GATE).astype(BF16)
    gl = proj(_C_GATE, _C_END)
    gates_ref[...] = (1.0 / (1.0 + jnp.exp(-gl))).astype(BF16)


def _in_proj(x2, S, W, tm=512):
    T = x2.shape[0]
    nseq = S // tm
    row = lambda i: (i, 0)
    pos = lambda i: (i % nseq, 0)
    dn_w = _C_Z - _C_DNQ
    return pl.pallas_call(
        _in_proj_kernel,
        grid=(T // tm,),
        in_specs=[
            pl.BlockSpec((tm, D_MODEL), row),
            _const_spec((D_MODEL, _C_END)),
            _const_spec((1, Q_LORA)), _const_spec((1, KV_LORA)),
            _const_spec((Q_LORA, MLA_HEADS * HEAD_PAD)), _const_spec((Q_LORA, MLA_HEADS * HEAD_PAD)),
            _const_spec((KV_LORA, MLA_HEADS * HEAD_PAD)), _const_spec((KV_LORA, MLA_HEADS * HEAD_PAD)),
            pl.BlockSpec((tm, HEAD_PAD), pos), pl.BlockSpec((tm, HEAD_PAD), pos),
            pl.BlockSpec((tm, HEAD_PAD), pos), pl.BlockSpec((tm, HEAD_PAD), pos),
            _const_spec((8, LANES)),
        ],
        out_specs=[
            pl.BlockSpec((tm, MLA_HEADS * HEAD_PAD), row),
            pl.BlockSpec((tm, MLA_HEADS * HEAD_PAD), row),
            pl.BlockSpec((tm, MLA_HEADS * HEAD_PAD), row),
            pl.BlockSpec((tm, dn_w), row),
            pl.BlockSpec((tm, DN_HEADS * DN_DV), row),
            pl.BlockSpec((tm, 2 * D_MODEL), row),
            pl.BlockSpec((2, tm, 2 * DN_HEADS), lambda i: (0, i, 0)),
        ],
        out_shape=[
            jax.ShapeDtypeStruct((T, MLA_HEADS * HEAD_PAD), BF16),
            jax.ShapeDtypeStruct((T, MLA_HEADS * HEAD_PAD), BF16),
            jax.ShapeDtypeStruct((T, MLA_HEADS * HEAD_PAD), BF16),
            jax.ShapeDtypeStruct((T, dn_w), BF16),
            jax.ShapeDtypeStruct((T, DN_HEADS * DN_DV), BF16),
            jax.ShapeDtypeStruct((T, 2 * D_MODEL), BF16),
            jax.ShapeDtypeStruct((2, T, 2 * DN_HEADS), F32),
        ],
        compiler_params=_params(("parallel",)),
        name="in_proj",
    )(x2, W["w1"], W["qan"], W["kvan"], W["wq"], W["wqs"], W["wk"], W["wv"],
      W["cosq"], W["sinq"], W["cosk"], W["sink"], W["abp"])


def _attn_kernel(q_ref, k_ref, v_ref, o_ref, *, tk, unroll):
    tq = q_ref.shape[0]
    S = k_ref.shape[0]
    outs = []
    for hh in range(2):
        sl = slice(hh * HEAD_PAD, (hh + 1) * HEAD_PAD)
        q = q_ref[:, sl]

        def body(j, carry, q=q, sl=sl):
            m, acc = carry
            off = pl.multiple_of(j * tk, tk)
            s = _dot_nt(q, k_ref[pl.ds(off, tk), sl])
            m_new = jnp.maximum(m, jnp.max(s, axis=-1, keepdims=True))
            p = jnp.exp2(s - m_new).astype(BF16)
            acc = acc * jnp.exp2(m - m_new) + _dot(p, v_ref[pl.ds(off, tk), sl])
            return m_new, acc

        m0 = jnp.full((tq, 1), -1e30, F32)
        acc0 = jnp.zeros((tq, HEAD_PAD), F32)
        _, acc = lax.fori_loop(0, S // tk, body, (m0, acc0), unroll=unroll)
        outs.append(acc / acc[:, V_DIM:V_DIM + 1])
    lane = lax.broadcasted_iota(jnp.int32, (1, HEAD_PAD), 1)
    o_ref[...] = jnp.where(lane < V_DIM, outs[0], pltpu.roll(outs[1], V_DIM, axis=1)).astype(BF16)


def _attention(q, k, v, B, S, tq=1024, tk=2048, unroll=4):
    T = q.shape[0]
    tq = min(tq, S)
    tk = min(tk, S)
    nq = S // tq
    return pl.pallas_call(
        functools.partial(_attn_kernel, tk=tk, unroll=unroll),
        grid=(B, MLA_HEADS // 2, nq),
        in_specs=[
            pl.BlockSpec((tq, 2 * HEAD_PAD), lambda b, h, i: (b * nq + i, h)),
            pl.BlockSpec((S, 2 * HEAD_PAD), lambda b, h, i: (b, h)),
            pl.BlockSpec((S, 2 * HEAD_PAD), lambda b, h, i: (b, h)),
        ],
        out_specs=pl.BlockSpec((tq, 2 * V_DIM), lambda b, h, i: (b * nq + i, h)),
        out_shape=jax.ShapeDtypeStruct((T, MLA_HEADS * V_DIM), BF16),
        compiler_params=_params(("parallel", "parallel", "arbitrary")),
        name="attention",
    )(q, k, v)


_HALO = 16


def _dn_prep_kernel(x_ref, prev_ref, next_ref, cw_ref, eq_ref, eqt_ref, evk_ref, evkt_ref,
                    gb_ref, tri_ref, shift_ref, qn_ref, vk_ref, kt_ref, aux_ref, *, nseq):
    i = pl.program_id(0)
    tp = x_ref.shape[0]
    first = (i % nseq) == 0
    last = (i % nseq) == nseq - 1
    half = CONV_K // 2
    xb = x_ref[...]

    y = xb.astype(F32) * cw_ref[half:half + 1, :]
    for j in range(CONV_K):
        if j != half:
            y = y + _dot(shift_ref[j], xb) * cw_ref[j:j + 1, :]

    def edge(slab, row0):
        n = slab.shape[0]
        acc = None
        for j in range(CONV_K):
            shift = (half - j) % n
            rolled = slab if shift == 0 else pltpu.roll(slab, shift, axis=0)
            term = rolled[row0:row0 + 8, :] * cw_ref[j:j + 1, :]
            acc = term if acc is None else acc + term
        return acc

    prev = jnp.where(first, 0.0, prev_ref[...].astype(F32))
    nxt = jnp.where(last, 0.0, next_ref[...].astype(F32))
    top = edge(jnp.concatenate([prev, xb[:_HALO].astype(F32)], axis=0), _HALO)
    bot = edge(jnp.concatenate([xb[tp - _HALO:].astype(F32), nxt], axis=0), _HALO - 8)
    y = jnp.concatenate([top, y[8:tp - 8], bot], axis=0)
    y = y * (1.0 / (1.0 + jnp.exp(-y)))

    def group_scale(v, e_ref, et_ref):
        ss = _dot((v * v).astype(BF16), e_ref[...])
        return _dot(lax.rsqrt(ss + 1e-6).astype(BF16), et_ref[...])

    nq = DN_HEADS * DN_DK
    yq = y[:, :nq]
    qn_ref[...] = (yq * group_scale(yq, eq_ref, eqt_ref) * (DN_DK ** -0.5)).astype(BF16)
    yvk = y[:, nq:]
    sc = group_scale(yvk, evk_ref, evkt_ref)
    lane = lax.broadcasted_iota(jnp.int32, (1, yvk.shape[1]), 1)
    is_k = (lane // DN_DV) % 2 == 1
    vk = yvk * jnp.where(is_k, sc, 1.0)
    vk_ref[...] = vk.astype(BF16)
    vkt = vk.T
    for h in range(DN_HEADS):
        r0 = h * (DN_DV + DN_DK) + DN_DV
        kt_ref[h * DN_DK:(h + 1) * DN_DK, :] = vkt[r0:r0 + DN_DK, :].astype(BF16)

    hcol = lax.broadcasted_iota(jnp.int32, (1, 2 * DN_HEADS), 1) < DN_HEADS
    for d in range(2):
        gb = gb_ref[d]
        for blk in range(tp // DN_BLOCK):
            rs = slice(blk * DN_BLOCK, (blk + 1) * DN_BLOCK)
            g = gb[rs]
            g_hi = g.astype(BF16)
            g_mid, g_lo = _split_bf16(g - g_hi.astype(F32))
            cs = _dot(tri_ref[d], g_hi) + _dot(tri_ref[d], g_mid) + _dot(tri_ref[d], g_lo)
            aux_ref[d, rs, :] = jnp.where(hcol, cs, g)


def _dn_prep(dn, gb, S, W):
    tp = DN_PREP_TILE
    T = dn.shape[0]
    nseq = S // tp
    hb = tp // _HALO
    nh = T // _HALO
    dn_w = dn.shape[1]
    nq = DN_HEADS * DN_DK
    return pl.pallas_call(
        functools.partial(_dn_prep_kernel, nseq=nseq),
        grid=(T // tp,),
        in_specs=[
            pl.BlockSpec((tp, dn_w), lambda i: (i, 0)),
            pl.BlockSpec((_HALO, dn_w), lambda i: (jnp.maximum(i * hb - 1, 0), 0)),
            pl.BlockSpec((_HALO, dn_w), lambda i: (jnp.minimum((i + 1) * hb, nh - 1), 0)),
            _const_spec((8, dn_w)),
            _const_spec((nq, LANES)), _const_spec((LANES, nq)),
            _const_spec((dn_w - nq, LANES)), _const_spec((LANES, dn_w - nq)),
            pl.BlockSpec((2, tp, 2 * DN_HEADS), lambda i: (0, i, 0)),
            _const_spec((2, DN_BLOCK, DN_BLOCK)),
            _const_spec((CONV_K, tp, tp)),
        ],
        out_specs=[
            pl.BlockSpec((tp, nq), lambda i: (i, 0)),
            pl.BlockSpec((tp, dn_w - nq), lambda i: (i, 0)),
            pl.BlockSpec((nq, tp), lambda i: (0, i)),
            pl.BlockSpec((2, tp, 2 * DN_HEADS), lambda i: (0, i, 0)),
        ],
        out_shape=[
            jax.ShapeDtypeStruct((T, nq), BF16),
            jax.ShapeDtypeStruct((T, dn_w - nq), BF16),
            jax.ShapeDtypeStruct((nq, T), BF16),
            jax.ShapeDtypeStruct((2, T, 2 * DN_HEADS), F32),
        ],
        compiler_params=_params(("parallel",)),
        name="dn_prep",
    )(dn, dn, dn, W["convw"], W["eq"], W["eqt"], W["evk"], W["evkt"], gb, W["tri"], W["conv_shift"])


_LEAF = 4
_FULL_MERGE_SIZES = (4, 8)
_MERGE_SIZES = (16, 32, 64, 128)


def _active_blocks(d, s):
    return range(1 - d, DN_BLOCK // s, 2)


def _dn_masks(d):
    r = np.arange(DN_BLOCK)[:, None]
    c = np.arange(DN_BLOCK)[None, :]
    rr, cc = (r, c) if d == 0 else (c, r)
    tri = np.stack([rr >= cc, rr > cc]).astype(np.float32)
    def off(s):
        return ((rr // s) % 2 == 1) & ((rr // s) == (cc // s) + 1)

    small = np.stack([(r // _LEAF) == (c // _LEAF)] + [off(s) for s in _FULL_MERGE_SIZES]).astype(np.float32)
    offc = []
    for s in _MERGE_SIZES:
        rows = np.concatenate([np.arange(b * s, (b + 1) * s) for b in _active_blocks(d, s)])
        assert not np.delete(off(s), rows, axis=0).any()
        offc.append(off(s)[rows].astype(np.float32))
    return tri, small, np.stack(offc)


def _deltanet_kernel(qn_ref, vk_ref, kt_ref, aux_ref, auxt_ref, tri_ref, small_ref, offc_ref, o_ref,
                     s_ref, nm_ref, p_ref, x_ref, aqk_ref, uw_ref, vn_ref, *, d):
    i = pl.program_id(1)
    C = DN_BLOCK
    heads = range(DN_HEADS)

    @pl.when(i == 0)
    def _():
        s_ref[...] = jnp.zeros_like(s_ref)

    rowi = lax.broadcasted_iota(jnp.int32, (C, C), 0)
    coli = lax.broadcasted_iota(jnp.int32, (C, C), 1)
    eye = (rowi == coli).astype(F32)
    lane = lax.broadcasted_iota(jnp.int32, (1, LANES), 1)
    r64 = lax.broadcasted_iota(jnp.int32, (DN_DK, LANES), 0)
    c64 = lax.broadcasted_iota(jnp.int32, (DN_DK, LANES), 1)
    zeros_kt = jnp.zeros((DN_DK, C), BF16)
    zeros_s = jnp.zeros((DN_DK, LANES), F32)

    def q_pair(h):
        return qn_ref[:, (h // 2) * LANES:(h // 2 + 1) * LANES]

    def vk_head(h):
        return vk_ref[:, h * LANES:(h + 1) * LANES]

    def kt_head(h):
        return kt_ref[h * DN_DK:(h + 1) * DN_DK, :]

    def gc_col(h):
        return aux_ref[0, :, h:h + 1]

    def beta_col(h):
        return aux_ref[0, :, DN_HEADS + h:DN_HEADS + h + 1]

    def gc_row(h):
        return auxt_ref[0, h:h + 1, :]

    for h in heads:
        kt = kt_head(h)
        kt_for_q = jnp.concatenate([kt, zeros_kt] if h % 2 == 0 else [zeros_kt, kt], axis=0)
        kt_for_k = jnp.concatenate([zeros_kt, kt], axis=0)
        qk = _dot(q_pair(h), kt_for_q)
        kk = _dot(vk_head(h), kt_for_k)
        e0 = jnp.exp(jnp.minimum(gc_col(h) - gc_row(h), 0.0))
        aqk_ref[h] = (qk * e0 * tri_ref[0]).astype(BF16)
        nm_ref[h] = (-(kk * e0 * tri_ref[1]) * beta_col(h)).astype(BF16)

    leaf = small_ref[0]
    for h in heads:
        nd = nm_ref[h] * leaf
        p_ref[h] = (eye + nd.astype(F32)).astype(BF16)
        x_ref[h] = _dot(nd, nd).astype(BF16)
    for h in heads:
        pb = p_ref[h]
        p_ref[h] = (pb.astype(F32) + _dot(pb, x_ref[h])).astype(BF16)
    for k in range(len(_FULL_MERGE_SIZES)):
        off_mask = small_ref[1 + k]
        for h in heads:
            x_ref[h] = _dot(nm_ref[h] * off_mask, p_ref[h]).astype(BF16)
        for h in heads:
            pb = p_ref[h]
            p_ref[h] = (pb.astype(F32) + _dot(pb, x_ref[h])).astype(BF16)
    for k, s in enumerate(_MERGE_SIZES):
        blocks = list(_active_blocks(d, s))
        zeros_blk = jnp.zeros((s, C), BF16)

        def active_rows(ref, h):
            return jnp.concatenate([ref[h, b * s:(b + 1) * s, :] for b in blocks], axis=0)

        for h in heads:
            xc = _dot(active_rows(nm_ref, h) * offc_ref[k], p_ref[h]).astype(BF16)
            pieces = []
            for j in range(len(blocks)):
                piece = xc[j * s:(j + 1) * s, :]
                pieces += [zeros_blk, piece] if d == 0 else [piece, zeros_blk]
            x_ref[h] = jnp.concatenate(pieces, axis=0)
        for h in heads:
            pr = active_rows(p_ref, h)
            prn = (pr.astype(F32) + _dot(pr, x_ref[h])).astype(BF16)
            for j, b in enumerate(blocks):
                p_ref[h, b * s:(b + 1) * s, :] = prn[j * s:(j + 1) * s, :]

    for h in heads:
        egc = jnp.exp(gc_col(h))
        rhs = (vk_head(h).astype(F32) * beta_col(h) * jnp.where(lane < DN_DV, 1.0, egc)).astype(BF16)
        uw_ref[h] = _dot(p_ref[h], rhs).astype(BF16)
    for h in heads:
        eye_pl = (c64 == r64 + (h % 2) * DN_DV).astype(F32)
        s_aug = jnp.concatenate([eye_pl, -s_ref[h]], axis=0).astype(BF16)
        vn_ref[h] = _dot(uw_ref[h], s_aug).astype(BF16)
    o_pair = None
    for h in heads:
        par = h % 2
        gc_r = gc_row(h)
        g_tot = gc_r[:, C - 1:C] if d == 0 else gc_r[:, 0:1]
        s_pl = s_ref[h]
        v_new = vn_ref[h]
        s_sel = jnp.concatenate([s_pl, zeros_s] if par == 0 else [zeros_s, s_pl], axis=0).astype(BF16)
        qd = (q_pair(h).astype(F32) * jnp.exp(gc_col(h))).astype(BF16)
        o_pl = _dot(qd, s_sel) + _dot(aqk_ref[h], v_new)
        kd = (kt_head(h).astype(F32) * jnp.exp(g_tot - gc_r)).astype(BF16)
        s_ref[h] = s_pl * jnp.exp(g_tot) + _dot(kd, v_new)
        if par == 0:
            o_pair = o_pl
        else:
            o_ref[:, (h // 2) * LANES:(h // 2 + 1) * LANES] = o_pair + o_pl


def _deltanet(qn, vk, kt, aux, auxt, B, S, W, d):
    T = qn.shape[0]
    C = DN_BLOCK
    nb = S // C

    def blk(b, i):
        return b * nb + (i if d == 0 else nb - 1 - i)

    nq = DN_HEADS * DN_DK
    tri, small, offc = W["dn_masks"][d]
    return pl.pallas_call(
        functools.partial(_deltanet_kernel, d=d),
        grid=(B, nb),
        in_specs=[
            pl.BlockSpec((C, nq), lambda b, i: (blk(b, i), 0)),
            pl.BlockSpec((C, 2 * nq), lambda b, i: (blk(b, i), 0)),
            pl.BlockSpec((nq, C), lambda b, i: (0, blk(b, i))),
            pl.BlockSpec((1, C, 2 * DN_HEADS), lambda b, i: (d, blk(b, i), 0)),
            pl.BlockSpec((1, 2 * DN_HEADS, C), lambda b, i: (d, 0, blk(b, i))),
            _const_spec((2, C, C)), _const_spec((1 + len(_FULL_MERGE_SIZES), C, C)),
            _const_spec((len(_MERGE_SIZES), C // 2, C)),
        ],
        out_specs=pl.BlockSpec((C, DN_HEADS * DN_DV), lambda b, i: (blk(b, i), 0)),
        out_shape=jax.ShapeDtypeStruct((T, DN_HEADS * DN_DV), F32),
        scratch_shapes=[pltpu.VMEM((DN_HEADS, DN_DK, LANES), F32)]
        + [pltpu.VMEM((DN_HEADS, C, C), BF16)] * 4
        + [pltpu.VMEM((DN_HEADS, C, LANES), BF16)] * 2,
        compiler_params=_params(("parallel", "arbitrary")),
        name="deltanet_fwd" if d == 0 else "deltanet_bwd",
    )(qn, vk, kt, aux, auxt, tri, small, offc)


def _layer_norm(v, g, b):
    mu = jnp.mean(v, axis=-1, keepdims=True)
    c = v - mu
    var = jnp.mean(c * c, axis=-1, keepdims=True)
    return c * lax.rsqrt(var + 1e-5) * g + b


def _mix_kernel(x_ref, attn_ref, of_ref, ob_ref, z_ref, gates_ref, p_ref,
                woa_ref, wod_ref, wout_ref, e8_ref, e8t_ref, dnorm_ref, ln1g_ref, ln1b_ref,
                wpg_ref, wpp_ref, rwh_ref, rwl_ref, rb_ref, ustrict_ref,
                r_ref, hb_ref, ti_ref, tg_ref, rank_ref, cnt_ref, run_ref):
    @pl.when(pl.program_id(0) == 0)
    def _():
        run_ref[...] = jnp.zeros_like(run_ref)

    oa = _dot(attn_ref[...], woa_ref[...])
    o = of_ref[...] + ob_ref[...]
    hi, lo = _split_bf16(o * o)
    ms = (_dot(hi, e8_ref[...]) + _dot(lo, e8_ref[...])) * (1.0 / DN_DV)
    ih, il = _split_bf16(lax.rsqrt(ms + 1e-6))
    sc = _dot(ih, e8t_ref[...]) + _dot(il, e8t_ref[...])
    zf = z_ref[...].astype(F32)
    od_in = o * sc * dnorm_ref[...] * (zf * (1.0 / (1.0 + jnp.exp(-zf))))
    od = _dot(od_in.astype(BF16), wod_ref[...])
    mix = gates_ref[:, :D_MODEL].astype(F32) * oa + gates_ref[:, D_MODEL:].astype(F32) * od
    mo = _dot(mix.astype(BF16), wout_ref[...])
    h = _layer_norm(DEEPNORM_ALPHA * x_ref[...] + mo, ln1g_ref[...], ln1b_ref[...])
    hb = h.astype(BF16)
    hb_ref[...] = _pack_halves(h)
    pg = _dot(hb, wpg_ref[...])
    pp = _dot(p_ref[...].astype(BF16), wpp_ref[...])
    r_ref[...] = DEEPNORM_ALPHA * h + pp * (1.0 / (1.0 + jnp.exp(-pg)))

    hl = (h - hb.astype(F32)).astype(BF16)
    logits = (_dot_nt(rwh_ref[...], hb) + _dot_nt(rwh_ref[...], hl)
              + _dot_nt(rwl_ref[...], hb) + rb_ref[...])
    eid = lax.broadcasted_iota(jnp.int32, logits.shape, 0)
    vals = []
    run = run_ref[:, 0:1]
    for k in range(TOP_K):
        m = jnp.max(logits, axis=0, keepdims=True)
        idx = jnp.min(jnp.where(logits == m, eid, N_EXPERTS), axis=0, keepdims=True)
        ti_ref[k:k + 1, :] = idx
        vals.append(m)
        hit = eid == idx
        logits = jnp.where(hit, -jnp.inf, logits)
        onehot = hit.astype(F32)
        earlier = _dot(onehot.astype(BF16), ustrict_ref[...])
        rank = jnp.sum(onehot * (run + earlier), axis=0, keepdims=True)
        rank_ref[k:k + 1, :] = rank.astype(jnp.int32)
        run = run + jnp.sum(onehot, axis=1, keepdims=True)
    run_ref[...] = jnp.broadcast_to(run, run_ref.shape)
    cnt_ref[...] = jnp.broadcast_to(run, cnt_ref.shape)
    es = [jnp.exp(v - vals[0]) for v in vals]
    den = es[0] + es[1] + es[2] + es[3]
    for k in range(TOP_K):
        tg_ref[k:k + 1, :] = es[k] / den


def _mix(x2, attn, o_f, o_b, z, gates, p2, W):
    tm = MIX_TILE
    T = x2.shape[0]
    row = lambda i: (i, 0)
    nd = DN_HEADS * DN_DV
    return pl.pallas_call(
        _mix_kernel,
        grid=(T // tm,),
        in_specs=[
            pl.BlockSpec((tm, D_MODEL), row),
            pl.BlockSpec((tm, MLA_HEADS * V_DIM), row),
            pl.BlockSpec((tm, nd), row),
            pl.BlockSpec((tm, nd), row),
            pl.BlockSpec((tm, nd), row),
            pl.BlockSpec((tm, 2 * D_MODEL), row),
            pl.BlockSpec((tm, PLE_DIM), row),
            _const_spec((MLA_HEADS * V_DIM, D_MODEL)), _const_spec((nd, D_MODEL)),
            _const_spec((D_MODEL, D_MODEL)),
            _const_spec((nd, LANES)), _const_spec((LANES, nd)), _const_spec((1, nd)),
            _const_spec((1, D_MODEL)), _const_spec((1, D_MODEL)),
            _const_spec((D_MODEL, D_MODEL)), _const_spec((PLE_DIM, D_MODEL)),
            _const_spec((N_EXPERTS, D_MODEL)), _const_spec((N_EXPERTS, D_MODEL)),
            _const_spec((N_EXPERTS, 1)),
            _const_spec((tm, tm)),
        ],
        out_specs=[
            pl.BlockSpec((tm, D_MODEL), row),
            pl.BlockSpec((tm, D_MODEL // 2), row),
            pl.BlockSpec((TOP_K, tm), lambda i: (0, i)),
            pl.BlockSpec((TOP_K, tm), lambda i: (0, i)),
            pl.BlockSpec((TOP_K, tm), lambda i: (0, i)),
            _const_spec((N_EXPERTS, LANES)),
        ],
        out_shape=[
            jax.ShapeDtypeStruct((T, D_MODEL), F32),
            jax.ShapeDtypeStruct((T, D_MODEL // 2), jnp.uint32),
            jax.ShapeDtypeStruct((TOP_K, T), jnp.int32),
            jax.ShapeDtypeStruct((TOP_K, T), F32),
            jax.ShapeDtypeStruct((TOP_K, T), jnp.int32),
            jax.ShapeDtypeStruct((N_EXPERTS, LANES), F32),
        ],
        scratch_shapes=[pltpu.VMEM((N_EXPERTS, LANES), F32)],
        compiler_params=_params(("arbitrary",)),
        name="mix",
    )(x2, attn, o_f, o_b, z, gates, p2, W["woa"], W["wod"], W["wout"], W["e8"], W["e8t"], W["dnorm"],
      W["ln1g"], W["ln1b"], W["wpg"], W["wpp"], W["rwh"], W["rwl"], W["rb"], W["ustrict"])


_CAST_ROWS = 256


def _moe_kernel(blk_e_ref, nvalid_ref, xs_ref, wgu32_ref, bgu_ref, wd32_ref, bd_ref, y_ref,
                wgu_ref, wd_ref, *, fc):
    i = pl.program_id(0)
    valid = i < nvalid_ref[0]
    new_expert = (i == 0) | (blk_e_ref[i] != blk_e_ref[jnp.maximum(i - 1, 0)])

    @pl.when(valid & new_expert)
    def _():
        for r in range(0, D_MODEL, _CAST_ROWS):
            wgu_ref[r:r + _CAST_ROWS, :] = wgu32_ref[0, r:r + _CAST_ROWS, :].astype(BF16)
        for r in range(0, D_FF, _CAST_ROWS):
            wd_ref[r:r + _CAST_ROWS, :] = wd32_ref[0, r:r + _CAST_ROWS, :].astype(BF16)

    @pl.when(valid)
    def _():
        xs = jnp.concatenate(_unpack_halves(xs_ref[...]), axis=1).astype(BF16)
        acc = None
        for c in range(D_FF // fc):
            lo, hi = c * fc, (c + 1) * fc
            gate = _dot(xs, wgu_ref[:, lo:hi]) + bgu_ref[0, :, lo:hi]
            up = _dot(xs, wgu_ref[:, D_FF + lo:D_FF + hi]) + bgu_ref[0, :, D_FF + lo:D_FF + hi]
            gate = jnp.minimum(gate, SWIGLU_LIMIT)
            up = jnp.clip(up, -SWIGLU_LIMIT, SWIGLU_LIMIT)
            act = gate * (1.0 / (1.0 + jnp.exp(-SWIGLU_ALPHA * gate))) * (up + 1.0)
            part = _dot(act.astype(BF16), wd_ref[lo:hi, :])
            acc = part if acc is None else acc + part
        y_ref[...] = _pack_halves(acc + bd_ref[0])

    @pl.when(jnp.logical_not(valid))
    def _():
        y_ref[...] = jnp.zeros_like(y_ref)


def _moe(xs, blk_e, nvalid, W, bm, fc=512):
    P = xs.shape[0]
    grid_spec = pltpu.PrefetchScalarGridSpec(
        num_scalar_prefetch=2,
        grid=(P // bm,),
        in_specs=[
            pl.BlockSpec((bm, D_MODEL // 2), lambda i, be, nv: (i, 0)),
            pl.BlockSpec((1, D_MODEL, 2 * D_FF), lambda i, be, nv: (be[i], 0, 0)),
            pl.BlockSpec((1, 1, 2 * D_FF), lambda i, be, nv: (be[i], 0, 0)),
            pl.BlockSpec((1, D_FF, D_MODEL), lambda i, be, nv: (be[i], 0, 0)),
            pl.BlockSpec((1, 1, D_MODEL), lambda i, be, nv: (be[i], 0, 0)),
        ],
        out_specs=pl.BlockSpec((bm, D_MODEL // 2), lambda i, be, nv: (i, 0)),
        scratch_shapes=[pltpu.VMEM((D_MODEL, 2 * D_FF), BF16), pltpu.VMEM((D_FF, D_MODEL), BF16)],
    )
    return pl.pallas_call(
        functools.partial(_moe_kernel, fc=fc),
        grid_spec=grid_spec,
        out_shape=jax.ShapeDtypeStruct((P, D_MODEL // 2), jnp.uint32),
        compiler_params=_params(("arbitrary",)),
        name="moe",
    )(blk_e, nvalid, xs, W["wgu"], W["bgu"], W["wd"], W["bd"])


def _final_kernel(r_ref, yg_ref, tg_ref, g_ref, b_ref, y_ref):
    ffn_lo = ffn_hi = None
    for k in range(TOP_K):
        lo, hi = _unpack_halves(yg_ref[k])
        g = tg_ref[:, k:k + 1]
        ffn_lo = lo * g if ffn_lo is None else ffn_lo + lo * g
        ffn_hi = hi * g if ffn_hi is None else ffn_hi + hi * g
    acc = r_ref[...] + jnp.concatenate([ffn_lo, ffn_hi], axis=1)
    y_ref[...] = _layer_norm(acc, g_ref[...], b_ref[...])


def _final(r, yg, tg, W, tm=512):
    T = r.shape[0]
    tm = min(tm, T)
    row = lambda i: (i, 0)
    return pl.pallas_call(
        _final_kernel,
        grid=(T // tm,),
        in_specs=[pl.BlockSpec((tm, D_MODEL), row),
                  pl.BlockSpec((TOP_K, tm, D_MODEL // 2), lambda i: (0, i, 0)),
                  pl.BlockSpec((tm, TOP_K), row),
                  _const_spec((1, D_MODEL)), _const_spec((1, D_MODEL))],
        out_specs=pl.BlockSpec((tm, D_MODEL), row),
        out_shape=jax.ShapeDtypeStruct((T, D_MODEL), F32),
        compiler_params=_params(("parallel",)),
        name="final_ln",
    )(r, yg, tg, W["ln2g"], W["ln2b"])


def _pad_heads(w, n_heads, width, start, size, dst=0):
    K = w.shape[0]
    w3 = w.reshape(K, n_heads, width)[:, :, start:start + size]
    out = jnp.zeros((K, n_heads, HEAD_PAD), w.dtype)
    out = out.at[:, :, dst:dst + size].set(w3)
    return out.reshape(K, n_heads * HEAD_PAD)


def _prep_weights(w_in, q_a_norm, w_uq, kv_a_norm, w_ukv, w_o_attn, dn_conv, dn_a_log, dn_dt_bias,
                  dn_norm, w_o_dn, w_out, ln1_g, ln1_b, router_w, router_b, w_gate_up, b_gate_up,
                  w_down, b_down, ple_w_proj, ple_w_gate, ln2_g, ln2_b):
    W = {}
    half = ROPE_DIM // 2
    o = 0
    cq = w_in[:, o:o + Q_LORA]; o += Q_LORA
    ckv = w_in[:, o:o + KV_LORA]; o += KV_LORA
    kr = w_in[:, o:o + ROPE_DIM]; o += ROPE_DIM
    nqk = DN_HEADS * DN_DK
    dq = w_in[:, o:o + nqk]; o += nqk
    dk = w_in[:, o:o + nqk]; o += nqk
    dv = w_in[:, o:o + DN_HEADS * DN_DV]; o += DN_HEADS * DN_DV
    dz = w_in[:, o:o + DN_HEADS * DN_DV]; o += DN_HEADS * DN_DV
    da = w_in[:, o:o + 2 * DN_HEADS]; o += 2 * DN_HEADS
    db = w_in[:, o:o + 2 * DN_HEADS]; o += 2 * DN_HEADS
    gate = w_in[:, o:o + 2 * D_MODEL]

    def lane_block(parts):
        w = jnp.concatenate(parts, axis=1)
        return jnp.pad(w, ((0, 0), (0, LANES - w.shape[1])))

    def misc_block(d, rope_cols):
        ab = [da[:, d * DN_HEADS:(d + 1) * DN_HEADS], db[:, d * DN_HEADS:(d + 1) * DN_HEADS]]
        return lane_block(ab + [jnp.zeros((D_MODEL, NOPE_DIM - 2 * DN_HEADS), F32), rope_cols])

    misc0 = misc_block(0, kr)
    misc1 = misc_block(1, jnp.concatenate([kr[:, half:], kr[:, :half]], axis=1))

    def interleave_vk(v, k):
        lead = v.shape[:-1]
        v3 = v.reshape(lead + (DN_HEADS, DN_DV))
        k3 = k.reshape(lead + (DN_HEADS, DN_DK))
        return jnp.concatenate([v3, k3], axis=-1).reshape(lead + (DN_HEADS * (DN_DV + DN_DK),))

    W["w1"] = jnp.concatenate([cq, ckv, misc0, misc1, dq, interleave_vk(dv, dk), dz, gate],
                              axis=1).astype(BF16)
    W["qan"] = q_a_norm.reshape(1, Q_LORA)
    W["kvan"] = kv_a_norm.reshape(1, KV_LORA)

    qw = NOPE_DIM + ROPE_DIM
    wq_nope = _pad_heads(w_uq, MLA_HEADS, qw, 0, NOPE_DIM, 0)
    wq_r1 = _pad_heads(w_uq, MLA_HEADS, qw, NOPE_DIM, half, NOPE_DIM)
    wq_r2 = _pad_heads(w_uq, MLA_HEADS, qw, NOPE_DIM + half, half, NOPE_DIM + half)
    W["wq"] = (wq_nope + wq_r1 + wq_r2).astype(BF16)
    wq_s1 = _pad_heads(w_uq, MLA_HEADS, qw, NOPE_DIM + half, half, NOPE_DIM)
    wq_s2 = _pad_heads(w_uq, MLA_HEADS, qw, NOPE_DIM, half, NOPE_DIM + half)
    W["wqs"] = (wq_s1 + wq_s2).astype(BF16)
    kvw = NOPE_DIM + V_DIM
    W["wk"] = _pad_heads(w_ukv, MLA_HEADS, kvw, 0, NOPE_DIM, 0).astype(BF16)
    W["wv"] = _pad_heads(w_ukv, MLA_HEADS, kvw, NOPE_DIM, V_DIM, 0).astype(BF16)

    neg_a = -jnp.exp(dn_a_log.astype(F32))
    abp = jnp.zeros((8, LANES), F32)
    for d in range(2):
        abp = abp.at[2 * d, :DN_HEADS].set(neg_a[d])
        abp = abp.at[2 * d + 1, :DN_HEADS].set(dn_dt_bias[d].astype(F32))
    W["abp"] = abp

    cw = jnp.concatenate([dn_conv[:, :nqk], interleave_vk(dn_conv[:, 2 * nqk:], dn_conv[:, nqk:2 * nqk])], axis=1)
    W["convw"] = jnp.pad(cw.astype(F32), ((0, 8 - CONV_K), (0, 0)))

    def group_indicator(width, group):
        e = (np.arange(width)[:, None] // group == np.arange(LANES)[None, :]).astype(np.float32)
        return e

    eq = group_indicator(nqk, DN_DK)
    W["eq"] = jnp.asarray(eq, BF16)
    W["eqt"] = jnp.asarray(eq.T, BF16)
    evk = group_indicator(2 * nqk, DN_DK)
    W["evk"] = jnp.asarray(evk, BF16)
    W["evkt"] = jnp.asarray(evk.T, BF16)
    W["e8"] = W["eq"]
    W["e8t"] = W["eqt"]
    r = np.arange(DN_BLOCK)
    W["tri"] = jnp.asarray(np.stack([r[:, None] >= r[None, :], r[:, None] <= r[None, :]]).astype(np.float32), BF16)
    rp = np.arange(DN_PREP_TILE)
    W["conv_shift"] = jnp.asarray(
        np.stack([rp[None, :] == rp[:, None] + (j - CONV_K // 2) for j in range(CONV_K)]).astype(np.float32), BF16)
    W["dn_masks"] = [(jnp.asarray(t), jnp.asarray(g, BF16), jnp.asarray(o, BF16))
                     for t, g, o in (_dn_masks(0), _dn_masks(1))]
    rt = np.arange(MIX_TILE)
    W["ustrict"] = jnp.asarray((rt[:, None] < rt[None, :]).astype(np.float32), BF16)

    W["woa"] = w_o_attn.astype(BF16)
    W["wod"] = w_o_dn.astype(BF16)
    W["wout"] = w_out.astype(BF16)
    W["dnorm"] = jnp.tile(dn_norm.astype(F32), DN_HEADS).reshape(1, DN_HEADS * DN_DV)
    W["ln1g"] = ln1_g.reshape(1, D_MODEL)
    W["ln1b"] = ln1_b.reshape(1, D_MODEL)
    W["ln2g"] = ln2_g.reshape(1, D_MODEL)
    W["ln2b"] = ln2_b.reshape(1, D_MODEL)
    W["wpg"] = ple_w_gate.astype(BF16)
    W["wpp"] = ple_w_proj.astype(BF16)
    rwt = router_w.T.astype(F32)
    W["rwh"], W["rwl"] = _split_bf16(rwt)
    W["rb"] = router_b.reshape(N_EXPERTS, 1).astype(F32)
    W["wgu"] = w_gate_up.astype(F32)
    W["bgu"] = b_gate_up.reshape(N_EXPERTS, 1, 2 * D_FF).astype(F32)
    W["wd"] = w_down.astype(F32)
    W["bd"] = b_down.reshape(N_EXPERTS, 1, D_MODEL).astype(F32)
    return W


def _rope_tables(S):
    half = ROPE_DIM // 2
    inv = ROPE_THETA ** (-jnp.arange(0, ROPE_DIM, 2, dtype=F32) / ROPE_DIM)
    ang = jnp.arange(S, dtype=F32)[:, None] * inv[None, :]
    cos, sin = jnp.cos(ang), jnp.sin(ang)
    c = (NOPE_DIM + ROPE_DIM) ** -0.5 * math.log2(math.e)
    pad = jnp.zeros((S, HEAD_PAD - NOPE_DIM - ROPE_DIM), F32)
    cos_blk = jnp.concatenate([cos, cos, pad], axis=1)
    sin_blk = jnp.concatenate([-sin, sin, pad], axis=1)
    cosq = jnp.concatenate([jnp.ones((S, NOPE_DIM), F32), cos_blk], axis=1) * c
    sinq = jnp.concatenate([jnp.zeros((S, NOPE_DIM), F32), sin_blk], axis=1) * c
    cosk = jnp.concatenate([jnp.zeros((S, NOPE_DIM), F32), cos_blk], axis=1)
    sink = jnp.concatenate([jnp.zeros((S, NOPE_DIM), F32), sin_blk], axis=1)
    return cosq, sinq, cosk, sink


def _dest_kernel(pstart_ref, ti_ref, rank_ref, dest_ref):
    ti = ti_ref[...]
    dest = rank_ref[...]
    for e in range(N_EXPERTS):
        dest = dest + jnp.where(ti == e, pstart_ref[e], 0)
    dest_ref[...] = dest


def _dest(p_start, top_i, rank, tile=8192):
    T = top_i.shape[1]
    tile = min(tile, T)
    spec = pl.BlockSpec((TOP_K, tile), lambda i, ps: (0, i))
    return pl.pallas_call(
        _dest_kernel,
        grid_spec=pltpu.PrefetchScalarGridSpec(num_scalar_prefetch=1, grid=(T // tile,),
                                               in_specs=[spec, spec], out_specs=spec),
        out_shape=jax.ShapeDtypeStruct((TOP_K, T), jnp.int32),
        compiler_params=_params(("parallel",)),
        name="slot_index",
    )(p_start, top_i, rank)


SC_CHUNK = 128


def _sc_mesh():
    info = plsc.get_sparse_core_info()
    mesh = plsc.VectorSubcoreMesh(core_axis_name="c", subcore_axis_name="s")
    return mesh, info.num_cores, info.num_cores * info.num_subcores


def _sc_dispatch(rows, dest3, P):
    T, D = rows.shape
    K = dest3.shape[0]
    mesh, n_cores, n_workers = _sc_mesh()
    n_chunks = T // (n_workers * SC_CHUNK)

    @functools.partial(
        pl.kernel, mesh=mesh, out_type=jax.ShapeDtypeStruct((P, D), rows.dtype),
        scratch_types=[pltpu.VMEM((K, SC_CHUNK), jnp.int32), pltpu.VMEM((SC_CHUNK, D), rows.dtype),
                       pltpu.SemaphoreType.DMA])
    def dispatch(rows_hbm, dest_hbm, out_hbm, idx_v, rows_v, sem):
        worker = lax.axis_index("s") * n_cores + lax.axis_index("c")

        @pl.loop(0, n_chunks)
        def _(j):
            g = worker * n_chunks + j
            base = pl.multiple_of(g * SC_CHUNK, SC_CHUNK)
            pltpu.sync_copy(rows_hbm.at[pl.ds(base, SC_CHUNK)], rows_v)
            pltpu.sync_copy(dest_hbm.at[:, g], idx_v)
            for k in range(K):
                pltpu.async_copy(rows_v, out_hbm.at[idx_v.at[k]], sem).wait()

    return dispatch(rows, dest3)


def _sc_gather(table, idx):
    M = idx.shape[0]
    D = table.shape[1]
    mesh, n_cores, n_workers = _sc_mesh()
    n_chunks = M // (n_workers * SC_CHUNK)

    @functools.partial(
        pl.kernel, mesh=mesh, out_type=jax.ShapeDtypeStruct((M, D), table.dtype),
        scratch_types=[pltpu.VMEM((SC_CHUNK,), jnp.int32), pltpu.VMEM((SC_CHUNK, D), table.dtype),
                       pltpu.SemaphoreType.DMA])
    def gather(table_hbm, idx_hbm, out_hbm, idx_v, rows_v, sem):
        worker = lax.axis_index("s") * n_cores + lax.axis_index("c")

        @pl.loop(0, n_chunks)
        def _(j):
            base = pl.multiple_of((worker * n_chunks + j) * SC_CHUNK, SC_CHUNK)
            pltpu.sync_copy(idx_hbm.at[pl.ds(base, SC_CHUNK)], idx_v)
            pltpu.async_copy(table_hbm.at[idx_v], rows_v, sem).wait()
            pltpu.sync_copy(rows_v, out_hbm.at[pl.ds(base, SC_CHUNK)])

    return gather(table, idx)


def _route(top_i, rank, counts, T, bm):
    A = TOP_K * T
    counts = counts.astype(jnp.int32)
    padded = ((counts + bm - 1) // bm) * bm
    p_end = jnp.cumsum(padded)
    p_start = p_end - padded
    dest = _dest(p_start, top_i, rank)
    nblk = A // bm + N_EXPERTS
    blk_start = jnp.arange(nblk, dtype=jnp.int32) * bm
    blk_e = jnp.minimum(jnp.sum(p_end[None, :] <= blk_start[:, None], axis=1), N_EXPERTS - 1).astype(jnp.int32)
    nvalid = (p_end[-1] // bm).astype(jnp.int32).reshape(1)
    return dest, nblk * bm, blk_e, nvalid


def _layer(x, p, W, bm):
    B, S, _ = x.shape
    T = B * S
    x2 = x.reshape(T, D_MODEL)
    p2 = p.reshape(T, PLE_DIM)
    Wl = dict(W)
    Wl["cosq"], Wl["sinq"], Wl["cosk"], Wl["sink"] = _rope_tables(S)

    q, k, v, dn, z, gates, gb = _in_proj(x2, S, Wl)
    attn = _attention(q, k, v, B, S)
    qn, vk, kt, aux = _dn_prep(dn, gb, S, Wl)
    auxt = jnp.swapaxes(aux, 1, 2)
    o_f = _deltanet(qn, vk, kt, aux, auxt, B, S, Wl, 0)
    o_b = _deltanet(qn, vk, kt, aux, auxt, B, S, Wl, 1)
    r, hb, top_i, top_g, rank, cnt = _mix(x2, attn, o_f, o_b, z, gates, p2, Wl)

    dest, P, blk_e, nvalid = _route(top_i, rank, cnt[:, 0], T, bm)
    xs = _sc_dispatch(hb, dest.reshape(TOP_K, T // SC_CHUNK, SC_CHUNK), P)
    yb = _moe(xs, blk_e, nvalid, Wl, bm)
    yg = _sc_gather(yb, dest.reshape(TOP_K * T)).reshape(TOP_K, T, D_MODEL // 2)
    y = _final(r, yg, top_g.T, Wl)
    return y.reshape(B, S, D_MODEL)


def kernel(x_prompt, x_sample, p_prompt, p_sample, w_in, q_a_norm, w_uq, kv_a_norm, w_ukv, w_o_attn, dn_conv, dn_a_log, dn_dt_bias, dn_norm, w_o_dn, w_out, ln1_g, ln1_b, router_w, router_b, w_gate_up, b_gate_up, w_down, b_down, ple_w_proj, ple_w_gate, ln2_g, ln2_b):
    y_prompt, y_sample = x_prompt, x_sample
    for l in range(DEPTH):
        W = _prep_weights(w_in[l], q_a_norm[l], w_uq[l], kv_a_norm[l], w_ukv[l], w_o_attn[l], dn_conv[l],
                          dn_a_log[l], dn_dt_bias[l], dn_norm[l], w_o_dn[l], w_out[l], ln1_g[l], ln1_b[l],
                          router_w[l], router_b[l], w_gate_up[l], b_gate_up[l], w_down[l], b_down[l],
                          ple_w_proj[l], ple_w_gate[l], ln2_g[l], ln2_b[l])
        y_prompt = _layer(y_prompt, p_prompt[l], W, bm=MOE_BLOCK)
        y_sample = _layer(y_sample, p_sample[l], W, bm=MOE_BLOCK)
    return (y_prompt, y_sample)
```

```python
import functools
import math

import numpy as np
import jax
import jax.numpy as jnp
from jax import lax
from jax.experimental import pallas as pl
from jax.experimental.pallas import tpu as pltpu
from jax.experimental.pallas import tpu_sc as plsc

D_MODEL = 1024
MLA_HEADS = 8
Q_LORA = 256
KV_LORA = 128
NOPE_DIM = 64
ROPE_DIM = 32
V_DIM = 64
ROPE_THETA = 10000.0
DN_HEADS = 8
DN_DK = 64
DN_DV = 64
CONV_K = 5
N_EXPERTS = 32
TOP_K = 4
D_FF = 1024
SWIGLU_LIMIT = 7.0
SWIGLU_ALPHA = 1.702
PLE_DIM = 256
DEPTH = 1
DEEPNORM_ALPHA = (2.0 * DEPTH) ** 0.25

LANES = 128
HEAD_PAD = 128
DN_BLOCK = 256
DN_STEP_BLOCKS = 2
DN_PREP_TILE = 256
MOE_BLOCK = 512
MIX_TILE = 512
VMEM_LIMIT = 56 * 1024 * 1024

_C_CQ = 0
_C_CKV = _C_CQ + Q_LORA
_C_MISC0 = _C_CKV + KV_LORA
_C_MISC1 = _C_MISC0 + LANES
_C_DNQ = _C_MISC1 + LANES
_C_DNVK = _C_DNQ + DN_HEADS * DN_DK
_C_Z = _C_DNVK + DN_HEADS * (DN_DK + DN_DV)
_C_GATE = _C_Z + DN_HEADS * DN_DV
_C_END = _C_GATE + 2 * D_MODEL

BF16 = jnp.bfloat16
F32 = jnp.float32


def _dot(a, b):
    return jnp.dot(a, b, preferred_element_type=F32)


def _dot_nt(a, b):
    return lax.dot_general(a, b, (((1,), (1,)), ((), ())), preferred_element_type=F32)


def _split_bf16(x):
    hi = x.astype(BF16)
    lo = (x - hi.astype(F32)).astype(BF16)
    return hi, lo


_HI_HALFWORD = 0xFFFF0000


def _pack_halves(x):
    w = x.shape[1] // 2
    bits = lax.bitcast_convert_type(x.astype(BF16).astype(F32), jnp.uint32)
    return (bits[:, :w] >> 16) | (bits[:, w:] & jnp.uint32(_HI_HALFWORD))


def _unpack_halves(words):
    lo = lax.bitcast_convert_type(words << 16, F32)
    hi = lax.bitcast_convert_type(words & jnp.uint32(_HI_HALFWORD), F32)
    return lo, hi


def _const_spec(shape):
    n = len(shape)
    return pl.BlockSpec(shape, lambda *_: (0,) * n, pipeline_mode=pl.Buffered(1))


def _params(sem):
    return pltpu.CompilerParams(dimension_semantics=sem, vmem_limit_bytes=VMEM_LIMIT)


def _in_proj_kernel(x_ref, w1_ref, qan_ref, kvan_ref, wq_ref, wqs_ref, wk_ref, wv_ref,
                    cosq_ref, sinq_ref, cosk_ref, sink_ref, abp_ref,
                    q_ref, k_ref, v_ref, dn_ref, z_ref, gates_ref, gb_ref):
    xb = x_ref[...].astype(BF16)

    def proj(lo, hi):
        return _dot(xb, w1_ref[:, lo:hi])

    def rms(c, g):
        return (c * lax.rsqrt(jnp.mean(c * c, axis=-1, keepdims=True) + 1e-6) * g).astype(BF16)

    cqn = rms(proj(_C_CQ, _C_CKV), qan_ref[...])
    qa = _dot(cqn, wq_ref[...])
    qb = _dot(cqn, wqs_ref[...])
    ckvn = rms(proj(_C_CKV, _C_MISC0), kvan_ref[...])
    kw = _dot(ckvn, wk_ref[...])
    vw = _dot(ckvn, wv_ref[...])
    misc = (proj(_C_MISC0, _C_MISC1), proj(_C_MISC1, _C_DNQ))
    kr = misc[0] * cosk_ref[...] + misc[1] * sink_ref[...]
    cosq = cosq_ref[...]
    sinq = sinq_ref[...]
    lane = lax.broadcasted_iota(jnp.int32, (1, HEAD_PAD), 1)
    ones_col = (lane == V_DIM).astype(F32)
    for h in range(MLA_HEADS):
        sl = slice(h * HEAD_PAD, (h + 1) * HEAD_PAD)
        q_ref[:, sl] = (qa[:, sl] * cosq + qb[:, sl] * sinq).astype(BF16)
        k_ref[:, sl] = (kw[:, sl] + kr).astype(BF16)
        v_ref[:, sl] = (vw[:, sl] + ones_col).astype(BF16)

    for d in range(2):
        ab = misc[d]
        neg_a = abp_ref[2 * d:2 * d + 1, :]
        dtb = abp_ref[2 * d + 1:2 * d + 2, :]
        t = ab + dtb
        sp = jnp.maximum(t, 0.0) + jnp.log(1.0 + jnp.exp(-jnp.abs(t)))
        g = neg_a * sp
        beta = 1.0 / (1.0 + jnp.exp(-ab))
        gb_ref[d] = jnp.where(lane < DN_HEADS, g, beta)[:, :2 * DN_HEADS]

    dn_ref[...] = proj(_C_DNQ, _C_Z).astype(BF16)
    z_ref[...] = proj(_C_Z, _C_GATE).astype(BF16)
    gl = proj(_C_GATE, _C_END)
    gates_ref[...] = (1.0 / (1.0 + jnp.exp(-gl))).astype(BF16)


def _in_proj(x2, S, W, tm=512):
    T = x2.shape[0]
    nseq = S // tm
    row = lambda i: (i, 0)
    pos = lambda i: (i % nseq, 0)
    dn_w = _C_Z - _C_DNQ
    return pl.pallas_call(
        _in_proj_kernel,
        grid=(T // tm,),
        in_specs=[
            pl.BlockSpec((tm, D_MODEL), row),
            _const_spec((D_MODEL, _C_END)),
            _const_spec((1, Q_LORA)), _const_spec((1, KV_LORA)),
            _const_spec((Q_LORA, MLA_HEADS * HEAD_PAD)), _const_spec((Q_LORA, MLA_HEADS * HEAD_PAD)),
            _const_spec((KV_LORA, MLA_HEADS * HEAD_PAD)), _const_spec((KV_LORA, MLA_HEADS * HEAD_PAD)),
            pl.BlockSpec((tm, HEAD_PAD), pos), pl.BlockSpec((tm, HEAD_PAD), pos),
            pl.BlockSpec((tm, HEAD_PAD), pos), pl.BlockSpec((tm, HEAD_PAD), pos),
            _const_spec((8, LANES)),
        ],
        out_specs=[
            pl.BlockSpec((tm, MLA_HEADS * HEAD_PAD), row),
            pl.BlockSpec((tm, MLA_HEADS * HEAD_PAD), row),
            pl.BlockSpec((tm, MLA_HEADS * HEAD_PAD), row),
            pl.BlockSpec((tm, dn_w), row),
            pl.BlockSpec((tm, DN_HEADS * DN_DV), row),
            pl.BlockSpec((tm, 2 * D_MODEL), row),
            pl.BlockSpec((2, tm, 2 * DN_HEADS), lambda i: (0, i, 0)),
        ],
        out_shape=[
            jax.ShapeDtypeStruct((T, MLA_HEADS * HEAD_PAD), BF16),
            jax.ShapeDtypeStruct((T, MLA_HEADS * HEAD_PAD), BF16),
            jax.ShapeDtypeStruct((T, MLA_HEADS * HEAD_PAD), BF16),
            jax.ShapeDtypeStruct((T, dn_w), BF16),
            jax.ShapeDtypeStruct((T, DN_HEADS * DN_DV), BF16),
            jax.ShapeDtypeStruct((T, 2 * D_MODEL), BF16),
            jax.ShapeDtypeStruct((2, T, 2 * DN_HEADS), F32),
        ],
        compiler_params=_params(("parallel",)),
        name="in_proj",
    )(x2, W["w1"], W["qan"], W["kvan"], W["wq"], W["wqs"], W["wk"], W["wv"],
      W["cosq"], W["sinq"], W["cosk"], W["sink"], W["abp"])


def _attn_kernel(q_ref, k_ref, v_ref, o_ref, *, tk, unroll):
    tq = q_ref.shape[0]
    S = k_ref.shape[0]
    outs = []
    for hh in range(2):
        sl = slice(hh * HEAD_PAD, (hh + 1) * HEAD_PAD)
        q = q_ref[:, sl]

        def body(j, carry, q=q, sl=sl):
            m, acc = carry
            off = pl.multiple_of(j * tk, tk)
            s = _dot_nt(q, k_ref[pl.ds(off, tk), sl])
            m_new = jnp.maximum(m, jnp.max(s, axis=-1, keepdims=True))
            p = jnp.exp2(s - m_new).astype(BF16)
            acc = acc * jnp.exp2(m - m_new) + _dot(p, v_ref[pl.ds(off, tk), sl])
            return m_new, acc

        m0 = jnp.full((tq, 1), -1e30, F32)
        acc0 = jnp.zeros((tq, HEAD_PAD), F32)
        _, acc = lax.fori_loop(0, S // tk, body, (m0, acc0), unroll=unroll)
        outs.append(acc / acc[:, V_DIM:V_DIM + 1])
    lane = lax.broadcasted_iota(jnp.int32, (1, HEAD_PAD), 1)
    o_ref[...] = jnp.where(lane < V_DIM, outs[0], pltpu.roll(outs[1], V_DIM, axis=1)).astype(BF16)


def _attention(q, k, v, B, S, tq=1024, tk=2048, unroll=4):
    T = q.shape[0]
    tq = min(tq, S)
    tk = min(tk, S)
    nq = S // tq
    return pl.pallas_call(
        functools.partial(_attn_kernel, tk=tk, unroll=unroll),
        grid=(B, MLA_HEADS // 2, nq),
        in_specs=[
            pl.BlockSpec((tq, 2 * HEAD_PAD), lambda b, h, i: (b * nq + i, h)),
            pl.BlockSpec((S, 2 * HEAD_PAD), lambda b, h, i: (b, h)),
            pl.BlockSpec((S, 2 * HEAD_PAD), lambda b, h, i: (b, h)),
        ],
        out_specs=pl.BlockSpec((tq, 2 * V_DIM), lambda b, h, i: (b * nq + i, h)),
        out_shape=jax.ShapeDtypeStruct((T, MLA_HEADS * V_DIM), BF16),
        compiler_params=_params(("parallel", "parallel", "arbitrary")),
        name="attention",
    )(q, k, v)


_HALO = 16


def _dn_prep_kernel(x_ref, prev_ref, next_ref, cw_ref, eq_ref, eqt_ref, evk_ref, evkt_ref,
                    gb_ref, tri_ref, shift_ref, qn_ref, vk_ref, kt_ref, aux_ref, *, nseq):
    i = pl.program_id(0)
    tp = x_ref.shape[0]
    first = (i % nseq) == 0
    last = (i % nseq) == nseq - 1
    half = CONV_K // 2
    xb = x_ref[...]

    y = xb.astype(F32) * cw_ref[half:half + 1, :]
    for j in range(CONV_K):
        if j != half:
            y = y + _dot(shift_ref[j], xb) * cw_ref[j:j + 1, :]

    def edge(slab, row0):
        n = slab.shape[0]
        acc = None
        for j in range(CONV_K):
            shift = (half - j) % n
            rolled = slab if shift == 0 else pltpu.roll(slab, shift, axis=0)
            term = rolled[row0:row0 + 8, :] * cw_ref[j:j + 1, :]
            acc = term if acc is None else acc + term
        return acc

    prev = jnp.where(first, 0.0, prev_ref[...].astype(F32))
    nxt = jnp.where(last, 0.0, next_ref[...].astype(F32))
    top = edge(jnp.concatenate([prev, xb[:_HALO].astype(F32)], axis=0), _HALO)
    bot = edge(jnp.concatenate([xb[tp - _HALO:].astype(F32), nxt], axis=0), _HALO - 8)
    y = jnp.concatenate([top, y[8:tp - 8], bot], axis=0)
    y = y * (1.0 / (1.0 + jnp.exp(-y)))

    def group_scale(v, e_ref, et_ref):
        ss = _dot((v * v).astype(BF16), e_ref[...])
        return _dot(lax.rsqrt(ss + 1e-6).astype(BF16), et_ref[...])

    nq = DN_HEADS * DN_DK
    yq = y[:, :nq]
    qn_ref[...] = (yq * group_scale(yq, eq_ref, eqt_ref) * (DN_DK ** -0.5)).astype(BF16)
    yvk = y[:, nq:]
    sc = group_scale(yvk, evk_ref, evkt_ref)
    lane = lax.broadcasted_iota(jnp.int32, (1, yvk.shape[1]), 1)
    is_k = (lane // DN_DV) % 2 == 1
    vk = yvk * jnp.where(is_k, sc, 1.0)
    vk_ref[...] = vk.astype(BF16)
    vkt = vk.T
    for h in range(DN_HEADS):
        r0 = h * (DN_DV + DN_DK) + DN_DV
        kt_ref[h * DN_DK:(h + 1) * DN_DK, :] = vkt[r0:r0 + DN_DK, :].astype(BF16)

    hcol = lax.broadcasted_iota(jnp.int32, (1, 2 * DN_HEADS), 1) < DN_HEADS
    for d in range(2):
        gb = gb_ref[d]
        for blk in range(tp // DN_BLOCK):
            rs = slice(blk * DN_BLOCK, (blk + 1) * DN_BLOCK)
            g = gb[rs]
            g_hi = g.astype(BF16)
            g_mid, g_lo = _split_bf16(g - g_hi.astype(F32))
            cs = _dot(tri_ref[d], g_hi) + _dot(tri_ref[d], g_mid) + _dot(tri_ref[d], g_lo)
            aux_ref[d, rs, :] = jnp.where(hcol, cs, g)


def _dn_prep(dn, gb, S, W):
    tp = DN_PREP_TILE
    T = dn.shape[0]
    nseq = S // tp
    hb = tp // _HALO
    nh = T // _HALO
    dn_w = dn.shape[1]
    nq = DN_HEADS * DN_DK
    return pl.pallas_call(
        functools.partial(_dn_prep_kernel, nseq=nseq),
        grid=(T // tp,),
        in_specs=[
            pl.BlockSpec((tp, dn_w), lambda i: (i, 0)),
            pl.BlockSpec((_HALO, dn_w), lambda i: (jnp.maximum(i * hb - 1, 0), 0)),
            pl.BlockSpec((_HALO, dn_w), lambda i: (jnp.minimum((i + 1) * hb, nh - 1), 0)),
            _const_spec((8, dn_w)),
            _const_spec((nq, LANES)), _const_spec((LANES, nq)),
            _const_spec((dn_w - nq, LANES)), _const_spec((LANES, dn_w - nq)),
            pl.BlockSpec((2, tp, 2 * DN_HEADS), lambda i: (0, i, 0)),
            _const_spec((2, DN_BLOCK, DN_BLOCK)),
            _const_spec((CONV_K, tp, tp)),
        ],
        out_specs=[
            pl.BlockSpec((tp, nq), lambda i: (i, 0)),
            pl.BlockSpec((tp, dn_w - nq), lambda i: (i, 0)),
            pl.BlockSpec((nq, tp), lambda i: (0, i)),
            pl.BlockSpec((2, tp, 2 * DN_HEADS), lambda i: (0, i, 0)),
        ],
        out_shape=[
            jax.ShapeDtypeStruct((T, nq), BF16),
            jax.ShapeDtypeStruct((T, dn_w - nq), BF16),
            jax.ShapeDtypeStruct((nq, T), BF16),
            jax.ShapeDtypeStruct((2, T, 2 * DN_HEADS), F32),
        ],
        compiler_params=_params(("parallel",)),
        name="dn_prep",
    )(dn, dn, dn, W["convw"], W["eq"], W["eqt"], W["evk"], W["evkt"], gb, W["tri"], W["conv_shift"])


_LEAF = 4
_FULL_MERGE_SIZES = (4, 8)
_MERGE_SIZES = (16, 32, 64, 128)


def _active_blocks(d, s):
    return range(1 - d, DN_BLOCK // s, 2)


def _dn_masks(d):
    r = np.arange(DN_BLOCK)[:, None]
    c = np.arange(DN_BLOCK)[None, :]
    rr, cc = (r, c) if d == 0 else (c, r)
    tri = np.stack([rr >= cc, rr > cc]).astype(np.float32)
    def off(s):
        return ((rr // s) % 2 == 1) & ((rr // s) == (cc // s) + 1)

    small = np.stack([(r // _LEAF) == (c // _LEAF)] + [off(s) for s in _FULL_MERGE_SIZES]).astype(np.float32)
    offc = []
    for s in _MERGE_SIZES:
        rows = np.concatenate([np.arange(b * s, (b + 1) * s) for b in _active_blocks(d, s)])
        assert not np.delete(off(s), rows, axis=0).any()
        offc.append(off(s)[rows].astype(np.float32))
    return tri, small, np.stack(offc)


def _deltanet_kernel(qn_ref, vk_ref, kt_ref, aux_ref, auxt_ref, tri_ref, small_ref, offc_ref, o_ref,
                     s_ref, *scratch, d):
    @pl.when(pl.program_id(1) == 0)
    def _():
        s_ref[...] = jnp.zeros_like(s_ref)

    C = DN_BLOCK
    order = range(DN_STEP_BLOCKS) if d == 0 else range(DN_STEP_BLOCKS - 1, -1, -1)
    for sub in order:
        rows = pl.ds(sub * C, C)
        _deltanet_block(qn_ref.at[rows], vk_ref.at[rows], kt_ref.at[:, rows], aux_ref.at[:, rows],
                        auxt_ref.at[:, :, rows], tri_ref, small_ref, offc_ref, o_ref.at[rows],
                        s_ref, *scratch, d=d)


def _deltanet_block(qn_ref, vk_ref, kt_ref, aux_ref, auxt_ref, tri_ref, small_ref, offc_ref, o_ref,
                    s_ref, nm_ref, p_ref, x_ref, aqk_ref, uw_ref, vn_ref, *, d):
    C = DN_BLOCK
    heads = range(DN_HEADS)

    rowi = lax.broadcasted_iota(jnp.int32, (C, C), 0)
    coli = lax.broadcasted_iota(jnp.int32, (C, C), 1)
    eye = (rowi == coli).astype(F32)
    lane = lax.broadcasted_iota(jnp.int32, (1, LANES), 1)
    r64 = lax.broadcasted_iota(jnp.int32, (DN_DK, LANES), 0)
    c64 = lax.broadcasted_iota(jnp.int32, (DN_DK, LANES), 1)
    zeros_kt = jnp.zeros((DN_DK, C), BF16)
    zeros_s = jnp.zeros((DN_DK, LANES), F32)

    def q_pair(h):
        return qn_ref[:, (h // 2) * LANES:(h // 2 + 1) * LANES]

    def vk_head(h):
        return vk_ref[:, h * LANES:(h + 1) * LANES]

    def kt_head(h):
        return kt_ref[h * DN_DK:(h + 1) * DN_DK, :]

    def gc_col(h):
        return aux_ref[0, :, h:h + 1]

    def beta_col(h):
        return aux_ref[0, :, DN_HEADS + h:DN_HEADS + h + 1]

    def gc_row(h):
        return auxt_ref[0, h:h + 1, :]

    for h in heads:
        kt = kt_head(h)
        kt_for_q = jnp.concatenate([kt, zeros_kt] if h % 2 == 0 else [zeros_kt, kt], axis=0)
        kt_for_k = jnp.concatenate([zeros_kt, kt], axis=0)
        qk = _dot(q_pair(h), kt_for_q)
        kk = _dot(vk_head(h), kt_for_k)
        e0 = jnp.exp(jnp.minimum(gc_col(h) - gc_row(h), 0.0))
        aqk_ref[h] = (qk * e0 * tri_ref[0]).astype(BF16)
        nm_ref[h] = (-(kk * e0 * tri_ref[1]) * beta_col(h)).astype(BF16)

    leaf = small_ref[0]
    for h in heads:
        nd = nm_ref[h] * leaf
        p_ref[h] = (eye + nd.astype(F32)).astype(BF16)
        x_ref[h] = _dot(nd, nd).astype(BF16)
    for h in heads:
        pb = p_ref[h]
        p_ref[h] = (pb.astype(F32) + _dot(pb, x_ref[h])).astype(BF16)
    for k in range(len(_FULL_MERGE_SIZES)):
        off_mask = small_ref[1 + k]
        for h in heads:
            x_ref[h] = _dot(nm_ref[h] * off_mask, p_ref[h]).astype(BF16)
        for h in heads:
            pb = p_ref[h]
            p_ref[h] = (pb.astype(F32) + _dot(pb, x_ref[h])).astype(BF16)
    for k, s in enumerate(_MERGE_SIZES):
        blocks = list(_active_blocks(d, s))
        zeros_blk = jnp.zeros((s, C), BF16)

        def active_rows(ref, h):
            return jnp.concatenate([ref[h, b * s:(b + 1) * s, :] for b in blocks], axis=0)

        for h in heads:
            xc = _dot(active_rows(nm_ref, h) * offc_ref[k], p_ref[h]).astype(BF16)
            pieces = []
            for j in range(len(blocks)):
                piece = xc[j * s:(j + 1) * s, :]
                pieces += [zeros_blk, piece] if d == 0 else [piece, zeros_blk]
            x_ref[h] = jnp.concatenate(pieces, axis=0)
        for h in heads:
            pr = active_rows(p_ref, h)
            prn = (pr.astype(F32) + _dot(pr, x_ref[h])).astype(BF16)
            for j, b in enumerate(blocks):
                p_ref[h, b * s:(b + 1) * s, :] = prn[j * s:(j + 1) * s, :]

    for h in heads:
        egc = jnp.exp(gc_col(h))
        rhs = (vk_head(h).astype(F32) * beta_col(h) * jnp.where(lane < DN_DV, 1.0, egc)).astype(BF16)
        uw_ref[h] = _dot(p_ref[h], rhs).astype(BF16)
    for h in heads:
        eye_pl = (c64 == r64 + (h % 2) * DN_DV).astype(F32)
        s_aug = jnp.concatenate([eye_pl, -s_ref[h]], axis=0).astype(BF16)
        vn_ref[h] = _dot(uw_ref[h], s_aug).astype(BF16)
    o_pair = None
    for h in heads:
        par = h % 2
        gc_r = gc_row(h)
        g_tot = gc_r[:, C - 1:C] if d == 0 else gc_r[:, 0:1]
        s_pl = s_ref[h]
        v_new = vn_ref[h]
        s_sel = jnp.concatenate([s_pl, zeros_s] if par == 0 else [zeros_s, s_pl], axis=0).astype(BF16)
        qd = (q_pair(h).astype(F32) * jnp.exp(gc_col(h))).astype(BF16)
        o_pl = _dot(qd, s_sel) + _dot(aqk_ref[h], v_new)
        kd = (kt_head(h).astype(F32) * jnp.exp(g_tot - gc_r)).astype(BF16)
        s_ref[h] = s_pl * jnp.exp(g_tot) + _dot(kd, v_new)
        if par == 0:
            o_pair = o_pl
        else:
            o_ref[:, (h // 2) * LANES:(h // 2 + 1) * LANES] = o_pair + o_pl


def _deltanet(qn, vk, kt, aux, auxt, B, S, W, d):
    T = qn.shape[0]
    C = DN_BLOCK
    R = DN_STEP_BLOCKS * C
    nb = S // R

    def blk(b, i):
        return b * nb + (i if d == 0 else nb - 1 - i)

    nq = DN_HEADS * DN_DK
    tri, small, offc = W["dn_masks"][d]
    return pl.pallas_call(
        functools.partial(_deltanet_kernel, d=d),
        grid=(B, nb),
        in_specs=[
            pl.BlockSpec((R, nq), lambda b, i: (blk(b, i), 0)),
            pl.BlockSpec((R, 2 * nq), lambda b, i: (blk(b, i), 0)),
            pl.BlockSpec((nq, R), lambda b, i: (0, blk(b, i))),
            pl.BlockSpec((1, R, 2 * DN_HEADS), lambda b, i: (d, blk(b, i), 0)),
            pl.BlockSpec((1, 2 * DN_HEADS, R), lambda b, i: (d, 0, blk(b, i))),
            _const_spec((2, C, C)), _const_spec((1 + len(_FULL_MERGE_SIZES), C, C)),
            _const_spec((len(_MERGE_SIZES), C // 2, C)),
        ],
        out_specs=pl.BlockSpec((R, DN_HEADS * DN_DV), lambda b, i: (blk(b, i), 0)),
        out_shape=jax.ShapeDtypeStruct((T, DN_HEADS * DN_DV), F32),
        scratch_shapes=[pltpu.VMEM((DN_HEADS, DN_DK, LANES), F32)]
        + [pltpu.VMEM((DN_HEADS, C, C), BF16)] * 4
        + [pltpu.VMEM((DN_HEADS, C, LANES), BF16)] * 2,
        compiler_params=_params(("parallel", "arbitrary")),
        name="deltanet_fwd" if d == 0 else "deltanet_bwd",
    )(qn, vk, kt, aux, auxt, tri, small, offc)


def _layer_norm(v, g, b):
    mu = jnp.mean(v, axis=-1, keepdims=True)
    c = v - mu
    var = jnp.mean(c * c, axis=-1, keepdims=True)
    return c * lax.rsqrt(var + 1e-5) * g + b


def _mix_kernel(x_ref, attn_ref, of_ref, ob_ref, z_ref, gates_ref, p_ref,
                woa_ref, wod_ref, wout_ref, e8_ref, e8t_ref, dnorm_ref, ln1g_ref, ln1b_ref,
                wpg_ref, wpp_ref, rwh_ref, rwl_ref, rb_ref, ustrict_ref,
                r_ref, hb_ref, ti_ref, tg_ref, rank_ref, cnt_ref, run_ref):
    @pl.when(pl.program_id(0) == 0)
    def _():
        run_ref[...] = jnp.zeros_like(run_ref)

    oa = _dot(attn_ref[...], woa_ref[...])
    o = of_ref[...] + ob_ref[...]
    hi, lo = _split_bf16(o * o)
    ms = (_dot(hi, e8_ref[...]) + _dot(lo, e8_ref[...])) * (1.0 / DN_DV)
    ih, il = _split_bf16(lax.rsqrt(ms + 1e-6))
    sc = _dot(ih, e8t_ref[...]) + _dot(il, e8t_ref[...])
    zf = z_ref[...].astype(F32)
    od_in = o * sc * dnorm_ref[...] * (zf * (1.0 / (1.0 + jnp.exp(-zf))))
    od = _dot(od_in.astype(BF16), wod_ref[...])
    mix = gates_ref[:, :D_MODEL].astype(F32) * oa + gates_ref[:, D_MODEL:].astype(F32) * od
    mo = _dot(mix.astype(BF16), wout_ref[...])
    h = _layer_norm(DEEPNORM_ALPHA * x_ref[...] + mo, ln1g_ref[...], ln1b_ref[...])
    hb = h.astype(BF16)
    hb_ref[...] = _pack_halves(h)
    pg = _dot(hb, wpg_ref[...])
    pp = _dot(p_ref[...].astype(BF16), wpp_ref[...])
    r_ref[...] = DEEPNORM_ALPHA * h + pp * (1.0 / (1.0 + jnp.exp(-pg)))

    hl = (h - hb.astype(F32)).astype(BF16)
    logits = (_dot_nt(rwh_ref[...], hb) + _dot_nt(rwh_ref[...], hl)
              + _dot_nt(rwl_ref[...], hb) + rb_ref[...])
    eid = lax.broadcasted_iota(jnp.int32, logits.shape, 0)
    vals = []
    run = run_ref[:, 0:1]
    for k in range(TOP_K):
        m = jnp.max(logits, axis=0, keepdims=True)
        idx = jnp.min(jnp.where(logits == m, eid, N_EXPERTS), axis=0, keepdims=True)
        ti_ref[k:k + 1, :] = idx
        vals.append(m)
        hit = eid == idx
        logits = jnp.where(hit, -jnp.inf, logits)
        onehot = hit.astype(F32)
        earlier = _dot(onehot.astype(BF16), ustrict_ref[...])
        rank = jnp.sum(onehot * (run + earlier), axis=0, keepdims=True)
        rank_ref[k:k + 1, :] = rank.astype(jnp.int32)
        run = run + jnp.sum(onehot, axis=1, keepdims=True)
    run_ref[...] = jnp.broadcast_to(run, run_ref.shape)
    cnt_ref[...] = jnp.broadcast_to(run, cnt_ref.shape)
    es = [jnp.exp(v - vals[0]) for v in vals]
    den = es[0] + es[1] + es[2] + es[3]
    for k in range(TOP_K):
        tg_ref[k:k + 1, :] = es[k] / den


def _mix(x2, attn, o_f, o_b, z, gates, p2, W):
    tm = MIX_TILE
    T = x2.shape[0]
    row = lambda i: (i, 0)
    nd = DN_HEADS * DN_DV
    return pl.pallas_call(
        _mix_kernel,
        grid=(T // tm,),
        in_specs=[
            pl.BlockSpec((tm, D_MODEL), row),
            pl.BlockSpec((tm, MLA_HEADS * V_DIM), row),
            pl.BlockSpec((tm, nd), row),
            pl.BlockSpec((tm, nd), row),
            pl.BlockSpec((tm, nd), row),
            pl.BlockSpec((tm, 2 * D_MODEL), row),
            pl.BlockSpec((tm, PLE_DIM), row),
            _const_spec((MLA_HEADS * V_DIM, D_MODEL)), _const_spec((nd, D_MODEL)),
            _const_spec((D_MODEL, D_MODEL)),
            _const_spec((nd, LANES)), _const_spec((LANES, nd)), _const_spec((1, nd)),
            _const_spec((1, D_MODEL)), _const_spec((1, D_MODEL)),
            _const_spec((D_MODEL, D_MODEL)), _const_spec((PLE_DIM, D_MODEL)),
            _const_spec((N_EXPERTS, D_MODEL)), _const_spec((N_EXPERTS, D_MODEL)),
            _const_spec((N_EXPERTS, 1)),
            _const_spec((tm, tm)),
        ],
        out_specs=[
            pl.BlockSpec((tm, D_MODEL), row),
            pl.BlockSpec((tm, D_MODEL // 2), row),
            pl.BlockSpec((TOP_K, tm), lambda i: (0, i)),
            pl.BlockSpec((TOP_K, tm), lambda i: (0, i)),
            pl.BlockSpec((TOP_K, tm), lambda i: (0, i)),
            _const_spec((N_EXPERTS, LANES)),
        ],
        out_shape=[
            jax.ShapeDtypeStruct((T, D_MODEL), F32),
            jax.ShapeDtypeStruct((T, D_MODEL // 2), jnp.uint32),
            jax.ShapeDtypeStruct((TOP_K, T), jnp.int32),
            jax.ShapeDtypeStruct((TOP_K, T), F32),
            jax.ShapeDtypeStruct((TOP_K, T), jnp.int32),
            jax.ShapeDtypeStruct((N_EXPERTS, LANES), F32),
        ],
        scratch_shapes=[pltpu.VMEM((N_EXPERTS, LANES), F32)],
        compiler_params=_params(("arbitrary",)),
        name="mix",
    )(x2, attn, o_f, o_b, z, gates, p2, W["woa"], W["wod"], W["wout"], W["e8"], W["e8t"], W["dnorm"],
      W["ln1g"], W["ln1b"], W["wpg"], W["wpp"], W["rwh"], W["rwl"], W["rb"], W["ustrict"])


_CAST_ROWS = 256


def _moe_kernel(blk_e_ref, nvalid_ref, xs_ref, wgu32_ref, bgu_ref, wd32_ref, bd_ref, y_ref,
                wgu_ref, wd_ref, *, fc):
    i = pl.program_id(0)
    valid = i < nvalid_ref[0]
    new_expert = (i == 0) | (blk_e_ref[i] != blk_e_ref[jnp.maximum(i - 1, 0)])

    @pl.when(valid & new_expert)
    def _():
        for r in range(0, D_MODEL, _CAST_ROWS):
            wgu_ref[r:r + _CAST_ROWS, :] = wgu32_ref[0, r:r + _CAST_ROWS, :].astype(BF16)
        for r in range(0, D_FF, _CAST_ROWS):
            wd_ref[r:r + _CAST_ROWS, :] = wd32_ref[0, r:r + _CAST_ROWS, :].astype(BF16)

    @pl.when(valid)
    def _():
        xs = jnp.concatenate(_unpack_halves(xs_ref[...]), axis=1).astype(BF16)
        acc = None
        for c in range(D_FF // fc):
            lo, hi = c * fc, (c + 1) * fc
            gate = _dot(xs, wgu_ref[:, lo:hi]) + bgu_ref[0, :, lo:hi]
            up = _dot(xs, wgu_ref[:, D_FF + lo:D_FF + hi]) + bgu_ref[0, :, D_FF + lo:D_FF + hi]
            gate = jnp.minimum(gate, SWIGLU_LIMIT)
            up = jnp.clip(up, -SWIGLU_LIMIT, SWIGLU_LIMIT)
            act = gate * (1.0 / (1.0 + jnp.exp(-SWIGLU_ALPHA * gate))) * (up + 1.0)
            part = _dot(act.astype(BF16), wd_ref[lo:hi, :])
            acc = part if acc is None else acc + part
        y_ref[...] = _pack_halves(acc + bd_ref[0])

    @pl.when(jnp.logical_not(valid))
    def _():
        y_ref[...] = jnp.zeros_like(y_ref)


def _moe(xs, blk_e, nvalid, W, bm, fc=512):
    P = xs.shape[0]
    grid_spec = pltpu.PrefetchScalarGridSpec(
        num_scalar_prefetch=2,
        grid=(P // bm,),
        in_specs=[
            pl.BlockSpec((bm, D_MODEL // 2), lambda i, be, nv: (i, 0)),
            pl.BlockSpec((1, D_MODEL, 2 * D_FF), lambda i, be, nv: (be[i], 0, 0)),
            pl.BlockSpec((1, 1, 2 * D_FF), lambda i, be, nv: (be[i], 0, 0)),
            pl.BlockSpec((1, D_FF, D_MODEL), lambda i, be, nv: (be[i], 0, 0)),
            pl.BlockSpec((1, 1, D_MODEL), lambda i, be, nv: (be[i], 0, 0)),
        ],
        out_specs=pl.BlockSpec((bm, D_MODEL // 2), lambda i, be, nv: (i, 0)),
        scratch_shapes=[pltpu.VMEM((D_MODEL, 2 * D_FF), BF16), pltpu.VMEM((D_FF, D_MODEL), BF16)],
    )
    return pl.pallas_call(
        functools.partial(_moe_kernel, fc=fc),
        grid_spec=grid_spec,
        out_shape=jax.ShapeDtypeStruct((P, D_MODEL // 2), jnp.uint32),
        compiler_params=_params(("arbitrary",)),
        name="moe",
    )(blk_e, nvalid, xs, W["wgu"], W["bgu"], W["wd"], W["bd"])


def _final_kernel(r_ref, yg_ref, tg_ref, g_ref, b_ref, y_ref):
    ffn_lo = ffn_hi = None
    for k in range(TOP_K):
        lo, hi = _unpack_halves(yg_ref[k])
        g = tg_ref[:, k:k + 1]
        ffn_lo = lo * g if ffn_lo is None else ffn_lo + lo * g
        ffn_hi = hi * g if ffn_hi is None else ffn_hi + hi * g
    acc = r_ref[...] + jnp.concatenate([ffn_lo, ffn_hi], axis=1)
    y_ref[...] = _layer_norm(acc, g_ref[...], b_ref[...])


def _final(r, yg, tg, W, tm=512):
    T = r.shape[0]
    tm = min(tm, T)
    row = lambda i: (i, 0)
    return pl.pallas_call(
        _final_kernel,
        grid=(T // tm,),
        in_specs=[pl.BlockSpec((tm, D_MODEL), row),
                  pl.BlockSpec((TOP_K, tm, D_MODEL // 2), lambda i: (0, i, 0)),
                  pl.BlockSpec((tm, TOP_K), row),
                  _const_spec((1, D_MODEL)), _const_spec((1, D_MODEL))],
        out_specs=pl.BlockSpec((tm, D_MODEL), row),
        out_shape=jax.ShapeDtypeStruct((T, D_MODEL), F32),
        compiler_params=_params(("parallel",)),
        name="final_ln",
    )(r, yg, tg, W["ln2g"], W["ln2b"])


def _pad_heads(w, n_heads, width, start, size, dst=0):
    K = w.shape[0]
    w3 = w.reshape(K, n_heads, width)[:, :, start:start + size]
    out = jnp.zeros((K, n_heads, HEAD_PAD), w.dtype)
    out = out.at[:, :, dst:dst + size].set(w3)
    return out.reshape(K, n_heads * HEAD_PAD)


def _prep_weights(w_in, q_a_norm, w_uq, kv_a_norm, w_ukv, w_o_attn, dn_conv, dn_a_log, dn_dt_bias,
                  dn_norm, w_o_dn, w_out, ln1_g, ln1_b, router_w, router_b, w_gate_up, b_gate_up,
                  w_down, b_down, ple_w_proj, ple_w_gate, ln2_g, ln2_b):
    W = {}
    half = ROPE_DIM // 2
    o = 0
    cq = w_in[:, o:o + Q_LORA]; o += Q_LORA
    ckv = w_in[:, o:o + KV_LORA]; o += KV_LORA
    kr = w_in[:, o:o + ROPE_DIM]; o += ROPE_DIM
    nqk = DN_HEADS * DN_DK
    dq = w_in[:, o:o + nqk]; o += nqk
    dk = w_in[:, o:o + nqk]; o += nqk
    dv = w_in[:, o:o + DN_HEADS * DN_DV]; o += DN_HEADS * DN_DV
    dz = w_in[:, o:o + DN_HEADS * DN_DV]; o += DN_HEADS * DN_DV
    da = w_in[:, o:o + 2 * DN_HEADS]; o += 2 * DN_HEADS
    db = w_in[:, o:o + 2 * DN_HEADS]; o += 2 * DN_HEADS
    gate = w_in[:, o:o + 2 * D_MODEL]

    def lane_block(parts):
        w = jnp.concatenate(parts, axis=1)
        return jnp.pad(w, ((0, 0), (0, LANES - w.shape[1])))

    def misc_block(d, rope_cols):
        ab = [da[:, d * DN_HEADS:(d + 1) * DN_HEADS], db[:, d * DN_HEADS:(d + 1) * DN_HEADS]]
        return lane_block(ab + [jnp.zeros((D_MODEL, NOPE_DIM - 2 * DN_HEADS), F32), rope_cols])

    misc0 = misc_block(0, kr)
    misc1 = misc_block(1, jnp.concatenate([kr[:, half:], kr[:, :half]], axis=1))

    def interleave_vk(v, k):
        lead = v.shape[:-1]
        v3 = v.reshape(lead + (DN_HEADS, DN_DV))
        k3 = k.reshape(lead + (DN_HEADS, DN_DK))
        return jnp.concatenate([v3, k3], axis=-1).reshape(lead + (DN_HEADS * (DN_DV + DN_DK),))

    W["w1"] = jnp.concatenate([cq, ckv, misc0, misc1, dq, interleave_vk(dv, dk), dz, gate],
                              axis=1).astype(BF16)
    W["qan"] = q_a_norm.reshape(1, Q_LORA)
    W["kvan"] = kv_a_norm.reshape(1, KV_LORA)

    qw = NOPE_DIM + ROPE_DIM
    wq_nope = _pad_heads(w_uq, MLA_HEADS, qw, 0, NOPE_DIM, 0)
    wq_r1 = _pad_heads(w_uq, MLA_HEADS, qw, NOPE_DIM, half, NOPE_DIM)
    wq_r2 = _pad_heads(w_uq, MLA_HEADS, qw, NOPE_DIM + half, half, NOPE_DIM + half)
    W["wq"] = (wq_nope + wq_r1 + wq_r2).astype(BF16)
    wq_s1 = _pad_heads(w_uq, MLA_HEADS, qw, NOPE_DIM + half, half, NOPE_DIM)
    wq_s2 = _pad_heads(w_uq, MLA_HEADS, qw, NOPE_DIM, half, NOPE_DIM + half)
    W["wqs"] = (wq_s1 + wq_s2).astype(BF16)
    kvw = NOPE_DIM + V_DIM
    W["wk"] = _pad_heads(w_ukv, MLA_HEADS, kvw, 0, NOPE_DIM, 0).astype(BF16)
    W["wv"] = _pad_heads(w_ukv, MLA_HEADS, kvw, NOPE_DIM, V_DIM, 0).astype(BF16)

    neg_a = -jnp.exp(dn_a_log.astype(F32))
    abp = jnp.zeros((8, LANES), F32)
    for d in range(2):
        abp = abp.at[2 * d, :DN_HEADS].set(neg_a[d])
        abp = abp.at[2 * d + 1, :DN_HEADS].set(dn_dt_bias[d].astype(F32))
    W["abp"] = abp

    cw = jnp.concatenate([dn_conv[:, :nqk], interleave_vk(dn_conv[:, 2 * nqk:], dn_conv[:, nqk:2 * nqk])], axis=1)
    W["convw"] = jnp.pad(cw.astype(F32), ((0, 8 - CONV_K), (0, 0)))

    def group_indicator(width, group):
        e = (np.arange(width)[:, None] // group == np.arange(LANES)[None, :]).astype(np.float32)
        return e

    eq = group_indicator(nqk, DN_DK)
    W["eq"] = jnp.asarray(eq, BF16)
    W["eqt"] = jnp.asarray(eq.T, BF16)
    evk = group_indicator(2 * nqk, DN_DK)
    W["evk"] = jnp.asarray(evk, BF16)
    W["evkt"] = jnp.asarray(evk.T, BF16)
    W["e8"] = W["eq"]
    W["e8t"] = W["eqt"]
    r = np.arange(DN_BLOCK)
    W["tri"] = jnp.asarray(np.stack([r[:, None] >= r[None, :], r[:, None] <= r[None, :]]).astype(np.float32), BF16)
    rp = np.arange(DN_PREP_TILE)
    W["conv_shift"] = jnp.asarray(
        np.stack([rp[None, :] == rp[:, None] + (j - CONV_K // 2) for j in range(CONV_K)]).astype(np.float32), BF16)
    W["dn_masks"] = [(jnp.asarray(t), jnp.asarray(g, BF16), jnp.asarray(o, BF16))
                     for t, g, o in (_dn_masks(0), _dn_masks(1))]
    rt = np.arange(MIX_TILE)
    W["ustrict"] = jnp.asarray((rt[:, None] < rt[None, :]).astype(np.float32), BF16)

    W["woa"] = w_o_attn.astype(BF16)
    W["wod"] = w_o_dn.astype(BF16)
    W["wout"] = w_out.astype(BF16)
    W["dnorm"] = jnp.tile(dn_norm.astype(F32), DN_HEADS).reshape(1, DN_HEADS * DN_DV)
    W["ln1g"] = ln1_g.reshape(1, D_MODEL)
    W["ln1b"] = ln1_b.reshape(1, D_MODEL)
    W["ln2g"] = ln2_g.reshape(1, D_MODEL)
    W["ln2b"] = ln2_b.reshape(1, D_MODEL)
    W["wpg"] = ple_w_gate.astype(BF16)
    W["wpp"] = ple_w_proj.astype(BF16)
    rwt = router_w.T.astype(F32)
    W["rwh"], W["rwl"] = _split_bf16(rwt)
    W["rb"] = router_b.reshape(N_EXPERTS, 1).astype(F32)
    W["wgu"] = w_gate_up.astype(F32)
    W["bgu"] = b_gate_up.reshape(N_EXPERTS, 1, 2 * D_FF).astype(F32)
    W["wd"] = w_down.astype(F32)
    W["bd"] = b_down.reshape(N_EXPERTS, 1, D_MODEL).astype(F32)
    return W


def _rope_tables(S):
    half = ROPE_DIM // 2
    inv = ROPE_THETA ** (-jnp.arange(0, ROPE_DIM, 2, dtype=F32) / ROPE_DIM)
    ang = jnp.arange(S, dtype=F32)[:, None] * inv[None, :]
    cos, sin = jnp.cos(ang), jnp.sin(ang)
    c = (NOPE_DIM + ROPE_DIM) ** -0.5 * math.log2(math.e)
    pad = jnp.zeros((S, HEAD_PAD - NOPE_DIM - ROPE_DIM), F32)
    cos_blk = jnp.concatenate([cos, cos, pad], axis=1)
    sin_blk = jnp.concatenate([-sin, sin, pad], axis=1)
    cosq = jnp.concatenate([jnp.ones((S, NOPE_DIM), F32), cos_blk], axis=1) * c
    sinq = jnp.concatenate([jnp.zeros((S, NOPE_DIM), F32), sin_blk], axis=1) * c
    cosk = jnp.concatenate([jnp.zeros((S, NOPE_DIM), F32), cos_blk], axis=1)
    sink = jnp.concatenate([jnp.zeros((S, NOPE_DIM), F32), sin_blk], axis=1)
    return cosq, sinq, cosk, sink


def _dest_kernel(pstart_ref, ti_ref, rank_ref, dest_ref):
    ti = ti_ref[...]
    dest = rank_ref[...]
    for e in range(N_EXPERTS):
        dest = dest + jnp.where(ti == e, pstart_ref[e], 0)
    dest_ref[...] = dest


def _dest(p_start, top_i, rank, tile=8192):
    T = top_i.shape[1]
    tile = min(tile, T)
    spec = pl.BlockSpec((TOP_K, tile), lambda i, ps: (0, i))
    return pl.pallas_call(
        _dest_kernel,
        grid_spec=pltpu.PrefetchScalarGridSpec(num_scalar_prefetch=1, grid=(T // tile,),
                                               in_specs=[spec, spec], out_specs=spec),
        out_shape=jax.ShapeDtypeStruct((TOP_K, T), jnp.int32),
        compiler_params=_params(("parallel",)),
        name="slot_index",
    )(p_start, top_i, rank)


SC_CHUNK = 128


def _sc_mesh():
    info = plsc.get_sparse_core_info()
    mesh = plsc.VectorSubcoreMesh(core_axis_name="c", subcore_axis_name="s")
    return mesh, info.num_cores, info.num_cores * info.num_subcores


def _sc_dispatch(rows, dest3, P):
    T, D = rows.shape
    K = dest3.shape[0]
    mesh, n_cores, n_workers = _sc_mesh()
    n_chunks = T // (n_workers * SC_CHUNK)

    @functools.partial(
        pl.kernel, mesh=mesh, out_type=jax.ShapeDtypeStruct((P, D), rows.dtype),
        scratch_types=[pltpu.VMEM((K, SC_CHUNK), jnp.int32), pltpu.VMEM((SC_CHUNK, D), rows.dtype),
                       pltpu.SemaphoreType.DMA])
    def dispatch(rows_hbm, dest_hbm, out_hbm, idx_v, rows_v, sem):
        worker = lax.axis_index("s") * n_cores + lax.axis_index("c")

        @pl.loop(0, n_chunks)
        def _(j):
            g = worker * n_chunks + j
            base = pl.multiple_of(g * SC_CHUNK, SC_CHUNK)
            pltpu.sync_copy(rows_hbm.at[pl.ds(base, SC_CHUNK)], rows_v)
            pltpu.sync_copy(dest_hbm.at[:, g], idx_v)
            for k in range(K):
                pltpu.async_copy(rows_v, out_hbm.at[idx_v.at[k]], sem).wait()

    return dispatch(rows, dest3)


def _sc_gather(table, idx):
    M = idx.shape[0]
    D = table.shape[1]
    mesh, n_cores, n_workers = _sc_mesh()
    n_chunks = M // (n_workers * SC_CHUNK)

    @functools.partial(
        pl.kernel, mesh=mesh, out_type=jax.ShapeDtypeStruct((M, D), table.dtype),
        scratch_types=[pltpu.VMEM((SC_CHUNK,), jnp.int32), pltpu.VMEM((SC_CHUNK, D), table.dtype),
                       pltpu.SemaphoreType.DMA])
    def gather(table_hbm, idx_hbm, out_hbm, idx_v, rows_v, sem):
        worker = lax.axis_index("s") * n_cores + lax.axis_index("c")

        @pl.loop(0, n_chunks)
        def _(j):
            base = pl.multiple_of((worker * n_chunks + j) * SC_CHUNK, SC_CHUNK)
            pltpu.sync_copy(idx_hbm.at[pl.ds(base, SC_CHUNK)], idx_v)
            pltpu.async_copy(table_hbm.at[idx_v], rows_v, sem).wait()
            pltpu.sync_copy(rows_v, out_hbm.at[pl.ds(base, SC_CHUNK)])

    return gather(table, idx)


def _route(top_i, rank, counts, T, bm):
    A = TOP_K * T
    counts = counts.astype(jnp.int32)
    padded = ((counts + bm - 1) // bm) * bm
    p_end = jnp.cumsum(padded)
    p_start = p_end - padded
    dest = _dest(p_start, top_i, rank)
    nblk = A // bm + N_EXPERTS
    blk_start = jnp.arange(nblk, dtype=jnp.int32) * bm
    blk_e = jnp.minimum(jnp.sum(p_end[None, :] <= blk_start[:, None], axis=1), N_EXPERTS - 1).astype(jnp.int32)
    nvalid = (p_end[-1] // bm).astype(jnp.int32).reshape(1)
    return dest, nblk * bm, blk_e, nvalid


def _layer(x, p, W, bm):
    B, S, _ = x.shape
    T = B * S
    x2 = x.reshape(T, D_MODEL)
    p2 = p.reshape(T, PLE_DIM)
    Wl = dict(W)
    Wl["cosq"], Wl["sinq"], Wl["cosk"], Wl["sink"] = _rope_tables(S)

    q, k, v, dn, z, gates, gb = _in_proj(x2, S, Wl)
    attn = _attention(q, k, v, B, S)
    qn, vk, kt, aux = _dn_prep(dn, gb, S, Wl)
    auxt = jnp.swapaxes(aux, 1, 2)
    o_f = _deltanet(qn, vk, kt, aux, auxt, B, S, Wl, 0)
    o_b = _deltanet(qn, vk, kt, aux, auxt, B, S, Wl, 1)
    r, hb, top_i, top_g, rank, cnt = _mix(x2, attn, o_f, o_b, z, gates, p2, Wl)

    dest, P, blk_e, nvalid = _route(top_i, rank, cnt[:, 0], T, bm)
    xs = _sc_dispatch(hb, dest.reshape(TOP_K, T // SC_CHUNK, SC_CHUNK), P)
    yb = _moe(xs, blk_e, nvalid, Wl, bm)
    yg = _sc_gather(yb, dest.reshape(TOP_K * T)).reshape(TOP_K, T, D_MODEL // 2)
    y = _final(r, yg, top_g.T, Wl)
    return y.reshape(B, S, D_MODEL)


def kernel(x_prompt, x_sample, p_prompt, p_sample, w_in, q_a_norm, w_uq, kv_a_norm, w_ukv, w_o_attn, dn_conv, dn_a_log, dn_dt_bias, dn_norm, w_o_dn, w_out, ln1_g, ln1_b, router_w, router_b, w_gate_up, b_gate_up, w_down, b_down, ple_w_proj, ple_w_gate, ln2_g, ln2_b):
    y_prompt, y_sample = x_prompt, x_sample
    for l in range(DEPTH):
        W = _prep_weights(w_in[l], q_a_norm[l], w_uq[l], kv_a_norm[l], w_ukv[l], w_o_attn[l], dn_conv[l],
                          dn_a_log[l], dn_dt_bias[l], dn_norm[l], w_o_dn[l], w_out[l], ln1_g[l], ln1_b[l],
                          router_w[l], router_b[l], w_gate_up[l], b_gate_up[l], w_down[l], b_down[l],
                          ple_w_proj[l], ple_w_gate[l], ln2_g[l], ln2_b[l])
        y_prompt = _layer(y_prompt, p_prompt[l], W, bm=MOE_BLOCK)
        y_sample = _layer(y_sample, p_sample[l], W, bm=MOE_BLOCK)
    return (y_prompt, y_sample)
```

```python
import functools
import math

import numpy as np
import jax
import jax.numpy as jnp
from jax import lax
from jax.experimental import pallas as pl
from jax.experimental.pallas import tpu as pltpu
from jax.experimental.pallas import tpu_sc as plsc

D_MODEL = 1024
MLA_HEADS = 8
Q_LORA = 256
KV_LORA = 128
NOPE_DIM = 64
ROPE_DIM = 32
V_DIM = 64
ROPE_THETA = 10000.0
DN_HEADS = 8
DN_DK = 64
DN_DV = 64
CONV_K = 5
N_EXPERTS = 32
TOP_K = 4
D_FF = 1024
SWIGLU_LIMIT = 7.0
SWIGLU_ALPHA = 1.702
PLE_DIM = 256
DEPTH = 1
DEEPNORM_ALPHA = (2.0 * DEPTH) ** 0.25

LANES = 128
HEAD_PAD = 128
DN_BLOCK = 256
DN_STEP_BLOCKS = 2
DN_PREP_TILE = 256
MOE_BLOCK = 512
MIX_TILE = 512
VMEM_LIMIT = 56 * 1024 * 1024

_C_CQ = 0
_C_CKV = _C_CQ + Q_LORA
_C_MISC0 = _C_CKV + KV_LORA
_C_MISC1 = _C_MISC0 + LANES
_C_DNQ = _C_MISC1 + LANES
_C_DNVK = _C_DNQ + DN_HEADS * DN_DK
_C_Z = _C_DNVK + DN_HEADS * (DN_DK + DN_DV)
_C_GATE = _C_Z + DN_HEADS * DN_DV
_C_END = _C_GATE + 2 * D_MODEL

BF16 = jnp.bfloat16
F32 = jnp.float32


def _dot(a, b):
    return jnp.dot(a, b, preferred_element_type=F32)


def _dot_nt(a, b):
    return lax.dot_general(a, b, (((1,), (1,)), ((), ())), preferred_element_type=F32)


def _split_bf16(x):
    hi = x.astype(BF16)
    lo = (x - hi.astype(F32)).astype(BF16)
    return hi, lo


_HI_HALFWORD = 0xFFFF0000


def _pack_halves(x):
    w = x.shape[1] // 2
    bits = lax.bitcast_convert_type(x.astype(BF16).astype(F32), jnp.uint32)
    return (bits[:, :w] >> 16) | (bits[:, w:] & jnp.uint32(_HI_HALFWORD))


def _unpack_halves(words):
    lo = lax.bitcast_convert_type(words << 16, F32)
    hi = lax.bitcast_convert_type(words & jnp.uint32(_HI_HALFWORD), F32)
    return lo, hi


def _const_spec(shape):
    n = len(shape)
    return pl.BlockSpec(shape, lambda *_: (0,) * n, pipeline_mode=pl.Buffered(1))


def _params(sem):
    return pltpu.CompilerParams(dimension_semantics=sem, vmem_limit_bytes=VMEM_LIMIT)


def _in_proj_kernel(x_ref, w1_ref, qan_ref, kvan_ref, wq_ref, wqs_ref, wk_ref, wv_ref,
                    cosq_ref, sinq_ref, cosk_ref, sink_ref, abp_ref,
                    q_ref, k_ref, v_ref, dn_ref, z_ref, gates_ref, gb_ref):
    xb = x_ref[...].astype(BF16)

    def proj(lo, hi):
        return _dot(xb, w1_ref[:, lo:hi])

    def rms(c, g):
        return (c * lax.rsqrt(jnp.mean(c * c, axis=-1, keepdims=True) + 1e-6) * g).astype(BF16)

    cqn = rms(proj(_C_CQ, _C_CKV), qan_ref[...])
    qa = _dot(cqn, wq_ref[...])
    qb = _dot(cqn, wqs_ref[...])
    ckvn = rms(proj(_C_CKV, _C_MISC0), kvan_ref[...])
    kw = _dot(ckvn, wk_ref[...])
    vw = _dot(ckvn, wv_ref[...])
    misc = (proj(_C_MISC0, _C_MISC1), proj(_C_MISC1, _C_DNQ))
    kr = misc[0] * cosk_ref[...] + misc[1] * sink_ref[...]
    cosq = cosq_ref[...]
    sinq = sinq_ref[...]
    lane = lax.broadcasted_iota(jnp.int32, (1, HEAD_PAD), 1)
    ones_col = (lane == V_DIM).astype(F32)
    for h in range(MLA_HEADS):
        sl = slice(h * HEAD_PAD, (h + 1) * HEAD_PAD)
        q_ref[:, sl] = (qa[:, sl] * cosq + qb[:, sl] * sinq).astype(BF16)
        k_ref[:, sl] = (kw[:, sl] + kr).astype(BF16)
        v_ref[:, sl] = (vw[:, sl] + ones_col).astype(BF16)

    for d in range(2):
        ab = misc[d]
        neg_a = abp_ref[2 * d:2 * d + 1, :]
        dtb = abp_ref[2 * d + 1:2 * d + 2, :]
        t = ab + dtb
        sp = jnp.maximum(t, 0.0) + jnp.log(1.0 + jnp.exp(-jnp.abs(t)))
        g = neg_a * sp
        beta = 1.0 / (1.0 + jnp.exp(-ab))
        gb_ref[d] = jnp.where(lane < DN_HEADS, g, beta)[:, :2 * DN_HEADS]

    dn_ref[...] = proj(_C_DNQ, _C_Z).astype(BF16)
    z_ref[...] = proj(_C_Z, _C_GATE).astype(BF16)
    gl = proj(_C_GATE, _C_END)
    gates_ref[...] = (1.0 / (1.0 + jnp.exp(-gl))).astype(BF16)


def _in_proj(x2, S, W, tm=512):
    T = x2.shape[0]
    nseq = S // tm
    row = lambda i: (i, 0)
    pos = lambda i: (i % nseq, 0)
    dn_w = _C_Z - _C_DNQ
    return pl.pallas_call(
        _in_proj_kernel,
        grid=(T // tm,),
        in_specs=[
            pl.BlockSpec((tm, D_MODEL), row),
            _const_spec((D_MODEL, _C_END)),
            _const_spec((1, Q_LORA)), _const_spec((1, KV_LORA)),
            _const_spec((Q_LORA, MLA_HEADS * HEAD_PAD)), _const_spec((Q_LORA, MLA_HEADS * HEAD_PAD)),
            _const_spec((KV_LORA, MLA_HEADS * HEAD_PAD)), _const_spec((KV_LORA, MLA_HEADS * HEAD_PAD)),
            pl.BlockSpec((tm, HEAD_PAD), pos), pl.BlockSpec((tm, HEAD_PAD), pos),
            pl.BlockSpec((tm, HEAD_PAD), pos), pl.BlockSpec((tm, HEAD_PAD), pos),
            _const_spec((8, LANES)),
        ],
        out_specs=[
            pl.BlockSpec((tm, MLA_HEADS * HEAD_PAD), row),
            pl.BlockSpec((tm, MLA_HEADS * HEAD_PAD), row),
            pl.BlockSpec((tm, MLA_HEADS * HEAD_PAD), row),
            pl.BlockSpec((tm, dn_w), row),
            pl.BlockSpec((tm, DN_HEADS * DN_DV), row),
            pl.BlockSpec((tm, 2 * D_MODEL), row),
            pl.BlockSpec((2, tm, 2 * DN_HEADS), lambda i: (0, i, 0)),
        ],
        out_shape=[
            jax.ShapeDtypeStruct((T, MLA_HEADS * HEAD_PAD), BF16),
            jax.ShapeDtypeStruct((T, MLA_HEADS * HEAD_PAD), BF16),
            jax.ShapeDtypeStruct((T, MLA_HEADS * HEAD_PAD), BF16),
            jax.ShapeDtypeStruct((T, dn_w), BF16),
            jax.ShapeDtypeStruct((T, DN_HEADS * DN_DV), BF16),
            jax.ShapeDtypeStruct((T, 2 * D_MODEL), BF16),
            jax.ShapeDtypeStruct((2, T, 2 * DN_HEADS), F32),
        ],
        compiler_params=_params(("parallel",)),
        name="in_proj",
    )(x2, W["w1"], W["qan"], W["kvan"], W["wq"], W["wqs"], W["wk"], W["wv"],
      W["cosq"], W["sinq"], W["cosk"], W["sink"], W["abp"])


def _attn_kernel(q_ref, k_ref, v_ref, o_ref, *, tk, unroll):
    tq = q_ref.shape[0]
    S = k_ref.shape[0]
    outs = []
    for hh in range(2):
        sl = slice(hh * HEAD_PAD, (hh + 1) * HEAD_PAD)
        q = q_ref[:, sl]

        def body(j, carry, q=q, sl=sl):
            m, acc = carry
            off = pl.multiple_of(j * tk, tk)
            s = _dot_nt(q, k_ref[pl.ds(off, tk), sl])
            m_new = jnp.maximum(m, jnp.max(s, axis=-1, keepdims=True))
            p = jnp.exp2(s - m_new).astype(BF16)
            acc = acc * jnp.exp2(m - m_new) + _dot(p, v_ref[pl.ds(off, tk), sl])
            return m_new, acc

        m0 = jnp.full((tq, 1), -1e30, F32)
        acc0 = jnp.zeros((tq, HEAD_PAD), F32)
        _, acc = lax.fori_loop(0, S // tk, body, (m0, acc0), unroll=unroll)
        outs.append(acc / acc[:, V_DIM:V_DIM + 1])
    lane = lax.broadcasted_iota(jnp.int32, (1, HEAD_PAD), 1)
    o_ref[...] = jnp.where(lane < V_DIM, outs[0], pltpu.roll(outs[1], V_DIM, axis=1)).astype(BF16)


def _attention(q, k, v, B, S, tq=1024, tk=2048, unroll=4):
    T = q.shape[0]
    tq = min(tq, S)
    tk = min(tk, S)
    nq = S // tq
    return pl.pallas_call(
        functools.partial(_attn_kernel, tk=tk, unroll=unroll),
        grid=(B, MLA_HEADS // 2, nq),
        in_specs=[
            pl.BlockSpec((tq, 2 * HEAD_PAD), lambda b, h, i: (b * nq + i, h)),
            pl.BlockSpec((S, 2 * HEAD_PAD), lambda b, h, i: (b, h)),
            pl.BlockSpec((S, 2 * HEAD_PAD), lambda b, h, i: (b, h)),
        ],
        out_specs=pl.BlockSpec((tq, 2 * V_DIM), lambda b, h, i: (b * nq + i, h)),
        out_shape=jax.ShapeDtypeStruct((T, MLA_HEADS * V_DIM), BF16),
        compiler_params=_params(("parallel", "parallel", "arbitrary")),
        name="attention",
    )(q, k, v)


_HALO = 16


def _dn_prep_kernel(x_ref, prev_ref, next_ref, cw_ref, eq_ref, eqt_ref, evk_ref, evkt_ref,
                    gb_ref, tri_ref, shift_ref, qn_ref, vk_ref, kt_ref, aux_ref, *, nseq):
    i = pl.program_id(0)
    tp = x_ref.shape[0]
    first = (i % nseq) == 0
    last = (i % nseq) == nseq - 1
    half = CONV_K // 2
    xb = x_ref[...]

    y = xb.astype(F32) * cw_ref[half:half + 1, :]
    for j in range(CONV_K):
        if j != half:
            y = y + _dot(shift_ref[j], xb) * cw_ref[j:j + 1, :]

    def edge(slab, row0):
        n = slab.shape[0]
        acc = None
        for j in range(CONV_K):
            shift = (half - j) % n
            rolled = slab if shift == 0 else pltpu.roll(slab, shift, axis=0)
            term = rolled[row0:row0 + 8, :] * cw_ref[j:j + 1, :]
            acc = term if acc is None else acc + term
        return acc

    prev = jnp.where(first, 0.0, prev_ref[...].astype(F32))
    nxt = jnp.where(last, 0.0, next_ref[...].astype(F32))
    top = edge(jnp.concatenate([prev, xb[:_HALO].astype(F32)], axis=0), _HALO)
    bot = edge(jnp.concatenate([xb[tp - _HALO:].astype(F32), nxt], axis=0), _HALO - 8)
    y = jnp.concatenate([top, y[8:tp - 8], bot], axis=0)
    y = y * (1.0 / (1.0 + jnp.exp(-y)))

    def group_scale(v, e_ref, et_ref):
        ss = _dot((v * v).astype(BF16), e_ref[...])
        return _dot(lax.rsqrt(ss + 1e-6).astype(BF16), et_ref[...])

    nq = DN_HEADS * DN_DK
    yq = y[:, :nq]
    qn_ref[...] = (yq * group_scale(yq, eq_ref, eqt_ref) * (DN_DK ** -0.5)).astype(BF16)
    yvk = y[:, nq:]
    sc = group_scale(yvk, evk_ref, evkt_ref)
    lane = lax.broadcasted_iota(jnp.int32, (1, yvk.shape[1]), 1)
    is_k = (lane // DN_DV) % 2 == 1
    vk = yvk * jnp.where(is_k, sc, 1.0)
    vk_ref[...] = vk.astype(BF16)
    vkt = vk.T
    for h in range(DN_HEADS):
        r0 = h * (DN_DV + DN_DK) + DN_DV
        kt_ref[h * DN_DK:(h + 1) * DN_DK, :] = vkt[r0:r0 + DN_DK, :].astype(BF16)

    hcol = lax.broadcasted_iota(jnp.int32, (1, 2 * DN_HEADS), 1) < DN_HEADS
    for d in range(2):
        gb = gb_ref[d]
        for blk in range(tp // DN_BLOCK):
            rs = slice(blk * DN_BLOCK, (blk + 1) * DN_BLOCK)
            g = gb[rs]
            g_hi = g.astype(BF16)
            g_mid, g_lo = _split_bf16(g - g_hi.astype(F32))
            cs = _dot(tri_ref[d], g_hi) + _dot(tri_ref[d], g_mid) + _dot(tri_ref[d], g_lo)
            aux_ref[d, rs, :] = jnp.where(hcol, cs, g)


def _dn_prep(dn, gb, S, W):
    tp = DN_PREP_TILE
    T = dn.shape[0]
    nseq = S // tp
    hb = tp // _HALO
    nh = T // _HALO
    dn_w = dn.shape[1]
    nq = DN_HEADS * DN_DK
    return pl.pallas_call(
        functools.partial(_dn_prep_kernel, nseq=nseq),
        grid=(T // tp,),
        in_specs=[
            pl.BlockSpec((tp, dn_w), lambda i: (i, 0)),
            pl.BlockSpec((_HALO, dn_w), lambda i: (jnp.maximum(i * hb - 1, 0), 0)),
            pl.BlockSpec((_HALO, dn_w), lambda i: (jnp.minimum((i + 1) * hb, nh - 1), 0)),
            _const_spec((8, dn_w)),
            _const_spec((nq, LANES)), _const_spec((LANES, nq)),
            _const_spec((dn_w - nq, LANES)), _const_spec((LANES, dn_w - nq)),
            pl.BlockSpec((2, tp, 2 * DN_HEADS), lambda i: (0, i, 0)),
            _const_spec((2, DN_BLOCK, DN_BLOCK)),
            _const_spec((CONV_K, tp, tp)),
        ],
        out_specs=[
            pl.BlockSpec((tp, nq), lambda i: (i, 0)),
            pl.BlockSpec((tp, dn_w - nq), lambda i: (i, 0)),
            pl.BlockSpec((nq, tp), lambda i: (0, i)),
            pl.BlockSpec((2, tp, 2 * DN_HEADS), lambda i: (0, i, 0)),
        ],
        out_shape=[
            jax.ShapeDtypeStruct((T, nq), BF16),
            jax.ShapeDtypeStruct((T, dn_w - nq), BF16),
            jax.ShapeDtypeStruct((nq, T), BF16),
            jax.ShapeDtypeStruct((2, T, 2 * DN_HEADS), F32),
        ],
        compiler_params=_params(("parallel",)),
        name="dn_prep",
    )(dn, dn, dn, W["convw"], W["eq"], W["eqt"], W["evk"], W["evkt"], gb, W["tri"], W["conv_shift"])


_LEAF = 4
_FULL_MERGE_SIZES = (4, 8)
_MERGE_SIZES = (16, 32, 64, 128)


def _active_blocks(d, s):
    return range(1 - d, DN_BLOCK // s, 2)


def _dn_masks(d):
    r = np.arange(DN_BLOCK)[:, None]
    c = np.arange(DN_BLOCK)[None, :]
    rr, cc = (r, c) if d == 0 else (c, r)
    tri = np.stack([rr >= cc, rr > cc]).astype(np.float32)
    def off(s):
        return ((rr // s) % 2 == 1) & ((rr // s) == (cc // s) + 1)

    small = np.stack([(r // _LEAF) == (c // _LEAF)] + [off(s) for s in _FULL_MERGE_SIZES]).astype(np.float32)
    offc = []
    for s in _MERGE_SIZES:
        rows = np.concatenate([np.arange(b * s, (b + 1) * s) for b in _active_blocks(d, s)])
        assert not np.delete(off(s), rows, axis=0).any()
        offc.append(off(s)[rows].astype(np.float32))
    return tri, small, np.stack(offc)


def _deltanet_kernel(qn_ref, vk_ref, kt_ref, aux_ref, auxt_ref, tri_ref, small_ref, offc_ref, o_ref,
                     s_ref, *scratch, d):
    @pl.when(pl.program_id(1) == 0)
    def _():
        s_ref[...] = jnp.zeros_like(s_ref)

    C = DN_BLOCK
    order = range(DN_STEP_BLOCKS) if d == 0 else range(DN_STEP_BLOCKS - 1, -1, -1)
    for sub in order:
        rows = pl.ds(sub * C, C)
        _deltanet_block(qn_ref.at[rows], vk_ref.at[rows], kt_ref.at[:, rows], aux_ref.at[:, rows],
                        auxt_ref.at[:, :, rows], tri_ref, small_ref, offc_ref, o_ref.at[rows],
                        s_ref, *scratch, d=d)


def _deltanet_block(qn_ref, vk_ref, kt_ref, aux_ref, auxt_ref, tri_ref, small_ref, offc_ref, o_ref,
                    s_ref, nm_ref, p_ref, x_ref, aqk_ref, uw_ref, vn_ref, *, d):
    C = DN_BLOCK
    heads = range(DN_HEADS)

    rowi = lax.broadcasted_iota(jnp.int32, (C, C), 0)
    coli = lax.broadcasted_iota(jnp.int32, (C, C), 1)
    eye = (rowi == coli).astype(F32)
    lane = lax.broadcasted_iota(jnp.int32, (1, LANES), 1)
    r64 = lax.broadcasted_iota(jnp.int32, (DN_DK, LANES), 0)
    c64 = lax.broadcasted_iota(jnp.int32, (DN_DK, LANES), 1)
    zeros_kt = jnp.zeros((DN_DK, C), BF16)
    zeros_s = jnp.zeros((DN_DK, LANES), F32)

    def q_pair(h):
        return qn_ref[:, (h // 2) * LANES:(h // 2 + 1) * LANES]

    def vk_head(h):
        return vk_ref[:, h * LANES:(h + 1) * LANES]

    def kt_head(h):
        return kt_ref[h * DN_DK:(h + 1) * DN_DK, :]

    def gc_col(h):
        return aux_ref[0, :, h:h + 1]

    def beta_col(h):
        return aux_ref[0, :, DN_HEADS + h:DN_HEADS + h + 1]

    def gc_row(h):
        return auxt_ref[0, h:h + 1, :]

    for h in heads:
        kt = kt_head(h)
        kt_for_q = jnp.concatenate([kt, zeros_kt] if h % 2 == 0 else [zeros_kt, kt], axis=0)
        kt_for_k = jnp.concatenate([zeros_kt, kt], axis=0)
        qk = _dot(q_pair(h), kt_for_q)
        kk = _dot(vk_head(h), kt_for_k)
        e0 = jnp.exp(jnp.minimum(gc_col(h) - gc_row(h), 0.0))
        aqk_ref[h] = (qk * e0 * tri_ref[0]).astype(BF16)
        nm_ref[h] = (-(kk * e0 * tri_ref[1]) * beta_col(h)).astype(BF16)

    leaf = small_ref[0]
    for h in heads:
        nd = nm_ref[h] * leaf
        p_ref[h] = (eye + nd.astype(F32)).astype(BF16)
        x_ref[h] = _dot(nd, nd).astype(BF16)
    for h in heads:
        pb = p_ref[h]
        p_ref[h] = (pb.astype(F32) + _dot(pb, x_ref[h])).astype(BF16)
    for k in range(len(_FULL_MERGE_SIZES)):
        off_mask = small_ref[1 + k]
        for h in heads:
            x_ref[h] = _dot(nm_ref[h] * off_mask, p_ref[h]).astype(BF16)
        for h in heads:
            pb = p_ref[h]
            p_ref[h] = (pb.astype(F32) + _dot(pb, x_ref[h])).astype(BF16)
    for k, s in enumerate(_MERGE_SIZES):
        blocks = list(_active_blocks(d, s))
        zeros_blk = jnp.zeros((s, C), BF16)

        def active_rows(ref, h):
            return jnp.concatenate([ref[h, b * s:(b + 1) * s, :] for b in blocks], axis=0)

        for h in heads:
            xc = _dot(active_rows(nm_ref, h) * offc_ref[k], p_ref[h]).astype(BF16)
            pieces = []
            for j in range(len(blocks)):
                piece = xc[j * s:(j + 1) * s, :]
                pieces += [zeros_blk, piece] if d == 0 else [piece, zeros_blk]
            x_ref[h] = jnp.concatenate(pieces, axis=0)
        for h in heads:
            pr = active_rows(p_ref, h)
            prn = (pr.astype(F32) + _dot(pr, x_ref[h])).astype(BF16)
            for j, b in enumerate(blocks):
                p_ref[h, b * s:(b + 1) * s, :] = prn[j * s:(j + 1) * s, :]

    for h in heads:
        egc = jnp.exp(gc_col(h))
        rhs = (vk_head(h).astype(F32) * beta_col(h) * jnp.where(lane < DN_DV, 1.0, egc)).astype(BF16)
        uw_ref[h] = _dot(p_ref[h], rhs).astype(BF16)
    for h in heads:
        eye_pl = (c64 == r64 + (h % 2) * DN_DV).astype(F32)
        s_aug = jnp.concatenate([eye_pl, -s_ref[h]], axis=0).astype(BF16)
        vn_ref[h] = _dot(uw_ref[h], s_aug).astype(BF16)
    o_pair = None
    for h in heads:
        par = h % 2
        gc_r = gc_row(h)
        g_tot = gc_r[:, C - 1:C] if d == 0 else gc_r[:, 0:1]
        s_pl = s_ref[h]
        v_new = vn_ref[h]
        s_sel = jnp.concatenate([s_pl, zeros_s] if par == 0 else [zeros_s, s_pl], axis=0).astype(BF16)
        qd = (q_pair(h).astype(F32) * jnp.exp(gc_col(h))).astype(BF16)
        o_pl = _dot(qd, s_sel) + _dot(aqk_ref[h], v_new)
        kd = (kt_head(h).astype(F32) * jnp.exp(g_tot - gc_r)).astype(BF16)
        s_ref[h] = s_pl * jnp.exp(g_tot) + _dot(kd, v_new)
        if par == 0:
            o_pair = o_pl
        else:
            o_ref[:, (h // 2) * LANES:(h // 2 + 1) * LANES] = o_pair + o_pl


def _deltanet(qn, vk, kt, aux, auxt, B, S, W, d):
    T = qn.shape[0]
    C = DN_BLOCK
    R = DN_STEP_BLOCKS * C
    nb = S // R

    def blk(b, i):
        return b * nb + (i if d == 0 else nb - 1 - i)

    nq = DN_HEADS * DN_DK
    tri, small, offc = W["dn_masks"][d]
    return pl.pallas_call(
        functools.partial(_deltanet_kernel, d=d),
        grid=(B, nb),
        in_specs=[
            pl.BlockSpec((R, nq), lambda b, i: (blk(b, i), 0)),
            pl.BlockSpec((R, 2 * nq), lambda b, i: (blk(b, i), 0)),
            pl.BlockSpec((nq, R), lambda b, i: (0, blk(b, i))),
            pl.BlockSpec((1, R, 2 * DN_HEADS), lambda b, i: (d, blk(b, i), 0)),
            pl.BlockSpec((1, 2 * DN_HEADS, R), lambda b, i: (d, 0, blk(b, i))),
            _const_spec((2, C, C)), _const_spec((1 + len(_FULL_MERGE_SIZES), C, C)),
            _const_spec((len(_MERGE_SIZES), C // 2, C)),
        ],
        out_specs=pl.BlockSpec((R, DN_HEADS * DN_DV), lambda b, i: (blk(b, i), 0)),
        out_shape=jax.ShapeDtypeStruct((T, DN_HEADS * DN_DV), F32),
        scratch_shapes=[pltpu.VMEM((DN_HEADS, DN_DK, LANES), F32)]
        + [pltpu.VMEM((DN_HEADS, C, C), BF16)] * 4
        + [pltpu.VMEM((DN_HEADS, C, LANES), BF16)] * 2,
        compiler_params=_params(("parallel", "arbitrary")),
        name="deltanet_fwd" if d == 0 else "deltanet_bwd",
    )(qn, vk, kt, aux, auxt, tri, small, offc)


def _layer_norm(v, g, b):
    mu = jnp.mean(v, axis=-1, keepdims=True)
    c = v - mu
    var = jnp.mean(c * c, axis=-1, keepdims=True)
    return c * lax.rsqrt(var + 1e-5) * g + b


def _mix_kernel(x_ref, attn_ref, of_ref, ob_ref, z_ref, gates_ref, p_ref,
                woa_ref, wod_ref, wout_ref, e8_ref, e8t_ref, dnorm_ref, ln1g_ref, ln1b_ref,
                wpg_ref, wpp_ref, rwh_ref, rwl_ref, rb_ref, ustrict_ref,
                r_ref, hb_ref, ti_ref, tg_ref, rank_ref, cnt_ref, run_ref):
    @pl.when(pl.program_id(0) == 0)
    def _():
        run_ref[...] = jnp.zeros_like(run_ref)

    oa = _dot(attn_ref[...], woa_ref[...])
    o = of_ref[...] + ob_ref[...]
    hi, lo = _split_bf16(o * o)
    ms = (_dot(hi, e8_ref[...]) + _dot(lo, e8_ref[...])) * (1.0 / DN_DV)
    ih, il = _split_bf16(lax.rsqrt(ms + 1e-6))
    sc = _dot(ih, e8t_ref[...]) + _dot(il, e8t_ref[...])
    zf = z_ref[...].astype(F32)
    od_in = o * sc * dnorm_ref[...] * (zf * (1.0 / (1.0 + jnp.exp(-zf))))
    od = _dot(od_in.astype(BF16), wod_ref[...])
    mix = gates_ref[:, :D_MODEL].astype(F32) * oa + gates_ref[:, D_MODEL:].astype(F32) * od
    mo = _dot(mix.astype(BF16), wout_ref[...])
    h = _layer_norm(DEEPNORM_ALPHA * x_ref[...] + mo, ln1g_ref[...], ln1b_ref[...])
    hb = h.astype(BF16)
    hb_ref[...] = _pack_halves(h)
    pg = _dot(hb, wpg_ref[...])
    pp = _dot(p_ref[...].astype(BF16), wpp_ref[...])
    r_ref[...] = DEEPNORM_ALPHA * h + pp * (1.0 / (1.0 + jnp.exp(-pg)))

    hl = (h - hb.astype(F32)).astype(BF16)
    logits = (_dot_nt(rwh_ref[...], hb) + _dot_nt(rwh_ref[...], hl)
              + _dot_nt(rwl_ref[...], hb) + rb_ref[...])
    eid = lax.broadcasted_iota(jnp.int32, logits.shape, 0)
    vals = []
    run = run_ref[:, 0:1]
    for k in range(TOP_K):
        m = jnp.max(logits, axis=0, keepdims=True)
        idx = jnp.min(jnp.where(logits == m, eid, N_EXPERTS), axis=0, keepdims=True)
        ti_ref[k:k + 1, :] = idx
        vals.append(m)
        hit = eid == idx
        logits = jnp.where(hit, -jnp.inf, logits)
        onehot = hit.astype(F32)
        earlier = _dot(onehot.astype(BF16), ustrict_ref[...])
        rank = jnp.sum(onehot * (run + earlier), axis=0, keepdims=True)
        rank_ref[k:k + 1, :] = rank.astype(jnp.int32)
        run = run + jnp.sum(onehot, axis=1, keepdims=True)
    run_ref[...] = jnp.broadcast_to(run, run_ref.shape)
    cnt_ref[...] = jnp.broadcast_to(run, cnt_ref.shape)
    es = [jnp.exp(v - vals[0]) for v in vals]
    den = es[0] + es[1] + es[2] + es[3]
    for k in range(TOP_K):
        tg_ref[k:k + 1, :] = es[k] / den


def _mix(x2, attn, o_f, o_b, z, gates, p2, W):
    tm = MIX_TILE
    T = x2.shape[0]
    row = lambda i: (i, 0)
    nd = DN_HEADS * DN_DV
    return pl.pallas_call(
        _mix_kernel,
        grid=(T // tm,),
        in_specs=[
            pl.BlockSpec((tm, D_MODEL), row),
            pl.BlockSpec((tm, MLA_HEADS * V_DIM), row),
            pl.BlockSpec((tm, nd), row),
            pl.BlockSpec((tm, nd), row),
            pl.BlockSpec((tm, nd), row),
            pl.BlockSpec((tm, 2 * D_MODEL), row),
            pl.BlockSpec((tm, PLE_DIM), row),
            _const_spec((MLA_HEADS * V_DIM, D_MODEL)), _const_spec((nd, D_MODEL)),
            _const_spec((D_MODEL, D_MODEL)),
            _const_spec((nd, LANES)), _const_spec((LANES, nd)), _const_spec((1, nd)),
            _const_spec((1, D_MODEL)), _const_spec((1, D_MODEL)),
            _const_spec((D_MODEL, D_MODEL)), _const_spec((PLE_DIM, D_MODEL)),
            _const_spec((N_EXPERTS, D_MODEL)), _const_spec((N_EXPERTS, D_MODEL)),
            _const_spec((N_EXPERTS, 1)),
            _const_spec((tm, tm)),
        ],
        out_specs=[
            pl.BlockSpec((tm, D_MODEL), row),
            pl.BlockSpec((tm, D_MODEL // 2), row),
            pl.BlockSpec((TOP_K, tm), lambda i: (0, i)),
            pl.BlockSpec((TOP_K, tm), lambda i: (0, i)),
            pl.BlockSpec((TOP_K, tm), lambda i: (0, i)),
            _const_spec((N_EXPERTS, LANES)),
        ],
        out_shape=[
            jax.ShapeDtypeStruct((T, D_MODEL), F32),
            jax.ShapeDtypeStruct((T, D_MODEL // 2), jnp.uint32),
            jax.ShapeDtypeStruct((TOP_K, T), jnp.int32),
            jax.ShapeDtypeStruct((TOP_K, T), F32),
            jax.ShapeDtypeStruct((TOP_K, T), jnp.int32),
            jax.ShapeDtypeStruct((N_EXPERTS, LANES), F32),
        ],
        scratch_shapes=[pltpu.VMEM((N_EXPERTS, LANES), F32)],
        compiler_params=_params(("arbitrary",)),
        name="mix",
    )(x2, attn, o_f, o_b, z, gates, p2, W["woa"], W["wod"], W["wout"], W["e8"], W["e8t"], W["dnorm"],
      W["ln1g"], W["ln1b"], W["wpg"], W["wpp"], W["rwh"], W["rwl"], W["rb"], W["ustrict"])


_CAST_ROWS = 256


def _moe_kernel(blk_e_ref, nvalid_ref, xs_ref, wgu32_ref, bgu_ref, wd32_ref, bd_ref, y_ref,
                wgu_ref, wd_ref, *, fc):
    i = pl.program_id(0)
    valid = i < nvalid_ref[0]
    new_expert = (i == 0) | (blk_e_ref[i] != blk_e_ref[jnp.maximum(i - 1, 0)])

    @pl.when(valid & new_expert)
    def _():
        for r in range(0, D_MODEL, _CAST_ROWS):
            wgu_ref[r:r + _CAST_ROWS, :] = wgu32_ref[0, r:r + _CAST_ROWS, :].astype(BF16)
        for r in range(0, D_FF, _CAST_ROWS):
            wd_ref[r:r + _CAST_ROWS, :] = wd32_ref[0, r:r + _CAST_ROWS, :].astype(BF16)

    @pl.when(valid)
    def _():
        xs = jnp.concatenate(_unpack_halves(xs_ref[...]), axis=1).astype(BF16)
        acc = None
        for c in range(D_FF // fc):
            lo, hi = c * fc, (c + 1) * fc
            gate = _dot(xs, wgu_ref[:, lo:hi]) + bgu_ref[0, :, lo:hi]
            up = _dot(xs, wgu_ref[:, D_FF + lo:D_FF + hi]) + bgu_ref[0, :, D_FF + lo:D_FF + hi]
            gate = jnp.minimum(gate, SWIGLU_LIMIT)
            up = jnp.clip(up, -SWIGLU_LIMIT, SWIGLU_LIMIT)
            act = gate * (1.0 / (1.0 + jnp.exp(-SWIGLU_ALPHA * gate))) * (up + 1.0)
            part = _dot(act.astype(BF16), wd_ref[lo:hi, :])
            acc = part if acc is None else acc + part
        y_ref[...] = _pack_halves(acc + bd_ref[0])

    @pl.when(jnp.logical_not(valid))
    def _():
        y_ref[...] = jnp.zeros_like(y_ref)


def _moe(xs, blk_e, nvalid, W, bm, fc=512):
    P = xs.shape[0]
    grid_spec = pltpu.PrefetchScalarGridSpec(
        num_scalar_prefetch=2,
        grid=(P // bm,),
        in_specs=[
            pl.BlockSpec((bm, D_MODEL // 2), lambda i, be, nv: (i, 0)),
            pl.BlockSpec((1, D_MODEL, 2 * D_FF), lambda i, be, nv: (be[i], 0, 0)),
            pl.BlockSpec((1, 1, 2 * D_FF), lambda i, be, nv: (be[i], 0, 0)),
            pl.BlockSpec((1, D_FF, D_MODEL), lambda i, be, nv: (be[i], 0, 0)),
            pl.BlockSpec((1, 1, D_MODEL), lambda i, be, nv: (be[i], 0, 0)),
        ],
        out_specs=pl.BlockSpec((bm, D_MODEL // 2), lambda i, be, nv: (i, 0)),
        scratch_shapes=[pltpu.VMEM((D_MODEL, 2 * D_FF), BF16), pltpu.VMEM((D_FF, D_MODEL), BF16)],
    )
    return pl.pallas_call(
        functools.partial(_moe_kernel, fc=fc),
        grid_spec=grid_spec,
        out_shape=jax.ShapeDtypeStruct((P, D_MODEL // 2), jnp.uint32),
        compiler_params=_params(("arbitrary",)),
        name="moe",
    )(blk_e, nvalid, xs, W["wgu"], W["bgu"], W["wd"], W["bd"])


def _final_kernel(r_ref, yg_ref, tg_ref, g_ref, b_ref, y_ref):
    ffn_lo = ffn_hi = None
    for k in range(TOP_K):
        lo, hi = _unpack_halves(yg_ref[k])
        g = tg_ref[:, k:k + 1]
        ffn_lo = lo * g if ffn_lo is None else ffn_lo + lo * g
        ffn_hi = hi * g if ffn_hi is None else ffn_hi + hi * g
    acc = r_ref[...] + jnp.concatenate([ffn_lo, ffn_hi], axis=1)
    y_ref[...] = _layer_norm(acc, g_ref[...], b_ref[...])


def _final(r, yg, tg, W, tm=512):
    T = r.shape[0]
    tm = min(tm, T)
    row = lambda i: (i, 0)
    return pl.pallas_call(
        _final_kernel,
        grid=(T // tm,),
        in_specs=[pl.BlockSpec((tm, D_MODEL), row),
                  pl.BlockSpec((TOP_K, tm, D_MODEL // 2), lambda i: (0, i, 0)),
                  pl.BlockSpec((tm, TOP_K), row),
                  _const_spec((1, D_MODEL)), _const_spec((1, D_MODEL))],
        out_specs=pl.BlockSpec((tm, D_MODEL), row),
        out_shape=jax.ShapeDtypeStruct((T, D_MODEL), F32),
        compiler_params=_params(("parallel",)),
        name="final_ln",
    )(r, yg, tg, W["ln2g"], W["ln2b"])


def _pad_heads(w, n_heads, width, start, size, dst=0):
    K = w.shape[0]
    w3 = w.reshape(K, n_heads, width)[:, :, start:start + size]
    out = jnp.zeros((K, n_heads, HEAD_PAD), w.dtype)
    out = out.at[:, :, dst:dst + size].set(w3)
    return out.reshape(K, n_heads * HEAD_PAD)


def _prep_weights(w_in, q_a_norm, w_uq, kv_a_norm, w_ukv, w_o_attn, dn_conv, dn_a_log, dn_dt_bias,
                  dn_norm, w_o_dn, w_out, ln1_g, ln1_b, router_w, router_b, w_gate_up, b_gate_up,
                  w_down, b_down, ple_w_proj, ple_w_gate, ln2_g, ln2_b):
    W = {}
    half = ROPE_DIM // 2
    o = 0
    cq = w_in[:, o:o + Q_LORA]; o += Q_LORA
    ckv = w_in[:, o:o + KV_LORA]; o += KV_LORA
    kr = w_in[:, o:o + ROPE_DIM]; o += ROPE_DIM
    nqk = DN_HEADS * DN_DK
    dq = w_in[:, o:o + nqk]; o += nqk
    dk = w_in[:, o:o + nqk]; o += nqk
    dv = w_in[:, o:o + DN_HEADS * DN_DV]; o += DN_HEADS * DN_DV
    dz = w_in[:, o:o + DN_HEADS * DN_DV]; o += DN_HEADS * DN_DV
    da = w_in[:, o:o + 2 * DN_HEADS]; o += 2 * DN_HEADS
    db = w_in[:, o:o + 2 * DN_HEADS]; o += 2 * DN_HEADS
    gate = w_in[:, o:o + 2 * D_MODEL]

    def lane_block(parts):
        w = jnp.concatenate(parts, axis=1)
        return jnp.pad(w, ((0, 0), (0, LANES - w.shape[1])))

    def misc_block(d, rope_cols):
        ab = [da[:, d * DN_HEADS:(d + 1) * DN_HEADS], db[:, d * DN_HEADS:(d + 1) * DN_HEADS]]
        return lane_block(ab + [jnp.zeros((D_MODEL, NOPE_DIM - 2 * DN_HEADS), F32), rope_cols])

    misc0 = misc_block(0, kr)
    misc1 = misc_block(1, jnp.concatenate([kr[:, half:], kr[:, :half]], axis=1))

    def interleave_vk(v, k):
        lead = v.shape[:-1]
        v3 = v.reshape(lead + (DN_HEADS, DN_DV))
        k3 = k.reshape(lead + (DN_HEADS, DN_DK))
        return jnp.concatenate([v3, k3], axis=-1).reshape(lead + (DN_HEADS * (DN_DV + DN_DK),))

    W["w1"] = jnp.concatenate([cq, ckv, misc0, misc1, dq, interleave_vk(dv, dk), dz, gate],
                              axis=1).astype(BF16)
    W["qan"] = q_a_norm.reshape(1, Q_LORA)
    W["kvan"] = kv_a_norm.reshape(1, KV_LORA)

    qw = NOPE_DIM + ROPE_DIM
    wq_nope = _pad_heads(w_uq, MLA_HEADS, qw, 0, NOPE_DIM, 0)
    wq_r1 = _pad_heads(w_uq, MLA_HEADS, qw, NOPE_DIM, half, NOPE_DIM)
    wq_r2 = _pad_heads(w_uq, MLA_HEADS, qw, NOPE_DIM + half, half, NOPE_DIM + half)
    W["wq"] = (wq_nope + wq_r1 + wq_r2).astype(BF16)
    wq_s1 = _pad_heads(w_uq, MLA_HEADS, qw, NOPE_DIM + half, half, NOPE_DIM)
    wq_s2 = _pad_heads(w_uq, MLA_HEADS, qw, NOPE_DIM, half, NOPE_DIM + half)
    W["wqs"] = (wq_s1 + wq_s2).astype(BF16)
    kvw = NOPE_DIM + V_DIM
    W["wk"] = _pad_heads(w_ukv, MLA_HEADS, kvw, 0, NOPE_DIM, 0).astype(BF16)
    W["wv"] = _pad_heads(w_ukv, MLA_HEADS, kvw, NOPE_DIM, V_DIM, 0).astype(BF16)

    neg_a = -jnp.exp(dn_a_log.astype(F32))
    abp = jnp.zeros((8, LANES), F32)
    for d in range(2):
        abp = abp.at[2 * d, :DN_HEADS].set(neg_a[d])
        abp = abp.at[2 * d + 1, :DN_HEADS].set(dn_dt_bias[d].astype(F32))
    W["abp"] = abp

    cw = jnp.concatenate([dn_conv[:, :nqk], interleave_vk(dn_conv[:, 2 * nqk:], dn_conv[:, nqk:2 * nqk])], axis=1)
    W["convw"] = jnp.pad(cw.astype(F32), ((0, 8 - CONV_K), (0, 0)))

    def group_indicator(width, group):
        e = (np.arange(width)[:, None] // group == np.arange(LANES)[None, :]).astype(np.float32)
        return e

    eq = group_indicator(nqk, DN_DK)
    W["eq"] = jnp.asarray(eq, BF16)
    W["eqt"] = jnp.asarray(eq.T, BF16)
    evk = group_indicator(2 * nqk, DN_DK)
    W["evk"] = jnp.asarray(evk, BF16)
    W["evkt"] = jnp.asarray(evk.T, BF16)
    W["e8"] = W["eq"]
    W["e8t"] = W["eqt"]
    r = np.arange(DN_BLOCK)
    W["tri"] = jnp.asarray(np.stack([r[:, None] >= r[None, :], r[:, None] <= r[None, :]]).astype(np.float32), BF16)
    rp = np.arange(DN_PREP_TILE)
    W["conv_shift"] = jnp.asarray(
        np.stack([rp[None, :] == rp[:, None] + (j - CONV_K // 2) for j in range(CONV_K)]).astype(np.float32), BF16)
    W["dn_masks"] = [(jnp.asarray(t), jnp.asarray(g, BF16), jnp.asarray(o, BF16))
                     for t, g, o in (_dn_masks(0), _dn_masks(1))]
    rt = np.arange(MIX_TILE)
    W["ustrict"] = jnp.asarray((rt[:, None] < rt[None, :]).astype(np.float32), BF16)

    W["woa"] = w_o_attn.astype(BF16)
    W["wod"] = w_o_dn.astype(BF16)
    W["wout"] = w_out.astype(BF16)
    W["dnorm"] = jnp.tile(dn_norm.astype(F32), DN_HEADS).reshape(1, DN_HEADS * DN_DV)
    W["ln1g"] = ln1_g.reshape(1, D_MODEL)
    W["ln1b"] = ln1_b.reshape(1, D_MODEL)
    W["ln2g"] = ln2_g.reshape(1, D_MODEL)
    W["ln2b"] = ln2_b.reshape(1, D_MODEL)
    W["wpg"] = ple_w_gate.astype(BF16)
    W["wpp"] = ple_w_proj.astype(BF16)
    rwt = router_w.T.astype(F32)
    W["rwh"], W["rwl"] = _split_bf16(rwt)
    W["rb"] = router_b.reshape(N_EXPERTS, 1).astype(F32)
    W["wgu"] = w_gate_up.astype(F32)
    W["bgu"] = b_gate_up.reshape(N_EXPERTS, 1, 2 * D_FF).astype(F32)
    W["wd"] = w_down.astype(F32)
    W["bd"] = b_down.reshape(N_EXPERTS, 1, D_MODEL).astype(F32)
    return W


def _rope_tables(S):
    half = ROPE_DIM // 2
    inv = ROPE_THETA ** (-jnp.arange(0, ROPE_DIM, 2, dtype=F32) / ROPE_DIM)
    ang = jnp.arange(S, dtype=F32)[:, None] * inv[None, :]
    cos, sin = jnp.cos(ang), jnp.sin(ang)
    c = (NOPE_DIM + ROPE_DIM) ** -0.5 * math.log2(math.e)
    pad = jnp.zeros((S, HEAD_PAD - NOPE_DIM - ROPE_DIM), F32)
    cos_blk = jnp.concatenate([cos, cos, pad], axis=1)
    sin_blk = jnp.concatenate([-sin, sin, pad], axis=1)
    cosq = jnp.concatenate([jnp.ones((S, NOPE_DIM), F32), cos_blk], axis=1) * c
    sinq = jnp.concatenate([jnp.zeros((S, NOPE_DIM), F32), sin_blk], axis=1) * c
    cosk = jnp.concatenate([jnp.zeros((S, NOPE_DIM), F32), cos_blk], axis=1)
    sink = jnp.concatenate([jnp.zeros((S, NOPE_DIM), F32), sin_blk], axis=1)
    return cosq, sinq, cosk, sink


def _dest_kernel(pstart_ref, ti_ref, rank_ref, dest_ref):
    ti = ti_ref[...]
    dest = rank_ref[...]
    for e in range(N_EXPERTS):
        dest = dest + jnp.where(ti == e, pstart_ref[e], 0)
    dest_ref[...] = dest


def _dest(p_start, top_i, rank, tile=8192):
    T = top_i.shape[1]
    tile = min(tile, T)
    spec = pl.BlockSpec((TOP_K, tile), lambda i, ps: (0, i))
    return pl.pallas_call(
        _dest_kernel,
        grid_spec=pltpu.PrefetchScalarGridSpec(num_scalar_prefetch=1, grid=(T // tile,),
                                               in_specs=[spec, spec], out_specs=spec),
        out_shape=jax.ShapeDtypeStruct((TOP_K, T), jnp.int32),
        compiler_params=_params(("parallel",)),
        name="slot_index",
    )(p_start, top_i, rank)


SC_CHUNK = 128


def _sc_mesh():
    info = plsc.get_sparse_core_info()
    mesh = plsc.VectorSubcoreMesh(core_axis_name="c", subcore_axis_name="s")
    return mesh, info.num_cores, info.num_cores * info.num_subcores


def _sc_dispatch(rows, dest3, P):
    T, D = rows.shape
    K = dest3.shape[0]
    mesh, n_cores, n_workers = _sc_mesh()
    n_chunks = T // (n_workers * SC_CHUNK)

    @functools.partial(
        pl.kernel, mesh=mesh, out_type=jax.ShapeDtypeStruct((P, D), rows.dtype),
        scratch_types=[pltpu.VMEM((K, n_chunks, SC_CHUNK), jnp.int32), pltpu.VMEM((SC_CHUNK, D), rows.dtype),
                       pltpu.SemaphoreType.DMA])
    def dispatch(rows_hbm, dest_hbm, out_hbm, idx_v, rows_v, sem):
        worker = lax.axis_index("s") * n_cores + lax.axis_index("c")
        first = pl.multiple_of(worker * n_chunks, n_chunks)
        pltpu.sync_copy(dest_hbm.at[:, pl.ds(first, n_chunks)], idx_v)

        @pl.loop(0, n_chunks)
        def _(j):
            base = pl.multiple_of((first + j) * SC_CHUNK, SC_CHUNK)
            pltpu.sync_copy(rows_hbm.at[pl.ds(base, SC_CHUNK)], rows_v)
            for k in range(K):
                pltpu.async_copy(rows_v, out_hbm.at[idx_v.at[k, j]], sem).wait()

    return dispatch(rows, dest3)


def _sc_gather(table, idx2):
    M = idx2.shape[0] * SC_CHUNK
    D = table.shape[1]
    mesh, n_cores, n_workers = _sc_mesh()
    n_chunks = M // (n_workers * SC_CHUNK)

    @functools.partial(
        pl.kernel, mesh=mesh, out_type=jax.ShapeDtypeStruct((M, D), table.dtype),
        scratch_types=[pltpu.VMEM((n_chunks, SC_CHUNK), jnp.int32), pltpu.VMEM((SC_CHUNK, D), table.dtype),
                       pltpu.SemaphoreType.DMA])
    def gather(table_hbm, idx_hbm, out_hbm, idx_v, rows_v, sem):
        worker = lax.axis_index("s") * n_cores + lax.axis_index("c")
        first = pl.multiple_of(worker * n_chunks, n_chunks)
        pltpu.sync_copy(idx_hbm.at[pl.ds(first, n_chunks)], idx_v)

        @pl.loop(0, n_chunks)
        def _(j):
            base = pl.multiple_of((first + j) * SC_CHUNK, SC_CHUNK)
            pltpu.async_copy(table_hbm.at[idx_v.at[j]], rows_v, sem).wait()
            pltpu.sync_copy(rows_v, out_hbm.at[pl.ds(base, SC_CHUNK)])

    return gather(table, idx2)


def _route(top_i, rank, counts, T, bm):
    A = TOP_K * T
    counts = counts.astype(jnp.int32)
    padded = ((counts + bm - 1) // bm) * bm
    p_end = jnp.cumsum(padded)
    p_start = p_end - padded
    dest = _dest(p_start, top_i, rank)
    nblk = A // bm + N_EXPERTS
    blk_start = jnp.arange(nblk, dtype=jnp.int32) * bm
    blk_e = jnp.minimum(jnp.sum(p_end[None, :] <= blk_start[:, None], axis=1), N_EXPERTS - 1).astype(jnp.int32)
    nvalid = (p_end[-1] // bm).astype(jnp.int32).reshape(1)
    return dest, nblk * bm, blk_e, nvalid


def _layer(x, p, W, bm):
    B, S, _ = x.shape
    T = B * S
    x2 = x.reshape(T, D_MODEL)
    p2 = p.reshape(T, PLE_DIM)
    Wl = dict(W)
    Wl["cosq"], Wl["sinq"], Wl["cosk"], Wl["sink"] = _rope_tables(S)

    q, k, v, dn, z, gates, gb = _in_proj(x2, S, Wl)
    attn = _attention(q, k, v, B, S)
    qn, vk, kt, aux = _dn_prep(dn, gb, S, Wl)
    auxt = jnp.swapaxes(aux, 1, 2)
    o_f = _deltanet(qn, vk, kt, aux, auxt, B, S, Wl, 0)
    o_b = _deltanet(qn, vk, kt, aux, auxt, B, S, Wl, 1)
    r, hb, top_i, top_g, rank, cnt = _mix(x2, attn, o_f, o_b, z, gates, p2, Wl)

    dest, P, blk_e, nvalid = _route(top_i, rank, cnt[:, 0], T, bm)
    xs = _sc_dispatch(hb, dest.reshape(TOP_K, T // SC_CHUNK, SC_CHUNK), P)
    yb = _moe(xs, blk_e, nvalid, Wl, bm)
    yg = _sc_gather(yb, dest.reshape(TOP_K * T // SC_CHUNK, SC_CHUNK)).reshape(TOP_K, T, D_MODEL // 2)
    y = _final(r, yg, top_g.T, Wl)
    return y.reshape(B, S, D_MODEL)


def kernel(x_prompt, x_sample, p_prompt, p_sample, w_in, q_a_norm, w_uq, kv_a_norm, w_ukv, w_o_attn, dn_conv, dn_a_log, dn_dt_bias, dn_norm, w_o_dn, w_out, ln1_g, ln1_b, router_w, router_b, w_gate_up, b_gate_up, w_down, b_down, ple_w_proj, ple_w_gate, ln2_g, ln2_b):
    y_prompt, y_sample = x_prompt, x_sample
    for l in range(DEPTH):
        W = _prep_weights(w_in[l], q_a_norm[l], w_uq[l], kv_a_norm[l], w_ukv[l], w_o_attn[l], dn_conv[l],
                          dn_a_log[l], dn_dt_bias[l], dn_norm[l], w_o_dn[l], w_out[l], ln1_g[l], ln1_b[l],
                          router_w[l], router_b[l], w_gate_up[l], b_gate_up[l], w_down[l], b_down[l],
                          ple_w_proj[l], ple_w_gate[l], ln2_g[l], ln2_b[l])
        y_prompt = _layer(y_prompt, p_prompt[l], W, bm=MOE_BLOCK)
        y_sample = _layer(y_sample, p_sample[l], W, bm=MOE_BLOCK)
    return (y_prompt, y_sample)
```

```python
import functools
import math

import numpy as np
import jax
import jax.numpy as jnp
from jax import lax
from jax.experimental import pallas as pl
from jax.experimental.pallas import tpu as pltpu
from jax.experimental.pallas import tpu_sc as plsc

D_MODEL = 1024
MLA_HEADS = 8
Q_LORA = 256
KV_LORA = 128
NOPE_DIM = 64
ROPE_DIM = 32
V_DIM = 64
ROPE_THETA = 10000.0
DN_HEADS = 8
DN_DK = 64
DN_DV = 64
CONV_K = 5
N_EXPERTS = 32
TOP_K = 4
D_FF = 1024
SWIGLU_LIMIT = 7.0
SWIGLU_ALPHA = 1.702
PLE_DIM = 256
DEPTH = 1
DEEPNORM_ALPHA = (2.0 * DEPTH) ** 0.25

LANES = 128
HEAD_PAD = 128
DN_BLOCK = 256
DN_STEP_BLOCKS = 2
DN_PREP_TILE = 256
MOE_BLOCK = 512
MIX_TILE = 512
VMEM_LIMIT = 56 * 1024 * 1024

_C_CQ = 0
_C_CKV = _C_CQ + Q_LORA
_C_MISC0 = _C_CKV + KV_LORA
_C_MISC1 = _C_MISC0 + LANES
_C_DNQ = _C_MISC1 + LANES
_C_DNVK = _C_DNQ + DN_HEADS * DN_DK
_C_Z = _C_DNVK + DN_HEADS * (DN_DK + DN_DV)
_C_GATE = _C_Z + DN_HEADS * DN_DV
_C_END = _C_GATE + 2 * D_MODEL

BF16 = jnp.bfloat16
F32 = jnp.float32


def _dot(a, b):
    return jnp.dot(a, b, preferred_element_type=F32)


def _dot_nt(a, b):
    return lax.dot_general(a, b, (((1,), (1,)), ((), ())), preferred_element_type=F32)


def _split_bf16(x):
    hi = x.astype(BF16)
    lo = (x - hi.astype(F32)).astype(BF16)
    return hi, lo


_HI_HALFWORD = 0xFFFF0000


def _pack_halves(x):
    w = x.shape[1] // 2
    bits = lax.bitcast_convert_type(x.astype(BF16).astype(F32), jnp.uint32)
    return (bits[:, :w] >> 16) | (bits[:, w:] & jnp.uint32(_HI_HALFWORD))


def _unpack_halves(words):
    lo = lax.bitcast_convert_type(words << 16, F32)
    hi = lax.bitcast_convert_type(words & jnp.uint32(_HI_HALFWORD), F32)
    return lo, hi


def _const_spec(shape):
    n = len(shape)
    return pl.BlockSpec(shape, lambda *_: (0,) * n, pipeline_mode=pl.Buffered(1))


def _params(sem):
    return pltpu.CompilerParams(dimension_semantics=sem, vmem_limit_bytes=VMEM_LIMIT)


def _in_proj_kernel(x_ref, w1_ref, qan_ref, kvan_ref, wq_ref, wqs_ref, wk_ref, wv_ref,
                    cosq_ref, sinq_ref, cosk_ref, sink_ref, abp_ref,
                    q_ref, k_ref, v_ref, dn_ref, z_ref, gates_ref, gb_ref):
    xb = x_ref[...].astype(BF16)

    def proj(lo, hi):
        return _dot(xb, w1_ref[:, lo:hi])

    def rms(c, g):
        return (c * lax.rsqrt(jnp.mean(c * c, axis=-1, keepdims=True) + 1e-6) * g).astype(BF16)

    cqn = rms(proj(_C_CQ, _C_CKV), qan_ref[...])
    qa = _dot(cqn, wq_ref[...])
    qb = _dot(cqn, wqs_ref[...])
    ckvn = rms(proj(_C_CKV, _C_MISC0), kvan_ref[...])
    kw = _dot(ckvn, wk_ref[...])
    vw = _dot(ckvn, wv_ref[...])
    misc = (proj(_C_MISC0, _C_MISC1), proj(_C_MISC1, _C_DNQ))
    kr = misc[0] * cosk_ref[...] + misc[1] * sink_ref[...]
    cosq = cosq_ref[...]
    sinq = sinq_ref[...]
    lane = lax.broadcasted_iota(jnp.int32, (1, HEAD_PAD), 1)
    ones_col = (lane == V_DIM).astype(F32)
    for h in range(MLA_HEADS):
        sl = slice(h * HEAD_PAD, (h + 1) * HEAD_PAD)
        q_ref[:, sl] = (qa[:, sl] * cosq + qb[:, sl] * sinq).astype(BF16)
        k_ref[:, sl] = (kw[:, sl] + kr).astype(BF16)
        v_ref[:, sl] = (vw[:, sl] + ones_col).astype(BF16)

    for d in range(2):
        ab = misc[d]
        neg_a = abp_ref[2 * d:2 * d + 1, :]
        dtb = abp_ref[2 * d + 1:2 * d + 2, :]
        t = ab + dtb
        sp = jnp.maximum(t, 0.0) + jnp.log(1.0 + jnp.exp(-jnp.abs(t)))
        g = neg_a * sp
        beta = 1.0 / (1.0 + jnp.exp(-ab))
        gb_ref[d] = jnp.where(lane < DN_HEADS, g, beta)[:, :2 * DN_HEADS]

    dn_ref[...] = proj(_C_DNQ, _C_Z).astype(BF16)
    z_ref[...] = proj(_C_Z, _C_GATE).astype(BF16)
    gl = proj(_C_GATE, _C_END)
    gates_ref[...] = (1.0 / (1.0 + jnp.exp(-gl))).astype(BF16)


def _in_proj(x2, S, W, tm=512):
    T = x2.shape[0]
    nseq = S // tm
    row = lambda i: (i, 0)
    pos = lambda i: (i % nseq, 0)
    dn_w = _C_Z - _C_DNQ
    return pl.pallas_call(
        _in_proj_kernel,
        grid=(T // tm,),
        in_specs=[
            pl.BlockSpec((tm, D_MODEL), row),
            _const_spec((D_MODEL, _C_END)),
            _const_spec((1, Q_LORA)), _const_spec((1, KV_LORA)),
            _const_spec((Q_LORA, MLA_HEADS * HEAD_PAD)), _const_spec((Q_LORA, MLA_HEADS * HEAD_PAD)),
            _const_spec((KV_LORA, MLA_HEADS * HEAD_PAD)), _const_spec((KV_LORA, MLA_HEADS * HEAD_PAD)),
            pl.BlockSpec((tm, HEAD_PAD), pos), pl.BlockSpec((tm, HEAD_PAD), pos),
            pl.BlockSpec((tm, HEAD_PAD), pos), pl.BlockSpec((tm, HEAD_PAD), pos),
            _const_spec((8, LANES)),
        ],
        out_specs=[
            pl.BlockSpec((tm, MLA_HEADS * HEAD_PAD), row),
            pl.BlockSpec((tm, MLA_HEADS * HEAD_PAD), row),
            pl.BlockSpec((tm, MLA_HEADS * HEAD_PAD), row),
            pl.BlockSpec((tm, dn_w), row),
            pl.BlockSpec((tm, DN_HEADS * DN_DV), row),
            pl.BlockSpec((tm, 2 * D_MODEL), row),
            pl.BlockSpec((2, tm, 2 * DN_HEADS), lambda i: (0, i, 0)),
        ],
        out_shape=[
            jax.ShapeDtypeStruct((T, MLA_HEADS * HEAD_PAD), BF16),
            jax.ShapeDtypeStruct((T, MLA_HEADS * HEAD_PAD), BF16),
            jax.ShapeDtypeStruct((T, MLA_HEADS * HEAD_PAD), BF16),
            jax.ShapeDtypeStruct((T, dn_w), BF16),
            jax.ShapeDtypeStruct((T, DN_HEADS * DN_DV), BF16),
            jax.ShapeDtypeStruct((T, 2 * D_MODEL), BF16),
            jax.ShapeDtypeStruct((2, T, 2 * DN_HEADS), F32),
        ],
        compiler_params=_params(("parallel",)),
        name="in_proj",
    )(x2, W["w1"], W["qan"], W["kvan"], W["wq"], W["wqs"], W["wk"], W["wv"],
      W["cosq"], W["sinq"], W["cosk"], W["sink"], W["abp"])


def _attn_kernel(q_ref, k_ref, v_ref, o_ref, *, tk, unroll):
    tq = q_ref.shape[0]
    S = k_ref.shape[0]
    outs = []
    for hh in range(2):
        sl = slice(hh * HEAD_PAD, (hh + 1) * HEAD_PAD)
        q = q_ref[:, sl]

        def body(j, carry, q=q, sl=sl):
            m, acc = carry
            off = pl.multiple_of(j * tk, tk)
            s = _dot_nt(q, k_ref[pl.ds(off, tk), sl])
            m_new = jnp.maximum(m, jnp.max(s, axis=-1, keepdims=True))
            p = jnp.exp2(s - m_new).astype(BF16)
            acc = acc * jnp.exp2(m - m_new) + _dot(p, v_ref[pl.ds(off, tk), sl])
            return m_new, acc

        m0 = jnp.full((tq, 1), -1e30, F32)
        acc0 = jnp.zeros((tq, HEAD_PAD), F32)
        _, acc = lax.fori_loop(0, S // tk, body, (m0, acc0), unroll=unroll)
        outs.append(acc / acc[:, V_DIM:V_DIM + 1])
    lane = lax.broadcasted_iota(jnp.int32, (1, HEAD_PAD), 1)
    o_ref[...] = jnp.where(lane < V_DIM, outs[0], pltpu.roll(outs[1], V_DIM, axis=1)).astype(BF16)


def _attention(q, k, v, B, S, tq=1024, tk=2048, unroll=4):
    T = q.shape[0]
    tq = min(tq, S)
    tk = min(tk, S)
    nq = S // tq
    return pl.pallas_call(
        functools.partial(_attn_kernel, tk=tk, unroll=unroll),
        grid=(B, MLA_HEADS // 2, nq),
        in_specs=[
            pl.BlockSpec((tq, 2 * HEAD_PAD), lambda b, h, i: (b * nq + i, h)),
            pl.BlockSpec((S, 2 * HEAD_PAD), lambda b, h, i: (b, h)),
            pl.BlockSpec((S, 2 * HEAD_PAD), lambda b, h, i: (b, h)),
        ],
        out_specs=pl.BlockSpec((tq, 2 * V_DIM), lambda b, h, i: (b * nq + i, h)),
        out_shape=jax.ShapeDtypeStruct((T, MLA_HEADS * V_DIM), BF16),
        compiler_params=_params(("parallel", "parallel", "arbitrary")),
        name="attention",
    )(q, k, v)


_HALO = 16


def _dn_prep_kernel(x_ref, prev_ref, next_ref, cw_ref, eq_ref, eqt_ref, evk_ref, evkt_ref,
                    gb_ref, tri_ref, shift_ref, qn_ref, vk_ref, kt_ref, aux_ref, *, nseq):
    i = pl.program_id(0)
    tp = x_ref.shape[0]
    first = (i % nseq) == 0
    last = (i % nseq) == nseq - 1
    half = CONV_K // 2
    xb = x_ref[...]

    y = xb.astype(F32) * cw_ref[half:half + 1, :]
    for j in range(CONV_K):
        if j != half:
            y = y + _dot(shift_ref[j], xb) * cw_ref[j:j + 1, :]

    def edge(slab, row0):
        n = slab.shape[0]
        acc = None
        for j in range(CONV_K):
            shift = (half - j) % n
            rolled = slab if shift == 0 else pltpu.roll(slab, shift, axis=0)
            term = rolled[row0:row0 + 8, :] * cw_ref[j:j + 1, :]
            acc = term if acc is None else acc + term
        return acc

    prev = jnp.where(first, 0.0, prev_ref[...].astype(F32))
    nxt = jnp.where(last, 0.0, next_ref[...].astype(F32))
    top = edge(jnp.concatenate([prev, xb[:_HALO].astype(F32)], axis=0), _HALO)
    bot = edge(jnp.concatenate([xb[tp - _HALO:].astype(F32), nxt], axis=0), _HALO - 8)
    y = jnp.concatenate([top, y[8:tp - 8], bot], axis=0)
    y = y * (1.0 / (1.0 + jnp.exp(-y)))

    def group_scale(v, e_ref, et_ref):
        ss = _dot((v * v).astype(BF16), e_ref[...])
        return _dot(lax.rsqrt(ss + 1e-6).astype(BF16), et_ref[...])

    nq = DN_HEADS * DN_DK
    yq = y[:, :nq]
    qn_ref[...] = (yq * group_scale(yq, eq_ref, eqt_ref) * (DN_DK ** -0.5)).astype(BF16)
    yvk = y[:, nq:]
    sc = group_scale(yvk, evk_ref, evkt_ref)
    lane = lax.broadcasted_iota(jnp.int32, (1, yvk.shape[1]), 1)
    is_k = (lane // DN_DV) % 2 == 1
    vk = yvk * jnp.where(is_k, sc, 1.0)
    vk_ref[...] = vk.astype(BF16)
    vkt = vk.T
    for h in range(DN_HEADS):
        r0 = h * (DN_DV + DN_DK) + DN_DV
        kt_ref[h * DN_DK:(h + 1) * DN_DK, :] = vkt[r0:r0 + DN_DK, :].astype(BF16)

    hcol = lax.broadcasted_iota(jnp.int32, (1, 2 * DN_HEADS), 1) < DN_HEADS
    for d in range(2):
        gb = gb_ref[d]
        for blk in range(tp // DN_BLOCK):
            rs = slice(blk * DN_BLOCK, (blk + 1) * DN_BLOCK)
            g = gb[rs]
            g_hi = g.astype(BF16)
            g_mid, g_lo = _split_bf16(g - g_hi.astype(F32))
            cs = _dot(tri_ref[d], g_hi) + _dot(tri_ref[d], g_mid) + _dot(tri_ref[d], g_lo)
            aux_ref[d, rs, :] = jnp.where(hcol, cs, g)


def _dn_prep(dn, gb, S, W):
    tp = DN_PREP_TILE
    T = dn.shape[0]
    nseq = S // tp
    hb = tp // _HALO
    nh = T // _HALO
    dn_w = dn.shape[1]
    nq = DN_HEADS * DN_DK
    return pl.pallas_call(
        functools.partial(_dn_prep_kernel, nseq=nseq),
        grid=(T // tp,),
        in_specs=[
            pl.BlockSpec((tp, dn_w), lambda i: (i, 0)),
            pl.BlockSpec((_HALO, dn_w), lambda i: (jnp.maximum(i * hb - 1, 0), 0)),
            pl.BlockSpec((_HALO, dn_w), lambda i: (jnp.minimum((i + 1) * hb, nh - 1), 0)),
            _const_spec((8, dn_w)),
            _const_spec((nq, LANES)), _const_spec((LANES, nq)),
            _const_spec((dn_w - nq, LANES)), _const_spec((LANES, dn_w - nq)),
            pl.BlockSpec((2, tp, 2 * DN_HEADS), lambda i: (0, i, 0)),
            _const_spec((2, DN_BLOCK, DN_BLOCK)),
            _const_spec((CONV_K, tp, tp)),
        ],
        out_specs=[
            pl.BlockSpec((tp, nq), lambda i: (i, 0)),
            pl.BlockSpec((tp, dn_w - nq), lambda i: (i, 0)),
            pl.BlockSpec((nq, tp), lambda i: (0, i)),
            pl.BlockSpec((2, tp, 2 * DN_HEADS), lambda i: (0, i, 0)),
        ],
        out_shape=[
            jax.ShapeDtypeStruct((T, nq), BF16),
            jax.ShapeDtypeStruct((T, dn_w - nq), BF16),
            jax.ShapeDtypeStruct((nq, T), BF16),
            jax.ShapeDtypeStruct((2, T, 2 * DN_HEADS), F32),
        ],
        compiler_params=_params(("parallel",)),
        name="dn_prep",
    )(dn, dn, dn, W["convw"], W["eq"], W["eqt"], W["evk"], W["evkt"], gb, W["tri"], W["conv_shift"])


_LEAF = 4
_FULL_MERGE_SIZES = (4, 8)
_MERGE_SIZES = (16, 32, 64, 128)


def _active_blocks(d, s):
    return range(1 - d, DN_BLOCK // s, 2)


def _dn_masks(d):
    r = np.arange(DN_BLOCK)[:, None]
    c = np.arange(DN_BLOCK)[None, :]
    rr, cc = (r, c) if d == 0 else (c, r)
    tri = np.stack([rr >= cc, rr > cc]).astype(np.float32)
    def off(s):
        return ((rr // s) % 2 == 1) & ((rr // s) == (cc // s) + 1)

    small = np.stack([(r // _LEAF) == (c // _LEAF)] + [off(s) for s in _FULL_MERGE_SIZES]).astype(np.float32)
    offc = []
    for s in _MERGE_SIZES:
        rows = np.concatenate([np.arange(b * s, (b + 1) * s) for b in _active_blocks(d, s)])
        assert not np.delete(off(s), rows, axis=0).any()
        offc.append(off(s)[rows].astype(np.float32))
    return tri, small, np.stack(offc)


def _deltanet_kernel(qn_ref, vk_ref, kt_ref, aux_ref, auxt_ref, tri_ref, small_ref, offc_ref, o_ref,
                     s_ref, *scratch, d):
    @pl.when(pl.program_id(1) == 0)
    def _():
        s_ref[...] = jnp.zeros_like(s_ref)

    C = DN_BLOCK
    order = range(DN_STEP_BLOCKS) if d == 0 else range(DN_STEP_BLOCKS - 1, -1, -1)
    for sub in order:
        rows = pl.ds(sub * C, C)
        _deltanet_block(qn_ref.at[rows], vk_ref.at[rows], kt_ref.at[:, rows], aux_ref.at[:, rows],
                        auxt_ref.at[:, :, rows], tri_ref, small_ref, offc_ref, o_ref.at[rows],
                        s_ref, *scratch, d=d)


def _deltanet_block(qn_ref, vk_ref, kt_ref, aux_ref, auxt_ref, tri_ref, small_ref, offc_ref, o_ref,
                    s_ref, nm_ref, p_ref, x_ref, aqk_ref, uw_ref, vn_ref, *, d):
    C = DN_BLOCK
    heads = range(DN_HEADS)

    rowi = lax.broadcasted_iota(jnp.int32, (C, C), 0)
    coli = lax.broadcasted_iota(jnp.int32, (C, C), 1)
    eye = (rowi == coli).astype(F32)
    lane = lax.broadcasted_iota(jnp.int32, (1, LANES), 1)
    r64 = lax.broadcasted_iota(jnp.int32, (DN_DK, LANES), 0)
    c64 = lax.broadcasted_iota(jnp.int32, (DN_DK, LANES), 1)
    zeros_kt = jnp.zeros((DN_DK, C), BF16)
    zeros_s = jnp.zeros((DN_DK, LANES), F32)

    def q_pair(h):
        return qn_ref[:, (h // 2) * LANES:(h // 2 + 1) * LANES]

    def vk_head(h):
        return vk_ref[:, h * LANES:(h + 1) * LANES]

    def kt_head(h):
        return kt_ref[h * DN_DK:(h + 1) * DN_DK, :]

    def gc_col(h):
        return aux_ref[0, :, h:h + 1]

    def beta_col(h):
        return aux_ref[0, :, DN_HEADS + h:DN_HEADS + h + 1]

    def gc_row(h):
        return auxt_ref[0, h:h + 1, :]

    for h in heads:
        kt = kt_head(h)
        kt_for_q = jnp.concatenate([kt, zeros_kt] if h % 2 == 0 else [zeros_kt, kt], axis=0)
        kt_for_k = jnp.concatenate([zeros_kt, kt], axis=0)
        qk = _dot(q_pair(h), kt_for_q)
        kk = _dot(vk_head(h), kt_for_k)
        e0 = jnp.exp(jnp.minimum(gc_col(h) - gc_row(h), 0.0))
        aqk_ref[h] = (qk * e0 * tri_ref[0]).astype(BF16)
        nm_ref[h] = (-(kk * e0 * tri_ref[1]) * beta_col(h)).astype(BF16)

    leaf = small_ref[0]
    for h in heads:
        nd = nm_ref[h] * leaf
        p_ref[h] = (eye + nd.astype(F32)).astype(BF16)
        x_ref[h] = _dot(nd, nd).astype(BF16)
    for h in heads:
        pb = p_ref[h]
        p_ref[h] = (pb.astype(F32) + _dot(pb, x_ref[h])).astype(BF16)
    for k in range(len(_FULL_MERGE_SIZES)):
        off_mask = small_ref[1 + k]
        for h in heads:
            x_ref[h] = _dot(nm_ref[h] * off_mask, p_ref[h]).astype(BF16)
        for h in heads:
            pb = p_ref[h]
            p_ref[h] = (pb.astype(F32) + _dot(pb, x_ref[h])).astype(BF16)
    for k, s in enumerate(_MERGE_SIZES):
        blocks = list(_active_blocks(d, s))
        zeros_blk = jnp.zeros((s, C), BF16)

        def active_rows(ref, h):
            return jnp.concatenate([ref[h, b * s:(b + 1) * s, :] for b in blocks], axis=0)

        for h in heads:
            xc = _dot(active_rows(nm_ref, h) * offc_ref[k], p_ref[h]).astype(BF16)
            pieces = []
            for j in range(len(blocks)):
                piece = xc[j * s:(j + 1) * s, :]
                pieces += [zeros_blk, piece] if d == 0 else [piece, zeros_blk]
            x_ref[h] = jnp.concatenate(pieces, axis=0)
        for h in heads:
            pr = active_rows(p_ref, h)
            prn = (pr.astype(F32) + _dot(pr, x_ref[h])).astype(BF16)
            for j, b in enumerate(blocks):
                p_ref[h, b * s:(b + 1) * s, :] = prn[j * s:(j + 1) * s, :]

    for h in heads:
        egc = jnp.exp(gc_col(h))
        rhs = (vk_head(h).astype(F32) * beta_col(h) * jnp.where(lane < DN_DV, 1.0, egc)).astype(BF16)
        uw_ref[h] = _dot(p_ref[h], rhs).astype(BF16)
    for h in heads:
        eye_pl = (c64 == r64 + (h % 2) * DN_DV).astype(F32)
        s_aug = jnp.concatenate([eye_pl, -s_ref[h]], axis=0).astype(BF16)
        vn_ref[h] = _dot(uw_ref[h], s_aug).astype(BF16)
    o_pair = None
    for h in heads:
        par = h % 2
        gc_r = gc_row(h)
        g_tot = gc_r[:, C - 1:C] if d == 0 else gc_r[:, 0:1]
        s_pl = s_ref[h]
        v_new = vn_ref[h]
        s_sel = jnp.concatenate([s_pl, zeros_s] if par == 0 else [zeros_s, s_pl], axis=0).astype(BF16)
        qd = (q_pair(h).astype(F32) * jnp.exp(gc_col(h))).astype(BF16)
        o_pl = _dot(qd, s_sel) + _dot(aqk_ref[h], v_new)
        kd = (kt_head(h).astype(F32) * jnp.exp(g_tot - gc_r)).astype(BF16)
        s_ref[h] = s_pl * jnp.exp(g_tot) + _dot(kd, v_new)
        if par == 0:
            o_pair = o_pl
        else:
            o_ref[:, (h // 2) * LANES:(h // 2 + 1) * LANES] = o_pair + o_pl


def _deltanet(qn, vk, kt, aux, auxt, B, S, W, d):
    T = qn.shape[0]
    C = DN_BLOCK
    R = DN_STEP_BLOCKS * C
    nb = S // R

    def blk(b, i):
        return b * nb + (i if d == 0 else nb - 1 - i)

    nq = DN_HEADS * DN_DK
    tri, small, offc = W["dn_masks"][d]
    return pl.pallas_call(
        functools.partial(_deltanet_kernel, d=d),
        grid=(B, nb),
        in_specs=[
            pl.BlockSpec((R, nq), lambda b, i: (blk(b, i), 0)),
            pl.BlockSpec((R, 2 * nq), lambda b, i: (blk(b, i), 0)),
            pl.BlockSpec((nq, R), lambda b, i: (0, blk(b, i))),
            pl.BlockSpec((1, R, 2 * DN_HEADS), lambda b, i: (d, blk(b, i), 0)),
            pl.BlockSpec((1, 2 * DN_HEADS, R), lambda b, i: (d, 0, blk(b, i))),
            _const_spec((2, C, C)), _const_spec((1 + len(_FULL_MERGE_SIZES), C, C)),
            _const_spec((len(_MERGE_SIZES), C // 2, C)),
        ],
        out_specs=pl.BlockSpec((R, DN_HEADS * DN_DV), lambda b, i: (blk(b, i), 0)),
        out_shape=jax.ShapeDtypeStruct((T, DN_HEADS * DN_DV), F32),
        scratch_shapes=[pltpu.VMEM((DN_HEADS, DN_DK, LANES), F32)]
        + [pltpu.VMEM((DN_HEADS, C, C), BF16)] * 4
        + [pltpu.VMEM((DN_HEADS, C, LANES), BF16)] * 2,
        compiler_params=_params(("parallel", "arbitrary")),
        name="deltanet_fwd" if d == 0 else "deltanet_bwd",
    )(qn, vk, kt, aux, auxt, tri, small, offc)


def _layer_norm(v, g, b):
    mu = jnp.mean(v, axis=-1, keepdims=True)
    c = v - mu
    var = jnp.mean(c * c, axis=-1, keepdims=True)
    return c * lax.rsqrt(var + 1e-5) * g + b


def _mix_kernel(x_ref, attn_ref, of_ref, ob_ref, z_ref, gates_ref, p_ref,
                woa_ref, wod_ref, wout_ref, e8_ref, e8t_ref, dnorm_ref, ln1g_ref, ln1b_ref,
                wpg_ref, wpp_ref, rwh_ref, rwl_ref, rb_ref, ustrict_ref,
                r_ref, hb_ref, ti_ref, tg_ref, rank_ref, cnt_ref, run_ref):
    @pl.when(pl.program_id(0) == 0)
    def _():
        run_ref[...] = jnp.zeros_like(run_ref)

    oa = _dot(attn_ref[...], woa_ref[...])
    o = of_ref[...] + ob_ref[...]
    hi, lo = _split_bf16(o * o)
    ms = (_dot(hi, e8_ref[...]) + _dot(lo, e8_ref[...])) * (1.0 / DN_DV)
    ih, il = _split_bf16(lax.rsqrt(ms + 1e-6))
    sc = _dot(ih, e8t_ref[...]) + _dot(il, e8t_ref[...])
    zf = z_ref[...].astype(F32)
    od_in = o * sc * dnorm_ref[...] * (zf * (1.0 / (1.0 + jnp.exp(-zf))))
    od = _dot(od_in.astype(BF16), wod_ref[...])
    mix = gates_ref[:, :D_MODEL].astype(F32) * oa + gates_ref[:, D_MODEL:].astype(F32) * od
    mo = _dot(mix.astype(BF16), wout_ref[...])
    h = _layer_norm(DEEPNORM_ALPHA * x_ref[...] + mo, ln1g_ref[...], ln1b_ref[...])
    hb = h.astype(BF16)
    hb_ref[...] = _pack_halves(h)
    pg = _dot(hb, wpg_ref[...])
    pp = _dot(p_ref[...].astype(BF16), wpp_ref[...])
    r_ref[...] = DEEPNORM_ALPHA * h + pp * (1.0 / (1.0 + jnp.exp(-pg)))

    hl = (h - hb.astype(F32)).astype(BF16)
    logits = (_dot_nt(rwh_ref[...], hb) + _dot_nt(rwh_ref[...], hl)
              + _dot_nt(rwl_ref[...], hb) + rb_ref[...])
    eid = lax.broadcasted_iota(jnp.int32, logits.shape, 0)
    vals = []
    run = run_ref[:, 0:1]
    for k in range(TOP_K):
        m = jnp.max(logits, axis=0, keepdims=True)
        idx = jnp.min(jnp.where(logits == m, eid, N_EXPERTS), axis=0, keepdims=True)
        ti_ref[k:k + 1, :] = idx
        vals.append(m)
        hit = eid == idx
        logits = jnp.where(hit, -jnp.inf, logits)
        onehot = hit.astype(F32)
        earlier = _dot(onehot.astype(BF16), ustrict_ref[...])
        rank = jnp.sum(onehot * (run + earlier), axis=0, keepdims=True)
        rank_ref[k:k + 1, :] = rank.astype(jnp.int32)
        run = run + jnp.sum(onehot, axis=1, keepdims=True)
    run_ref[...] = jnp.broadcast_to(run, run_ref.shape)
    cnt_ref[...] = jnp.broadcast_to(run, cnt_ref.shape)
    es = [jnp.exp(v - vals[0]) for v in vals]
    den = es[0] + es[1] + es[2] + es[3]
    for k in range(TOP_K):
        tg_ref[k:k + 1, :] = es[k] / den


def _mix(x2, attn, o_f, o_b, z, gates, p2, W):
    tm = MIX_TILE
    T = x2.shape[0]
    row = lambda i: (i, 0)
    nd = DN_HEADS * DN_DV
    return pl.pallas_call(
        _mix_kernel,
        grid=(T // tm,),
        in_specs=[
            pl.BlockSpec((tm, D_MODEL), row),
            pl.BlockSpec((tm, MLA_HEADS * V_DIM), row),
            pl.BlockSpec((tm, nd), row),
            pl.BlockSpec((tm, nd), row),
            pl.BlockSpec((tm, nd), row),
            pl.BlockSpec((tm, 2 * D_MODEL), row),
            pl.BlockSpec((tm, PLE_DIM), row),
            _const_spec((MLA_HEADS * V_DIM, D_MODEL)), _const_spec((nd, D_MODEL)),
            _const_spec((D_MODEL, D_MODEL)),
            _const_spec((nd, LANES)), _const_spec((LANES, nd)), _const_spec((1, nd)),
            _const_spec((1, D_MODEL)), _const_spec((1, D_MODEL)),
            _const_spec((D_MODEL, D_MODEL)), _const_spec((PLE_DIM, D_MODEL)),
            _const_spec((N_EXPERTS, D_MODEL)), _const_spec((N_EXPERTS, D_MODEL)),
            _const_spec((N_EXPERTS, 1)),
            _const_spec((tm, tm)),
        ],
        out_specs=[
            pl.BlockSpec((tm, D_MODEL), row),
            pl.BlockSpec((tm, D_MODEL // 2), row),
            pl.BlockSpec((TOP_K, tm), lambda i: (0, i)),
            pl.BlockSpec((TOP_K, tm), lambda i: (0, i)),
            pl.BlockSpec((TOP_K, tm), lambda i: (0, i)),
            _const_spec((N_EXPERTS, LANES)),
        ],
        out_shape=[
            jax.ShapeDtypeStruct((T, D_MODEL), F32),
            jax.ShapeDtypeStruct((T, D_MODEL // 2), jnp.uint32),
            jax.ShapeDtypeStruct((TOP_K, T), jnp.int32),
            jax.ShapeDtypeStruct((TOP_K, T), F32),
            jax.ShapeDtypeStruct((TOP_K, T), jnp.int32),
            jax.ShapeDtypeStruct((N_EXPERTS, LANES), F32),
        ],
        scratch_shapes=[pltpu.VMEM((N_EXPERTS, LANES), F32)],
        compiler_params=_params(("arbitrary",)),
        name="mix",
    )(x2, attn, o_f, o_b, z, gates, p2, W["woa"], W["wod"], W["wout"], W["e8"], W["e8t"], W["dnorm"],
      W["ln1g"], W["ln1b"], W["wpg"], W["wpp"], W["rwh"], W["rwl"], W["rb"], W["ustrict"])


_CAST_ROWS = 256


_S_EXPERT, _S_FIRST, _S_NEXT, _S_SLOT = range(4)


def _moe_kernel(sched_ref, nvalid_ref, xs_ref, wgu_hbm, bgu_ref, wd_hbm, bd_ref, y_ref,
                wgu_ref, wd_ref, gu32_ref, d32_ref, sem, *, fc):
    i = pl.program_id(0)
    valid = i < nvalid_ref[0]
    expert = sched_ref[_S_EXPERT, i]
    slot = sched_ref[_S_SLOT, i]

    def weight_copies(e, s):
        return (pltpu.make_async_copy(wgu_hbm.at[e], gu32_ref.at[s], sem.at[0, s]),
                pltpu.make_async_copy(wd_hbm.at[e], d32_ref.at[s], sem.at[1, s]))

    @pl.when(i == 0)
    def _():
        for c in weight_copies(expert, slot):
            c.start()

    @pl.when(valid & (sched_ref[_S_FIRST, i] == 1))
    def _():
        for c in weight_copies(expert, slot):
            c.wait()
        for r in range(0, D_MODEL, _CAST_ROWS):
            wgu_ref[r:r + _CAST_ROWS, :] = gu32_ref[slot, r:r + _CAST_ROWS, :].astype(BF16)
        for r in range(0, D_FF, _CAST_ROWS):
            wd_ref[r:r + _CAST_ROWS, :] = d32_ref[slot, r:r + _CAST_ROWS, :].astype(BF16)
        nxt = sched_ref[_S_NEXT, i]

        @pl.when(nxt >= 0)
        def _():
            for c in weight_copies(nxt, 1 - slot):
                c.start()

    @pl.when(valid)
    def _():
        xs = jnp.concatenate(_unpack_halves(xs_ref[...]), axis=1).astype(BF16)
        acc = None
        for c in range(D_FF // fc):
            lo, hi = c * fc, (c + 1) * fc
            gate = _dot(xs, wgu_ref[:, lo:hi]) + bgu_ref[0, :, lo:hi]
            up = _dot(xs, wgu_ref[:, D_FF + lo:D_FF + hi]) + bgu_ref[0, :, D_FF + lo:D_FF + hi]
            gate = jnp.minimum(gate, SWIGLU_LIMIT)
            up = jnp.clip(up, -SWIGLU_LIMIT, SWIGLU_LIMIT)
            act = gate * (1.0 / (1.0 + jnp.exp(-SWIGLU_ALPHA * gate))) * (up + 1.0)
            part = _dot(act.astype(BF16), wd_ref[lo:hi, :])
            acc = part if acc is None else acc + part
        y_ref[...] = _pack_halves(acc + bd_ref[0])

    @pl.when(jnp.logical_not(valid))
    def _():
        y_ref[...] = jnp.zeros_like(y_ref)


def _moe(xs, sched, nvalid, W, bm, fc=512):
    P = xs.shape[0]
    by_expert = lambda i, sc, nv: (sc[_S_EXPERT, i], 0, 0)
    grid_spec = pltpu.PrefetchScalarGridSpec(
        num_scalar_prefetch=2,
        grid=(P // bm,),
        in_specs=[
            pl.BlockSpec((bm, D_MODEL // 2), lambda i, sc, nv: (i, 0)),
            pl.BlockSpec(memory_space=pl.ANY),
            pl.BlockSpec((1, 1, 2 * D_FF), by_expert),
            pl.BlockSpec(memory_space=pl.ANY),
            pl.BlockSpec((1, 1, D_MODEL), by_expert),
        ],
        out_specs=pl.BlockSpec((bm, D_MODEL // 2), lambda i, sc, nv: (i, 0)),
        scratch_shapes=[pltpu.VMEM((D_MODEL, 2 * D_FF), BF16), pltpu.VMEM((D_FF, D_MODEL), BF16),
                        pltpu.VMEM((2, D_MODEL, 2 * D_FF), F32), pltpu.VMEM((2, D_FF, D_MODEL), F32),
                        pltpu.SemaphoreType.DMA((2, 2))],
    )
    return pl.pallas_call(
        functools.partial(_moe_kernel, fc=fc),
        grid_spec=grid_spec,
        out_shape=jax.ShapeDtypeStruct((P, D_MODEL // 2), jnp.uint32),
        compiler_params=_params(("arbitrary",)),
        name="moe",
    )(sched, nvalid, xs, W["wgu"], W["bgu"], W["wd"], W["bd"])


def _final_kernel(r_ref, yg_ref, tg_ref, g_ref, b_ref, y_ref):
    ffn_lo = ffn_hi = None
    for k in range(TOP_K):
        lo, hi = _unpack_halves(yg_ref[k])
        g = tg_ref[:, k:k + 1]
        ffn_lo = lo * g if ffn_lo is None else ffn_lo + lo * g
        ffn_hi = hi * g if ffn_hi is None else ffn_hi + hi * g
    acc = r_ref[...] + jnp.concatenate([ffn_lo, ffn_hi], axis=1)
    y_ref[...] = _layer_norm(acc, g_ref[...], b_ref[...])


def _final(r, yg, tg, W, tm=512):
    T = r.shape[0]
    tm = min(tm, T)
    row = lambda i: (i, 0)
    return pl.pallas_call(
        _final_kernel,
        grid=(T // tm,),
        in_specs=[pl.BlockSpec((tm, D_MODEL), row),
                  pl.BlockSpec((TOP_K, tm, D_MODEL // 2), lambda i: (0, i, 0)),
                  pl.BlockSpec((tm, TOP_K), row),
                  _const_spec((1, D_MODEL)), _const_spec((1, D_MODEL))],
        out_specs=pl.BlockSpec((tm, D_MODEL), row),
        out_shape=jax.ShapeDtypeStruct((T, D_MODEL), F32),
        compiler_params=_params(("parallel",)),
        name="final_ln",
    )(r, yg, tg, W["ln2g"], W["ln2b"])


def _pad_heads(w, n_heads, width, start, size, dst=0):
    K = w.shape[0]
    w3 = w.reshape(K, n_heads, width)[:, :, start:start + size]
    out = jnp.zeros((K, n_heads, HEAD_PAD), w.dtype)
    out = out.at[:, :, dst:dst + size].set(w3)
    return out.reshape(K, n_heads * HEAD_PAD)


def _prep_weights(w_in, q_a_norm, w_uq, kv_a_norm, w_ukv, w_o_attn, dn_conv, dn_a_log, dn_dt_bias,
                  dn_norm, w_o_dn, w_out, ln1_g, ln1_b, router_w, router_b, w_gate_up, b_gate_up,
                  w_down, b_down, ple_w_proj, ple_w_gate, ln2_g, ln2_b):
    W = {}
    half = ROPE_DIM // 2
    o = 0
    cq = w_in[:, o:o + Q_LORA]; o += Q_LORA
    ckv = w_in[:, o:o + KV_LORA]; o += KV_LORA
    kr = w_in[:, o:o + ROPE_DIM]; o += ROPE_DIM
    nqk = DN_HEADS * DN_DK
    dq = w_in[:, o:o + nqk]; o += nqk
    dk = w_in[:, o:o + nqk]; o += nqk
    dv = w_in[:, o:o + DN_HEADS * DN_DV]; o += DN_HEADS * DN_DV
    dz = w_in[:, o:o + DN_HEADS * DN_DV]; o += DN_HEADS * DN_DV
    da = w_in[:, o:o + 2 * DN_HEADS]; o += 2 * DN_HEADS
    db = w_in[:, o:o + 2 * DN_HEADS]; o += 2 * DN_HEADS
    gate = w_in[:, o:o + 2 * D_MODEL]

    def lane_block(parts):
        w = jnp.concatenate(parts, axis=1)
        return jnp.pad(w, ((0, 0), (0, LANES - w.shape[1])))

    def misc_block(d, rope_cols):
        ab = [da[:, d * DN_HEADS:(d + 1) * DN_HEADS], db[:, d * DN_HEADS:(d + 1) * DN_HEADS]]
        return lane_block(ab + [jnp.zeros((D_MODEL, NOPE_DIM - 2 * DN_HEADS), F32), rope_cols])

    misc0 = misc_block(0, kr)
    misc1 = misc_block(1, jnp.concatenate([kr[:, half:], kr[:, :half]], axis=1))

    def interleave_vk(v, k):
        lead = v.shape[:-1]
        v3 = v.reshape(lead + (DN_HEADS, DN_DV))
        k3 = k.reshape(lead + (DN_HEADS, DN_DK))
        return jnp.concatenate([v3, k3], axis=-1).reshape(lead + (DN_HEADS * (DN_DV + DN_DK),))

    W["w1"] = jnp.concatenate([cq, ckv, misc0, misc1, dq, interleave_vk(dv, dk), dz, gate],
                              axis=1).astype(BF16)
    W["qan"] = q_a_norm.reshape(1, Q_LORA)
    W["kvan"] = kv_a_norm.reshape(1, KV_LORA)

    qw = NOPE_DIM + ROPE_DIM
    wq_nope = _pad_heads(w_uq, MLA_HEADS, qw, 0, NOPE_DIM, 0)
    wq_r1 = _pad_heads(w_uq, MLA_HEADS, qw, NOPE_DIM, half, NOPE_DIM)
    wq_r2 = _pad_heads(w_uq, MLA_HEADS, qw, NOPE_DIM + half, half, NOPE_DIM + half)
    W["wq"] = (wq_nope + wq_r1 + wq_r2).astype(BF16)
    wq_s1 = _pad_heads(w_uq, MLA_HEADS, qw, NOPE_DIM + half, half, NOPE_DIM)
    wq_s2 = _pad_heads(w_uq, MLA_HEADS, qw, NOPE_DIM, half, NOPE_DIM + half)
    W["wqs"] = (wq_s1 + wq_s2).astype(BF16)
    kvw = NOPE_DIM + V_DIM
    W["wk"] = _pad_heads(w_ukv, MLA_HEADS, kvw, 0, NOPE_DIM, 0).astype(BF16)
    W["wv"] = _pad_heads(w_ukv, MLA_HEADS, kvw, NOPE_DIM, V_DIM, 0).astype(BF16)

    neg_a = -jnp.exp(dn_a_log.astype(F32))
    abp = jnp.zeros((8, LANES), F32)
    for d in range(2):
        abp = abp.at[2 * d, :DN_HEADS].set(neg_a[d])
        abp = abp.at[2 * d + 1, :DN_HEADS].set(dn_dt_bias[d].astype(F32))
    W["abp"] = abp

    cw = jnp.concatenate([dn_conv[:, :nqk], interleave_vk(dn_conv[:, 2 * nqk:], dn_conv[:, nqk:2 * nqk])], axis=1)
    W["convw"] = jnp.pad(cw.astype(F32), ((0, 8 - CONV_K), (0, 0)))

    def group_indicator(width, group):
        e = (np.arange(width)[:, None] // group == np.arange(LANES)[None, :]).astype(np.float32)
        return e

    eq = group_indicator(nqk, DN_DK)
    W["eq"] = jnp.asarray(eq, BF16)
    W["eqt"] = jnp.asarray(eq.T, BF16)
    evk = group_indicator(2 * nqk, DN_DK)
    W["evk"] = jnp.asarray(evk, BF16)
    W["evkt"] = jnp.asarray(evk.T, BF16)
    W["e8"] = W["eq"]
    W["e8t"] = W["eqt"]
    r = np.arange(DN_BLOCK)
    W["tri"] = jnp.asarray(np.stack([r[:, None] >= r[None, :], r[:, None] <= r[None, :]]).astype(np.float32), BF16)
    rp = np.arange(DN_PREP_TILE)
    W["conv_shift"] = jnp.asarray(
        np.stack([rp[None, :] == rp[:, None] + (j - CONV_K // 2) for j in range(CONV_K)]).astype(np.float32), BF16)
    W["dn_masks"] = [(jnp.asarray(t), jnp.asarray(g, BF16), jnp.asarray(o, BF16))
                     for t, g, o in (_dn_masks(0), _dn_masks(1))]
    rt = np.arange(MIX_TILE)
    W["ustrict"] = jnp.asarray((rt[:, None] < rt[None, :]).astype(np.float32), BF16)

    W["woa"] = w_o_attn.astype(BF16)
    W["wod"] = w_o_dn.astype(BF16)
    W["wout"] = w_out.astype(BF16)
    W["dnorm"] = jnp.tile(dn_norm.astype(F32), DN_HEADS).reshape(1, DN_HEADS * DN_DV)
    W["ln1g"] = ln1_g.reshape(1, D_MODEL)
    W["ln1b"] = ln1_b.reshape(1, D_MODEL)
    W["ln2g"] = ln2_g.reshape(1, D_MODEL)
    W["ln2b"] = ln2_b.reshape(1, D_MODEL)
    W["wpg"] = ple_w_gate.astype(BF16)
    W["wpp"] = ple_w_proj.astype(BF16)
    rwt = router_w.T.astype(F32)
    W["rwh"], W["rwl"] = _split_bf16(rwt)
    W["rb"] = router_b.reshape(N_EXPERTS, 1).astype(F32)
    W["wgu"] = w_gate_up.astype(F32)
    W["bgu"] = b_gate_up.reshape(N_EXPERTS, 1, 2 * D_FF).astype(F32)
    W["wd"] = w_down.astype(F32)
    W["bd"] = b_down.reshape(N_EXPERTS, 1, D_MODEL).astype(F32)
    return W


def _rope_tables(S):
    half = ROPE_DIM // 2
    inv = ROPE_THETA ** (-jnp.arange(0, ROPE_DIM, 2, dtype=F32) / ROPE_DIM)
    ang = jnp.arange(S, dtype=F32)[:, None] * inv[None, :]
    cos, sin = jnp.cos(ang), jnp.sin(ang)
    c = (NOPE_DIM + ROPE_DIM) ** -0.5 * math.log2(math.e)
    pad = jnp.zeros((S, HEAD_PAD - NOPE_DIM - ROPE_DIM), F32)
    cos_blk = jnp.concatenate([cos, cos, pad], axis=1)
    sin_blk = jnp.concatenate([-sin, sin, pad], axis=1)
    cosq = jnp.concatenate([jnp.ones((S, NOPE_DIM), F32), cos_blk], axis=1) * c
    sinq = jnp.concatenate([jnp.zeros((S, NOPE_DIM), F32), sin_blk], axis=1) * c
    cosk = jnp.concatenate([jnp.zeros((S, NOPE_DIM), F32), cos_blk], axis=1)
    sink = jnp.concatenate([jnp.zeros((S, NOPE_DIM), F32), sin_blk], axis=1)
    return cosq, sinq, cosk, sink


def _dest_kernel(pstart_ref, ti_ref, rank_ref, dest_ref):
    ti = ti_ref[...]
    dest = rank_ref[...]
    for e in range(N_EXPERTS):
        dest = dest + jnp.where(ti == e, pstart_ref[e], 0)
    dest_ref[...] = dest


def _dest(p_start, top_i, rank, tile=8192):
    T = top_i.shape[1]
    tile = min(tile, T)
    spec = pl.BlockSpec((TOP_K, tile), lambda i, ps: (0, i))
    return pl.pallas_call(
        _dest_kernel,
        grid_spec=pltpu.PrefetchScalarGridSpec(num_scalar_prefetch=1, grid=(T // tile,),
                                               in_specs=[spec, spec], out_specs=spec),
        out_shape=jax.ShapeDtypeStruct((TOP_K, T), jnp.int32),
        compiler_params=_params(("parallel",)),
        name="slot_index",
    )(p_start, top_i, rank)


SC_CHUNK = 128


def _sc_mesh():
    info = plsc.get_sparse_core_info()
    mesh = plsc.VectorSubcoreMesh(core_axis_name="c", subcore_axis_name="s")
    return mesh, info.num_cores, info.num_cores * info.num_subcores


def _sc_dispatch(rows, dest3, P):
    T, D = rows.shape
    K = dest3.shape[0]
    mesh, n_cores, n_workers = _sc_mesh()
    n_chunks = T // (n_workers * SC_CHUNK)

    @functools.partial(
        pl.kernel, mesh=mesh, out_type=jax.ShapeDtypeStruct((P, D), rows.dtype),
        scratch_types=[pltpu.VMEM((K, n_chunks, SC_CHUNK), jnp.int32), pltpu.VMEM((SC_CHUNK, D), rows.dtype),
                       pltpu.SemaphoreType.DMA])
    def dispatch(rows_hbm, dest_hbm, out_hbm, idx_v, rows_v, sem):
        worker = lax.axis_index("s") * n_cores + lax.axis_index("c")
        first = pl.multiple_of(worker * n_chunks, n_chunks)
        pltpu.sync_copy(dest_hbm.at[:, pl.ds(first, n_chunks)], idx_v)

        @pl.loop(0, n_chunks)
        def _(j):
            base = pl.multiple_of((first + j) * SC_CHUNK, SC_CHUNK)
            pltpu.sync_copy(rows_hbm.at[pl.ds(base, SC_CHUNK)], rows_v)
            for k in range(K):
                pltpu.async_copy(rows_v, out_hbm.at[idx_v.at[k, j]], sem).wait()

    return dispatch(rows, dest3)


def _sc_gather(table, idx2):
    M = idx2.shape[0] * SC_CHUNK
    D = table.shape[1]
    mesh, n_cores, n_workers = _sc_mesh()
    n_chunks = M // (n_workers * SC_CHUNK)

    @functools.partial(
        pl.kernel, mesh=mesh, out_type=jax.ShapeDtypeStruct((M, D), table.dtype),
        scratch_types=[pltpu.VMEM((n_chunks, SC_CHUNK), jnp.int32), pltpu.VMEM((SC_CHUNK, D), table.dtype),
                       pltpu.SemaphoreType.DMA])
    def gather(table_hbm, idx_hbm, out_hbm, idx_v, rows_v, sem):
        worker = lax.axis_index("s") * n_cores + lax.axis_index("c")
        first = pl.multiple_of(worker * n_chunks, n_chunks)
        pltpu.sync_copy(idx_hbm.at[pl.ds(first, n_chunks)], idx_v)

        @pl.loop(0, n_chunks)
        def _(j):
            base = pl.multiple_of((first + j) * SC_CHUNK, SC_CHUNK)
            pltpu.async_copy(table_hbm.at[idx_v.at[j]], rows_v, sem).wait()
            pltpu.sync_copy(rows_v, out_hbm.at[pl.ds(base, SC_CHUNK)])

    return gather(table, idx2)


def _route(top_i, rank, counts, T, bm):
    A = TOP_K * T
    counts = counts.astype(jnp.int32)
    padded = ((counts + bm - 1) // bm) * bm
    p_end = jnp.cumsum(padded)
    p_start = p_end - padded
    dest = _dest(p_start, top_i, rank)
    nblk = A // bm + N_EXPERTS
    blk_start = jnp.arange(nblk, dtype=jnp.int32) * bm
    blk_e = jnp.minimum(jnp.sum(p_end[None, :] <= blk_start[:, None], axis=1), N_EXPERTS - 1).astype(jnp.int32)
    nvalid = (p_end[-1] // bm).astype(jnp.int32).reshape(1)
    idx = jnp.arange(nblk, dtype=jnp.int32)
    first = ((idx == 0) | (blk_e != jnp.roll(blk_e, 1))).astype(jnp.int32)
    after = p_end[blk_e] // bm
    nxt = jnp.where(after < nvalid[0], blk_e[jnp.minimum(after, nblk - 1)], -1).astype(jnp.int32)
    slot = (jnp.cumsum(first) - 1) % 2
    sched = jnp.stack([blk_e, first, nxt, slot.astype(jnp.int32)])
    return dest, nblk * bm, sched, nvalid


def _layer(x, p, W, bm):
    B, S, _ = x.shape
    T = B * S
    x2 = x.reshape(T, D_MODEL)
    p2 = p.reshape(T, PLE_DIM)
    Wl = dict(W)
    Wl["cosq"], Wl["sinq"], Wl["cosk"], Wl["sink"] = _rope_tables(S)

    q, k, v, dn, z, gates, gb = _in_proj(x2, S, Wl)
    attn = _attention(q, k, v, B, S)
    qn, vk, kt, aux = _dn_prep(dn, gb, S, Wl)
    auxt = jnp.swapaxes(aux, 1, 2)
    o_f = _deltanet(qn, vk, kt, aux, auxt, B, S, Wl, 0)
    o_b = _deltanet(qn, vk, kt, aux, auxt, B, S, Wl, 1)
    r, hb, top_i, top_g, rank, cnt = _mix(x2, attn, o_f, o_b, z, gates, p2, Wl)

    dest, P, blk_e, nvalid = _route(top_i, rank, cnt[:, 0], T, bm)
    xs = _sc_dispatch(hb, dest.reshape(TOP_K, T // SC_CHUNK, SC_CHUNK), P)
    yb = _moe(xs, blk_e, nvalid, Wl, bm)
    yg = _sc_gather(yb, dest.reshape(TOP_K * T // SC_CHUNK, SC_CHUNK)).reshape(TOP_K, T, D_MODEL // 2)
    y = _final(r, yg, top_g.T, Wl)
    return y.reshape(B, S, D_MODEL)


def kernel(x_prompt, x_sample, p_prompt, p_sample, w_in, q_a_norm, w_uq, kv_a_norm, w_ukv, w_o_attn, dn_conv, dn_a_log, dn_dt_bias, dn_norm, w_o_dn, w_out, ln1_g, ln1_b, router_w, router_b, w_gate_up, b_gate_up, w_down, b_down, ple_w_proj, ple_w_gate, ln2_g, ln2_b):
    y_prompt, y_sample = x_prompt, x_sample
    for l in range(DEPTH):
        W = _prep_weights(w_in[l], q_a_norm[l], w_uq[l], kv_a_norm[l], w_ukv[l], w_o_attn[l], dn_conv[l],
                          dn_a_log[l], dn_dt_bias[l], dn_norm[l], w_o_dn[l], w_out[l], ln1_g[l], ln1_b[l],
                          router_w[l], router_b[l], w_gate_up[l], b_gate_up[l], w_down[l], b_down[l],
                          ple_w_proj[l], ple_w_gate[l], ln2_g[l], ln2_b[l])
        y_prompt = _layer(y_prompt, p_prompt[l], W, bm=MOE_BLOCK)
        y_sample = _layer(y_sample, p_sample[l], W, bm=MOE_BLOCK)
    return (y_prompt, y_sample)
```

```python
import functools
import math

import numpy as np
import jax
import jax.numpy as jnp
from jax import lax
from jax.experimental import pallas as pl
from jax.experimental.pallas import tpu as pltpu
from jax.experimental.pallas import tpu_sc as plsc

D_MODEL = 1024
MLA_HEADS = 8
Q_LORA = 256
KV_LORA = 128
NOPE_DIM = 64
ROPE_DIM = 32
V_DIM = 64
ROPE_THETA = 10000.0
DN_HEADS = 8
DN_DK = 64
DN_DV = 64
CONV_K = 5
N_EXPERTS = 32
TOP_K = 4
D_FF = 1024
SWIGLU_LIMIT = 7.0
SWIGLU_ALPHA = 1.702
PLE_DIM = 256
DEPTH = 1
DEEPNORM_ALPHA = (2.0 * DEPTH) ** 0.25

LANES = 128
HEAD_PAD = 128
DN_BLOCK = 256
DN_STEP_BLOCKS = 2
DN_PREP_TILE = 256
MOE_BLOCK = 512
MIX_TILE = 512
VMEM_LIMIT = 56 * 1024 * 1024

_C_CQ = 0
_C_CKV = _C_CQ + Q_LORA
_C_MISC0 = _C_CKV + KV_LORA
_C_MISC1 = _C_MISC0 + LANES
_C_DNQ = _C_MISC1 + LANES
_C_DNVK = _C_DNQ + DN_HEADS * DN_DK
_C_Z = _C_DNVK + DN_HEADS * (DN_DK + DN_DV)
_C_GATE = _C_Z + DN_HEADS * DN_DV
_C_END = _C_GATE + 2 * D_MODEL

BF16 = jnp.bfloat16
F32 = jnp.float32


def _dot(a, b):
    return jnp.dot(a, b, preferred_element_type=F32)


def _dot_nt(a, b):
    return lax.dot_general(a, b, (((1,), (1,)), ((), ())), preferred_element_type=F32)


def _split_bf16(x):
    hi = x.astype(BF16)
    lo = (x - hi.astype(F32)).astype(BF16)
    return hi, lo


_HI_HALFWORD = 0xFFFF0000


def _pack_halves(x):
    w = x.shape[1] // 2
    bits = lax.bitcast_convert_type(x.astype(BF16).astype(F32), jnp.uint32)
    return (bits[:, :w] >> 16) | (bits[:, w:] & jnp.uint32(_HI_HALFWORD))


def _unpack_halves(words):
    lo = lax.bitcast_convert_type(words << 16, F32)
    hi = lax.bitcast_convert_type(words & jnp.uint32(_HI_HALFWORD), F32)
    return lo, hi


def _const_spec(shape):
    n = len(shape)
    return pl.BlockSpec(shape, lambda *_: (0,) * n, pipeline_mode=pl.Buffered(1))


def _params(sem):
    return pltpu.CompilerParams(dimension_semantics=sem, vmem_limit_bytes=VMEM_LIMIT)


def _in_proj_kernel(x_ref, w1_ref, qan_ref, kvan_ref, wq_ref, wqs_ref, wk_ref, wv_ref,
                    cosq_ref, sinq_ref, cosk_ref, sink_ref, abp_ref,
                    q_ref, k_ref, v_ref, dn_ref, z_ref, gates_ref, gb_ref):
    xb = x_ref[...].astype(BF16)

    def proj(lo, hi):
        return _dot(xb, w1_ref[:, lo:hi])

    def rms(c, g):
        return (c * lax.rsqrt(jnp.mean(c * c, axis=-1, keepdims=True) + 1e-6) * g).astype(BF16)

    cqn = rms(proj(_C_CQ, _C_CKV), qan_ref[...])
    qa = _dot(cqn, wq_ref[...])
    qb = _dot(cqn, wqs_ref[...])
    ckvn = rms(proj(_C_CKV, _C_MISC0), kvan_ref[...])
    kw = _dot(ckvn, wk_ref[...])
    vw = _dot(ckvn, wv_ref[...])
    misc = (proj(_C_MISC0, _C_MISC1), proj(_C_MISC1, _C_DNQ))
    kr = misc[0] * cosk_ref[...] + misc[1] * sink_ref[...]
    cosq = cosq_ref[...]
    sinq = sinq_ref[...]
    lane = lax.broadcasted_iota(jnp.int32, (1, HEAD_PAD), 1)
    ones_col = (lane == V_DIM).astype(F32)
    for h in range(MLA_HEADS):
        sl = slice(h * HEAD_PAD, (h + 1) * HEAD_PAD)
        q_ref[:, sl] = (qa[:, sl] * cosq + qb[:, sl] * sinq).astype(BF16)
        k_ref[:, sl] = (kw[:, sl] + kr).astype(BF16)
        v_ref[:, sl] = (vw[:, sl] + ones_col).astype(BF16)

    for d in range(2):
        ab = misc[d]
        neg_a = abp_ref[2 * d:2 * d + 1, :]
        dtb = abp_ref[2 * d + 1:2 * d + 2, :]
        t = ab + dtb
        sp = jnp.maximum(t, 0.0) + jnp.log(1.0 + jnp.exp(-jnp.abs(t)))
        g = neg_a * sp
        beta = 1.0 / (1.0 + jnp.exp(-ab))
        gb_ref[d] = jnp.where(lane < DN_HEADS, g, beta)[:, :2 * DN_HEADS]

    dn_ref[...] = proj(_C_DNQ, _C_Z).astype(BF16)
    z_ref[...] = proj(_C_Z, _C_GATE).astype(BF16)
    gl = proj(_C_GATE, _C_END)
    gates_ref[...] = (1.0 / (1.0 + jnp.exp(-gl))).astype(BF16)


def _in_proj(x2, S, W, tm=512):
    T = x2.shape[0]
    nseq = S // tm
    row = lambda i: (i, 0)
    pos = lambda i: (i % nseq, 0)
    dn_w = _C_Z - _C_DNQ
    return pl.pallas_call(
        _in_proj_kernel,
        grid=(T // tm,),
        in_specs=[
            pl.BlockSpec((tm, D_MODEL), row),
            _const_spec((D_MODEL, _C_END)),
            _const_spec((1, Q_LORA)), _const_spec((1, KV_LORA)),
            _const_spec((Q_LORA, MLA_HEADS * HEAD_PAD)), _const_spec((Q_LORA, MLA_HEADS * HEAD_PAD)),
            _const_spec((KV_LORA, MLA_HEADS * HEAD_PAD)), _const_spec((KV_LORA, MLA_HEADS * HEAD_PAD)),
            pl.BlockSpec((tm, HEAD_PAD), pos), pl.BlockSpec((tm, HEAD_PAD), pos),
            pl.BlockSpec((tm, HEAD_PAD), pos), pl.BlockSpec((tm, HEAD_PAD), pos),
            _const_spec((8, LANES)),
        ],
        out_specs=[
            pl.BlockSpec((tm, MLA_HEADS * HEAD_PAD), row),
            pl.BlockSpec((tm, MLA_HEADS * HEAD_PAD), row),
            pl.BlockSpec((tm, MLA_HEADS * HEAD_PAD), row),
            pl.BlockSpec((tm, dn_w), row),
            pl.BlockSpec((tm, DN_HEADS * DN_DV), row),
            pl.BlockSpec((tm, 2 * D_MODEL), row),
            pl.BlockSpec((2, tm, 2 * DN_HEADS), lambda i: (0, i, 0)),
        ],
        out_shape=[
            jax.ShapeDtypeStruct((T, MLA_HEADS * HEAD_PAD), BF16),
            jax.ShapeDtypeStruct((T, MLA_HEADS * HEAD_PAD), BF16),
            jax.ShapeDtypeStruct((T, MLA_HEADS * HEAD_PAD), BF16),
            jax.ShapeDtypeStruct((T, dn_w), BF16),
            jax.ShapeDtypeStruct((T, DN_HEADS * DN_DV), BF16),
            jax.ShapeDtypeStruct((T, 2 * D_MODEL), BF16),
            jax.ShapeDtypeStruct((2, T, 2 * DN_HEADS), F32),
        ],
        compiler_params=_params(("parallel",)),
        name="in_proj",
    )(x2, W["w1"], W["qan"], W["kvan"], W["wq"], W["wqs"], W["wk"], W["wv"],
      W["cosq"], W["sinq"], W["cosk"], W["sink"], W["abp"])


def _attn_kernel(q_ref, k_ref, v_ref, o_ref, *, tk, unroll):
    tq = q_ref.shape[0]
    S = k_ref.shape[0]
    outs = []
    for hh in range(2):
        sl = slice(hh * HEAD_PAD, (hh + 1) * HEAD_PAD)
        q = q_ref[:, sl]

        def body(j, carry, q=q, sl=sl):
            m, acc = carry
            off = pl.multiple_of(j * tk, tk)
            s = _dot_nt(q, k_ref[pl.ds(off, tk), sl])
            m_new = jnp.maximum(m, jnp.max(s, axis=-1, keepdims=True))
            p = jnp.exp2(s - m_new).astype(BF16)
            acc = acc * jnp.exp2(m - m_new) + _dot(p, v_ref[pl.ds(off, tk), sl])
            return m_new, acc

        m0 = jnp.full((tq, 1), -1e30, F32)
        acc0 = jnp.zeros((tq, HEAD_PAD), F32)
        _, acc = lax.fori_loop(0, S // tk, body, (m0, acc0), unroll=unroll)
        outs.append(acc / acc[:, V_DIM:V_DIM + 1])
    lane = lax.broadcasted_iota(jnp.int32, (1, HEAD_PAD), 1)
    o_ref[...] = jnp.where(lane < V_DIM, outs[0], pltpu.roll(outs[1], V_DIM, axis=1)).astype(BF16)


def _attention(q, k, v, B, S, tq=1024, tk=2048, unroll=4):
    T = q.shape[0]
    tq = min(tq, S)
    tk = min(tk, S)
    nq = S // tq
    return pl.pallas_call(
        functools.partial(_attn_kernel, tk=tk, unroll=unroll),
        grid=(B, MLA_HEADS // 2, nq),
        in_specs=[
            pl.BlockSpec((tq, 2 * HEAD_PAD), lambda b, h, i: (b * nq + i, h)),
            pl.BlockSpec((S, 2 * HEAD_PAD), lambda b, h, i: (b, h)),
            pl.BlockSpec((S, 2 * HEAD_PAD), lambda b, h, i: (b, h)),
        ],
        out_specs=pl.BlockSpec((tq, 2 * V_DIM), lambda b, h, i: (b * nq + i, h)),
        out_shape=jax.ShapeDtypeStruct((T, MLA_HEADS * V_DIM), BF16),
        compiler_params=_params(("parallel", "parallel", "arbitrary")),
        name="attention",
    )(q, k, v)


_HALO = 16


def _dn_prep_kernel(x_ref, prev_ref, next_ref, cw_ref, eq_ref, eqt_ref, evk_ref, evkt_ref,
                    gb_ref, tri_ref, shift_ref, qn_ref, vk_ref, kt_ref, aux_ref, *, nseq):
    i = pl.program_id(0)
    tp = x_ref.shape[0]
    first = (i % nseq) == 0
    last = (i % nseq) == nseq - 1
    half = CONV_K // 2
    xb = x_ref[...]

    y = xb.astype(F32) * cw_ref[half:half + 1, :]
    for j in range(CONV_K):
        if j != half:
            y = y + _dot(shift_ref[j], xb) * cw_ref[j:j + 1, :]

    def edge(slab, row0):
        n = slab.shape[0]
        acc = None
        for j in range(CONV_K):
            shift = (half - j) % n
            rolled = slab if shift == 0 else pltpu.roll(slab, shift, axis=0)
            term = rolled[row0:row0 + 8, :] * cw_ref[j:j + 1, :]
            acc = term if acc is None else acc + term
        return acc

    prev = jnp.where(first, 0.0, prev_ref[...].astype(F32))
    nxt = jnp.where(last, 0.0, next_ref[...].astype(F32))
    top = edge(jnp.concatenate([prev, xb[:_HALO].astype(F32)], axis=0), _HALO)
    bot = edge(jnp.concatenate([xb[tp - _HALO:].astype(F32), nxt], axis=0), _HALO - 8)
    y = jnp.concatenate([top, y[8:tp - 8], bot], axis=0)
    y = y * (1.0 / (1.0 + jnp.exp(-y)))

    def group_scale(v, e_ref, et_ref):
        ss = _dot((v * v).astype(BF16), e_ref[...])
        return _dot(lax.rsqrt(ss + 1e-6).astype(BF16), et_ref[...])

    nq = DN_HEADS * DN_DK
    yq = y[:, :nq]
    qn_ref[...] = (yq * group_scale(yq, eq_ref, eqt_ref) * (DN_DK ** -0.5)).astype(BF16)
    yvk = y[:, nq:]
    sc = group_scale(yvk, evk_ref, evkt_ref)
    lane = lax.broadcasted_iota(jnp.int32, (1, yvk.shape[1]), 1)
    is_k = (lane // DN_DV) % 2 == 1
    vk = yvk * jnp.where(is_k, sc, 1.0)
    vk_ref[...] = vk.astype(BF16)
    vkt = vk.T
    for h in range(DN_HEADS):
        r0 = h * (DN_DV + DN_DK) + DN_DV
        kt_ref[h * DN_DK:(h + 1) * DN_DK, :] = vkt[r0:r0 + DN_DK, :].astype(BF16)

    hcol = lax.broadcasted_iota(jnp.int32, (1, 2 * DN_HEADS), 1) < DN_HEADS
    for d in range(2):
        gb = gb_ref[d]
        for blk in range(tp // DN_BLOCK):
            rs = slice(blk * DN_BLOCK, (blk + 1) * DN_BLOCK)
            g = gb[rs]
            g_hi = g.astype(BF16)
            g_mid, g_lo = _split_bf16(g - g_hi.astype(F32))
            cs = _dot(tri_ref[d], g_hi) + _dot(tri_ref[d], g_mid) + _dot(tri_ref[d], g_lo)
            aux_ref[d, rs, :] = jnp.where(hcol, cs, g)


def _dn_prep(dn, gb, S, W):
    tp = DN_PREP_TILE
    T = dn.shape[0]
    nseq = S // tp
    hb = tp // _HALO
    nh = T // _HALO
    dn_w = dn.shape[1]
    nq = DN_HEADS * DN_DK
    return pl.pallas_call(
        functools.partial(_dn_prep_kernel, nseq=nseq),
        grid=(T // tp,),
        in_specs=[
            pl.BlockSpec((tp, dn_w), lambda i: (i, 0)),
            pl.BlockSpec((_HALO, dn_w), lambda i: (jnp.maximum(i * hb - 1, 0), 0)),
            pl.BlockSpec((_HALO, dn_w), lambda i: (jnp.minimum((i + 1) * hb, nh - 1), 0)),
            _const_spec((8, dn_w)),
            _const_spec((nq, LANES)), _const_spec((LANES, nq)),
            _const_spec((dn_w - nq, LANES)), _const_spec((LANES, dn_w - nq)),
            pl.BlockSpec((2, tp, 2 * DN_HEADS), lambda i: (0, i, 0)),
            _const_spec((2, DN_BLOCK, DN_BLOCK)),
            _const_spec((CONV_K, tp, tp)),
        ],
        out_specs=[
            pl.BlockSpec((tp, nq), lambda i: (i, 0)),
            pl.BlockSpec((tp, dn_w - nq), lambda i: (i, 0)),
            pl.BlockSpec((nq, tp), lambda i: (0, i)),
            pl.BlockSpec((2, tp, 2 * DN_HEADS), lambda i: (0, i, 0)),
        ],
        out_shape=[
            jax.ShapeDtypeStruct((T, nq), BF16),
            jax.ShapeDtypeStruct((T, dn_w - nq), BF16),
            jax.ShapeDtypeStruct((nq, T), BF16),
            jax.ShapeDtypeStruct((2, T, 2 * DN_HEADS), F32),
        ],
        compiler_params=_params(("parallel",)),
        name="dn_prep",
    )(dn, dn, dn, W["convw"], W["eq"], W["eqt"], W["evk"], W["evkt"], gb, W["tri"], W["conv_shift"])


_LEAF = 4
_FULL_MERGE_SIZES = (4, 8)
_MERGE_SIZES = (16, 32, 64, 128)


def _active_blocks(d, s):
    return range(1 - d, DN_BLOCK // s, 2)


def _dn_masks(d):
    r = np.arange(DN_BLOCK)[:, None]
    c = np.arange(DN_BLOCK)[None, :]
    rr, cc = (r, c) if d == 0 else (c, r)
    tri = np.stack([rr >= cc, rr > cc]).astype(np.float32)
    def off(s):
        return ((rr // s) % 2 == 1) & ((rr // s) == (cc // s) + 1)

    small = np.stack([(r // _LEAF) == (c // _LEAF)] + [off(s) for s in _FULL_MERGE_SIZES]).astype(np.float32)
    offc = []
    for s in _MERGE_SIZES:
        rows = np.concatenate([np.arange(b * s, (b + 1) * s) for b in _active_blocks(d, s)])
        assert not np.delete(off(s), rows, axis=0).any()
        offc.append(off(s)[rows].astype(np.float32))
    return tri, small, np.stack(offc)


def _deltanet_kernel(qn_ref, vk_ref, kt_ref, aux_ref, auxt_ref, tri_ref, small_ref, offc_ref, o_ref,
                     s_ref, *scratch, d):
    @pl.when(pl.program_id(1) == 0)
    def _():
        s_ref[...] = jnp.zeros_like(s_ref)

    C = DN_BLOCK
    order = range(DN_STEP_BLOCKS) if d == 0 else range(DN_STEP_BLOCKS - 1, -1, -1)
    for sub in order:
        rows = pl.ds(sub * C, C)
        _deltanet_block(qn_ref.at[rows], vk_ref.at[rows], kt_ref.at[:, rows], aux_ref.at[:, rows],
                        auxt_ref.at[:, :, rows], tri_ref, small_ref, offc_ref, o_ref.at[rows],
                        s_ref, *scratch, d=d)


def _deltanet_block(qn_ref, vk_ref, kt_ref, aux_ref, auxt_ref, tri_ref, small_ref, offc_ref, o_ref,
                    s_ref, nm_ref, p_ref, x_ref, aqk_ref, uw_ref, vn_ref, *, d):
    C = DN_BLOCK
    heads = range(DN_HEADS)

    rowi = lax.broadcasted_iota(jnp.int32, (C, C), 0)
    coli = lax.broadcasted_iota(jnp.int32, (C, C), 1)
    eye = (rowi == coli).astype(F32)
    lane = lax.broadcasted_iota(jnp.int32, (1, LANES), 1)
    r64 = lax.broadcasted_iota(jnp.int32, (DN_DK, LANES), 0)
    c64 = lax.broadcasted_iota(jnp.int32, (DN_DK, LANES), 1)
    zeros_kt = jnp.zeros((DN_DK, C), BF16)
    zeros_s = jnp.zeros((DN_DK, LANES), F32)

    def q_pair(h):
        return qn_ref[:, (h // 2) * LANES:(h // 2 + 1) * LANES]

    def vk_head(h):
        return vk_ref[:, h * LANES:(h + 1) * LANES]

    def kt_head(h):
        return kt_ref[h * DN_DK:(h + 1) * DN_DK, :]

    def gc_col(h):
        return aux_ref[0, :, h:h + 1]

    def beta_col(h):
        return aux_ref[0, :, DN_HEADS + h:DN_HEADS + h + 1]

    def gc_row(h):
        return auxt_ref[0, h:h + 1, :]

    for h in heads:
        kt = kt_head(h)
        kt_for_q = jnp.concatenate([kt, zeros_kt] if h % 2 == 0 else [zeros_kt, kt], axis=0)
        kt_for_k = jnp.concatenate([zeros_kt, kt], axis=0)
        qk = _dot(q_pair(h), kt_for_q)
        kk = _dot(vk_head(h), kt_for_k)
        e0 = jnp.exp(jnp.minimum(gc_col(h) - gc_row(h), 0.0))
        aqk_ref[h] = (qk * e0 * tri_ref[0]).astype(BF16)
        nm_ref[h] = (-(kk * e0 * tri_ref[1]) * beta_col(h)).astype(BF16)

    leaf = small_ref[0]
    for h in heads:
        nd = nm_ref[h] * leaf
        p_ref[h] = (eye + nd.astype(F32)).astype(BF16)
        x_ref[h] = _dot(nd, nd).astype(BF16)
    for h in heads:
        pb = p_ref[h]
        p_ref[h] = (pb.astype(F32) + _dot(pb, x_ref[h])).astype(BF16)
    for k in range(len(_FULL_MERGE_SIZES)):
        off_mask = small_ref[1 + k]
        for h in heads:
            x_ref[h] = _dot(nm_ref[h] * off_mask, p_ref[h]).astype(BF16)
        for h in heads:
            pb = p_ref[h]
            p_ref[h] = (pb.astype(F32) + _dot(pb, x_ref[h])).astype(BF16)
    for k, s in enumerate(_MERGE_SIZES):
        blocks = list(_active_blocks(d, s))
        zeros_blk = jnp.zeros((s, C), BF16)

        def active_rows(ref, h):
            return jnp.concatenate([ref[h, b * s:(b + 1) * s, :] for b in blocks], axis=0)

        for h in heads:
            xc = _dot(active_rows(nm_ref, h) * offc_ref[k], p_ref[h]).astype(BF16)
            pieces = []
            for j in range(len(blocks)):
                piece = xc[j * s:(j + 1) * s, :]
                pieces += [zeros_blk, piece] if d == 0 else [piece, zeros_blk]
            x_ref[h] = jnp.concatenate(pieces, axis=0)
        for h in heads:
            pr = active_rows(p_ref, h)
            prn = (pr.astype(F32) + _dot(pr, x_ref[h])).astype(BF16)
            for j, b in enumerate(blocks):
                p_ref[h, b * s:(b + 1) * s, :] = prn[j * s:(j + 1) * s, :]

    for h in heads:
        egc = jnp.exp(gc_col(h))
        rhs = (vk_head(h).astype(F32) * beta_col(h) * jnp.where(lane < DN_DV, 1.0, egc)).astype(BF16)
        uw_ref[h] = _dot(p_ref[h], rhs).astype(BF16)
    for h in heads:
        eye_pl = (c64 == r64 + (h % 2) * DN_DV).astype(F32)
        s_aug = jnp.concatenate([eye_pl, -s_ref[h]], axis=0).astype(BF16)
        vn_ref[h] = _dot(uw_ref[h], s_aug).astype(BF16)
    o_pair = None
    for h in heads:
        par = h % 2
        gc_r = gc_row(h)
        g_tot = gc_r[:, C - 1:C] if d == 0 else gc_r[:, 0:1]
        s_pl = s_ref[h]
        v_new = vn_ref[h]
        s_sel = jnp.concatenate([s_pl, zeros_s] if par == 0 else [zeros_s, s_pl], axis=0).astype(BF16)
        qd = (q_pair(h).astype(F32) * jnp.exp(gc_col(h))).astype(BF16)
        o_pl = _dot(qd, s_sel) + _dot(aqk_ref[h], v_new)
        kd = (kt_head(h).astype(F32) * jnp.exp(g_tot - gc_r)).astype(BF16)
        s_ref[h] = s_pl * jnp.exp(g_tot) + _dot(kd, v_new)
        if par == 0:
            o_pair = o_pl
        else:
            o_ref[:, (h // 2) * LANES:(h // 2 + 1) * LANES] = o_pair + o_pl


def _deltanet(qn, vk, kt, aux, auxt, B, S, W, d):
    T = qn.shape[0]
    C = DN_BLOCK
    R = DN_STEP_BLOCKS * C
    nb = S // R

    def blk(b, i):
        return b * nb + (i if d == 0 else nb - 1 - i)

    nq = DN_HEADS * DN_DK
    tri, small, offc = W["dn_masks"][d]
    return pl.pallas_call(
        functools.partial(_deltanet_kernel, d=d),
        grid=(B, nb),
        in_specs=[
            pl.BlockSpec((R, nq), lambda b, i: (blk(b, i), 0)),
            pl.BlockSpec((R, 2 * nq), lambda b, i: (blk(b, i), 0)),
            pl.BlockSpec((nq, R), lambda b, i: (0, blk(b, i))),
            pl.BlockSpec((1, R, 2 * DN_HEADS), lambda b, i: (d, blk(b, i), 0)),
            pl.BlockSpec((1, 2 * DN_HEADS, R), lambda b, i: (d, 0, blk(b, i))),
            _const_spec((2, C, C)), _const_spec((1 + len(_FULL_MERGE_SIZES), C, C)),
            _const_spec((len(_MERGE_SIZES), C // 2, C)),
        ],
        out_specs=pl.BlockSpec((R, DN_HEADS * DN_DV), lambda b, i: (blk(b, i), 0)),
        out_shape=jax.ShapeDtypeStruct((T, DN_HEADS * DN_DV), F32),
        scratch_shapes=[pltpu.VMEM((DN_HEADS, DN_DK, LANES), F32)]
        + [pltpu.VMEM((DN_HEADS, C, C), BF16)] * 4
        + [pltpu.VMEM((DN_HEADS, C, LANES), BF16)] * 2,
        compiler_params=_params(("parallel", "arbitrary")),
        name="deltanet_fwd" if d == 0 else "deltanet_bwd",
    )(qn, vk, kt, aux, auxt, tri, small, offc)


def _layer_norm(v, g, b):
    mu = jnp.mean(v, axis=-1, keepdims=True)
    c = v - mu
    var = jnp.mean(c * c, axis=-1, keepdims=True)
    return c * lax.rsqrt(var + 1e-5) * g + b


def _mix_kernel(x_ref, attn_ref, of_ref, ob_ref, z_ref, gates_ref, p_ref,
                woa_ref, wod_ref, wout_ref, e8_ref, e8t_ref, dnorm_ref, ln1g_ref, ln1b_ref,
                wpg_ref, wpp_ref, rwh_ref, rwl_ref, rb_ref, ustrict_ref,
                r_ref, hb_ref, ti_ref, tg_ref, rank_ref, cnt_ref, run_ref):
    @pl.when(pl.program_id(0) == 0)
    def _():
        run_ref[...] = jnp.zeros_like(run_ref)

    oa = _dot(attn_ref[...], woa_ref[...])
    o = of_ref[...] + ob_ref[...]
    hi, lo = _split_bf16(o * o)
    ms = (_dot(hi, e8_ref[...]) + _dot(lo, e8_ref[...])) * (1.0 / DN_DV)
    ih, il = _split_bf16(lax.rsqrt(ms + 1e-6))
    sc = _dot(ih, e8t_ref[...]) + _dot(il, e8t_ref[...])
    zf = z_ref[...].astype(F32)
    od_in = o * sc * dnorm_ref[...] * (zf * (1.0 / (1.0 + jnp.exp(-zf))))
    od = _dot(od_in.astype(BF16), wod_ref[...])
    mix = gates_ref[:, :D_MODEL].astype(F32) * oa + gates_ref[:, D_MODEL:].astype(F32) * od
    mo = _dot(mix.astype(BF16), wout_ref[...])
    h = _layer_norm(DEEPNORM_ALPHA * x_ref[...] + mo, ln1g_ref[...], ln1b_ref[...])
    hb = h.astype(BF16)
    hb_ref[...] = _pack_halves(h)
    pg = _dot(hb, wpg_ref[...])
    pp = _dot(p_ref[...].astype(BF16), wpp_ref[...])
    r_ref[...] = DEEPNORM_ALPHA * h + pp * (1.0 / (1.0 + jnp.exp(-pg)))

    hl = (h - hb.astype(F32)).astype(BF16)
    logits = (_dot_nt(rwh_ref[...], hb) + _dot_nt(rwh_ref[...], hl)
              + _dot_nt(rwl_ref[...], hb) + rb_ref[...])
    eid = lax.broadcasted_iota(jnp.int32, logits.shape, 0)
    vals = []
    run = run_ref[:, 0:1]
    for k in range(TOP_K):
        m = jnp.max(logits, axis=0, keepdims=True)
        idx = jnp.min(jnp.where(logits == m, eid, N_EXPERTS), axis=0, keepdims=True)
        ti_ref[k:k + 1, :] = idx
        vals.append(m)
        hit = eid == idx
        logits = jnp.where(hit, -jnp.inf, logits)
        onehot = hit.astype(F32)
        earlier = _dot(onehot.astype(BF16), ustrict_ref[...])
        rank = jnp.sum(onehot * (run + earlier), axis=0, keepdims=True)
        rank_ref[k:k + 1, :] = rank.astype(jnp.int32)
        run = run + jnp.sum(onehot, axis=1, keepdims=True)
    run_ref[...] = jnp.broadcast_to(run, run_ref.shape)
    cnt_ref[...] = jnp.broadcast_to(run, cnt_ref.shape)
    es = [jnp.exp(v - vals[0]) for v in vals]
    den = es[0] + es[1] + es[2] + es[3]
    for k in range(TOP_K):
        tg_ref[k:k + 1, :] = es[k] / den


def _mix(x2, attn, o_f, o_b, z, gates, p2, W):
    tm = MIX_TILE
    T = x2.shape[0]
    row = lambda i: (i, 0)
    nd = DN_HEADS * DN_DV
    return pl.pallas_call(
        _mix_kernel,
        grid=(T // tm,),
        in_specs=[
            pl.BlockSpec((tm, D_MODEL), row),
            pl.BlockSpec((tm, MLA_HEADS * V_DIM), row),
            pl.BlockSpec((tm, nd), row),
            pl.BlockSpec((tm, nd), row),
            pl.BlockSpec((tm, nd), row),
            pl.BlockSpec((tm, 2 * D_MODEL), row),
            pl.BlockSpec((tm, PLE_DIM), row),
            _const_spec((MLA_HEADS * V_DIM, D_MODEL)), _const_spec((nd, D_MODEL)),
            _const_spec((D_MODEL, D_MODEL)),
            _const_spec((nd, LANES)), _const_spec((LANES, nd)), _const_spec((1, nd)),
            _const_spec((1, D_MODEL)), _const_spec((1, D_MODEL)),
            _const_spec((D_MODEL, D_MODEL)), _const_spec((PLE_DIM, D_MODEL)),
            _const_spec((N_EXPERTS, D_MODEL)), _const_spec((N_EXPERTS, D_MODEL)),
            _const_spec((N_EXPERTS, 1)),
            _const_spec((tm, tm)),
        ],
        out_specs=[
            pl.BlockSpec((tm, D_MODEL), row),
            pl.BlockSpec((tm, D_MODEL // 2), row),
            pl.BlockSpec((TOP_K, tm), lambda i: (0, i)),
            pl.BlockSpec((TOP_K, tm), lambda i: (0, i)),
            pl.BlockSpec((TOP_K, tm), lambda i: (0, i)),
            _const_spec((N_EXPERTS, LANES)),
        ],
        out_shape=[
            jax.ShapeDtypeStruct((T, D_MODEL), F32),
            jax.ShapeDtypeStruct((T, D_MODEL // 2), jnp.uint32),
            jax.ShapeDtypeStruct((TOP_K, T), jnp.int32),
            jax.ShapeDtypeStruct((TOP_K, T), F32),
            jax.ShapeDtypeStruct((TOP_K, T), jnp.int32),
            jax.ShapeDtypeStruct((N_EXPERTS, LANES), F32),
        ],
        scratch_shapes=[pltpu.VMEM((N_EXPERTS, LANES), F32)],
        compiler_params=_params(("arbitrary",)),
        name="mix",
    )(x2, attn, o_f, o_b, z, gates, p2, W["woa"], W["wod"], W["wout"], W["e8"], W["e8t"], W["dnorm"],
      W["ln1g"], W["ln1b"], W["wpg"], W["wpp"], W["rwh"], W["rwl"], W["rb"], W["ustrict"])


_CAST_ROWS = 256


_S_EXPERT, _S_FIRST, _S_NEXT, _S_SLOT = range(4)


def _moe_kernel(sched_ref, nvalid_ref, xs_ref, wgu_hbm, bgu_ref, wd_hbm, bd_ref, y_ref,
                wgu_ref, wd_ref, gu32_ref, d32_ref, sem, *, fc):
    i = pl.program_id(0)
    valid = i < nvalid_ref[0]
    expert = sched_ref[_S_EXPERT, i]
    slot = sched_ref[_S_SLOT, i]

    def weight_copies(e, s):
        return (pltpu.make_async_copy(wgu_hbm.at[e], gu32_ref.at[s], sem.at[0, s]),
                pltpu.make_async_copy(wd_hbm.at[e], d32_ref.at[s], sem.at[1, s]))

    @pl.when(i == 0)
    def _():
        for c in weight_copies(expert, slot):
            c.start()

    @pl.when(valid & (sched_ref[_S_FIRST, i] == 1))
    def _():
        for c in weight_copies(expert, slot):
            c.wait()
        for r in range(0, D_MODEL, _CAST_ROWS):
            wgu_ref[r:r + _CAST_ROWS, :] = gu32_ref[slot, r:r + _CAST_ROWS, :].astype(BF16)
        for r in range(0, D_FF, _CAST_ROWS):
            wd_ref[r:r + _CAST_ROWS, :] = d32_ref[slot, r:r + _CAST_ROWS, :].astype(BF16)
        nxt = sched_ref[_S_NEXT, i]

        @pl.when(nxt >= 0)
        def _():
            for c in weight_copies(nxt, 1 - slot):
                c.start()

    @pl.when(valid)
    def _():
        xs = jnp.concatenate(_unpack_halves(xs_ref[...]), axis=1).astype(BF16)
        acc = None
        for c in range(D_FF // fc):
            lo, hi = c * fc, (c + 1) * fc
            gate = _dot(xs, wgu_ref[:, lo:hi]) + bgu_ref[0, :, lo:hi]
            up = _dot(xs, wgu_ref[:, D_FF + lo:D_FF + hi]) + bgu_ref[0, :, D_FF + lo:D_FF + hi]
            gate = jnp.minimum(gate, SWIGLU_LIMIT)
            up = jnp.clip(up, -SWIGLU_LIMIT, SWIGLU_LIMIT)
            act = gate * (1.0 / (1.0 + jnp.exp(-SWIGLU_ALPHA * gate))) * (up + 1.0)
            part = _dot(act.astype(BF16), wd_ref[lo:hi, :])
            acc = part if acc is None else acc + part
        y_ref[...] = _pack_halves(acc + bd_ref[0])

    @pl.when(jnp.logical_not(valid))
    def _():
        y_ref[...] = jnp.zeros_like(y_ref)


def _moe(xs, sched, nvalid, W, bm, fc=512):
    P = xs.shape[0]
    by_expert = lambda i, sc, nv: (sc[_S_EXPERT, i], 0, 0)
    grid_spec = pltpu.PrefetchScalarGridSpec(
        num_scalar_prefetch=2,
        grid=(P // bm,),
        in_specs=[
            pl.BlockSpec((bm, D_MODEL // 2), lambda i, sc, nv: (i, 0)),
            pl.BlockSpec(memory_space=pl.ANY),
            pl.BlockSpec((1, 1, 2 * D_FF), by_expert),
            pl.BlockSpec(memory_space=pl.ANY),
            pl.BlockSpec((1, 1, D_MODEL), by_expert),
        ],
        out_specs=pl.BlockSpec((bm, D_MODEL // 2), lambda i, sc, nv: (i, 0)),
        scratch_shapes=[pltpu.VMEM((D_MODEL, 2 * D_FF), BF16), pltpu.VMEM((D_FF, D_MODEL), BF16),
                        pltpu.VMEM((2, D_MODEL, 2 * D_FF), F32), pltpu.VMEM((2, D_FF, D_MODEL), F32),
                        pltpu.SemaphoreType.DMA((2, 2))],
    )
    return pl.pallas_call(
        functools.partial(_moe_kernel, fc=fc),
        grid_spec=grid_spec,
        out_shape=jax.ShapeDtypeStruct((P, D_MODEL // 2), jnp.uint32),
        compiler_params=_params(("arbitrary",)),
        name="moe",
    )(sched, nvalid, xs, W["wgu"], W["bgu"], W["wd"], W["bd"])


def _final_kernel(r_ref, yg_ref, tg_ref, g_ref, b_ref, y_ref):
    ffn_lo = ffn_hi = None
    for k in range(TOP_K):
        lo, hi = _unpack_halves(yg_ref[k])
        g = tg_ref[:, k:k + 1]
        ffn_lo = lo * g if ffn_lo is None else ffn_lo + lo * g
        ffn_hi = hi * g if ffn_hi is None else ffn_hi + hi * g
    acc = r_ref[...] + jnp.concatenate([ffn_lo, ffn_hi], axis=1)
    y_ref[...] = _layer_norm(acc, g_ref[...], b_ref[...])


def _final(r, yg, tg, W, tm=512):
    T = r.shape[0]
    tm = min(tm, T)
    row = lambda i: (i, 0)
    return pl.pallas_call(
        _final_kernel,
        grid=(T // tm,),
        in_specs=[pl.BlockSpec((tm, D_MODEL), row),
                  pl.BlockSpec((TOP_K, tm, D_MODEL // 2), lambda i: (0, i, 0)),
                  pl.BlockSpec((tm, TOP_K), row),
                  _const_spec((1, D_MODEL)), _const_spec((1, D_MODEL))],
        out_specs=pl.BlockSpec((tm, D_MODEL), row),
        out_shape=jax.ShapeDtypeStruct((T, D_MODEL), F32),
        compiler_params=_params(("parallel",)),
        name="final_ln",
    )(r, yg, tg, W["ln2g"], W["ln2b"])


def _pad_heads(w, n_heads, width, start, size, dst=0):
    K = w.shape[0]
    w3 = w.reshape(K, n_heads, width)[:, :, start:start + size]
    out = jnp.zeros((K, n_heads, HEAD_PAD), w.dtype)
    out = out.at[:, :, dst:dst + size].set(w3)
    return out.reshape(K, n_heads * HEAD_PAD)


def _prep_weights(w_in, q_a_norm, w_uq, kv_a_norm, w_ukv, w_o_attn, dn_conv, dn_a_log, dn_dt_bias,
                  dn_norm, w_o_dn, w_out, ln1_g, ln1_b, router_w, router_b, w_gate_up, b_gate_up,
                  w_down, b_down, ple_w_proj, ple_w_gate, ln2_g, ln2_b):
    W = {}
    half = ROPE_DIM // 2
    o = 0
    cq = w_in[:, o:o + Q_LORA]; o += Q_LORA
    ckv = w_in[:, o:o + KV_LORA]; o += KV_LORA
    kr = w_in[:, o:o + ROPE_DIM]; o += ROPE_DIM
    nqk = DN_HEADS * DN_DK
    dq = w_in[:, o:o + nqk]; o += nqk
    dk = w_in[:, o:o + nqk]; o += nqk
    dv = w_in[:, o:o + DN_HEADS * DN_DV]; o += DN_HEADS * DN_DV
    dz = w_in[:, o:o + DN_HEADS * DN_DV]; o += DN_HEADS * DN_DV
    da = w_in[:, o:o + 2 * DN_HEADS]; o += 2 * DN_HEADS
    db = w_in[:, o:o + 2 * DN_HEADS]; o += 2 * DN_HEADS
    gate = w_in[:, o:o + 2 * D_MODEL]

    def lane_block(parts):
        w = jnp.concatenate(parts, axis=1)
        return jnp.pad(w, ((0, 0), (0, LANES - w.shape[1])))

    def misc_block(d, rope_cols):
        ab = [da[:, d * DN_HEADS:(d + 1) * DN_HEADS], db[:, d * DN_HEADS:(d + 1) * DN_HEADS]]
        return lane_block(ab + [jnp.zeros((D_MODEL, NOPE_DIM - 2 * DN_HEADS), F32), rope_cols])

    misc0 = misc_block(0, kr)
    misc1 = misc_block(1, jnp.concatenate([kr[:, half:], kr[:, :half]], axis=1))

    def interleave_vk(v, k):
        lead = v.shape[:-1]
        v3 = v.reshape(lead + (DN_HEADS, DN_DV))
        k3 = k.reshape(lead + (DN_HEADS, DN_DK))
        return jnp.concatenate([v3, k3], axis=-1).reshape(lead + (DN_HEADS * (DN_DV + DN_DK),))

    W["w1"] = jnp.concatenate([cq, ckv, misc0, misc1, dq, interleave_vk(dv, dk), dz, gate],
                              axis=1).astype(BF16)
    W["qan"] = q_a_norm.reshape(1, Q_LORA)
    W["kvan"] = kv_a_norm.reshape(1, KV_LORA)

    qw = NOPE_DIM + ROPE_DIM
    wq_nope = _pad_heads(w_uq, MLA_HEADS, qw, 0, NOPE_DIM, 0)
    wq_r1 = _pad_heads(w_uq, MLA_HEADS, qw, NOPE_DIM, half, NOPE_DIM)
    wq_r2 = _pad_heads(w_uq, MLA_HEADS, qw, NOPE_DIM + half, half, NOPE_DIM + half)
    W["wq"] = (wq_nope + wq_r1 + wq_r2).astype(BF16)
    wq_s1 = _pad_heads(w_uq, MLA_HEADS, qw, NOPE_DIM + half, half, NOPE_DIM)
    wq_s2 = _pad_heads(w_uq, MLA_HEADS, qw, NOPE_DIM, half, NOPE_DIM + half)
    W["wqs"] = (wq_s1 + wq_s2).astype(BF16)
    kvw = NOPE_DIM + V_DIM
    W["wk"] = _pad_heads(w_ukv, MLA_HEADS, kvw, 0, NOPE_DIM, 0).astype(BF16)
    W["wv"] = _pad_heads(w_ukv, MLA_HEADS, kvw, NOPE_DIM, V_DIM, 0).astype(BF16)

    neg_a = -jnp.exp(dn_a_log.astype(F32))
    abp = jnp.zeros((8, LANES), F32)
    for d in range(2):
        abp = abp.at[2 * d, :DN_HEADS].set(neg_a[d])
        abp = abp.at[2 * d + 1, :DN_HEADS].set(dn_dt_bias[d].astype(F32))
    W["abp"] = abp

    cw = jnp.concatenate([dn_conv[:, :nqk], interleave_vk(dn_conv[:, 2 * nqk:], dn_conv[:, nqk:2 * nqk])], axis=1)
    W["convw"] = jnp.pad(cw.astype(F32), ((0, 8 - CONV_K), (0, 0)))

    def group_indicator(width, group):
        e = (np.arange(width)[:, None] // group == np.arange(LANES)[None, :]).astype(np.float32)
        return e

    eq = group_indicator(nqk, DN_DK)
    W["eq"] = jnp.asarray(eq, BF16)
    W["eqt"] = jnp.asarray(eq.T, BF16)
    evk = group_indicator(2 * nqk, DN_DK)
    W["evk"] = jnp.asarray(evk, BF16)
    W["evkt"] = jnp.asarray(evk.T, BF16)
    W["e8"] = W["eq"]
    W["e8t"] = W["eqt"]
    r = np.arange(DN_BLOCK)
    W["tri"] = jnp.asarray(np.stack([r[:, None] >= r[None, :], r[:, None] <= r[None, :]]).astype(np.float32), BF16)
    rp = np.arange(DN_PREP_TILE)
    W["conv_shift"] = jnp.asarray(
        np.stack([rp[None, :] == rp[:, None] + (j - CONV_K // 2) for j in range(CONV_K)]).astype(np.float32), BF16)
    W["dn_masks"] = [(jnp.asarray(t), jnp.asarray(g, BF16), jnp.asarray(o, BF16))
                     for t, g, o in (_dn_masks(0), _dn_masks(1))]
    rt = np.arange(MIX_TILE)
    W["ustrict"] = jnp.asarray((rt[:, None] < rt[None, :]).astype(np.float32), BF16)

    W["woa"] = w_o_attn.astype(BF16)
    W["wod"] = w_o_dn.astype(BF16)
    W["wout"] = w_out.astype(BF16)
    W["dnorm"] = jnp.tile(dn_norm.astype(F32), DN_HEADS).reshape(1, DN_HEADS * DN_DV)
    W["ln1g"] = ln1_g.reshape(1, D_MODEL)
    W["ln1b"] = ln1_b.reshape(1, D_MODEL)
    W["ln2g"] = ln2_g.reshape(1, D_MODEL)
    W["ln2b"] = ln2_b.reshape(1, D_MODEL)
    W["wpg"] = ple_w_gate.astype(BF16)
    W["wpp"] = ple_w_proj.astype(BF16)
    rwt = router_w.T.astype(F32)
    W["rwh"], W["rwl"] = _split_bf16(rwt)
    W["rb"] = router_b.reshape(N_EXPERTS, 1).astype(F32)
    W["wgu"] = w_gate_up.astype(F32)
    W["bgu"] = b_gate_up.reshape(N_EXPERTS, 1, 2 * D_FF).astype(F32)
    W["wd"] = w_down.astype(F32)
    W["bd"] = b_down.reshape(N_EXPERTS, 1, D_MODEL).astype(F32)
    return W


def _rope_tables(S):
    half = ROPE_DIM // 2
    inv = ROPE_THETA ** (-jnp.arange(0, ROPE_DIM, 2, dtype=F32) / ROPE_DIM)
    ang = jnp.arange(S, dtype=F32)[:, None] * inv[None, :]
    cos, sin = jnp.cos(ang), jnp.sin(ang)
    c = (NOPE_DIM + ROPE_DIM) ** -0.5 * math.log2(math.e)
    pad = jnp.zeros((S, HEAD_PAD - NOPE_DIM - ROPE_DIM), F32)
    cos_blk = jnp.concatenate([cos, cos, pad], axis=1)
    sin_blk = jnp.concatenate([-sin, sin, pad], axis=1)
    cosq = jnp.concatenate([jnp.ones((S, NOPE_DIM), F32), cos_blk], axis=1) * c
    sinq = jnp.concatenate([jnp.zeros((S, NOPE_DIM), F32), sin_blk], axis=1) * c
    cosk = jnp.concatenate([jnp.zeros((S, NOPE_DIM), F32), cos_blk], axis=1)
    sink = jnp.concatenate([jnp.zeros((S, NOPE_DIM), F32), sin_blk], axis=1)
    return cosq, sinq, cosk, sink


def _dest_kernel(pstart_ref, ti_ref, rank_ref, dest_ref):
    ti = ti_ref[...]
    dest = rank_ref[...]
    for e in range(N_EXPERTS):
        dest = dest + jnp.where(ti == e, pstart_ref[e], 0)
    dest_ref[...] = dest


def _dest(p_start, top_i, rank, tile=8192):
    T = top_i.shape[1]
    tile = min(tile, T)
    spec = pl.BlockSpec((TOP_K, tile), lambda i, ps: (0, i))
    return pl.pallas_call(
        _dest_kernel,
        grid_spec=pltpu.PrefetchScalarGridSpec(num_scalar_prefetch=1, grid=(T // tile,),
                                               in_specs=[spec, spec], out_specs=spec),
        out_shape=jax.ShapeDtypeStruct((TOP_K, T), jnp.int32),
        compiler_params=_params(("parallel",)),
        name="slot_index",
    )(p_start, top_i, rank)


SC_CHUNK = 128


def _sc_mesh():
    info = plsc.get_sparse_core_info()
    mesh = plsc.VectorSubcoreMesh(core_axis_name="c", subcore_axis_name="s")
    return mesh, info.num_cores, info.num_cores * info.num_subcores


def _sc_dispatch(rows, dest3, P):
    T, D = rows.shape
    K = dest3.shape[0]
    mesh, n_cores, n_workers = _sc_mesh()
    n_chunks = T // (n_workers * SC_CHUNK)

    @functools.partial(
        pl.kernel, mesh=mesh, out_type=jax.ShapeDtypeStruct((P, D), rows.dtype),
        scratch_types=[pltpu.VMEM((K, n_chunks, SC_CHUNK), jnp.int32), pltpu.VMEM((SC_CHUNK, D), rows.dtype),
                       pltpu.SemaphoreType.DMA])
    def dispatch(rows_hbm, dest_hbm, out_hbm, idx_v, rows_v, sem):
        worker = lax.axis_index("s") * n_cores + lax.axis_index("c")
        first = pl.multiple_of(worker * n_chunks, n_chunks)
        pltpu.sync_copy(dest_hbm.at[:, pl.ds(first, n_chunks)], idx_v)

        @pl.loop(0, n_chunks)
        def _(j):
            base = pl.multiple_of((first + j) * SC_CHUNK, SC_CHUNK)
            pltpu.sync_copy(rows_hbm.at[pl.ds(base, SC_CHUNK)], rows_v)
            for k in range(K):
                pltpu.async_copy(rows_v, out_hbm.at[idx_v.at[k, j]], sem).wait()

    return dispatch(rows, dest3)


def _sc_gather(table, idx2):
    M = idx2.shape[0] * SC_CHUNK
    D = table.shape[1]
    mesh, n_cores, n_workers = _sc_mesh()
    n_chunks = M // (n_workers * SC_CHUNK)

    @functools.partial(
        pl.kernel, mesh=mesh, out_type=jax.ShapeDtypeStruct((M, D), table.dtype),
        scratch_types=[pltpu.VMEM((n_chunks, SC_CHUNK), jnp.int32), pltpu.VMEM((SC_CHUNK, D), table.dtype),
                       pltpu.SemaphoreType.DMA])
    def gather(table_hbm, idx_hbm, out_hbm, idx_v, rows_v, sem):
        worker = lax.axis_index("s") * n_cores + lax.axis_index("c")
        first = pl.multiple_of(worker * n_chunks, n_chunks)
        pltpu.sync_copy(idx_hbm.at[pl.ds(first, n_chunks)], idx_v)

        @pl.loop(0, n_chunks)
        def _(j):
            base = pl.multiple_of((first + j) * SC_CHUNK, SC_CHUNK)
            pltpu.async_copy(table_hbm.at[idx_v.at[j]], rows_v, sem).wait()
            pltpu.sync_copy(rows_v, out_hbm.at[pl.ds(base, SC_CHUNK)])

    return gather(table, idx2)


def _route(top_i, rank, counts, T, bm):
    A = TOP_K * T
    counts = counts.astype(jnp.int32)
    padded = ((counts + bm - 1) // bm) * bm
    p_end = jnp.cumsum(padded)
    p_start = p_end - padded
    dest = _dest(p_start, top_i, rank)
    nblk = A // bm + N_EXPERTS
    blk_start = jnp.arange(nblk, dtype=jnp.int32) * bm
    blk_e = jnp.minimum(jnp.sum(p_end[None, :] <= blk_start[:, None], axis=1), N_EXPERTS - 1).astype(jnp.int32)
    nvalid = (p_end[-1] // bm).astype(jnp.int32).reshape(1)
    idx = jnp.arange(nblk, dtype=jnp.int32)
    first = ((idx == 0) | (blk_e != jnp.roll(blk_e, 1))).astype(jnp.int32)
    experts = jnp.arange(N_EXPERTS, dtype=jnp.int32)
    after = jnp.sum(jnp.where(blk_e[:, None] == experts[None, :], p_end[None, :], 0), axis=1)
    following = jnp.minimum(jnp.sum(p_end[None, :] <= after[:, None], axis=1), N_EXPERTS - 1)
    nxt = jnp.where(after < p_end[-1], following, -1).astype(jnp.int32)
    slot = (jnp.cumsum(first) - 1) % 2
    sched = jnp.stack([blk_e, first, nxt, slot.astype(jnp.int32)])
    return dest, nblk * bm, sched, nvalid


def _layer(x, p, W, bm):
    B, S, _ = x.shape
    T = B * S
    x2 = x.reshape(T, D_MODEL)
    p2 = p.reshape(T, PLE_DIM)
    Wl = dict(W)
    Wl["cosq"], Wl["sinq"], Wl["cosk"], Wl["sink"] = _rope_tables(S)

    q, k, v, dn, z, gates, gb = _in_proj(x2, S, Wl)
    attn = _attention(q, k, v, B, S)
    qn, vk, kt, aux = _dn_prep(dn, gb, S, Wl)
    auxt = jnp.swapaxes(aux, 1, 2)
    o_f = _deltanet(qn, vk, kt, aux, auxt, B, S, Wl, 0)
    o_b = _deltanet(qn, vk, kt, aux, auxt, B, S, Wl, 1)
    r, hb, top_i, top_g, rank, cnt = _mix(x2, attn, o_f, o_b, z, gates, p2, Wl)

    dest, P, blk_e, nvalid = _route(top_i, rank, cnt[:, 0], T, bm)
    xs = _sc_dispatch(hb, dest.reshape(TOP_K, T // SC_CHUNK, SC_CHUNK), P)
    yb = _moe(xs, blk_e, nvalid, Wl, bm)
    yg = _sc_gather(yb, dest.reshape(TOP_K * T // SC_CHUNK, SC_CHUNK)).reshape(TOP_K, T, D_MODEL // 2)
    y = _final(r, yg, top_g.T, Wl)
    return y.reshape(B, S, D_MODEL)


def kernel(x_prompt, x_sample, p_prompt, p_sample, w_in, q_a_norm, w_uq, kv_a_norm, w_ukv, w_o_attn, dn_conv, dn_a_log, dn_dt_bias, dn_norm, w_o_dn, w_out, ln1_g, ln1_b, router_w, router_b, w_gate_up, b_gate_up, w_down, b_down, ple_w_proj, ple_w_gate, ln2_g, ln2_b):
    y_prompt, y_sample = x_prompt, x_sample
    for l in range(DEPTH):
        W = _prep_weights(w_in[l], q_a_norm[l], w_uq[l], kv_a_norm[l], w_ukv[l], w_o_attn[l], dn_conv[l],
                          dn_a_log[l], dn_dt_bias[l], dn_norm[l], w_o_dn[l], w_out[l], ln1_g[l], ln1_b[l],
                          router_w[l], router_b[l], w_gate_up[l], b_gate_up[l], w_down[l], b_down[l],
                          ple_w_proj[l], ple_w_gate[l], ln2_g[l], ln2_b[l])
        y_prompt = _layer(y_prompt, p_prompt[l], W, bm=MOE_BLOCK)
        y_sample = _layer(y_sample, p_sample[l], W, bm=MOE_BLOCK)
    return (y_prompt, y_sample)
```

```python
import functools
import math

import numpy as np
import jax
import jax.numpy as jnp
from jax import lax
from jax.experimental import pallas as pl
from jax.experimental.pallas import tpu as pltpu
from jax.experimental.pallas import tpu_sc as plsc

D_MODEL = 1024
MLA_HEADS = 8
Q_LORA = 256
KV_LORA = 128
NOPE_DIM = 64
ROPE_DIM = 32
V_DIM = 64
ROPE_THETA = 10000.0
DN_HEADS = 8
DN_DK = 64
DN_DV = 64
CONV_K = 5
N_EXPERTS = 32
TOP_K = 4
D_FF = 1024
SWIGLU_LIMIT = 7.0
SWIGLU_ALPHA = 1.702
PLE_DIM = 256
DEPTH = 1
DEEPNORM_ALPHA = (2.0 * DEPTH) ** 0.25

LANES = 128
HEAD_PAD = 128
DN_BLOCK = 256
DN_STEP_BLOCKS = 2
DN_PREP_TILE = 256
MOE_BLOCK = 512
MIX_TILE = 512
VMEM_LIMIT = 56 * 1024 * 1024

_C_CQ = 0
_C_CKV = _C_CQ + Q_LORA
_C_MISC0 = _C_CKV + KV_LORA
_C_MISC1 = _C_MISC0 + LANES
_C_DNQ = _C_MISC1 + LANES
_C_DNVK = _C_DNQ + DN_HEADS * DN_DK
_C_Z = _C_DNVK + DN_HEADS * (DN_DK + DN_DV)
_C_GATE = _C_Z + DN_HEADS * DN_DV
_C_END = _C_GATE + 2 * D_MODEL

BF16 = jnp.bfloat16
F32 = jnp.float32


def _dot(a, b):
    return jnp.dot(a, b, preferred_element_type=F32)


def _dot_nt(a, b):
    return lax.dot_general(a, b, (((1,), (1,)), ((), ())), preferred_element_type=F32)


def _split_bf16(x):
    hi = x.astype(BF16)
    lo = (x - hi.astype(F32)).astype(BF16)
    return hi, lo


_HI_HALFWORD = 0xFFFF0000


def _pack_halves(x):
    w = x.shape[1] // 2
    bits = lax.bitcast_convert_type(x.astype(BF16).astype(F32), jnp.uint32)
    return (bits[:, :w] >> 16) | (bits[:, w:] & jnp.uint32(_HI_HALFWORD))


def _unpack_halves(words):
    lo = lax.bitcast_convert_type(words << 16, F32)
    hi = lax.bitcast_convert_type(words & jnp.uint32(_HI_HALFWORD), F32)
    return lo, hi


def _const_spec(shape):
    n = len(shape)
    return pl.BlockSpec(shape, lambda *_: (0,) * n, pipeline_mode=pl.Buffered(1))


def _params(sem):
    return pltpu.CompilerParams(dimension_semantics=sem, vmem_limit_bytes=VMEM_LIMIT)


def _in_proj_kernel(x_ref, w1_ref, qan_ref, kvan_ref, wq_ref, wqs_ref, wk_ref, wv_ref,
                    cosq_ref, sinq_ref, cosk_ref, sink_ref, abp_ref,
                    q_ref, k_ref, v_ref, dn_ref, z_ref, gates_ref, gb_ref):
    xb = x_ref[...].astype(BF16)

    def proj(lo, hi):
        return _dot(xb, w1_ref[:, lo:hi])

    def rms(c, g):
        return (c * lax.rsqrt(jnp.mean(c * c, axis=-1, keepdims=True) + 1e-6) * g).astype(BF16)

    cqn = rms(proj(_C_CQ, _C_CKV), qan_ref[...])
    qa = _dot(cqn, wq_ref[...])
    qb = _dot(cqn, wqs_ref[...])
    ckvn = rms(proj(_C_CKV, _C_MISC0), kvan_ref[...])
    kw = _dot(ckvn, wk_ref[...])
    vw = _dot(ckvn, wv_ref[...])
    misc = (proj(_C_MISC0, _C_MISC1), proj(_C_MISC1, _C_DNQ))
    kr = misc[0] * cosk_ref[...] + misc[1] * sink_ref[...]
    cosq = cosq_ref[...]
    sinq = sinq_ref[...]
    lane = lax.broadcasted_iota(jnp.int32, (1, HEAD_PAD), 1)
    ones_col = (lane == V_DIM).astype(F32)
    for h in range(MLA_HEADS):
        sl = slice(h * HEAD_PAD, (h + 1) * HEAD_PAD)
        q_ref[:, sl] = (qa[:, sl] * cosq + qb[:, sl] * sinq).astype(BF16)
        k_ref[:, sl] = (kw[:, sl] + kr).astype(BF16)
        v_ref[:, sl] = (vw[:, sl] + ones_col).astype(BF16)

    for d in range(2):
        ab = misc[d]
        neg_a = abp_ref[2 * d:2 * d + 1, :]
        dtb = abp_ref[2 * d + 1:2 * d + 2, :]
        t = ab + dtb
        sp = jnp.maximum(t, 0.0) + jnp.log(1.0 + jnp.exp(-jnp.abs(t)))
        g = neg_a * sp
        beta = 1.0 / (1.0 + jnp.exp(-ab))
        gb_ref[d] = jnp.where(lane < DN_HEADS, g, beta)[:, :2 * DN_HEADS]

    dn_ref[...] = proj(_C_DNQ, _C_Z).astype(BF16)
    z_ref[...] = proj(_C_Z, _C_GATE).astype(BF16)
    gl = proj(_C_GATE, _C_END)
    gates_ref[...] = (1.0 / (1.0 + jnp.exp(-gl))).astype(BF16)


def _in_proj(x2, S, W, tm=512):
    T = x2.shape[0]
    nseq = S // tm
    row = lambda i: (i, 0)
    pos = lambda i: (i % nseq, 0)
    dn_w = _C_Z - _C_DNQ
    return pl.pallas_call(
        _in_proj_kernel,
        grid=(T // tm,),
        in_specs=[
            pl.BlockSpec((tm, D_MODEL), row),
            _const_spec((D_MODEL, _C_END)),
            _const_spec((1, Q_LORA)), _const_spec((1, KV_LORA)),
            _const_spec((Q_LORA, MLA_HEADS * HEAD_PAD)), _const_spec((Q_LORA, MLA_HEADS * HEAD_PAD)),
            _const_spec((KV_LORA, MLA_HEADS * HEAD_PAD)), _const_spec((KV_LORA, MLA_HEADS * HEAD_PAD)),
            pl.BlockSpec((tm, HEAD_PAD), pos), pl.BlockSpec((tm, HEAD_PAD), pos),
            pl.BlockSpec((tm, HEAD_PAD), pos), pl.BlockSpec((tm, HEAD_PAD), pos),
            _const_spec((8, LANES)),
        ],
        out_specs=[
            pl.BlockSpec((tm, MLA_HEADS * HEAD_PAD), row),
            pl.BlockSpec((tm, MLA_HEADS * HEAD_PAD), row),
            pl.BlockSpec((tm, MLA_HEADS * HEAD_PAD), row),
            pl.BlockSpec((tm, dn_w), row),
            pl.BlockSpec((tm, DN_HEADS * DN_DV), row),
            pl.BlockSpec((tm, 2 * D_MODEL), row),
            pl.BlockSpec((2, tm, 2 * DN_HEADS), lambda i: (0, i, 0)),
        ],
        out_shape=[
            jax.ShapeDtypeStruct((T, MLA_HEADS * HEAD_PAD), BF16),
            jax.ShapeDtypeStruct((T, MLA_HEADS * HEAD_PAD), BF16),
            jax.ShapeDtypeStruct((T, MLA_HEADS * HEAD_PAD), BF16),
            jax.ShapeDtypeStruct((T, dn_w), BF16),
            jax.ShapeDtypeStruct((T, DN_HEADS * DN_DV), BF16),
            jax.ShapeDtypeStruct((T, 2 * D_MODEL), BF16),
            jax.ShapeDtypeStruct((2, T, 2 * DN_HEADS), F32),
        ],
        compiler_params=_params(("parallel",)),
        name="in_proj",
    )(x2, W["w1"], W["qan"], W["kvan"], W["wq"], W["wqs"], W["wk"], W["wv"],
      W["cosq"], W["sinq"], W["cosk"], W["sink"], W["abp"])


def _attn_kernel(q_ref, k_ref, v_ref, o_ref, *, tk, unroll):
    tq = q_ref.shape[0]
    S = k_ref.shape[0]
    outs = []
    for hh in range(2):
        sl = slice(hh * HEAD_PAD, (hh + 1) * HEAD_PAD)
        q = q_ref[:, sl]

        def body(j, carry, q=q, sl=sl):
            m, acc = carry
            off = pl.multiple_of(j * tk, tk)
            s = _dot_nt(q, k_ref[pl.ds(off, tk), sl])
            m_new = jnp.maximum(m, jnp.max(s, axis=-1, keepdims=True))
            p = jnp.exp2(s - m_new).astype(BF16)
            acc = acc * jnp.exp2(m - m_new) + _dot(p, v_ref[pl.ds(off, tk), sl])
            return m_new, acc

        m0 = jnp.full((tq, 1), -1e30, F32)
        acc0 = jnp.zeros((tq, HEAD_PAD), F32)
        _, acc = lax.fori_loop(0, S // tk, body, (m0, acc0), unroll=unroll)
        outs.append(acc / acc[:, V_DIM:V_DIM + 1])
    lane = lax.broadcasted_iota(jnp.int32, (1, HEAD_PAD), 1)
    o_ref[...] = jnp.where(lane < V_DIM, outs[0], pltpu.roll(outs[1], V_DIM, axis=1)).astype(BF16)


def _attention(q, k, v, B, S, tq=1024, tk=2048, unroll=4):
    T = q.shape[0]
    tq = min(tq, S)
    tk = min(tk, S)
    nq = S // tq
    return pl.pallas_call(
        functools.partial(_attn_kernel, tk=tk, unroll=unroll),
        grid=(B, MLA_HEADS // 2, nq),
        in_specs=[
            pl.BlockSpec((tq, 2 * HEAD_PAD), lambda b, h, i: (b * nq + i, h)),
            pl.BlockSpec((S, 2 * HEAD_PAD), lambda b, h, i: (b, h)),
            pl.BlockSpec((S, 2 * HEAD_PAD), lambda b, h, i: (b, h)),
        ],
        out_specs=pl.BlockSpec((tq, 2 * V_DIM), lambda b, h, i: (b * nq + i, h)),
        out_shape=jax.ShapeDtypeStruct((T, MLA_HEADS * V_DIM), BF16),
        compiler_params=_params(("parallel", "parallel", "arbitrary")),
        name="attention",
    )(q, k, v)


_HALO = 16


def _dn_prep_kernel(x_ref, prev_ref, next_ref, cw_ref, eq_ref, eqt_ref, evk_ref, evkt_ref,
                    gb_ref, tri_ref, shift_ref, qn_ref, vk_ref, kt_ref, aux_ref, *, nseq):
    i = pl.program_id(0)
    tp = x_ref.shape[0]
    first = (i % nseq) == 0
    last = (i % nseq) == nseq - 1
    half = CONV_K // 2
    xb = x_ref[...]

    y = xb.astype(F32) * cw_ref[half:half + 1, :]
    for j in range(CONV_K):
        if j != half:
            y = y + _dot(shift_ref[j], xb) * cw_ref[j:j + 1, :]

    def edge(slab, row0):
        n = slab.shape[0]
        acc = None
        for j in range(CONV_K):
            shift = (half - j) % n
            rolled = slab if shift == 0 else pltpu.roll(slab, shift, axis=0)
            term = rolled[row0:row0 + 8, :] * cw_ref[j:j + 1, :]
            acc = term if acc is None else acc + term
        return acc

    prev = jnp.where(first, 0.0, prev_ref[...].astype(F32))
    nxt = jnp.where(last, 0.0, next_ref[...].astype(F32))
    top = edge(jnp.concatenate([prev, xb[:_HALO].astype(F32)], axis=0), _HALO)
    bot = edge(jnp.concatenate([xb[tp - _HALO:].astype(F32), nxt], axis=0), _HALO - 8)
    y = jnp.concatenate([top, y[8:tp - 8], bot], axis=0)
    y = y * (1.0 / (1.0 + jnp.exp(-y)))

    def group_scale(v, e_ref, et_ref):
        ss = _dot((v * v).astype(BF16), e_ref[...])
        return _dot(lax.rsqrt(ss + 1e-6).astype(BF16), et_ref[...])

    nq = DN_HEADS * DN_DK
    yq = y[:, :nq]
    qn_ref[...] = (yq * group_scale(yq, eq_ref, eqt_ref) * (DN_DK ** -0.5)).astype(BF16)
    yvk = y[:, nq:]
    sc = group_scale(yvk, evk_ref, evkt_ref)
    lane = lax.broadcasted_iota(jnp.int32, (1, yvk.shape[1]), 1)
    is_k = (lane // DN_DV) % 2 == 1
    vk = yvk * jnp.where(is_k, sc, 1.0)
    vk_ref[...] = vk.astype(BF16)
    vkt = vk.T
    for h in range(DN_HEADS):
        r0 = h * (DN_DV + DN_DK) + DN_DV
        kt_ref[h * DN_DK:(h + 1) * DN_DK, :] = vkt[r0:r0 + DN_DK, :].astype(BF16)

    hcol = lax.broadcasted_iota(jnp.int32, (1, 2 * DN_HEADS), 1) < DN_HEADS
    for d in range(2):
        gb = gb_ref[d]
        for blk in range(tp // DN_BLOCK):
            rs = slice(blk * DN_BLOCK, (blk + 1) * DN_BLOCK)
            g = gb[rs]
            g_hi = g.astype(BF16)
            g_mid, g_lo = _split_bf16(g - g_hi.astype(F32))
            cs = _dot(tri_ref[d], g_hi) + _dot(tri_ref[d], g_mid) + _dot(tri_ref[d], g_lo)
            aux_ref[d, rs, :] = jnp.where(hcol, cs, g)


def _dn_prep(dn, gb, S, W):
    tp = DN_PREP_TILE
    T = dn.shape[0]
    nseq = S // tp
    hb = tp // _HALO
    nh = T // _HALO
    dn_w = dn.shape[1]
    nq = DN_HEADS * DN_DK
    return pl.pallas_call(
        functools.partial(_dn_prep_kernel, nseq=nseq),
        grid=(T // tp,),
        in_specs=[
            pl.BlockSpec((tp, dn_w), lambda i: (i, 0)),
            pl.BlockSpec((_HALO, dn_w), lambda i: (jnp.maximum(i * hb - 1, 0), 0)),
            pl.BlockSpec((_HALO, dn_w), lambda i: (jnp.minimum((i + 1) * hb, nh - 1), 0)),
            _const_spec((8, dn_w)),
            _const_spec((nq, LANES)), _const_spec((LANES, nq)),
            _const_spec((dn_w - nq, LANES)), _const_spec((LANES, dn_w - nq)),
            pl.BlockSpec((2, tp, 2 * DN_HEADS), lambda i: (0, i, 0)),
            _const_spec((2, DN_BLOCK, DN_BLOCK)),
            _const_spec((CONV_K, tp, tp)),
        ],
        out_specs=[
            pl.BlockSpec((tp, nq), lambda i: (i, 0)),
            pl.BlockSpec((tp, dn_w - nq), lambda i: (i, 0)),
            pl.BlockSpec((nq, tp), lambda i: (0, i)),
            pl.BlockSpec((2, tp, 2 * DN_HEADS), lambda i: (0, i, 0)),
        ],
        out_shape=[
            jax.ShapeDtypeStruct((T, nq), BF16),
            jax.ShapeDtypeStruct((T, dn_w - nq), BF16),
            jax.ShapeDtypeStruct((nq, T), BF16),
            jax.ShapeDtypeStruct((2, T, 2 * DN_HEADS), F32),
        ],
        compiler_params=_params(("parallel",)),
        name="dn_prep",
    )(dn, dn, dn, W["convw"], W["eq"], W["eqt"], W["evk"], W["evkt"], gb, W["tri"], W["conv_shift"])


_LEAF = 4
_FULL_MERGE_SIZES = (4, 8)
_MERGE_SIZES = (16, 32, 64, 128)


def _active_blocks(d, s):
    return range(1 - d, DN_BLOCK // s, 2)


def _dn_masks(d):
    r = np.arange(DN_BLOCK)[:, None]
    c = np.arange(DN_BLOCK)[None, :]
    rr, cc = (r, c) if d == 0 else (c, r)
    tri = np.stack([rr >= cc, rr > cc]).astype(np.float32)
    def off(s):
        return ((rr // s) % 2 == 1) & ((rr // s) == (cc // s) + 1)

    small = np.stack([(r // _LEAF) == (c // _LEAF)] + [off(s) for s in _FULL_MERGE_SIZES]).astype(np.float32)
    offc = []
    for s in _MERGE_SIZES:
        rows = np.concatenate([np.arange(b * s, (b + 1) * s) for b in _active_blocks(d, s)])
        assert not np.delete(off(s), rows, axis=0).any()
        offc.append(off(s)[rows].astype(np.float32))
    return tri, small, np.stack(offc)


def _deltanet_kernel(qn_ref, vk_ref, kt_ref, aux_ref, auxt_ref, tri_ref, small_ref, offc_ref, o_ref,
                     s_ref, *scratch, d):
    @pl.when(pl.program_id(1) == 0)
    def _():
        s_ref[...] = jnp.zeros_like(s_ref)

    C = DN_BLOCK
    order = range(DN_STEP_BLOCKS) if d == 0 else range(DN_STEP_BLOCKS - 1, -1, -1)
    for sub in order:
        rows = pl.ds(sub * C, C)
        _deltanet_block(qn_ref.at[rows], vk_ref.at[rows], kt_ref.at[:, rows], aux_ref.at[:, rows],
                        auxt_ref.at[:, :, rows], tri_ref, small_ref, offc_ref, o_ref.at[rows],
                        s_ref, *scratch, d=d)


def _deltanet_block(qn_ref, vk_ref, kt_ref, aux_ref, auxt_ref, tri_ref, small_ref, offc_ref, o_ref,
                    s_ref, nm_ref, p_ref, x_ref, aqk_ref, uw_ref, vn_ref, *, d):
    C = DN_BLOCK
    heads = range(DN_HEADS)

    rowi = lax.broadcasted_iota(jnp.int32, (C, C), 0)
    coli = lax.broadcasted_iota(jnp.int32, (C, C), 1)
    eye = (rowi == coli).astype(F32)
    lane = lax.broadcasted_iota(jnp.int32, (1, LANES), 1)
    r64 = lax.broadcasted_iota(jnp.int32, (DN_DK, LANES), 0)
    c64 = lax.broadcasted_iota(jnp.int32, (DN_DK, LANES), 1)
    zeros_kt = jnp.zeros((DN_DK, C), BF16)
    zeros_s = jnp.zeros((DN_DK, LANES), F32)

    def q_pair(h):
        return qn_ref[:, (h // 2) * LANES:(h // 2 + 1) * LANES]

    def vk_head(h):
        return vk_ref[:, h * LANES:(h + 1) * LANES]

    def kt_head(h):
        return kt_ref[h * DN_DK:(h + 1) * DN_DK, :]

    def gc_col(h):
        return aux_ref[0, :, h:h + 1]

    def beta_col(h):
        return aux_ref[0, :, DN_HEADS + h:DN_HEADS + h + 1]

    def gc_row(h):
        return auxt_ref[0, h:h + 1, :]

    for h in heads:
        kt = kt_head(h)
        kt_for_q = jnp.concatenate([kt, zeros_kt] if h % 2 == 0 else [zeros_kt, kt], axis=0)
        kt_for_k = jnp.concatenate([zeros_kt, kt], axis=0)
        qk = _dot(q_pair(h), kt_for_q)
        kk = _dot(vk_head(h), kt_for_k)
        e0 = jnp.exp(jnp.minimum(gc_col(h) - gc_row(h), 0.0))
        aqk_ref[h] = (qk * e0 * tri_ref[0]).astype(BF16)
        nm_ref[h] = (-(kk * e0 * tri_ref[1]) * beta_col(h)).astype(BF16)

    leaf = small_ref[0]
    for h in heads:
        nd = nm_ref[h] * leaf
        p_ref[h] = (eye + nd.astype(F32)).astype(BF16)
        x_ref[h] = _dot(nd, nd).astype(BF16)
    for h in heads:
        pb = p_ref[h]
        p_ref[h] = (pb.astype(F32) + _dot(pb, x_ref[h])).astype(BF16)
    for k in range(len(_FULL_MERGE_SIZES)):
        off_mask = small_ref[1 + k]
        for h in heads:
            x_ref[h] = _dot(nm_ref[h] * off_mask, p_ref[h]).astype(BF16)
        for h in heads:
            pb = p_ref[h]
            p_ref[h] = (pb.astype(F32) + _dot(pb, x_ref[h])).astype(BF16)
    for k, s in enumerate(_MERGE_SIZES):
        blocks = list(_active_blocks(d, s))
        zeros_blk = jnp.zeros((s, C), BF16)

        def active_rows(ref, h):
            return jnp.concatenate([ref[h, b * s:(b + 1) * s, :] for b in blocks], axis=0)

        for h in heads:
            xc = _dot(active_rows(nm_ref, h) * offc_ref[k], p_ref[h]).astype(BF16)
            pieces = []
            for j in range(len(blocks)):
                piece = xc[j * s:(j + 1) * s, :]
                pieces += [zeros_blk, piece] if d == 0 else [piece, zeros_blk]
            x_ref[h] = jnp.concatenate(pieces, axis=0)
        for h in heads:
            pr = active_rows(p_ref, h)
            prn = (pr.astype(F32) + _dot(pr, x_ref[h])).astype(BF16)
            for j, b in enumerate(blocks):
                p_ref[h, b * s:(b + 1) * s, :] = prn[j * s:(j + 1) * s, :]

    for h in heads:
        egc = jnp.exp(gc_col(h))
        rhs = (vk_head(h).astype(F32) * beta_col(h) * jnp.where(lane < DN_DV, 1.0, egc)).astype(BF16)
        uw_ref[h] = _dot(p_ref[h], rhs).astype(BF16)
    for h in heads:
        eye_pl = (c64 == r64 + (h % 2) * DN_DV).astype(F32)
        s_aug = jnp.concatenate([eye_pl, -s_ref[h]], axis=0).astype(BF16)
        vn_ref[h] = _dot(uw_ref[h], s_aug).astype(BF16)
    o_pair = None
    for h in heads:
        par = h % 2
        gc_r = gc_row(h)
        g_tot = gc_r[:, C - 1:C] if d == 0 else gc_r[:, 0:1]
        s_pl = s_ref[h]
        v_new = vn_ref[h]
        s_sel = jnp.concatenate([s_pl, zeros_s] if par == 0 else [zeros_s, s_pl], axis=0).astype(BF16)
        qd = (q_pair(h).astype(F32) * jnp.exp(gc_col(h))).astype(BF16)
        o_pl = _dot(qd, s_sel) + _dot(aqk_ref[h], v_new)
        kd = (kt_head(h).astype(F32) * jnp.exp(g_tot - gc_r)).astype(BF16)
        s_ref[h] = s_pl * jnp.exp(g_tot) + _dot(kd, v_new)
        if par == 0:
            o_pair = o_pl
        else:
            o_ref[:, (h // 2) * LANES:(h // 2 + 1) * LANES] = o_pair + o_pl


def _deltanet(qn, vk, kt, aux, auxt, B, S, W, d):
    T = qn.shape[0]
    C = DN_BLOCK
    R = DN_STEP_BLOCKS * C
    nb = S // R

    def blk(b, i):
        return b * nb + (i if d == 0 else nb - 1 - i)

    nq = DN_HEADS * DN_DK
    tri, small, offc = W["dn_masks"][d]
    return pl.pallas_call(
        functools.partial(_deltanet_kernel, d=d),
        grid=(B, nb),
        in_specs=[
            pl.BlockSpec((R, nq), lambda b, i: (blk(b, i), 0)),
            pl.BlockSpec((R, 2 * nq), lambda b, i: (blk(b, i), 0)),
            pl.BlockSpec((nq, R), lambda b, i: (0, blk(b, i))),
            pl.BlockSpec((1, R, 2 * DN_HEADS), lambda b, i: (d, blk(b, i), 0)),
            pl.BlockSpec((1, 2 * DN_HEADS, R), lambda b, i: (d, 0, blk(b, i))),
            _const_spec((2, C, C)), _const_spec((1 + len(_FULL_MERGE_SIZES), C, C)),
            _const_spec((len(_MERGE_SIZES), C // 2, C)),
        ],
        out_specs=pl.BlockSpec((R, DN_HEADS * DN_DV), lambda b, i: (blk(b, i), 0)),
        out_shape=jax.ShapeDtypeStruct((T, DN_HEADS * DN_DV), F32),
        scratch_shapes=[pltpu.VMEM((DN_HEADS, DN_DK, LANES), F32)]
        + [pltpu.VMEM((DN_HEADS, C, C), BF16)] * 4
        + [pltpu.VMEM((DN_HEADS, C, LANES), BF16)] * 2,
        compiler_params=_params(("parallel", "arbitrary")),
        name="deltanet_fwd" if d == 0 else "deltanet_bwd",
    )(qn, vk, kt, aux, auxt, tri, small, offc)


def _layer_norm(v, g, b):
    mu = jnp.mean(v, axis=-1, keepdims=True)
    c = v - mu
    var = jnp.mean(c * c, axis=-1, keepdims=True)
    return c * lax.rsqrt(var + 1e-5) * g + b


def _mix_kernel(x_ref, attn_ref, of_ref, ob_ref, z_ref, gates_ref, p_ref,
                woa_ref, wod_ref, wout_ref, e8_ref, e8t_ref, dnorm_ref, ln1g_ref, ln1b_ref,
                wpg_ref, wpp_ref, rwh_ref, rwl_ref, rb_ref, ustrict_ref,
                r_ref, hb_ref, ti_ref, tg_ref, rank_ref, cnt_ref, run_ref):
    @pl.when(pl.program_id(0) == 0)
    def _():
        run_ref[...] = jnp.zeros_like(run_ref)

    oa = _dot(attn_ref[...], woa_ref[...])
    o = of_ref[...] + ob_ref[...]
    hi, lo = _split_bf16(o * o)
    ms = (_dot(hi, e8_ref[...]) + _dot(lo, e8_ref[...])) * (1.0 / DN_DV)
    ih, il = _split_bf16(lax.rsqrt(ms + 1e-6))
    sc = _dot(ih, e8t_ref[...]) + _dot(il, e8t_ref[...])
    zf = z_ref[...].astype(F32)
    od_in = o * sc * dnorm_ref[...] * (zf * (1.0 / (1.0 + jnp.exp(-zf))))
    od = _dot(od_in.astype(BF16), wod_ref[...])
    mix = gates_ref[:, :D_MODEL].astype(F32) * oa + gates_ref[:, D_MODEL:].astype(F32) * od
    mo = _dot(mix.astype(BF16), wout_ref[...])
    h = _layer_norm(DEEPNORM_ALPHA * x_ref[...] + mo, ln1g_ref[...], ln1b_ref[...])
    hb = h.astype(BF16)
    hb_ref[...] = _pack_halves(h)
    pg = _dot(hb, wpg_ref[...])
    pp = _dot(p_ref[...].astype(BF16), wpp_ref[...])
    r_ref[...] = DEEPNORM_ALPHA * h + pp * (1.0 / (1.0 + jnp.exp(-pg)))

    hl = (h - hb.astype(F32)).astype(BF16)
    logits = (_dot_nt(rwh_ref[...], hb) + _dot_nt(rwh_ref[...], hl)
              + _dot_nt(rwl_ref[...], hb) + rb_ref[...])
    eid = lax.broadcasted_iota(jnp.int32, logits.shape, 0)
    vals = []
    run = run_ref[:, 0:1]
    for k in range(TOP_K):
        m = jnp.max(logits, axis=0, keepdims=True)
        idx = jnp.min(jnp.where(logits == m, eid, N_EXPERTS), axis=0, keepdims=True)
        ti_ref[k:k + 1, :] = idx
        vals.append(m)
        hit = eid == idx
        logits = jnp.where(hit, -jnp.inf, logits)
        onehot = hit.astype(F32)
        earlier = _dot(onehot.astype(BF16), ustrict_ref[...])
        rank = jnp.sum(onehot * (run + earlier), axis=0, keepdims=True)
        rank_ref[k:k + 1, :] = rank.astype(jnp.int32)
        run = run + jnp.sum(onehot, axis=1, keepdims=True)
    run_ref[...] = jnp.broadcast_to(run, run_ref.shape)
    cnt_ref[...] = jnp.broadcast_to(run, cnt_ref.shape)
    es = [jnp.exp(v - vals[0]) for v in vals]
    den = es[0] + es[1] + es[2] + es[3]
    for k in range(TOP_K):
        tg_ref[k:k + 1, :] = es[k] / den


def _mix(x2, attn, o_f, o_b, z, gates, p2, W):
    tm = MIX_TILE
    T = x2.shape[0]
    row = lambda i: (i, 0)
    nd = DN_HEADS * DN_DV
    return pl.pallas_call(
        _mix_kernel,
        grid=(T // tm,),
        in_specs=[
            pl.BlockSpec((tm, D_MODEL), row),
            pl.BlockSpec((tm, MLA_HEADS * V_DIM), row),
            pl.BlockSpec((tm, nd), row),
            pl.BlockSpec((tm, nd), row),
            pl.BlockSpec((tm, nd), row),
            pl.BlockSpec((tm, 2 * D_MODEL), row),
            pl.BlockSpec((tm, PLE_DIM), row),
            _const_spec((MLA_HEADS * V_DIM, D_MODEL)), _const_spec((nd, D_MODEL)),
            _const_spec((D_MODEL, D_MODEL)),
            _const_spec((nd, LANES)), _const_spec((LANES, nd)), _const_spec((1, nd)),
            _const_spec((1, D_MODEL)), _const_spec((1, D_MODEL)),
            _const_spec((D_MODEL, D_MODEL)), _const_spec((PLE_DIM, D_MODEL)),
            _const_spec((N_EXPERTS, D_MODEL)), _const_spec((N_EXPERTS, D_MODEL)),
            _const_spec((N_EXPERTS, 1)),
            _const_spec((tm, tm)),
        ],
        out_specs=[
            pl.BlockSpec((tm, D_MODEL), row),
            pl.BlockSpec((tm, D_MODEL // 2), row),
            pl.BlockSpec((TOP_K, tm), lambda i: (0, i)),
            pl.BlockSpec((TOP_K, tm), lambda i: (0, i)),
            pl.BlockSpec((TOP_K, tm), lambda i: (0, i)),
            _const_spec((N_EXPERTS, LANES)),
        ],
        out_shape=[
            jax.ShapeDtypeStruct((T, D_MODEL), F32),
            jax.ShapeDtypeStruct((T, D_MODEL // 2), jnp.uint32),
            jax.ShapeDtypeStruct((TOP_K, T), jnp.int32),
            jax.ShapeDtypeStruct((TOP_K, T), F32),
            jax.ShapeDtypeStruct((TOP_K, T), jnp.int32),
            jax.ShapeDtypeStruct((N_EXPERTS, LANES), F32),
        ],
        scratch_shapes=[pltpu.VMEM((N_EXPERTS, LANES), F32)],
        compiler_params=_params(("arbitrary",)),
        name="mix",
    )(x2, attn, o_f, o_b, z, gates, p2, W["woa"], W["wod"], W["wout"], W["e8"], W["e8t"], W["dnorm"],
      W["ln1g"], W["ln1b"], W["wpg"], W["wpp"], W["rwh"], W["rwl"], W["rb"], W["ustrict"])


_CAST_ROWS = 256


_S_EXPERT, _S_FIRST, _S_NEXT, _S_SLOT, _S_HALF = range(5)


def _moe_kernel(sched_ref, nvalid_ref, xs_ref, wgu_hbm, bgu_ref, wd_hbm, bd_ref, y_ref,
                wgu_ref, wd_ref, gu32_ref, d32_ref, sem, *, fc):
    i = pl.program_id(0)
    valid = i < nvalid_ref[0]
    expert = sched_ref[_S_EXPERT, i]
    slot = sched_ref[_S_SLOT, i]

    def weight_copies(e, s):
        return (pltpu.make_async_copy(wgu_hbm.at[e], gu32_ref.at[s], sem.at[0, s]),
                pltpu.make_async_copy(wd_hbm.at[e], d32_ref.at[s], sem.at[1, s]))

    @pl.when(i == 0)
    def _():
        for c in weight_copies(expert, slot):
            c.start()

    @pl.when(valid & (sched_ref[_S_FIRST, i] == 1))
    def _():
        for c in weight_copies(expert, slot):
            c.wait()
        for r in range(0, D_MODEL, _CAST_ROWS):
            wgu_ref[r:r + _CAST_ROWS, :] = gu32_ref[slot, r:r + _CAST_ROWS, :].astype(BF16)
        for r in range(0, D_FF, _CAST_ROWS):
            wd_ref[r:r + _CAST_ROWS, :] = d32_ref[slot, r:r + _CAST_ROWS, :].astype(BF16)
        nxt = sched_ref[_S_NEXT, i]

        @pl.when(nxt >= 0)
        def _():
            for c in weight_copies(nxt, 1 - slot):
                c.start()

    def expert_rows(n):
        xs = jnp.concatenate(_unpack_halves(xs_ref[0:n, :]), axis=1).astype(BF16)
        acc = None
        for c in range(D_FF // fc):
            lo, hi = c * fc, (c + 1) * fc
            gate = _dot(xs, wgu_ref[:, lo:hi]) + bgu_ref[0, :, lo:hi]
            up = _dot(xs, wgu_ref[:, D_FF + lo:D_FF + hi]) + bgu_ref[0, :, D_FF + lo:D_FF + hi]
            gate = jnp.minimum(gate, SWIGLU_LIMIT)
            up = jnp.clip(up, -SWIGLU_LIMIT, SWIGLU_LIMIT)
            act = gate * (1.0 / (1.0 + jnp.exp(-SWIGLU_ALPHA * gate))) * (up + 1.0)
            part = _dot(act.astype(BF16), wd_ref[lo:hi, :])
            acc = part if acc is None else acc + part
        y_ref[0:n, :] = _pack_halves(acc + bd_ref[0])

    bm = xs_ref.shape[0]
    half_only = sched_ref[_S_HALF, i] == 1

    @pl.when(valid & jnp.logical_not(half_only))
    def _():
        expert_rows(bm)

    @pl.when(valid & half_only)
    def _():
        expert_rows(bm // 2)
        y_ref[bm // 2:, :] = jnp.zeros((bm - bm // 2, y_ref.shape[1]), y_ref.dtype)

    @pl.when(jnp.logical_not(valid))
    def _():
        y_ref[...] = jnp.zeros_like(y_ref)


def _moe(xs, sched, nvalid, W, bm, fc=512):
    P = xs.shape[0]
    by_expert = lambda i, sc, nv: (sc[_S_EXPERT, i], 0, 0)
    grid_spec = pltpu.PrefetchScalarGridSpec(
        num_scalar_prefetch=2,
        grid=(P // bm,),
        in_specs=[
            pl.BlockSpec((bm, D_MODEL // 2), lambda i, sc, nv: (i, 0)),
            pl.BlockSpec(memory_space=pl.ANY),
            pl.BlockSpec((1, 1, 2 * D_FF), by_expert),
            pl.BlockSpec(memory_space=pl.ANY),
            pl.BlockSpec((1, 1, D_MODEL), by_expert),
        ],
        out_specs=pl.BlockSpec((bm, D_MODEL // 2), lambda i, sc, nv: (i, 0)),
        scratch_shapes=[pltpu.VMEM((D_MODEL, 2 * D_FF), BF16), pltpu.VMEM((D_FF, D_MODEL), BF16),
                        pltpu.VMEM((2, D_MODEL, 2 * D_FF), F32), pltpu.VMEM((2, D_FF, D_MODEL), F32),
                        pltpu.SemaphoreType.DMA((2, 2))],
    )
    return pl.pallas_call(
        functools.partial(_moe_kernel, fc=fc),
        grid_spec=grid_spec,
        out_shape=jax.ShapeDtypeStruct((P, D_MODEL // 2), jnp.uint32),
        compiler_params=_params(("arbitrary",)),
        name="moe",
    )(sched, nvalid, xs, W["wgu"], W["bgu"], W["wd"], W["bd"])


def _final_kernel(r_ref, yg_ref, tg_ref, g_ref, b_ref, y_ref):
    ffn_lo = ffn_hi = None
    for k in range(TOP_K):
        lo, hi = _unpack_halves(yg_ref[k])
        g = tg_ref[:, k:k + 1]
        ffn_lo = lo * g if ffn_lo is None else ffn_lo + lo * g
        ffn_hi = hi * g if ffn_hi is None else ffn_hi + hi * g
    acc = r_ref[...] + jnp.concatenate([ffn_lo, ffn_hi], axis=1)
    y_ref[...] = _layer_norm(acc, g_ref[...], b_ref[...])


def _final(r, yg, tg, W, tm=512):
    T = r.shape[0]
    tm = min(tm, T)
    row = lambda i: (i, 0)
    return pl.pallas_call(
        _final_kernel,
        grid=(T // tm,),
        in_specs=[pl.BlockSpec((tm, D_MODEL), row),
                  pl.BlockSpec((TOP_K, tm, D_MODEL // 2), lambda i: (0, i, 0)),
                  pl.BlockSpec((tm, TOP_K), row),
                  _const_spec((1, D_MODEL)), _const_spec((1, D_MODEL))],
        out_specs=pl.BlockSpec((tm, D_MODEL), row),
        out_shape=jax.ShapeDtypeStruct((T, D_MODEL), F32),
        compiler_params=_params(("parallel",)),
        name="final_ln",
    )(r, yg, tg, W["ln2g"], W["ln2b"])


def _pad_heads(w, n_heads, width, start, size, dst=0):
    K = w.shape[0]
    w3 = w.reshape(K, n_heads, width)[:, :, start:start + size]
    out = jnp.zeros((K, n_heads, HEAD_PAD), w.dtype)
    out = out.at[:, :, dst:dst + size].set(w3)
    return out.reshape(K, n_heads * HEAD_PAD)


def _prep_weights(w_in, q_a_norm, w_uq, kv_a_norm, w_ukv, w_o_attn, dn_conv, dn_a_log, dn_dt_bias,
                  dn_norm, w_o_dn, w_out, ln1_g, ln1_b, router_w, router_b, w_gate_up, b_gate_up,
                  w_down, b_down, ple_w_proj, ple_w_gate, ln2_g, ln2_b):
    W = {}
    half = ROPE_DIM // 2
    o = 0
    cq = w_in[:, o:o + Q_LORA]; o += Q_LORA
    ckv = w_in[:, o:o + KV_LORA]; o += KV_LORA
    kr = w_in[:, o:o + ROPE_DIM]; o += ROPE_DIM
    nqk = DN_HEADS * DN_DK
    dq = w_in[:, o:o + nqk]; o += nqk
    dk = w_in[:, o:o + nqk]; o += nqk
    dv = w_in[:, o:o + DN_HEADS * DN_DV]; o += DN_HEADS * DN_DV
    dz = w_in[:, o:o + DN_HEADS * DN_DV]; o += DN_HEADS * DN_DV
    da = w_in[:, o:o + 2 * DN_HEADS]; o += 2 * DN_HEADS
    db = w_in[:, o:o + 2 * DN_HEADS]; o += 2 * DN_HEADS
    gate = w_in[:, o:o + 2 * D_MODEL]

    def lane_block(parts):
        w = jnp.concatenate(parts, axis=1)
        return jnp.pad(w, ((0, 0), (0, LANES - w.shape[1])))

    def misc_block(d, rope_cols):
        ab = [da[:, d * DN_HEADS:(d + 1) * DN_HEADS], db[:, d * DN_HEADS:(d + 1) * DN_HEADS]]
        return lane_block(ab + [jnp.zeros((D_MODEL, NOPE_DIM - 2 * DN_HEADS), F32), rope_cols])

    misc0 = misc_block(0, kr)
    misc1 = misc_block(1, jnp.concatenate([kr[:, half:], kr[:, :half]], axis=1))

    def interleave_vk(v, k):
        lead = v.shape[:-1]
        v3 = v.reshape(lead + (DN_HEADS, DN_DV))
        k3 = k.reshape(lead + (DN_HEADS, DN_DK))
        return jnp.concatenate([v3, k3], axis=-1).reshape(lead + (DN_HEADS * (DN_DV + DN_DK),))

    W["w1"] = jnp.concatenate([cq, ckv, misc0, misc1, dq, interleave_vk(dv, dk), dz, gate],
                              axis=1).astype(BF16)
    W["qan"] = q_a_norm.reshape(1, Q_LORA)
    W["kvan"] = kv_a_norm.reshape(1, KV_LORA)

    qw = NOPE_DIM + ROPE_DIM
    wq_nope = _pad_heads(w_uq, MLA_HEADS, qw, 0, NOPE_DIM, 0)
    wq_r1 = _pad_heads(w_uq, MLA_HEADS, qw, NOPE_DIM, half, NOPE_DIM)
    wq_r2 = _pad_heads(w_uq, MLA_HEADS, qw, NOPE_DIM + half, half, NOPE_DIM + half)
    W["wq"] = (wq_nope + wq_r1 + wq_r2).astype(BF16)
    wq_s1 = _pad_heads(w_uq, MLA_HEADS, qw, NOPE_DIM + half, half, NOPE_DIM)
    wq_s2 = _pad_heads(w_uq, MLA_HEADS, qw, NOPE_DIM, half, NOPE_DIM + half)
    W["wqs"] = (wq_s1 + wq_s2).astype(BF16)
    kvw = NOPE_DIM + V_DIM
    W["wk"] = _pad_heads(w_ukv, MLA_HEADS, kvw, 0, NOPE_DIM, 0).astype(BF16)
    W["wv"] = _pad_heads(w_ukv, MLA_HEADS, kvw, NOPE_DIM, V_DIM, 0).astype(BF16)

    neg_a = -jnp.exp(dn_a_log.astype(F32))
    abp = jnp.zeros((8, LANES), F32)
    for d in range(2):
        abp = abp.at[2 * d, :DN_HEADS].set(neg_a[d])
        abp = abp.at[2 * d + 1, :DN_HEADS].set(dn_dt_bias[d].astype(F32))
    W["abp"] = abp

    cw = jnp.concatenate([dn_conv[:, :nqk], interleave_vk(dn_conv[:, 2 * nqk:], dn_conv[:, nqk:2 * nqk])], axis=1)
    W["convw"] = jnp.pad(cw.astype(F32), ((0, 8 - CONV_K), (0, 0)))

    def group_indicator(width, group):
        e = (np.arange(width)[:, None] // group == np.arange(LANES)[None, :]).astype(np.float32)
        return e

    eq = group_indicator(nqk, DN_DK)
    W["eq"] = jnp.asarray(eq, BF16)
    W["eqt"] = jnp.asarray(eq.T, BF16)
    evk = group_indicator(2 * nqk, DN_DK)
    W["evk"] = jnp.asarray(evk, BF16)
    W["evkt"] = jnp.asarray(evk.T, BF16)
    W["e8"] = W["eq"]
    W["e8t"] = W["eqt"]
    r = np.arange(DN_BLOCK)
    W["tri"] = jnp.asarray(np.stack([r[:, None] >= r[None, :], r[:, None] <= r[None, :]]).astype(np.float32), BF16)
    rp = np.arange(DN_PREP_TILE)
    W["conv_shift"] = jnp.asarray(
        np.stack([rp[None, :] == rp[:, None] + (j - CONV_K // 2) for j in range(CONV_K)]).astype(np.float32), BF16)
    W["dn_masks"] = [(jnp.asarray(t), jnp.asarray(g, BF16), jnp.asarray(o, BF16))
                     for t, g, o in (_dn_masks(0), _dn_masks(1))]
    rt = np.arange(MIX_TILE)
    W["ustrict"] = jnp.asarray((rt[:, None] < rt[None, :]).astype(np.float32), BF16)

    W["woa"] = w_o_attn.astype(BF16)
    W["wod"] = w_o_dn.astype(BF16)
    W["wout"] = w_out.astype(BF16)
    W["dnorm"] = jnp.tile(dn_norm.astype(F32), DN_HEADS).reshape(1, DN_HEADS * DN_DV)
    W["ln1g"] = ln1_g.reshape(1, D_MODEL)
    W["ln1b"] = ln1_b.reshape(1, D_MODEL)
    W["ln2g"] = ln2_g.reshape(1, D_MODEL)
    W["ln2b"] = ln2_b.reshape(1, D_MODEL)
    W["wpg"] = ple_w_gate.astype(BF16)
    W["wpp"] = ple_w_proj.astype(BF16)
    rwt = router_w.T.astype(F32)
    W["rwh"], W["rwl"] = _split_bf16(rwt)
    W["rb"] = router_b.reshape(N_EXPERTS, 1).astype(F32)
    W["wgu"] = w_gate_up.astype(F32)
    W["bgu"] = b_gate_up.reshape(N_EXPERTS, 1, 2 * D_FF).astype(F32)
    W["wd"] = w_down.astype(F32)
    W["bd"] = b_down.reshape(N_EXPERTS, 1, D_MODEL).astype(F32)
    return W


def _rope_tables(S):
    half = ROPE_DIM // 2
    inv = ROPE_THETA ** (-jnp.arange(0, ROPE_DIM, 2, dtype=F32) / ROPE_DIM)
    ang = jnp.arange(S, dtype=F32)[:, None] * inv[None, :]
    cos, sin = jnp.cos(ang), jnp.sin(ang)
    c = (NOPE_DIM + ROPE_DIM) ** -0.5 * math.log2(math.e)
    pad = jnp.zeros((S, HEAD_PAD - NOPE_DIM - ROPE_DIM), F32)
    cos_blk = jnp.concatenate([cos, cos, pad], axis=1)
    sin_blk = jnp.concatenate([-sin, sin, pad], axis=1)
    cosq = jnp.concatenate([jnp.ones((S, NOPE_DIM), F32), cos_blk], axis=1) * c
    sinq = jnp.concatenate([jnp.zeros((S, NOPE_DIM), F32), sin_blk], axis=1) * c
    cosk = jnp.concatenate([jnp.zeros((S, NOPE_DIM), F32), cos_blk], axis=1)
    sink = jnp.concatenate([jnp.zeros((S, NOPE_DIM), F32), sin_blk], axis=1)
    return cosq, sinq, cosk, sink


def _dest_kernel(pstart_ref, ti_ref, rank_ref, dest_ref):
    ti = ti_ref[...]
    dest = rank_ref[...]
    for e in range(N_EXPERTS):
        dest = dest + jnp.where(ti == e, pstart_ref[e], 0)
    dest_ref[...] = dest


def _dest(p_start, top_i, rank, tile=8192):
    T = top_i.shape[1]
    tile = min(tile, T)
    spec = pl.BlockSpec((TOP_K, tile), lambda i, ps: (0, i))
    return pl.pallas_call(
        _dest_kernel,
        grid_spec=pltpu.PrefetchScalarGridSpec(num_scalar_prefetch=1, grid=(T // tile,),
                                               in_specs=[spec, spec], out_specs=spec),
        out_shape=jax.ShapeDtypeStruct((TOP_K, T), jnp.int32),
        compiler_params=_params(("parallel",)),
        name="slot_index",
    )(p_start, top_i, rank)


SC_CHUNK = 128


def _sc_mesh():
    info = plsc.get_sparse_core_info()
    mesh = plsc.VectorSubcoreMesh(core_axis_name="c", subcore_axis_name="s")
    return mesh, info.num_cores, info.num_cores * info.num_subcores


def _sc_dispatch(rows, dest3, P):
    T, D = rows.shape
    K = dest3.shape[0]
    mesh, n_cores, n_workers = _sc_mesh()
    n_chunks = T // (n_workers * SC_CHUNK)

    @functools.partial(
        pl.kernel, mesh=mesh, out_type=jax.ShapeDtypeStruct((P, D), rows.dtype),
        scratch_types=[pltpu.VMEM((K, n_chunks, SC_CHUNK), jnp.int32), pltpu.VMEM((SC_CHUNK, D), rows.dtype),
                       pltpu.SemaphoreType.DMA])
    def dispatch(rows_hbm, dest_hbm, out_hbm, idx_v, rows_v, sem):
        worker = lax.axis_index("s") * n_cores + lax.axis_index("c")
        first = pl.multiple_of(worker * n_chunks, n_chunks)
        pltpu.sync_copy(dest_hbm.at[:, pl.ds(first, n_chunks)], idx_v)

        @pl.loop(0, n_chunks)
        def _(j):
            base = pl.multiple_of((first + j) * SC_CHUNK, SC_CHUNK)
            pltpu.sync_copy(rows_hbm.at[pl.ds(base, SC_CHUNK)], rows_v)
            for k in range(K):
                pltpu.async_copy(rows_v, out_hbm.at[idx_v.at[k, j]], sem).wait()

    return dispatch(rows, dest3)


def _sc_gather(table, idx2):
    M = idx2.shape[0] * SC_CHUNK
    D = table.shape[1]
    mesh, n_cores, n_workers = _sc_mesh()
    n_chunks = M // (n_workers * SC_CHUNK)

    @functools.partial(
        pl.kernel, mesh=mesh, out_type=jax.ShapeDtypeStruct((M, D), table.dtype),
        scratch_types=[pltpu.VMEM((n_chunks, SC_CHUNK), jnp.int32), pltpu.VMEM((SC_CHUNK, D), table.dtype),
                       pltpu.SemaphoreType.DMA])
    def gather(table_hbm, idx_hbm, out_hbm, idx_v, rows_v, sem):
        worker = lax.axis_index("s") * n_cores + lax.axis_index("c")
        first = pl.multiple_of(worker * n_chunks, n_chunks)
        pltpu.sync_copy(idx_hbm.at[pl.ds(first, n_chunks)], idx_v)

        @pl.loop(0, n_chunks)
        def _(j):
            base = pl.multiple_of((first + j) * SC_CHUNK, SC_CHUNK)
            pltpu.async_copy(table_hbm.at[idx_v.at[j]], rows_v, sem).wait()
            pltpu.sync_copy(rows_v, out_hbm.at[pl.ds(base, SC_CHUNK)])

    return gather(table, idx2)


def _route(top_i, rank, counts, T, bm):
    A = TOP_K * T
    counts = counts.astype(jnp.int32)
    padded = ((counts + bm - 1) // bm) * bm
    p_end = jnp.cumsum(padded)
    p_start = p_end - padded
    dest = _dest(p_start, top_i, rank)
    nblk = A // bm + N_EXPERTS
    blk_start = jnp.arange(nblk, dtype=jnp.int32) * bm
    blk_e = jnp.minimum(jnp.sum(p_end[None, :] <= blk_start[:, None], axis=1), N_EXPERTS - 1).astype(jnp.int32)
    nvalid = (p_end[-1] // bm).astype(jnp.int32).reshape(1)
    idx = jnp.arange(nblk, dtype=jnp.int32)
    first = ((idx == 0) | (blk_e != jnp.roll(blk_e, 1))).astype(jnp.int32)
    experts = jnp.arange(N_EXPERTS, dtype=jnp.int32)
    after = jnp.sum(jnp.where(blk_e[:, None] == experts[None, :], p_end[None, :], 0), axis=1)
    following = jnp.minimum(jnp.sum(p_end[None, :] <= after[:, None], axis=1), N_EXPERTS - 1)
    nxt = jnp.where(after < p_end[-1], following, -1).astype(jnp.int32)
    slot = (jnp.cumsum(first) - 1) % 2
    real_end = jnp.sum(jnp.where(blk_e[:, None] == experts[None, :], (p_start + counts)[None, :], 0), axis=1)
    half = (real_end - blk_start <= bm // 2).astype(jnp.int32)
    sched = jnp.stack([blk_e, first, nxt, slot.astype(jnp.int32), half])
    return dest, nblk * bm, sched, nvalid


def _layer(x, p, W, bm):
    B, S, _ = x.shape
    T = B * S
    x2 = x.reshape(T, D_MODEL)
    p2 = p.reshape(T, PLE_DIM)
    Wl = dict(W)
    Wl["cosq"], Wl["sinq"], Wl["cosk"], Wl["sink"] = _rope_tables(S)

    q, k, v, dn, z, gates, gb = _in_proj(x2, S, Wl)
    attn = _attention(q, k, v, B, S)
    qn, vk, kt, aux = _dn_prep(dn, gb, S, Wl)
    auxt = jnp.swapaxes(aux, 1, 2)
    o_f = _deltanet(qn, vk, kt, aux, auxt, B, S, Wl, 0)
    o_b = _deltanet(qn, vk, kt, aux, auxt, B, S, Wl, 1)
    r, hb, top_i, top_g, rank, cnt = _mix(x2, attn, o_f, o_b, z, gates, p2, Wl)

    dest, P, blk_e, nvalid = _route(top_i, rank, cnt[:, 0], T, bm)
    xs = _sc_dispatch(hb, dest.reshape(TOP_K, T // SC_CHUNK, SC_CHUNK), P)
    yb = _moe(xs, blk_e, nvalid, Wl, bm)
    yg = _sc_gather(yb, dest.reshape(TOP_K * T // SC_CHUNK, SC_CHUNK)).reshape(TOP_K, T, D_MODEL // 2)
    y = _final(r, yg, top_g.T, Wl)
    return y.reshape(B, S, D_MODEL)


def kernel(x_prompt, x_sample, p_prompt, p_sample, w_in, q_a_norm, w_uq, kv_a_norm, w_ukv, w_o_attn, dn_conv, dn_a_log, dn_dt_bias, dn_norm, w_o_dn, w_out, ln1_g, ln1_b, router_w, router_b, w_gate_up, b_gate_up, w_down, b_down, ple_w_proj, ple_w_gate, ln2_g, ln2_b):
    y_prompt, y_sample = x_prompt, x_sample
    for l in range(DEPTH):
        W = _prep_weights(w_in[l], q_a_norm[l], w_uq[l], kv_a_norm[l], w_ukv[l], w_o_attn[l], dn_conv[l],
                          dn_a_log[l], dn_dt_bias[l], dn_norm[l], w_o_dn[l], w_out[l], ln1_g[l], ln1_b[l],
                          router_w[l], router_b[l], w_gate_up[l], b_gate_up[l], w_down[l], b_down[l],
                          ple_w_proj[l], ple_w_gate[l], ln2_g[l], ln2_b[l])
        y_prompt = _layer(y_prompt, p_prompt[l], W, bm=MOE_BLOCK)
        y_sample = _layer(y_sample, p_sample[l], W, bm=MOE_BLOCK)
    return (y_prompt, y_sample)
```
